```python
import math
import jax, jax.numpy as jnp
from jax import lax
import numpy as np

D_MODEL = 1024
BATCH = 8
SEQ = 4096
DEPTH = 2

CHUNK = 64
D_MIX = D_MODEL
NORM_EPS = 1e-5
GLA_HEADS = 4
GLA_WIDTH = 3 * D_MIX // 8
GLA_DV = GLA_WIDTH // GLA_HEADS
GLA_DK = GLA_DV // 2
GLA_GATE_RANK = 16
GLA_GATE_TEMP = 16.0
RET_HEADS = 4
RET_WIDTH = 3 * D_MIX // 8
RET_DV = RET_WIDTH // RET_HEADS
RET_DK = RET_DV // 2
ROPE_BASE = 10000.0
S5_WIDTH = D_MIX - GLA_WIDTH - RET_WIDTH
S5_GROUP_DIM = 16
S5_GROUPS = S5_WIDTH // S5_GROUP_DIM
S5_STATE = 64
N_EXPERTS = 32
TOP_K = 4
D_FF = D_MODEL
SWIGLU_LIMIT = 7.0
SWIGLU_ALPHA = 1.702
EXPERT_BLOCK = 128
IN_SIZES = (GLA_HEADS * GLA_DK, GLA_HEADS * GLA_DK, GLA_WIDTH, GLA_WIDTH, GLA_GATE_RANK,
            RET_HEADS * RET_DK, RET_HEADS * RET_DK, RET_WIDTH, RET_WIDTH, S5_WIDTH)
IN_COLS = sum(IN_SIZES)

kernel_name = 'hybrid_gla_retnet_s5_moe_adaln'


def rms_norm(x, w):
    xf = x.astype(jnp.float32)
    y = xf * lax.rsqrt(jnp.mean(xf * xf, axis=-1, keepdims=True) + NORM_EPS)
    return (y * w.astype(jnp.float32)).astype(x.dtype)


def head_rms_norm(o, w):
    y = o * lax.rsqrt(jnp.mean(o * o, axis=-1, keepdims=True) + NORM_EPS)
    return y * w.astype(jnp.float32).reshape(o.shape[-2], o.shape[-1])


def head_group_norm(o, w):
    mu = jnp.mean(o, axis=-1, keepdims=True)
    oc = o - mu
    y = oc * lax.rsqrt(jnp.mean(oc * oc, axis=-1, keepdims=True) + NORM_EPS)
    return y * w.astype(jnp.float32).reshape(o.shape[-2], o.shape[-1])


def rotary(t, cos, sin):
    t1, t2 = jnp.split(t, 2, axis=-1)
    return jnp.concatenate([t1 * cos - t2 * sin, t1 * sin + t2 * cos], axis=-1)


def chunked_gated_linear_attention(q, k, v, log_a):
    b, l, h, dk = q.shape
    dv = v.shape[-1]
    n = l // CHUNK

    def chunks(t):
        return t.reshape(b, n, CHUNK, h, t.shape[-1]).transpose(0, 3, 1, 2, 4)

    q, k, v, log_a = chunks(q), chunks(k), chunks(v), chunks(log_a)
    cum = jnp.cumsum(log_a, axis=3)
    q_dec = q * jnp.exp(cum)
    k_inv = k * jnp.exp(-cum)
    causal = jnp.tril(jnp.ones((CHUNK, CHUNK), dtype=bool))
    scores = jnp.where(causal, jnp.einsum('bhnid,bhnjd->bhnij', q_dec, k_inv), 0.0)
    o_intra = jnp.einsum('bhnij,bhnjv->bhniv', scores, v)
    total = cum[:, :, :, -1]
    k_end = k * jnp.exp(total[:, :, :, None, :] - cum)
    d_state = jnp.einsum('bhncd,bhncv->nbhdv', k_end, v)
    decay = jnp.moveaxis(jnp.exp(total), 2, 0)

    def step(state, inp):
        ds, dec = inp
        return dec[..., None] * state + ds, state

    _, s_prev = lax.scan(step, jnp.zeros((b, h, dk, dv), q.dtype), (d_state, decay))
    o_inter = jnp.einsum('bhncd,nbhdv->bhncv', q_dec, s_prev)
    o = o_intra + o_inter
    return o.transpose(0, 2, 3, 1, 4).reshape(b, l, h, dv)


def s5_mixer(u, a_re, a_im, log_dt, b_re, b_im, c_re, c_im, d_skip, w_glu, b_glu):
    f32 = jnp.float32
    b, l, _ = u.shape
    uf = u.reshape(b, l, S5_GROUPS, S5_GROUP_DIM)
    lam = lax.complex(a_re.astype(f32), a_im.astype(f32))
    dt = jnp.exp(log_dt.astype(f32))[:, None]
    lam_bar = jnp.exp(lam * dt)
    b_mat = lax.complex(b_re.astype(f32), b_im.astype(f32))
    c_mat = lax.complex(c_re.astype(f32), c_im.astype(f32))
    b_bar = ((lam_bar - 1.0) / lam)[..., None] * b_mat
    bu = jnp.einsum('blgh,gph->blgp', uf.astype(jnp.complex64), b_bar)
    a_elems = jnp.broadcast_to(lam_bar, bu.shape)

    def combine(e1, e2):
        a1, x1 = e1
        a2, x2 = e2
        return a1 * a2, a2 * x1 + x2

    _, states = lax.associative_scan(combine, (a_elems, bu), axis=1)
    y = jnp.einsum('blgp,ghp->blgh', states, c_mat).real
    y = y + d_skip.astype(f32).reshape(S5_GROUPS, S5_GROUP_DIM) * uf
    y = jax.nn.gelu(y.reshape(b, l, S5_WIDTH))
    return y * jax.nn.sigmoid(y @ w_glu.astype(f32) + b_glu.astype(f32))


def token_mixer(h, w_in, gla_w_a2, gla_b_a, gla_norm_w, ret_norm_w, s5_a_re, s5_a_im,
                s5_log_dt, s5_b_re, s5_b_im, s5_c_re, s5_c_im, s5_d, s5_w_glu, s5_b_glu, w_out):
    f32 = jnp.float32
    b, l, _ = h.shape
    proj = (h @ w_in).astype(f32)
    points = np.cumsum(IN_SIZES)[:-1].tolist()
    gq, gk, gv, gg, ga, rq, rk, rv, rg, su = jnp.split(proj, points, axis=-1)

    log_a = jax.nn.log_sigmoid(ga @ gla_w_a2.astype(f32) + gla_b_a.astype(f32)) / GLA_GATE_TEMP
    o_gla = chunked_gated_linear_attention(
        gq.reshape(b, l, GLA_HEADS, GLA_DK) * GLA_DK ** -0.5,
        gk.reshape(b, l, GLA_HEADS, GLA_DK),
        gv.reshape(b, l, GLA_HEADS, GLA_DV),
        log_a.reshape(b, l, GLA_HEADS, GLA_DK))
    o_gla = head_rms_norm(o_gla, gla_norm_w).reshape(b, l, GLA_WIDTH) * jax.nn.silu(gg)

    pos = jnp.arange(l, dtype=f32)
    inv_freq = ROPE_BASE ** (-jnp.arange(0, RET_DK, 2, dtype=f32) / RET_DK)
    ang = pos[:, None] * inv_freq[None, :]
    cos, sin = jnp.cos(ang)[None, :, None, :], jnp.sin(ang)[None, :, None, :]
    q_r = rotary(rq.reshape(b, l, RET_HEADS, RET_DK), cos, sin)
    k_r = rotary(rk.reshape(b, l, RET_HEADS, RET_DK), cos, sin) * RET_DK ** -0.5
    log_gamma = jnp.log1p(-jnp.exp2(-5.0 - jnp.arange(RET_HEADS, dtype=f32)))
    log_decay = jnp.broadcast_to(log_gamma[None, None, :, None], (b, l, RET_HEADS, RET_DK))
    o_ret = chunked_gated_linear_attention(q_r, k_r, rv.reshape(b, l, RET_HEADS, RET_DV), log_decay)
    o_ret = head_group_norm(o_ret, ret_norm_w).reshape(b, l, RET_WIDTH) * jax.nn.silu(rg)

    o_s5 = s5_mixer(su, s5_a_re, s5_a_im, s5_log_dt, s5_b_re, s5_b_im, s5_c_re, s5_c_im,
                    s5_d, s5_w_glu, s5_b_glu)

    mixed = jnp.concatenate([o_gla, o_ret, o_s5], axis=-1).astype(h.dtype)
    return mixed @ w_out


def moe_ffn(h, router_w, router_b, w_up, b_up, w_down, b_down):
    b, l, d = h.shape
    t = b * l
    hf = h.reshape(t, d)
    logits = hf.astype(jnp.float32) @ router_w.astype(jnp.float32) + router_b.astype(jnp.float32)
    top_logits, top_idx = lax.top_k(logits, TOP_K)
    gates = jax.nn.softmax(top_logits, axis=-1)
    n_slots = t * TOP_K
    flat_e = top_idx.reshape(-1)
    order = jnp.argsort(flat_e)
    sorted_e = flat_e[order]
    counts = jnp.bincount(flat_e, length=N_EXPERTS)
    starts = jnp.cumsum(counts) - counts
    padded = (counts + EXPERT_BLOCK - 1) // EXPERT_BLOCK * EXPERT_BLOCK
    pad_ends = jnp.cumsum(padded)
    pad_starts = pad_ends - padded
    dest_sorted = pad_starts[sorted_e] + jnp.arange(n_slots) - starts[sorted_e]
    dest = jnp.zeros_like(dest_sorted).at[order].set(dest_sorted)
    n_blocks = n_slots // EXPERT_BLOCK + N_EXPERTS
    rows = jnp.zeros((n_blocks * EXPERT_BLOCK, d), h.dtype).at[dest].set(jnp.repeat(hf, TOP_K, axis=0))
    block_e = jnp.minimum(jnp.searchsorted(pad_ends, jnp.arange(n_blocks) * EXPERT_BLOCK, side='right'),
                          N_EXPERTS - 1)

    def expert_block(args):
        xb, e = args
        up = xb @ w_up[e] + b_up[e]
        x_glu, x_lin = jnp.split(up, 2, axis=-1)
        x_glu = jnp.minimum(x_glu, SWIGLU_LIMIT)
        x_lin = jnp.clip(x_lin, -SWIGLU_LIMIT, SWIGLU_LIMIT)
        act = x_glu * jax.nn.sigmoid(SWIGLU_ALPHA * x_glu) * (x_lin + 1.0)
        return act @ w_down[e] + b_down[e]

    out_rows = lax.map(expert_block, (rows.reshape(n_blocks, EXPERT_BLOCK, d), block_e)).reshape(-1, d)
    y = jnp.einsum('tkd,tk->td', out_rows[dest].reshape(t, TOP_K, d), gates.astype(h.dtype))
    return y.reshape(b, l, d)


def setup_inputs(seed: int = 0) -> dict:
    key = jax.random.key(seed)
    ks = iter(jax.random.split(key, 40))
    nrm = lambda shape, s: jax.random.normal(next(ks), shape, jnp.float32) * s
    gain = lambda shape: 1.0 + nrm(shape, 0.02)
    L = DEPTH
    d = D_MODEL
    a_im0 = jnp.pi * jnp.arange(S5_STATE, dtype=jnp.float32)
    return {
        'x': nrm((BATCH, SEQ, d), 1.0),
        'c': nrm((BATCH, d), 1.0),
        'norm1_w': gain((L, d)),
        'norm2_w': gain((L, d)),
        'w_mod': nrm((L, d, 6 * d), 0.5 * d ** -0.5),
        'b_mod': nrm((L, 6 * d), 0.02),
        'w_in': nrm((L, d, IN_COLS), d ** -0.5),
        'gla_w_a2': nrm((L, GLA_GATE_RANK, GLA_HEADS * GLA_DK), GLA_GATE_RANK ** -0.5),
        'gla_b_a': nrm((L, GLA_HEADS * GLA_DK), 0.1),
        'gla_norm_w': gain((L, GLA_WIDTH)),
        'ret_norm_w': gain((L, RET_WIDTH)),
        's5_a_re': -0.5 + nrm((L, S5_GROUPS, S5_STATE), 0.01),
        's5_a_im': a_im0 + nrm((L, S5_GROUPS, S5_STATE), 0.01),
        's5_log_dt': jax.random.uniform(next(ks), (L, S5_GROUPS), jnp.float32,
                                        math.log(1e-3), math.log(1e-1)),
        's5_b_re': nrm((L, S5_GROUPS, S5_STATE, S5_GROUP_DIM), (2.0 * S5_GROUP_DIM) ** -0.5),
        's5_b_im': nrm((L, S5_GROUPS, S5_STATE, S5_GROUP_DIM), (2.0 * S5_GROUP_DIM) ** -0.5),
        's5_c_re': nrm((L, S5_GROUPS, S5_GROUP_DIM, S5_STATE), S5_STATE ** -0.5),
        's5_c_im': nrm((L, S5_GROUPS, S5_GROUP_DIM, S5_STATE), S5_STATE ** -0.5),
        's5_d': nrm((L, S5_WIDTH), 1.0),
        's5_w_glu': nrm((L, S5_WIDTH, S5_WIDTH), S5_WIDTH ** -0.5),
        's5_b_glu': nrm((L, S5_WIDTH), 0.02),
        'w_out': nrm((L, D_MIX, d), D_MIX ** -0.5),
        'router_w': nrm((L, d, N_EXPERTS), d ** -0.5),
        'router_b': nrm((L, N_EXPERTS), 0.01),
        'w_up': nrm((L, N_EXPERTS, d, 2 * D_FF), d ** -0.5),
        'b_up': nrm((L, N_EXPERTS, 2 * D_FF), 0.02),
        'w_down': nrm((L, N_EXPERTS, D_FF, d), D_FF ** -0.5),
        'b_down': nrm((L, N_EXPERTS, d), 0.02),
        'final_norm_w': gain((d,)),
    }


def reference(x, c, norm1_w, norm2_w, w_mod, b_mod, w_in, gla_w_a2, gla_b_a, gla_norm_w,
              ret_norm_w, s5_a_re, s5_a_im, s5_log_dt, s5_b_re, s5_b_im, s5_c_re, s5_c_im,
              s5_d, s5_w_glu, s5_b_glu, w_out, router_w, router_b, w_up, b_up, w_down,
              b_down, final_norm_w):
    cond = jax.nn.silu(c)
    for i in range(DEPTH):
        mod = cond @ w_mod[i] + b_mod[i]
        sh1, sc1, g1, sh2, sc2, g2 = jnp.split(mod[:, None, :], 6, axis=-1)
        hdn = rms_norm(x, norm1_w[i]) * (1.0 + sc1) + sh1
        x = x + g1 * token_mixer(hdn, w_in[i], gla_w_a2[i], gla_b_a[i], gla_norm_w[i],
                                 ret_norm_w[i], s5_a_re[i], s5_a_im[i], s5_log_dt[i],
                                 s5_b_re[i], s5_b_im[i], s5_c_re[i], s5_c_im[i], s5_d[i],
                                 s5_w_glu[i], s5_b_glu[i], w_out[i])
        hdn = rms_norm(x, norm2_w[i]) * (1.0 + sc2) + sh2
        x = x + g2 * moe_ffn(hdn, router_w[i], router_b[i], w_up[i], b_up[i], w_down[i], b_down[i])
    return rms_norm(x, final_norm_w)
```

```python
import functools

import numpy as np
import jax
import jax.numpy as jnp
from jax import lax
from jax.experimental import pallas as pl
from jax.experimental.pallas import tpu as pltpu

D_MODEL = 1024
CHUNK = 64
NORM_EPS = 1e-5
N_HEADS = 4
HEAD_DK = 48
HEAD_DV = 96
GATE_RANK = 16
GATE_TEMP = 16.0
ROPE_BASE = 10000.0
S5_WIDTH = 256
S5_GROUP_DIM = 16
S5_GROUPS = 16
S5_STATE = 64
N_EXPERTS = 32
TOP_K = 4
D_FF = 1024
SWIGLU_LIMIT = 7.0
SWIGLU_ALPHA = 1.702

LANES = 128
HEAD_PAD = LANES
HP = N_HEADS * HEAD_PAD
VMEM_LIMIT = 56 * 1024 * 1024

OFF_GQ, OFF_GK, OFF_GV, OFF_GG = 0, HP, 2 * HP, 3 * HP
OFF_RQ, OFF_RK, OFF_RV, OFF_RG = 4 * HP, 5 * HP, 6 * HP, 7 * HP
OFF_SU = 8 * HP
OFF_GA = OFF_SU + S5_WIDTH
NP_COLS = OFF_GA + LANES
PROJ_CH = 896

ROW_BLK = 256

F32 = jnp.float32
BF16 = jnp.bfloat16


def _in_col_map():
    src = -np.ones((NP_COLS,), np.int64)
    kq = N_HEADS * HEAD_DK
    kv = N_HEADS * HEAD_DV
    base = dict(gq=0, gk=kq, gv=2 * kq, gg=2 * kq + kv, ga=2 * kq + 2 * kv)
    r0 = base['ga'] + GATE_RANK
    base.update(rq=r0, rk=r0 + kq, rv=r0 + 2 * kq, rg=r0 + 2 * kq + kv, su=r0 + 2 * kq + 2 * kv)
    half = HEAD_DK // 2
    for h in range(N_HEADS):
        for d in range(HEAD_DK):
            src[OFF_GQ + h * HEAD_PAD + d] = base['gq'] + h * HEAD_DK + d
            src[OFF_GK + h * HEAD_PAD + d] = base['gk'] + h * HEAD_DK + d
            lane = d if d < half else 64 + (d - half)
            src[OFF_RQ + h * HEAD_PAD + lane] = base['rq'] + h * HEAD_DK + d
            src[OFF_RK + h * HEAD_PAD + lane] = base['rk'] + h * HEAD_DK + d
        for d in range(HEAD_DV):
            src[OFF_GV + h * HEAD_PAD + d] = base['gv'] + h * HEAD_DV + d
            src[OFF_GG + h * HEAD_PAD + d] = base['gg'] + h * HEAD_DV + d
            src[OFF_RV + h * HEAD_PAD + d] = base['rv'] + h * HEAD_DV + d
            src[OFF_RG + h * HEAD_PAD + d] = base['rg'] + h * HEAD_DV + d
    src[OFF_SU:OFF_SU + S5_WIDTH] = base['su'] + np.arange(S5_WIDTH)
    src[OFF_GA:OFF_GA + GATE_RANK] = base['ga'] + np.arange(GATE_RANK)
    return src


_IN_SRC = _in_col_map()


def _head_pad_map(width):
    src = -np.ones((HP,), np.int64)
    for h in range(N_HEADS):
        src[h * HEAD_PAD:h * HEAD_PAD + width] = h * width + np.arange(width)
    return src


_DV_SRC = _head_pad_map(HEAD_DV)
_DK_SRC = _head_pad_map(HEAD_DK)


def _take_cols(w, src):
    out = jnp.take(w, jnp.asarray(np.maximum(src, 0)), axis=-1)
    return jnp.where(jnp.asarray(src >= 0), out, 0)


def _take_rows(w, src):
    out = jnp.take(w, jnp.asarray(np.maximum(src, 0)), axis=0)
    return jnp.where(jnp.asarray(src >= 0)[:, None], out, 0)


def _cparams(sem):
    return pltpu.CompilerParams(dimension_semantics=sem, vmem_limit_bytes=VMEM_LIMIT)


def _mod_kernel(c_ref, w_ref, b_ref, o_ref):
    c = c_ref[...]
    cond = c * jax.nn.sigmoid(c)
    o_ref[0] = jnp.dot(cond, w_ref[0], preferred_element_type=F32,
                       precision=lax.Precision.HIGHEST) + b_ref[0]


def _modulation(c, w_mod, b_mod):
    depth, d, n = w_mod.shape
    b = c.shape[0]
    nb = 1536
    return pl.pallas_call(
        _mod_kernel,
        grid=(depth, n // nb),
        in_specs=[pl.BlockSpec((b, d), lambda l, j: (0, 0)),
                  pl.BlockSpec((1, d, nb), lambda l, j: (l, 0, j)),
                  pl.BlockSpec((1, 1, nb), lambda l, j: (l, 0, j))],
        out_specs=pl.BlockSpec((1, b, nb), lambda l, j: (l, 0, j)),
        out_shape=jax.ShapeDtypeStruct((depth, b, n), F32),
        compiler_params=_cparams(("arbitrary", "arbitrary")),
        name="adaln_mod",
    )(c, w_mod, b_mod.reshape(depth, 1, n))


def _rms_mod(x, nw, sc, sh):
    y = x * lax.rsqrt(jnp.mean(x * x, axis=-1, keepdims=True) + NORM_EPS)
    return (y * nw) * (1.0 + sc) + sh


def _inproj_kernel(x_ref, sc_ref, sh_ref, nw_ref, w_ref, o_ref):
    h = _rms_mod(x_ref[...], nw_ref[...], sc_ref[0], sh_ref[0]).astype(BF16)
    for j in range(NP_COLS // PROJ_CH):
        cs = slice(j * PROJ_CH, (j + 1) * PROJ_CH)
        o_ref[:, cs] = jnp.dot(h, w_ref[:, cs], preferred_element_type=F32).astype(BF16)


def _in_projection(x2d, sc, sh, nw, w_p, seq, tm=512):
    t, d = x2d.shape
    per_b = seq // tm
    return pl.pallas_call(
        _inproj_kernel,
        grid=(t // tm,),
        in_specs=[pl.BlockSpec((tm, d), lambda i: (i, 0)),
                  pl.BlockSpec((1, 1, d), lambda i: (i // per_b, 0, 0)),
                  pl.BlockSpec((1, 1, d), lambda i: (i // per_b, 0, 0)),
                  pl.BlockSpec((1, d), lambda i: (0, 0)),
                  pl.BlockSpec((d, NP_COLS), lambda i: (0, 0))],
        out_specs=pl.BlockSpec((tm, NP_COLS), lambda i: (i, 0)),
        out_shape=jax.ShapeDtypeStruct((t, NP_COLS), BF16),
        compiler_params=_cparams(("arbitrary",)),
        name="in_proj",
    )(x2d, sc, sh, nw, w_p)


_NT = (((1,), (1,)), ((), ()))
_TN = (((0,), (0,)), ((), ()))


def _tri_mask():
    r = lax.broadcasted_iota(jnp.int32, (CHUNK, CHUNK), 0)
    c = lax.broadcasted_iota(jnp.int32, (CHUNK, CHUNK), 1)
    return r >= c


def _head_attention(qd, ki, ke, vh, et, st_ref, h, causal):
    qb = qd.astype(BF16)
    sc = lax.dot_general(qb, ki.astype(BF16), _NT, preferred_element_type=F32)
    sc = jnp.where(causal, sc, 0.0)
    st = st_ref[h]
    o = jnp.dot(sc.astype(BF16), vh, preferred_element_type=F32)
    o = o + lax.dot_general(qb, st.astype(BF16), _NT, preferred_element_type=F32)
    st_ref[h] = st * et + lax.dot_general(vh, ke.astype(BF16), _TN, preferred_element_type=F32)
    return o


def _gla_kernel(q_ref, k_ref, v_ref, g_ref, a_ref, wa_ref, ba_ref, nw_ref, o_ref, st_ref):
    @pl.when(pl.program_id(1) == 0)
    def _():
        st_ref[...] = jnp.zeros_like(st_ref)

    causal = _tri_mask()
    tri = causal.astype(BF16)
    n_chunks = q_ref.shape[0] // CHUNK

    def chunk(c, carry):
        r = pl.ds(pl.multiple_of(c * CHUNK, CHUNK), CHUNK)
        z = jnp.dot(a_ref[r, :], wa_ref[...], preferred_element_type=F32) + ba_ref[...]
        la = (jnp.minimum(z, 0.0) - jnp.log1p(jnp.exp(-jnp.abs(z)))) * (1.0 / GATE_TEMP)
        hi = la.astype(BF16)
        lo = (la - hi.astype(F32)).astype(BF16)
        cum = (jnp.dot(tri, hi, preferred_element_type=F32)
               + jnp.dot(tri, lo, preferred_element_type=F32))
        tot = cum[CHUNK - 1:CHUNK, :]
        qd = (q_ref[r, :].astype(F32) * (HEAD_DK ** -0.5)) * jnp.exp(cum)
        kf = k_ref[r, :].astype(F32)
        ki = kf * jnp.exp(-cum)
        ke = kf * jnp.exp(tot - cum)
        et = jnp.exp(tot)
        for h in range(N_HEADS):
            sl = slice(h * HEAD_PAD, (h + 1) * HEAD_PAD)
            o = _head_attention(qd[:, sl], ki[:, sl], ke[:, sl], v_ref[r, sl], et[:, sl],
                                st_ref, h, causal)
            ms = jnp.sum(o * o, axis=-1, keepdims=True) * (1.0 / HEAD_DV)
            y = (o * lax.rsqrt(ms + NORM_EPS)) * nw_ref[:, sl]
            g = g_ref[r, sl].astype(F32)
            o_ref[r, sl] = (y * (g * jax.nn.sigmoid(g))).astype(BF16)
        return carry

    lax.fori_loop(0, n_chunks, chunk, 0)


def _ret_kernel(q_ref, k_ref, v_ref, g_ref, cos_ref, sin_ref, dq_ref, dki_ref, dke_ref,
                dt_ref, nw_ref, o_ref, st_ref):
    @pl.when(pl.program_id(1) == 0)
    def _():
        st_ref[...] = jnp.zeros_like(st_ref)

    causal = _tri_mask()
    n_chunks = q_ref.shape[0] // CHUNK
    lane = lax.broadcasted_iota(jnp.int32, (CHUNK, HEAD_PAD), 1)
    real = lane < HEAD_DV

    def chunk(c, carry):
        r = pl.ds(pl.multiple_of(c * CHUNK, CHUNK), CHUNK)
        cos = cos_ref[r, :]
        sin = sin_ref[r, :]
        for h in range(N_HEADS):
            sl = slice(h * HEAD_PAD, (h + 1) * HEAD_PAD)
            qh = q_ref[r, sl].astype(F32)
            kh = k_ref[r, sl].astype(F32)
            qr = qh * cos + pltpu.roll(qh, 64, 1) * sin
            kr = (kh * cos + pltpu.roll(kh, 64, 1) * sin) * (HEAD_DK ** -0.5)
            o = _head_attention(qr * dq_ref[:, sl], kr * dki_ref[:, sl], kr * dke_ref[:, sl],
                                v_ref[r, sl], dt_ref[:, sl], st_ref, h, causal)
            mu = jnp.sum(o, axis=-1, keepdims=True) * (1.0 / HEAD_DV)
            oc = jnp.where(real, o - mu, 0.0)
            var = jnp.sum(oc * oc, axis=-1, keepdims=True) * (1.0 / HEAD_DV)
            y = (oc * lax.rsqrt(var + NORM_EPS)) * nw_ref[:, sl]
            g = g_ref[r, sl].astype(F32)
            o_ref[r, sl] = (y * (g * jax.nn.sigmoid(g))).astype(BF16)
        return carry

    lax.fori_loop(0, n_chunks, chunk, 0)


def _proj_spec(tl, width, col_off, per_b):
    cb = col_off // width
    return pl.BlockSpec((tl, width), lambda b, l: (b * per_b + l, cb))


def _full(shape):
    return pl.BlockSpec(shape, lambda b, l: (0,) * len(shape))


def _gla_mixer(proj, batch, seq, wa_p, ba_p, nw_p, tl=512):
    per_b = seq // tl
    return pl.pallas_call(
        _gla_kernel,
        grid=(batch, per_b),
        in_specs=[_proj_spec(tl, HP, OFF_GQ, per_b), _proj_spec(tl, HP, OFF_GK, per_b),
                  _proj_spec(tl, HP, OFF_GV, per_b), _proj_spec(tl, HP, OFF_GG, per_b),
                  _proj_spec(tl, LANES, OFF_GA, per_b),
                  _full((LANES, HP)), _full((1, HP)), _full((1, HP))],
        out_specs=pl.BlockSpec((tl, HP), lambda b, l: (b * per_b + l, 0)),
        out_shape=jax.ShapeDtypeStruct((batch * seq, HP), BF16),
        scratch_shapes=[pltpu.VMEM((N_HEADS, HEAD_PAD, HEAD_PAD), F32)],
        compiler_params=_cparams(("arbitrary", "arbitrary")),
        name="gla_mixer",
    )(proj, proj, proj, proj, proj, wa_p, ba_p, nw_p)


def _ret_mixer(proj, batch, seq, cos_t, sin_t, dq, dki, dke, dtot, nw_p, tl=512):
    per_b = seq // tl
    return pl.pallas_call(
        _ret_kernel,
        grid=(batch, per_b),
        in_specs=[_proj_spec(tl, HP, OFF_RQ, per_b), _proj_spec(tl, HP, OFF_RK, per_b),
                  _proj_spec(tl, HP, OFF_RV, per_b), _proj_spec(tl, HP, OFF_RG, per_b),
                  pl.BlockSpec((tl, HEAD_PAD), lambda b, l: (l, 0)),
                  pl.BlockSpec((tl, HEAD_PAD), lambda b, l: (l, 0)),
                  _full((CHUNK, HP)), _full((CHUNK, HP)), _full((CHUNK, HP)),
                  _full((1, HP)), _full((1, HP))],
        out_specs=pl.BlockSpec((tl, HP), lambda b, l: (b * per_b + l, 0)),
        out_shape=jax.ShapeDtypeStruct((batch * seq, HP), BF16),
        scratch_shapes=[pltpu.VMEM((N_HEADS, HEAD_PAD, HEAD_PAD), F32)],
        compiler_params=_cparams(("arbitrary", "arbitrary")),
        name="ret_mixer",
    )(proj, proj, proj, proj, cos_t, sin_t, dq, dki, dke, dtot, nw_p)


S5_N = S5_GROUPS * S5_STATE


def _gelu_tanh(x):
    return 0.5 * x * (1.0 + jnp.tanh(np.sqrt(2.0 / np.pi) * (x + 0.044715 * (x * x * x))))


def _s5_kernel(u_ref, bb_ref, cb_ref, nr_ref, ni_ref, pr_ref, pi_ref, lr_ref, li_ref,
               d_ref, wg_ref, bg_ref, o_ref, sr_ref, si_ref):
    @pl.when(pl.program_id(1) == 0)
    def _():
        sr_ref[...] = jnp.zeros_like(sr_ref)
        si_ref[...] = jnp.zeros_like(si_ref)

    tri = _tri_mask().astype(BF16)
    n_chunks = u_ref.shape[0] // CHUNK

    def split_dot(x):
        hi = x.astype(BF16)
        lo = (x - hi.astype(F32)).astype(BF16)
        return (jnp.dot(tri, hi, preferred_element_type=F32)
                + jnp.dot(tri, lo, preferred_element_type=F32))

    def chunk(c, carry):
        r = pl.ds(pl.multiple_of(c * CHUNK, CHUNK), CHUNK)
        u = u_ref[r, :]
        x = jnp.dot(u, bb_ref[...], preferred_element_type=F32)
        xr, xi = x[:, :S5_N], x[:, S5_N:]
        nr, ni = nr_ref[...], ni_ref[...]
        p_r = split_dot(xr * nr - xi * ni)
        p_i = split_dot(xr * ni + xi * nr)
        s0r, s0i = sr_ref[...], si_ref[...]
        lr, li = lr_ref[...], li_ref[...]
        q_r = p_r + (s0r * lr - s0i * li)
        q_i = p_i + (s0r * li + s0i * lr)
        pr, pi = pr_ref[...], pi_ref[...]
        s_r = q_r * pr - q_i * pi
        s_i = q_r * pi + q_i * pr
        sr_ref[...] = s_r[CHUNK - 1:CHUNK, :]
        si_ref[...] = s_i[CHUNK - 1:CHUNK, :]
        y = (jnp.dot(s_r.astype(BF16), cb_ref[:S5_N, :], preferred_element_type=F32)
             + jnp.dot(s_i.astype(BF16), cb_ref[S5_N:, :], preferred_element_type=F32))
        y = _gelu_tanh(y + d_ref[...] * u.astype(F32))
        gate = jnp.dot(y.astype(BF16), wg_ref[...], preferred_element_type=F32) + bg_ref[...]
        o_ref[r, :] = (y * jax.nn.sigmoid(gate)).astype(BF16)
        return carry

    lax.fori_loop(0, n_chunks, chunk, 0)


def _s5_mixer(proj, batch, seq, tabs, tl=512):
    per_b = seq // tl
    bb, cb, nr, ni, pr, pi, lr, li, dsk, wg, bg = tabs
    return pl.pallas_call(
        _s5_kernel,
        grid=(batch, per_b),
        in_specs=[_proj_spec(tl, S5_WIDTH, OFF_SU, per_b),
                  _full((S5_WIDTH, 2 * S5_N)), _full((2 * S5_N, S5_WIDTH)),
                  _full((CHUNK, S5_N)), _full((CHUNK, S5_N)),
                  _full((CHUNK, S5_N)), _full((CHUNK, S5_N)),
                  _full((1, S5_N)), _full((1, S5_N)),
                  _full((1, S5_WIDTH)), _full((S5_WIDTH, S5_WIDTH)), _full((1, S5_WIDTH))],
        out_specs=pl.BlockSpec((tl, S5_WIDTH), lambda b, l: (b * per_b + l, 0)),
        out_shape=jax.ShapeDtypeStruct((batch * seq, S5_WIDTH), BF16),
        scratch_shapes=[pltpu.VMEM((1, S5_N), F32), pltpu.VMEM((1, S5_N), F32)],
        compiler_params=_cparams(("arbitrary", "arbitrary")),
        name="s5_mixer",
    )(proj, bb, cb, nr, ni, pr, pi, lr, li, dsk, wg, bg)


def _s5_tables(a_re, a_im, log_dt, b_re, b_im, c_re, c_im, d_skip, w_glu, b_glu):
    lam = lax.complex(a_re, a_im)
    dt = jnp.exp(log_dt)[:, None]
    lam_bar = jnp.exp(lam * dt)
    b_bar = ((lam_bar - 1.0) / lam)[..., None] * lax.complex(b_re, b_im)
    eye = jnp.eye(S5_GROUPS, dtype=F32)
    def blk_b(m):
        return jnp.einsum('gph,gk->ghkp', m, eye).reshape(S5_WIDTH, S5_N)
    bb = jnp.concatenate([blk_b(jnp.real(b_bar)), blk_b(jnp.imag(b_bar))], axis=1)
    def blk_c(m):
        return jnp.einsum('ghp,gk->kpgh', m, eye).reshape(S5_N, S5_WIDTH)
    cb = jnp.concatenate([blk_c(c_re), blk_c(-c_im)], axis=0)
    steps = jnp.arange(CHUNK, dtype=F32)[:, None, None]
    lam_dt = (lam * dt)[None]
    pos = jnp.exp(lam_dt * steps).reshape(CHUNK, S5_N)
    neg = jnp.exp(-lam_dt * steps).reshape(CHUNK, S5_N)
    one = lam_bar.reshape(1, S5_N)
    return (bb.astype(BF16), cb.astype(BF16), jnp.real(neg), jnp.imag(neg), jnp.real(pos),
            jnp.imag(pos), jnp.real(one), jnp.imag(one), d_skip.reshape(1, S5_WIDTH),
            w_glu.astype(BF16), b_glu.reshape(1, S5_WIDTH))


def _router_kernel(og_ref, or_ref, os_ref, wg_ref, wr_ref, ws_ref, x_ref, g1_ref, sc_ref,
                   sh_ref, nw_ref, rw_ref, rb_ref,
                   x1_ref, h_ref, idx_ref, gate_ref, rank_ref, cnt_ref, carry_ref):
    i = pl.program_id(0)

    @pl.when(i == 0)
    def _():
        carry_ref[...] = jnp.zeros_like(carry_ref)

    mix = (jnp.dot(og_ref[...], wg_ref[...], preferred_element_type=F32)
           + jnp.dot(or_ref[...], wr_ref[...], preferred_element_type=F32)
           + jnp.dot(os_ref[...], ws_ref[...], preferred_element_type=F32))
    x1 = x_ref[...] + g1_ref[0] * mix
    x1_ref[...] = x1
    hdn = _rms_mod(x1, nw_ref[...], sc_ref[0], sh_ref[0])
    h_ref[...] = hdn
    logits = jnp.dot(hdn, rw_ref[...], preferred_element_type=F32,
                     precision=lax.Precision.HIGHEST) + rb_ref[...]
    tm = logits.shape[0]
    lane = lax.broadcasted_iota(jnp.int32, (tm, LANES), 1)
    work = logits
    onehot = jnp.zeros((tm, LANES), F32)
    vals, idxs = [], []
    for _ in range(TOP_K):
        m = jnp.max(work, axis=-1, keepdims=True)
        ix = jnp.min(jnp.where(work == m, lane, LANES), axis=-1, keepdims=True)
        sel = lane == ix
        work = jnp.where(sel, -jnp.inf, work)
        onehot = onehot + sel.astype(F32)
        vals.append(m)
        idxs.append(ix)
    exps = [jnp.exp(v - vals[0]) for v in vals]
    denom = exps[0] + exps[1] + exps[2] + exps[3]
    r = lax.broadcasted_iota(jnp.int32, (tm, tm), 0)
    c = lax.broadcasted_iota(jnp.int32, (tm, tm), 1)
    strict = (r > c).astype(BF16)
    before = jnp.dot(strict, onehot.astype(BF16), preferred_element_type=F32) + carry_ref[...]
    idx_out = jnp.zeros((tm, LANES), jnp.int32)
    gate_out = jnp.zeros((tm, LANES), F32)
    rank_out = jnp.zeros((tm, LANES), F32)
    for k in range(TOP_K):
        rk = jnp.sum(jnp.where(lane == idxs[k], before, 0.0), axis=-1, keepdims=True)
        idx_out = jnp.where(lane == k, idxs[k], idx_out)
        gate_out = jnp.where(lane == k, exps[k] / denom, gate_out)
        rank_out = jnp.where(lane == k, rk, rank_out)
    idx_ref[...] = idx_out
    gate_ref[...] = gate_out
    rank_ref[...] = rank_out.astype(jnp.int32)
    total = carry_ref[...] + jnp.sum(onehot, axis=0, keepdims=True)
    carry_ref[...] = total
    cnt_ref[...] = total


def _outproj_router(o_gla, o_ret, o_s5, wg, wr, ws, x2d, g1, sc2, sh2, nw2, rw_p, rb_p, seq,
                    tm=256):
    t, d = x2d.shape
    per_b = seq // tm
    row = lambda w: pl.BlockSpec((tm, w), lambda i: (i, 0))
    full = lambda s: pl.BlockSpec(s, lambda i: (0,) * len(s))
    per_batch = pl.BlockSpec((1, 1, d), lambda i: (i // per_b, 0, 0))
    return pl.pallas_call(
        _router_kernel,
        grid=(t // tm,),
        in_specs=[row(HP), row(HP), row(S5_WIDTH), full((HP, d)), full((HP, d)),
                  full((S5_WIDTH, d)), row(d), per_batch, per_batch, per_batch,
                  full((1, d)), full((d, LANES)), full((1, LANES))],
        out_specs=[row(d), row(d), row(LANES), row(LANES), row(LANES), full((1, LANES))],
        out_shape=[jax.ShapeDtypeStruct((t, d), F32), jax.ShapeDtypeStruct((t, d), F32),
                   jax.ShapeDtypeStruct((t, LANES), jnp.int32),
                   jax.ShapeDtypeStruct((t, LANES), F32),
                   jax.ShapeDtypeStruct((t, LANES), jnp.int32),
                   jax.ShapeDtypeStruct((1, LANES), F32)],
        scratch_shapes=[pltpu.VMEM((1, LANES), F32)],
        compiler_params=_cparams(("arbitrary",)),
        name="outproj_router",
    )(o_gla, o_ret, o_s5, wg, wr, ws, x2d, g1, sc2, sh2, nw2, rw_p, rb_p)


def _row_copy(src_ref, src_row, dst_ref, dst_row, sem):
    return pltpu.make_async_copy(src_ref.at[pl.ds(src_row, 1), :],
                                 dst_ref.at[pl.ds(dst_row, 1), :], sem)


def _dispatch_kernel(dest_hbm, h_ref, rows_in, rows_out, idx_smem, sem_idx, sem):
    del rows_in
    i = pl.program_id(0)
    cp = pltpu.make_async_copy(dest_hbm.at[i], idx_smem, sem_idx)
    cp.start()
    cp.wait()
    tf = h_ref.shape[0]

    def issue(j, carry):
        for k in range(TOP_K):
            _row_copy(h_ref, j, rows_out, idx_smem[j * TOP_K + k], sem).start()
        return carry

    lax.fori_loop(0, tf, issue, 0)

    def drain(j, carry):
        for k in range(TOP_K):
            _row_copy(h_ref, j, rows_out, idx_smem[j * TOP_K + k], sem).wait()
        return carry

    lax.fori_loop(0, tf, drain, 0)


def _dispatch(dest, hdn, n_rows, tf=256):
    t, d = hdn.shape
    steps = t // tf
    rows0 = jnp.zeros((n_rows, d), F32)
    return pl.pallas_call(
        _dispatch_kernel,
        grid=(steps,),
        in_specs=[pl.BlockSpec(memory_space=pl.ANY),
                  pl.BlockSpec((tf, d), lambda i: (i, 0)),
                  pl.BlockSpec(memory_space=pl.ANY)],
        out_specs=pl.BlockSpec(memory_space=pl.ANY),
        out_shape=jax.ShapeDtypeStruct((n_rows, d), F32),
        scratch_shapes=[pltpu.SMEM((tf * TOP_K,), jnp.int32),
                        pltpu.SemaphoreType.DMA, pltpu.SemaphoreType.DMA],
        input_output_aliases={2: 0},
        compiler_params=_cparams(("arbitrary",)),
        name="moe_dispatch",
    )(dest.reshape(steps, tf * TOP_K), hdn, rows0)


def _expert_kernel(be_ref, nu_ref, rows_ref, wu_ref, bu_ref, wd_ref, bd_ref, o_ref,
                   wu_bf, wd_bf):
    i = pl.program_id(0)
    e = be_ref[i]
    prev = be_ref[jnp.maximum(i - 1, 0)]

    @pl.when((i == 0) | (e != prev))
    def _():
        wu_bf[...] = wu_ref[0].astype(BF16)
        wd_bf[...] = wd_ref[0].astype(BF16)

    @pl.when(i < nu_ref[0])
    def _():
        x = rows_ref[...].astype(BF16)
        up = jnp.dot(x, wu_bf[...], preferred_element_type=F32) + bu_ref[0]
        x_glu = jnp.minimum(up[:, :D_FF], SWIGLU_LIMIT)
        x_lin = jnp.clip(up[:, D_FF:], -SWIGLU_LIMIT, SWIGLU_LIMIT)
        act = x_glu * jax.nn.sigmoid(SWIGLU_ALPHA * x_glu) * (x_lin + 1.0)
        o_ref[...] = jnp.dot(act.astype(BF16), wd_bf[...], preferred_element_type=F32) + bd_ref[0]

    @pl.when(i >= nu_ref[0])
    def _():
        o_ref[...] = jnp.zeros_like(o_ref)


def _experts(block_e, n_used, rows, w_up, b_up, w_down, b_down):
    n_rows, d = rows.shape
    n_blocks = n_rows // ROW_BLK
    ne, _, f2 = w_up.shape
    grid_spec = pltpu.PrefetchScalarGridSpec(
        num_scalar_prefetch=2,
        grid=(n_blocks,),
        in_specs=[pl.BlockSpec((ROW_BLK, d), lambda i, be, nu: (i, 0)),
                  pl.BlockSpec((1, d, f2), lambda i, be, nu: (be[i], 0, 0)),
                  pl.BlockSpec((1, 1, f2), lambda i, be, nu: (be[i], 0, 0)),
                  pl.BlockSpec((1, D_FF, d), lambda i, be, nu: (be[i], 0, 0)),
                  pl.BlockSpec((1, 1, d), lambda i, be, nu: (be[i], 0, 0))],
        out_specs=pl.BlockSpec((ROW_BLK, d), lambda i, be, nu: (i, 0)),
        scratch_shapes=[pltpu.VMEM((d, f2), BF16), pltpu.VMEM((D_FF, d), BF16)],
    )
    return pl.pallas_call(
        _expert_kernel,
        grid_spec=grid_spec,
        out_shape=jax.ShapeDtypeStruct((n_rows, d), F32),
        compiler_params=_cparams(("arbitrary",)),
        name="moe_experts",
    )(block_e, n_used, rows, w_up, b_up.reshape(ne, 1, f2), w_down, b_down.reshape(ne, 1, d))


def _combine_kernel(dest_hbm, gate_ref, x1_ref, g2_ref, fw_ref, rows_hbm, o_ref,
                    buf, idx_smem, sem_idx, sem, *, final):
    i = pl.program_id(0)
    cp = pltpu.make_async_copy(dest_hbm.at[i], idx_smem, sem_idx)
    cp.start()
    cp.wait()
    th = x1_ref.shape[0]

    def issue(j, carry):
        for k in range(TOP_K):
            _row_copy(rows_hbm, idx_smem[j * TOP_K + k], buf.at[k], j, sem).start()
        return carry

    lax.fori_loop(0, th, issue, 0)

    def drain(j, carry):
        for k in range(TOP_K):
            _row_copy(rows_hbm, idx_smem[j * TOP_K + k], buf.at[k], j, sem).wait()
        return carry

    lax.fori_loop(0, th, drain, 0)
    gates = gate_ref[...]
    y = gates[:, 0:1] * buf[0]
    for k in range(1, TOP_K):
        y = y + gates[:, k:k + 1] * buf[k]
    x2 = x1_ref[...] + g2_ref[0] * y
    if final:
        x2 = (x2 * lax.rsqrt(jnp.mean(x2 * x2, axis=-1, keepdims=True) + NORM_EPS)) * fw_ref[...]
    o_ref[...] = x2


def _combine(dest, gates, x1, g2, fw, out_rows, seq, final, th=128):
    t, d = x1.shape
    steps = t // th
    per_b = seq // th
    return pl.pallas_call(
        functools.partial(_combine_kernel, final=final),
        grid=(steps,),
        in_specs=[pl.BlockSpec(memory_space=pl.ANY),
                  pl.BlockSpec((th, LANES), lambda i: (i, 0)),
                  pl.BlockSpec((th, d), lambda i: (i, 0)),
                  pl.BlockSpec((1, 1, d), lambda i: (i // per_b, 0, 0)),
                  pl.BlockSpec((1, d), lambda i: (0, 0)),
                  pl.BlockSpec(memory_space=pl.ANY)],
        out_specs=pl.BlockSpec((th, d), lambda i: (i, 0)),
        out_shape=jax.ShapeDtypeStruct((t, d), F32),
        scratch_shapes=[pltpu.VMEM((TOP_K, th, d), F32),
                        pltpu.SMEM((th * TOP_K,), jnp.int32),
                        pltpu.SemaphoreType.DMA, pltpu.SemaphoreType.DMA],
        compiler_params=_cparams(("arbitrary",)),
        name="moe_combine",
    )(dest.reshape(steps, th * TOP_K), gates, x1, g2, fw, out_rows)


def _retention_tables(seq):
    pos = jnp.arange(seq, dtype=F32)
    inv_freq = ROPE_BASE ** (-jnp.arange(0, HEAD_DK, 2, dtype=F32) / HEAD_DK)
    ang = pos[:, None] * inv_freq[None, :]
    cos, sin = jnp.cos(ang), jnp.sin(ang)
    half = HEAD_DK // 2
    zpad = jnp.zeros((seq, 64 - half), F32)
    cos_t = jnp.concatenate([cos, zpad, cos, zpad], axis=1)
    sin_t = jnp.concatenate([-sin, zpad, sin, zpad], axis=1)
    log_gamma = jnp.log1p(-jnp.exp2(-5.0 - jnp.arange(N_HEADS, dtype=F32)))
    log_decay = jnp.broadcast_to(log_gamma[None, :, None], (CHUNK, N_HEADS, HEAD_PAD))
    cum = jnp.cumsum(log_decay, axis=0)
    tot = cum[-1:]
    shp = lambda a: a.reshape(a.shape[0], HP)
    return (cos_t, sin_t, shp(jnp.exp(cum)), shp(jnp.exp(-cum)), shp(jnp.exp(tot - cum)),
            shp(jnp.exp(tot)))


def kernel(x, c, norm1_w, norm2_w, w_mod, b_mod, w_in, gla_w_a2, gla_b_a, gla_norm_w, ret_norm_w, s5_a_re, s5_a_im, s5_log_dt, s5_b_re, s5_b_im, s5_c_re, s5_c_im, s5_d, s5_w_glu, s5_b_glu, w_out, router_w, router_b, w_up, b_up, w_down, b_down, final_norm_w):
    batch, seq, d = x.shape
    depth = w_mod.shape[0]
    t = batch * seq
    n_slots = t * TOP_K
    n_blocks = n_slots // ROW_BLK + N_EXPERTS
    n_rows = n_blocks * ROW_BLK

    mod = _modulation(c, w_mod, b_mod)
    ret_tabs = _retention_tables(seq)
    x2d = x.reshape(t, d)

    for i in range(depth):
        m6 = mod[i].reshape(batch, 6, 1, d)
        sh1, sc1, g1, sh2, sc2, g2 = (m6[:, j] for j in range(6))

        w_p = _take_cols(w_in[i], _IN_SRC).astype(BF16)
        wa_p = jnp.zeros((LANES, HP), F32).at[:GATE_RANK].set(_take_cols(gla_w_a2[i], _DK_SRC))
        ba_p = _take_cols(gla_b_a[i], _DK_SRC).reshape(1, HP)
        gnw = _take_cols(gla_norm_w[i], _DV_SRC).reshape(1, HP)
        rnw = _take_cols(ret_norm_w[i], _DV_SRC).reshape(1, HP)
        kv = N_HEADS * HEAD_DV
        wo_g = _take_rows(w_out[i, :kv], _DV_SRC).astype(BF16)
        wo_r = _take_rows(w_out[i, kv:2 * kv], _DV_SRC).astype(BF16)
        wo_s = w_out[i, 2 * kv:].astype(BF16)
        rw_p = jnp.zeros((d, LANES), F32).at[:, :N_EXPERTS].set(router_w[i])
        rb_p = jnp.full((1, LANES), -1e30, F32).at[0, :N_EXPERTS].set(router_b[i])

        proj = _in_projection(x2d, sc1, sh1, norm1_w[i].reshape(1, d), w_p, seq)
        o_gla = _gla_mixer(proj, batch, seq, wa_p.astype(BF16), ba_p, gnw)
        o_ret = _ret_mixer(proj, batch, seq, *ret_tabs, rnw)
        s5_tabs = _s5_tables(s5_a_re[i], s5_a_im[i], s5_log_dt[i], s5_b_re[i], s5_b_im[i],
                             s5_c_re[i], s5_c_im[i], s5_d[i], s5_w_glu[i], s5_b_glu[i])
        o_s5 = _s5_mixer(proj, batch, seq, s5_tabs)

        x1, hdn, idx, gates, rank, counts = _outproj_router(
            o_gla, o_ret, o_s5, wo_g, wo_r, wo_s, x2d, g1, sc2, sh2,
            norm2_w[i].reshape(1, d), rw_p, rb_p, seq)

        cnt = counts[0, :N_EXPERTS].astype(jnp.int32)
        padded = (cnt + ROW_BLK - 1) // ROW_BLK * ROW_BLK
        pad_ends = jnp.cumsum(padded)
        pad_starts = pad_ends - padded
        dest = (pad_starts[idx[:, :TOP_K]] + rank[:, :TOP_K]).astype(jnp.int32)
        block_e = jnp.minimum(
            jnp.searchsorted(pad_ends, jnp.arange(n_blocks, dtype=jnp.int32) * ROW_BLK,
                             side='right'), N_EXPERTS - 1).astype(jnp.int32)
        n_used = (pad_ends[-1:] // ROW_BLK).astype(jnp.int32)

        rows = _dispatch(dest, hdn, n_rows)
        out_rows = _experts(block_e, n_used, rows, w_up[i], b_up[i], w_down[i], b_down[i])
        x2d = _combine(dest, gates, x1, g2, final_norm_w.reshape(1, d), out_rows, seq,
                       final=(i == depth - 1))

    return x2d.reshape(batch, seq, d)
```

```python
import functools

import numpy as np
import jax
import jax.numpy as jnp
from jax import lax
from jax.experimental import pallas as pl
from jax.experimental.pallas import tpu as pltpu
from jax.experimental.pallas import tpu_sc as plsc

D_MODEL = 1024
CHUNK = 64
NORM_EPS = 1e-5
N_HEADS = 4
HEAD_DK = 48
HEAD_DV = 96
GATE_RANK = 16
GATE_TEMP = 16.0
ROPE_BASE = 10000.0
S5_WIDTH = 256
S5_GROUP_DIM = 16
S5_GROUPS = 16
S5_STATE = 64
N_EXPERTS = 32
TOP_K = 4
D_FF = 1024
SWIGLU_LIMIT = 7.0
SWIGLU_ALPHA = 1.702

LANES = 128
HEAD_PAD = LANES
HP = N_HEADS * HEAD_PAD
VMEM_LIMIT = 56 * 1024 * 1024

OFF_GQ, OFF_GK, OFF_GV, OFF_GG = 0, HP, 2 * HP, 3 * HP
OFF_RQ, OFF_RK, OFF_RV, OFF_RG = 4 * HP, 5 * HP, 6 * HP, 7 * HP
OFF_SU = 8 * HP
OFF_GA = OFF_SU + S5_WIDTH
NP_COLS = OFF_GA + LANES
PROJ_CH = 896

ROW_BLK = 256

F32 = jnp.float32
BF16 = jnp.bfloat16


def _in_col_map():
    src = -np.ones((NP_COLS,), np.int64)
    kq = N_HEADS * HEAD_DK
    kv = N_HEADS * HEAD_DV
    base = dict(gq=0, gk=kq, gv=2 * kq, gg=2 * kq + kv, ga=2 * kq + 2 * kv)
    r0 = base['ga'] + GATE_RANK
    base.update(rq=r0, rk=r0 + kq, rv=r0 + 2 * kq, rg=r0 + 2 * kq + kv, su=r0 + 2 * kq + 2 * kv)
    half = HEAD_DK // 2
    for h in range(N_HEADS):
        for d in range(HEAD_DK):
            src[OFF_GQ + h * HEAD_PAD + d] = base['gq'] + h * HEAD_DK + d
            src[OFF_GK + h * HEAD_PAD + d] = base['gk'] + h * HEAD_DK + d
            lane = d if d < half else 64 + (d - half)
            src[OFF_RQ + h * HEAD_PAD + lane] = base['rq'] + h * HEAD_DK + d
            src[OFF_RK + h * HEAD_PAD + lane] = base['rk'] + h * HEAD_DK + d
        for d in range(HEAD_DV):
            src[OFF_GV + h * HEAD_PAD + d] = base['gv'] + h * HEAD_DV + d
            src[OFF_GG + h * HEAD_PAD + d] = base['gg'] + h * HEAD_DV + d
            src[OFF_RV + h * HEAD_PAD + d] = base['rv'] + h * HEAD_DV + d
            src[OFF_RG + h * HEAD_PAD + d] = base['rg'] + h * HEAD_DV + d
    src[OFF_SU:OFF_SU + S5_WIDTH] = base['su'] + np.arange(S5_WIDTH)
    src[OFF_GA:OFF_GA + GATE_RANK] = base['ga'] + np.arange(GATE_RANK)
    return src


_IN_SRC = _in_col_map()


def _head_pad_map(width):
    src = -np.ones((HP,), np.int64)
    for h in range(N_HEADS):
        src[h * HEAD_PAD:h * HEAD_PAD + width] = h * width + np.arange(width)
    return src


_DV_SRC = _head_pad_map(HEAD_DV)
_DK_SRC = _head_pad_map(HEAD_DK)


def _take_cols(w, src):
    out = jnp.take(w, jnp.asarray(np.maximum(src, 0)), axis=-1)
    return jnp.where(jnp.asarray(src >= 0), out, 0)


def _take_rows(w, src):
    out = jnp.take(w, jnp.asarray(np.maximum(src, 0)), axis=0)
    return jnp.where(jnp.asarray(src >= 0)[:, None], out, 0)


def _cparams(sem):
    return pltpu.CompilerParams(dimension_semantics=sem, vmem_limit_bytes=VMEM_LIMIT)


def _mod_kernel(c_ref, w_ref, b_ref, o_ref):
    c = c_ref[...]
    cond = c * jax.nn.sigmoid(c)
    o_ref[0] = jnp.dot(cond, w_ref[0], preferred_element_type=F32,
                       precision=lax.Precision.HIGHEST) + b_ref[0]


def _modulation(c, w_mod, b_mod):
    depth, d, n = w_mod.shape
    b = c.shape[0]
    nb = 1536
    return pl.pallas_call(
        _mod_kernel,
        grid=(depth, n // nb),
        in_specs=[pl.BlockSpec((b, d), lambda l, j: (0, 0)),
                  pl.BlockSpec((1, d, nb), lambda l, j: (l, 0, j)),
                  pl.BlockSpec((1, 1, nb), lambda l, j: (l, 0, j))],
        out_specs=pl.BlockSpec((1, b, nb), lambda l, j: (l, 0, j)),
        out_shape=jax.ShapeDtypeStruct((depth, b, n), F32),
        compiler_params=_cparams(("arbitrary", "arbitrary")),
        name="adaln_mod",
    )(c, w_mod, b_mod.reshape(depth, 1, n))


def _rms_mod(x, nw, sc, sh):
    y = x * lax.rsqrt(jnp.mean(x * x, axis=-1, keepdims=True) + NORM_EPS)
    return (y * nw) * (1.0 + sc) + sh


def _inproj_kernel(x_ref, sc_ref, sh_ref, nw_ref, w_ref, o_ref):
    h = _rms_mod(x_ref[...], nw_ref[...], sc_ref[0], sh_ref[0]).astype(BF16)
    for j in range(NP_COLS // PROJ_CH):
        cs = slice(j * PROJ_CH, (j + 1) * PROJ_CH)
        o_ref[:, cs] = jnp.dot(h, w_ref[:, cs], preferred_element_type=F32).astype(BF16)


def _in_projection(x2d, sc, sh, nw, w_p, seq, tm=512):
    t, d = x2d.shape
    per_b = seq // tm
    return pl.pallas_call(
        _inproj_kernel,
        grid=(t // tm,),
        in_specs=[pl.BlockSpec((tm, d), lambda i: (i, 0)),
                  pl.BlockSpec((1, 1, d), lambda i: (i // per_b, 0, 0)),
                  pl.BlockSpec((1, 1, d), lambda i: (i // per_b, 0, 0)),
                  pl.BlockSpec((1, d), lambda i: (0, 0)),
                  pl.BlockSpec((d, NP_COLS), lambda i: (0, 0))],
        out_specs=pl.BlockSpec((tm, NP_COLS), lambda i: (i, 0)),
        out_shape=jax.ShapeDtypeStruct((t, NP_COLS), BF16),
        compiler_params=_cparams(("arbitrary",)),
        name="in_proj",
    )(x2d, sc, sh, nw, w_p)


_NT = (((1,), (1,)), ((), ()))
_TN = (((0,), (0,)), ((), ()))


def _tri_mask():
    r = lax.broadcasted_iota(jnp.int32, (CHUNK, CHUNK), 0)
    c = lax.broadcasted_iota(jnp.int32, (CHUNK, CHUNK), 1)
    return r >= c


def _head_attention(qd, ki, ke, vh, et, st_ref, h, causal):
    qb = qd.astype(BF16)
    sc = lax.dot_general(qb, ki.astype(BF16), _NT, preferred_element_type=F32)
    sc = jnp.where(causal, sc, 0.0)
    st = st_ref[h]
    o = jnp.dot(sc.astype(BF16), vh, preferred_element_type=F32)
    o = o + lax.dot_general(qb, st.astype(BF16), _NT, preferred_element_type=F32)
    st_ref[h] = st * et + lax.dot_general(vh, ke.astype(BF16), _TN, preferred_element_type=F32)
    return o


def _gla_kernel(q_ref, k_ref, v_ref, g_ref, a_ref, wa_ref, ba_ref, nw_ref, o_ref, st_ref):
    @pl.when(pl.program_id(1) == 0)
    def _():
        st_ref[...] = jnp.zeros_like(st_ref)

    causal = _tri_mask()
    tri = causal.astype(BF16)
    n_chunks = q_ref.shape[0] // CHUNK

    def chunk(c, carry):
        r = pl.ds(pl.multiple_of(c * CHUNK, CHUNK), CHUNK)
        z = jnp.dot(a_ref[r, :], wa_ref[...], preferred_element_type=F32) + ba_ref[...]
        la = (jnp.minimum(z, 0.0) - jnp.log1p(jnp.exp(-jnp.abs(z)))) * (1.0 / GATE_TEMP)
        hi = la.astype(BF16)
        lo = (la - hi.astype(F32)).astype(BF16)
        cum = (jnp.dot(tri, hi, preferred_element_type=F32)
               + jnp.dot(tri, lo, preferred_element_type=F32))
        tot = cum[CHUNK - 1:CHUNK, :]
        qd = (q_ref[r, :].astype(F32) * (HEAD_DK ** -0.5)) * jnp.exp(cum)
        kf = k_ref[r, :].astype(F32)
        ki = kf * jnp.exp(-cum)
        ke = kf * jnp.exp(tot - cum)
        et = jnp.exp(tot)
        for h in range(N_HEADS):
            sl = slice(h * HEAD_PAD, (h + 1) * HEAD_PAD)
            o = _head_attention(qd[:, sl], ki[:, sl], ke[:, sl], v_ref[r, sl], et[:, sl],
                                st_ref, h, causal)
            ms = jnp.sum(o * o, axis=-1, keepdims=True) * (1.0 / HEAD_DV)
            y = (o * lax.rsqrt(ms + NORM_EPS)) * nw_ref[:, sl]
            g = g_ref[r, sl].astype(F32)
            o_ref[r, sl] = (y * (g * jax.nn.sigmoid(g))).astype(BF16)
        return carry

    lax.fori_loop(0, n_chunks, chunk, 0)


def _ret_kernel(q_ref, k_ref, v_ref, g_ref, cos_ref, sin_ref, dq_ref, dki_ref, dke_ref,
                dt_ref, nw_ref, o_ref, st_ref):
    @pl.when(pl.program_id(1) == 0)
    def _():
        st_ref[...] = jnp.zeros_like(st_ref)

    causal = _tri_mask()
    n_chunks = q_ref.shape[0] // CHUNK
    lane = lax.broadcasted_iota(jnp.int32, (CHUNK, HEAD_PAD), 1)
    real = lane < HEAD_DV

    def chunk(c, carry):
        r = pl.ds(pl.multiple_of(c * CHUNK, CHUNK), CHUNK)
        cos = cos_ref[r, :]
        sin = sin_ref[r, :]
        for h in range(N_HEADS):
            sl = slice(h * HEAD_PAD, (h + 1) * HEAD_PAD)
            qh = q_ref[r, sl].astype(F32)
            kh = k_ref[r, sl].astype(F32)
            qr = qh * cos + pltpu.roll(qh, 64, 1) * sin
            kr = (kh * cos + pltpu.roll(kh, 64, 1) * sin) * (HEAD_DK ** -0.5)
            o = _head_attention(qr * dq_ref[:, sl], kr * dki_ref[:, sl], kr * dke_ref[:, sl],
                                v_ref[r, sl], dt_ref[:, sl], st_ref, h, causal)
            mu = jnp.sum(o, axis=-1, keepdims=True) * (1.0 / HEAD_DV)
            oc = jnp.where(real, o - mu, 0.0)
            var = jnp.sum(oc * oc, axis=-1, keepdims=True) * (1.0 / HEAD_DV)
            y = (oc * lax.rsqrt(var + NORM_EPS)) * nw_ref[:, sl]
            g = g_ref[r, sl].astype(F32)
            o_ref[r, sl] = (y * (g * jax.nn.sigmoid(g))).astype(BF16)
        return carry

    lax.fori_loop(0, n_chunks, chunk, 0)


def _proj_spec(tl, width, col_off, per_b):
    cb = col_off // width
    return pl.BlockSpec((tl, width), lambda b, l: (b * per_b + l, cb))


def _full(shape):
    return pl.BlockSpec(shape, lambda b, l: (0,) * len(shape))


def _gla_mixer(proj, batch, seq, wa_p, ba_p, nw_p, tl=512):
    per_b = seq // tl
    return pl.pallas_call(
        _gla_kernel,
        grid=(batch, per_b),
        in_specs=[_proj_spec(tl, HP, OFF_GQ, per_b), _proj_spec(tl, HP, OFF_GK, per_b),
                  _proj_spec(tl, HP, OFF_GV, per_b), _proj_spec(tl, HP, OFF_GG, per_b),
                  _proj_spec(tl, LANES, OFF_GA, per_b),
                  _full((LANES, HP)), _full((1, HP)), _full((1, HP))],
        out_specs=pl.BlockSpec((tl, HP), lambda b, l: (b * per_b + l, 0)),
        out_shape=jax.ShapeDtypeStruct((batch * seq, HP), BF16),
        scratch_shapes=[pltpu.VMEM((N_HEADS, HEAD_PAD, HEAD_PAD), F32)],
        compiler_params=_cparams(("arbitrary", "arbitrary")),
        name="gla_mixer",
    )(proj, proj, proj, proj, proj, wa_p, ba_p, nw_p)


def _ret_mixer(proj, batch, seq, cos_t, sin_t, dq, dki, dke, dtot, nw_p, tl=512):
    per_b = seq // tl
    return pl.pallas_call(
        _ret_kernel,
        grid=(batch, per_b),
        in_specs=[_proj_spec(tl, HP, OFF_RQ, per_b), _proj_spec(tl, HP, OFF_RK, per_b),
                  _proj_spec(tl, HP, OFF_RV, per_b), _proj_spec(tl, HP, OFF_RG, per_b),
                  pl.BlockSpec((tl, HEAD_PAD), lambda b, l: (l, 0)),
                  pl.BlockSpec((tl, HEAD_PAD), lambda b, l: (l, 0)),
                  _full((CHUNK, HP)), _full((CHUNK, HP)), _full((CHUNK, HP)),
                  _full((1, HP)), _full((1, HP))],
        out_specs=pl.BlockSpec((tl, HP), lambda b, l: (b * per_b + l, 0)),
        out_shape=jax.ShapeDtypeStruct((batch * seq, HP), BF16),
        scratch_shapes=[pltpu.VMEM((N_HEADS, HEAD_PAD, HEAD_PAD), F32)],
        compiler_params=_cparams(("arbitrary", "arbitrary")),
        name="ret_mixer",
    )(proj, proj, proj, proj, cos_t, sin_t, dq, dki, dke, dtot, nw_p)


S5_N = S5_GROUPS * S5_STATE


def _gelu_tanh(x):
    return 0.5 * x * (1.0 + jnp.tanh(np.sqrt(2.0 / np.pi) * (x + 0.044715 * (x * x * x))))


def _s5_kernel(u_ref, bb_ref, cb_ref, nr_ref, ni_ref, pr_ref, pi_ref, lr_ref, li_ref,
               d_ref, wg_ref, bg_ref, o_ref, sr_ref, si_ref):
    @pl.when(pl.program_id(1) == 0)
    def _():
        sr_ref[...] = jnp.zeros_like(sr_ref)
        si_ref[...] = jnp.zeros_like(si_ref)

    tri = _tri_mask().astype(BF16)
    n_chunks = u_ref.shape[0] // CHUNK

    def split_dot(x):
        hi = x.astype(BF16)
        lo = (x - hi.astype(F32)).astype(BF16)
        return (jnp.dot(tri, hi, preferred_element_type=F32)
                + jnp.dot(tri, lo, preferred_element_type=F32))

    def chunk(c, carry):
        r = pl.ds(pl.multiple_of(c * CHUNK, CHUNK), CHUNK)
        u = u_ref[r, :]
        x = jnp.dot(u, bb_ref[...], preferred_element_type=F32)
        xr, xi = x[:, :S5_N], x[:, S5_N:]
        nr, ni = nr_ref[...], ni_ref[...]
        p_r = split_dot(xr * nr - xi * ni)
        p_i = split_dot(xr * ni + xi * nr)
        s0r, s0i = sr_ref[...], si_ref[...]
        lr, li = lr_ref[...], li_ref[...]
        q_r = p_r + (s0r * lr - s0i * li)
        q_i = p_i + (s0r * li + s0i * lr)
        pr, pi = pr_ref[...], pi_ref[...]
        s_r = q_r * pr - q_i * pi
        s_i = q_r * pi + q_i * pr
        sr_ref[...] = s_r[CHUNK - 1:CHUNK, :]
        si_ref[...] = s_i[CHUNK - 1:CHUNK, :]
        y = (jnp.dot(s_r.astype(BF16), cb_ref[:S5_N, :], preferred_element_type=F32)
             + jnp.dot(s_i.astype(BF16), cb_ref[S5_N:, :], preferred_element_type=F32))
        y = _gelu_tanh(y + d_ref[...] * u.astype(F32))
        gate = jnp.dot(y.astype(BF16), wg_ref[...], preferred_element_type=F32) + bg_ref[...]
        o_ref[r, :] = (y * jax.nn.sigmoid(gate)).astype(BF16)
        return carry

    lax.fori_loop(0, n_chunks, chunk, 0)


def _s5_mixer(proj, batch, seq, tabs, tl=512):
    per_b = seq // tl
    bb, cb, nr, ni, pr, pi, lr, li, dsk, wg, bg = tabs
    return pl.pallas_call(
        _s5_kernel,
        grid=(batch, per_b),
        in_specs=[_proj_spec(tl, S5_WIDTH, OFF_SU, per_b),
                  _full((S5_WIDTH, 2 * S5_N)), _full((2 * S5_N, S5_WIDTH)),
                  _full((CHUNK, S5_N)), _full((CHUNK, S5_N)),
                  _full((CHUNK, S5_N)), _full((CHUNK, S5_N)),
                  _full((1, S5_N)), _full((1, S5_N)),
                  _full((1, S5_WIDTH)), _full((S5_WIDTH, S5_WIDTH)), _full((1, S5_WIDTH))],
        out_specs=pl.BlockSpec((tl, S5_WIDTH), lambda b, l: (b * per_b + l, 0)),
        out_shape=jax.ShapeDtypeStruct((batch * seq, S5_WIDTH), BF16),
        scratch_shapes=[pltpu.VMEM((1, S5_N), F32), pltpu.VMEM((1, S5_N), F32)],
        compiler_params=_cparams(("arbitrary", "arbitrary")),
        name="s5_mixer",
    )(proj, bb, cb, nr, ni, pr, pi, lr, li, dsk, wg, bg)


def _s5_tables(a_re, a_im, log_dt, b_re, b_im, c_re, c_im, d_skip, w_glu, b_glu):
    lam = lax.complex(a_re, a_im)
    dt = jnp.exp(log_dt)[:, None]
    lam_bar = jnp.exp(lam * dt)
    b_bar = ((lam_bar - 1.0) / lam)[..., None] * lax.complex(b_re, b_im)
    eye = jnp.eye(S5_GROUPS, dtype=F32)
    def blk_b(m):
        return jnp.einsum('gph,gk->ghkp', m, eye).reshape(S5_WIDTH, S5_N)
    bb = jnp.concatenate([blk_b(jnp.real(b_bar)), blk_b(jnp.imag(b_bar))], axis=1)
    def blk_c(m):
        return jnp.einsum('ghp,gk->kpgh', m, eye).reshape(S5_N, S5_WIDTH)
    cb = jnp.concatenate([blk_c(c_re), blk_c(-c_im)], axis=0)
    steps = jnp.arange(CHUNK, dtype=F32)[:, None, None]
    lam_dt = (lam * dt)[None]
    pos = jnp.exp(lam_dt * steps).reshape(CHUNK, S5_N)
    neg = jnp.exp(-lam_dt * steps).reshape(CHUNK, S5_N)
    one = lam_bar.reshape(1, S5_N)
    return (bb.astype(BF16), cb.astype(BF16), jnp.real(neg), jnp.imag(neg), jnp.real(pos),
            jnp.imag(pos), jnp.real(one), jnp.imag(one), d_skip.reshape(1, S5_WIDTH),
            w_glu.astype(BF16), b_glu.reshape(1, S5_WIDTH))


def _router_kernel(og_ref, or_ref, os_ref, wg_ref, wr_ref, ws_ref, x_ref, g1_ref, sc_ref,
                   sh_ref, nw_ref, rw_ref, rb_ref,
                   x1_ref, h_ref, idx_ref, gate_ref, rank_ref, cnt_ref, carry_ref):
    i = pl.program_id(0)

    @pl.when(i == 0)
    def _():
        carry_ref[...] = jnp.zeros_like(carry_ref)

    mix = (jnp.dot(og_ref[...], wg_ref[...], preferred_element_type=F32)
           + jnp.dot(or_ref[...], wr_ref[...], preferred_element_type=F32)
           + jnp.dot(os_ref[...], ws_ref[...], preferred_element_type=F32))
    x1 = x_ref[...] + g1_ref[0] * mix
    x1_ref[...] = x1
    hdn = _rms_mod(x1, nw_ref[...], sc_ref[0], sh_ref[0])
    h_ref[...] = hdn
    logits = jnp.dot(hdn, rw_ref[...], preferred_element_type=F32,
                     precision=lax.Precision.HIGHEST) + rb_ref[...]
    tm = logits.shape[0]
    lane = lax.broadcasted_iota(jnp.int32, (tm, LANES), 1)
    work = logits
    onehot = jnp.zeros((tm, LANES), F32)
    vals, idxs = [], []
    for _ in range(TOP_K):
        m = jnp.max(work, axis=-1, keepdims=True)
        ix = jnp.min(jnp.where(work == m, lane, LANES), axis=-1, keepdims=True)
        sel = lane == ix
        work = jnp.where(sel, -jnp.inf, work)
        onehot = onehot + sel.astype(F32)
        vals.append(m)
        idxs.append(ix)
    exps = [jnp.exp(v - vals[0]) for v in vals]
    denom = exps[0] + exps[1] + exps[2] + exps[3]
    r = lax.broadcasted_iota(jnp.int32, (tm, tm), 0)
    c = lax.broadcasted_iota(jnp.int32, (tm, tm), 1)
    strict = (r > c).astype(BF16)
    before = jnp.dot(strict, onehot.astype(BF16), preferred_element_type=F32) + carry_ref[...]
    idx_out = jnp.zeros((tm, LANES), jnp.int32)
    gate_out = jnp.zeros((tm, LANES), F32)
    rank_out = jnp.zeros((tm, LANES), F32)
    for k in range(TOP_K):
        rk = jnp.sum(jnp.where(lane == idxs[k], before, 0.0), axis=-1, keepdims=True)
        idx_out = jnp.where(lane == k, idxs[k], idx_out)
        gate_out = jnp.where(lane == k, exps[k] / denom, gate_out)
        rank_out = jnp.where(lane == k, rk, rank_out)
    idx_ref[...] = idx_out
    gate_ref[...] = gate_out
    rank_ref[...] = rank_out.astype(jnp.int32)
    total = carry_ref[...] + jnp.sum(onehot, axis=0, keepdims=True)
    carry_ref[...] = total
    cnt_ref[...] = total


def _outproj_router(o_gla, o_ret, o_s5, wg, wr, ws, x2d, g1, sc2, sh2, nw2, rw_p, rb_p, seq,
                    tm=256):
    t, d = x2d.shape
    per_b = seq // tm
    row = lambda w: pl.BlockSpec((tm, w), lambda i: (i, 0))
    full = lambda s: pl.BlockSpec(s, lambda i: (0,) * len(s))
    per_batch = pl.BlockSpec((1, 1, d), lambda i: (i // per_b, 0, 0))
    return pl.pallas_call(
        _router_kernel,
        grid=(t // tm,),
        in_specs=[row(HP), row(HP), row(S5_WIDTH), full((HP, d)), full((HP, d)),
                  full((S5_WIDTH, d)), row(d), per_batch, per_batch, per_batch,
                  full((1, d)), full((d, LANES)), full((1, LANES))],
        out_specs=[row(d), row(d), row(LANES), row(LANES), row(LANES), full((1, LANES))],
        out_shape=[jax.ShapeDtypeStruct((t, d), F32), jax.ShapeDtypeStruct((t, d), F32),
                   jax.ShapeDtypeStruct((t, LANES), jnp.int32),
                   jax.ShapeDtypeStruct((t, LANES), F32),
                   jax.ShapeDtypeStruct((t, LANES), jnp.int32),
                   jax.ShapeDtypeStruct((1, LANES), F32)],
        scratch_shapes=[pltpu.VMEM((1, LANES), F32)],
        compiler_params=_cparams(("arbitrary",)),
        name="outproj_router",
    )(o_gla, o_ret, o_s5, wg, wr, ws, x2d, g1, sc2, sh2, nw2, rw_p, rb_p)


GATHER_WIN = 32


def _gather_rows(table, idx):
    m = idx.shape[0]
    w = table.shape[1]
    mesh = plsc.VectorSubcoreMesh(core_axis_name="core", subcore_axis_name="subcore")

    @functools.partial(pl.kernel, out_type=jax.ShapeDtypeStruct((m, w), table.dtype),
                       mesh=mesh, name="sc_row_gather")
    def gather(x_hbm, i_hbm, o_hbm):
        def body(i_vmem, o_vmem):
            pltpu.sync_copy(x_hbm.at[i_vmem], o_vmem)

        pltpu.emit_pipeline(
            body,
            grid=(m // GATHER_WIN,),
            in_specs=[pl.BlockSpec((GATHER_WIN,), lambda i: (i,))],
            out_specs=[pl.BlockSpec((GATHER_WIN, w), lambda i: (i, 0))],
            core_axis_name=("core", "subcore"),
            dimension_semantics=(pltpu.PARALLEL,),
        )(i_hbm, o_hbm)

    return gather(table, idx)


def _expert_kernel(be_ref, nu_ref, rows_ref, wu_ref, bu_ref, wd_ref, bd_ref, o_ref,
                   wu_bf, wd_bf):
    i = pl.program_id(0)
    e = be_ref[i]
    prev = be_ref[jnp.maximum(i - 1, 0)]

    @pl.when((i == 0) | (e != prev))
    def _():
        wu_bf[...] = wu_ref[0, 0].astype(BF16)
        wd_bf[...] = wd_ref[0, 0].astype(BF16)

    @pl.when(i < nu_ref[0])
    def _():
        x = rows_ref[...].astype(BF16)
        up = jnp.dot(x, wu_bf[...], preferred_element_type=F32) + bu_ref[0, 0]
        x_glu = jnp.minimum(up[:, :D_FF], SWIGLU_LIMIT)
        x_lin = jnp.clip(up[:, D_FF:], -SWIGLU_LIMIT, SWIGLU_LIMIT)
        act = x_glu * jax.nn.sigmoid(SWIGLU_ALPHA * x_glu) * (x_lin + 1.0)
        o_ref[...] = (jnp.dot(act.astype(BF16), wd_bf[...], preferred_element_type=F32)
                      + bd_ref[0, 0])

    @pl.when(i >= nu_ref[0])
    def _():
        o_ref[...] = jnp.zeros_like(o_ref)


def _experts(layer, block_e, n_used, rows, w_up, b_up, w_down, b_down):
    n_rows, d = rows.shape
    n_blocks = n_rows // ROW_BLK
    depth, ne, _, f2 = w_up.shape
    wsel = lambda i, be, nu: (layer, be[i], 0, 0)
    grid_spec = pltpu.PrefetchScalarGridSpec(
        num_scalar_prefetch=2,
        grid=(n_blocks,),
        in_specs=[pl.BlockSpec((ROW_BLK, d), lambda i, be, nu: (i, 0)),
                  pl.BlockSpec((1, 1, d, f2), wsel),
                  pl.BlockSpec((1, 1, 1, f2), wsel),
                  pl.BlockSpec((1, 1, D_FF, d), wsel),
                  pl.BlockSpec((1, 1, 1, d), wsel)],
        out_specs=pl.BlockSpec((ROW_BLK, d), lambda i, be, nu: (i, 0)),
        scratch_shapes=[pltpu.VMEM((d, f2), BF16), pltpu.VMEM((D_FF, d), BF16)],
    )
    return pl.pallas_call(
        _expert_kernel,
        grid_spec=grid_spec,
        out_shape=jax.ShapeDtypeStruct((n_rows, d), F32),
        compiler_params=_cparams(("arbitrary",)),
        name="moe_experts",
    )(block_e, n_used, rows, w_up, b_up.reshape(depth, ne, 1, f2), w_down,
      b_down.reshape(depth, ne, 1, d))


def _combine_kernel(y0_ref, y1_ref, y2_ref, y3_ref, gate_ref, x1_ref, g2_ref, fw_ref, o_ref,
                    *, final):
    gates = gate_ref[...]
    y = gates[:, 0:1] * y0_ref[...]
    for k, y_ref in ((1, y1_ref), (2, y2_ref), (3, y3_ref)):
        y = y + gates[:, k:k + 1] * y_ref[...]
    x2 = x1_ref[...] + g2_ref[0] * y
    if final:
        x2 = (x2 * lax.rsqrt(jnp.mean(x2 * x2, axis=-1, keepdims=True) + NORM_EPS)) * fw_ref[...]
    o_ref[...] = x2


def _combine(y4, gates, x1, g2, fw, seq, final, th=256):
    t, d = x1.shape
    steps = t // th
    per_b = seq // th
    slot = lambda k: pl.BlockSpec((th, d), lambda i: (k * steps + i, 0))
    return pl.pallas_call(
        functools.partial(_combine_kernel, final=final),
        grid=(steps,),
        in_specs=[slot(0), slot(1), slot(2), slot(3),
                  pl.BlockSpec((th, LANES), lambda i: (i, 0)),
                  pl.BlockSpec((th, d), lambda i: (i, 0)),
                  pl.BlockSpec((1, 1, d), lambda i: (i // per_b, 0, 0)),
                  pl.BlockSpec((1, d), lambda i: (0, 0))],
        out_specs=pl.BlockSpec((th, d), lambda i: (i, 0)),
        out_shape=jax.ShapeDtypeStruct((t, d), F32),
        compiler_params=_cparams(("arbitrary",)),
        name="moe_combine",
    )(y4, y4, y4, y4, gates, x1, g2, fw)


def _retention_tables(seq):
    pos = jnp.arange(seq, dtype=F32)
    inv_freq = ROPE_BASE ** (-jnp.arange(0, HEAD_DK, 2, dtype=F32) / HEAD_DK)
    ang = pos[:, None] * inv_freq[None, :]
    cos, sin = jnp.cos(ang), jnp.sin(ang)
    half = HEAD_DK // 2
    zpad = jnp.zeros((seq, 64 - half), F32)
    cos_t = jnp.concatenate([cos, zpad, cos, zpad], axis=1)
    sin_t = jnp.concatenate([-sin, zpad, sin, zpad], axis=1)
    log_gamma = jnp.log1p(-jnp.exp2(-5.0 - jnp.arange(N_HEADS, dtype=F32)))
    log_decay = jnp.broadcast_to(log_gamma[None, :, None], (CHUNK, N_HEADS, HEAD_PAD))
    cum = jnp.cumsum(log_decay, axis=0)
    tot = cum[-1:]
    shp = lambda a: a.reshape(a.shape[0], HP)
    return (cos_t, sin_t, shp(jnp.exp(cum)), shp(jnp.exp(-cum)), shp(jnp.exp(tot - cum)),
            shp(jnp.exp(tot)))


def kernel(x, c, norm1_w, norm2_w, w_mod, b_mod, w_in, gla_w_a2, gla_b_a, gla_norm_w, ret_norm_w, s5_a_re, s5_a_im, s5_log_dt, s5_b_re, s5_b_im, s5_c_re, s5_c_im, s5_d, s5_w_glu, s5_b_glu, w_out, router_w, router_b, w_up, b_up, w_down, b_down, final_norm_w):
    batch, seq, d = x.shape
    depth = w_mod.shape[0]
    t = batch * seq
    n_slots = t * TOP_K
    n_blocks = n_slots // ROW_BLK + N_EXPERTS
    n_rows = n_blocks * ROW_BLK

    mod = _modulation(c, w_mod, b_mod)
    ret_tabs = _retention_tables(seq)
    x2d = x.reshape(t, d)

    for i in range(depth):
        m6 = mod[i].reshape(batch, 6, 1, d)
        sh1, sc1, g1, sh2, sc2, g2 = (m6[:, j] for j in range(6))

        w_p = _take_cols(w_in[i], _IN_SRC).astype(BF16)
        wa_p = jnp.zeros((LANES, HP), F32).at[:GATE_RANK].set(_take_cols(gla_w_a2[i], _DK_SRC))
        ba_p = _take_cols(gla_b_a[i], _DK_SRC).reshape(1, HP)
        gnw = _take_cols(gla_norm_w[i], _DV_SRC).reshape(1, HP)
        rnw = _take_cols(ret_norm_w[i], _DV_SRC).reshape(1, HP)
        kv = N_HEADS * HEAD_DV
        wo_g = _take_rows(w_out[i, :kv], _DV_SRC).astype(BF16)
        wo_r = _take_rows(w_out[i, kv:2 * kv], _DV_SRC).astype(BF16)
        wo_s = w_out[i, 2 * kv:].astype(BF16)
        rw_p = jnp.zeros((d, LANES), F32).at[:, :N_EXPERTS].set(router_w[i])
        rb_p = jnp.full((1, LANES), -1e30, F32).at[0, :N_EXPERTS].set(router_b[i])

        proj = _in_projection(x2d, sc1, sh1, norm1_w[i].reshape(1, d), w_p, seq)
        o_gla = _gla_mixer(proj, batch, seq, wa_p.astype(BF16), ba_p, gnw)
        o_ret = _ret_mixer(proj, batch, seq, *ret_tabs, rnw)
        s5_tabs = _s5_tables(s5_a_re[i], s5_a_im[i], s5_log_dt[i], s5_b_re[i], s5_b_im[i],
                             s5_c_re[i], s5_c_im[i], s5_d[i], s5_w_glu[i], s5_b_glu[i])
        o_s5 = _s5_mixer(proj, batch, seq, s5_tabs)

        x1, hdn, idx, gates, rank, counts = _outproj_router(
            o_gla, o_ret, o_s5, wo_g, wo_r, wo_s, x2d, g1, sc2, sh2,
            norm2_w[i].reshape(1, d), rw_p, rb_p, seq)

        cnt = counts[0, :N_EXPERTS].astype(jnp.int32)
        padded = (cnt + ROW_BLK - 1) // ROW_BLK * ROW_BLK
        pad_ends = jnp.cumsum(padded)
        pad_starts = pad_ends - padded
        dest = (pad_starts[idx[:, :TOP_K]] + rank[:, :TOP_K]).astype(jnp.int32)
        blk_start = jnp.arange(n_blocks, dtype=jnp.int32) * ROW_BLK
        block_e = jnp.minimum(jnp.sum(pad_ends[None, :] <= blk_start[:, None], axis=1),
                              N_EXPERTS - 1).astype(jnp.int32)
        n_used = (pad_ends[-1:] // ROW_BLK).astype(jnp.int32)
        slot_token = jnp.arange(n_slots, dtype=jnp.int32) // TOP_K
        src = jnp.zeros((n_rows,), jnp.int32).at[dest.reshape(-1)].set(slot_token)

        rows = _gather_rows(hdn, src)
        out_rows = _experts(i, block_e, n_used, rows, w_up, b_up, w_down, b_down)
        y4 = _gather_rows(out_rows, dest.T.reshape(-1))
        x2d = _combine(y4, gates, x1, g2, final_norm_w.reshape(1, d), seq,
                       final=(i == depth - 1))

    return x2d.reshape(batch, seq, d)
```

```python
import functools

import numpy as np
import jax
import jax.numpy as jnp
from jax import lax
from jax.experimental import pallas as pl
from jax.experimental.pallas import tpu as pltpu
from jax.experimental.pallas import tpu_sc as plsc

D_MODEL = 1024
CHUNK = 64
NORM_EPS = 1e-5
N_HEADS = 4
HEAD_DK = 48
HEAD_DV = 96
GATE_RANK = 16
GATE_TEMP = 16.0
ROPE_BASE = 10000.0
S5_WIDTH = 256
S5_GROUP_DIM = 16
S5_GROUPS = 16
S5_STATE = 64
N_EXPERTS = 32
TOP_K = 4
D_FF = 1024
SWIGLU_LIMIT = 7.0
SWIGLU_ALPHA = 1.702

LANES = 128
HEAD_PAD = LANES
HP = N_HEADS * HEAD_PAD
VMEM_LIMIT = 56 * 1024 * 1024

OFF_GQ, OFF_GK, OFF_GV, OFF_GG = 0, HP, 2 * HP, 3 * HP
OFF_RQ, OFF_RK, OFF_RV, OFF_RG = 4 * HP, 5 * HP, 6 * HP, 7 * HP
OFF_SU = 8 * HP
OFF_GA = OFF_SU + S5_WIDTH
NP_COLS = OFF_GA + LANES
PROJ_CH = 896

ROW_BLK = 256

F32 = jnp.float32
BF16 = jnp.bfloat16


def _in_col_map():
    src = -np.ones((NP_COLS,), np.int64)
    kq = N_HEADS * HEAD_DK
    kv = N_HEADS * HEAD_DV
    base = dict(gq=0, gk=kq, gv=2 * kq, gg=2 * kq + kv, ga=2 * kq + 2 * kv)
    r0 = base['ga'] + GATE_RANK
    base.update(rq=r0, rk=r0 + kq, rv=r0 + 2 * kq, rg=r0 + 2 * kq + kv, su=r0 + 2 * kq + 2 * kv)
    half = HEAD_DK // 2
    for h in range(N_HEADS):
        for d in range(HEAD_DK):
            src[OFF_GQ + h * HEAD_PAD + d] = base['gq'] + h * HEAD_DK + d
            src[OFF_GK + h * HEAD_PAD + d] = base['gk'] + h * HEAD_DK + d
            lane = d if d < half else 64 + (d - half)
            src[OFF_RQ + h * HEAD_PAD + lane] = base['rq'] + h * HEAD_DK + d
            src[OFF_RK + h * HEAD_PAD + lane] = base['rk'] + h * HEAD_DK + d
        for d in range(HEAD_DV):
            src[OFF_GV + h * HEAD_PAD + d] = base['gv'] + h * HEAD_DV + d
            src[OFF_GG + h * HEAD_PAD + d] = base['gg'] + h * HEAD_DV + d
            src[OFF_RV + h * HEAD_PAD + d] = base['rv'] + h * HEAD_DV + d
            src[OFF_RG + h * HEAD_PAD + d] = base['rg'] + h * HEAD_DV + d
    src[OFF_SU:OFF_SU + S5_WIDTH] = base['su'] + np.arange(S5_WIDTH)
    src[OFF_GA:OFF_GA + GATE_RANK] = base['ga'] + np.arange(GATE_RANK)
    return src


_IN_SRC = _in_col_map()


def _head_pad_map(width):
    src = -np.ones((HP,), np.int64)
    for h in range(N_HEADS):
        src[h * HEAD_PAD:h * HEAD_PAD + width] = h * width + np.arange(width)
    return src


_DV_SRC = _head_pad_map(HEAD_DV)
_DK_SRC = _head_pad_map(HEAD_DK)


def _take_cols(w, src):
    out = jnp.take(w, jnp.asarray(np.maximum(src, 0)), axis=-1)
    return jnp.where(jnp.asarray(src >= 0), out, 0)


def _take_rows(w, src):
    out = jnp.take(w, jnp.asarray(np.maximum(src, 0)), axis=0)
    return jnp.where(jnp.asarray(src >= 0)[:, None], out, 0)


def _cparams(sem):
    return pltpu.CompilerParams(dimension_semantics=sem, vmem_limit_bytes=VMEM_LIMIT)


def _mod_kernel(c_ref, w_ref, b_ref, o_ref):
    c = c_ref[...]
    cond = c * jax.nn.sigmoid(c)
    o_ref[0] = jnp.dot(cond, w_ref[0], preferred_element_type=F32,
                       precision=lax.Precision.HIGHEST) + b_ref[0]


def _modulation(c, w_mod, b_mod):
    depth, d, n = w_mod.shape
    b = c.shape[0]
    nb = 1536
    return pl.pallas_call(
        _mod_kernel,
        grid=(depth, n // nb),
        in_specs=[pl.BlockSpec((b, d), lambda l, j: (0, 0)),
                  pl.BlockSpec((1, d, nb), lambda l, j: (l, 0, j)),
                  pl.BlockSpec((1, 1, nb), lambda l, j: (l, 0, j))],
        out_specs=pl.BlockSpec((1, b, nb), lambda l, j: (l, 0, j)),
        out_shape=jax.ShapeDtypeStruct((depth, b, n), F32),
        compiler_params=_cparams(("arbitrary", "arbitrary")),
        name="adaln_mod",
    )(c, w_mod, b_mod.reshape(depth, 1, n))


def _rms_mod(x, nw, sc, sh):
    y = x * lax.rsqrt(jnp.mean(x * x, axis=-1, keepdims=True) + NORM_EPS)
    return (y * nw) * (1.0 + sc) + sh


def _inproj_kernel(x_ref, sc_ref, sh_ref, nw_ref, w_ref, o_ref):
    h = _rms_mod(x_ref[...], nw_ref[...], sc_ref[0], sh_ref[0]).astype(BF16)
    for j in range(NP_COLS // PROJ_CH):
        cs = slice(j * PROJ_CH, (j + 1) * PROJ_CH)
        o_ref[:, cs] = jnp.dot(h, w_ref[:, cs], preferred_element_type=F32).astype(BF16)


def _in_projection(x2d, sc, sh, nw, w_p, seq, tm=512):
    t, d = x2d.shape
    per_b = seq // tm
    return pl.pallas_call(
        _inproj_kernel,
        grid=(t // tm,),
        in_specs=[pl.BlockSpec((tm, d), lambda i: (i, 0)),
                  pl.BlockSpec((1, 1, d), lambda i: (i // per_b, 0, 0)),
                  pl.BlockSpec((1, 1, d), lambda i: (i // per_b, 0, 0)),
                  pl.BlockSpec((1, d), lambda i: (0, 0)),
                  pl.BlockSpec((d, NP_COLS), lambda i: (0, 0))],
        out_specs=pl.BlockSpec((tm, NP_COLS), lambda i: (i, 0)),
        out_shape=jax.ShapeDtypeStruct((t, NP_COLS), BF16),
        compiler_params=_cparams(("arbitrary",)),
        name="in_proj",
    )(x2d, sc, sh, nw, w_p)


_NT = (((1,), (1,)), ((), ()))
_TN = (((0,), (0,)), ((), ()))


def _tri_mask():
    r = lax.broadcasted_iota(jnp.int32, (CHUNK, CHUNK), 0)
    c = lax.broadcasted_iota(jnp.int32, (CHUNK, CHUNK), 1)
    return r >= c


def _head_attention(qd, ki, ke, vh, et, st_ref, h, causal):
    qb = qd.astype(BF16)
    sc = lax.dot_general(qb, ki.astype(BF16), _NT, preferred_element_type=F32)
    sc = jnp.where(causal, sc, 0.0)
    st = st_ref[h]
    o = jnp.dot(sc.astype(BF16), vh, preferred_element_type=F32)
    o = o + lax.dot_general(qb, st.astype(BF16), _NT, preferred_element_type=F32)
    st_ref[h] = st * et + lax.dot_general(vh, ke.astype(BF16), _TN, preferred_element_type=F32)
    return o


def _gla_kernel(q_ref, k_ref, v_ref, g_ref, a_ref, wa_ref, ba_ref, nw_ref, o_ref, st_ref):
    @pl.when(pl.program_id(1) == 0)
    def _():
        st_ref[...] = jnp.zeros_like(st_ref)

    causal = _tri_mask()
    tri = causal.astype(BF16)
    n_chunks = q_ref.shape[0] // CHUNK

    def chunk(c, carry):
        r = pl.ds(pl.multiple_of(c * CHUNK, CHUNK), CHUNK)
        z = jnp.dot(a_ref[r, :], wa_ref[...], preferred_element_type=F32) + ba_ref[...]
        la = (jnp.minimum(z, 0.0) - jnp.log1p(jnp.exp(-jnp.abs(z)))) * (1.0 / GATE_TEMP)
        hi = la.astype(BF16)
        lo = (la - hi.astype(F32)).astype(BF16)
        cum = (jnp.dot(tri, hi, preferred_element_type=F32)
               + jnp.dot(tri, lo, preferred_element_type=F32))
        tot = cum[CHUNK - 1:CHUNK, :]
        qd = (q_ref[r, :].astype(F32) * (HEAD_DK ** -0.5)) * jnp.exp(cum)
        kf = k_ref[r, :].astype(F32)
        ki = kf * jnp.exp(-cum)
        ke = kf * jnp.exp(tot - cum)
        et = jnp.exp(tot)
        for h in range(N_HEADS):
            sl = slice(h * HEAD_PAD, (h + 1) * HEAD_PAD)
            o = _head_attention(qd[:, sl], ki[:, sl], ke[:, sl], v_ref[r, sl], et[:, sl],
                                st_ref, h, causal)
            ms = jnp.sum(o * o, axis=-1, keepdims=True) * (1.0 / HEAD_DV)
            y = (o * lax.rsqrt(ms + NORM_EPS)) * nw_ref[:, sl]
            g = g_ref[r, sl].astype(F32)
            o_ref[r, sl] = (y * (g * jax.nn.sigmoid(g))).astype(BF16)
        return carry

    lax.fori_loop(0, n_chunks, chunk, 0, unroll=4)


def _ret_kernel(q_ref, k_ref, v_ref, g_ref, cos_ref, sin_ref, dq_ref, dki_ref, dke_ref,
                dt_ref, nw_ref, o_ref, st_ref):
    @pl.when(pl.program_id(1) == 0)
    def _():
        st_ref[...] = jnp.zeros_like(st_ref)

    causal = _tri_mask()
    n_chunks = q_ref.shape[0] // CHUNK
    lane = lax.broadcasted_iota(jnp.int32, (CHUNK, HEAD_PAD), 1)
    real = lane < HEAD_DV

    def chunk(c, carry):
        r = pl.ds(pl.multiple_of(c * CHUNK, CHUNK), CHUNK)
        cos = cos_ref[r, :]
        sin = sin_ref[r, :]
        for h in range(N_HEADS):
            sl = slice(h * HEAD_PAD, (h + 1) * HEAD_PAD)
            qh = q_ref[r, sl].astype(F32)
            kh = k_ref[r, sl].astype(F32)
            qr = qh * cos + pltpu.roll(qh, 64, 1) * sin
            kr = (kh * cos + pltpu.roll(kh, 64, 1) * sin) * (HEAD_DK ** -0.5)
            o = _head_attention(qr * dq_ref[:, sl], kr * dki_ref[:, sl], kr * dke_ref[:, sl],
                                v_ref[r, sl], dt_ref[:, sl], st_ref, h, causal)
            mu = jnp.sum(o, axis=-1, keepdims=True) * (1.0 / HEAD_DV)
            oc = jnp.where(real, o - mu, 0.0)
            var = jnp.sum(oc * oc, axis=-1, keepdims=True) * (1.0 / HEAD_DV)
            y = (oc * lax.rsqrt(var + NORM_EPS)) * nw_ref[:, sl]
            g = g_ref[r, sl].astype(F32)
            o_ref[r, sl] = (y * (g * jax.nn.sigmoid(g))).astype(BF16)
        return carry

    lax.fori_loop(0, n_chunks, chunk, 0, unroll=2)


def _proj_spec(tl, width, col_off, per_b):
    cb = col_off // width
    return pl.BlockSpec((tl, width), lambda b, l: (b * per_b + l, cb))


def _full(shape):
    return pl.BlockSpec(shape, lambda b, l: (0,) * len(shape))


def _gla_mixer(proj, batch, seq, wa_p, ba_p, nw_p, tl=512):
    per_b = seq // tl
    return pl.pallas_call(
        _gla_kernel,
        grid=(batch, per_b),
        in_specs=[_proj_spec(tl, HP, OFF_GQ, per_b), _proj_spec(tl, HP, OFF_GK, per_b),
                  _proj_spec(tl, HP, OFF_GV, per_b), _proj_spec(tl, HP, OFF_GG, per_b),
                  _proj_spec(tl, LANES, OFF_GA, per_b),
                  _full((LANES, HP)), _full((1, HP)), _full((1, HP))],
        out_specs=pl.BlockSpec((tl, HP), lambda b, l: (b * per_b + l, 0)),
        out_shape=jax.ShapeDtypeStruct((batch * seq, HP), BF16),
        scratch_shapes=[pltpu.VMEM((N_HEADS, HEAD_PAD, HEAD_PAD), F32)],
        compiler_params=_cparams(("arbitrary", "arbitrary")),
        name="gla_mixer",
    )(proj, proj, proj, proj, proj, wa_p, ba_p, nw_p)


def _ret_mixer(proj, batch, seq, cos_t, sin_t, dq, dki, dke, dtot, nw_p, tl=512):
    per_b = seq // tl
    return pl.pallas_call(
        _ret_kernel,
        grid=(batch, per_b),
        in_specs=[_proj_spec(tl, HP, OFF_RQ, per_b), _proj_spec(tl, HP, OFF_RK, per_b),
                  _proj_spec(tl, HP, OFF_RV, per_b), _proj_spec(tl, HP, OFF_RG, per_b),
                  pl.BlockSpec((tl, HEAD_PAD), lambda b, l: (l, 0)),
                  pl.BlockSpec((tl, HEAD_PAD), lambda b, l: (l, 0)),
                  _full((CHUNK, HP)), _full((CHUNK, HP)), _full((CHUNK, HP)),
                  _full((1, HP)), _full((1, HP))],
        out_specs=pl.BlockSpec((tl, HP), lambda b, l: (b * per_b + l, 0)),
        out_shape=jax.ShapeDtypeStruct((batch * seq, HP), BF16),
        scratch_shapes=[pltpu.VMEM((N_HEADS, HEAD_PAD, HEAD_PAD), F32)],
        compiler_params=_cparams(("arbitrary", "arbitrary")),
        name="ret_mixer",
    )(proj, proj, proj, proj, cos_t, sin_t, dq, dki, dke, dtot, nw_p)


S5_N = S5_GROUPS * S5_STATE


def _gelu_tanh(x):
    return 0.5 * x * (1.0 + jnp.tanh(np.sqrt(2.0 / np.pi) * (x + 0.044715 * (x * x * x))))


def _s5_kernel(u_ref, bb_ref, cb_ref, nr_ref, ni_ref, pr_ref, pi_ref, lr_ref, li_ref,
               d_ref, wg_ref, bg_ref, o_ref, sr_ref, si_ref):
    @pl.when(pl.program_id(1) == 0)
    def _():
        sr_ref[...] = jnp.zeros_like(sr_ref)
        si_ref[...] = jnp.zeros_like(si_ref)

    tri = _tri_mask().astype(BF16)
    n_chunks = u_ref.shape[0] // CHUNK

    def split_dot(x):
        hi = x.astype(BF16)
        lo = (x - hi.astype(F32)).astype(BF16)
        return (jnp.dot(tri, hi, preferred_element_type=F32)
                + jnp.dot(tri, lo, preferred_element_type=F32))

    def chunk(c, carry):
        r = pl.ds(pl.multiple_of(c * CHUNK, CHUNK), CHUNK)
        u = u_ref[r, :]
        x = jnp.dot(u, bb_ref[...], preferred_element_type=F32)
        xr, xi = x[:, :S5_N], x[:, S5_N:]
        nr, ni = nr_ref[...], ni_ref[...]
        p_r = split_dot(xr * nr - xi * ni)
        p_i = split_dot(xr * ni + xi * nr)
        s0r, s0i = sr_ref[...], si_ref[...]
        lr, li = lr_ref[...], li_ref[...]
        q_r = p_r + (s0r * lr - s0i * li)
        q_i = p_i + (s0r * li + s0i * lr)
        pr, pi = pr_ref[...], pi_ref[...]
        s_r = q_r * pr - q_i * pi
        s_i = q_r * pi + q_i * pr
        sr_ref[...] = s_r[CHUNK - 1:CHUNK, :]
        si_ref[...] = s_i[CHUNK - 1:CHUNK, :]
        y = (jnp.dot(s_r.astype(BF16), cb_ref[:S5_N, :], preferred_element_type=F32)
             + jnp.dot(s_i.astype(BF16), cb_ref[S5_N:, :], preferred_element_type=F32))
        y = _gelu_tanh(y + d_ref[...] * u.astype(F32))
        gate = jnp.dot(y.astype(BF16), wg_ref[...], preferred_element_type=F32) + bg_ref[...]
        o_ref[r, :] = (y * jax.nn.sigmoid(gate)).astype(BF16)
        return carry

    lax.fori_loop(0, n_chunks, chunk, 0, unroll=2)


def _s5_mixer(proj, batch, seq, tabs, tl=512):
    per_b = seq // tl
    bb, cb, nr, ni, pr, pi, lr, li, dsk, wg, bg = tabs
    return pl.pallas_call(
        _s5_kernel,
        grid=(batch, per_b),
        in_specs=[_proj_spec(tl, S5_WIDTH, OFF_SU, per_b),
                  _full((S5_WIDTH, 2 * S5_N)), _full((2 * S5_N, S5_WIDTH)),
                  _full((CHUNK, S5_N)), _full((CHUNK, S5_N)),
                  _full((CHUNK, S5_N)), _full((CHUNK, S5_N)),
                  _full((1, S5_N)), _full((1, S5_N)),
                  _full((1, S5_WIDTH)), _full((S5_WIDTH, S5_WIDTH)), _full((1, S5_WIDTH))],
        out_specs=pl.BlockSpec((tl, S5_WIDTH), lambda b, l: (b * per_b + l, 0)),
        out_shape=jax.ShapeDtypeStruct((batch * seq, S5_WIDTH), BF16),
        scratch_shapes=[pltpu.VMEM((1, S5_N), F32), pltpu.VMEM((1, S5_N), F32)],
        compiler_params=_cparams(("arbitrary", "arbitrary")),
        name="s5_mixer",
    )(proj, bb, cb, nr, ni, pr, pi, lr, li, dsk, wg, bg)


def _s5_tables(a_re, a_im, log_dt, b_re, b_im, c_re, c_im, d_skip, w_glu, b_glu):
    lam = lax.complex(a_re, a_im)
    dt = jnp.exp(log_dt)[:, None]
    lam_bar = jnp.exp(lam * dt)
    b_bar = ((lam_bar - 1.0) / lam)[..., None] * lax.complex(b_re, b_im)
    eye = jnp.eye(S5_GROUPS, dtype=F32)
    def blk_b(m):
        return jnp.einsum('gph,gk->ghkp', m, eye).reshape(S5_WIDTH, S5_N)
    bb = jnp.concatenate([blk_b(jnp.real(b_bar)), blk_b(jnp.imag(b_bar))], axis=1)
    def blk_c(m):
        return jnp.einsum('ghp,gk->kpgh', m, eye).reshape(S5_N, S5_WIDTH)
    cb = jnp.concatenate([blk_c(c_re), blk_c(-c_im)], axis=0)
    steps = jnp.arange(CHUNK, dtype=F32)[:, None, None]
    lam_dt = (lam * dt)[None]
    pos = jnp.exp(lam_dt * steps).reshape(CHUNK, S5_N)
    neg = jnp.exp(-lam_dt * steps).reshape(CHUNK, S5_N)
    one = lam_bar.reshape(1, S5_N)
    return (bb.astype(BF16), cb.astype(BF16), jnp.real(neg), jnp.imag(neg), jnp.real(pos),
            jnp.imag(pos), jnp.real(one), jnp.imag(one), d_skip.reshape(1, S5_WIDTH),
            w_glu.astype(BF16), b_glu.reshape(1, S5_WIDTH))


def _router_kernel(og_ref, or_ref, os_ref, wg_ref, wr_ref, ws_ref, x_ref, g1_ref, sc_ref,
                   sh_ref, nw_ref, rw_ref, rb_ref,
                   x1_ref, h_ref, idx_ref, gate_ref, rank_ref, cnt_ref, carry_ref):
    i = pl.program_id(0)

    @pl.when(i == 0)
    def _():
        carry_ref[...] = jnp.zeros_like(carry_ref)

    mix = (jnp.dot(og_ref[...], wg_ref[...], preferred_element_type=F32)
           + jnp.dot(or_ref[...], wr_ref[...], preferred_element_type=F32)
           + jnp.dot(os_ref[...], ws_ref[...], preferred_element_type=F32))
    x1 = x_ref[...] + g1_ref[0] * mix
    x1_ref[...] = x1
    hdn = _rms_mod(x1, nw_ref[...], sc_ref[0], sh_ref[0])
    h_ref[...] = hdn
    h_hi = hdn.astype(BF16)
    h_lo = (hdn - h_hi.astype(F32)).astype(BF16)
    p = jnp.dot(h_hi, rw_ref[...], preferred_element_type=F32)
    logits = (p[:, :LANES] + p[:, LANES:]
              + jnp.dot(h_lo, rw_ref[:, :LANES], preferred_element_type=F32)) + rb_ref[...]
    tm = logits.shape[0]
    lane = lax.broadcasted_iota(jnp.int32, (tm, LANES), 1)
    work = logits
    onehot = jnp.zeros((tm, LANES), F32)
    vals, idxs = [], []
    for _ in range(TOP_K):
        m = jnp.max(work, axis=-1, keepdims=True)
        ix = jnp.min(jnp.where(work == m, lane, LANES), axis=-1, keepdims=True)
        sel = lane == ix
        work = jnp.where(sel, -jnp.inf, work)
        onehot = onehot + sel.astype(F32)
        vals.append(m)
        idxs.append(ix)
    exps = [jnp.exp(v - vals[0]) for v in vals]
    denom = exps[0] + exps[1] + exps[2] + exps[3]
    r = lax.broadcasted_iota(jnp.int32, (tm, tm), 0)
    c = lax.broadcasted_iota(jnp.int32, (tm, tm), 1)
    strict = (r > c).astype(BF16)
    before = jnp.dot(strict, onehot.astype(BF16), preferred_element_type=F32) + carry_ref[...]
    idx_out = jnp.zeros((tm, LANES), jnp.int32)
    gate_out = jnp.zeros((tm, LANES), F32)
    rank_out = jnp.zeros((tm, LANES), F32)
    for k in range(TOP_K):
        rk = jnp.sum(jnp.where(lane == idxs[k], before, 0.0), axis=-1, keepdims=True)
        idx_out = jnp.where(lane == k, idxs[k], idx_out)
        gate_out = jnp.where(lane == k, exps[k] / denom, gate_out)
        rank_out = jnp.where(lane == k, rk, rank_out)
    idx_ref[...] = idx_out
    gate_ref[...] = gate_out
    rank_ref[...] = rank_out.astype(jnp.int32)
    total = carry_ref[...] + jnp.sum(onehot, axis=0, keepdims=True)
    carry_ref[...] = total
    cnt_ref[...] = total


def _outproj_router(o_gla, o_ret, o_s5, wg, wr, ws, x2d, g1, sc2, sh2, nw2, rw_p, rb_p, seq,
                    tm=512):
    t, d = x2d.shape
    per_b = seq // tm
    row = lambda w: pl.BlockSpec((tm, w), lambda i: (i, 0))
    full = lambda s: pl.BlockSpec(s, lambda i: (0,) * len(s))
    per_batch = pl.BlockSpec((1, 1, d), lambda i: (i // per_b, 0, 0))
    return pl.pallas_call(
        _router_kernel,
        grid=(t // tm,),
        in_specs=[row(HP), row(HP), row(S5_WIDTH), full((HP, d)), full((HP, d)),
                  full((S5_WIDTH, d)), row(d), per_batch, per_batch, per_batch,
                  full((1, d)), full((d, 2 * LANES)), full((1, LANES))],
        out_specs=[row(d), row(d), row(LANES), row(LANES), row(LANES), full((1, LANES))],
        out_shape=[jax.ShapeDtypeStruct((t, d), F32), jax.ShapeDtypeStruct((t, d), F32),
                   jax.ShapeDtypeStruct((t, LANES), jnp.int32),
                   jax.ShapeDtypeStruct((t, LANES), F32),
                   jax.ShapeDtypeStruct((t, LANES), jnp.int32),
                   jax.ShapeDtypeStruct((1, LANES), F32)],
        scratch_shapes=[pltpu.VMEM((1, LANES), F32)],
        compiler_params=_cparams(("arbitrary",)),
        name="outproj_router",
    )(o_gla, o_ret, o_s5, wg, wr, ws, x2d, g1, sc2, sh2, nw2, rw_p, rb_p)


GATHER_WIN = 32


def _gather_rows(table, idx):
    m = idx.shape[0]
    w = table.shape[1]
    mesh = plsc.VectorSubcoreMesh(core_axis_name="core", subcore_axis_name="subcore")

    @functools.partial(pl.kernel, out_type=jax.ShapeDtypeStruct((m, w), table.dtype),
                       mesh=mesh, name="sc_row_gather")
    def gather(x_hbm, i_hbm, o_hbm):
        def body(i_vmem, o_vmem):
            pltpu.sync_copy(x_hbm.at[i_vmem], o_vmem)

        pltpu.emit_pipeline(
            body,
            grid=(m // GATHER_WIN,),
            in_specs=[pl.BlockSpec((GATHER_WIN,), lambda i: (i,))],
            out_specs=[pl.BlockSpec((GATHER_WIN, w), lambda i: (i, 0))],
            core_axis_name=("core", "subcore"),
            dimension_semantics=(pltpu.PARALLEL,),
        )(i_hbm, o_hbm)

    return gather(table, idx)


def _scatter_rows(x, dest_slot_major, n_rows):
    t, w = x.shape
    steps = t // GATHER_WIN
    mesh = plsc.VectorSubcoreMesh(core_axis_name="core", subcore_axis_name="subcore")

    @functools.partial(pl.kernel, out_type=jax.ShapeDtypeStruct((n_rows, w), x.dtype),
                       mesh=mesh, name="sc_row_scatter")
    def scatter(x_hbm, i_hbm, o_hbm):
        def body(x_vmem, i0, i1, i2, i3):
            for i_vmem in (i0, i1, i2, i3):
                pltpu.sync_copy(x_vmem, o_hbm.at[i_vmem])

        slot = lambda k: pl.BlockSpec((GATHER_WIN,), lambda i: (k * steps + i,))
        pltpu.emit_pipeline(
            body,
            grid=(steps,),
            in_specs=[pl.BlockSpec((GATHER_WIN, w), lambda i: (i, 0)),
                      slot(0), slot(1), slot(2), slot(3)],
            out_specs=[],
            core_axis_name=("core", "subcore"),
            dimension_semantics=(pltpu.PARALLEL,),
        )(x_hbm, i_hbm, i_hbm, i_hbm, i_hbm)

    return scatter(x, dest_slot_major)


def _expert_kernel(be_ref, nv_ref, rows_ref, wu_ref, bu_ref, wd_ref, bd_ref, o_ref,
                   wu_bf, wd_bf):
    i = pl.program_id(0)
    e = be_ref[i]
    prev = be_ref[jnp.maximum(i - 1, 0)]

    @pl.when((i == 0) | (e != prev))
    def _():
        wu_bf[...] = wu_ref[0, 0].astype(BF16)
        wd_bf[...] = wd_ref[0, 0].astype(BF16)

    @pl.when(nv_ref[i] > 0)
    def _():
        row = lax.broadcasted_iota(jnp.int32, rows_ref.shape, 0)
        x = jnp.where(row < nv_ref[i], rows_ref[...], 0.0).astype(BF16)
        up = jnp.dot(x, wu_bf[...], preferred_element_type=F32) + bu_ref[0, 0]
        x_glu = jnp.minimum(up[:, :D_FF], SWIGLU_LIMIT)
        x_lin = jnp.clip(up[:, D_FF:], -SWIGLU_LIMIT, SWIGLU_LIMIT)
        act = x_glu * jax.nn.sigmoid(SWIGLU_ALPHA * x_glu) * (x_lin + 1.0)
        o_ref[...] = (jnp.dot(act.astype(BF16), wd_bf[...], preferred_element_type=F32)
                      + bd_ref[0, 0])

    @pl.when(nv_ref[i] <= 0)
    def _():
        o_ref[...] = jnp.zeros_like(o_ref)


def _experts(layer, block_e, n_valid, rows, w_up, b_up, w_down, b_down):
    n_rows, d = rows.shape
    n_blocks = n_rows // ROW_BLK
    depth, ne, _, f2 = w_up.shape
    wsel = lambda i, be, nu: (layer, be[i], 0, 0)
    grid_spec = pltpu.PrefetchScalarGridSpec(
        num_scalar_prefetch=2,
        grid=(n_blocks,),
        in_specs=[pl.BlockSpec((ROW_BLK, d), lambda i, be, nu: (i, 0)),
                  pl.BlockSpec((1, 1, d, f2), wsel),
                  pl.BlockSpec((1, 1, 1, f2), wsel),
                  pl.BlockSpec((1, 1, D_FF, d), wsel),
                  pl.BlockSpec((1, 1, 1, d), wsel)],
        out_specs=pl.BlockSpec((ROW_BLK, d), lambda i, be, nu: (i, 0)),
        scratch_shapes=[pltpu.VMEM((d, f2), BF16), pltpu.VMEM((D_FF, d), BF16)],
    )
    return pl.pallas_call(
        _expert_kernel,
        grid_spec=grid_spec,
        out_shape=jax.ShapeDtypeStruct((n_rows, d), F32),
        compiler_params=_cparams(("arbitrary",)),
        name="moe_experts",
    )(block_e, n_valid, rows, w_up, b_up.reshape(depth, ne, 1, f2), w_down,
      b_down.reshape(depth, ne, 1, d))


def _combine_kernel(y0_ref, y1_ref, y2_ref, y3_ref, gate_ref, x1_ref, g2_ref, fw_ref, o_ref,
                    *, final):
    gates = gate_ref[...]
    y = gates[:, 0:1] * y0_ref[...]
    for k, y_ref in ((1, y1_ref), (2, y2_ref), (3, y3_ref)):
        y = y + gates[:, k:k + 1] * y_ref[...]
    x2 = x1_ref[...] + g2_ref[0] * y
    if final:
        x2 = (x2 * lax.rsqrt(jnp.mean(x2 * x2, axis=-1, keepdims=True) + NORM_EPS)) * fw_ref[...]
    o_ref[...] = x2


def _combine(y4, gates, x1, g2, fw, seq, final, th=256):
    t, d = x1.shape
    steps = t // th
    per_b = seq // th
    slot = lambda k: pl.BlockSpec((th, d), lambda i: (k * steps + i, 0))
    return pl.pallas_call(
        functools.partial(_combine_kernel, final=final),
        grid=(steps,),
        in_specs=[slot(0), slot(1), slot(2), slot(3),
                  pl.BlockSpec((th, LANES), lambda i: (i, 0)),
                  pl.BlockSpec((th, d), lambda i: (i, 0)),
                  pl.BlockSpec((1, 1, d), lambda i: (i // per_b, 0, 0)),
                  pl.BlockSpec((1, d), lambda i: (0, 0))],
        out_specs=pl.BlockSpec((th, d), lambda i: (i, 0)),
        out_shape=jax.ShapeDtypeStruct((t, d), F32),
        compiler_params=_cparams(("arbitrary",)),
        name="moe_combine",
    )(y4, y4, y4, y4, gates, x1, g2, fw)


def _retention_tables(seq):
    pos = jnp.arange(seq, dtype=F32)
    inv_freq = ROPE_BASE ** (-jnp.arange(0, HEAD_DK, 2, dtype=F32) / HEAD_DK)
    ang = pos[:, None] * inv_freq[None, :]
    cos, sin = jnp.cos(ang), jnp.sin(ang)
    half = HEAD_DK // 2
    zpad = jnp.zeros((seq, 64 - half), F32)
    cos_t = jnp.concatenate([cos, zpad, cos, zpad], axis=1)
    sin_t = jnp.concatenate([-sin, zpad, sin, zpad], axis=1)
    log_gamma = jnp.log1p(-jnp.exp2(-5.0 - jnp.arange(N_HEADS, dtype=F32)))
    log_decay = jnp.broadcast_to(log_gamma[None, :, None], (CHUNK, N_HEADS, HEAD_PAD))
    cum = jnp.cumsum(log_decay, axis=0)
    tot = cum[-1:]
    shp = lambda a: a.reshape(a.shape[0], HP)
    return (cos_t, sin_t, shp(jnp.exp(cum)), shp(jnp.exp(-cum)), shp(jnp.exp(tot - cum)),
            shp(jnp.exp(tot)))


def kernel(x, c, norm1_w, norm2_w, w_mod, b_mod, w_in, gla_w_a2, gla_b_a, gla_norm_w, ret_norm_w, s5_a_re, s5_a_im, s5_log_dt, s5_b_re, s5_b_im, s5_c_re, s5_c_im, s5_d, s5_w_glu, s5_b_glu, w_out, router_w, router_b, w_up, b_up, w_down, b_down, final_norm_w):
    batch, seq, d = x.shape
    depth = w_mod.shape[0]
    t = batch * seq
    n_slots = t * TOP_K
    n_blocks = n_slots // ROW_BLK + N_EXPERTS
    n_rows = n_blocks * ROW_BLK

    mod = _modulation(c, w_mod, b_mod)
    ret_tabs = _retention_tables(seq)
    x2d = x.reshape(t, d)

    for i in range(depth):
        m6 = mod[i].reshape(batch, 6, 1, d)
        sh1, sc1, g1, sh2, sc2, g2 = (m6[:, j] for j in range(6))

        w_p = _take_cols(w_in[i], _IN_SRC).astype(BF16)
        wa_p = jnp.zeros((LANES, HP), F32).at[:GATE_RANK].set(_take_cols(gla_w_a2[i], _DK_SRC))
        ba_p = _take_cols(gla_b_a[i], _DK_SRC).reshape(1, HP)
        gnw = _take_cols(gla_norm_w[i], _DV_SRC).reshape(1, HP)
        rnw = _take_cols(ret_norm_w[i], _DV_SRC).reshape(1, HP)
        kv = N_HEADS * HEAD_DV
        wo_g = _take_rows(w_out[i, :kv], _DV_SRC).astype(BF16)
        wo_r = _take_rows(w_out[i, kv:2 * kv], _DV_SRC).astype(BF16)
        wo_s = w_out[i, 2 * kv:].astype(BF16)
        rw_f = jnp.zeros((d, LANES), F32).at[:, :N_EXPERTS].set(router_w[i])
        rw_hi = rw_f.astype(BF16)
        rw_p = jnp.concatenate([rw_hi, (rw_f - rw_hi.astype(F32)).astype(BF16)], axis=1)
        rb_p = jnp.full((1, LANES), -1e30, F32).at[0, :N_EXPERTS].set(router_b[i])

        proj = _in_projection(x2d, sc1, sh1, norm1_w[i].reshape(1, d), w_p, seq)
        o_gla = _gla_mixer(proj, batch, seq, wa_p.astype(BF16), ba_p, gnw)
        o_ret = _ret_mixer(proj, batch, seq, *ret_tabs, rnw)
        s5_tabs = _s5_tables(s5_a_re[i], s5_a_im[i], s5_log_dt[i], s5_b_re[i], s5_b_im[i],
                             s5_c_re[i], s5_c_im[i], s5_d[i], s5_w_glu[i], s5_b_glu[i])
        o_s5 = _s5_mixer(proj, batch, seq, s5_tabs)

        x1, hdn, idx, gates, rank, counts = _outproj_router(
            o_gla, o_ret, o_s5, wo_g, wo_r, wo_s, x2d, g1, sc2, sh2,
            norm2_w[i].reshape(1, d), rw_p, rb_p, seq)

        cnt = counts[0, :N_EXPERTS].astype(jnp.int32)
        padded = (cnt + ROW_BLK - 1) // ROW_BLK * ROW_BLK
        pad_ends = jnp.cumsum(padded)
        pad_starts = pad_ends - padded
        dest = (pad_starts[idx[:, :TOP_K]] + rank[:, :TOP_K]).astype(jnp.int32)
        blk_start = jnp.arange(n_blocks, dtype=jnp.int32) * ROW_BLK
        block_e = jnp.minimum(jnp.sum(pad_ends[None, :] <= blk_start[:, None], axis=1),
                              N_EXPERTS - 1).astype(jnp.int32)
        n_valid = jnp.clip((pad_starts + cnt)[block_e] - blk_start, 0, ROW_BLK).astype(jnp.int32)
        dest_sm = dest.T.reshape(-1)

        rows = _scatter_rows(hdn, dest_sm, n_rows)
        out_rows = _experts(i, block_e, n_valid, rows, w_up, b_up, w_down, b_down)
        y4 = _gather_rows(out_rows, dest_sm)
        x2d = _combine(y4, gates, x1, g2, final_norm_w.reshape(1, d), seq,
                       final=(i == depth - 1))

    return x2d.reshape(batch, seq, d)
```

```python
import functools

import numpy as np
import jax
import jax.numpy as jnp
from jax import lax
from jax.experimental import pallas as pl
from jax.experimental.pallas import tpu as pltpu
from jax.experimental.pallas import tpu_sc as plsc

D_MODEL = 1024
CHUNK = 64
NORM_EPS = 1e-5
N_HEADS = 4
HEAD_DK = 48
HEAD_DV = 96
GATE_RANK = 16
GATE_TEMP = 16.0
ROPE_BASE = 10000.0
S5_WIDTH = 256
S5_GROUP_DIM = 16
S5_GROUPS = 16
S5_STATE = 64
N_EXPERTS = 32
TOP_K = 4
D_FF = 1024
SWIGLU_LIMIT = 7.0
SWIGLU_ALPHA = 1.702

LANES = 128
HEAD_PAD = LANES
HP = N_HEADS * HEAD_PAD
VMEM_LIMIT = 56 * 1024 * 1024

OFF_GQ, OFF_GK, OFF_GV, OFF_GG = 0, HP, 2 * HP, 3 * HP
OFF_RQ, OFF_RK, OFF_RV, OFF_RG = 4 * HP, 5 * HP, 6 * HP, 7 * HP
OFF_SU = 8 * HP
OFF_GA = OFF_SU + S5_WIDTH
NP_COLS = OFF_GA + LANES
PROJ_CH = 896

ROW_BLK = 512

F32 = jnp.float32
BF16 = jnp.bfloat16


def _in_col_map():
    src = -np.ones((NP_COLS,), np.int64)
    kq = N_HEADS * HEAD_DK
    kv = N_HEADS * HEAD_DV
    base = dict(gq=0, gk=kq, gv=2 * kq, gg=2 * kq + kv, ga=2 * kq + 2 * kv)
    r0 = base['ga'] + GATE_RANK
    base.update(rq=r0, rk=r0 + kq, rv=r0 + 2 * kq, rg=r0 + 2 * kq + kv, su=r0 + 2 * kq + 2 * kv)
    half = HEAD_DK // 2
    for h in range(N_HEADS):
        for d in range(HEAD_DK):
            src[OFF_GQ + h * HEAD_PAD + d] = base['gq'] + h * HEAD_DK + d
            src[OFF_GK + h * HEAD_PAD + d] = base['gk'] + h * HEAD_DK + d
            lane = d if d < half else 64 + (d - half)
            src[OFF_RQ + h * HEAD_PAD + lane] = base['rq'] + h * HEAD_DK + d
            src[OFF_RK + h * HEAD_PAD + lane] = base['rk'] + h * HEAD_DK + d
        for d in range(HEAD_DV):
            src[OFF_GV + h * HEAD_PAD + d] = base['gv'] + h * HEAD_DV + d
            src[OFF_GG + h * HEAD_PAD + d] = base['gg'] + h * HEAD_DV + d
            src[OFF_RV + h * HEAD_PAD + d] = base['rv'] + h * HEAD_DV + d
            src[OFF_RG + h * HEAD_PAD + d] = base['rg'] + h * HEAD_DV + d
    src[OFF_SU:OFF_SU + S5_WIDTH] = base['su'] + np.arange(S5_WIDTH)
    src[OFF_GA:OFF_GA + GATE_RANK] = base['ga'] + np.arange(GATE_RANK)
    return src


_IN_SRC = _in_col_map()


def _head_pad_map(width):
    src = -np.ones((HP,), np.int64)
    for h in range(N_HEADS):
        src[h * HEAD_PAD:h * HEAD_PAD + width] = h * width + np.arange(width)
    return src


_DV_SRC = _head_pad_map(HEAD_DV)
_DK_SRC = _head_pad_map(HEAD_DK)


def _take_cols(w, src):
    out = jnp.take(w, jnp.asarray(np.maximum(src, 0)), axis=-1)
    return jnp.where(jnp.asarray(src >= 0), out, 0)


def _take_rows(w, src):
    out = jnp.take(w, jnp.asarray(np.maximum(src, 0)), axis=0)
    return jnp.where(jnp.asarray(src >= 0)[:, None], out, 0)


def _cparams(sem):
    return pltpu.CompilerParams(dimension_semantics=sem, vmem_limit_bytes=VMEM_LIMIT)


def _mod_kernel(c_ref, w_ref, b_ref, o_ref):
    c = c_ref[...]
    cond = c * jax.nn.sigmoid(c)
    o_ref[0] = jnp.dot(cond, w_ref[0], preferred_element_type=F32,
                       precision=lax.Precision.HIGHEST) + b_ref[0]


def _modulation(c, w_mod, b_mod):
    depth, d, n = w_mod.shape
    b = c.shape[0]
    nb = 1536
    return pl.pallas_call(
        _mod_kernel,
        grid=(depth, n // nb),
        in_specs=[pl.BlockSpec((b, d), lambda l, j: (0, 0)),
                  pl.BlockSpec((1, d, nb), lambda l, j: (l, 0, j)),
                  pl.BlockSpec((1, 1, nb), lambda l, j: (l, 0, j))],
        out_specs=pl.BlockSpec((1, b, nb), lambda l, j: (l, 0, j)),
        out_shape=jax.ShapeDtypeStruct((depth, b, n), F32),
        compiler_params=_cparams(("arbitrary", "arbitrary")),
        name="adaln_mod",
    )(c, w_mod, b_mod.reshape(depth, 1, n))


def _rms_mod(x, nw, sc, sh):
    y = x * lax.rsqrt(jnp.mean(x * x, axis=-1, keepdims=True) + NORM_EPS)
    return (y * nw) * (1.0 + sc) + sh


def _inproj_kernel(x_ref, sc_ref, sh_ref, nw_ref, w_ref, o_ref):
    h = _rms_mod(x_ref[...], nw_ref[...], sc_ref[0], sh_ref[0]).astype(BF16)
    for j in range(NP_COLS // PROJ_CH):
        cs = slice(j * PROJ_CH, (j + 1) * PROJ_CH)
        o_ref[:, cs] = jnp.dot(h, w_ref[:, cs], preferred_element_type=F32).astype(BF16)


def _in_projection(x2d, sc, sh, nw, w_p, seq, tm=512):
    t, d = x2d.shape
    per_b = seq // tm
    return pl.pallas_call(
        _inproj_kernel,
        grid=(t // tm,),
        in_specs=[pl.BlockSpec((tm, d), lambda i: (i, 0)),
                  pl.BlockSpec((1, 1, d), lambda i: (i // per_b, 0, 0)),
                  pl.BlockSpec((1, 1, d), lambda i: (i // per_b, 0, 0)),
                  pl.BlockSpec((1, d), lambda i: (0, 0)),
                  pl.BlockSpec((d, NP_COLS), lambda i: (0, 0))],
        out_specs=pl.BlockSpec((tm, NP_COLS), lambda i: (i, 0)),
        out_shape=jax.ShapeDtypeStruct((t, NP_COLS), BF16),
        compiler_params=_cparams(("arbitrary",)),
        name="in_proj",
    )(x2d, sc, sh, nw, w_p)


_NT = (((1,), (1,)), ((), ()))
_TN = (((0,), (0,)), ((), ()))


def _tri_mask():
    r = lax.broadcasted_iota(jnp.int32, (CHUNK, CHUNK), 0)
    c = lax.broadcasted_iota(jnp.int32, (CHUNK, CHUNK), 1)
    return r >= c


def _head_attention(qd, ki, ke, vh, et, st_ref, h, causal):
    qb = qd.astype(BF16)
    sc = lax.dot_general(qb, ki.astype(BF16), _NT, preferred_element_type=F32)
    sc = jnp.where(causal, sc, 0.0)
    st = st_ref[h]
    o = jnp.dot(sc.astype(BF16), vh, preferred_element_type=F32)
    o = o + lax.dot_general(qb, st.astype(BF16), _NT, preferred_element_type=F32)
    st_ref[h] = st * et + lax.dot_general(vh, ke.astype(BF16), _TN, preferred_element_type=F32)
    return o


def _gla_kernel(q_ref, k_ref, v_ref, g_ref, a_ref, wa_ref, ba_ref, nw_ref, o_ref, st_ref):
    @pl.when(pl.program_id(1) == 0)
    def _():
        st_ref[...] = jnp.zeros_like(st_ref)

    causal = _tri_mask()
    tri = causal.astype(BF16)
    n_chunks = q_ref.shape[0] // CHUNK

    def chunk(c, carry):
        r = pl.ds(pl.multiple_of(c * CHUNK, CHUNK), CHUNK)
        z = jnp.dot(a_ref[r, :], wa_ref[...], preferred_element_type=F32) + ba_ref[...]
        la = (jnp.minimum(z, 0.0) - jnp.log1p(jnp.exp(-jnp.abs(z)))) * (1.0 / GATE_TEMP)
        hi = la.astype(BF16)
        lo = (la - hi.astype(F32)).astype(BF16)
        cum = (jnp.dot(tri, hi, preferred_element_type=F32)
               + jnp.dot(tri, lo, preferred_element_type=F32))
        tot = cum[CHUNK - 1:CHUNK, :]
        qd = (q_ref[r, :].astype(F32) * (HEAD_DK ** -0.5)) * jnp.exp(cum)
        kf = k_ref[r, :].astype(F32)
        ki = kf * jnp.exp(-cum)
        ke = kf * jnp.exp(tot - cum)
        et = jnp.exp(tot)
        for h in range(N_HEADS):
            sl = slice(h * HEAD_PAD, (h + 1) * HEAD_PAD)
            o = _head_attention(qd[:, sl], ki[:, sl], ke[:, sl], v_ref[r, sl], et[:, sl],
                                st_ref, h, causal)
            ms = jnp.sum(o * o, axis=-1, keepdims=True) * (1.0 / HEAD_DV)
            y = (o * lax.rsqrt(ms + NORM_EPS)) * nw_ref[:, sl]
            g = g_ref[r, sl].astype(F32)
            o_ref[r, sl] = (y * (g * jax.nn.sigmoid(g))).astype(BF16)
        return carry

    lax.fori_loop(0, n_chunks, chunk, 0, unroll=4)


def _ret_kernel(q_ref, k_ref, v_ref, g_ref, cos_ref, sin_ref, dq_ref, dki_ref, dke_ref,
                dt_ref, nw_ref, o_ref, st_ref):
    @pl.when(pl.program_id(1) == 0)
    def _():
        st_ref[...] = jnp.zeros_like(st_ref)

    causal = _tri_mask()
    n_chunks = q_ref.shape[0] // CHUNK
    lane = lax.broadcasted_iota(jnp.int32, (CHUNK, HEAD_PAD), 1)
    real = lane < HEAD_DV

    def chunk(c, carry):
        r = pl.ds(pl.multiple_of(c * CHUNK, CHUNK), CHUNK)
        cos = cos_ref[r, :]
        sin = sin_ref[r, :]
        for h in range(N_HEADS):
            sl = slice(h * HEAD_PAD, (h + 1) * HEAD_PAD)
            qh = q_ref[r, sl].astype(F32)
            kh = k_ref[r, sl].astype(F32)
            qr = qh * cos + pltpu.roll(qh, 64, 1) * sin
            kr = (kh * cos + pltpu.roll(kh, 64, 1) * sin) * (HEAD_DK ** -0.5)
            o = _head_attention(qr * dq_ref[:, sl], kr * dki_ref[:, sl], kr * dke_ref[:, sl],
                                v_ref[r, sl], dt_ref[:, sl], st_ref, h, causal)
            mu = jnp.sum(o, axis=-1, keepdims=True) * (1.0 / HEAD_DV)
            oc = jnp.where(real, o - mu, 0.0)
            var = jnp.sum(oc * oc, axis=-1, keepdims=True) * (1.0 / HEAD_DV)
            y = (oc * lax.rsqrt(var + NORM_EPS)) * nw_ref[:, sl]
            g = g_ref[r, sl].astype(F32)
            o_ref[r, sl] = (y * (g * jax.nn.sigmoid(g))).astype(BF16)
        return carry

    lax.fori_loop(0, n_chunks, chunk, 0, unroll=2)


def _proj_spec(tl, width, col_off, per_b):
    cb = col_off // width
    return pl.BlockSpec((tl, width), lambda b, l: (b * per_b + l, cb))


def _full(shape):
    return pl.BlockSpec(shape, lambda b, l: (0,) * len(shape))


def _gla_mixer(proj, batch, seq, wa_p, ba_p, nw_p, tl=512):
    per_b = seq // tl
    return pl.pallas_call(
        _gla_kernel,
        grid=(batch, per_b),
        in_specs=[_proj_spec(tl, HP, OFF_GQ, per_b), _proj_spec(tl, HP, OFF_GK, per_b),
                  _proj_spec(tl, HP, OFF_GV, per_b), _proj_spec(tl, HP, OFF_GG, per_b),
                  _proj_spec(tl, LANES, OFF_GA, per_b),
                  _full((LANES, HP)), _full((1, HP)), _full((1, HP))],
        out_specs=pl.BlockSpec((tl, HP), lambda b, l: (b * per_b + l, 0)),
        out_shape=jax.ShapeDtypeStruct((batch * seq, HP), BF16),
        scratch_shapes=[pltpu.VMEM((N_HEADS, HEAD_PAD, HEAD_PAD), F32)],
        compiler_params=_cparams(("arbitrary", "arbitrary")),
        name="gla_mixer",
    )(proj, proj, proj, proj, proj, wa_p, ba_p, nw_p)


def _ret_mixer(proj, batch, seq, cos_t, sin_t, dq, dki, dke, dtot, nw_p, tl=512):
    per_b = seq // tl
    return pl.pallas_call(
        _ret_kernel,
        grid=(batch, per_b),
        in_specs=[_proj_spec(tl, HP, OFF_RQ, per_b), _proj_spec(tl, HP, OFF_RK, per_b),
                  _proj_spec(tl, HP, OFF_RV, per_b), _proj_spec(tl, HP, OFF_RG, per_b),
                  pl.BlockSpec((tl, HEAD_PAD), lambda b, l: (l, 0)),
                  pl.BlockSpec((tl, HEAD_PAD), lambda b, l: (l, 0)),
                  _full((CHUNK, HP)), _full((CHUNK, HP)), _full((CHUNK, HP)),
                  _full((1, HP)), _full((1, HP))],
        out_specs=pl.BlockSpec((tl, HP), lambda b, l: (b * per_b + l, 0)),
        out_shape=jax.ShapeDtypeStruct((batch * seq, HP), BF16),
        scratch_shapes=[pltpu.VMEM((N_HEADS, HEAD_PAD, HEAD_PAD), F32)],
        compiler_params=_cparams(("arbitrary", "arbitrary")),
        name="ret_mixer",
    )(proj, proj, proj, proj, cos_t, sin_t, dq, dki, dke, dtot, nw_p)


S5_N = S5_GROUPS * S5_STATE


def _gelu_tanh(x):
    return 0.5 * x * (1.0 + jnp.tanh(np.sqrt(2.0 / np.pi) * (x + 0.044715 * (x * x * x))))


def _s5_kernel(u_ref, bb_ref, cb_ref, nr_ref, ni_ref, pr_ref, pi_ref, lr_ref, li_ref,
               d_ref, wg_ref, bg_ref, o_ref, sr_ref, si_ref, x_scr, s_scr):
    @pl.when(pl.program_id(1) == 0)
    def _():
        sr_ref[...] = jnp.zeros_like(sr_ref)
        si_ref[...] = jnp.zeros_like(si_ref)

    tri = _tri_mask().astype(BF16)
    n_chunks = u_ref.shape[0] // CHUNK
    u = u_ref[...]
    x_scr[...] = jnp.dot(u, bb_ref[...], preferred_element_type=F32)

    def chunk(c, carry):
        r = pl.ds(pl.multiple_of(c * CHUNK, CHUNK), CHUNK)
        xr, xi = x_scr[r, :S5_N], x_scr[r, S5_N:]
        nr, ni = nr_ref[...], ni_ref[...]
        p_r = jnp.dot(tri, (xr * nr - xi * ni).astype(BF16), preferred_element_type=F32)
        p_i = jnp.dot(tri, (xr * ni + xi * nr).astype(BF16), preferred_element_type=F32)
        s0r, s0i = sr_ref[...], si_ref[...]
        lr, li = lr_ref[...], li_ref[...]
        q_r = p_r + (s0r * lr - s0i * li)
        q_i = p_i + (s0r * li + s0i * lr)
        pr, pi = pr_ref[...], pi_ref[...]
        s_r = q_r * pr - q_i * pi
        s_i = q_r * pi + q_i * pr
        sr_ref[...] = s_r[CHUNK - 1:CHUNK, :]
        si_ref[...] = s_i[CHUNK - 1:CHUNK, :]
        s_scr[r, :S5_N] = s_r.astype(BF16)
        s_scr[r, S5_N:] = s_i.astype(BF16)
        return carry

    lax.fori_loop(0, n_chunks, chunk, 0, unroll=2)
    y = jnp.dot(s_scr[...], cb_ref[...], preferred_element_type=F32)
    y = _gelu_tanh(y + d_ref[...] * u.astype(F32))
    gate = jnp.dot(y.astype(BF16), wg_ref[...], preferred_element_type=F32) + bg_ref[...]
    o_ref[...] = (y * jax.nn.sigmoid(gate)).astype(BF16)


def _s5_mixer(proj, batch, seq, tabs, tl=512):
    per_b = seq // tl
    bb, cb, nr, ni, pr, pi, lr, li, dsk, wg, bg = tabs
    return pl.pallas_call(
        _s5_kernel,
        grid=(batch, per_b),
        in_specs=[_proj_spec(tl, S5_WIDTH, OFF_SU, per_b),
                  _full((S5_WIDTH, 2 * S5_N)), _full((2 * S5_N, S5_WIDTH)),
                  _full((CHUNK, S5_N)), _full((CHUNK, S5_N)),
                  _full((CHUNK, S5_N)), _full((CHUNK, S5_N)),
                  _full((1, S5_N)), _full((1, S5_N)),
                  _full((1, S5_WIDTH)), _full((S5_WIDTH, S5_WIDTH)), _full((1, S5_WIDTH))],
        out_specs=pl.BlockSpec((tl, S5_WIDTH), lambda b, l: (b * per_b + l, 0)),
        out_shape=jax.ShapeDtypeStruct((batch * seq, S5_WIDTH), BF16),
        scratch_shapes=[pltpu.VMEM((1, S5_N), F32), pltpu.VMEM((1, S5_N), F32),
                        pltpu.VMEM((tl, 2 * S5_N), F32), pltpu.VMEM((tl, 2 * S5_N), BF16)],
        compiler_params=_cparams(("arbitrary", "arbitrary")),
        name="s5_mixer",
    )(proj, bb, cb, nr, ni, pr, pi, lr, li, dsk, wg, bg)


def _s5_tables(a_re, a_im, log_dt, b_re, b_im, c_re, c_im, d_skip, w_glu, b_glu):
    lam = lax.complex(a_re, a_im)
    dt = jnp.exp(log_dt)[:, None]
    lam_bar = jnp.exp(lam * dt)
    b_bar = ((lam_bar - 1.0) / lam)[..., None] * lax.complex(b_re, b_im)
    eye = jnp.eye(S5_GROUPS, dtype=F32)
    def blk_b(m):
        return jnp.einsum('gph,gk->ghkp', m, eye).reshape(S5_WIDTH, S5_N)
    bb = jnp.concatenate([blk_b(jnp.real(b_bar)), blk_b(jnp.imag(b_bar))], axis=1)
    def blk_c(m):
        return jnp.einsum('ghp,gk->kpgh', m, eye).reshape(S5_N, S5_WIDTH)
    cb = jnp.concatenate([blk_c(c_re), blk_c(-c_im)], axis=0)
    steps = jnp.arange(CHUNK, dtype=F32)[:, None, None]
    lam_dt = (lam * dt)[None]
    pos = jnp.exp(lam_dt * steps).reshape(CHUNK, S5_N)
    neg = jnp.exp(-lam_dt * steps).reshape(CHUNK, S5_N)
    one = lam_bar.reshape(1, S5_N)
    return (bb.astype(BF16), cb.astype(BF16), jnp.real(neg), jnp.imag(neg), jnp.real(pos),
            jnp.imag(pos), jnp.real(one), jnp.imag(one), d_skip.reshape(1, S5_WIDTH),
            w_glu.astype(BF16), b_glu.reshape(1, S5_WIDTH))


def _pack_bf16_pairs(x):
    w = x.shape[1] // 2
    xb = x.astype(BF16).astype(F32)
    hi = lax.bitcast_convert_type(xb[:, :w], jnp.uint32)
    lo = lax.bitcast_convert_type(xb[:, w:], jnp.uint32)
    return hi | (lo >> 16)


def _unpack_bf16_pairs(p):
    hi = lax.bitcast_convert_type(p & jnp.uint32(0xFFFF0000), F32)
    lo = lax.bitcast_convert_type(p << 16, F32)
    return hi, lo


def _router_kernel(og_ref, or_ref, os_ref, wg_ref, wr_ref, ws_ref, x_ref, g1_ref, sc_ref,
                   sh_ref, nw_ref, rw_ref, rb_ref,
                   x1_ref, h_ref, idx_ref, gate_ref, rank_ref, cnt_ref, carry_ref):
    i = pl.program_id(0)

    @pl.when(i == 0)
    def _():
        carry_ref[...] = jnp.zeros_like(carry_ref)

    mix = (jnp.dot(og_ref[...], wg_ref[...], preferred_element_type=F32)
           + jnp.dot(or_ref[...], wr_ref[...], preferred_element_type=F32)
           + jnp.dot(os_ref[...], ws_ref[...], preferred_element_type=F32))
    x1 = x_ref[...] + g1_ref[0] * mix
    x1_ref[...] = x1
    hdn = _rms_mod(x1, nw_ref[...], sc_ref[0], sh_ref[0])
    h_ref[...] = _pack_bf16_pairs(hdn)
    h_hi = hdn.astype(BF16)
    h_lo = (hdn - h_hi.astype(F32)).astype(BF16)
    p = jnp.dot(h_hi, rw_ref[...], preferred_element_type=F32)
    logits = (p[:, :LANES] + p[:, LANES:]
              + jnp.dot(h_lo, rw_ref[:, :LANES], preferred_element_type=F32)) + rb_ref[...]
    tm = logits.shape[0]
    lane = lax.broadcasted_iota(jnp.int32, (tm, LANES), 1)
    work = logits
    onehot = jnp.zeros((tm, LANES), F32)
    vals, idxs = [], []
    for _ in range(TOP_K):
        m = jnp.max(work, axis=-1, keepdims=True)
        ix = jnp.min(jnp.where(work == m, lane, LANES), axis=-1, keepdims=True)
        sel = lane == ix
        work = jnp.where(sel, -jnp.inf, work)
        onehot = onehot + sel.astype(F32)
        vals.append(m)
        idxs.append(ix)
    exps = [jnp.exp(v - vals[0]) for v in vals]
    denom = exps[0] + exps[1] + exps[2] + exps[3]
    r = lax.broadcasted_iota(jnp.int32, (tm, tm), 0)
    c = lax.broadcasted_iota(jnp.int32, (tm, tm), 1)
    strict = (r > c).astype(BF16)
    before = jnp.dot(strict, onehot.astype(BF16), preferred_element_type=F32) + carry_ref[...]
    idx_out = jnp.zeros((tm, LANES), jnp.int32)
    gate_out = jnp.zeros((tm, LANES), F32)
    rank_out = jnp.zeros((tm, LANES), F32)
    for k in range(TOP_K):
        rk = jnp.sum(jnp.where(lane == idxs[k], before, 0.0), axis=-1, keepdims=True)
        idx_out = jnp.where(lane == k, idxs[k], idx_out)
        gate_out = jnp.where(lane == k, exps[k] / denom, gate_out)
        rank_out = jnp.where(lane == k, rk, rank_out)
    idx_ref[...] = idx_out
    gate_ref[...] = gate_out
    rank_ref[...] = rank_out.astype(jnp.int32)
    total = carry_ref[...] + jnp.sum(onehot, axis=0, keepdims=True)
    carry_ref[...] = total
    cnt_ref[...] = total


def _outproj_router(o_gla, o_ret, o_s5, wg, wr, ws, x2d, g1, sc2, sh2, nw2, rw_p, rb_p, seq,
                    tm=512):
    t, d = x2d.shape
    per_b = seq // tm
    row = lambda w: pl.BlockSpec((tm, w), lambda i: (i, 0))
    full = lambda s: pl.BlockSpec(s, lambda i: (0,) * len(s))
    per_batch = pl.BlockSpec((1, 1, d), lambda i: (i // per_b, 0, 0))
    return pl.pallas_call(
        _router_kernel,
        grid=(t // tm,),
        in_specs=[row(HP), row(HP), row(S5_WIDTH), full((HP, d)), full((HP, d)),
                  full((S5_WIDTH, d)), row(d), per_batch, per_batch, per_batch,
                  full((1, d)), full((d, 2 * LANES)), full((1, LANES))],
        out_specs=[row(d), row(d // 2), row(LANES), row(LANES), row(LANES), full((1, LANES))],
        out_shape=[jax.ShapeDtypeStruct((t, d), F32),
                   jax.ShapeDtypeStruct((t, d // 2), jnp.uint32),
                   jax.ShapeDtypeStruct((t, LANES), jnp.int32),
                   jax.ShapeDtypeStruct((t, LANES), F32),
                   jax.ShapeDtypeStruct((t, LANES), jnp.int32),
                   jax.ShapeDtypeStruct((1, LANES), F32)],
        scratch_shapes=[pltpu.VMEM((1, LANES), F32)],
        compiler_params=_cparams(("arbitrary",)),
        name="outproj_router",
    )(o_gla, o_ret, o_s5, wg, wr, ws, x2d, g1, sc2, sh2, nw2, rw_p, rb_p)


GATHER_WIN = 64


def _gather_rows(table, idx):
    m = idx.shape[0]
    w = table.shape[1]
    mesh = plsc.VectorSubcoreMesh(core_axis_name="core", subcore_axis_name="subcore")

    @functools.partial(pl.kernel, out_type=jax.ShapeDtypeStruct((m, w), table.dtype),
                       mesh=mesh, name="sc_row_gather")
    def gather(x_hbm, i_hbm, o_hbm):
        def body(i_vmem, o_vmem):
            pltpu.sync_copy(x_hbm.at[i_vmem], o_vmem)

        pltpu.emit_pipeline(
            body,
            grid=(m // GATHER_WIN,),
            in_specs=[pl.BlockSpec((GATHER_WIN,), lambda i: (i,))],
            out_specs=[pl.BlockSpec((GATHER_WIN, w), lambda i: (i, 0))],
            core_axis_name=("core", "subcore"),
            dimension_semantics=(pltpu.PARALLEL,),
        )(i_hbm, o_hbm)

    return gather(table, idx)


def _scatter_rows(x, dest_slot_major, n_rows):
    t, w = x.shape
    steps = t // GATHER_WIN
    mesh = plsc.VectorSubcoreMesh(core_axis_name="core", subcore_axis_name="subcore")

    @functools.partial(pl.kernel, out_type=jax.ShapeDtypeStruct((n_rows, w), x.dtype),
                       mesh=mesh, name="sc_row_scatter")
    def scatter(x_hbm, i_hbm, o_hbm):
        def body(x_vmem, i0, i1, i2, i3):
            for i_vmem in (i0, i1, i2, i3):
                pltpu.sync_copy(x_vmem, o_hbm.at[i_vmem])

        slot = lambda k: pl.BlockSpec((GATHER_WIN,), lambda i: (k * steps + i,))
        pltpu.emit_pipeline(
            body,
            grid=(steps,),
            in_specs=[pl.BlockSpec((GATHER_WIN, w), lambda i: (i, 0)),
                      slot(0), slot(1), slot(2), slot(3)],
            out_specs=[],
            core_axis_name=("core", "subcore"),
            dimension_semantics=(pltpu.PARALLEL,),
        )(x_hbm, i_hbm, i_hbm, i_hbm, i_hbm)

    return scatter(x, dest_slot_major)


def _expert_kernel(be_ref, nv_ref, rows_ref, wu_ref, bu_ref, wd_ref, bd_ref, o_ref,
                   wu_bf, wd_bf):
    i = pl.program_id(0)
    e = be_ref[i]
    prev = be_ref[jnp.maximum(i - 1, 0)]

    @pl.when((i == 0) | (e != prev))
    def _():
        wu_bf[...] = wu_ref[0, 0].astype(BF16)
        wd_bf[...] = wd_ref[0, 0].astype(BF16)

    @pl.when(nv_ref[i] > 0)
    def _():
        row = lax.broadcasted_iota(jnp.int32, rows_ref.shape, 0)
        x_hi, x_lo = _unpack_bf16_pairs(jnp.where(row < nv_ref[i], rows_ref[...], jnp.uint32(0)))
        half = D_MODEL // 2
        up = (jnp.dot(x_hi.astype(BF16), wu_bf[:half, :], preferred_element_type=F32)
              + jnp.dot(x_lo.astype(BF16), wu_bf[half:, :], preferred_element_type=F32)
              + bu_ref[0, 0])
        x_glu = jnp.minimum(up[:, :D_FF], SWIGLU_LIMIT)
        x_lin = jnp.clip(up[:, D_FF:], -SWIGLU_LIMIT, SWIGLU_LIMIT)
        act = x_glu * jax.nn.sigmoid(SWIGLU_ALPHA * x_glu) * (x_lin + 1.0)
        o_ref[...] = _pack_bf16_pairs(
            jnp.dot(act.astype(BF16), wd_bf[...], preferred_element_type=F32) + bd_ref[0, 0])

    @pl.when(nv_ref[i] <= 0)
    def _():
        o_ref[...] = jnp.zeros_like(o_ref)


def _experts(layer, block_e, n_valid, rows, w_up, b_up, w_down, b_down):
    n_rows, dh = rows.shape
    d = 2 * dh
    n_blocks = n_rows // ROW_BLK
    depth, ne, _, f2 = w_up.shape
    wsel = lambda i, be, nu: (layer, be[i], 0, 0)
    grid_spec = pltpu.PrefetchScalarGridSpec(
        num_scalar_prefetch=2,
        grid=(n_blocks,),
        in_specs=[pl.BlockSpec((ROW_BLK, dh), lambda i, be, nu: (i, 0)),
                  pl.BlockSpec((1, 1, d, f2), wsel),
                  pl.BlockSpec((1, 1, 1, f2), wsel),
                  pl.BlockSpec((1, 1, D_FF, d), wsel),
                  pl.BlockSpec((1, 1, 1, d), wsel)],
        out_specs=pl.BlockSpec((ROW_BLK, dh), lambda i, be, nu: (i, 0)),
        scratch_shapes=[pltpu.VMEM((d, f2), BF16), pltpu.VMEM((D_FF, d), BF16)],
    )
    return pl.pallas_call(
        _expert_kernel,
        grid_spec=grid_spec,
        out_shape=jax.ShapeDtypeStruct((n_rows, dh), jnp.uint32),
        compiler_params=_cparams(("arbitrary",)),
        name="moe_experts",
    )(block_e, n_valid, rows, w_up, b_up.reshape(depth, ne, 1, f2), w_down,
      b_down.reshape(depth, ne, 1, d))


def _combine_kernel(y0_ref, y1_ref, y2_ref, y3_ref, gate_ref, x1_ref, g2_ref, fw_ref, o_ref,
                    *, final):
    gates = gate_ref[...]
    y_hi, y_lo = None, None
    for k, y_ref in enumerate((y0_ref, y1_ref, y2_ref, y3_ref)):
        hi, lo = _unpack_bf16_pairs(y_ref[...])
        g = gates[:, k:k + 1]
        y_hi = g * hi if y_hi is None else y_hi + g * hi
        y_lo = g * lo if y_lo is None else y_lo + g * lo
    y = jnp.concatenate([y_hi, y_lo], axis=1)
    x2 = x1_ref[...] + g2_ref[0] * y
    if final:
        x2 = (x2 * lax.rsqrt(jnp.mean(x2 * x2, axis=-1, keepdims=True) + NORM_EPS)) * fw_ref[...]
    o_ref[...] = x2


def _combine(y4, gates, x1, g2, fw, seq, final, th=256):
    t, d = x1.shape
    steps = t // th
    per_b = seq // th
    slot = lambda k: pl.BlockSpec((th, d // 2), lambda i: (k * steps + i, 0))
    return pl.pallas_call(
        functools.partial(_combine_kernel, final=final),
        grid=(steps,),
        in_specs=[slot(0), slot(1), slot(2), slot(3),
                  pl.BlockSpec((th, LANES), lambda i: (i, 0)),
                  pl.BlockSpec((th, d), lambda i: (i, 0)),
                  pl.BlockSpec((1, 1, d), lambda i: (i // per_b, 0, 0)),
                  pl.BlockSpec((1, d), lambda i: (0, 0))],
        out_specs=pl.BlockSpec((th, d), lambda i: (i, 0)),
        out_shape=jax.ShapeDtypeStruct((t, d), F32),
        compiler_params=_cparams(("arbitrary",)),
        name="moe_combine",
    )(y4, y4, y4, y4, gates, x1, g2, fw)


def _retention_tables(seq):
    pos = jnp.arange(seq, dtype=F32)
    inv_freq = ROPE_BASE ** (-jnp.arange(0, HEAD_DK, 2, dtype=F32) / HEAD_DK)
    ang = pos[:, None] * inv_freq[None, :]
    cos, sin = jnp.cos(ang), jnp.sin(ang)
    half = HEAD_DK // 2
    zpad = jnp.zeros((seq, 64 - half), F32)
    cos_t = jnp.concatenate([cos, zpad, cos, zpad], axis=1)
    sin_t = jnp.concatenate([-sin, zpad, sin, zpad], axis=1)
    log_gamma = jnp.log1p(-jnp.exp2(-5.0 - jnp.arange(N_HEADS, dtype=F32)))
    log_decay = jnp.broadcast_to(log_gamma[None, :, None], (CHUNK, N_HEADS, HEAD_PAD))
    cum = jnp.cumsum(log_decay, axis=0)
    tot = cum[-1:]
    shp = lambda a: a.reshape(a.shape[0], HP)
    return (cos_t, sin_t, shp(jnp.exp(cum)), shp(jnp.exp(-cum)), shp(jnp.exp(tot - cum)),
            shp(jnp.exp(tot)))


def kernel(x, c, norm1_w, norm2_w, w_mod, b_mod, w_in, gla_w_a2, gla_b_a, gla_norm_w, ret_norm_w, s5_a_re, s5_a_im, s5_log_dt, s5_b_re, s5_b_im, s5_c_re, s5_c_im, s5_d, s5_w_glu, s5_b_glu, w_out, router_w, router_b, w_up, b_up, w_down, b_down, final_norm_w):
    batch, seq, d = x.shape
    depth = w_mod.shape[0]
    t = batch * seq
    n_slots = t * TOP_K
    n_blocks = n_slots // ROW_BLK + N_EXPERTS
    n_rows = n_blocks * ROW_BLK

    mod = _modulation(c, w_mod, b_mod)
    ret_tabs = _retention_tables(seq)
    x2d = x.reshape(t, d)

    for i in range(depth):
        m6 = mod[i].reshape(batch, 6, 1, d)
        sh1, sc1, g1, sh2, sc2, g2 = (m6[:, j] for j in range(6))

        w_p = _take_cols(w_in[i], _IN_SRC).astype(BF16)
        wa_p = jnp.zeros((LANES, HP), F32).at[:GATE_RANK].set(_take_cols(gla_w_a2[i], _DK_SRC))
        ba_p = _take_cols(gla_b_a[i], _DK_SRC).reshape(1, HP)
        gnw = _take_cols(gla_norm_w[i], _DV_SRC).reshape(1, HP)
        rnw = _take_cols(ret_norm_w[i], _DV_SRC).reshape(1, HP)
        kv = N_HEADS * HEAD_DV
        wo_g = _take_rows(w_out[i, :kv], _DV_SRC).astype(BF16)
        wo_r = _take_rows(w_out[i, kv:2 * kv], _DV_SRC).astype(BF16)
        wo_s = w_out[i, 2 * kv:].astype(BF16)
        rw_f = jnp.zeros((d, LANES), F32).at[:, :N_EXPERTS].set(router_w[i])
        rw_hi = rw_f.astype(BF16)
        rw_p = jnp.concatenate([rw_hi, (rw_f - rw_hi.astype(F32)).astype(BF16)], axis=1)
        rb_p = jnp.full((1, LANES), -1e30, F32).at[0, :N_EXPERTS].set(router_b[i])

        proj = _in_projection(x2d, sc1, sh1, norm1_w[i].reshape(1, d), w_p, seq)
        o_gla = _gla_mixer(proj, batch, seq, wa_p.astype(BF16), ba_p, gnw)
        o_ret = _ret_mixer(proj, batch, seq, *ret_tabs, rnw)
        s5_tabs = _s5_tables(s5_a_re[i], s5_a_im[i], s5_log_dt[i], s5_b_re[i], s5_b_im[i],
                             s5_c_re[i], s5_c_im[i], s5_d[i], s5_w_glu[i], s5_b_glu[i])
        o_s5 = _s5_mixer(proj, batch, seq, s5_tabs)

        x1, hdn, idx, gates, rank, counts = _outproj_router(
            o_gla, o_ret, o_s5, wo_g, wo_r, wo_s, x2d, g1, sc2, sh2,
            norm2_w[i].reshape(1, d), rw_p, rb_p, seq)

        cnt = counts[0, :N_EXPERTS].astype(jnp.int32)
        padded = (cnt + ROW_BLK - 1) // ROW_BLK * ROW_BLK
        pad_ends = jnp.cumsum(padded)
        pad_starts = pad_ends - padded
        dest = (pad_starts[idx[:, :TOP_K]] + rank[:, :TOP_K]).astype(jnp.int32)
        blk_start = jnp.arange(n_blocks, dtype=jnp.int32) * ROW_BLK
        block_e = jnp.minimum(jnp.sum(pad_ends[None, :] <= blk_start[:, None], axis=1),
                              N_EXPERTS - 1).astype(jnp.int32)
        n_valid = jnp.clip((pad_starts + cnt)[block_e] - blk_start, 0, ROW_BLK).astype(jnp.int32)
        dest_sm = dest.T.reshape(-1)

        rows = _scatter_rows(hdn, dest_sm, n_rows)
        out_rows = _experts(i, block_e, n_valid, rows, w_up, b_up, w_down, b_down)
        y4 = _gather_rows(out_rows, dest_sm)
        x2d = _combine(y4, gates, x1, g2, final_norm_w.reshape(1, d), seq,
                       final=(i == depth - 1))

    return x2d.reshape(batch, seq, d)
```

```python
import functools

import numpy as np
import jax
import jax.numpy as jnp
from jax import lax
from jax.experimental import pallas as pl
from jax.experimental.pallas import tpu as pltpu
from jax.experimental.pallas import tpu_sc as plsc

D_MODEL = 1024
CHUNK = 64
RET_CHUNK = 128
NORM_EPS = 1e-5
N_HEADS = 4
HEAD_DK = 48
HEAD_DV = 96
GATE_RANK = 16
GATE_TEMP = 16.0
ROPE_BASE = 10000.0
S5_WIDTH = 256
S5_GROUP_DIM = 16
S5_GROUPS = 16
S5_STATE = 64
N_EXPERTS = 32
TOP_K = 4
D_FF = 1024
SWIGLU_LIMIT = 7.0
SWIGLU_ALPHA = 1.702

LANES = 128
HEAD_PAD = LANES
HP = N_HEADS * HEAD_PAD
VMEM_LIMIT = 56 * 1024 * 1024

OFF_GQ, OFF_GK, OFF_GV, OFF_GG = 0, HP, 2 * HP, 3 * HP
OFF_RQ, OFF_RK, OFF_RV, OFF_RG = 4 * HP, 5 * HP, 6 * HP, 7 * HP
OFF_SU = 8 * HP
OFF_GA = OFF_SU + S5_WIDTH
NP_COLS = OFF_GA + LANES
PROJ_CH = 896

ROW_BLK = 512

F32 = jnp.float32
BF16 = jnp.bfloat16


def _in_col_map():
    src = -np.ones((NP_COLS,), np.int64)
    kq = N_HEADS * HEAD_DK
    kv = N_HEADS * HEAD_DV
    base = dict(gq=0, gk=kq, gv=2 * kq, gg=2 * kq + kv, ga=2 * kq + 2 * kv)
    r0 = base['ga'] + GATE_RANK
    base.update(rq=r0, rk=r0 + kq, rv=r0 + 2 * kq, rg=r0 + 2 * kq + kv, su=r0 + 2 * kq + 2 * kv)
    half = HEAD_DK // 2
    for h in range(N_HEADS):
        for d in range(HEAD_DK):
            src[OFF_GQ + h * HEAD_PAD + d] = base['gq'] + h * HEAD_DK + d
            src[OFF_GK + h * HEAD_PAD + d] = base['gk'] + h * HEAD_DK + d
            lane = d if d < half else 64 + (d - half)
            src[OFF_RQ + h * HEAD_PAD + lane] = base['rq'] + h * HEAD_DK + d
            src[OFF_RK + h * HEAD_PAD + lane] = base['rk'] + h * HEAD_DK + d
        for d in range(HEAD_DV):
            src[OFF_GV + h * HEAD_PAD + d] = base['gv'] + h * HEAD_DV + d
            src[OFF_GG + h * HEAD_PAD + d] = base['gg'] + h * HEAD_DV + d
            src[OFF_RV + h * HEAD_PAD + d] = base['rv'] + h * HEAD_DV + d
            src[OFF_RG + h * HEAD_PAD + d] = base['rg'] + h * HEAD_DV + d
    src[OFF_SU:OFF_SU + S5_WIDTH] = base['su'] + np.arange(S5_WIDTH)
    src[OFF_GA:OFF_GA + GATE_RANK] = base['ga'] + np.arange(GATE_RANK)
    return src


_IN_SRC = _in_col_map()


def _head_pad_map(width):
    src = -np.ones((HP,), np.int64)
    for h in range(N_HEADS):
        src[h * HEAD_PAD:h * HEAD_PAD + width] = h * width + np.arange(width)
    return src


_DV_SRC = _head_pad_map(HEAD_DV)
_DK_SRC = _head_pad_map(HEAD_DK)


def _take_cols(w, src):
    out = jnp.take(w, jnp.asarray(np.maximum(src, 0)), axis=-1)
    return jnp.where(jnp.asarray(src >= 0), out, 0)


def _take_rows(w, src):
    out = jnp.take(w, jnp.asarray(np.maximum(src, 0)), axis=0)
    return jnp.where(jnp.asarray(src >= 0)[:, None], out, 0)


def _cparams(sem):
    return pltpu.CompilerParams(dimension_semantics=sem, vmem_limit_bytes=VMEM_LIMIT)


def _mod_kernel(c_ref, w_ref, b_ref, o_ref):
    c = c_ref[...]
    cond = c * jax.nn.sigmoid(c)
    o_ref[0] = jnp.dot(cond, w_ref[0], preferred_element_type=F32,
                       precision=lax.Precision.HIGHEST) + b_ref[0]


def _modulation(c, w_mod, b_mod):
    depth, d, n = w_mod.shape
    b = c.shape[0]
    nb = 1536
    return pl.pallas_call(
        _mod_kernel,
        grid=(depth, n // nb),
        in_specs=[pl.BlockSpec((b, d), lambda l, j: (0, 0)),
                  pl.BlockSpec((1, d, nb), lambda l, j: (l, 0, j)),
                  pl.BlockSpec((1, 1, nb), lambda l, j: (l, 0, j))],
        out_specs=pl.BlockSpec((1, b, nb), lambda l, j: (l, 0, j)),
        out_shape=jax.ShapeDtypeStruct((depth, b, n), F32),
        compiler_params=_cparams(("arbitrary", "arbitrary")),
        name="adaln_mod",
    )(c, w_mod, b_mod.reshape(depth, 1, n))


def _rms_mod(x, nw, sc, sh):
    y = x * lax.rsqrt(jnp.mean(x * x, axis=-1, keepdims=True) + NORM_EPS)
    return (y * nw) * (1.0 + sc) + sh


def _inproj_kernel(x_ref, sc_ref, sh_ref, nw_ref, w_ref, o_ref):
    h = _rms_mod(x_ref[...], nw_ref[...], sc_ref[0], sh_ref[0]).astype(BF16)
    for j in range(NP_COLS // PROJ_CH):
        cs = slice(j * PROJ_CH, (j + 1) * PROJ_CH)
        o_ref[:, cs] = jnp.dot(h, w_ref[:, cs], preferred_element_type=F32).astype(BF16)


def _in_projection(x2d, sc, sh, nw, w_p, seq, tm=512):
    t, d = x2d.shape
    per_b = seq // tm
    return pl.pallas_call(
        _inproj_kernel,
        grid=(t // tm,),
        in_specs=[pl.BlockSpec((tm, d), lambda i: (i, 0)),
                  pl.BlockSpec((1, 1, d), lambda i: (i // per_b, 0, 0)),
                  pl.BlockSpec((1, 1, d), lambda i: (i // per_b, 0, 0)),
                  pl.BlockSpec((1, d), lambda i: (0, 0)),
                  pl.BlockSpec((d, NP_COLS), lambda i: (0, 0))],
        out_specs=pl.BlockSpec((tm, NP_COLS), lambda i: (i, 0)),
        out_shape=jax.ShapeDtypeStruct((t, NP_COLS), BF16),
        compiler_params=_cparams(("arbitrary",)),
        name="in_proj",
    )(x2d, sc, sh, nw, w_p)


_NT = (((1,), (1,)), ((), ()))
_TN = (((0,), (0,)), ((), ()))


def _tri_mask(n=CHUNK):
    r = lax.broadcasted_iota(jnp.int32, (n, n), 0)
    c = lax.broadcasted_iota(jnp.int32, (n, n), 1)
    return r >= c


def _head_attention(qd, ki, ke, vh, et, st_ref, h, causal):
    qb = qd.astype(BF16)
    sc = lax.dot_general(qb, ki.astype(BF16), _NT, preferred_element_type=F32)
    sc = jnp.where(causal, sc, 0.0)
    st = st_ref[h]
    o = jnp.dot(sc.astype(BF16), vh, preferred_element_type=F32)
    o = o + lax.dot_general(qb, st.astype(BF16), _NT, preferred_element_type=F32)
    st_ref[h] = st * et + lax.dot_general(vh, ke.astype(BF16), _TN, preferred_element_type=F32)
    return o


def _gla_kernel(q_ref, k_ref, v_ref, g_ref, a_ref, wa_ref, ba_ref, nw_ref, o_ref, st_ref):
    @pl.when(pl.program_id(1) == 0)
    def _():
        st_ref[...] = jnp.zeros_like(st_ref)

    causal = _tri_mask()
    tri = causal.astype(BF16)
    n_chunks = q_ref.shape[0] // CHUNK

    def chunk(c, carry):
        r = pl.ds(pl.multiple_of(c * CHUNK, CHUNK), CHUNK)
        z = jnp.dot(a_ref[r, :], wa_ref[...], preferred_element_type=F32) + ba_ref[...]
        la = (jnp.minimum(z, 0.0) - jnp.log1p(jnp.exp(-jnp.abs(z)))) * (1.0 / GATE_TEMP)
        hi = la.astype(BF16)
        lo = (la - hi.astype(F32)).astype(BF16)
        cum = (jnp.dot(tri, hi, preferred_element_type=F32)
               + jnp.dot(tri, lo, preferred_element_type=F32))
        tot = cum[CHUNK - 1:CHUNK, :]
        qd = (q_ref[r, :].astype(F32) * (HEAD_DK ** -0.5)) * jnp.exp(cum)
        kf = k_ref[r, :].astype(F32)
        ki = kf * jnp.exp(-cum)
        ke = kf * jnp.exp(tot - cum)
        et = jnp.exp(tot)
        for h in range(N_HEADS):
            sl = slice(h * HEAD_PAD, (h + 1) * HEAD_PAD)
            o = _head_attention(qd[:, sl], ki[:, sl], ke[:, sl], v_ref[r, sl], et[:, sl],
                                st_ref, h, causal)
            ms = jnp.sum(o * o, axis=-1, keepdims=True) * (1.0 / HEAD_DV)
            y = (o * lax.rsqrt(ms + NORM_EPS)) * nw_ref[:, sl]
            g = g_ref[r, sl].astype(F32)
            o_ref[r, sl] = (y * (g * jax.nn.sigmoid(g))).astype(BF16)
        return carry

    lax.fori_loop(0, n_chunks, chunk, 0, unroll=4)


def _ret_kernel(q_ref, k_ref, v_ref, g_ref, cos_ref, sin_ref, dq_ref, dki_ref, dke_ref,
                dt_ref, nw_ref, o_ref, st_ref):
    @pl.when(pl.program_id(1) == 0)
    def _():
        st_ref[...] = jnp.zeros_like(st_ref)

    causal = _tri_mask(RET_CHUNK)
    n_chunks = q_ref.shape[0] // RET_CHUNK
    lane = lax.broadcasted_iota(jnp.int32, (RET_CHUNK, HEAD_PAD), 1)
    real = lane < HEAD_DV

    def chunk(c, carry):
        r = pl.ds(pl.multiple_of(c * RET_CHUNK, RET_CHUNK), RET_CHUNK)
        cos = cos_ref[r, :]
        sin = sin_ref[r, :]
        for h in range(N_HEADS):
            sl = slice(h * HEAD_PAD, (h + 1) * HEAD_PAD)
            qh = q_ref[r, sl].astype(F32)
            kh = k_ref[r, sl].astype(F32)
            qr = qh * cos + pltpu.roll(qh, 64, 1) * sin
            kr = (kh * cos + pltpu.roll(kh, 64, 1) * sin) * (HEAD_DK ** -0.5)
            o = _head_attention(qr * dq_ref[:, sl], kr * dki_ref[:, sl], kr * dke_ref[:, sl],
                                v_ref[r, sl], dt_ref[:, sl], st_ref, h, causal)
            mu = jnp.sum(o, axis=-1, keepdims=True) * (1.0 / HEAD_DV)
            oc = jnp.where(real, o - mu, 0.0)
            var = jnp.sum(oc * oc, axis=-1, keepdims=True) * (1.0 / HEAD_DV)
            y = (oc * lax.rsqrt(var + NORM_EPS)) * nw_ref[:, sl]
            g = g_ref[r, sl].astype(F32)
            o_ref[r, sl] = (y * (g * jax.nn.sigmoid(g))).astype(BF16)
        return carry

    lax.fori_loop(0, n_chunks, chunk, 0, unroll=2)


def _proj_spec(tl, width, col_off, per_b):
    cb = col_off // width
    return pl.BlockSpec((tl, width), lambda b, l: (b * per_b + l, cb))


def _full(shape):
    return pl.BlockSpec(shape, lambda b, l: (0,) * len(shape))


def _gla_mixer(proj, batch, seq, wa_p, ba_p, nw_p, tl=512):
    per_b = seq // tl
    return pl.pallas_call(
        _gla_kernel,
        grid=(batch, per_b),
        in_specs=[_proj_spec(tl, HP, OFF_GQ, per_b), _proj_spec(tl, HP, OFF_GK, per_b),
                  _proj_spec(tl, HP, OFF_GV, per_b), _proj_spec(tl, HP, OFF_GG, per_b),
                  _proj_spec(tl, LANES, OFF_GA, per_b),
                  _full((LANES, HP)), _full((1, HP)), _full((1, HP))],
        out_specs=pl.BlockSpec((tl, HP), lambda b, l: (b * per_b + l, 0)),
        out_shape=jax.ShapeDtypeStruct((batch * seq, HP), BF16),
        scratch_shapes=[pltpu.VMEM((N_HEADS, HEAD_PAD, HEAD_PAD), F32)],
        compiler_params=_cparams(("arbitrary", "arbitrary")),
        name="gla_mixer",
    )(proj, proj, proj, proj, proj, wa_p, ba_p, nw_p)


def _ret_mixer(proj, batch, seq, cos_t, sin_t, dq, dki, dke, dtot, nw_p, tl=512):
    per_b = seq // tl
    return pl.pallas_call(
        _ret_kernel,
        grid=(batch, per_b),
        in_specs=[_proj_spec(tl, HP, OFF_RQ, per_b), _proj_spec(tl, HP, OFF_RK, per_b),
                  _proj_spec(tl, HP, OFF_RV, per_b), _proj_spec(tl, HP, OFF_RG, per_b),
                  pl.BlockSpec((tl, HEAD_PAD), lambda b, l: (l, 0)),
                  pl.BlockSpec((tl, HEAD_PAD), lambda b, l: (l, 0)),
                  _full((RET_CHUNK, HP)), _full((RET_CHUNK, HP)), _full((RET_CHUNK, HP)),
                  _full((1, HP)), _full((1, HP))],
        out_specs=pl.BlockSpec((tl, HP), lambda b, l: (b * per_b + l, 0)),
        out_shape=jax.ShapeDtypeStruct((batch * seq, HP), BF16),
        scratch_shapes=[pltpu.VMEM((N_HEADS, HEAD_PAD, HEAD_PAD), F32)],
        compiler_params=_cparams(("arbitrary", "arbitrary")),
        name="ret_mixer",
    )(proj, proj, proj, proj, cos_t, sin_t, dq, dki, dke, dtot, nw_p)


S5_N = S5_GROUPS * S5_STATE


def _gelu_tanh(x):
    return 0.5 * x * (1.0 + jnp.tanh(np.sqrt(2.0 / np.pi) * (x + 0.044715 * (x * x * x))))


def _s5_kernel(u_ref, bb_ref, cb_ref, nr_ref, ni_ref, pr_ref, pi_ref, lr_ref, li_ref,
               d_ref, wg_ref, bg_ref, o_ref, sr_ref, si_ref, x_scr, s_scr):
    @pl.when(pl.program_id(1) == 0)
    def _():
        sr_ref[...] = jnp.zeros_like(sr_ref)
        si_ref[...] = jnp.zeros_like(si_ref)

    tri = _tri_mask().astype(BF16)
    n_chunks = u_ref.shape[0] // CHUNK
    u = u_ref[...]
    x_scr[...] = jnp.dot(u, bb_ref[...], preferred_element_type=F32)

    def chunk(c, carry):
        r = pl.ds(pl.multiple_of(c * CHUNK, CHUNK), CHUNK)
        xr, xi = x_scr[r, :S5_N], x_scr[r, S5_N:]
        nr, ni = nr_ref[...], ni_ref[...]
        p_r = jnp.dot(tri, (xr * nr - xi * ni).astype(BF16), preferred_element_type=F32)
        p_i = jnp.dot(tri, (xr * ni + xi * nr).astype(BF16), preferred_element_type=F32)
        s0r, s0i = sr_ref[...], si_ref[...]
        lr, li = lr_ref[...], li_ref[...]
        q_r = p_r + (s0r * lr - s0i * li)
        q_i = p_i + (s0r * li + s0i * lr)
        pr, pi = pr_ref[...], pi_ref[...]
        s_r = q_r * pr - q_i * pi
        s_i = q_r * pi + q_i * pr
        sr_ref[...] = s_r[CHUNK - 1:CHUNK, :]
        si_ref[...] = s_i[CHUNK - 1:CHUNK, :]
        s_scr[r, :S5_N] = s_r.astype(BF16)
        s_scr[r, S5_N:] = s_i.astype(BF16)
        return carry

    lax.fori_loop(0, n_chunks, chunk, 0, unroll=2)
    y = jnp.dot(s_scr[...], cb_ref[...], preferred_element_type=F32)
    y = _gelu_tanh(y + d_ref[...] * u.astype(F32))
    gate = jnp.dot(y.astype(BF16), wg_ref[...], preferred_element_type=F32) + bg_ref[...]
    o_ref[...] = (y * jax.nn.sigmoid(gate)).astype(BF16)


def _s5_mixer(proj, batch, seq, tabs, tl=512):
    per_b = seq // tl
    bb, cb, nr, ni, pr, pi, lr, li, dsk, wg, bg = tabs
    return pl.pallas_call(
        _s5_kernel,
        grid=(batch, per_b),
        in_specs=[_proj_spec(tl, S5_WIDTH, OFF_SU, per_b),
                  _full((S5_WIDTH, 2 * S5_N)), _full((2 * S5_N, S5_WIDTH)),
                  _full((CHUNK, S5_N)), _full((CHUNK, S5_N)),
                  _full((CHUNK, S5_N)), _full((CHUNK, S5_N)),
                  _full((1, S5_N)), _full((1, S5_N)),
                  _full((1, S5_WIDTH)), _full((S5_WIDTH, S5_WIDTH)), _full((1, S5_WIDTH))],
        out_specs=pl.BlockSpec((tl, S5_WIDTH), lambda b, l: (b * per_b + l, 0)),
        out_shape=jax.ShapeDtypeStruct((batch * seq, S5_WIDTH), BF16),
        scratch_shapes=[pltpu.VMEM((1, S5_N), F32), pltpu.VMEM((1, S5_N), F32),
                        pltpu.VMEM((tl, 2 * S5_N), F32), pltpu.VMEM((tl, 2 * S5_N), BF16)],
        compiler_params=_cparams(("arbitrary", "arbitrary")),
        name="s5_mixer",
    )(proj, bb, cb, nr, ni, pr, pi, lr, li, dsk, wg, bg)


def _s5_tables(a_re, a_im, log_dt, b_re, b_im, c_re, c_im, d_skip, w_glu, b_glu):
    lam = lax.complex(a_re, a_im)
    dt = jnp.exp(log_dt)[:, None]
    lam_bar = jnp.exp(lam * dt)
    b_bar = ((lam_bar - 1.0) / lam)[..., None] * lax.complex(b_re, b_im)
    eye = jnp.eye(S5_GROUPS, dtype=F32)
    def blk_b(m):
        return jnp.einsum('gph,gk->ghkp', m, eye).reshape(S5_WIDTH, S5_N)
    bb = jnp.concatenate([blk_b(jnp.real(b_bar)), blk_b(jnp.imag(b_bar))], axis=1)
    def blk_c(m):
        return jnp.einsum('ghp,gk->kpgh', m, eye).reshape(S5_N, S5_WIDTH)
    cb = jnp.concatenate([blk_c(c_re), blk_c(-c_im)], axis=0)
    steps = jnp.arange(CHUNK, dtype=F32)[:, None, None]
    lam_dt = (lam * dt)[None]
    pos = jnp.exp(lam_dt * steps).reshape(CHUNK, S5_N)
    neg = jnp.exp(-lam_dt * steps).reshape(CHUNK, S5_N)
    one = lam_bar.reshape(1, S5_N)
    return (bb.astype(BF16), cb.astype(BF16), jnp.real(neg), jnp.imag(neg), jnp.real(pos),
            jnp.imag(pos), jnp.real(one), jnp.imag(one), d_skip.reshape(1, S5_WIDTH),
            w_glu.astype(BF16), b_glu.reshape(1, S5_WIDTH))


def _pack_bf16_pairs(x):
    w = x.shape[1] // 2
    xb = x.astype(BF16).astype(F32)
    hi = lax.bitcast_convert_type(xb[:, :w], jnp.uint32)
    lo = lax.bitcast_convert_type(xb[:, w:], jnp.uint32)
    return hi | (lo >> 16)


def _unpack_bf16_pairs(p):
    hi = lax.bitcast_convert_type(p & jnp.uint32(0xFFFF0000), F32)
    lo = lax.bitcast_convert_type(p << 16, F32)
    return hi, lo


def _router_kernel(og_ref, or_ref, os_ref, wg_ref, wr_ref, ws_ref, x_ref, g1_ref, sc_ref,
                   sh_ref, nw_ref, rw_ref, rb_ref,
                   x1_ref, h_ref, idx_ref, gate_ref, rank_ref, cnt_ref, carry_ref):
    i = pl.program_id(0)

    @pl.when(i == 0)
    def _():
        carry_ref[...] = jnp.zeros_like(carry_ref)

    mix = (jnp.dot(og_ref[...], wg_ref[...], preferred_element_type=F32)
           + jnp.dot(or_ref[...], wr_ref[...], preferred_element_type=F32)
           + jnp.dot(os_ref[...], ws_ref[...], preferred_element_type=F32))
    x1 = x_ref[...] + g1_ref[0] * mix
    x1_ref[...] = x1
    hdn = _rms_mod(x1, nw_ref[...], sc_ref[0], sh_ref[0])
    h_ref[...] = _pack_bf16_pairs(hdn)
    h_hi = hdn.astype(BF16)
    h_lo = (hdn - h_hi.astype(F32)).astype(BF16)
    p = lax.dot_general(rw_ref[...], h_hi, _NT, preferred_element_type=F32)
    q = lax.dot_general(rw_ref[:N_EXPERTS, :], h_lo, _NT, preferred_element_type=F32)
    logits = (p[:N_EXPERTS] + p[N_EXPERTS:] + q) + rb_ref[:, 0:1]
    tm = logits.shape[1]
    eidx = lax.broadcasted_iota(jnp.int32, (N_EXPERTS, tm), 0)
    work = logits
    onehot = jnp.zeros((N_EXPERTS, tm), F32)
    vals, idxs, sels = [], [], []
    for _ in range(TOP_K):
        m = jnp.max(work, axis=0, keepdims=True)
        ix = jnp.min(jnp.where(work == m, eidx, N_EXPERTS), axis=0, keepdims=True)
        sel = eidx == ix
        work = jnp.where(sel, -jnp.inf, work)
        onehot = onehot + sel.astype(F32)
        vals.append(m)
        idxs.append(ix)
        sels.append(sel)
    exps = [jnp.exp(v - vals[0]) for v in vals]
    denom = exps[0] + exps[1] + exps[2] + exps[3]
    r = lax.broadcasted_iota(jnp.int32, (tm, tm), 0)
    c = lax.broadcasted_iota(jnp.int32, (tm, tm), 1)
    earlier = (r < c).astype(BF16)
    before = (jnp.dot(onehot.astype(BF16), earlier, preferred_element_type=F32)
              + carry_ref[:, 0:1])
    row8 = lax.broadcasted_iota(jnp.int32, (8, tm), 0)
    idx_out = jnp.zeros((8, tm), jnp.int32)
    gate_out = jnp.zeros((8, tm), F32)
    rank_out = jnp.zeros((8, tm), F32)
    for k in range(TOP_K):
        rk = jnp.sum(jnp.where(sels[k], before, 0.0), axis=0, keepdims=True)
        idx_out = jnp.where(row8 == k, idxs[k], idx_out)
        gate_out = jnp.where(row8 == k, exps[k] / denom, gate_out)
        rank_out = jnp.where(row8 == k, rk, rank_out)
    idx_ref[...] = idx_out
    gate_ref[...] = gate_out
    rank_ref[...] = rank_out.astype(jnp.int32)
    total = carry_ref[...] + jnp.sum(onehot, axis=1, keepdims=True)
    carry_ref[...] = total
    cnt_ref[...] = total


def _outproj_router(o_gla, o_ret, o_s5, wg, wr, ws, x2d, g1, sc2, sh2, nw2, rw_p, rb_p, seq,
                    tm=512):
    t, d = x2d.shape
    per_b = seq // tm
    row = lambda w: pl.BlockSpec((tm, w), lambda i: (i, 0))
    full = lambda s: pl.BlockSpec(s, lambda i: (0,) * len(s))
    per_batch = pl.BlockSpec((1, 1, d), lambda i: (i // per_b, 0, 0))
    slot_t = pl.BlockSpec((8, tm), lambda i: (0, i))
    return pl.pallas_call(
        _router_kernel,
        grid=(t // tm,),
        in_specs=[row(HP), row(HP), row(S5_WIDTH), full((HP, d)), full((HP, d)),
                  full((S5_WIDTH, d)), row(d), per_batch, per_batch, per_batch,
                  full((1, d)), full((2 * N_EXPERTS, d)), full((N_EXPERTS, LANES))],
        out_specs=[row(d), row(d // 2), slot_t, slot_t, slot_t, full((N_EXPERTS, LANES))],
        out_shape=[jax.ShapeDtypeStruct((t, d), F32),
                   jax.ShapeDtypeStruct((t, d // 2), jnp.uint32),
                   jax.ShapeDtypeStruct((8, t), jnp.int32),
                   jax.ShapeDtypeStruct((8, t), F32),
                   jax.ShapeDtypeStruct((8, t), jnp.int32),
                   jax.ShapeDtypeStruct((N_EXPERTS, LANES), F32)],
        scratch_shapes=[pltpu.VMEM((N_EXPERTS, LANES), F32)],
        compiler_params=_cparams(("arbitrary",)),
        name="outproj_router",
    )(o_gla, o_ret, o_s5, wg, wr, ws, x2d, g1, sc2, sh2, nw2, rw_p, rb_p)


GATHER_WIN = 64


def _gather_rows(table, idx):
    m = idx.shape[0]
    w = table.shape[1]
    mesh = plsc.VectorSubcoreMesh(core_axis_name="core", subcore_axis_name="subcore")

    @functools.partial(pl.kernel, out_type=jax.ShapeDtypeStruct((m, w), table.dtype),
                       mesh=mesh, name="sc_row_gather")
    def gather(x_hbm, i_hbm, o_hbm):
        def body(i_vmem, o_vmem):
            pltpu.sync_copy(x_hbm.at[i_vmem], o_vmem)

        pltpu.emit_pipeline(
            body,
            grid=(m // GATHER_WIN,),
            in_specs=[pl.BlockSpec((GATHER_WIN,), lambda i: (i,))],
            out_specs=[pl.BlockSpec((GATHER_WIN, w), lambda i: (i, 0))],
            core_axis_name=("core", "subcore"),
            dimension_semantics=(pltpu.PARALLEL,),
        )(i_hbm, o_hbm)

    return gather(table, idx)


def _scatter_rows(x, dest_slot_major, n_rows):
    t, w = x.shape
    steps = t // GATHER_WIN
    mesh = plsc.VectorSubcoreMesh(core_axis_name="core", subcore_axis_name="subcore")

    @functools.partial(pl.kernel, out_type=jax.ShapeDtypeStruct((n_rows, w), x.dtype),
                       mesh=mesh, name="sc_row_scatter")
    def scatter(x_hbm, i_hbm, o_hbm):
        def body(x_vmem, i0, i1, i2, i3):
            for i_vmem in (i0, i1, i2, i3):
                pltpu.sync_copy(x_vmem, o_hbm.at[i_vmem])

        slot = lambda k: pl.BlockSpec((GATHER_WIN,), lambda i: (k * steps + i,))
        pltpu.emit_pipeline(
            body,
            grid=(steps,),
            in_specs=[pl.BlockSpec((GATHER_WIN, w), lambda i: (i, 0)),
                      slot(0), slot(1), slot(2), slot(3)],
            out_specs=[],
            core_axis_name=("core", "subcore"),
            dimension_semantics=(pltpu.PARALLEL,),
        )(x_hbm, i_hbm, i_hbm, i_hbm, i_hbm)

    return scatter(x, dest_slot_major)


def _expert_kernel(be_ref, nv_ref, rows_ref, wu_ref, bu_ref, wd_ref, bd_ref, o_ref,
                   wu_bf, wd_bf):
    i = pl.program_id(0)
    e = be_ref[i]
    prev = be_ref[jnp.maximum(i - 1, 0)]

    @pl.when((i == 0) | (e != prev))
    def _():
        wu_bf[...] = wu_ref[0, 0].astype(BF16)
        wd_bf[...] = wd_ref[0, 0].astype(BF16)

    @pl.when(nv_ref[i] > 0)
    def _():
        row = lax.broadcasted_iota(jnp.int32, rows_ref.shape, 0)
        x_hi, x_lo = _unpack_bf16_pairs(jnp.where(row < nv_ref[i], rows_ref[...], jnp.uint32(0)))
        half = D_MODEL // 2
        up = (jnp.dot(x_hi.astype(BF16), wu_bf[:half, :], preferred_element_type=F32)
              + jnp.dot(x_lo.astype(BF16), wu_bf[half:, :], preferred_element_type=F32)
              + bu_ref[0, 0])
        x_glu = jnp.minimum(up[:, :D_FF], SWIGLU_LIMIT)
        x_lin = jnp.clip(up[:, D_FF:], -SWIGLU_LIMIT, SWIGLU_LIMIT)
        act = x_glu * jax.nn.sigmoid(SWIGLU_ALPHA * x_glu) * (x_lin + 1.0)
        o_ref[...] = _pack_bf16_pairs(
            jnp.dot(act.astype(BF16), wd_bf[...], preferred_element_type=F32) + bd_ref[0, 0])

    @pl.when(nv_ref[i] <= 0)
    def _():
        o_ref[...] = jnp.zeros_like(o_ref)


def _experts(layer, block_e, n_valid, rows, w_up, b_up, w_down, b_down):
    n_rows, dh = rows.shape
    d = 2 * dh
    n_blocks = n_rows // ROW_BLK
    depth, ne, _, f2 = w_up.shape
    wsel = lambda i, be, nu: (layer, be[i], 0, 0)
    grid_spec = pltpu.PrefetchScalarGridSpec(
        num_scalar_prefetch=2,
        grid=(n_blocks,),
        in_specs=[pl.BlockSpec((ROW_BLK, dh), lambda i, be, nu: (i, 0)),
                  pl.BlockSpec((1, 1, d, f2), wsel),
                  pl.BlockSpec((1, 1, 1, f2), wsel),
                  pl.BlockSpec((1, 1, D_FF, d), wsel),
                  pl.BlockSpec((1, 1, 1, d), wsel)],
        out_specs=pl.BlockSpec((ROW_BLK, dh), lambda i, be, nu: (i, 0)),
        scratch_shapes=[pltpu.VMEM((d, f2), BF16), pltpu.VMEM((D_FF, d), BF16)],
    )
    return pl.pallas_call(
        _expert_kernel,
        grid_spec=grid_spec,
        out_shape=jax.ShapeDtypeStruct((n_rows, dh), jnp.uint32),
        compiler_params=_cparams(("arbitrary",)),
        name="moe_experts",
    )(block_e, n_valid, rows, w_up, b_up.reshape(depth, ne, 1, f2), w_down,
      b_down.reshape(depth, ne, 1, d))


def _combine_kernel(y0_ref, y1_ref, y2_ref, y3_ref, gate_ref, x1_ref, g2_ref, fw_ref, o_ref,
                    *, final):
    gates = gate_ref[...]
    y_hi, y_lo = None, None
    for k, y_ref in enumerate((y0_ref, y1_ref, y2_ref, y3_ref)):
        hi, lo = _unpack_bf16_pairs(y_ref[...])
        g = gates[:, k:k + 1]
        y_hi = g * hi if y_hi is None else y_hi + g * hi
        y_lo = g * lo if y_lo is None else y_lo + g * lo
    y = jnp.concatenate([y_hi, y_lo], axis=1)
    x2 = x1_ref[...] + g2_ref[0] * y
    if final:
        x2 = (x2 * lax.rsqrt(jnp.mean(x2 * x2, axis=-1, keepdims=True) + NORM_EPS)) * fw_ref[...]
    o_ref[...] = x2


def _combine(y4, gates, x1, g2, fw, seq, final, th=256):
    t, d = x1.shape
    steps = t // th
    per_b = seq // th
    slot = lambda k: pl.BlockSpec((th, d // 2), lambda i: (k * steps + i, 0))
    return pl.pallas_call(
        functools.partial(_combine_kernel, final=final),
        grid=(steps,),
        in_specs=[slot(0), slot(1), slot(2), slot(3),
                  pl.BlockSpec((th, LANES), lambda i: (i, 0)),
                  pl.BlockSpec((th, d), lambda i: (i, 0)),
                  pl.BlockSpec((1, 1, d), lambda i: (i // per_b, 0, 0)),
                  pl.BlockSpec((1, d), lambda i: (0, 0))],
        out_specs=pl.BlockSpec((th, d), lambda i: (i, 0)),
        out_shape=jax.ShapeDtypeStruct((t, d), F32),
        compiler_params=_cparams(("arbitrary",)),
        name="moe_combine",
    )(y4, y4, y4, y4, gates, x1, g2, fw)


def _retention_tables(seq):
    pos = jnp.arange(seq, dtype=F32)
    inv_freq = ROPE_BASE ** (-jnp.arange(0, HEAD_DK, 2, dtype=F32) / HEAD_DK)
    ang = pos[:, None] * inv_freq[None, :]
    cos, sin = jnp.cos(ang), jnp.sin(ang)
    half = HEAD_DK // 2
    zpad = jnp.zeros((seq, 64 - half), F32)
    cos_t = jnp.concatenate([cos, zpad, cos, zpad], axis=1)
    sin_t = jnp.concatenate([-sin, zpad, sin, zpad], axis=1)
    log_gamma = jnp.log1p(-jnp.exp2(-5.0 - jnp.arange(N_HEADS, dtype=F32)))
    log_decay = jnp.broadcast_to(log_gamma[None, :, None], (RET_CHUNK, N_HEADS, HEAD_PAD))
    cum = jnp.cumsum(log_decay, axis=0)
    tot = cum[-1:]
    shp = lambda a: a.reshape(a.shape[0], HP)
    return (cos_t, sin_t, shp(jnp.exp(cum)), shp(jnp.exp(-cum)), shp(jnp.exp(tot - cum)),
            shp(jnp.exp(tot)))


def kernel(x, c, norm1_w, norm2_w, w_mod, b_mod, w_in, gla_w_a2, gla_b_a, gla_norm_w, ret_norm_w, s5_a_re, s5_a_im, s5_log_dt, s5_b_re, s5_b_im, s5_c_re, s5_c_im, s5_d, s5_w_glu, s5_b_glu, w_out, router_w, router_b, w_up, b_up, w_down, b_down, final_norm_w):
    batch, seq, d = x.shape
    depth = w_mod.shape[0]
    t = batch * seq
    n_slots = t * TOP_K
    n_blocks = n_slots // ROW_BLK + N_EXPERTS
    n_rows = n_blocks * ROW_BLK

    mod = _modulation(c, w_mod, b_mod)
    ret_tabs = _retention_tables(seq)
    x2d = x.reshape(t, d)

    for i in range(depth):
        m6 = mod[i].reshape(batch, 6, 1, d)
        sh1, sc1, g1, sh2, sc2, g2 = (m6[:, j] for j in range(6))

        w_p = _take_cols(w_in[i], _IN_SRC).astype(BF16)
        wa_p = jnp.zeros((LANES, HP), F32).at[:GATE_RANK].set(_take_cols(gla_w_a2[i], _DK_SRC))
        ba_p = _take_cols(gla_b_a[i], _DK_SRC).reshape(1, HP)
        gnw = _take_cols(gla_norm_w[i], _DV_SRC).reshape(1, HP)
        rnw = _take_cols(ret_norm_w[i], _DV_SRC).reshape(1, HP)
        kv = N_HEADS * HEAD_DV
        wo_g = _take_rows(w_out[i, :kv], _DV_SRC).astype(BF16)
        wo_r = _take_rows(w_out[i, kv:2 * kv], _DV_SRC).astype(BF16)
        wo_s = w_out[i, 2 * kv:].astype(BF16)
        rw_t = router_w[i].T
        rw_hi = rw_t.astype(BF16)
        rw_p = jnp.concatenate([rw_hi, (rw_t - rw_hi.astype(F32)).astype(BF16)], axis=0)
        rb_p = jnp.broadcast_to(router_b[i][:, None], (N_EXPERTS, LANES))

        proj = _in_projection(x2d, sc1, sh1, norm1_w[i].reshape(1, d), w_p, seq)
        o_gla = _gla_mixer(proj, batch, seq, wa_p.astype(BF16), ba_p, gnw)
        o_ret = _ret_mixer(proj, batch, seq, *ret_tabs, rnw)
        s5_tabs = _s5_tables(s5_a_re[i], s5_a_im[i], s5_log_dt[i], s5_b_re[i], s5_b_im[i],
                             s5_c_re[i], s5_c_im[i], s5_d[i], s5_w_glu[i], s5_b_glu[i])
        o_s5 = _s5_mixer(proj, batch, seq, s5_tabs)

        x1, hdn, idx, gates, rank, counts = _outproj_router(
            o_gla, o_ret, o_s5, wo_g, wo_r, wo_s, x2d, g1, sc2, sh2,
            norm2_w[i].reshape(1, d), rw_p, rb_p, seq)

        cnt = counts[:, 0].astype(jnp.int32)
        padded = (cnt + ROW_BLK - 1) // ROW_BLK * ROW_BLK
        pad_ends = jnp.cumsum(padded)
        pad_starts = pad_ends - padded
        dest_sm = (pad_starts[idx[:TOP_K]] + rank[:TOP_K]).astype(jnp.int32).reshape(-1)
        gates_tm = jnp.pad(gates[:TOP_K].T, ((0, 0), (0, LANES - TOP_K)))
        blk_start = jnp.arange(n_blocks, dtype=jnp.int32) * ROW_BLK
        block_e = jnp.minimum(jnp.sum(pad_ends[None, :] <= blk_start[:, None], axis=1),
                              N_EXPERTS - 1).astype(jnp.int32)
        n_valid = jnp.clip((pad_starts + cnt)[block_e] - blk_start, 0, ROW_BLK).astype(jnp.int32)

        rows = _scatter_rows(hdn, dest_sm, n_rows)
        out_rows = _experts(i, block_e, n_valid, rows, w_up, b_up, w_down, b_down)
        y4 = _gather_rows(out_rows, dest_sm)
        x2d = _combine(y4, gates_tm, x1, g2, final_norm_w.reshape(1, d), seq,
                       final=(i == depth - 1))

    return x2d.reshape(batch, seq, d)
```

```python
import functools

import numpy as np
import jax
import jax.numpy as jnp
from jax import lax
from jax.experimental import pallas as pl
from jax.experimental.pallas import tpu as pltpu
from jax.experimental.pallas import tpu_sc as plsc

D_MODEL = 1024
CHUNK = 64
RET_CHUNK = 128
NORM_EPS = 1e-5
N_HEADS = 4
HEAD_DK = 48
HEAD_DV = 96
GATE_RANK = 16
GATE_TEMP = 16.0
ROPE_BASE = 10000.0
S5_WIDTH = 256
S5_GROUP_DIM = 16
S5_GROUPS = 16
S5_STATE = 64
N_EXPERTS = 32
TOP_K = 4
D_FF = 1024
SWIGLU_LIMIT = 7.0
SWIGLU_ALPHA = 1.702

LANES = 128
HEAD_PAD = LANES
HP = N_HEADS * HEAD_PAD
VMEM_LIMIT = 56 * 1024 * 1024

OFF_GQ, OFF_GK, OFF_GV, OFF_GG = 0, HP, 2 * HP, 3 * HP
OFF_RQ, OFF_RK, OFF_RV, OFF_RG = 4 * HP, 5 * HP, 6 * HP, 7 * HP
OFF_SU = 8 * HP
OFF_GA = OFF_SU + S5_WIDTH
NP_COLS = OFF_GA + LANES
PROJ_CH = 896

ROW_BLK = 512

F32 = jnp.float32
BF16 = jnp.bfloat16


def _in_col_map():
    src = -np.ones((NP_COLS,), np.int64)
    kq = N_HEADS * HEAD_DK
    kv = N_HEADS * HEAD_DV
    base = dict(gq=0, gk=kq, gv=2 * kq, gg=2 * kq + kv, ga=2 * kq + 2 * kv)
    r0 = base['ga'] + GATE_RANK
    base.update(rq=r0, rk=r0 + kq, rv=r0 + 2 * kq, rg=r0 + 2 * kq + kv, su=r0 + 2 * kq + 2 * kv)
    half = HEAD_DK // 2
    for h in range(N_HEADS):
        for d in range(HEAD_DK):
            src[OFF_GQ + h * HEAD_PAD + d] = base['gq'] + h * HEAD_DK + d
            src[OFF_GK + h * HEAD_PAD + d] = base['gk'] + h * HEAD_DK + d
            lane = d if d < half else 64 + (d - half)
            src[OFF_RQ + h * HEAD_PAD + lane] = base['rq'] + h * HEAD_DK + d
            src[OFF_RK + h * HEAD_PAD + lane] = base['rk'] + h * HEAD_DK + d
        for d in range(HEAD_DV):
            src[OFF_GV + h * HEAD_PAD + d] = base['gv'] + h * HEAD_DV + d
            src[OFF_GG + h * HEAD_PAD + d] = base['gg'] + h * HEAD_DV + d
            src[OFF_RV + h * HEAD_PAD + d] = base['rv'] + h * HEAD_DV + d
            src[OFF_RG + h * HEAD_PAD + d] = base['rg'] + h * HEAD_DV + d
    src[OFF_SU:OFF_SU + S5_WIDTH] = base['su'] + np.arange(S5_WIDTH)
    src[OFF_GA:OFF_GA + GATE_RANK] = base['ga'] + np.arange(GATE_RANK)
    return src


_IN_SRC = _in_col_map()


def _head_pad_map(width):
    src = -np.ones((HP,), np.int64)
    for h in range(N_HEADS):
        src[h * HEAD_PAD:h * HEAD_PAD + width] = h * width + np.arange(width)
    return src


_DV_SRC = _head_pad_map(HEAD_DV)
_DK_SRC = _head_pad_map(HEAD_DK)


def _take_cols(w, src):
    out = jnp.take(w, jnp.asarray(np.maximum(src, 0)), axis=-1)
    return jnp.where(jnp.asarray(src >= 0), out, 0)


def _take_rows(w, src):
    out = jnp.take(w, jnp.asarray(np.maximum(src, 0)), axis=0)
    return jnp.where(jnp.asarray(src >= 0)[:, None], out, 0)


def _cparams(sem):
    return pltpu.CompilerParams(dimension_semantics=sem, vmem_limit_bytes=VMEM_LIMIT)


def _mod_kernel(c_ref, w_ref, b_ref, o_ref):
    c = c_ref[...]
    cond = c * jax.nn.sigmoid(c)
    o_ref[0] = jnp.dot(cond, w_ref[0], preferred_element_type=F32,
                       precision=lax.Precision.HIGHEST) + b_ref[0]


def _modulation(c, w_mod, b_mod):
    depth, d, n = w_mod.shape
    b = c.shape[0]
    nb = 1536
    return pl.pallas_call(
        _mod_kernel,
        grid=(depth, n // nb),
        in_specs=[pl.BlockSpec((b, d), lambda l, j: (0, 0)),
                  pl.BlockSpec((1, d, nb), lambda l, j: (l, 0, j)),
                  pl.BlockSpec((1, 1, nb), lambda l, j: (l, 0, j))],
        out_specs=pl.BlockSpec((1, b, nb), lambda l, j: (l, 0, j)),
        out_shape=jax.ShapeDtypeStruct((depth, b, n), F32),
        compiler_params=_cparams(("arbitrary", "arbitrary")),
        name="adaln_mod",
    )(c, w_mod, b_mod.reshape(depth, 1, n))


def _rms_mod(x, nw, sc, sh):
    y = x * lax.rsqrt(jnp.mean(x * x, axis=-1, keepdims=True) + NORM_EPS)
    return (y * nw) * (1.0 + sc) + sh


def _inproj_kernel(x_ref, sc_ref, sh_ref, nw_ref, w_ref, o_ref):
    h = _rms_mod(x_ref[...], nw_ref[...], sc_ref[0], sh_ref[0]).astype(BF16)
    for j in range(NP_COLS // PROJ_CH):
        cs = slice(j * PROJ_CH, (j + 1) * PROJ_CH)
        o_ref[:, cs] = jnp.dot(h, w_ref[:, cs], preferred_element_type=F32).astype(BF16)


def _in_projection(x2d, sc, sh, nw, w_p, seq, tm=512):
    t, d = x2d.shape
    per_b = seq // tm
    return pl.pallas_call(
        _inproj_kernel,
        grid=(t // tm,),
        in_specs=[pl.BlockSpec((tm, d), lambda i: (i, 0)),
                  pl.BlockSpec((1, 1, d), lambda i: (i // per_b, 0, 0)),
                  pl.BlockSpec((1, 1, d), lambda i: (i // per_b, 0, 0)),
                  pl.BlockSpec((1, d), lambda i: (0, 0)),
                  pl.BlockSpec((d, NP_COLS), lambda i: (0, 0))],
        out_specs=pl.BlockSpec((tm, NP_COLS), lambda i: (i, 0)),
        out_shape=jax.ShapeDtypeStruct((t, NP_COLS), BF16),
        compiler_params=_cparams(("arbitrary",)),
        name="in_proj",
    )(x2d, sc, sh, nw, w_p)


_NT = (((1,), (1,)), ((), ()))
_TN = (((0,), (0,)), ((), ()))


def _tri_mask(n=CHUNK):
    r = lax.broadcasted_iota(jnp.int32, (n, n), 0)
    c = lax.broadcasted_iota(jnp.int32, (n, n), 1)
    return r >= c


def _head_attention(qd, ki, ke, vh, et, st_ref, h, causal):
    qb = qd.astype(BF16)
    sc = lax.dot_general(qb, ki.astype(BF16), _NT, preferred_element_type=F32)
    sc = jnp.where(causal, sc, 0.0)
    st = st_ref[h]
    o = jnp.dot(sc.astype(BF16), vh, preferred_element_type=F32)
    o = o + lax.dot_general(qb, st.astype(BF16), _NT, preferred_element_type=F32)
    st_ref[h] = st * et + lax.dot_general(vh, ke.astype(BF16), _TN, preferred_element_type=F32)
    return o


def _gla_kernel(q_ref, k_ref, v_ref, g_ref, a_ref, wa_ref, ba_ref, nw_ref, o_ref, st_ref):
    @pl.when(pl.program_id(1) == 0)
    def _():
        st_ref[...] = jnp.zeros_like(st_ref)

    causal = _tri_mask()
    tri = causal.astype(BF16)
    n_chunks = q_ref.shape[0] // CHUNK

    def chunk(c, carry):
        r = pl.ds(pl.multiple_of(c * CHUNK, CHUNK), CHUNK)
        z = jnp.dot(a_ref[r, :], wa_ref[...], preferred_element_type=F32) + ba_ref[...]
        la = (jnp.minimum(z, 0.0) - jnp.log1p(jnp.exp(-jnp.abs(z)))) * (1.0 / GATE_TEMP)
        hi = la.astype(BF16)
        lo = (la - hi.astype(F32)).astype(BF16)
        cum = (jnp.dot(tri, hi, preferred_element_type=F32)
               + jnp.dot(tri, lo, preferred_element_type=F32))
        tot = cum[CHUNK - 1:CHUNK, :]
        qd = (q_ref[r, :].astype(F32) * (HEAD_DK ** -0.5)) * jnp.exp(cum)
        kf = k_ref[r, :].astype(F32)
        ki = kf * jnp.exp(-cum)
        ke = kf * jnp.exp(tot - cum)
        et = jnp.exp(tot)
        for h in range(N_HEADS):
            sl = slice(h * HEAD_PAD, (h + 1) * HEAD_PAD)
            o = _head_attention(qd[:, sl], ki[:, sl], ke[:, sl], v_ref[r, sl], et[:, sl],
                                st_ref, h, causal)
            ms = jnp.sum(o * o, axis=-1, keepdims=True) * (1.0 / HEAD_DV)
            y = (o * lax.rsqrt(ms + NORM_EPS)) * nw_ref[:, sl]
            g = g_ref[r, sl].astype(F32)
            o_ref[r, sl] = (y * (g * jax.nn.sigmoid(g))).astype(BF16)
        return carry

    lax.fori_loop(0, n_chunks, chunk, 0, unroll=4)


def _ret_kernel(q_ref, k_ref, v_ref, g_ref, cos_ref, sin_ref, dq_ref, dki_ref, dke_ref,
                dt_ref, nw_ref, o_ref, st_ref):
    @pl.when(pl.program_id(1) == 0)
    def _():
        st_ref[...] = jnp.zeros_like(st_ref)

    causal = _tri_mask(RET_CHUNK)
    n_chunks = q_ref.shape[0] // RET_CHUNK
    lane = lax.broadcasted_iota(jnp.int32, (RET_CHUNK, HEAD_PAD), 1)
    real = lane < HEAD_DV

    def chunk(c, carry):
        r = pl.ds(pl.multiple_of(c * RET_CHUNK, RET_CHUNK), RET_CHUNK)
        cos = cos_ref[r, :]
        sin = sin_ref[r, :]
        for h in range(N_HEADS):
            sl = slice(h * HEAD_PAD, (h + 1) * HEAD_PAD)
            qh = q_ref[r, sl].astype(F32)
            kh = k_ref[r, sl].astype(F32)
            qr = qh * cos + pltpu.roll(qh, 64, 1) * sin
            kr = (kh * cos + pltpu.roll(kh, 64, 1) * sin) * (HEAD_DK ** -0.5)
            o = _head_attention(qr * dq_ref[:, sl], kr * dki_ref[:, sl], kr * dke_ref[:, sl],
                                v_ref[r, sl], dt_ref[:, sl], st_ref, h, causal)
            mu = jnp.sum(o, axis=-1, keepdims=True) * (1.0 / HEAD_DV)
            oc = jnp.where(real, o - mu, 0.0)
            var = jnp.sum(oc * oc, axis=-1, keepdims=True) * (1.0 / HEAD_DV)
            y = (oc * lax.rsqrt(var + NORM_EPS)) * nw_ref[:, sl]
            g = g_ref[r, sl].astype(F32)
            o_ref[r, sl] = (y * (g * jax.nn.sigmoid(g))).astype(BF16)
        return carry

    lax.fori_loop(0, n_chunks, chunk, 0, unroll=2)


def _proj_spec(tl, width, col_off, per_b):
    cb = col_off // width
    return pl.BlockSpec((tl, width), lambda b, l: (b * per_b + l, cb))


def _full(shape):
    return pl.BlockSpec(shape, lambda b, l: (0,) * len(shape))


def _gla_mixer(proj, batch, seq, wa_p, ba_p, nw_p, tl=512):
    per_b = seq // tl
    return pl.pallas_call(
        _gla_kernel,
        grid=(batch, per_b),
        in_specs=[_proj_spec(tl, HP, OFF_GQ, per_b), _proj_spec(tl, HP, OFF_GK, per_b),
                  _proj_spec(tl, HP, OFF_GV, per_b), _proj_spec(tl, HP, OFF_GG, per_b),
                  _proj_spec(tl, LANES, OFF_GA, per_b),
                  _full((LANES, HP)), _full((1, HP)), _full((1, HP))],
        out_specs=pl.BlockSpec((tl, HP), lambda b, l: (b * per_b + l, 0)),
        out_shape=jax.ShapeDtypeStruct((batch * seq, HP), BF16),
        scratch_shapes=[pltpu.VMEM((N_HEADS, HEAD_PAD, HEAD_PAD), F32)],
        compiler_params=_cparams(("arbitrary", "arbitrary")),
        name="gla_mixer",
    )(proj, proj, proj, proj, proj, wa_p, ba_p, nw_p)


def _ret_mixer(proj, batch, seq, cos_t, sin_t, dq, dki, dke, dtot, nw_p, tl=512):
    per_b = seq // tl
    return pl.pallas_call(
        _ret_kernel,
        grid=(batch, per_b),
        in_specs=[_proj_spec(tl, HP, OFF_RQ, per_b), _proj_spec(tl, HP, OFF_RK, per_b),
                  _proj_spec(tl, HP, OFF_RV, per_b), _proj_spec(tl, HP, OFF_RG, per_b),
                  pl.BlockSpec((tl, HEAD_PAD), lambda b, l: (l, 0)),
                  pl.BlockSpec((tl, HEAD_PAD), lambda b, l: (l, 0)),
                  _full((RET_CHUNK, HP)), _full((RET_CHUNK, HP)), _full((RET_CHUNK, HP)),
                  _full((1, HP)), _full((1, HP))],
        out_specs=pl.BlockSpec((tl, HP), lambda b, l: (b * per_b + l, 0)),
        out_shape=jax.ShapeDtypeStruct((batch * seq, HP), BF16),
        scratch_shapes=[pltpu.VMEM((N_HEADS, HEAD_PAD, HEAD_PAD), F32)],
        compiler_params=_cparams(("arbitrary", "arbitrary")),
        name="ret_mixer",
    )(proj, proj, proj, proj, cos_t, sin_t, dq, dki, dke, dtot, nw_p)


S5_N = S5_GROUPS * S5_STATE


def _gelu_tanh(x):
    return 0.5 * x * (1.0 + jnp.tanh(np.sqrt(2.0 / np.pi) * (x + 0.044715 * (x * x * x))))


def _s5_kernel(u_ref, bb_ref, cb_ref, nr_ref, ni_ref, pr_ref, pi_ref, lr_ref, li_ref,
               d_ref, wg_ref, bg_ref, o_ref, sr_ref, si_ref, x_scr, s_scr):
    @pl.when(pl.program_id(1) == 0)
    def _():
        sr_ref[...] = jnp.zeros_like(sr_ref)
        si_ref[...] = jnp.zeros_like(si_ref)

    tri = _tri_mask().astype(BF16)
    n_chunks = u_ref.shape[0] // CHUNK
    u = u_ref[...]
    x_scr[...] = jnp.dot(u, bb_ref[...], preferred_element_type=F32)

    def chunk(c, carry):
        r = pl.ds(pl.multiple_of(c * CHUNK, CHUNK), CHUNK)
        xr, xi = x_scr[r, :S5_N], x_scr[r, S5_N:]
        nr, ni = nr_ref[...], ni_ref[...]
        p_r = jnp.dot(tri, (xr * nr - xi * ni).astype(BF16), preferred_element_type=F32)
        p_i = jnp.dot(tri, (xr * ni + xi * nr).astype(BF16), preferred_element_type=F32)
        s0r, s0i = sr_ref[...], si_ref[...]
        lr, li = lr_ref[...], li_ref[...]
        q_r = p_r + (s0r * lr - s0i * li)
        q_i = p_i + (s0r * li + s0i * lr)
        pr, pi = pr_ref[...], pi_ref[...]
        s_r = q_r * pr - q_i * pi
        s_i = q_r * pi + q_i * pr
        sr_ref[...] = s_r[CHUNK - 1:CHUNK, :]
        si_ref[...] = s_i[CHUNK - 1:CHUNK, :]
        s_scr[r, :S5_N] = s_r.astype(BF16)
        s_scr[r, S5_N:] = s_i.astype(BF16)
        return carry

    lax.fori_loop(0, n_chunks, chunk, 0, unroll=2)
    y = jnp.dot(s_scr[...], cb_ref[...], preferred_element_type=F32)
    y = _gelu_tanh(y + d_ref[...] * u.astype(F32))
    gate = jnp.dot(y.astype(BF16), wg_ref[...], preferred_element_type=F32) + bg_ref[...]
    o_ref[...] = (y * jax.nn.sigmoid(gate)).astype(BF16)


def _s5_mixer(proj, batch, seq, tabs, tl=512):
    per_b = seq // tl
    bb, cb, nr, ni, pr, pi, lr, li, dsk, wg, bg = tabs
    return pl.pallas_call(
        _s5_kernel,
        grid=(batch, per_b),
        in_specs=[_proj_spec(tl, S5_WIDTH, OFF_SU, per_b),
                  _full((S5_WIDTH, 2 * S5_N)), _full((2 * S5_N, S5_WIDTH)),
                  _full((CHUNK, S5_N)), _full((CHUNK, S5_N)),
                  _full((CHUNK, S5_N)), _full((CHUNK, S5_N)),
                  _full((1, S5_N)), _full((1, S5_N)),
                  _full((1, S5_WIDTH)), _full((S5_WIDTH, S5_WIDTH)), _full((1, S5_WIDTH))],
        out_specs=pl.BlockSpec((tl, S5_WIDTH), lambda b, l: (b * per_b + l, 0)),
        out_shape=jax.ShapeDtypeStruct((batch * seq, S5_WIDTH), BF16),
        scratch_shapes=[pltpu.VMEM((1, S5_N), F32), pltpu.VMEM((1, S5_N), F32),
                        pltpu.VMEM((tl, 2 * S5_N), F32), pltpu.VMEM((tl, 2 * S5_N), BF16)],
        compiler_params=_cparams(("arbitrary", "arbitrary")),
        name="s5_mixer",
    )(proj, bb, cb, nr, ni, pr, pi, lr, li, dsk, wg, bg)


def _s5_tables(a_re, a_im, log_dt, b_re, b_im, c_re, c_im, d_skip, w_glu, b_glu):
    lam = lax.complex(a_re, a_im)
    dt = jnp.exp(log_dt)[:, None]
    lam_bar = jnp.exp(lam * dt)
    b_bar = ((lam_bar - 1.0) / lam)[..., None] * lax.complex(b_re, b_im)
    eye = jnp.eye(S5_GROUPS, dtype=F32)
    def blk_b(m):
        return jnp.einsum('gph,gk->ghkp', m, eye).reshape(S5_WIDTH, S5_N)
    bb = jnp.concatenate([blk_b(jnp.real(b_bar)), blk_b(jnp.imag(b_bar))], axis=1)
    def blk_c(m):
        return jnp.einsum('ghp,gk->kpgh', m, eye).reshape(S5_N, S5_WIDTH)
    cb = jnp.concatenate([blk_c(c_re), blk_c(-c_im)], axis=0)
    steps = jnp.arange(CHUNK, dtype=F32)[:, None, None]
    lam_dt = (lam * dt)[None]
    pos = jnp.exp(lam_dt * steps).reshape(CHUNK, S5_N)
    neg = jnp.exp(-lam_dt * steps).reshape(CHUNK, S5_N)
    one = lam_bar.reshape(1, S5_N)
    return (bb.astype(BF16), cb.astype(BF16), jnp.real(neg), jnp.imag(neg), jnp.real(pos),
            jnp.imag(pos), jnp.real(one), jnp.imag(one), d_skip.reshape(1, S5_WIDTH),
            w_glu.astype(BF16), b_glu.reshape(1, S5_WIDTH))


def _pack_bf16_pairs(x):
    w = x.shape[1] // 2
    xb = x.astype(BF16).astype(F32)
    hi = lax.bitcast_convert_type(xb[:, :w], jnp.uint32)
    lo = lax.bitcast_convert_type(xb[:, w:], jnp.uint32)
    return hi | (lo >> 16)


def _unpack_bf16_pairs(p):
    hi = lax.bitcast_convert_type(p & jnp.uint32(0xFFFF0000), F32)
    lo = lax.bitcast_convert_type(p << 16, F32)
    return hi, lo


def _router_kernel(og_ref, or_ref, os_ref, wg_ref, wr_ref, ws_ref, x_ref, g1_ref, sc_ref,
                   sh_ref, nw_ref, rw_ref, rb_ref,
                   x1_ref, h_ref, idx_ref, gate_ref, rank_ref, cnt_ref, carry_ref):
    i = pl.program_id(0)

    @pl.when(i == 0)
    def _():
        carry_ref[...] = jnp.zeros_like(carry_ref)

    mix = (jnp.dot(og_ref[...], wg_ref[...], preferred_element_type=F32)
           + jnp.dot(or_ref[...], wr_ref[...], preferred_element_type=F32)
           + jnp.dot(os_ref[...], ws_ref[...], preferred_element_type=F32))
    x1 = x_ref[...] + g1_ref[0] * mix
    x1_ref[...] = x1
    hdn = _rms_mod(x1, nw_ref[...], sc_ref[0], sh_ref[0])
    h_ref[...] = _pack_bf16_pairs(hdn)
    h_hi = hdn.astype(BF16)
    h_lo = (hdn - h_hi.astype(F32)).astype(BF16)
    p = lax.dot_general(rw_ref[...], h_hi, _NT, preferred_element_type=F32)
    q = lax.dot_general(rw_ref[:N_EXPERTS, :], h_lo, _NT, preferred_element_type=F32)
    logits = (p[:N_EXPERTS] + p[N_EXPERTS:] + q) + rb_ref[:, 0:1]
    tm = logits.shape[1]
    eidx = lax.broadcasted_iota(jnp.int32, (N_EXPERTS, tm), 0)
    work = logits
    onehot = jnp.zeros((N_EXPERTS, tm), F32)
    vals, idxs, sels = [], [], []
    for _ in range(TOP_K):
        m = jnp.max(work, axis=0, keepdims=True)
        ix = jnp.min(jnp.where(work == m, eidx, N_EXPERTS), axis=0, keepdims=True)
        sel = eidx == ix
        work = jnp.where(sel, -jnp.inf, work)
        onehot = onehot + sel.astype(F32)
        vals.append(m)
        idxs.append(ix)
        sels.append(sel)
    exps = [jnp.exp(v - vals[0]) for v in vals]
    denom = exps[0] + exps[1] + exps[2] + exps[3]
    r = lax.broadcasted_iota(jnp.int32, (tm, tm), 0)
    c = lax.broadcasted_iota(jnp.int32, (tm, tm), 1)
    earlier = (r < c).astype(BF16)
    before = (jnp.dot(onehot.astype(BF16), earlier, preferred_element_type=F32)
              + carry_ref[:, 0:1])
    row8 = lax.broadcasted_iota(jnp.int32, (8, tm), 0)
    idx_out = jnp.zeros((8, tm), jnp.int32)
    gate_out = jnp.zeros((8, tm), F32)
    rank_out = jnp.zeros((8, tm), F32)
    for k in range(TOP_K):
        rk = jnp.sum(jnp.where(sels[k], before, 0.0), axis=0, keepdims=True)
        idx_out = jnp.where(row8 == k, idxs[k], idx_out)
        gate_out = jnp.where(row8 == k, exps[k] / denom, gate_out)
        rank_out = jnp.where(row8 == k, rk, rank_out)
    idx_ref[...] = idx_out
    gate_ref[...] = gate_out
    rank_ref[...] = rank_out.astype(jnp.int32)
    total = carry_ref[...] + jnp.sum(onehot, axis=1, keepdims=True)
    carry_ref[...] = total
    cnt_ref[...] = total


def _outproj_router(o_gla, o_ret, o_s5, wg, wr, ws, x2d, g1, sc2, sh2, nw2, rw_p, rb_p, seq,
                    tm=512):
    t, d = x2d.shape
    per_b = seq // tm
    row = lambda w: pl.BlockSpec((tm, w), lambda i: (i, 0))
    full = lambda s: pl.BlockSpec(s, lambda i: (0,) * len(s))
    per_batch = pl.BlockSpec((1, 1, d), lambda i: (i // per_b, 0, 0))
    slot_t = pl.BlockSpec((8, tm), lambda i: (0, i))
    return pl.pallas_call(
        _router_kernel,
        grid=(t // tm,),
        in_specs=[row(HP), row(HP), row(S5_WIDTH), full((HP, d)), full((HP, d)),
                  full((S5_WIDTH, d)), row(d), per_batch, per_batch, per_batch,
                  full((1, d)), full((2 * N_EXPERTS, d)), full((N_EXPERTS, LANES))],
        out_specs=[row(d), row(d // 2), slot_t, slot_t, slot_t, full((N_EXPERTS, LANES))],
        out_shape=[jax.ShapeDtypeStruct((t, d), F32),
                   jax.ShapeDtypeStruct((t, d // 2), jnp.uint32),
                   jax.ShapeDtypeStruct((8, t), jnp.int32),
                   jax.ShapeDtypeStruct((8, t), F32),
                   jax.ShapeDtypeStruct((8, t), jnp.int32),
                   jax.ShapeDtypeStruct((N_EXPERTS, LANES), F32)],
        scratch_shapes=[pltpu.VMEM((N_EXPERTS, LANES), F32)],
        compiler_params=_cparams(("arbitrary",)),
        name="outproj_router",
    )(o_gla, o_ret, o_s5, wg, wr, ws, x2d, g1, sc2, sh2, nw2, rw_p, rb_p)


GATHER_WIN = 64


def _gather_rows(table, idx):
    m = idx.shape[0]
    w = table.shape[1]
    mesh = plsc.VectorSubcoreMesh(core_axis_name="core", subcore_axis_name="subcore")

    @functools.partial(pl.kernel, out_type=jax.ShapeDtypeStruct((m, w), table.dtype),
                       mesh=mesh, name="sc_row_gather")
    def gather(x_hbm, i_hbm, o_hbm):
        def body(i_vmem, o_vmem):
            pltpu.sync_copy(x_hbm.at[i_vmem], o_vmem)

        pltpu.emit_pipeline(
            body,
            grid=(m // GATHER_WIN,),
            in_specs=[pl.BlockSpec((GATHER_WIN,), lambda i: (i,))],
            out_specs=[pl.BlockSpec((GATHER_WIN, w), lambda i: (i, 0))],
            core_axis_name=("core", "subcore"),
            dimension_semantics=(pltpu.PARALLEL,),
        )(i_hbm, o_hbm)

    return gather(table, idx)


def _scatter_rows(x, dest_slot_major, n_rows):
    t, w = x.shape
    steps = t // GATHER_WIN
    mesh = plsc.VectorSubcoreMesh(core_axis_name="core", subcore_axis_name="subcore")

    @functools.partial(pl.kernel, out_type=jax.ShapeDtypeStruct((n_rows, w), x.dtype),
                       mesh=mesh, name="sc_row_scatter")
    def scatter(x_hbm, i_hbm, o_hbm):
        def body(x_vmem, i0, i1, i2, i3):
            for i_vmem in (i0, i1, i2, i3):
                pltpu.sync_copy(x_vmem, o_hbm.at[i_vmem])

        slot = lambda k: pl.BlockSpec((GATHER_WIN,), lambda i: (k * steps + i,))
        pltpu.emit_pipeline(
            body,
            grid=(steps,),
            in_specs=[pl.BlockSpec((GATHER_WIN, w), lambda i: (i, 0)),
                      slot(0), slot(1), slot(2), slot(3)],
            out_specs=[],
            core_axis_name=("core", "subcore"),
            dimension_semantics=(pltpu.PARALLEL,),
        )(x_hbm, i_hbm, i_hbm, i_hbm, i_hbm)

    return scatter(x, dest_slot_major)


def _expert_kernel(be_ref, nv_ref, rows_ref, wu_ref, bu_ref, wd_ref, bd_ref, o_ref,
                   wu_bf, wd_bf):
    i = pl.program_id(0)
    e = be_ref[i]
    prev = be_ref[jnp.maximum(i - 1, 0)]

    @pl.when((i == 0) | (e != prev))
    def _():
        wu_bf[...] = wu_ref[0, 0].astype(BF16)
        wd_bf[...] = wd_ref[0, 0].astype(BF16)

    @pl.when(nv_ref[i] > 0)
    def _():
        row = lax.broadcasted_iota(jnp.int32, rows_ref.shape, 0)
        x_hi, x_lo = _unpack_bf16_pairs(jnp.where(row < nv_ref[i], rows_ref[...], jnp.uint32(0)))
        half = D_MODEL // 2
        up = (jnp.dot(x_hi.astype(BF16), wu_bf[:half, :], preferred_element_type=F32)
              + jnp.dot(x_lo.astype(BF16), wu_bf[half:, :], preferred_element_type=F32)
              + bu_ref[0, 0])
        x_glu = jnp.minimum(up[:, :D_FF], SWIGLU_LIMIT)
        x_lin = jnp.clip(up[:, D_FF:], -SWIGLU_LIMIT, SWIGLU_LIMIT)
        act = x_glu * jax.nn.sigmoid(SWIGLU_ALPHA * x_glu) * (x_lin + 1.0)
        o_ref[...] = _pack_bf16_pairs(
            jnp.dot(act.astype(BF16), wd_bf[...], preferred_element_type=F32) + bd_ref[0, 0])

    @pl.when(nv_ref[i] <= 0)
    def _():
        o_ref[...] = jnp.zeros_like(o_ref)


def _experts(layer, block_e, n_valid, rows, w_up, b_up, w_down, b_down):
    n_rows, dh = rows.shape
    d = 2 * dh
    n_blocks = n_rows // ROW_BLK
    depth, ne, _, f2 = w_up.shape
    wsel = lambda i, be, nu: (layer, be[i], 0, 0)
    grid_spec = pltpu.PrefetchScalarGridSpec(
        num_scalar_prefetch=2,
        grid=(n_blocks,),
        in_specs=[pl.BlockSpec((ROW_BLK, dh), lambda i, be, nu: (i, 0)),
                  pl.BlockSpec((1, 1, d, f2), wsel),
                  pl.BlockSpec((1, 1, 1, f2), wsel),
                  pl.BlockSpec((1, 1, D_FF, d), wsel),
                  pl.BlockSpec((1, 1, 1, d), wsel)],
        out_specs=pl.BlockSpec((ROW_BLK, dh), lambda i, be, nu: (i, 0)),
        scratch_shapes=[pltpu.VMEM((d, f2), BF16), pltpu.VMEM((D_FF, d), BF16)],
    )
    return pl.pallas_call(
        _expert_kernel,
        grid_spec=grid_spec,
        out_shape=jax.ShapeDtypeStruct((n_rows, dh), jnp.uint32),
        compiler_params=_cparams(("arbitrary",)),
        name="moe_experts",
    )(block_e, n_valid, rows, w_up, b_up.reshape(depth, ne, 1, f2), w_down,
      b_down.reshape(depth, ne, 1, d))


def _combine_kernel(y0_ref, y1_ref, y2_ref, y3_ref, gate_ref, x1_ref, g2_ref, fw_ref, o_ref,
                    *, final):
    gates = gate_ref[...]
    y_hi, y_lo = None, None
    for k, y_ref in enumerate((y0_ref, y1_ref, y2_ref, y3_ref)):
        hi, lo = _unpack_bf16_pairs(y_ref[...])
        g = gates[:, k:k + 1]
        y_hi = g * hi if y_hi is None else y_hi + g * hi
        y_lo = g * lo if y_lo is None else y_lo + g * lo
    y = jnp.concatenate([y_hi, y_lo], axis=1)
    x2 = x1_ref[...] + g2_ref[0] * y
    if final:
        x2 = (x2 * lax.rsqrt(jnp.mean(x2 * x2, axis=-1, keepdims=True) + NORM_EPS)) * fw_ref[...]
    o_ref[...] = x2


def _combine(y4, gates, x1, g2, fw, seq, final, th=256):
    t, d = x1.shape
    steps = t // th
    per_b = seq // th
    slot = lambda k: pl.BlockSpec((th, d // 2), lambda i: (k * steps + i, 0))
    return pl.pallas_call(
        functools.partial(_combine_kernel, final=final),
        grid=(steps,),
        in_specs=[slot(0), slot(1), slot(2), slot(3),
                  pl.BlockSpec((th, LANES), lambda i: (i, 0)),
                  pl.BlockSpec((th, d), lambda i: (i, 0)),
                  pl.BlockSpec((1, 1, d), lambda i: (i // per_b, 0, 0)),
                  pl.BlockSpec((1, d), lambda i: (0, 0))],
        out_specs=pl.BlockSpec((th, d), lambda i: (i, 0)),
        out_shape=jax.ShapeDtypeStruct((t, d), F32),
        compiler_params=_cparams(("arbitrary",)),
        name="moe_combine",
    )(y4, y4, y4, y4, gates, x1, g2, fw)


def _retention_tables(seq):
    pos = jnp.arange(seq, dtype=F32)
    inv_freq = ROPE_BASE ** (-jnp.arange(0, HEAD_DK, 2, dtype=F32) / HEAD_DK)
    ang = pos[:, None] * inv_freq[None, :]
    cos, sin = jnp.cos(ang), jnp.sin(ang)
    half = HEAD_DK // 2
    zpad = jnp.zeros((seq, 64 - half), F32)
    cos_t = jnp.concatenate([cos, zpad, cos, zpad], axis=1)
    sin_t = jnp.concatenate([-sin, zpad, sin, zpad], axis=1)
    log_gamma = jnp.log1p(-jnp.exp2(-5.0 - jnp.arange(N_HEADS, dtype=F32)))
    log_decay = jnp.broadcast_to(log_gamma[None, :, None], (RET_CHUNK, N_HEADS, HEAD_PAD))
    cum = jnp.cumsum(log_decay, axis=0)
    tot = cum[-1:]
    shp = lambda a: a.reshape(a.shape[0], HP)
    return (cos_t, sin_t, shp(jnp.exp(cum)), shp(jnp.exp(-cum)), shp(jnp.exp(tot - cum)),
            shp(jnp.exp(tot)))


def kernel(x, c, norm1_w, norm2_w, w_mod, b_mod, w_in, gla_w_a2, gla_b_a, gla_norm_w, ret_norm_w, s5_a_re, s5_a_im, s5_log_dt, s5_b_re, s5_b_im, s5_c_re, s5_c_im, s5_d, s5_w_glu, s5_b_glu, w_out, router_w, router_b, w_up, b_up, w_down, b_down, final_norm_w):
    batch, seq, d = x.shape
    depth = w_mod.shape[0]
    t = batch * seq
    n_slots = t * TOP_K
    n_blocks = n_slots // ROW_BLK + N_EXPERTS
    n_rows = n_blocks * ROW_BLK

    mod = _modulation(c, w_mod, b_mod)
    ret_tabs = _retention_tables(seq)
    x2d = x.reshape(t, d)

    for i in range(depth):
        m6 = mod[i].reshape(batch, 6, 1, d)
        sh1, sc1, g1, sh2, sc2, g2 = (m6[:, j] for j in range(6))

        w_p = _take_cols(w_in[i], _IN_SRC).astype(BF16)
        wa_p = jnp.zeros((LANES, HP), F32).at[:GATE_RANK].set(_take_cols(gla_w_a2[i], _DK_SRC))
        ba_p = _take_cols(gla_b_a[i], _DK_SRC).reshape(1, HP)
        gnw = _take_cols(gla_norm_w[i], _DV_SRC).reshape(1, HP)
        rnw = _take_cols(ret_norm_w[i], _DV_SRC).reshape(1, HP)
        kv = N_HEADS * HEAD_DV
        wo_g = _take_rows(w_out[i, :kv], _DV_SRC).astype(BF16)
        wo_r = _take_rows(w_out[i, kv:2 * kv], _DV_SRC).astype(BF16)
        wo_s = w_out[i, 2 * kv:].astype(BF16)
        rw_t = router_w[i].T
        rw_hi = rw_t.astype(BF16)
        rw_p = jnp.concatenate([rw_hi, (rw_t - rw_hi.astype(F32)).astype(BF16)], axis=0)
        rb_p = jnp.broadcast_to(router_b[i][:, None], (N_EXPERTS, LANES))

        proj = _in_projection(x2d, sc1, sh1, norm1_w[i].reshape(1, d), w_p, seq)
        o_gla = _gla_mixer(proj, batch, seq, wa_p.astype(BF16), ba_p, gnw)
        o_ret = _ret_mixer(proj, batch, seq, *ret_tabs, rnw)
        s5_tabs = _s5_tables(s5_a_re[i], s5_a_im[i], s5_log_dt[i], s5_b_re[i], s5_b_im[i],
                             s5_c_re[i], s5_c_im[i], s5_d[i], s5_w_glu[i], s5_b_glu[i])
        o_s5 = _s5_mixer(proj, batch, seq, s5_tabs)

        x1, hdn, idx, gates, rank, counts = _outproj_router(
            o_gla, o_ret, o_s5, wo_g, wo_r, wo_s, x2d, g1, sc2, sh2,
            norm2_w[i].reshape(1, d), rw_p, rb_p, seq)

        cnt = counts[:, 0].astype(jnp.int32)
        padded = (cnt + ROW_BLK - 1) // ROW_BLK * ROW_BLK
        pad_ends = jnp.cumsum(padded)
        pad_starts = pad_ends - padded
        slot_start = jnp.sum(jnp.where(idx[:TOP_K, :, None] == jnp.arange(N_EXPERTS), pad_starts,
                                       0), axis=-1)
        dest_sm = (slot_start + rank[:TOP_K]).astype(jnp.int32).reshape(-1)
        gates_tm = jnp.pad(gates[:TOP_K].T, ((0, 0), (0, LANES - TOP_K)))
        blk_start = jnp.arange(n_blocks, dtype=jnp.int32) * ROW_BLK
        block_e = jnp.minimum(jnp.sum(pad_ends[None, :] <= blk_start[:, None], axis=1),
                              N_EXPERTS - 1).astype(jnp.int32)
        n_valid = jnp.clip((pad_starts + cnt)[block_e] - blk_start, 0, ROW_BLK).astype(jnp.int32)

        rows = _scatter_rows(hdn, dest_sm, n_rows)
        out_rows = _experts(i, block_e, n_valid, rows, w_up, b_up, w_down, b_down)
        y4 = _gather_rows(out_rows, dest_sm)
        x2d = _combine(y4, gates_tm, x1, g2, final_norm_w.reshape(1, d), seq,
                       final=(i == depth - 1))

    return x2d.reshape(batch, seq, d)
```

```python
import functools

import numpy as np
import jax
import jax.numpy as jnp
from jax import lax
from jax.experimental import pallas as pl
from jax.experimental.pallas import tpu as pltpu
from jax.experimental.pallas import tpu_sc as plsc

D_MODEL = 1024
CHUNK = 64
RET_CHUNK = 128
NORM_EPS = 1e-5
N_HEADS = 4
HEAD_DK = 48
HEAD_DV = 96
GATE_RANK = 16
GATE_TEMP = 16.0
ROPE_BASE = 10000.0
S5_WIDTH = 256
S5_GROUP_DIM = 16
S5_GROUPS = 16
S5_STATE = 64
N_EXPERTS = 32
TOP_K = 4
D_FF = 1024
SWIGLU_LIMIT = 7.0
SWIGLU_ALPHA = 1.702

LANES = 128
HEAD_PAD = LANES
HP = N_HEADS * HEAD_PAD
VMEM_LIMIT = 56 * 1024 * 1024

OFF_GQ, OFF_GK, OFF_GV, OFF_GG = 0, HP, 2 * HP, 3 * HP
OFF_RQ, OFF_RK, OFF_RV, OFF_RG = 4 * HP, 5 * HP, 6 * HP, 7 * HP
OFF_SU = 8 * HP
OFF_GA = OFF_SU + S5_WIDTH
NP_COLS = OFF_GA + LANES
PROJ_CH = 896

ROW_BLK = 512

F32 = jnp.float32
BF16 = jnp.bfloat16


def _in_col_map():
    src = -np.ones((NP_COLS,), np.int64)
    kq = N_HEADS * HEAD_DK
    kv = N_HEADS * HEAD_DV
    base = dict(gq=0, gk=kq, gv=2 * kq, gg=2 * kq + kv, ga=2 * kq + 2 * kv)
    r0 = base['ga'] + GATE_RANK
    base.update(rq=r0, rk=r0 + kq, rv=r0 + 2 * kq, rg=r0 + 2 * kq + kv, su=r0 + 2 * kq + 2 * kv)
    half = HEAD_DK // 2
    for h in range(N_HEADS):
        for d in range(HEAD_DK):
            src[OFF_GQ + h * HEAD_PAD + d] = base['gq'] + h * HEAD_DK + d
            src[OFF_GK + h * HEAD_PAD + d] = base['gk'] + h * HEAD_DK + d
            lane = d if d < half else 64 + (d - half)
            src[OFF_RQ + h * HEAD_PAD + lane] = base['rq'] + h * HEAD_DK + d
            src[OFF_RK + h * HEAD_PAD + lane] = base['rk'] + h * HEAD_DK + d
        for d in range(HEAD_DV):
            src[OFF_GV + h * HEAD_PAD + d] = base['gv'] + h * HEAD_DV + d
            src[OFF_GG + h * HEAD_PAD + d] = base['gg'] + h * HEAD_DV + d
            src[OFF_RV + h * HEAD_PAD + d] = base['rv'] + h * HEAD_DV + d
            src[OFF_RG + h * HEAD_PAD + d] = base['rg'] + h * HEAD_DV + d
    src[OFF_SU:OFF_SU + S5_WIDTH] = base['su'] + np.arange(S5_WIDTH)
    src[OFF_GA:OFF_GA + GATE_RANK] = base['ga'] + np.arange(GATE_RANK)
    return src


_IN_SRC = _in_col_map()


def _head_pad_map(width):
    src = -np.ones((HP,), np.int64)
    for h in range(N_HEADS):
        src[h * HEAD_PAD:h * HEAD_PAD + width] = h * width + np.arange(width)
    return src


_DV_SRC = _head_pad_map(HEAD_DV)
_DK_SRC = _head_pad_map(HEAD_DK)


def _take_cols(w, src):
    out = jnp.take(w, jnp.asarray(np.maximum(src, 0)), axis=-1)
    return jnp.where(jnp.asarray(src >= 0), out, 0)


def _take_rows(w, src):
    out = jnp.take(w, jnp.asarray(np.maximum(src, 0)), axis=0)
    return jnp.where(jnp.asarray(src >= 0)[:, None], out, 0)


def _cparams(sem):
    return pltpu.CompilerParams(dimension_semantics=sem, vmem_limit_bytes=VMEM_LIMIT)


def _mod_kernel(c_ref, w_ref, b_ref, o_ref):
    c = c_ref[...]
    cond = c * jax.nn.sigmoid(c)
    o_ref[0] = jnp.dot(cond, w_ref[0], preferred_element_type=F32,
                       precision=lax.Precision.HIGHEST) + b_ref[0]


def _modulation(c, w_mod, b_mod):
    depth, d, n = w_mod.shape
    b = c.shape[0]
    nb = 1536
    return pl.pallas_call(
        _mod_kernel,
        grid=(depth, n // nb),
        in_specs=[pl.BlockSpec((b, d), lambda l, j: (0, 0)),
                  pl.BlockSpec((1, d, nb), lambda l, j: (l, 0, j)),
                  pl.BlockSpec((1, 1, nb), lambda l, j: (l, 0, j))],
        out_specs=pl.BlockSpec((1, b, nb), lambda l, j: (l, 0, j)),
        out_shape=jax.ShapeDtypeStruct((depth, b, n), F32),
        compiler_params=_cparams(("arbitrary", "arbitrary")),
        name="adaln_mod",
    )(c, w_mod, b_mod.reshape(depth, 1, n))


def _rms_mod(x, nw, sc, sh):
    y = x * lax.rsqrt(jnp.mean(x * x, axis=-1, keepdims=True) + NORM_EPS)
    return (y * nw) * (1.0 + sc) + sh


def _inproj_kernel(x_ref, sc_ref, sh_ref, nw_ref, w_ref, o_ref):
    h = _rms_mod(x_ref[...], nw_ref[...], sc_ref[0], sh_ref[0]).astype(BF16)
    for j in range(NP_COLS // PROJ_CH):
        cs = slice(j * PROJ_CH, (j + 1) * PROJ_CH)
        o_ref[:, cs] = jnp.dot(h, w_ref[:, cs], preferred_element_type=F32).astype(BF16)


def _in_projection(x2d, sc, sh, nw, w_p, seq, tm=512):
    t, d = x2d.shape
    per_b = seq // tm
    return pl.pallas_call(
        _inproj_kernel,
        grid=(t // tm,),
        in_specs=[pl.BlockSpec((tm, d), lambda i: (i, 0)),
                  pl.BlockSpec((1, 1, d), lambda i: (i // per_b, 0, 0)),
                  pl.BlockSpec((1, 1, d), lambda i: (i // per_b, 0, 0)),
                  pl.BlockSpec((1, d), lambda i: (0, 0)),
                  pl.BlockSpec((d, NP_COLS), lambda i: (0, 0))],
        out_specs=pl.BlockSpec((tm, NP_COLS), lambda i: (i, 0)),
        out_shape=jax.ShapeDtypeStruct((t, NP_COLS), BF16),
        compiler_params=_cparams(("arbitrary",)),
        name="in_proj",
    )(x2d, sc, sh, nw, w_p)


_NT = (((1,), (1,)), ((), ()))
_TN = (((0,), (0,)), ((), ()))


def _tri_mask(n=CHUNK):
    r = lax.broadcasted_iota(jnp.int32, (n, n), 0)
    c = lax.broadcasted_iota(jnp.int32, (n, n), 1)
    return r >= c


def _head_attention(qd, ki, ke, vh, et, st_ref, h, causal):
    qb = qd.astype(BF16)
    sc = lax.dot_general(qb, ki.astype(BF16), _NT, preferred_element_type=F32)
    sc = jnp.where(causal, sc, 0.0)
    st = st_ref[h]
    o = jnp.dot(sc.astype(BF16), vh, preferred_element_type=F32)
    o = o + lax.dot_general(qb, st.astype(BF16), _NT, preferred_element_type=F32)
    st_ref[h] = st * et + lax.dot_general(vh, ke.astype(BF16), _TN, preferred_element_type=F32)
    return o


def _gla_kernel(q_ref, k_ref, v_ref, g_ref, a_ref, wa_ref, ba_ref, nw_ref, tri_ref, o_ref,
                st_ref, qd_s, ki_s, ke_s, et_s):
    @pl.when(pl.program_id(1) == 0)
    def _():
        st_ref[...] = jnp.zeros_like(st_ref)

    causal = _tri_mask()
    tl = q_ref.shape[0]
    n_chunks = tl // CHUNK

    z = jnp.dot(a_ref[...], wa_ref[...], preferred_element_type=F32) + ba_ref[...]
    la = (jnp.minimum(z, 0.0) - jnp.log1p(jnp.exp(-jnp.abs(z)))) * (1.0 / GATE_TEMP)
    hi = la.astype(BF16)
    lo = (la - hi.astype(F32)).astype(BF16)
    cum = (jnp.dot(tri_ref[...], hi, preferred_element_type=F32)
           + jnp.dot(tri_ref[...], lo, preferred_element_type=F32))
    cum3 = cum.reshape(n_chunks, CHUNK, HP)
    tot3 = cum3[:, CHUNK - 1:CHUNK, :]
    qd_s[...] = ((q_ref[...].astype(F32) * (HEAD_DK ** -0.5)) * jnp.exp(cum)).astype(BF16)
    kf = k_ref[...].astype(F32)
    ki_s[...] = (kf * jnp.exp(-cum)).astype(BF16)
    ke_s[...] = (kf * jnp.exp(tot3 - cum3).reshape(tl, HP)).astype(BF16)
    et_s[...] = jnp.exp(tot3).reshape(n_chunks, HP)

    def chunk(c, carry):
        r = pl.ds(pl.multiple_of(c * CHUNK, CHUNK), CHUNK)
        et = et_s[pl.ds(c, 1), :]
        for h in range(N_HEADS):
            sl = slice(h * HEAD_PAD, (h + 1) * HEAD_PAD)
            o = _head_attention(qd_s[r, sl], ki_s[r, sl], ke_s[r, sl], v_ref[r, sl], et[:, sl],
                                st_ref, h, causal)
            ms = jnp.sum(o * o, axis=-1, keepdims=True) * (1.0 / HEAD_DV)
            y = (o * lax.rsqrt(ms + NORM_EPS)) * nw_ref[:, sl]
            g = g_ref[r, sl].astype(F32)
            o_ref[r, sl] = (y * (g * jax.nn.sigmoid(g))).astype(BF16)
        return carry

    lax.fori_loop(0, n_chunks, chunk, 0, unroll=4)


def _ret_kernel(q_ref, k_ref, v_ref, g_ref, cos_ref, sin_ref, dq_ref, dki_ref, dke_ref,
                dt_ref, nw_ref, o_ref, st_ref):
    @pl.when(pl.program_id(1) == 0)
    def _():
        st_ref[...] = jnp.zeros_like(st_ref)

    causal = _tri_mask(RET_CHUNK)
    n_chunks = q_ref.shape[0] // RET_CHUNK
    lane = lax.broadcasted_iota(jnp.int32, (RET_CHUNK, HEAD_PAD), 1)
    real = lane < HEAD_DV

    def chunk(c, carry):
        r = pl.ds(pl.multiple_of(c * RET_CHUNK, RET_CHUNK), RET_CHUNK)
        cos = cos_ref[r, :]
        sin = sin_ref[r, :]
        for h in range(N_HEADS):
            sl = slice(h * HEAD_PAD, (h + 1) * HEAD_PAD)
            qh = q_ref[r, sl].astype(F32)
            kh = k_ref[r, sl].astype(F32)
            qr = qh * cos + pltpu.roll(qh, 64, 1) * sin
            kr = (kh * cos + pltpu.roll(kh, 64, 1) * sin) * (HEAD_DK ** -0.5)
            o = _head_attention(qr * dq_ref[:, sl], kr * dki_ref[:, sl], kr * dke_ref[:, sl],
                                v_ref[r, sl], dt_ref[:, sl], st_ref, h, causal)
            mu = jnp.sum(o, axis=-1, keepdims=True) * (1.0 / HEAD_DV)
            oc = jnp.where(real, o - mu, 0.0)
            var = jnp.sum(oc * oc, axis=-1, keepdims=True) * (1.0 / HEAD_DV)
            y = (oc * lax.rsqrt(var + NORM_EPS)) * nw_ref[:, sl]
            g = g_ref[r, sl].astype(F32)
            o_ref[r, sl] = (y * (g * jax.nn.sigmoid(g))).astype(BF16)
        return carry

    lax.fori_loop(0, n_chunks, chunk, 0, unroll=2)


def _proj_spec(tl, width, col_off, per_b):
    cb = col_off // width
    return pl.BlockSpec((tl, width), lambda b, l: (b * per_b + l, cb))


def _full(shape):
    return pl.BlockSpec(shape, lambda b, l: (0,) * len(shape))


def _gla_mixer(proj, batch, seq, wa_p, ba_p, nw_p, tl=512):
    per_b = seq // tl
    pos = np.arange(tl)
    tri_bd = jnp.asarray((pos[:, None] // CHUNK == pos[None, :] // CHUNK)
                         & (pos[:, None] >= pos[None, :]), BF16)
    return pl.pallas_call(
        _gla_kernel,
        grid=(batch, per_b),
        in_specs=[_proj_spec(tl, HP, OFF_GQ, per_b), _proj_spec(tl, HP, OFF_GK, per_b),
                  _proj_spec(tl, HP, OFF_GV, per_b), _proj_spec(tl, HP, OFF_GG, per_b),
                  _proj_spec(tl, LANES, OFF_GA, per_b),
                  _full((LANES, HP)), _full((1, HP)), _full((1, HP)), _full((tl, tl))],
        out_specs=pl.BlockSpec((tl, HP), lambda b, l: (b * per_b + l, 0)),
        out_shape=jax.ShapeDtypeStruct((batch * seq, HP), BF16),
        scratch_shapes=[pltpu.VMEM((N_HEADS, HEAD_PAD, HEAD_PAD), F32),
                        pltpu.VMEM((tl, HP), BF16), pltpu.VMEM((tl, HP), BF16),
                        pltpu.VMEM((tl, HP), BF16), pltpu.VMEM((tl // CHUNK, HP), F32)],
        compiler_params=_cparams(("arbitrary", "arbitrary")),
        name="gla_mixer",
    )(proj, proj, proj, proj, proj, wa_p, ba_p, nw_p, tri_bd)


def _ret_mixer(proj, batch, seq, cos_t, sin_t, dq, dki, dke, dtot, nw_p, tl=512):
    per_b = seq // tl
    return pl.pallas_call(
        _ret_kernel,
        grid=(batch, per_b),
        in_specs=[_proj_spec(tl, HP, OFF_RQ, per_b), _proj_spec(tl, HP, OFF_RK, per_b),
                  _proj_spec(tl, HP, OFF_RV, per_b), _proj_spec(tl, HP, OFF_RG, per_b),
                  pl.BlockSpec((tl, HEAD_PAD), lambda b, l: (l, 0)),
                  pl.BlockSpec((tl, HEAD_PAD), lambda b, l: (l, 0)),
                  _full((RET_CHUNK, HP)), _full((RET_CHUNK, HP)), _full((RET_CHUNK, HP)),
                  _full((1, HP)), _full((1, HP))],
        out_specs=pl.BlockSpec((tl, HP), lambda b, l: (b * per_b + l, 0)),
        out_shape=jax.ShapeDtypeStruct((batch * seq, HP), BF16),
        scratch_shapes=[pltpu.VMEM((N_HEADS, HEAD_PAD, HEAD_PAD), F32)],
        compiler_params=_cparams(("arbitrary", "arbitrary")),
        name="ret_mixer",
    )(proj, proj, proj, proj, cos_t, sin_t, dq, dki, dke, dtot, nw_p)


S5_N = S5_GROUPS * S5_STATE


def _gelu_tanh(x):
    return 0.5 * x * (1.0 + jnp.tanh(np.sqrt(2.0 / np.pi) * (x + 0.044715 * (x * x * x))))


def _s5_kernel(u_ref, bb_ref, cb_ref, nr_ref, ni_ref, pr_ref, pi_ref, lr_ref, li_ref,
               d_ref, wg_ref, bg_ref, o_ref, sr_ref, si_ref, x_scr, s_scr):
    @pl.when(pl.program_id(1) == 0)
    def _():
        sr_ref[...] = jnp.zeros_like(sr_ref)
        si_ref[...] = jnp.zeros_like(si_ref)

    tri = _tri_mask().astype(BF16)
    n_chunks = u_ref.shape[0] // CHUNK
    u = u_ref[...]
    x_scr[...] = jnp.dot(u, bb_ref[...], preferred_element_type=F32)

    def chunk(c, carry):
        r = pl.ds(pl.multiple_of(c * CHUNK, CHUNK), CHUNK)
        xr, xi = x_scr[r, :S5_N], x_scr[r, S5_N:]
        nr, ni = nr_ref[...], ni_ref[...]
        p_r = jnp.dot(tri, (xr * nr - xi * ni).astype(BF16), preferred_element_type=F32)
        p_i = jnp.dot(tri, (xr * ni + xi * nr).astype(BF16), preferred_element_type=F32)
        s0r, s0i = sr_ref[...], si_ref[...]
        lr, li = lr_ref[...], li_ref[...]
        q_r = p_r + (s0r * lr - s0i * li)
        q_i = p_i + (s0r * li + s0i * lr)
        pr, pi = pr_ref[...], pi_ref[...]
        s_r = q_r * pr - q_i * pi
        s_i = q_r * pi + q_i * pr
        sr_ref[...] = s_r[CHUNK - 1:CHUNK, :]
        si_ref[...] = s_i[CHUNK - 1:CHUNK, :]
        s_scr[r, :S5_N] = s_r.astype(BF16)
        s_scr[r, S5_N:] = s_i.astype(BF16)
        return carry

    lax.fori_loop(0, n_chunks, chunk, 0, unroll=2)
    y = jnp.dot(s_scr[...], cb_ref[...], preferred_element_type=F32)
    y = _gelu_tanh(y + d_ref[...] * u.astype(F32))
    gate = jnp.dot(y.astype(BF16), wg_ref[...], preferred_element_type=F32) + bg_ref[...]
    o_ref[...] = (y * jax.nn.sigmoid(gate)).astype(BF16)


def _s5_mixer(proj, batch, seq, tabs, tl=512):
    per_b = seq // tl
    bb, cb, nr, ni, pr, pi, lr, li, dsk, wg, bg = tabs
    return pl.pallas_call(
        _s5_kernel,
        grid=(batch, per_b),
        in_specs=[_proj_spec(tl, S5_WIDTH, OFF_SU, per_b),
                  _full((S5_WIDTH, 2 * S5_N)), _full((2 * S5_N, S5_WIDTH)),
                  _full((CHUNK, S5_N)), _full((CHUNK, S5_N)),
                  _full((CHUNK, S5_N)), _full((CHUNK, S5_N)),
                  _full((1, S5_N)), _full((1, S5_N)),
                  _full((1, S5_WIDTH)), _full((S5_WIDTH, S5_WIDTH)), _full((1, S5_WIDTH))],
        out_specs=pl.BlockSpec((tl, S5_WIDTH), lambda b, l: (b * per_b + l, 0)),
        out_shape=jax.ShapeDtypeStruct((batch * seq, S5_WIDTH), BF16),
        scratch_shapes=[pltpu.VMEM((1, S5_N), F32), pltpu.VMEM((1, S5_N), F32),
                        pltpu.VMEM((tl, 2 * S5_N), F32), pltpu.VMEM((tl, 2 * S5_N), BF16)],
        compiler_params=_cparams(("arbitrary", "arbitrary")),
        name="s5_mixer",
    )(proj, bb, cb, nr, ni, pr, pi, lr, li, dsk, wg, bg)


def _s5_tables(a_re, a_im, log_dt, b_re, b_im, c_re, c_im, d_skip, w_glu, b_glu):
    lam = lax.complex(a_re, a_im)
    dt = jnp.exp(log_dt)[:, None]
    lam_bar = jnp.exp(lam * dt)
    b_bar = ((lam_bar - 1.0) / lam)[..., None] * lax.complex(b_re, b_im)
    eye = jnp.eye(S5_GROUPS, dtype=F32)
    def blk_b(m):
        return jnp.einsum('gph,gk->ghkp', m, eye).reshape(S5_WIDTH, S5_N)
    bb = jnp.concatenate([blk_b(jnp.real(b_bar)), blk_b(jnp.imag(b_bar))], axis=1)
    def blk_c(m):
        return jnp.einsum('ghp,gk->kpgh', m, eye).reshape(S5_N, S5_WIDTH)
    cb = jnp.concatenate([blk_c(c_re), blk_c(-c_im)], axis=0)
    steps = jnp.arange(CHUNK, dtype=F32)[:, None, None]
    lam_dt = (lam * dt)[None]
    pos = jnp.exp(lam_dt * steps).reshape(CHUNK, S5_N)
    neg = jnp.exp(-lam_dt * steps).reshape(CHUNK, S5_N)
    one = lam_bar.reshape(1, S5_N)
    return (bb.astype(BF16), cb.astype(BF16), jnp.real(neg), jnp.imag(neg), jnp.real(pos),
            jnp.imag(pos), jnp.real(one), jnp.imag(one), d_skip.reshape(1, S5_WIDTH),
            w_glu.astype(BF16), b_glu.reshape(1, S5_WIDTH))


def _pack_bf16_pairs(x):
    w = x.shape[1] // 2
    xb = x.astype(BF16).astype(F32)
    hi = lax.bitcast_convert_type(xb[:, :w], jnp.uint32)
    lo = lax.bitcast_convert_type(xb[:, w:], jnp.uint32)
    return hi | (lo >> 16)


def _unpack_bf16_pairs(p):
    hi = lax.bitcast_convert_type(p & jnp.uint32(0xFFFF0000), F32)
    lo = lax.bitcast_convert_type(p << 16, F32)
    return hi, lo


def _router_kernel(og_ref, or_ref, os_ref, wg_ref, wr_ref, ws_ref, x_ref, g1_ref, sc_ref,
                   sh_ref, nw_ref, rw_ref, rb_ref,
                   x1_ref, h_ref, idx_ref, gate_ref, rank_ref, cnt_ref, carry_ref):
    i = pl.program_id(0)

    @pl.when(i == 0)
    def _():
        carry_ref[...] = jnp.zeros_like(carry_ref)

    mix = (jnp.dot(og_ref[...], wg_ref[...], preferred_element_type=F32)
           + jnp.dot(or_ref[...], wr_ref[...], preferred_element_type=F32)
           + jnp.dot(os_ref[...], ws_ref[...], preferred_element_type=F32))
    x1 = x_ref[...] + g1_ref[0] * mix
    x1_ref[...] = x1
    hdn = _rms_mod(x1, nw_ref[...], sc_ref[0], sh_ref[0])
    h_ref[...] = _pack_bf16_pairs(hdn)
    h_hi = hdn.astype(BF16)
    h_lo = (hdn - h_hi.astype(F32)).astype(BF16)
    p = lax.dot_general(rw_ref[...], h_hi, _NT, preferred_element_type=F32)
    q = lax.dot_general(rw_ref[:N_EXPERTS, :], h_lo, _NT, preferred_element_type=F32)
    logits = (p[:N_EXPERTS] + p[N_EXPERTS:] + q) + rb_ref[:, 0:1]
    tm = logits.shape[1]
    eidx = lax.broadcasted_iota(jnp.int32, (N_EXPERTS, tm), 0)
    work = logits
    onehot = jnp.zeros((N_EXPERTS, tm), F32)
    vals, idxs, sels = [], [], []
    for _ in range(TOP_K):
        m = jnp.max(work, axis=0, keepdims=True)
        ix = jnp.min(jnp.where(work == m, eidx, N_EXPERTS), axis=0, keepdims=True)
        sel = eidx == ix
        work = jnp.where(sel, -jnp.inf, work)
        onehot = onehot + sel.astype(F32)
        vals.append(m)
        idxs.append(ix)
        sels.append(sel)
    exps = [jnp.exp(v - vals[0]) for v in vals]
    denom = exps[0] + exps[1] + exps[2] + exps[3]
    r = lax.broadcasted_iota(jnp.int32, (tm, tm), 0)
    c = lax.broadcasted_iota(jnp.int32, (tm, tm), 1)
    earlier = (r < c).astype(BF16)
    before = (jnp.dot(onehot.astype(BF16), earlier, preferred_element_type=F32)
              + carry_ref[:, 0:1])
    row8 = lax.broadcasted_iota(jnp.int32, (8, tm), 0)
    idx_out = jnp.zeros((8, tm), jnp.int32)
    gate_out = jnp.zeros((8, tm), F32)
    rank_out = jnp.zeros((8, tm), F32)
    for k in range(TOP_K):
        rk = jnp.sum(jnp.where(sels[k], before, 0.0), axis=0, keepdims=True)
        idx_out = jnp.where(row8 == k, idxs[k], idx_out)
        gate_out = jnp.where(row8 == k, exps[k] / denom, gate_out)
        rank_out = jnp.where(row8 == k, rk, rank_out)
    idx_ref[...] = idx_out
    gate_ref[...] = gate_out
    rank_ref[...] = rank_out.astype(jnp.int32)
    total = carry_ref[...] + jnp.sum(onehot, axis=1, keepdims=True)
    carry_ref[...] = total
    cnt_ref[...] = total


def _outproj_router(o_gla, o_ret, o_s5, wg, wr, ws, x2d, g1, sc2, sh2, nw2, rw_p, rb_p, seq,
                    tm=512):
    t, d = x2d.shape
    per_b = seq // tm
    row = lambda w: pl.BlockSpec((tm, w), lambda i: (i, 0))
    full = lambda s: pl.BlockSpec(s, lambda i: (0,) * len(s))
    per_batch = pl.BlockSpec((1, 1, d), lambda i: (i // per_b, 0, 0))
    slot_t = pl.BlockSpec((8, tm), lambda i: (0, i))
    return pl.pallas_call(
        _router_kernel,
        grid=(t // tm,),
        in_specs=[row(HP), row(HP), row(S5_WIDTH), full((HP, d)), full((HP, d)),
                  full((S5_WIDTH, d)), row(d), per_batch, per_batch, per_batch,
                  full((1, d)), full((2 * N_EXPERTS, d)), full((N_EXPERTS, LANES))],
        out_specs=[row(d), row(d // 2), slot_t, slot_t, slot_t, full((N_EXPERTS, LANES))],
        out_shape=[jax.ShapeDtypeStruct((t, d), F32),
                   jax.ShapeDtypeStruct((t, d // 2), jnp.uint32),
                   jax.ShapeDtypeStruct((8, t), jnp.int32),
                   jax.ShapeDtypeStruct((8, t), F32),
                   jax.ShapeDtypeStruct((8, t), jnp.int32),
                   jax.ShapeDtypeStruct((N_EXPERTS, LANES), F32)],
        scratch_shapes=[pltpu.VMEM((N_EXPERTS, LANES), F32)],
        compiler_params=_cparams(("arbitrary",)),
        name="outproj_router",
    )(o_gla, o_ret, o_s5, wg, wr, ws, x2d, g1, sc2, sh2, nw2, rw_p, rb_p)


GATHER_WIN = 64


def _gather_rows(table, idx):
    m = idx.shape[0]
    w = table.shape[1]
    mesh = plsc.VectorSubcoreMesh(core_axis_name="core", subcore_axis_name="subcore")

    @functools.partial(pl.kernel, out_type=jax.ShapeDtypeStruct((m, w), table.dtype),
                       mesh=mesh, name="sc_row_gather")
    def gather(x_hbm, i_hbm, o_hbm):
        def body(i_vmem, o_vmem):
            pltpu.sync_copy(x_hbm.at[i_vmem], o_vmem)

        pltpu.emit_pipeline(
            body,
            grid=(m // GATHER_WIN,),
            in_specs=[pl.BlockSpec((GATHER_WIN,), lambda i: (i,))],
            out_specs=[pl.BlockSpec((GATHER_WIN, w), lambda i: (i, 0))],
            core_axis_name=("core", "subcore"),
            dimension_semantics=(pltpu.PARALLEL,),
        )(i_hbm, o_hbm)

    return gather(table, idx)


def _scatter_rows(x, dest_slot_major, n_rows):
    t, w = x.shape
    steps = t // GATHER_WIN
    mesh = plsc.VectorSubcoreMesh(core_axis_name="core", subcore_axis_name="subcore")

    @functools.partial(pl.kernel, out_type=jax.ShapeDtypeStruct((n_rows, w), x.dtype),
                       mesh=mesh, name="sc_row_scatter")
    def scatter(x_hbm, i_hbm, o_hbm):
        def body(x_vmem, i0, i1, i2, i3):
            for i_vmem in (i0, i1, i2, i3):
                pltpu.sync_copy(x_vmem, o_hbm.at[i_vmem])

        slot = lambda k: pl.BlockSpec((GATHER_WIN,), lambda i: (k * steps + i,))
        pltpu.emit_pipeline(
            body,
            grid=(steps,),
            in_specs=[pl.BlockSpec((GATHER_WIN, w), lambda i: (i, 0)),
                      slot(0), slot(1), slot(2), slot(3)],
            out_specs=[],
            core_axis_name=("core", "subcore"),
            dimension_semantics=(pltpu.PARALLEL,),
        )(x_hbm, i_hbm, i_hbm, i_hbm, i_hbm)

    return scatter(x, dest_slot_major)


W_UP_PARTS = 4
W_DOWN_PARTS = 2


def _expert_kernel(be_ref, nv_ref, rows_ref, *refs):
    wu_refs = refs[:W_UP_PARTS]
    bu_ref = refs[W_UP_PARTS]
    wd_refs = refs[W_UP_PARTS + 1:W_UP_PARTS + 1 + W_DOWN_PARTS]
    bd_ref, o_ref, wu_bf, wd_bf = refs[W_UP_PARTS + 1 + W_DOWN_PARTS:]
    i = pl.program_id(0)
    e = be_ref[i]
    prev = be_ref[jnp.maximum(i - 1, 0)]

    @pl.when((i == 0) | (e != prev))
    def _():
        ru = wu_bf.shape[0] // W_UP_PARTS
        for j, w_ref in enumerate(wu_refs):
            wu_bf[j * ru:(j + 1) * ru, :] = w_ref[0, 0].astype(BF16)
        rd = wd_bf.shape[0] // W_DOWN_PARTS
        for j, w_ref in enumerate(wd_refs):
            wd_bf[j * rd:(j + 1) * rd, :] = w_ref[0, 0].astype(BF16)

    @pl.when(nv_ref[i] > 0)
    def _():
        row = lax.broadcasted_iota(jnp.int32, rows_ref.shape, 0)
        x_hi, x_lo = _unpack_bf16_pairs(jnp.where(row < nv_ref[i], rows_ref[...], jnp.uint32(0)))
        half = D_MODEL // 2
        up = (jnp.dot(x_hi.astype(BF16), wu_bf[:half, :], preferred_element_type=F32)
              + jnp.dot(x_lo.astype(BF16), wu_bf[half:, :], preferred_element_type=F32)
              + bu_ref[0, 0])
        x_glu = jnp.minimum(up[:, :D_FF], SWIGLU_LIMIT)
        x_lin = jnp.clip(up[:, D_FF:], -SWIGLU_LIMIT, SWIGLU_LIMIT)
        act = x_glu * jax.nn.sigmoid(SWIGLU_ALPHA * x_glu) * (x_lin + 1.0)
        o_ref[...] = _pack_bf16_pairs(
            jnp.dot(act.astype(BF16), wd_bf[...], preferred_element_type=F32) + bd_ref[0, 0])

    @pl.when(nv_ref[i] <= 0)
    def _():
        o_ref[...] = jnp.zeros_like(o_ref)


def _experts(layer, block_e, n_valid, rows, w_up, b_up, w_down, b_down):
    n_rows, dh = rows.shape
    d = 2 * dh
    n_blocks = n_rows // ROW_BLK
    depth, ne, _, f2 = w_up.shape
    wsel = lambda i, be, nu: (layer, be[i], 0, 0)
    part = lambda j: (lambda i, be, nu: (layer, be[i], j, 0))
    up_parts = [pl.BlockSpec((1, 1, d // W_UP_PARTS, f2), part(j)) for j in range(W_UP_PARTS)]
    down_parts = [pl.BlockSpec((1, 1, D_FF // W_DOWN_PARTS, d), part(j))
                  for j in range(W_DOWN_PARTS)]
    grid_spec = pltpu.PrefetchScalarGridSpec(
        num_scalar_prefetch=2,
        grid=(n_blocks,),
        in_specs=[pl.BlockSpec((ROW_BLK, dh), lambda i, be, nu: (i, 0)),
                  *up_parts,
                  pl.BlockSpec((1, 1, 1, f2), wsel),
                  *down_parts,
                  pl.BlockSpec((1, 1, 1, d), wsel)],
        out_specs=pl.BlockSpec((ROW_BLK, dh), lambda i, be, nu: (i, 0)),
        scratch_shapes=[pltpu.VMEM((d, f2), BF16), pltpu.VMEM((D_FF, d), BF16)],
    )
    return pl.pallas_call(
        _expert_kernel,
        grid_spec=grid_spec,
        out_shape=jax.ShapeDtypeStruct((n_rows, dh), jnp.uint32),
        compiler_params=_cparams(("arbitrary",)),
        name="moe_experts",
    )(block_e, n_valid, rows, *([w_up] * W_UP_PARTS), b_up.reshape(depth, ne, 1, f2),
      *([w_down] * W_DOWN_PARTS), b_down.reshape(depth, ne, 1, d))


def _combine_kernel(y0_ref, y1_ref, y2_ref, y3_ref, gate_ref, x1_ref, g2_ref, fw_ref, o_ref,
                    *, final):
    gates = gate_ref[...]
    y_hi, y_lo = None, None
    for k, y_ref in enumerate((y0_ref, y1_ref, y2_ref, y3_ref)):
        hi, lo = _unpack_bf16_pairs(y_ref[...])
        g = gates[:, k:k + 1]
        y_hi = g * hi if y_hi is None else y_hi + g * hi
        y_lo = g * lo if y_lo is None else y_lo + g * lo
    y = jnp.concatenate([y_hi, y_lo], axis=1)
    x2 = x1_ref[...] + g2_ref[0] * y
    if final:
        x2 = (x2 * lax.rsqrt(jnp.mean(x2 * x2, axis=-1, keepdims=True) + NORM_EPS)) * fw_ref[...]
    o_ref[...] = x2


def _combine(y4, gates, x1, g2, fw, seq, final, th=256):
    t, d = x1.shape
    steps = t // th
    per_b = seq // th
    slot = lambda k: pl.BlockSpec((th, d // 2), lambda i: (k * steps + i, 0))
    return pl.pallas_call(
        functools.partial(_combine_kernel, final=final),
        grid=(steps,),
        in_specs=[slot(0), slot(1), slot(2), slot(3),
                  pl.BlockSpec((th, LANES), lambda i: (i, 0)),
                  pl.BlockSpec((th, d), lambda i: (i, 0)),
                  pl.BlockSpec((1, 1, d), lambda i: (i // per_b, 0, 0)),
                  pl.BlockSpec((1, d), lambda i: (0, 0))],
        out_specs=pl.BlockSpec((th, d), lambda i: (i, 0)),
        out_shape=jax.ShapeDtypeStruct((t, d), F32),
        compiler_params=_cparams(("arbitrary",)),
        name="moe_combine",
    )(y4, y4, y4, y4, gates, x1, g2, fw)


def _retention_tables(seq):
    pos = jnp.arange(seq, dtype=F32)
    inv_freq = ROPE_BASE ** (-jnp.arange(0, HEAD_DK, 2, dtype=F32) / HEAD_DK)
    ang = pos[:, None] * inv_freq[None, :]
    cos, sin = jnp.cos(ang), jnp.sin(ang)
    half = HEAD_DK // 2
    zpad = jnp.zeros((seq, 64 - half), F32)
    cos_t = jnp.concatenate([cos, zpad, cos, zpad], axis=1)
    sin_t = jnp.concatenate([-sin, zpad, sin, zpad], axis=1)
    log_gamma = jnp.log1p(-jnp.exp2(-5.0 - jnp.arange(N_HEADS, dtype=F32)))
    log_decay = jnp.broadcast_to(log_gamma[None, :, None], (RET_CHUNK, N_HEADS, HEAD_PAD))
    cum = jnp.cumsum(log_decay, axis=0)
    tot = cum[-1:]
    shp = lambda a: a.reshape(a.shape[0], HP)
    return (cos_t, sin_t, shp(jnp.exp(cum)), shp(jnp.exp(-cum)), shp(jnp.exp(tot - cum)),
            shp(jnp.exp(tot)))


def kernel(x, c, norm1_w, norm2_w, w_mod, b_mod, w_in, gla_w_a2, gla_b_a, gla_norm_w, ret_norm_w, s5_a_re, s5_a_im, s5_log_dt, s5_b_re, s5_b_im, s5_c_re, s5_c_im, s5_d, s5_w_glu, s5_b_glu, w_out, router_w, router_b, w_up, b_up, w_down, b_down, final_norm_w):
    batch, seq, d = x.shape
    depth = w_mod.shape[0]
    t = batch * seq
    n_slots = t * TOP_K
    n_blocks = n_slots // ROW_BLK + N_EXPERTS
    n_rows = n_blocks * ROW_BLK

    mod = _modulation(c, w_mod, b_mod)
    ret_tabs = _retention_tables(seq)
    x2d = x.reshape(t, d)

    for i in range(depth):
        m6 = mod[i].reshape(batch, 6, 1, d)
        sh1, sc1, g1, sh2, sc2, g2 = (m6[:, j] for j in range(6))

        w_p = _take_cols(w_in[i], _IN_SRC).astype(BF16)
        wa_p = jnp.zeros((LANES, HP), F32).at[:GATE_RANK].set(_take_cols(gla_w_a2[i], _DK_SRC))
        ba_p = _take_cols(gla_b_a[i], _DK_SRC).reshape(1, HP)
        gnw = _take_cols(gla_norm_w[i], _DV_SRC).reshape(1, HP)
        rnw = _take_cols(ret_norm_w[i], _DV_SRC).reshape(1, HP)
        kv = N_HEADS * HEAD_DV
        wo_g = _take_rows(w_out[i, :kv], _DV_SRC).astype(BF16)
        wo_r = _take_rows(w_out[i, kv:2 * kv], _DV_SRC).astype(BF16)
        wo_s = w_out[i, 2 * kv:].astype(BF16)
        rw_t = router_w[i].T
        rw_hi = rw_t.astype(BF16)
        rw_p = jnp.concatenate([rw_hi, (rw_t - rw_hi.astype(F32)).astype(BF16)], axis=0)
        rb_p = jnp.broadcast_to(router_b[i][:, None], (N_EXPERTS, LANES))

        proj = _in_projection(x2d, sc1, sh1, norm1_w[i].reshape(1, d), w_p, seq)
        o_gla = _gla_mixer(proj, batch, seq, wa_p.astype(BF16), ba_p, gnw)
        o_ret = _ret_mixer(proj, batch, seq, *ret_tabs, rnw)
        s5_tabs = _s5_tables(s5_a_re[i], s5_a_im[i], s5_log_dt[i], s5_b_re[i], s5_b_im[i],
                             s5_c_re[i], s5_c_im[i], s5_d[i], s5_w_glu[i], s5_b_glu[i])
        o_s5 = _s5_mixer(proj, batch, seq, s5_tabs)

        x1, hdn, idx, gates, rank, counts = _outproj_router(
            o_gla, o_ret, o_s5, wo_g, wo_r, wo_s, x2d, g1, sc2, sh2,
            norm2_w[i].reshape(1, d), rw_p, rb_p, seq)

        cnt = counts[:, 0].astype(jnp.int32)
        padded = (cnt + ROW_BLK - 1) // ROW_BLK * ROW_BLK
        pad_ends = jnp.cumsum(padded)
        pad_starts = pad_ends - padded
        slot_start = jnp.sum(jnp.where(idx[:TOP_K, :, None] == jnp.arange(N_EXPERTS), pad_starts,
                                       0), axis=-1)
        dest_sm = (slot_start + rank[:TOP_K]).astype(jnp.int32).reshape(-1)
        gates_tm = jnp.pad(gates[:TOP_K].T, ((0, 0), (0, LANES - TOP_K)))
        blk_start = jnp.arange(n_blocks, dtype=jnp.int32) * ROW_BLK
        block_e = jnp.minimum(jnp.sum(pad_ends[None, :] <= blk_start[:, None], axis=1),
                              N_EXPERTS - 1).astype(jnp.int32)
        n_valid = jnp.clip((pad_starts + cnt)[block_e] - blk_start, 0, ROW_BLK).astype(jnp.int32)

        rows = _scatter_rows(hdn, dest_sm, n_rows)
        out_rows = _experts(i, block_e, n_valid, rows, w_up, b_up, w_down, b_down)
        y4 = _gather_rows(out_rows, dest_sm)
        x2d = _combine(y4, gates_tm, x1, g2, final_norm_w.reshape(1, d), seq,
                       final=(i == depth - 1))

    return x2d.reshape(batch, seq, d)
```

```python
import functools

import numpy as np
import jax
import jax.numpy as jnp
from jax import lax
from jax.experimental import pallas as pl
from jax.experimental.pallas import tpu as pltpu
from jax.experimental.pallas import tpu_sc as plsc

D_MODEL = 1024
CHUNK = 64
RET_CHUNK = 128
NORM_EPS = 1e-5
N_HEADS = 4
HEAD_DK = 48
HEAD_DV = 96
GATE_RANK = 16
GATE_TEMP = 16.0
ROPE_BASE = 10000.0
S5_WIDTH = 256
S5_GROUP_DIM = 16
S5_GROUPS = 16
S5_STATE = 64
N_EXPERTS = 32
TOP_K = 4
D_FF = 1024
SWIGLU_LIMIT = 7.0
SWIGLU_ALPHA = 1.702

LANES = 128
HEAD_PAD = LANES
HP = N_HEADS * HEAD_PAD
N_PAIRS = N_HEADS // 2
QKP = N_PAIRS * LANES
VMEM_LIMIT = 56 * 1024 * 1024

OFF_GQ, OFF_GK, OFF_GV, OFF_GG = 0, QKP, 2 * QKP, 2 * QKP + HP
OFF_RQ = OFF_GG + HP
OFF_RK, OFF_RV, OFF_RG = OFF_RQ + QKP, OFF_RQ + 2 * QKP, OFF_RQ + 2 * QKP + HP
OFF_SU = OFF_RG + HP
OFF_GA = OFF_SU + S5_WIDTH
NP_COLS = OFF_GA + LANES
PROJ_CH = 1152

ROW_BLK = 512

F32 = jnp.float32
BF16 = jnp.bfloat16


def _DK_SRC_LANE(h, d):
    return (h // 2) * LANES + (h % 2) * HEAD_DK + d


def _in_col_map():
    src = -np.ones((NP_COLS,), np.int64)
    kq = N_HEADS * HEAD_DK
    kv = N_HEADS * HEAD_DV
    base = dict(gq=0, gk=kq, gv=2 * kq, gg=2 * kq + kv, ga=2 * kq + 2 * kv)
    r0 = base['ga'] + GATE_RANK
    base.update(rq=r0, rk=r0 + kq, rv=r0 + 2 * kq, rg=r0 + 2 * kq + kv, su=r0 + 2 * kq + 2 * kv)
    for h in range(N_HEADS):
        for d in range(HEAD_DK):
            lane = _DK_SRC_LANE(h, d)
            src[OFF_GQ + lane] = base['gq'] + h * HEAD_DK + d
            src[OFF_GK + lane] = base['gk'] + h * HEAD_DK + d
            src[OFF_RQ + lane] = base['rq'] + h * HEAD_DK + d
            src[OFF_RK + lane] = base['rk'] + h * HEAD_DK + d
        for d in range(HEAD_DV):
            src[OFF_GV + h * HEAD_PAD + d] = base['gv'] + h * HEAD_DV + d
            src[OFF_GG + h * HEAD_PAD + d] = base['gg'] + h * HEAD_DV + d
            src[OFF_RV + h * HEAD_PAD + d] = base['rv'] + h * HEAD_DV + d
            src[OFF_RG + h * HEAD_PAD + d] = base['rg'] + h * HEAD_DV + d
    src[OFF_SU:OFF_SU + S5_WIDTH] = base['su'] + np.arange(S5_WIDTH)
    src[OFF_GA:OFF_GA + GATE_RANK] = base['ga'] + np.arange(GATE_RANK)
    return src


_IN_SRC = _in_col_map()


def _head_pad_map(width):
    src = -np.ones((HP,), np.int64)
    for h in range(N_HEADS):
        src[h * HEAD_PAD:h * HEAD_PAD + width] = h * width + np.arange(width)
    return src


_DV_SRC = _head_pad_map(HEAD_DV)
_DK_SRC = -np.ones((QKP,), np.int64)
for _h in range(N_HEADS):
    for _d in range(HEAD_DK):
        _DK_SRC[_DK_SRC_LANE(_h, _d)] = _h * HEAD_DK + _d


def _take_cols(w, src):
    out = jnp.take(w, jnp.asarray(np.maximum(src, 0)), axis=-1)
    return jnp.where(jnp.asarray(src >= 0), out, 0)


def _take_rows(w, src):
    out = jnp.take(w, jnp.asarray(np.maximum(src, 0)), axis=0)
    return jnp.where(jnp.asarray(src >= 0)[:, None], out, 0)


def _cparams(sem):
    return pltpu.CompilerParams(dimension_semantics=sem, vmem_limit_bytes=VMEM_LIMIT)


def _mod_kernel(c_ref, w_ref, b_ref, o_ref):
    c = c_ref[...]
    cond = c * jax.nn.sigmoid(c)
    o_ref[0] = jnp.dot(cond, w_ref[0], preferred_element_type=F32,
                       precision=lax.Precision.HIGHEST) + b_ref[0]


def _modulation(c, w_mod, b_mod):
    depth, d, n = w_mod.shape
    b = c.shape[0]
    nb = 1536
    return pl.pallas_call(
        _mod_kernel,
        grid=(depth, n // nb),
        in_specs=[pl.BlockSpec((b, d), lambda l, j: (0, 0)),
                  pl.BlockSpec((1, d, nb), lambda l, j: (l, 0, j)),
                  pl.BlockSpec((1, 1, nb), lambda l, j: (l, 0, j))],
        out_specs=pl.BlockSpec((1, b, nb), lambda l, j: (l, 0, j)),
        out_shape=jax.ShapeDtypeStruct((depth, b, n), F32),
        compiler_params=_cparams(("arbitrary", "arbitrary")),
        name="adaln_mod",
    )(c, w_mod, b_mod.reshape(depth, 1, n))


def _rms_mod(x, nw, sc, sh):
    y = x * lax.rsqrt(jnp.mean(x * x, axis=-1, keepdims=True) + NORM_EPS)
    return (y * nw) * (1.0 + sc) + sh


def _inproj_kernel(x_ref, sc_ref, sh_ref, nw_ref, w_ref, o_ref):
    h = _rms_mod(x_ref[...], nw_ref[...], sc_ref[0], sh_ref[0]).astype(BF16)
    for j in range(NP_COLS // PROJ_CH):
        cs = slice(j * PROJ_CH, (j + 1) * PROJ_CH)
        o_ref[:, cs] = jnp.dot(h, w_ref[:, cs], preferred_element_type=F32).astype(BF16)


def _in_projection(x2d, sc, sh, nw, w_p, seq, tm=512):
    t, d = x2d.shape
    per_b = seq // tm
    return pl.pallas_call(
        _inproj_kernel,
        grid=(t // tm,),
        in_specs=[pl.BlockSpec((tm, d), lambda i: (i, 0)),
                  pl.BlockSpec((1, 1, d), lambda i: (i // per_b, 0, 0)),
                  pl.BlockSpec((1, 1, d), lambda i: (i // per_b, 0, 0)),
                  pl.BlockSpec((1, d), lambda i: (0, 0)),
                  pl.BlockSpec((d, NP_COLS), lambda i: (0, 0))],
        out_specs=pl.BlockSpec((tm, NP_COLS), lambda i: (i, 0)),
        out_shape=jax.ShapeDtypeStruct((t, NP_COLS), BF16),
        compiler_params=_cparams(("arbitrary",)),
        name="in_proj",
    )(x2d, sc, sh, nw, w_p)


_NT = (((1,), (1,)), ((), ()))
_TN = (((0,), (0,)), ((), ()))


def _tri_mask(n=CHUNK):
    r = lax.broadcasted_iota(jnp.int32, (n, n), 0)
    c = lax.broadcasted_iota(jnp.int32, (n, n), 1)
    return r >= c


def _pair_masks(rows):
    lane = lax.broadcasted_iota(jnp.int32, (rows, LANES), 1)
    return lane < HEAD_DK, (lane >= HEAD_DK) & (lane < 2 * HEAD_DK)


def _head_attention(qd, ki, ke, vh, et, st_ref, h, causal):
    qb = qd.astype(BF16)
    sc = lax.dot_general(qb, ki.astype(BF16), _NT, preferred_element_type=F32)
    sc = jnp.where(causal, sc, 0.0)
    st = st_ref[h]
    o = jnp.dot(sc.astype(BF16), vh, preferred_element_type=F32)
    o = o + lax.dot_general(qb, st.astype(BF16), _NT, preferred_element_type=F32)
    st_ref[h] = st * et + lax.dot_general(vh, ke.astype(BF16), _TN, preferred_element_type=F32)
    return o


def _gla_kernel(q_ref, k_ref, v_ref, g_ref, a_ref, wa_ref, ba_ref, nw_ref, tri_ref, o_ref,
                st_ref, qd_s, ki_s, ke_s, et_s):
    @pl.when(pl.program_id(1) == 0)
    def _():
        st_ref[...] = jnp.zeros_like(st_ref)

    causal = _tri_mask()
    tl = q_ref.shape[0]
    n_chunks = tl // CHUNK

    z = jnp.dot(a_ref[...], wa_ref[...], preferred_element_type=F32) + ba_ref[...]
    la = (jnp.minimum(z, 0.0) - jnp.log1p(jnp.exp(-jnp.abs(z)))) * (1.0 / GATE_TEMP)
    hi = la.astype(BF16)
    lo = (la - hi.astype(F32)).astype(BF16)
    cum = (jnp.dot(tri_ref[...], hi, preferred_element_type=F32)
           + jnp.dot(tri_ref[...], lo, preferred_element_type=F32))
    cum3 = cum.reshape(n_chunks, CHUNK, QKP)
    tot3 = cum3[:, CHUNK - 1:CHUNK, :]
    qd = (q_ref[...].astype(F32) * (HEAD_DK ** -0.5)) * jnp.exp(cum)
    masks = _pair_masks(tl)
    for h in range(N_HEADS):
        pair = slice((h // 2) * LANES, (h // 2 + 1) * LANES)
        qd_s[:, h * HEAD_PAD:(h + 1) * HEAD_PAD] = jnp.where(masks[h % 2], qd[:, pair],
                                                              0.0).astype(BF16)
    kf = k_ref[...].astype(F32)
    ki_s[...] = (kf * jnp.exp(-cum)).astype(BF16)
    ke_s[...] = (kf * jnp.exp(tot3 - cum3).reshape(tl, QKP)).astype(BF16)
    et_s[...] = jnp.exp(tot3).reshape(n_chunks, QKP)

    def chunk(c, carry):
        r = pl.ds(pl.multiple_of(c * CHUNK, CHUNK), CHUNK)
        et = et_s[pl.ds(c, 1), :]
        for h in range(N_HEADS):
            sl = slice(h * HEAD_PAD, (h + 1) * HEAD_PAD)
            pair = slice((h // 2) * LANES, (h // 2 + 1) * LANES)
            o = _head_attention(qd_s[r, sl], ki_s[r, pair], ke_s[r, pair], v_ref[r, sl],
                                et[:, pair], st_ref, h, causal)
            ms = jnp.sum(o * o, axis=-1, keepdims=True) * (1.0 / HEAD_DV)
            y = (o * lax.rsqrt(ms + NORM_EPS)) * nw_ref[:, sl]
            g = g_ref[r, sl].astype(F32)
            o_ref[r, sl] = (y * (g * jax.nn.sigmoid(g))).astype(BF16)
        return carry

    lax.fori_loop(0, n_chunks, chunk, 0, unroll=4)


def _ret_kernel(q_ref, k_ref, v_ref, g_ref, cos_ref, sina_ref, sinb_ref, dq_ref, dki_ref,
                dke_ref, dt_ref, nw_ref, o_ref, st_ref):
    @pl.when(pl.program_id(1) == 0)
    def _():
        st_ref[...] = jnp.zeros_like(st_ref)

    causal = _tri_mask(RET_CHUNK)
    n_chunks = q_ref.shape[0] // RET_CHUNK
    lane = lax.broadcasted_iota(jnp.int32, (RET_CHUNK, HEAD_PAD), 1)
    real = lane < HEAD_DV
    masks = _pair_masks(RET_CHUNK)
    half = HEAD_DK // 2

    def rotary(t, cos, sina, sinb):
        return (t * cos + pltpu.roll(t, LANES - half, 1) * sina + pltpu.roll(t, half, 1) * sinb)

    def chunk(c, carry):
        r = pl.ds(pl.multiple_of(c * RET_CHUNK, RET_CHUNK), RET_CHUNK)
        cos, sina, sinb = cos_ref[r, :], sina_ref[r, :], sinb_ref[r, :]
        pair_q, pair_ki, pair_ke = [], [], []
        for p in range(N_PAIRS):
            ps = slice(p * LANES, (p + 1) * LANES)
            qr = rotary(q_ref[r, ps].astype(F32), cos, sina, sinb) * dq_ref[:, ps]
            kr = rotary(k_ref[r, ps].astype(F32), cos, sina, sinb) * (HEAD_DK ** -0.5)
            pair_q.append(qr)
            pair_ki.append((kr * dki_ref[:, ps]).astype(BF16))
            pair_ke.append((kr * dke_ref[:, ps]).astype(BF16))
        for h in range(N_HEADS):
            sl = slice(h * HEAD_PAD, (h + 1) * HEAD_PAD)
            p = h // 2
            qd = jnp.where(masks[h % 2], pair_q[p], 0.0)
            o = _head_attention(qd, pair_ki[p], pair_ke[p], v_ref[r, sl],
                                dt_ref[:, p * LANES:(p + 1) * LANES], st_ref, h, causal)
            mu = jnp.sum(o, axis=-1, keepdims=True) * (1.0 / HEAD_DV)
            oc = jnp.where(real, o - mu, 0.0)
            var = jnp.sum(oc * oc, axis=-1, keepdims=True) * (1.0 / HEAD_DV)
            y = (oc * lax.rsqrt(var + NORM_EPS)) * nw_ref[:, sl]
            g = g_ref[r, sl].astype(F32)
            o_ref[r, sl] = (y * (g * jax.nn.sigmoid(g))).astype(BF16)
        return carry

    lax.fori_loop(0, n_chunks, chunk, 0, unroll=2)


def _proj_spec(tl, width, col_off, per_b):
    cb = col_off // width
    return pl.BlockSpec((tl, width), lambda b, l: (b * per_b + l, cb))


def _full(shape):
    return pl.BlockSpec(shape, lambda b, l: (0,) * len(shape))


def _gla_mixer(proj, batch, seq, wa_p, ba_p, nw_p, tl=512):
    per_b = seq // tl
    pos = np.arange(tl)
    tri_bd = jnp.asarray((pos[:, None] // CHUNK == pos[None, :] // CHUNK)
                         & (pos[:, None] >= pos[None, :]), BF16)
    return pl.pallas_call(
        _gla_kernel,
        grid=(batch, per_b),
        in_specs=[_proj_spec(tl, QKP, OFF_GQ, per_b), _proj_spec(tl, QKP, OFF_GK, per_b),
                  _proj_spec(tl, HP, OFF_GV, per_b), _proj_spec(tl, HP, OFF_GG, per_b),
                  _proj_spec(tl, LANES, OFF_GA, per_b),
                  _full((LANES, QKP)), _full((1, QKP)), _full((1, HP)), _full((tl, tl))],
        out_specs=pl.BlockSpec((tl, HP), lambda b, l: (b * per_b + l, 0)),
        out_shape=jax.ShapeDtypeStruct((batch * seq, HP), BF16),
        scratch_shapes=[pltpu.VMEM((N_HEADS, HEAD_PAD, HEAD_PAD), F32),
                        pltpu.VMEM((tl, HP), BF16), pltpu.VMEM((tl, QKP), BF16),
                        pltpu.VMEM((tl, QKP), BF16), pltpu.VMEM((tl // CHUNK, QKP), F32)],
        compiler_params=_cparams(("arbitrary", "arbitrary")),
        name="gla_mixer",
    )(proj, proj, proj, proj, proj, wa_p, ba_p, nw_p, tri_bd)


def _ret_mixer(proj, batch, seq, cos_t, sina_t, sinb_t, dq, dki, dke, dtot, nw_p, tl=512):
    per_b = seq // tl
    return pl.pallas_call(
        _ret_kernel,
        grid=(batch, per_b),
        in_specs=[_proj_spec(tl, QKP, OFF_RQ, per_b), _proj_spec(tl, QKP, OFF_RK, per_b),
                  _proj_spec(tl, HP, OFF_RV, per_b), _proj_spec(tl, HP, OFF_RG, per_b),
                  pl.BlockSpec((tl, LANES), lambda b, l: (l, 0)),
                  pl.BlockSpec((tl, LANES), lambda b, l: (l, 0)),
                  pl.BlockSpec((tl, LANES), lambda b, l: (l, 0)),
                  _full((RET_CHUNK, QKP)), _full((RET_CHUNK, QKP)), _full((RET_CHUNK, QKP)),
                  _full((1, QKP)), _full((1, HP))],
        out_specs=pl.BlockSpec((tl, HP), lambda b, l: (b * per_b + l, 0)),
        out_shape=jax.ShapeDtypeStruct((batch * seq, HP), BF16),
        scratch_shapes=[pltpu.VMEM((N_HEADS, HEAD_PAD, HEAD_PAD), F32)],
        compiler_params=_cparams(("arbitrary", "arbitrary")),
        name="ret_mixer",
    )(proj, proj, proj, proj, cos_t, sina_t, sinb_t, dq, dki, dke, dtot, nw_p)


S5_N = S5_GROUPS * S5_STATE


def _gelu_tanh(x):
    return 0.5 * x * (1.0 + jnp.tanh(np.sqrt(2.0 / np.pi) * (x + 0.044715 * (x * x * x))))


def _s5_kernel(u_ref, bb_ref, cb_ref, nr_ref, ni_ref, pr_ref, pi_ref, lr_ref, li_ref,
               d_ref, wg_ref, bg_ref, o_ref, sr_ref, si_ref, x_scr, s_scr):
    @pl.when(pl.program_id(1) == 0)
    def _():
        sr_ref[...] = jnp.zeros_like(sr_ref)
        si_ref[...] = jnp.zeros_like(si_ref)

    tri = _tri_mask().astype(BF16)
    n_chunks = u_ref.shape[0] // CHUNK
    u = u_ref[...]
    x_scr[...] = jnp.dot(u, bb_ref[...], preferred_element_type=F32)

    def chunk(c, carry):
        r = pl.ds(pl.multiple_of(c * CHUNK, CHUNK), CHUNK)
        xr, xi = x_scr[r, :S5_N], x_scr[r, S5_N:]
        nr, ni = nr_ref[...], ni_ref[...]
        p_r = jnp.dot(tri, (xr * nr - xi * ni).astype(BF16), preferred_element_type=F32)
        p_i = jnp.dot(tri, (xr * ni + xi * nr).astype(BF16), preferred_element_type=F32)
        s0r, s0i = sr_ref[...], si_ref[...]
        lr, li = lr_ref[...], li_ref[...]
        q_r = p_r + (s0r * lr - s0i * li)
        q_i = p_i + (s0r * li + s0i * lr)
        pr, pi = pr_ref[...], pi_ref[...]
        s_r = q_r * pr - q_i * pi
        s_i = q_r * pi + q_i * pr
        sr_ref[...] = s_r[CHUNK - 1:CHUNK, :]
        si_ref[...] = s_i[CHUNK - 1:CHUNK, :]
        s_scr[r, :S5_N] = s_r.astype(BF16)
        s_scr[r, S5_N:] = s_i.astype(BF16)
        return carry

    lax.fori_loop(0, n_chunks, chunk, 0, unroll=2)
    y = jnp.dot(s_scr[...], cb_ref[...], preferred_element_type=F32)
    y = _gelu_tanh(y + d_ref[...] * u.astype(F32))
    gate = jnp.dot(y.astype(BF16), wg_ref[...], preferred_element_type=F32) + bg_ref[...]
    o_ref[...] = (y * jax.nn.sigmoid(gate)).astype(BF16)


def _s5_mixer(proj, batch, seq, tabs, tl=512):
    per_b = seq // tl
    bb, cb, nr, ni, pr, pi, lr, li, dsk, wg, bg = tabs
    return pl.pallas_call(
        _s5_kernel,
        grid=(batch, per_b),
        in_specs=[_proj_spec(tl, S5_WIDTH, OFF_SU, per_b),
                  _full((S5_WIDTH, 2 * S5_N)), _full((2 * S5_N, S5_WIDTH)),
                  _full((CHUNK, S5_N)), _full((CHUNK, S5_N)),
                  _full((CHUNK, S5_N)), _full((CHUNK, S5_N)),
                  _full((1, S5_N)), _full((1, S5_N)),
                  _full((1, S5_WIDTH)), _full((S5_WIDTH, S5_WIDTH)), _full((1, S5_WIDTH))],
        out_specs=pl.BlockSpec((tl, S5_WIDTH), lambda b, l: (b * per_b + l, 0)),
        out_shape=jax.ShapeDtypeStruct((batch * seq, S5_WIDTH), BF16),
        scratch_shapes=[pltpu.VMEM((1, S5_N), F32), pltpu.VMEM((1, S5_N), F32),
                        pltpu.VMEM((tl, 2 * S5_N), F32), pltpu.VMEM((tl, 2 * S5_N), BF16)],
        compiler_params=_cparams(("arbitrary", "arbitrary")),
        name="s5_mixer",
    )(proj, bb, cb, nr, ni, pr, pi, lr, li, dsk, wg, bg)


def _s5_tables(a_re, a_im, log_dt, b_re, b_im, c_re, c_im, d_skip, w_glu, b_glu):
    lam = lax.complex(a_re, a_im)
    dt = jnp.exp(log_dt)[:, None]
    lam_bar = jnp.exp(lam * dt)
    b_bar = ((lam_bar - 1.0) / lam)[..., None] * lax.complex(b_re, b_im)
    eye = jnp.eye(S5_GROUPS, dtype=F32)
    def blk_b(m):
        return jnp.einsum('gph,gk->ghkp', m, eye).reshape(S5_WIDTH, S5_N)
    bb = jnp.concatenate([blk_b(jnp.real(b_bar)), blk_b(jnp.imag(b_bar))], axis=1)
    def blk_c(m):
        return jnp.einsum('ghp,gk->kpgh', m, eye).reshape(S5_N, S5_WIDTH)
    cb = jnp.concatenate([blk_c(c_re), blk_c(-c_im)], axis=0)
    steps = jnp.arange(CHUNK, dtype=F32)[:, None, None]
    lam_dt = (lam * dt)[None]
    pos = jnp.exp(lam_dt * steps).reshape(CHUNK, S5_N)
    neg = jnp.exp(-lam_dt * steps).reshape(CHUNK, S5_N)
    one = lam_bar.reshape(1, S5_N)
    return (bb.astype(BF16), cb.astype(BF16), jnp.real(neg), jnp.imag(neg), jnp.real(pos),
            jnp.imag(pos), jnp.real(one), jnp.imag(one), d_skip.reshape(1, S5_WIDTH),
            w_glu.astype(BF16), b_glu.reshape(1, S5_WIDTH))


def _pack_bf16_pairs(x):
    w = x.shape[1] // 2
    xb = x.astype(BF16).astype(F32)
    hi = lax.bitcast_convert_type(xb[:, :w], jnp.uint32)
    lo = lax.bitcast_convert_type(xb[:, w:], jnp.uint32)
    return hi | (lo >> 16)


def _unpack_bf16_pairs(p):
    hi = lax.bitcast_convert_type(p & jnp.uint32(0xFFFF0000), F32)
    lo = lax.bitcast_convert_type(p << 16, F32)
    return hi, lo


def _router_kernel(og_ref, or_ref, os_ref, wg_ref, wr_ref, ws_ref, x_ref, g1_ref, sc_ref,
                   sh_ref, nw_ref, rw_ref, rb_ref,
                   x1_ref, h_ref, idx_ref, gate_ref, rank_ref, cnt_ref, carry_ref):
    i = pl.program_id(0)

    @pl.when(i == 0)
    def _():
        carry_ref[...] = jnp.zeros_like(carry_ref)

    mix = (jnp.dot(og_ref[...], wg_ref[...], preferred_element_type=F32)
           + jnp.dot(or_ref[...], wr_ref[...], preferred_element_type=F32)
           + jnp.dot(os_ref[...], ws_ref[...], preferred_element_type=F32))
    x1 = x_ref[...] + g1_ref[0] * mix
    x1_ref[...] = x1
    hdn = _rms_mod(x1, nw_ref[...], sc_ref[0], sh_ref[0])
    h_ref[...] = _pack_bf16_pairs(hdn)
    h_hi = hdn.astype(BF16)
    h_lo = (hdn - h_hi.astype(F32)).astype(BF16)
    p = lax.dot_general(rw_ref[...], h_hi, _NT, preferred_element_type=F32)
    q = lax.dot_general(rw_ref[:N_EXPERTS, :], h_lo, _NT, preferred_element_type=F32)
    logits = (p[:N_EXPERTS] + p[N_EXPERTS:] + q) + rb_ref[:, 0:1]
    tm = logits.shape[1]
    eidx = lax.broadcasted_iota(jnp.int32, (N_EXPERTS, tm), 0)
    work = logits
    onehot = jnp.zeros((N_EXPERTS, tm), F32)
    vals, idxs, sels = [], [], []
    for _ in range(TOP_K):
        m = jnp.max(work, axis=0, keepdims=True)
        ix = jnp.min(jnp.where(work == m, eidx, N_EXPERTS), axis=0, keepdims=True)
        sel = eidx == ix
        work = jnp.where(sel, -jnp.inf, work)
        onehot = onehot + sel.astype(F32)
        vals.append(m)
        idxs.append(ix)
        sels.append(sel)
    exps = [jnp.exp(v - vals[0]) for v in vals]
    denom = exps[0] + exps[1] + exps[2] + exps[3]
    r = lax.broadcasted_iota(jnp.int32, (tm, tm), 0)
    c = lax.broadcasted_iota(jnp.int32, (tm, tm), 1)
    earlier = (r < c).astype(BF16)
    before = (jnp.dot(onehot.astype(BF16), earlier, preferred_element_type=F32)
              + carry_ref[:, 0:1])
    row8 = lax.broadcasted_iota(jnp.int32, (8, tm), 0)
    idx_out = jnp.zeros((8, tm), jnp.int32)
    gate_out = jnp.zeros((8, tm), F32)
    rank_out = jnp.zeros((8, tm), F32)
    for k in range(TOP_K):
        rk = jnp.sum(jnp.where(sels[k], before, 0.0), axis=0, keepdims=True)
        idx_out = jnp.where(row8 == k, idxs[k], idx_out)
        gate_out = jnp.where(row8 == k, exps[k] / denom, gate_out)
        rank_out = jnp.where(row8 == k, rk, rank_out)
    idx_ref[...] = idx_out
    gate_ref[...] = gate_out
    rank_ref[...] = rank_out.astype(jnp.int32)
    total = carry_ref[...] + jnp.sum(onehot, axis=1, keepdims=True)
    carry_ref[...] = total
    cnt_ref[...] = total


def _outproj_router(o_gla, o_ret, o_s5, wg, wr, ws, x2d, g1, sc2, sh2, nw2, rw_p, rb_p, seq,
                    tm=512):
    t, d = x2d.shape
    per_b = seq // tm
    row = lambda w: pl.BlockSpec((tm, w), lambda i: (i, 0))
    full = lambda s: pl.BlockSpec(s, lambda i: (0,) * len(s))
    per_batch = pl.BlockSpec((1, 1, d), lambda i: (i // per_b, 0, 0))
    slot_t = pl.BlockSpec((8, tm), lambda i: (0, i))
    return pl.pallas_call(
        _router_kernel,
        grid=(t // tm,),
        in_specs=[row(HP), row(HP), row(S5_WIDTH), full((HP, d)), full((HP, d)),
                  full((S5_WIDTH, d)), row(d), per_batch, per_batch, per_batch,
                  full((1, d)), full((2 * N_EXPERTS, d)), full((N_EXPERTS, LANES))],
        out_specs=[row(d), row(d // 2), slot_t, slot_t, slot_t, full((N_EXPERTS, LANES))],
        out_shape=[jax.ShapeDtypeStruct((t, d), F32),
                   jax.ShapeDtypeStruct((t, d // 2), jnp.uint32),
                   jax.ShapeDtypeStruct((8, t), jnp.int32),
                   jax.ShapeDtypeStruct((8, t), F32),
                   jax.ShapeDtypeStruct((8, t), jnp.int32),
                   jax.ShapeDtypeStruct((N_EXPERTS, LANES), F32)],
        scratch_shapes=[pltpu.VMEM((N_EXPERTS, LANES), F32)],
        compiler_params=_cparams(("arbitrary",)),
        name="outproj_router",
    )(o_gla, o_ret, o_s5, wg, wr, ws, x2d, g1, sc2, sh2, nw2, rw_p, rb_p)


GATHER_WIN = 64


def _gather_rows(table, idx):
    m = idx.shape[0]
    w = table.shape[1]
    mesh = plsc.VectorSubcoreMesh(core_axis_name="core", subcore_axis_name="subcore")

    @functools.partial(pl.kernel, out_type=jax.ShapeDtypeStruct((m, w), table.dtype),
                       mesh=mesh, name="sc_row_gather")
    def gather(x_hbm, i_hbm, o_hbm):
        def body(i_vmem, o_vmem):
            pltpu.sync_copy(x_hbm.at[i_vmem], o_vmem)

        pltpu.emit_pipeline(
            body,
            grid=(m // GATHER_WIN,),
            in_specs=[pl.BlockSpec((GATHER_WIN,), lambda i: (i,))],
            out_specs=[pl.BlockSpec((GATHER_WIN, w), lambda i: (i, 0))],
            core_axis_name=("core", "subcore"),
            dimension_semantics=(pltpu.PARALLEL,),
        )(i_hbm, o_hbm)

    return gather(table, idx)


def _scatter_rows(x, dest_slot_major, n_rows):
    t, w = x.shape
    steps = t // GATHER_WIN
    mesh = plsc.VectorSubcoreMesh(core_axis_name="core", subcore_axis_name="subcore")

    @functools.partial(pl.kernel, out_type=jax.ShapeDtypeStruct((n_rows, w), x.dtype),
                       mesh=mesh, name="sc_row_scatter")
    def scatter(x_hbm, i_hbm, o_hbm):
        def body(x_vmem, i0, i1, i2, i3):
            for i_vmem in (i0, i1, i2, i3):
                pltpu.sync_copy(x_vmem, o_hbm.at[i_vmem])

        slot = lambda k: pl.BlockSpec((GATHER_WIN,), lambda i: (k * steps + i,))
        pltpu.emit_pipeline(
            body,
            grid=(steps,),
            in_specs=[pl.BlockSpec((GATHER_WIN, w), lambda i: (i, 0)),
                      slot(0), slot(1), slot(2), slot(3)],
            out_specs=[],
            core_axis_name=("core", "subcore"),
            dimension_semantics=(pltpu.PARALLEL,),
        )(x_hbm, i_hbm, i_hbm, i_hbm, i_hbm)

    return scatter(x, dest_slot_major)


def _expert_kernel(be_ref, nv_ref, rows_ref, wu_ref, bu_ref, wd_ref, bd_ref, o_ref,
                   wu_bf, wd_bf):
    i = pl.program_id(0)
    e = be_ref[i]
    prev = be_ref[jnp.maximum(i - 1, 0)]

    @pl.when((i == 0) | (e != prev))
    def _():
        wu_bf[...] = wu_ref[0, 0].astype(BF16)
        wd_bf[...] = wd_ref[0, 0].astype(BF16)

    @pl.when(nv_ref[i] > 0)
    def _():
        row = lax.broadcasted_iota(jnp.int32, rows_ref.shape, 0)
        x_hi, x_lo = _unpack_bf16_pairs(jnp.where(row < nv_ref[i], rows_ref[...], jnp.uint32(0)))
        half = D_MODEL // 2
        up = (jnp.dot(x_hi.astype(BF16), wu_bf[:half, :], preferred_element_type=F32)
              + jnp.dot(x_lo.astype(BF16), wu_bf[half:, :], preferred_element_type=F32)
              + bu_ref[0, 0])
        x_glu = jnp.minimum(up[:, :D_FF], SWIGLU_LIMIT)
        x_lin = jnp.clip(up[:, D_FF:], -SWIGLU_LIMIT, SWIGLU_LIMIT)
        act = x_glu * jax.nn.sigmoid(SWIGLU_ALPHA * x_glu) * (x_lin + 1.0)
        o_ref[...] = _pack_bf16_pairs(
            jnp.dot(act.astype(BF16), wd_bf[...], preferred_element_type=F32) + bd_ref[0, 0])

    @pl.when(nv_ref[i] <= 0)
    def _():
        o_ref[...] = jnp.zeros_like(o_ref)


def _experts(layer, block_e, n_valid, rows, w_up, b_up, w_down, b_down):
    n_rows, dh = rows.shape
    d = 2 * dh
    n_blocks = n_rows // ROW_BLK
    depth, ne, _, f2 = w_up.shape
    wsel = lambda i, be, nu: (layer, be[i], 0, 0)
    grid_spec = pltpu.PrefetchScalarGridSpec(
        num_scalar_prefetch=2,
        grid=(n_blocks,),
        in_specs=[pl.BlockSpec((ROW_BLK, dh), lambda i, be, nu: (i, 0)),
                  pl.BlockSpec((1, 1, d, f2), wsel),
                  pl.BlockSpec((1, 1, 1, f2), wsel),
                  pl.BlockSpec((1, 1, D_FF, d), wsel),
                  pl.BlockSpec((1, 1, 1, d), wsel)],
        out_specs=pl.BlockSpec((ROW_BLK, dh), lambda i, be, nu: (i, 0)),
        scratch_shapes=[pltpu.VMEM((d, f2), BF16), pltpu.VMEM((D_FF, d), BF16)],
    )
    return pl.pallas_call(
        _expert_kernel,
        grid_spec=grid_spec,
        out_shape=jax.ShapeDtypeStruct((n_rows, dh), jnp.uint32),
        compiler_params=_cparams(("arbitrary",)),
        name="moe_experts",
    )(block_e, n_valid, rows, w_up, b_up.reshape(depth, ne, 1, f2), w_down,
      b_down.reshape(depth, ne, 1, d))


def _combine_kernel(y0_ref, y1_ref, y2_ref, y3_ref, gate_ref, x1_ref, g2_ref, fw_ref, o_ref,
                    *, final):
    gates = gate_ref[...]
    y_hi, y_lo = None, None
    for k, y_ref in enumerate((y0_ref, y1_ref, y2_ref, y3_ref)):
        hi, lo = _unpack_bf16_pairs(y_ref[...])
        g = gates[:, k:k + 1]
        y_hi = g * hi if y_hi is None else y_hi + g * hi
        y_lo = g * lo if y_lo is None else y_lo + g * lo
    y = jnp.concatenate([y_hi, y_lo], axis=1)
    x2 = x1_ref[...] + g2_ref[0] * y
    if final:
        x2 = (x2 * lax.rsqrt(jnp.mean(x2 * x2, axis=-1, keepdims=True) + NORM_EPS)) * fw_ref[...]
    o_ref[...] = x2


def _combine(y4, gates, x1, g2, fw, seq, final, th=256):
    t, d = x1.shape
    steps = t // th
    per_b = seq // th
    slot = lambda k: pl.BlockSpec((th, d // 2), lambda i: (k * steps + i, 0))
    return pl.pallas_call(
        functools.partial(_combine_kernel, final=final),
        grid=(steps,),
        in_specs=[slot(0), slot(1), slot(2), slot(3),
                  pl.BlockSpec((th, LANES), lambda i: (i, 0)),
                  pl.BlockSpec((th, d), lambda i: (i, 0)),
                  pl.BlockSpec((1, 1, d), lambda i: (i // per_b, 0, 0)),
                  pl.BlockSpec((1, d), lambda i: (0, 0))],
        out_specs=pl.BlockSpec((th, d), lambda i: (i, 0)),
        out_shape=jax.ShapeDtypeStruct((t, d), F32),
        compiler_params=_cparams(("arbitrary",)),
        name="moe_combine",
    )(y4, y4, y4, y4, gates, x1, g2, fw)


def _retention_tables(seq):
    pos = jnp.arange(seq, dtype=F32)
    inv_freq = ROPE_BASE ** (-jnp.arange(0, HEAD_DK, 2, dtype=F32) / HEAD_DK)
    ang = pos[:, None] * inv_freq[None, :]
    cos, sin = jnp.cos(ang), jnp.sin(ang)
    zero = jnp.zeros_like(sin)
    zpad = jnp.zeros((seq, LANES - 2 * HEAD_DK), F32)
    cos_t = jnp.concatenate([cos, cos, cos, cos, zpad], axis=1)
    sina_t = jnp.concatenate([-sin, zero, -sin, zero, zpad], axis=1)
    sinb_t = jnp.concatenate([zero, sin, zero, sin, zpad], axis=1)
    log_gamma = jnp.log1p(-jnp.exp2(-5.0 - jnp.arange(N_HEADS, dtype=F32)))
    log_decay = jnp.broadcast_to(log_gamma[None, :, None], (RET_CHUNK, N_HEADS, HEAD_DK))
    cum = jnp.cumsum(log_decay, axis=0)
    tot = cum[-1:]
    shp = lambda a: _take_cols(a.reshape(a.shape[0], N_HEADS * HEAD_DK), _DK_SRC) + jnp.asarray(
        _DK_SRC < 0, F32)
    return (cos_t, sina_t, sinb_t, shp(jnp.exp(cum)), shp(jnp.exp(-cum)), shp(jnp.exp(tot - cum)),
            shp(jnp.exp(tot)))


def kernel(x, c, norm1_w, norm2_w, w_mod, b_mod, w_in, gla_w_a2, gla_b_a, gla_norm_w, ret_norm_w, s5_a_re, s5_a_im, s5_log_dt, s5_b_re, s5_b_im, s5_c_re, s5_c_im, s5_d, s5_w_glu, s5_b_glu, w_out, router_w, router_b, w_up, b_up, w_down, b_down, final_norm_w):
    batch, seq, d = x.shape
    depth = w_mod.shape[0]
    t = batch * seq
    n_slots = t * TOP_K
    n_blocks = n_slots // ROW_BLK + N_EXPERTS
    n_rows = n_blocks * ROW_BLK

    mod = _modulation(c, w_mod, b_mod)
    ret_tabs = _retention_tables(seq)
    x2d = x.reshape(t, d)

    for i in range(depth):
        m6 = mod[i].reshape(batch, 6, 1, d)
        sh1, sc1, g1, sh2, sc2, g2 = (m6[:, j] for j in range(6))

        w_p = _take_cols(w_in[i], _IN_SRC).astype(BF16)
        wa_p = jnp.zeros((LANES, QKP), F32).at[:GATE_RANK].set(_take_cols(gla_w_a2[i], _DK_SRC))
        ba_p = _take_cols(gla_b_a[i], _DK_SRC).reshape(1, QKP)
        gnw = _take_cols(gla_norm_w[i], _DV_SRC).reshape(1, HP)
        rnw = _take_cols(ret_norm_w[i], _DV_SRC).reshape(1, HP)
        kv = N_HEADS * HEAD_DV
        wo_g = _take_rows(w_out[i, :kv], _DV_SRC).astype(BF16)
        wo_r = _take_rows(w_out[i, kv:2 * kv], _DV_SRC).astype(BF16)
        wo_s = w_out[i, 2 * kv:].astype(BF16)
        rw_t = router_w[i].T
        rw_hi = rw_t.astype(BF16)
        rw_p = jnp.concatenate([rw_hi, (rw_t - rw_hi.astype(F32)).astype(BF16)], axis=0)
        rb_p = jnp.broadcast_to(router_b[i][:, None], (N_EXPERTS, LANES))

        proj = _in_projection(x2d, sc1, sh1, norm1_w[i].reshape(1, d), w_p, seq)
        o_gla = _gla_mixer(proj, batch, seq, wa_p.astype(BF16), ba_p, gnw)
        o_ret = _ret_mixer(proj, batch, seq, *ret_tabs, rnw)
        s5_tabs = _s5_tables(s5_a_re[i], s5_a_im[i], s5_log_dt[i], s5_b_re[i], s5_b_im[i],
                             s5_c_re[i], s5_c_im[i], s5_d[i], s5_w_glu[i], s5_b_glu[i])
        o_s5 = _s5_mixer(proj, batch, seq, s5_tabs)

        x1, hdn, idx, gates, rank, counts = _outproj_router(
            o_gla, o_ret, o_s5, wo_g, wo_r, wo_s, x2d, g1, sc2, sh2,
            norm2_w[i].reshape(1, d), rw_p, rb_p, seq)

        cnt = counts[:, 0].astype(jnp.int32)
        padded = (cnt + ROW_BLK - 1) // ROW_BLK * ROW_BLK
        pad_ends = jnp.cumsum(padded)
        pad_starts = pad_ends - padded
        slot_start = jnp.sum(jnp.where(idx[:TOP_K, :, None] == jnp.arange(N_EXPERTS), pad_starts,
                                       0), axis=-1)
        dest_sm = (slot_start + rank[:TOP_K]).astype(jnp.int32).reshape(-1)
        gates_tm = jnp.pad(gates[:TOP_K].T, ((0, 0), (0, LANES - TOP_K)))
        blk_start = jnp.arange(n_blocks, dtype=jnp.int32) * ROW_BLK
        block_e = jnp.minimum(jnp.sum(pad_ends[None, :] <= blk_start[:, None], axis=1),
                              N_EXPERTS - 1).astype(jnp.int32)
        n_valid = jnp.clip((pad_starts + cnt)[block_e] - blk_start, 0, ROW_BLK).astype(jnp.int32)

        rows = _scatter_rows(hdn, dest_sm, n_rows)
        out_rows = _experts(i, block_e, n_valid, rows, w_up, b_up, w_down, b_down)
        y4 = _gather_rows(out_rows, dest_sm)
        x2d = _combine(y4, gates_tm, x1, g2, final_norm_w.reshape(1, d), seq,
                       final=(i == depth - 1))

    return x2d.reshape(batch, seq, d)
```

```python
import functools

import numpy as np
import jax
import jax.numpy as jnp
from jax import lax
from jax.experimental import pallas as pl
from jax.experimental.pallas import tpu as pltpu
from jax.experimental.pallas import tpu_sc as plsc

D_MODEL = 1024
CHUNK = 64
RET_CHUNK = 128
NORM_EPS = 1e-5
N_HEADS = 4
HEAD_DK = 48
HEAD_DV = 96
GATE_RANK = 16
GATE_TEMP = 16.0
ROPE_BASE = 10000.0
S5_WIDTH = 256
S5_GROUP_DIM = 16
S5_GROUPS = 16
S5_STATE = 64
N_EXPERTS = 32
TOP_K = 4
D_FF = 1024
SWIGLU_LIMIT = 7.0
SWIGLU_ALPHA = 1.702

LANES = 128
HEAD_PAD = LANES
HP = N_HEADS * HEAD_PAD
N_PAIRS = N_HEADS // 2
QKP = N_PAIRS * LANES
VMEM_LIMIT = 56 * 1024 * 1024

OFF_GQ, OFF_GK, OFF_GV, OFF_GG = 0, QKP, 2 * QKP, 2 * QKP + HP
OFF_RQ = OFF_GG + HP
OFF_RK, OFF_RV, OFF_RG = OFF_RQ + QKP, OFF_RQ + 2 * QKP, OFF_RQ + 2 * QKP + HP
OFF_SU = OFF_RG + HP
OFF_GA = OFF_SU + S5_WIDTH
NP_COLS = OFF_GA + LANES
PROJ_CH = 1152

ROW_BLK = 1024
ROW_SUB = 512

F32 = jnp.float32
BF16 = jnp.bfloat16


def _DK_SRC_LANE(h, d):
    return (h // 2) * LANES + (h % 2) * HEAD_DK + d


def _in_col_map():
    src = -np.ones((NP_COLS,), np.int64)
    kq = N_HEADS * HEAD_DK
    kv = N_HEADS * HEAD_DV
    base = dict(gq=0, gk=kq, gv=2 * kq, gg=2 * kq + kv, ga=2 * kq + 2 * kv)
    r0 = base['ga'] + GATE_RANK
    base.update(rq=r0, rk=r0 + kq, rv=r0 + 2 * kq, rg=r0 + 2 * kq + kv, su=r0 + 2 * kq + 2 * kv)
    for h in range(N_HEADS):
        for d in range(HEAD_DK):
            lane = _DK_SRC_LANE(h, d)
            src[OFF_GQ + lane] = base['gq'] + h * HEAD_DK + d
            src[OFF_GK + lane] = base['gk'] + h * HEAD_DK + d
            src[OFF_RQ + lane] = base['rq'] + h * HEAD_DK + d
            src[OFF_RK + lane] = base['rk'] + h * HEAD_DK + d
        for d in range(HEAD_DV):
            src[OFF_GV + h * HEAD_PAD + d] = base['gv'] + h * HEAD_DV + d
            src[OFF_GG + h * HEAD_PAD + d] = base['gg'] + h * HEAD_DV + d
            src[OFF_RV + h * HEAD_PAD + d] = base['rv'] + h * HEAD_DV + d
            src[OFF_RG + h * HEAD_PAD + d] = base['rg'] + h * HEAD_DV + d
    src[OFF_SU:OFF_SU + S5_WIDTH] = base['su'] + np.arange(S5_WIDTH)
    src[OFF_GA:OFF_GA + GATE_RANK] = base['ga'] + np.arange(GATE_RANK)
    return src


_IN_SRC = _in_col_map()


def _head_pad_map(width):
    src = -np.ones((HP,), np.int64)
    for h in range(N_HEADS):
        src[h * HEAD_PAD:h * HEAD_PAD + width] = h * width + np.arange(width)
    return src


_DV_SRC = _head_pad_map(HEAD_DV)
_DK_SRC = -np.ones((QKP,), np.int64)
for _h in range(N_HEADS):
    for _d in range(HEAD_DK):
        _DK_SRC[_DK_SRC_LANE(_h, _d)] = _h * HEAD_DK + _d


def _take_cols(w, src):
    out = jnp.take(w, jnp.asarray(np.maximum(src, 0)), axis=-1)
    return jnp.where(jnp.asarray(src >= 0), out, 0)


def _take_rows(w, src):
    out = jnp.take(w, jnp.asarray(np.maximum(src, 0)), axis=0)
    return jnp.where(jnp.asarray(src >= 0)[:, None], out, 0)


def _cparams(sem):
    return pltpu.CompilerParams(dimension_semantics=sem, vmem_limit_bytes=VMEM_LIMIT)


def _mod_kernel(c_ref, w_ref, b_ref, o_ref):
    c = c_ref[...]
    cond = c * jax.nn.sigmoid(c)
    o_ref[0] = jnp.dot(cond, w_ref[0], preferred_element_type=F32,
                       precision=lax.Precision.HIGHEST) + b_ref[0]


def _modulation(c, w_mod, b_mod):
    depth, d, n = w_mod.shape
    b = c.shape[0]
    nb = 1536
    return pl.pallas_call(
        _mod_kernel,
        grid=(depth, n // nb),
        in_specs=[pl.BlockSpec((b, d), lambda l, j: (0, 0)),
                  pl.BlockSpec((1, d, nb), lambda l, j: (l, 0, j)),
                  pl.BlockSpec((1, 1, nb), lambda l, j: (l, 0, j))],
        out_specs=pl.BlockSpec((1, b, nb), lambda l, j: (l, 0, j)),
        out_shape=jax.ShapeDtypeStruct((depth, b, n), F32),
        compiler_params=_cparams(("arbitrary", "arbitrary")),
        name="adaln_mod",
    )(c, w_mod, b_mod.reshape(depth, 1, n))


def _rms_mod(x, nw, sc, sh):
    y = x * lax.rsqrt(jnp.mean(x * x, axis=-1, keepdims=True) + NORM_EPS)
    return (y * nw) * (1.0 + sc) + sh


def _inproj_kernel(x_ref, sc_ref, sh_ref, nw_ref, w_ref, o_ref):
    h = _rms_mod(x_ref[...], nw_ref[...], sc_ref[0], sh_ref[0]).astype(BF16)
    for j in range(NP_COLS // PROJ_CH):
        cs = slice(j * PROJ_CH, (j + 1) * PROJ_CH)
        o_ref[:, cs] = jnp.dot(h, w_ref[:, cs], preferred_element_type=F32).astype(BF16)


def _in_projection(x2d, sc, sh, nw, w_p, seq, tm=512):
    t, d = x2d.shape
    per_b = seq // tm
    return pl.pallas_call(
        _inproj_kernel,
        grid=(t // tm,),
        in_specs=[pl.BlockSpec((tm, d), lambda i: (i, 0)),
                  pl.BlockSpec((1, 1, d), lambda i: (i // per_b, 0, 0)),
                  pl.BlockSpec((1, 1, d), lambda i: (i // per_b, 0, 0)),
                  pl.BlockSpec((1, d), lambda i: (0, 0)),
                  pl.BlockSpec((d, NP_COLS), lambda i: (0, 0))],
        out_specs=pl.BlockSpec((tm, NP_COLS), lambda i: (i, 0)),
        out_shape=jax.ShapeDtypeStruct((t, NP_COLS), BF16),
        compiler_params=_cparams(("arbitrary",)),
        name="in_proj",
    )(x2d, sc, sh, nw, w_p)


_NT = (((1,), (1,)), ((), ()))
_TN = (((0,), (0,)), ((), ()))


def _tri_mask(n=CHUNK):
    r = lax.broadcasted_iota(jnp.int32, (n, n), 0)
    c = lax.broadcasted_iota(jnp.int32, (n, n), 1)
    return r >= c


def _pair_masks(rows):
    lane = lax.broadcasted_iota(jnp.int32, (rows, LANES), 1)
    return lane < HEAD_DK, (lane >= HEAD_DK) & (lane < 2 * HEAD_DK)


def _head_attention(qd, ki, ke, vh, et, st_ref, h, causal):
    qb = qd.astype(BF16)
    sc = lax.dot_general(qb, ki.astype(BF16), _NT, preferred_element_type=F32)
    sc = jnp.where(causal, sc, 0.0)
    st = st_ref[h]
    o = jnp.dot(sc.astype(BF16), vh, preferred_element_type=F32)
    o = o + lax.dot_general(qb, st.astype(BF16), _NT, preferred_element_type=F32)
    st_ref[h] = st * et + lax.dot_general(vh, ke.astype(BF16), _TN, preferred_element_type=F32)
    return o


def _gla_kernel(q_ref, k_ref, v_ref, g_ref, a_ref, wa_ref, ba_ref, nw_ref, tri_ref, o_ref,
                st_ref, qd_s, ki_s, ke_s, et_s):
    @pl.when(pl.program_id(1) == 0)
    def _():
        st_ref[...] = jnp.zeros_like(st_ref)

    causal = _tri_mask()
    tl = q_ref.shape[0]
    n_chunks = tl // CHUNK

    z = jnp.dot(a_ref[...], wa_ref[...], preferred_element_type=F32) + ba_ref[...]
    la = (jnp.minimum(z, 0.0) - jnp.log1p(jnp.exp(-jnp.abs(z)))) * (1.0 / GATE_TEMP)
    hi = la.astype(BF16)
    lo = (la - hi.astype(F32)).astype(BF16)
    cum = (jnp.dot(tri_ref[...], hi, preferred_element_type=F32)
           + jnp.dot(tri_ref[...], lo, preferred_element_type=F32))
    cum3 = cum.reshape(n_chunks, CHUNK, QKP)
    tot3 = cum3[:, CHUNK - 1:CHUNK, :]
    qd = (q_ref[...].astype(F32) * (HEAD_DK ** -0.5)) * jnp.exp(cum)
    masks = _pair_masks(tl)
    for h in range(N_HEADS):
        pair = slice((h // 2) * LANES, (h // 2 + 1) * LANES)
        qd_s[:, h * HEAD_PAD:(h + 1) * HEAD_PAD] = jnp.where(masks[h % 2], qd[:, pair],
                                                              0.0).astype(BF16)
    kf = k_ref[...].astype(F32)
    ki_s[...] = (kf * jnp.exp(-cum)).astype(BF16)
    ke_s[...] = (kf * jnp.exp(tot3 - cum3).reshape(tl, QKP)).astype(BF16)
    et_s[...] = jnp.exp(tot3).reshape(n_chunks, QKP)

    def chunk(c, carry):
        r = pl.ds(pl.multiple_of(c * CHUNK, CHUNK), CHUNK)
        et = et_s[pl.ds(c, 1), :]
        for h in range(N_HEADS):
            sl = slice(h * HEAD_PAD, (h + 1) * HEAD_PAD)
            pair = slice((h // 2) * LANES, (h // 2 + 1) * LANES)
            o = _head_attention(qd_s[r, sl], ki_s[r, pair], ke_s[r, pair], v_ref[r, sl],
                                et[:, pair], st_ref, h, causal)
            ms = jnp.sum(o * o, axis=-1, keepdims=True) * (1.0 / HEAD_DV)
            y = (o * lax.rsqrt(ms + NORM_EPS)) * nw_ref[:, sl]
            g = g_ref[r, sl].astype(F32)
            o_ref[r, sl] = (y * (g * jax.nn.sigmoid(g))).astype(BF16)
        return carry

    lax.fori_loop(0, n_chunks, chunk, 0, unroll=4)


def _ret_kernel(q_ref, k_ref, v_ref, g_ref, cos_ref, sina_ref, sinb_ref, dq_ref, dki_ref,
                dke_ref, dt_ref, nw_ref, o_ref, st_ref):
    @pl.when(pl.program_id(1) == 0)
    def _():
        st_ref[...] = jnp.zeros_like(st_ref)

    causal = _tri_mask(RET_CHUNK)
    n_chunks = q_ref.shape[0] // RET_CHUNK
    lane = lax.broadcasted_iota(jnp.int32, (RET_CHUNK, HEAD_PAD), 1)
    real = lane < HEAD_DV
    masks = _pair_masks(RET_CHUNK)
    half = HEAD_DK // 2

    def rotary(t, cos, sina, sinb):
        return (t * cos + pltpu.roll(t, LANES - half, 1) * sina + pltpu.roll(t, half, 1) * sinb)

    def chunk(c, carry):
        r = pl.ds(pl.multiple_of(c * RET_CHUNK, RET_CHUNK), RET_CHUNK)
        cos, sina, sinb = cos_ref[r, :], sina_ref[r, :], sinb_ref[r, :]
        pair_q, pair_ki, pair_ke = [], [], []
        for p in range(N_PAIRS):
            ps = slice(p * LANES, (p + 1) * LANES)
            qr = rotary(q_ref[r, ps].astype(F32), cos, sina, sinb) * dq_ref[:, ps]
            kr = rotary(k_ref[r, ps].astype(F32), cos, sina, sinb) * (HEAD_DK ** -0.5)
            pair_q.append(qr)
            pair_ki.append((kr * dki_ref[:, ps]).astype(BF16))
            pair_ke.append((kr * dke_ref[:, ps]).astype(BF16))
        for h in range(N_HEADS):
            sl = slice(h * HEAD_PAD, (h + 1) * HEAD_PAD)
            p = h // 2
            qd = jnp.where(masks[h % 2], pair_q[p], 0.0)
            o = _head_attention(qd, pair_ki[p], pair_ke[p], v_ref[r, sl],
                                dt_ref[:, p * LANES:(p + 1) * LANES], st_ref, h, causal)
            mu = jnp.sum(o, axis=-1, keepdims=True) * (1.0 / HEAD_DV)
            oc = jnp.where(real, o - mu, 0.0)
            var = jnp.sum(oc * oc, axis=-1, keepdims=True) * (1.0 / HEAD_DV)
            y = (oc * lax.rsqrt(var + NORM_EPS)) * nw_ref[:, sl]
            g = g_ref[r, sl].astype(F32)
            o_ref[r, sl] = (y * (g * jax.nn.sigmoid(g))).astype(BF16)
        return carry

    lax.fori_loop(0, n_chunks, chunk, 0, unroll=2)


def _proj_spec(tl, width, col_off, per_b):
    cb = col_off // width
    return pl.BlockSpec((tl, width), lambda b, l: (b * per_b + l, cb))


def _full(shape):
    return pl.BlockSpec(shape, lambda b, l: (0,) * len(shape))


def _gla_mixer(proj, batch, seq, wa_p, ba_p, nw_p, tl=512):
    per_b = seq // tl
    pos = np.arange(tl)
    tri_bd = jnp.asarray((pos[:, None] // CHUNK == pos[None, :] // CHUNK)
                         & (pos[:, None] >= pos[None, :]), BF16)
    return pl.pallas_call(
        _gla_kernel,
        grid=(batch, per_b),
        in_specs=[_proj_spec(tl, QKP, OFF_GQ, per_b), _proj_spec(tl, QKP, OFF_GK, per_b),
                  _proj_spec(tl, HP, OFF_GV, per_b), _proj_spec(tl, HP, OFF_GG, per_b),
                  _proj_spec(tl, LANES, OFF_GA, per_b),
                  _full((LANES, QKP)), _full((1, QKP)), _full((1, HP)), _full((tl, tl))],
        out_specs=pl.BlockSpec((tl, HP), lambda b, l: (b * per_b + l, 0)),
        out_shape=jax.ShapeDtypeStruct((batch * seq, HP), BF16),
        scratch_shapes=[pltpu.VMEM((N_HEADS, HEAD_PAD, HEAD_PAD), F32),
                        pltpu.VMEM((tl, HP), BF16), pltpu.VMEM((tl, QKP), BF16),
                        pltpu.VMEM((tl, QKP), BF16), pltpu.VMEM((tl // CHUNK, QKP), F32)],
        compiler_params=_cparams(("arbitrary", "arbitrary")),
        name="gla_mixer",
    )(proj, proj, proj, proj, proj, wa_p, ba_p, nw_p, tri_bd)


def _ret_mixer(proj, batch, seq, cos_t, sina_t, sinb_t, dq, dki, dke, dtot, nw_p, tl=512):
    per_b = seq // tl
    return pl.pallas_call(
        _ret_kernel,
        grid=(batch, per_b),
        in_specs=[_proj_spec(tl, QKP, OFF_RQ, per_b), _proj_spec(tl, QKP, OFF_RK, per_b),
                  _proj_spec(tl, HP, OFF_RV, per_b), _proj_spec(tl, HP, OFF_RG, per_b),
                  pl.BlockSpec((tl, LANES), lambda b, l: (l, 0)),
                  pl.BlockSpec((tl, LANES), lambda b, l: (l, 0)),
                  pl.BlockSpec((tl, LANES), lambda b, l: (l, 0)),
                  _full((RET_CHUNK, QKP)), _full((RET_CHUNK, QKP)), _full((RET_CHUNK, QKP)),
                  _full((1, QKP)), _full((1, HP))],
        out_specs=pl.BlockSpec((tl, HP), lambda b, l: (b * per_b + l, 0)),
        out_shape=jax.ShapeDtypeStruct((batch * seq, HP), BF16),
        scratch_shapes=[pltpu.VMEM((N_HEADS, HEAD_PAD, HEAD_PAD), F32)],
        compiler_params=_cparams(("arbitrary", "arbitrary")),
        name="ret_mixer",
    )(proj, proj, proj, proj, cos_t, sina_t, sinb_t, dq, dki, dke, dtot, nw_p)


S5_N = S5_GROUPS * S5_STATE


def _gelu_tanh(x):
    return 0.5 * x * (1.0 + jnp.tanh(np.sqrt(2.0 / np.pi) * (x + 0.044715 * (x * x * x))))


def _s5_kernel(u_ref, bb_ref, cb_ref, nr_ref, ni_ref, pr_ref, pi_ref, lr_ref, li_ref,
               d_ref, wg_ref, bg_ref, o_ref, sr_ref, si_ref, x_scr, s_scr):
    @pl.when(pl.program_id(1) == 0)
    def _():
        sr_ref[...] = jnp.zeros_like(sr_ref)
        si_ref[...] = jnp.zeros_like(si_ref)

    tri = _tri_mask().astype(BF16)
    n_chunks = u_ref.shape[0] // CHUNK
    u = u_ref[...]
    x_scr[...] = jnp.dot(u, bb_ref[...], preferred_element_type=F32)

    def chunk(c, carry):
        r = pl.ds(pl.multiple_of(c * CHUNK, CHUNK), CHUNK)
        xr, xi = x_scr[r, :S5_N], x_scr[r, S5_N:]
        nr, ni = nr_ref[...], ni_ref[...]
        p_r = jnp.dot(tri, (xr * nr - xi * ni).astype(BF16), preferred_element_type=F32)
        p_i = jnp.dot(tri, (xr * ni + xi * nr).astype(BF16), preferred_element_type=F32)
        s0r, s0i = sr_ref[...], si_ref[...]
        lr, li = lr_ref[...], li_ref[...]
        q_r = p_r + (s0r * lr - s0i * li)
        q_i = p_i + (s0r * li + s0i * lr)
        pr, pi = pr_ref[...], pi_ref[...]
        s_r = q_r * pr - q_i * pi
        s_i = q_r * pi + q_i * pr
        sr_ref[...] = s_r[CHUNK - 1:CHUNK, :]
        si_ref[...] = s_i[CHUNK - 1:CHUNK, :]
        s_scr[r, :S5_N] = s_r.astype(BF16)
        s_scr[r, S5_N:] = s_i.astype(BF16)
        return carry

    lax.fori_loop(0, n_chunks, chunk, 0, unroll=2)
    y = jnp.dot(s_scr[...], cb_ref[...], preferred_element_type=F32)
    y = _gelu_tanh(y + d_ref[...] * u.astype(F32))
    gate = jnp.dot(y.astype(BF16), wg_ref[...], preferred_element_type=F32) + bg_ref[...]
    o_ref[...] = (y * jax.nn.sigmoid(gate)).astype(BF16)


def _s5_mixer(proj, batch, seq, tabs, tl=512):
    per_b = seq // tl
    bb, cb, nr, ni, pr, pi, lr, li, dsk, wg, bg = tabs
    return pl.pallas_call(
        _s5_kernel,
        grid=(batch, per_b),
        in_specs=[_proj_spec(tl, S5_WIDTH, OFF_SU, per_b),
                  _full((S5_WIDTH, 2 * S5_N)), _full((2 * S5_N, S5_WIDTH)),
                  _full((CHUNK, S5_N)), _full((CHUNK, S5_N)),
                  _full((CHUNK, S5_N)), _full((CHUNK, S5_N)),
                  _full((1, S5_N)), _full((1, S5_N)),
                  _full((1, S5_WIDTH)), _full((S5_WIDTH, S5_WIDTH)), _full((1, S5_WIDTH))],
        out_specs=pl.BlockSpec((tl, S5_WIDTH), lambda b, l: (b * per_b + l, 0)),
        out_shape=jax.ShapeDtypeStruct((batch * seq, S5_WIDTH), BF16),
        scratch_shapes=[pltpu.VMEM((1, S5_N), F32), pltpu.VMEM((1, S5_N), F32),
                        pltpu.VMEM((tl, 2 * S5_N), F32), pltpu.VMEM((tl, 2 * S5_N), BF16)],
        compiler_params=_cparams(("arbitrary", "arbitrary")),
        name="s5_mixer",
    )(proj, bb, cb, nr, ni, pr, pi, lr, li, dsk, wg, bg)


def _s5_tables(a_re, a_im, log_dt, b_re, b_im, c_re, c_im, d_skip, w_glu, b_glu):
    lam = lax.complex(a_re, a_im)
    dt = jnp.exp(log_dt)[:, None]
    lam_bar = jnp.exp(lam * dt)
    b_bar = ((lam_bar - 1.0) / lam)[..., None] * lax.complex(b_re, b_im)
    eye = jnp.eye(S5_GROUPS, dtype=F32)
    def blk_b(m):
        return jnp.einsum('gph,gk->ghkp', m, eye).reshape(S5_WIDTH, S5_N)
    bb = jnp.concatenate([blk_b(jnp.real(b_bar)), blk_b(jnp.imag(b_bar))], axis=1)
    def blk_c(m):
        return jnp.einsum('ghp,gk->kpgh', m, eye).reshape(S5_N, S5_WIDTH)
    cb = jnp.concatenate([blk_c(c_re), blk_c(-c_im)], axis=0)
    steps = jnp.arange(CHUNK, dtype=F32)[:, None, None]
    lam_dt = (lam * dt)[None]
    pos = jnp.exp(lam_dt * steps).reshape(CHUNK, S5_N)
    neg = jnp.exp(-lam_dt * steps).reshape(CHUNK, S5_N)
    one = lam_bar.reshape(1, S5_N)
    return (bb.astype(BF16), cb.astype(BF16), jnp.real(neg), jnp.imag(neg), jnp.real(pos),
            jnp.imag(pos), jnp.real(one), jnp.imag(one), d_skip.reshape(1, S5_WIDTH),
            w_glu.astype(BF16), b_glu.reshape(1, S5_WIDTH))


def _pack_bf16_pairs(x):
    w = x.shape[1] // 2
    xb = x.astype(BF16).astype(F32)
    hi = lax.bitcast_convert_type(xb[:, :w], jnp.uint32)
    lo = lax.bitcast_convert_type(xb[:, w:], jnp.uint32)
    return hi | (lo >> 16)


def _unpack_bf16_pairs(p):
    hi = lax.bitcast_convert_type(p & jnp.uint32(0xFFFF0000), F32)
    lo = lax.bitcast_convert_type(p << 16, F32)
    return hi, lo


def _router_kernel(og_ref, or_ref, os_ref, wg_ref, wr_ref, ws_ref, x_ref, g1_ref, sc_ref,
                   sh_ref, nw_ref, rw_ref, rb_ref,
                   x1_ref, h_ref, idx_ref, gate_ref, rank_ref, cnt_ref, carry_ref):
    i = pl.program_id(0)

    @pl.when(i == 0)
    def _():
        carry_ref[...] = jnp.zeros_like(carry_ref)

    mix = (jnp.dot(og_ref[...], wg_ref[...], preferred_element_type=F32)
           + jnp.dot(or_ref[...], wr_ref[...], preferred_element_type=F32)
           + jnp.dot(os_ref[...], ws_ref[...], preferred_element_type=F32))
    x1 = x_ref[...] + g1_ref[0] * mix
    x1_ref[...] = x1
    hdn = _rms_mod(x1, nw_ref[...], sc_ref[0], sh_ref[0])
    h_ref[...] = _pack_bf16_pairs(hdn)
    h_hi = hdn.astype(BF16)
    h_lo = (hdn - h_hi.astype(F32)).astype(BF16)
    p = lax.dot_general(rw_ref[...], h_hi, _NT, preferred_element_type=F32)
    q = lax.dot_general(rw_ref[:N_EXPERTS, :], h_lo, _NT, preferred_element_type=F32)
    logits = (p[:N_EXPERTS] + p[N_EXPERTS:] + q) + rb_ref[:, 0:1]
    tm = logits.shape[1]
    eidx = lax.broadcasted_iota(jnp.int32, (N_EXPERTS, tm), 0)
    work = logits
    onehot = jnp.zeros((N_EXPERTS, tm), F32)
    vals, idxs, sels = [], [], []
    for _ in range(TOP_K):
        m = jnp.max(work, axis=0, keepdims=True)
        ix = jnp.min(jnp.where(work == m, eidx, N_EXPERTS), axis=0, keepdims=True)
        sel = eidx == ix
        work = jnp.where(sel, -jnp.inf, work)
        onehot = onehot + sel.astype(F32)
        vals.append(m)
        idxs.append(ix)
        sels.append(sel)
    exps = [jnp.exp(v - vals[0]) for v in vals]
    denom = exps[0] + exps[1] + exps[2] + exps[3]
    r = lax.broadcasted_iota(jnp.int32, (tm, tm), 0)
    c = lax.broadcasted_iota(jnp.int32, (tm, tm), 1)
    earlier = (r < c).astype(BF16)
    before = (jnp.dot(onehot.astype(BF16), earlier, preferred_element_type=F32)
              + carry_ref[:, 0:1])
    row8 = lax.broadcasted_iota(jnp.int32, (8, tm), 0)
    idx_out = jnp.zeros((8, tm), jnp.int32)
    gate_out = jnp.zeros((8, tm), F32)
    rank_out = jnp.zeros((8, tm), F32)
    for k in range(TOP_K):
        rk = jnp.sum(jnp.where(sels[k], before, 0.0), axis=0, keepdims=True)
        idx_out = jnp.where(row8 == k, idxs[k], idx_out)
        gate_out = jnp.where(row8 == k, exps[k] / denom, gate_out)
        rank_out = jnp.where(row8 == k, rk, rank_out)
    idx_ref[...] = idx_out
    gate_ref[...] = gate_out
    rank_ref[...] = rank_out.astype(jnp.int32)
    total = carry_ref[...] + jnp.sum(onehot, axis=1, keepdims=True)
    carry_ref[...] = total
    cnt_ref[...] = total


def _outproj_router(o_gla, o_ret, o_s5, wg, wr, ws, x2d, g1, sc2, sh2, nw2, rw_p, rb_p, seq,
                    tm=512):
    t, d = x2d.shape
    per_b = seq // tm
    row = lambda w: pl.BlockSpec((tm, w), lambda i: (i, 0))
    full = lambda s: pl.BlockSpec(s, lambda i: (0,) * len(s))
    per_batch = pl.BlockSpec((1, 1, d), lambda i: (i // per_b, 0, 0))
    slot_t = pl.BlockSpec((8, tm), lambda i: (0, i))
    return pl.pallas_call(
        _router_kernel,
        grid=(t // tm,),
        in_specs=[row(HP), row(HP), row(S5_WIDTH), full((HP, d)), full((HP, d)),
                  full((S5_WIDTH, d)), row(d), per_batch, per_batch, per_batch,
                  full((1, d)), full((2 * N_EXPERTS, d)), full((N_EXPERTS, LANES))],
        out_specs=[row(d), row(d // 2), slot_t, slot_t, slot_t, full((N_EXPERTS, LANES))],
        out_shape=[jax.ShapeDtypeStruct((t, d), F32),
                   jax.ShapeDtypeStruct((t, d // 2), jnp.uint32),
                   jax.ShapeDtypeStruct((8, t), jnp.int32),
                   jax.ShapeDtypeStruct((8, t), F32),
                   jax.ShapeDtypeStruct((8, t), jnp.int32),
                   jax.ShapeDtypeStruct((N_EXPERTS, LANES), F32)],
        scratch_shapes=[pltpu.VMEM((N_EXPERTS, LANES), F32)],
        compiler_params=_cparams(("arbitrary",)),
        name="outproj_router",
    )(o_gla, o_ret, o_s5, wg, wr, ws, x2d, g1, sc2, sh2, nw2, rw_p, rb_p)


GATHER_WIN = 64


def _gather_rows(table, idx):
    m = idx.shape[0]
    w = table.shape[1]
    mesh = plsc.VectorSubcoreMesh(core_axis_name="core", subcore_axis_name="subcore")

    @functools.partial(pl.kernel, out_type=jax.ShapeDtypeStruct((m, w), table.dtype),
                       mesh=mesh, name="sc_row_gather")
    def gather(x_hbm, i_hbm, o_hbm):
        def body(i_vmem, o_vmem):
            pltpu.sync_copy(x_hbm.at[i_vmem], o_vmem)

        pltpu.emit_pipeline(
            body,
            grid=(m // GATHER_WIN,),
            in_specs=[pl.BlockSpec((GATHER_WIN,), lambda i: (i,))],
            out_specs=[pl.BlockSpec((GATHER_WIN, w), lambda i: (i, 0))],
            core_axis_name=("core", "subcore"),
            dimension_semantics=(pltpu.PARALLEL,),
        )(i_hbm, o_hbm)

    return gather(table, idx)


def _scatter_rows(x, dest_slot_major, n_rows):
    t, w = x.shape
    steps = t // GATHER_WIN
    mesh = plsc.VectorSubcoreMesh(core_axis_name="core", subcore_axis_name="subcore")

    @functools.partial(pl.kernel, out_type=jax.ShapeDtypeStruct((n_rows, w), x.dtype),
                       mesh=mesh, name="sc_row_scatter")
    def scatter(x_hbm, i_hbm, o_hbm):
        def body(x_vmem, i0, i1, i2, i3):
            for i_vmem in (i0, i1, i2, i3):
                pltpu.sync_copy(x_vmem, o_hbm.at[i_vmem])

        slot = lambda k: pl.BlockSpec((GATHER_WIN,), lambda i: (k * steps + i,))
        pltpu.emit_pipeline(
            body,
            grid=(steps,),
            in_specs=[pl.BlockSpec((GATHER_WIN, w), lambda i: (i, 0)),
                      slot(0), slot(1), slot(2), slot(3)],
            out_specs=[],
            core_axis_name=("core", "subcore"),
            dimension_semantics=(pltpu.PARALLEL,),
        )(x_hbm, i_hbm, i_hbm, i_hbm, i_hbm)

    return scatter(x, dest_slot_major)


def _expert_kernel(be_ref, nv_ref, rows_ref, wu_ref, bu_ref, wd_ref, bd_ref, o_ref,
                   wu_bf, wd_bf):
    i = pl.program_id(0)
    e = be_ref[i]
    prev = be_ref[jnp.maximum(i - 1, 0)]

    @pl.when((i == 0) | (e != prev))
    def _():
        wu_bf[...] = wu_ref[0, 0].astype(BF16)
        wd_bf[...] = wd_ref[0, 0].astype(BF16)

    for s in range(ROW_BLK // ROW_SUB):
        rs = slice(s * ROW_SUB, (s + 1) * ROW_SUB)
        left = nv_ref[i] - s * ROW_SUB

        @pl.when(left > 0)
        def _():
            row = lax.broadcasted_iota(jnp.int32, (ROW_SUB, rows_ref.shape[1]), 0)
            x_hi, x_lo = _unpack_bf16_pairs(jnp.where(row < left, rows_ref[rs, :], jnp.uint32(0)))
            x = jnp.concatenate([x_hi.astype(BF16), x_lo.astype(BF16)], axis=1)
            up = jnp.dot(x, wu_bf[...], preferred_element_type=F32) + bu_ref[0, 0]
            x_glu = jnp.minimum(up[:, :D_FF], SWIGLU_LIMIT)
            x_lin = jnp.clip(up[:, D_FF:], -SWIGLU_LIMIT, SWIGLU_LIMIT)
            act = x_glu * jax.nn.sigmoid(SWIGLU_ALPHA * x_glu) * (x_lin + 1.0)
            o_ref[rs, :] = _pack_bf16_pairs(
                jnp.dot(act.astype(BF16), wd_bf[...], preferred_element_type=F32) + bd_ref[0, 0])

        @pl.when(left <= 0)
        def _():
            o_ref[rs, :] = jnp.zeros((ROW_SUB, o_ref.shape[1]), o_ref.dtype)


def _experts(layer, block_e, n_valid, rows, w_up, b_up, w_down, b_down):
    n_rows, dh = rows.shape
    d = 2 * dh
    n_blocks = n_rows // ROW_BLK
    depth, ne, _, f2 = w_up.shape
    wsel = lambda i, be, nu: (layer, be[i], 0, 0)
    grid_spec = pltpu.PrefetchScalarGridSpec(
        num_scalar_prefetch=2,
        grid=(n_blocks,),
        in_specs=[pl.BlockSpec((ROW_BLK, dh), lambda i, be, nu: (i, 0)),
                  pl.BlockSpec((1, 1, d, f2), wsel),
                  pl.BlockSpec((1, 1, 1, f2), wsel),
                  pl.BlockSpec((1, 1, D_FF, d), wsel),
                  pl.BlockSpec((1, 1, 1, d), wsel)],
        out_specs=pl.BlockSpec((ROW_BLK, dh), lambda i, be, nu: (i, 0)),
        scratch_shapes=[pltpu.VMEM((d, f2), BF16), pltpu.VMEM((D_FF, d), BF16)],
    )
    return pl.pallas_call(
        _expert_kernel,
        grid_spec=grid_spec,
        out_shape=jax.ShapeDtypeStruct((n_rows, dh), jnp.uint32),
        compiler_params=_cparams(("arbitrary",)),
        name="moe_experts",
    )(block_e, n_valid, rows, w_up, b_up.reshape(depth, ne, 1, f2), w_down,
      b_down.reshape(depth, ne, 1, d))


def _combine_kernel(y0_ref, y1_ref, y2_ref, y3_ref, gate_ref, x1_ref, g2_ref, fw_ref, o_ref,
                    *, final):
    gates = gate_ref[...]
    y_hi, y_lo = None, None
    for k, y_ref in enumerate((y0_ref, y1_ref, y2_ref, y3_ref)):
        hi, lo = _unpack_bf16_pairs(y_ref[...])
        g = gates[:, k:k + 1]
        y_hi = g * hi if y_hi is None else y_hi + g * hi
        y_lo = g * lo if y_lo is None else y_lo + g * lo
    y = jnp.concatenate([y_hi, y_lo], axis=1)
    x2 = x1_ref[...] + g2_ref[0] * y
    if final:
        x2 = (x2 * lax.rsqrt(jnp.mean(x2 * x2, axis=-1, keepdims=True) + NORM_EPS)) * fw_ref[...]
    o_ref[...] = x2


def _combine(y4, gates, x1, g2, fw, seq, final, th=256):
    t, d = x1.shape
    steps = t // th
    per_b = seq // th
    slot = lambda k: pl.BlockSpec((th, d // 2), lambda i: (k * steps + i, 0))
    return pl.pallas_call(
        functools.partial(_combine_kernel, final=final),
        grid=(steps,),
        in_specs=[slot(0), slot(1), slot(2), slot(3),
                  pl.BlockSpec((th, TOP_K), lambda i: (i, 0)),
                  pl.BlockSpec((th, d), lambda i: (i, 0)),
                  pl.BlockSpec((1, 1, d), lambda i: (i // per_b, 0, 0)),
                  pl.BlockSpec((1, d), lambda i: (0, 0))],
        out_specs=pl.BlockSpec((th, d), lambda i: (i, 0)),
        out_shape=jax.ShapeDtypeStruct((t, d), F32),
        compiler_params=_cparams(("arbitrary",)),
        name="moe_combine",
    )(y4, y4, y4, y4, gates, x1, g2, fw)


def _retention_tables(seq):
    pos = jnp.arange(seq, dtype=F32)
    inv_freq = ROPE_BASE ** (-jnp.arange(0, HEAD_DK, 2, dtype=F32) / HEAD_DK)
    ang = pos[:, None] * inv_freq[None, :]
    cos, sin = jnp.cos(ang), jnp.sin(ang)
    zero = jnp.zeros_like(sin)
    zpad = jnp.zeros((seq, LANES - 2 * HEAD_DK), F32)
    cos_t = jnp.concatenate([cos, cos, cos, cos, zpad], axis=1)
    sina_t = jnp.concatenate([-sin, zero, -sin, zero, zpad], axis=1)
    sinb_t = jnp.concatenate([zero, sin, zero, sin, zpad], axis=1)
    log_gamma = jnp.log1p(-jnp.exp2(-5.0 - jnp.arange(N_HEADS, dtype=F32)))
    log_decay = jnp.broadcast_to(log_gamma[None, :, None], (RET_CHUNK, N_HEADS, HEAD_DK))
    cum = jnp.cumsum(log_decay, axis=0)
    tot = cum[-1:]
    shp = lambda a: _take_cols(a.reshape(a.shape[0], N_HEADS * HEAD_DK), _DK_SRC) + jnp.asarray(
        _DK_SRC < 0, F32)
    return (cos_t, sina_t, sinb_t, shp(jnp.exp(cum)), shp(jnp.exp(-cum)), shp(jnp.exp(tot - cum)),
            shp(jnp.exp(tot)))


def kernel(x, c, norm1_w, norm2_w, w_mod, b_mod, w_in, gla_w_a2, gla_b_a, gla_norm_w, ret_norm_w, s5_a_re, s5_a_im, s5_log_dt, s5_b_re, s5_b_im, s5_c_re, s5_c_im, s5_d, s5_w_glu, s5_b_glu, w_out, router_w, router_b, w_up, b_up, w_down, b_down, final_norm_w):
    batch, seq, d = x.shape
    depth = w_mod.shape[0]
    t = batch * seq
    n_slots = t * TOP_K
    n_blocks = n_slots // ROW_BLK + N_EXPERTS
    n_rows = n_blocks * ROW_BLK

    mod = _modulation(c, w_mod, b_mod)
    ret_tabs = _retention_tables(seq)
    x2d = x.reshape(t, d)

    for i in range(depth):
        m6 = mod[i].reshape(batch, 6, 1, d)
        sh1, sc1, g1, sh2, sc2, g2 = (m6[:, j] for j in range(6))

        w_p = _take_cols(w_in[i], _IN_SRC).astype(BF16)
        wa_p = jnp.zeros((LANES, QKP), F32).at[:GATE_RANK].set(_take_cols(gla_w_a2[i], _DK_SRC))
        ba_p = _take_cols(gla_b_a[i], _DK_SRC).reshape(1, QKP)
        gnw = _take_cols(gla_norm_w[i], _DV_SRC).reshape(1, HP)
        rnw = _take_cols(ret_norm_w[i], _DV_SRC).reshape(1, HP)
        kv = N_HEADS * HEAD_DV
        wo_g = _take_rows(w_out[i, :kv], _DV_SRC).astype(BF16)
        wo_r = _take_rows(w_out[i, kv:2 * kv], _DV_SRC).astype(BF16)
        wo_s = w_out[i, 2 * kv:].astype(BF16)
        rw_t = router_w[i].T
        rw_hi = rw_t.astype(BF16)
        rw_p = jnp.concatenate([rw_hi, (rw_t - rw_hi.astype(F32)).astype(BF16)], axis=0)
        rb_p = jnp.broadcast_to(router_b[i][:, None], (N_EXPERTS, LANES))

        proj = _in_projection(x2d, sc1, sh1, norm1_w[i].reshape(1, d), w_p, seq)
        o_gla = _gla_mixer(proj, batch, seq, wa_p.astype(BF16), ba_p, gnw)
        o_ret = _ret_mixer(proj, batch, seq, *ret_tabs, rnw)
        s5_tabs = _s5_tables(s5_a_re[i], s5_a_im[i], s5_log_dt[i], s5_b_re[i], s5_b_im[i],
                             s5_c_re[i], s5_c_im[i], s5_d[i], s5_w_glu[i], s5_b_glu[i])
        o_s5 = _s5_mixer(proj, batch, seq, s5_tabs)

        x1, hdn, idx, gates, rank, counts = _outproj_router(
            o_gla, o_ret, o_s5, wo_g, wo_r, wo_s, x2d, g1, sc2, sh2,
            norm2_w[i].reshape(1, d), rw_p, rb_p, seq)

        cnt = counts[:, 0].astype(jnp.int32)
        padded = (cnt + ROW_BLK - 1) // ROW_BLK * ROW_BLK
        pad_ends = jnp.cumsum(padded)
        pad_starts = pad_ends - padded
        slot_start = jnp.sum(jnp.where(idx[:TOP_K, :, None] == jnp.arange(N_EXPERTS), pad_starts,
                                       0), axis=-1)
        dest_sm = (slot_start + rank[:TOP_K]).astype(jnp.int32).reshape(-1)
        gates_tm = gates[:TOP_K].T
        blk_start = jnp.arange(n_blocks, dtype=jnp.int32) * ROW_BLK
        block_e = jnp.minimum(jnp.sum(pad_ends[None, :] <= blk_start[:, None], axis=1),
                              N_EXPERTS - 1).astype(jnp.int32)
        n_valid = jnp.clip((pad_starts + cnt)[block_e] - blk_start, 0, ROW_BLK).astype(jnp.int32)

        rows = _scatter_rows(hdn, dest_sm, n_rows)
        out_rows = _experts(i, block_e, n_valid, rows, w_up, b_up, w_down, b_down)
        y4 = _gather_rows(out_rows, dest_sm)
        x2d = _combine(y4, gates_tm, x1, g2, final_norm_w.reshape(1, d), seq,
                       final=(i == depth - 1))

    return x2d.reshape(batch, seq, d)
```

```python
import functools

import numpy as np
import jax
import jax.numpy as jnp
from jax import lax
from jax.experimental import pallas as pl
from jax.experimental.pallas import tpu as pltpu
from jax.experimental.pallas import tpu_sc as plsc

D_MODEL = 1024
CHUNK = 64
RET_CHUNK = 128
NORM_EPS = 1e-5
N_HEADS = 4
HEAD_DK = 48
HEAD_DV = 96
GATE_RANK = 16
GATE_TEMP = 16.0
ROPE_BASE = 10000.0
S5_WIDTH = 256
S5_GROUP_DIM = 16
S5_GROUPS = 16
S5_STATE = 64
N_EXPERTS = 32
TOP_K = 4
D_FF = 1024
SWIGLU_LIMIT = 7.0
SWIGLU_ALPHA = 1.702

LANES = 128
HEAD_PAD = LANES
HP = N_HEADS * HEAD_PAD
N_PAIRS = N_HEADS // 2
QKP = N_PAIRS * LANES
VMEM_LIMIT = 56 * 1024 * 1024

OFF_GQ, OFF_GK, OFF_GV, OFF_GG = 0, QKP, 2 * QKP, 2 * QKP + HP
OFF_RQ = OFF_GG + HP
OFF_RK, OFF_RV, OFF_RG = OFF_RQ + QKP, OFF_RQ + 2 * QKP, OFF_RQ + 2 * QKP + HP
OFF_SU = OFF_RG + HP
OFF_GA = OFF_SU + S5_WIDTH
NP_COLS = OFF_GA + LANES
PROJ_CH = 1152

ROW_BLK = 1024
ROW_SUB = 512
N_STREAMS = 2

F32 = jnp.float32
BF16 = jnp.bfloat16


def _DK_SRC_LANE(h, d):
    return (h // 2) * LANES + (h % 2) * HEAD_DK + d


def _in_col_map():
    src = -np.ones((NP_COLS,), np.int64)
    kq = N_HEADS * HEAD_DK
    kv = N_HEADS * HEAD_DV
    base = dict(gq=0, gk=kq, gv=2 * kq, gg=2 * kq + kv, ga=2 * kq + 2 * kv)
    r0 = base['ga'] + GATE_RANK
    base.update(rq=r0, rk=r0 + kq, rv=r0 + 2 * kq, rg=r0 + 2 * kq + kv, su=r0 + 2 * kq + 2 * kv)
    for h in range(N_HEADS):
        for d in range(HEAD_DK):
            lane = _DK_SRC_LANE(h, d)
            src[OFF_GQ + lane] = base['gq'] + h * HEAD_DK + d
            src[OFF_GK + lane] = base['gk'] + h * HEAD_DK + d
            src[OFF_RQ + lane] = base['rq'] + h * HEAD_DK + d
            src[OFF_RK + lane] = base['rk'] + h * HEAD_DK + d
        for d in range(HEAD_DV):
            src[OFF_GV + h * HEAD_PAD + d] = base['gv'] + h * HEAD_DV + d
            src[OFF_GG + h * HEAD_PAD + d] = base['gg'] + h * HEAD_DV + d
            src[OFF_RV + h * HEAD_PAD + d] = base['rv'] + h * HEAD_DV + d
            src[OFF_RG + h * HEAD_PAD + d] = base['rg'] + h * HEAD_DV + d
    src[OFF_SU:OFF_SU + S5_WIDTH] = base['su'] + np.arange(S5_WIDTH)
    src[OFF_GA:OFF_GA + GATE_RANK] = base['ga'] + np.arange(GATE_RANK)
    return src


_IN_SRC = _in_col_map()


def _head_pad_map(width):
    src = -np.ones((HP,), np.int64)
    for h in range(N_HEADS):
        src[h * HEAD_PAD:h * HEAD_PAD + width] = h * width + np.arange(width)
    return src


_DV_SRC = _head_pad_map(HEAD_DV)
_DK_SRC = -np.ones((QKP,), np.int64)
for _h in range(N_HEADS):
    for _d in range(HEAD_DK):
        _DK_SRC[_DK_SRC_LANE(_h, _d)] = _h * HEAD_DK + _d


def _take_cols(w, src):
    out = jnp.take(w, jnp.asarray(np.maximum(src, 0)), axis=-1)
    return jnp.where(jnp.asarray(src >= 0), out, 0)


def _take_rows(w, src):
    out = jnp.take(w, jnp.asarray(np.maximum(src, 0)), axis=0)
    return jnp.where(jnp.asarray(src >= 0)[:, None], out, 0)


def _cparams(sem):
    return pltpu.CompilerParams(dimension_semantics=sem, vmem_limit_bytes=VMEM_LIMIT)


def _mod_kernel(c_ref, w_ref, b_ref, o_ref):
    c = c_ref[...]
    cond = c * jax.nn.sigmoid(c)
    o_ref[0] = jnp.dot(cond, w_ref[0], preferred_element_type=F32,
                       precision=lax.Precision.HIGHEST) + b_ref[0]


def _modulation(c, w_mod, b_mod):
    depth, d, n = w_mod.shape
    b = c.shape[0]
    nb = 1536
    return pl.pallas_call(
        _mod_kernel,
        grid=(depth, n // nb),
        in_specs=[pl.BlockSpec((b, d), lambda l, j: (0, 0)),
                  pl.BlockSpec((1, d, nb), lambda l, j: (l, 0, j)),
                  pl.BlockSpec((1, 1, nb), lambda l, j: (l, 0, j))],
        out_specs=pl.BlockSpec((1, b, nb), lambda l, j: (l, 0, j)),
        out_shape=jax.ShapeDtypeStruct((depth, b, n), F32),
        compiler_params=_cparams(("arbitrary", "arbitrary")),
        name="adaln_mod",
    )(c, w_mod, b_mod.reshape(depth, 1, n))


def _rms_mod(x, nw, sc, sh):
    y = x * lax.rsqrt(jnp.mean(x * x, axis=-1, keepdims=True) + NORM_EPS)
    return (y * nw) * (1.0 + sc) + sh


def _inproj_kernel(x_ref, sc_ref, sh_ref, nw_ref, w_ref, o_ref):
    h = _rms_mod(x_ref[...], nw_ref[...], sc_ref[0], sh_ref[0]).astype(BF16)
    for j in range(NP_COLS // PROJ_CH):
        cs = slice(j * PROJ_CH, (j + 1) * PROJ_CH)
        o_ref[:, cs] = jnp.dot(h, w_ref[:, cs], preferred_element_type=F32).astype(BF16)


def _in_projection(x3, grp, sc, sh, nw, w_p, seq, tm=512):
    _, t, d = x3.shape
    per_b = seq // tm
    return pl.pallas_call(
        _inproj_kernel,
        grid=(t // tm,),
        in_specs=[pl.BlockSpec((None, tm, d), lambda i: (grp, i, 0)),
                  pl.BlockSpec((1, 1, d), lambda i: (i // per_b, 0, 0)),
                  pl.BlockSpec((1, 1, d), lambda i: (i // per_b, 0, 0)),
                  pl.BlockSpec((1, d), lambda i: (0, 0)),
                  pl.BlockSpec((d, NP_COLS), lambda i: (0, 0))],
        out_specs=pl.BlockSpec((tm, NP_COLS), lambda i: (i, 0)),
        out_shape=jax.ShapeDtypeStruct((t, NP_COLS), BF16),
        compiler_params=_cparams(("arbitrary",)),
        name="in_proj",
    )(x3, sc, sh, nw, w_p)


_NT = (((1,), (1,)), ((), ()))
_TN = (((0,), (0,)), ((), ()))


def _tri_mask(n=CHUNK):
    r = lax.broadcasted_iota(jnp.int32, (n, n), 0)
    c = lax.broadcasted_iota(jnp.int32, (n, n), 1)
    return r >= c


def _pair_masks(rows):
    lane = lax.broadcasted_iota(jnp.int32, (rows, LANES), 1)
    return lane < HEAD_DK, (lane >= HEAD_DK) & (lane < 2 * HEAD_DK)


def _head_attention(qd, ki, ke, vh, et, st_ref, h, causal):
    qb = qd.astype(BF16)
    sc = lax.dot_general(qb, ki.astype(BF16), _NT, preferred_element_type=F32)
    sc = jnp.where(causal, sc, 0.0)
    st = st_ref[h]
    o = jnp.dot(sc.astype(BF16), vh, preferred_element_type=F32)
    o = o + lax.dot_general(qb, st.astype(BF16), _NT, preferred_element_type=F32)
    st_ref[h] = st * et + lax.dot_general(vh, ke.astype(BF16), _TN, preferred_element_type=F32)
    return o


def _gla_kernel(q_ref, k_ref, v_ref, g_ref, a_ref, wa_ref, ba_ref, nw_ref, tri_ref, o_ref,
                st_ref, qd_s, ki_s, ke_s, et_s):
    @pl.when(pl.program_id(1) == 0)
    def _():
        st_ref[...] = jnp.zeros_like(st_ref)

    causal = _tri_mask()
    tl = q_ref.shape[0]
    n_chunks = tl // CHUNK

    z = jnp.dot(a_ref[...], wa_ref[...], preferred_element_type=F32) + ba_ref[...]
    la = (jnp.minimum(z, 0.0) - jnp.log1p(jnp.exp(-jnp.abs(z)))) * (1.0 / GATE_TEMP)
    hi = la.astype(BF16)
    lo = (la - hi.astype(F32)).astype(BF16)
    cum = (jnp.dot(tri_ref[...], hi, preferred_element_type=F32)
           + jnp.dot(tri_ref[...], lo, preferred_element_type=F32))
    cum3 = cum.reshape(n_chunks, CHUNK, QKP)
    tot3 = cum3[:, CHUNK - 1:CHUNK, :]
    qd = (q_ref[...].astype(F32) * (HEAD_DK ** -0.5)) * jnp.exp(cum)
    masks = _pair_masks(tl)
    for h in range(N_HEADS):
        pair = slice((h // 2) * LANES, (h // 2 + 1) * LANES)
        qd_s[:, h * HEAD_PAD:(h + 1) * HEAD_PAD] = jnp.where(masks[h % 2], qd[:, pair],
                                                              0.0).astype(BF16)
    kf = k_ref[...].astype(F32)
    ki_s[...] = (kf * jnp.exp(-cum)).astype(BF16)
    ke_s[...] = (kf * jnp.exp(tot3 - cum3).reshape(tl, QKP)).astype(BF16)
    et_s[...] = jnp.exp(tot3).reshape(n_chunks, QKP)

    def chunk(c, carry):
        r = pl.ds(pl.multiple_of(c * CHUNK, CHUNK), CHUNK)
        et = et_s[pl.ds(c, 1), :]
        for h in range(N_HEADS):
            sl = slice(h * HEAD_PAD, (h + 1) * HEAD_PAD)
            pair = slice((h // 2) * LANES, (h // 2 + 1) * LANES)
            o = _head_attention(qd_s[r, sl], ki_s[r, pair], ke_s[r, pair], v_ref[r, sl],
                                et[:, pair], st_ref, h, causal)
            ms = jnp.sum(o * o, axis=-1, keepdims=True) * (1.0 / HEAD_DV)
            y = (o * lax.rsqrt(ms + NORM_EPS)) * nw_ref[:, sl]
            g = g_ref[r, sl].astype(F32)
            o_ref[r, sl] = (y * (g * jax.nn.sigmoid(g))).astype(BF16)
        return carry

    lax.fori_loop(0, n_chunks, chunk, 0, unroll=4)


def _ret_kernel(q_ref, k_ref, v_ref, g_ref, cos_ref, sina_ref, sinb_ref, dq_ref, dki_ref,
                dke_ref, dt_ref, nw_ref, o_ref, st_ref):
    @pl.when(pl.program_id(1) == 0)
    def _():
        st_ref[...] = jnp.zeros_like(st_ref)

    causal = _tri_mask(RET_CHUNK)
    n_chunks = q_ref.shape[0] // RET_CHUNK
    lane = lax.broadcasted_iota(jnp.int32, (RET_CHUNK, HEAD_PAD), 1)
    real = lane < HEAD_DV
    masks = _pair_masks(RET_CHUNK)
    half = HEAD_DK // 2

    def rotary(t, cos, sina, sinb):
        return (t * cos + pltpu.roll(t, LANES - half, 1) * sina + pltpu.roll(t, half, 1) * sinb)

    def chunk(c, carry):
        r = pl.ds(pl.multiple_of(c * RET_CHUNK, RET_CHUNK), RET_CHUNK)
        cos, sina, sinb = cos_ref[r, :], sina_ref[r, :], sinb_ref[r, :]
        pair_q, pair_ki, pair_ke = [], [], []
        for p in range(N_PAIRS):
            ps = slice(p * LANES, (p + 1) * LANES)
            qr = rotary(q_ref[r, ps].astype(F32), cos, sina, sinb) * dq_ref[:, ps]
            kr = rotary(k_ref[r, ps].astype(F32), cos, sina, sinb) * (HEAD_DK ** -0.5)
            pair_q.append(qr)
            pair_ki.append((kr * dki_ref[:, ps]).astype(BF16))
            pair_ke.append((kr * dke_ref[:, ps]).astype(BF16))
        for h in range(N_HEADS):
            sl = slice(h * HEAD_PAD, (h + 1) * HEAD_PAD)
            p = h // 2
            qd = jnp.where(masks[h % 2], pair_q[p], 0.0)
            o = _head_attention(qd, pair_ki[p], pair_ke[p], v_ref[r, sl],
                                dt_ref[:, p * LANES:(p + 1) * LANES], st_ref, h, causal)
            mu = jnp.sum(o, axis=-1, keepdims=True) * (1.0 / HEAD_DV)
            oc = jnp.where(real, o - mu, 0.0)
            var = jnp.sum(oc * oc, axis=-1, keepdims=True) * (1.0 / HEAD_DV)
            y = (oc * lax.rsqrt(var + NORM_EPS)) * nw_ref[:, sl]
            g = g_ref[r, sl].astype(F32)
            o_ref[r, sl] = (y * (g * jax.nn.sigmoid(g))).astype(BF16)
        return carry

    lax.fori_loop(0, n_chunks, chunk, 0, unroll=2)


def _proj_spec(tl, width, col_off, per_b):
    cb = col_off // width
    return pl.BlockSpec((tl, width), lambda b, l: (b * per_b + l, cb))


def _full(shape):
    return pl.BlockSpec(shape, lambda b, l: (0,) * len(shape))


def _gla_mixer(proj, batch, seq, wa_p, ba_p, nw_p, tl=512):
    per_b = seq // tl
    pos = np.arange(tl)
    tri_bd = jnp.asarray((pos[:, None] // CHUNK == pos[None, :] // CHUNK)
                         & (pos[:, None] >= pos[None, :]), BF16)
    return pl.pallas_call(
        _gla_kernel,
        grid=(batch, per_b),
        in_specs=[_proj_spec(tl, QKP, OFF_GQ, per_b), _proj_spec(tl, QKP, OFF_GK, per_b),
                  _proj_spec(tl, HP, OFF_GV, per_b), _proj_spec(tl, HP, OFF_GG, per_b),
                  _proj_spec(tl, LANES, OFF_GA, per_b),
                  _full((LANES, QKP)), _full((1, QKP)), _full((1, HP)), _full((tl, tl))],
        out_specs=pl.BlockSpec((tl, HP), lambda b, l: (b * per_b + l, 0)),
        out_shape=jax.ShapeDtypeStruct((batch * seq, HP), BF16),
        scratch_shapes=[pltpu.VMEM((N_HEADS, HEAD_PAD, HEAD_PAD), F32),
                        pltpu.VMEM((tl, HP), BF16), pltpu.VMEM((tl, QKP), BF16),
                        pltpu.VMEM((tl, QKP), BF16), pltpu.VMEM((tl // CHUNK, QKP), F32)],
        compiler_params=_cparams(("arbitrary", "arbitrary")),
        name="gla_mixer",
    )(proj, proj, proj, proj, proj, wa_p, ba_p, nw_p, tri_bd)


def _ret_mixer(proj, batch, seq, cos_t, sina_t, sinb_t, dq, dki, dke, dtot, nw_p, tl=512):
    per_b = seq // tl
    return pl.pallas_call(
        _ret_kernel,
        grid=(batch, per_b),
        in_specs=[_proj_spec(tl, QKP, OFF_RQ, per_b), _proj_spec(tl, QKP, OFF_RK, per_b),
                  _proj_spec(tl, HP, OFF_RV, per_b), _proj_spec(tl, HP, OFF_RG, per_b),
                  pl.BlockSpec((tl, LANES), lambda b, l: (l, 0)),
                  pl.BlockSpec((tl, LANES), lambda b, l: (l, 0)),
                  pl.BlockSpec((tl, LANES), lambda b, l: (l, 0)),
                  _full((RET_CHUNK, QKP)), _full((RET_CHUNK, QKP)), _full((RET_CHUNK, QKP)),
                  _full((1, QKP)), _full((1, HP))],
        out_specs=pl.BlockSpec((tl, HP), lambda b, l: (b * per_b + l, 0)),
        out_shape=jax.ShapeDtypeStruct((batch * seq, HP), BF16),
        scratch_shapes=[pltpu.VMEM((N_HEADS, HEAD_PAD, HEAD_PAD), F32)],
        compiler_params=_cparams(("arbitrary", "arbitrary")),
        name="ret_mixer",
    )(proj, proj, proj, proj, cos_t, sina_t, sinb_t, dq, dki, dke, dtot, nw_p)


S5_N = S5_GROUPS * S5_STATE


def _gelu_tanh(x):
    return 0.5 * x * (1.0 + jnp.tanh(np.sqrt(2.0 / np.pi) * (x + 0.044715 * (x * x * x))))


def _s5_kernel(u_ref, bb_ref, cb_ref, nr_ref, ni_ref, pr_ref, pi_ref, lr_ref, li_ref,
               d_ref, wg_ref, bg_ref, o_ref, sr_ref, si_ref, x_scr, s_scr):
    @pl.when(pl.program_id(1) == 0)
    def _():
        sr_ref[...] = jnp.zeros_like(sr_ref)
        si_ref[...] = jnp.zeros_like(si_ref)

    tri = _tri_mask().astype(BF16)
    n_chunks = u_ref.shape[0] // CHUNK
    u = u_ref[...]
    x_scr[...] = jnp.dot(u, bb_ref[...], preferred_element_type=F32)

    def chunk(c, carry):
        r = pl.ds(pl.multiple_of(c * CHUNK, CHUNK), CHUNK)
        xr, xi = x_scr[r, :S5_N], x_scr[r, S5_N:]
        nr, ni = nr_ref[...], ni_ref[...]
        p_r = jnp.dot(tri, (xr * nr - xi * ni).astype(BF16), preferred_element_type=F32)
        p_i = jnp.dot(tri, (xr * ni + xi * nr).astype(BF16), preferred_element_type=F32)
        s0r, s0i = sr_ref[...], si_ref[...]
        lr, li = lr_ref[...], li_ref[...]
        q_r = p_r + (s0r * lr - s0i * li)
        q_i = p_i + (s0r * li + s0i * lr)
        pr, pi = pr_ref[...], pi_ref[...]
        s_r = q_r * pr - q_i * pi
        s_i = q_r * pi + q_i * pr
        sr_ref[...] = s_r[CHUNK - 1:CHUNK, :]
        si_ref[...] = s_i[CHUNK - 1:CHUNK, :]
        s_scr[r, :S5_N] = s_r.astype(BF16)
        s_scr[r, S5_N:] = s_i.astype(BF16)
        return carry

    lax.fori_loop(0, n_chunks, chunk, 0, unroll=2)
    y = jnp.dot(s_scr[...], cb_ref[...], preferred_element_type=F32)
    y = _gelu_tanh(y + d_ref[...] * u.astype(F32))
    gate = jnp.dot(y.astype(BF16), wg_ref[...], preferred_element_type=F32) + bg_ref[...]
    o_ref[...] = (y * jax.nn.sigmoid(gate)).astype(BF16)


def _s5_mixer(proj, batch, seq, tabs, tl=512):
    per_b = seq // tl
    bb, cb, nr, ni, pr, pi, lr, li, dsk, wg, bg = tabs
    return pl.pallas_call(
        _s5_kernel,
        grid=(batch, per_b),
        in_specs=[_proj_spec(tl, S5_WIDTH, OFF_SU, per_b),
                  _full((S5_WIDTH, 2 * S5_N)), _full((2 * S5_N, S5_WIDTH)),
                  _full((CHUNK, S5_N)), _full((CHUNK, S5_N)),
                  _full((CHUNK, S5_N)), _full((CHUNK, S5_N)),
                  _full((1, S5_N)), _full((1, S5_N)),
                  _full((1, S5_WIDTH)), _full((S5_WIDTH, S5_WIDTH)), _full((1, S5_WIDTH))],
        out_specs=pl.BlockSpec((tl, S5_WIDTH), lambda b, l: (b * per_b + l, 0)),
        out_shape=jax.ShapeDtypeStruct((batch * seq, S5_WIDTH), BF16),
        scratch_shapes=[pltpu.VMEM((1, S5_N), F32), pltpu.VMEM((1, S5_N), F32),
                        pltpu.VMEM((tl, 2 * S5_N), F32), pltpu.VMEM((tl, 2 * S5_N), BF16)],
        compiler_params=_cparams(("arbitrary", "arbitrary")),
        name="s5_mixer",
    )(proj, bb, cb, nr, ni, pr, pi, lr, li, dsk, wg, bg)


def _s5_tables(a_re, a_im, log_dt, b_re, b_im, c_re, c_im, d_skip, w_glu, b_glu):
    lam = lax.complex(a_re, a_im)
    dt = jnp.exp(log_dt)[:, None]
    lam_bar = jnp.exp(lam * dt)
    b_bar = ((lam_bar - 1.0) / lam)[..., None] * lax.complex(b_re, b_im)
    eye = jnp.eye(S5_GROUPS, dtype=F32)
    def blk_b(m):
        return jnp.einsum('gph,gk->ghkp', m, eye).reshape(S5_WIDTH, S5_N)
    bb = jnp.concatenate([blk_b(jnp.real(b_bar)), blk_b(jnp.imag(b_bar))], axis=1)
    def blk_c(m):
        return jnp.einsum('ghp,gk->kpgh', m, eye).reshape(S5_N, S5_WIDTH)
    cb = jnp.concatenate([blk_c(c_re), blk_c(-c_im)], axis=0)
    steps = jnp.arange(CHUNK, dtype=F32)[:, None, None]
    lam_dt = (lam * dt)[None]
    pos = jnp.exp(lam_dt * steps).reshape(CHUNK, S5_N)
    neg = jnp.exp(-lam_dt * steps).reshape(CHUNK, S5_N)
    one = lam_bar.reshape(1, S5_N)
    return (bb.astype(BF16), cb.astype(BF16), jnp.real(neg), jnp.imag(neg), jnp.real(pos),
            jnp.imag(pos), jnp.real(one), jnp.imag(one), d_skip.reshape(1, S5_WIDTH),
            w_glu.astype(BF16), b_glu.reshape(1, S5_WIDTH))


def _pack_bf16_pairs(x):
    w = x.shape[1] // 2
    xb = x.astype(BF16).astype(F32)
    hi = lax.bitcast_convert_type(xb[:, :w], jnp.uint32)
    lo = lax.bitcast_convert_type(xb[:, w:], jnp.uint32)
    return hi | (lo >> 16)


def _unpack_bf16_pairs(p):
    hi = lax.bitcast_convert_type(p & jnp.uint32(0xFFFF0000), F32)
    lo = lax.bitcast_convert_type(p << 16, F32)
    return hi, lo


def _router_kernel(og_ref, or_ref, os_ref, wg_ref, wr_ref, ws_ref, x_ref, g1_ref, sc_ref,
                   sh_ref, nw_ref, rw_ref, rb_ref,
                   x1_ref, h_ref, idx_ref, gate_ref, rank_ref, cnt_ref, carry_ref):
    i = pl.program_id(0)

    @pl.when(i == 0)
    def _():
        carry_ref[...] = jnp.zeros_like(carry_ref)

    mix = (jnp.dot(og_ref[...], wg_ref[...], preferred_element_type=F32)
           + jnp.dot(or_ref[...], wr_ref[...], preferred_element_type=F32)
           + jnp.dot(os_ref[...], ws_ref[...], preferred_element_type=F32))
    x1 = x_ref[...] + g1_ref[0] * mix
    x1_ref[...] = x1
    hdn = _rms_mod(x1, nw_ref[...], sc_ref[0], sh_ref[0])
    h_ref[...] = _pack_bf16_pairs(hdn)
    h_hi = hdn.astype(BF16)
    h_lo = (hdn - h_hi.astype(F32)).astype(BF16)
    p = lax.dot_general(rw_ref[...], h_hi, _NT, preferred_element_type=F32)
    q = lax.dot_general(rw_ref[:N_EXPERTS, :], h_lo, _NT, preferred_element_type=F32)
    logits = (p[:N_EXPERTS] + p[N_EXPERTS:] + q) + rb_ref[:, 0:1]
    tm = logits.shape[1]
    eidx = lax.broadcasted_iota(jnp.int32, (N_EXPERTS, tm), 0)
    work = logits
    onehot = jnp.zeros((N_EXPERTS, tm), F32)
    vals, idxs, sels = [], [], []
    for _ in range(TOP_K):
        m = jnp.max(work, axis=0, keepdims=True)
        ix = jnp.min(jnp.where(work == m, eidx, N_EXPERTS), axis=0, keepdims=True)
        sel = eidx == ix
        work = jnp.where(sel, -jnp.inf, work)
        onehot = onehot + sel.astype(F32)
        vals.append(m)
        idxs.append(ix)
        sels.append(sel)
    exps = [jnp.exp(v - vals[0]) for v in vals]
    denom = exps[0] + exps[1] + exps[2] + exps[3]
    r = lax.broadcasted_iota(jnp.int32, (tm, tm), 0)
    c = lax.broadcasted_iota(jnp.int32, (tm, tm), 1)
    earlier = (r < c).astype(BF16)
    before = (jnp.dot(onehot.astype(BF16), earlier, preferred_element_type=F32)
              + carry_ref[:, 0:1])
    row8 = lax.broadcasted_iota(jnp.int32, (8, tm), 0)
    idx_out = jnp.zeros((8, tm), jnp.int32)
    gate_out = jnp.zeros((8, tm), F32)
    rank_out = jnp.zeros((8, tm), F32)
    for k in range(TOP_K):
        rk = jnp.sum(jnp.where(sels[k], before, 0.0), axis=0, keepdims=True)
        idx_out = jnp.where(row8 == k, idxs[k], idx_out)
        gate_out = jnp.where(row8 == k, exps[k] / denom, gate_out)
        rank_out = jnp.where(row8 == k, rk, rank_out)
    idx_ref[...] = idx_out
    gate_ref[...] = gate_out
    rank_ref[...] = rank_out.astype(jnp.int32)
    total = carry_ref[...] + jnp.sum(onehot, axis=1, keepdims=True)
    carry_ref[...] = total
    cnt_ref[...] = total


def _outproj_router(o_gla, o_ret, o_s5, wg, wr, ws, x3, grp, g1, sc2, sh2, nw2, rw_p, rb_p, seq,
                    tm=512):
    _, t, d = x3.shape
    per_b = seq // tm
    row = lambda w: pl.BlockSpec((tm, w), lambda i: (i, 0))
    full = lambda s: pl.BlockSpec(s, lambda i: (0,) * len(s))
    per_batch = pl.BlockSpec((1, 1, d), lambda i: (i // per_b, 0, 0))
    slot_t = pl.BlockSpec((8, tm), lambda i: (0, i))
    return pl.pallas_call(
        _router_kernel,
        grid=(t // tm,),
        in_specs=[row(HP), row(HP), row(S5_WIDTH), full((HP, d)), full((HP, d)),
                  full((S5_WIDTH, d)), pl.BlockSpec((None, tm, d), lambda i: (grp, i, 0)),
                  per_batch, per_batch, per_batch,
                  full((1, d)), full((2 * N_EXPERTS, d)), full((N_EXPERTS, LANES))],
        out_specs=[row(d), row(d // 2), slot_t, slot_t, slot_t, full((N_EXPERTS, LANES))],
        out_shape=[jax.ShapeDtypeStruct((t, d), F32),
                   jax.ShapeDtypeStruct((t, d // 2), jnp.uint32),
                   jax.ShapeDtypeStruct((8, t), jnp.int32),
                   jax.ShapeDtypeStruct((8, t), F32),
                   jax.ShapeDtypeStruct((8, t), jnp.int32),
                   jax.ShapeDtypeStruct((N_EXPERTS, LANES), F32)],
        scratch_shapes=[pltpu.VMEM((N_EXPERTS, LANES), F32)],
        compiler_params=_cparams(("arbitrary",)),
        name="outproj_router",
    )(o_gla, o_ret, o_s5, wg, wr, ws, x3, g1, sc2, sh2, nw2, rw_p, rb_p)


GATHER_WIN = 64


def _gather_rows(table, idx):
    m = idx.shape[0]
    w = table.shape[1]
    mesh = plsc.VectorSubcoreMesh(core_axis_name="core", subcore_axis_name="subcore")

    @functools.partial(pl.kernel, out_type=jax.ShapeDtypeStruct((m, w), table.dtype),
                       mesh=mesh, name="sc_row_gather")
    def gather(x_hbm, i_hbm, o_hbm):
        def body(i_vmem, o_vmem):
            pltpu.sync_copy(x_hbm.at[i_vmem], o_vmem)

        pltpu.emit_pipeline(
            body,
            grid=(m // GATHER_WIN,),
            in_specs=[pl.BlockSpec((GATHER_WIN,), lambda i: (i,))],
            out_specs=[pl.BlockSpec((GATHER_WIN, w), lambda i: (i, 0))],
            core_axis_name=("core", "subcore"),
            dimension_semantics=(pltpu.PARALLEL,),
        )(i_hbm, o_hbm)

    return gather(table, idx)


def _scatter_rows(x, dest_slot_major, n_rows):
    t, w = x.shape
    steps = t // GATHER_WIN
    mesh = plsc.VectorSubcoreMesh(core_axis_name="core", subcore_axis_name="subcore")

    @functools.partial(pl.kernel, out_type=jax.ShapeDtypeStruct((n_rows, w), x.dtype),
                       mesh=mesh, name="sc_row_scatter")
    def scatter(x_hbm, i_hbm, o_hbm):
        def body(x_vmem, i0, i1, i2, i3):
            for i_vmem in (i0, i1, i2, i3):
                pltpu.sync_copy(x_vmem, o_hbm.at[i_vmem])

        slot = lambda k: pl.BlockSpec((GATHER_WIN,), lambda i: (k * steps + i,))
        pltpu.emit_pipeline(
            body,
            grid=(steps,),
            in_specs=[pl.BlockSpec((GATHER_WIN, w), lambda i: (i, 0)),
                      slot(0), slot(1), slot(2), slot(3)],
            out_specs=[],
            core_axis_name=("core", "subcore"),
            dimension_semantics=(pltpu.PARALLEL,),
        )(x_hbm, i_hbm, i_hbm, i_hbm, i_hbm)

    return scatter(x, dest_slot_major)


def _expert_kernel(be_ref, nv_ref, rows_ref, wu_ref, bu_ref, wd_ref, bd_ref, o_ref,
                   wu_bf, wd_bf):
    i = pl.program_id(0)
    e = be_ref[i]
    prev = be_ref[jnp.maximum(i - 1, 0)]

    @pl.when((i == 0) | (e != prev))
    def _():
        wu_bf[...] = wu_ref[0, 0].astype(BF16)
        wd_bf[...] = wd_ref[0, 0].astype(BF16)

    for s in range(ROW_BLK // ROW_SUB):
        rs = slice(s * ROW_SUB, (s + 1) * ROW_SUB)
        left = nv_ref[i] - s * ROW_SUB

        @pl.when(left > 0)
        def _():
            row = lax.broadcasted_iota(jnp.int32, (ROW_SUB, rows_ref.shape[1]), 0)
            x_hi, x_lo = _unpack_bf16_pairs(jnp.where(row < left, rows_ref[rs, :], jnp.uint32(0)))
            x = jnp.concatenate([x_hi.astype(BF16), x_lo.astype(BF16)], axis=1)
            up = jnp.dot(x, wu_bf[...], preferred_element_type=F32) + bu_ref[0, 0]
            x_glu = jnp.minimum(up[:, :D_FF], SWIGLU_LIMIT)
            x_lin = jnp.clip(up[:, D_FF:], -SWIGLU_LIMIT, SWIGLU_LIMIT)
            act = x_glu * jax.nn.sigmoid(SWIGLU_ALPHA * x_glu) * (x_lin + 1.0)
            o_ref[rs, :] = _pack_bf16_pairs(
                jnp.dot(act.astype(BF16), wd_bf[...], preferred_element_type=F32) + bd_ref[0, 0])

        @pl.when(left <= 0)
        def _():
            o_ref[rs, :] = jnp.zeros((ROW_SUB, o_ref.shape[1]), o_ref.dtype)


def _experts(layer, block_e, n_valid, rows, w_up, b_up, w_down, b_down):
    n_rows, dh = rows.shape
    d = 2 * dh
    n_blocks = n_rows // ROW_BLK
    depth, ne, _, f2 = w_up.shape
    wsel = lambda i, be, nu: (layer, be[i], 0, 0)
    grid_spec = pltpu.PrefetchScalarGridSpec(
        num_scalar_prefetch=2,
        grid=(n_blocks,),
        in_specs=[pl.BlockSpec((ROW_BLK, dh), lambda i, be, nu: (i, 0)),
                  pl.BlockSpec((1, 1, d, f2), wsel),
                  pl.BlockSpec((1, 1, 1, f2), wsel),
                  pl.BlockSpec((1, 1, D_FF, d), wsel),
                  pl.BlockSpec((1, 1, 1, d), wsel)],
        out_specs=pl.BlockSpec((ROW_BLK, dh), lambda i, be, nu: (i, 0)),
        scratch_shapes=[pltpu.VMEM((d, f2), BF16), pltpu.VMEM((D_FF, d), BF16)],
    )
    return pl.pallas_call(
        _expert_kernel,
        grid_spec=grid_spec,
        out_shape=jax.ShapeDtypeStruct((n_rows, dh), jnp.uint32),
        compiler_params=_cparams(("arbitrary",)),
        name="moe_experts",
    )(block_e, n_valid, rows, w_up, b_up.reshape(depth, ne, 1, f2), w_down,
      b_down.reshape(depth, ne, 1, d))


def _combine_kernel(y0_ref, y1_ref, y2_ref, y3_ref, gate_ref, x1_ref, g2_ref, fw_ref, *rest,
                    final):
    o_ref = rest[-1]
    gates = gate_ref[...]
    y_hi, y_lo = None, None
    for k, y_ref in enumerate((y0_ref, y1_ref, y2_ref, y3_ref)):
        hi, lo = _unpack_bf16_pairs(y_ref[...])
        g = gates[:, k:k + 1]
        y_hi = g * hi if y_hi is None else y_hi + g * hi
        y_lo = g * lo if y_lo is None else y_lo + g * lo
    y = jnp.concatenate([y_hi, y_lo], axis=1)
    x2 = x1_ref[...] + g2_ref[0] * y
    if final:
        x2 = (x2 * lax.rsqrt(jnp.mean(x2 * x2, axis=-1, keepdims=True) + NORM_EPS)) * fw_ref[...]
    o_ref[...] = x2


def _combine(y4, gates, x1, g2, fw, seq, final, grp, n_out, prev_out, th=256):
    t, d = x1.shape
    steps = t // th
    per_b = seq // th
    slot = lambda k: pl.BlockSpec((th, d // 2), lambda i: (k * steps + i, 0))
    return pl.pallas_call(
        functools.partial(_combine_kernel, final=final),
        grid=(steps,),
        in_specs=[slot(0), slot(1), slot(2), slot(3),
                  pl.BlockSpec((th, TOP_K), lambda i: (i, 0)),
                  pl.BlockSpec((th, d), lambda i: (i, 0)),
                  pl.BlockSpec((1, 1, d), lambda i: (i // per_b, 0, 0)),
                  pl.BlockSpec((1, d), lambda i: (0, 0))]
                 + ([] if prev_out is None else [pl.BlockSpec(memory_space=pl.ANY)]),
        out_specs=pl.BlockSpec((None, th, d), lambda i: (grp, i, 0)),
        out_shape=jax.ShapeDtypeStruct((n_out, t, d), F32),
        input_output_aliases={} if prev_out is None else {8: 0},
        compiler_params=_cparams(("arbitrary",)),
        name="moe_combine",
    )(y4, y4, y4, y4, gates, x1, g2, fw, *([] if prev_out is None else [prev_out]))


def _retention_tables(seq):
    pos = jnp.arange(seq, dtype=F32)
    inv_freq = ROPE_BASE ** (-jnp.arange(0, HEAD_DK, 2, dtype=F32) / HEAD_DK)
    ang = pos[:, None] * inv_freq[None, :]
    cos, sin = jnp.cos(ang), jnp.sin(ang)
    zero = jnp.zeros_like(sin)
    zpad = jnp.zeros((seq, LANES - 2 * HEAD_DK), F32)
    cos_t = jnp.concatenate([cos, cos, cos, cos, zpad], axis=1)
    sina_t = jnp.concatenate([-sin, zero, -sin, zero, zpad], axis=1)
    sinb_t = jnp.concatenate([zero, sin, zero, sin, zpad], axis=1)
    log_gamma = jnp.log1p(-jnp.exp2(-5.0 - jnp.arange(N_HEADS, dtype=F32)))
    log_decay = jnp.broadcast_to(log_gamma[None, :, None], (RET_CHUNK, N_HEADS, HEAD_DK))
    cum = jnp.cumsum(log_decay, axis=0)
    tot = cum[-1:]
    shp = lambda a: _take_cols(a.reshape(a.shape[0], N_HEADS * HEAD_DK), _DK_SRC) + jnp.asarray(
        _DK_SRC < 0, F32)
    return (cos_t, sina_t, sinb_t, shp(jnp.exp(cum)), shp(jnp.exp(-cum)), shp(jnp.exp(tot - cum)),
            shp(jnp.exp(tot)))


def kernel(x, c, norm1_w, norm2_w, w_mod, b_mod, w_in, gla_w_a2, gla_b_a, gla_norm_w, ret_norm_w, s5_a_re, s5_a_im, s5_log_dt, s5_b_re, s5_b_im, s5_c_re, s5_c_im, s5_d, s5_w_glu, s5_b_glu, w_out, router_w, router_b, w_up, b_up, w_down, b_down, final_norm_w):
    batch, seq, d = x.shape
    depth = w_mod.shape[0]
    gb = batch // N_STREAMS
    t = gb * seq
    n_slots = t * TOP_K
    n_blocks = n_slots // ROW_BLK + N_EXPERTS
    n_rows = n_blocks * ROW_BLK

    mod = _modulation(c, w_mod, b_mod)
    ret_tabs = _retention_tables(seq)
    xs = [(x.reshape(N_STREAMS, t, d), g) for g in range(N_STREAMS)]
    out = None

    for i in range(depth):
        w_p = _take_cols(w_in[i], _IN_SRC).astype(BF16)
        wa_p = jnp.zeros((LANES, QKP), F32).at[:GATE_RANK].set(_take_cols(gla_w_a2[i], _DK_SRC))
        ba_p = _take_cols(gla_b_a[i], _DK_SRC).reshape(1, QKP)
        gnw = _take_cols(gla_norm_w[i], _DV_SRC).reshape(1, HP)
        rnw = _take_cols(ret_norm_w[i], _DV_SRC).reshape(1, HP)
        kv = N_HEADS * HEAD_DV
        wo_g = _take_rows(w_out[i, :kv], _DV_SRC).astype(BF16)
        wo_r = _take_rows(w_out[i, kv:2 * kv], _DV_SRC).astype(BF16)
        wo_s = w_out[i, 2 * kv:].astype(BF16)
        rw_t = router_w[i].T
        rw_hi = rw_t.astype(BF16)
        rw_p = jnp.concatenate([rw_hi, (rw_t - rw_hi.astype(F32)).astype(BF16)], axis=0)
        rb_p = jnp.broadcast_to(router_b[i][:, None], (N_EXPERTS, LANES))
        s5_tabs = _s5_tables(s5_a_re[i], s5_a_im[i], s5_log_dt[i], s5_b_re[i], s5_b_im[i],
                             s5_c_re[i], s5_c_im[i], s5_d[i], s5_w_glu[i], s5_b_glu[i])
        final = i == depth - 1

        for g in range(N_STREAMS):
            x3, grp = xs[g]
            m6 = mod[i, g * gb:(g + 1) * gb].reshape(gb, 6, 1, d)
            sh1, sc1, g1, sh2, sc2, g2 = (m6[:, j] for j in range(6))

            proj = _in_projection(x3, grp, sc1, sh1, norm1_w[i].reshape(1, d), w_p, seq)
            o_gla = _gla_mixer(proj, gb, seq, wa_p.astype(BF16), ba_p, gnw)
            o_ret = _ret_mixer(proj, gb, seq, *ret_tabs, rnw)
            o_s5 = _s5_mixer(proj, gb, seq, s5_tabs)

            x1, hdn, idx, gates, rank, counts = _outproj_router(
                o_gla, o_ret, o_s5, wo_g, wo_r, wo_s, x3, grp, g1, sc2, sh2,
                norm2_w[i].reshape(1, d), rw_p, rb_p, seq)

            cnt = counts[:, 0].astype(jnp.int32)
            padded = (cnt + ROW_BLK - 1) // ROW_BLK * ROW_BLK
            pad_ends = jnp.cumsum(padded)
            pad_starts = pad_ends - padded
            slot_start = jnp.sum(jnp.where(idx[:TOP_K, :, None] == jnp.arange(N_EXPERTS),
                                           pad_starts, 0), axis=-1)
            dest_sm = (slot_start + rank[:TOP_K]).astype(jnp.int32).reshape(-1)
            gates_tm = gates[:TOP_K].T
            blk_start = jnp.arange(n_blocks, dtype=jnp.int32) * ROW_BLK
            block_e = jnp.minimum(jnp.sum(pad_ends[None, :] <= blk_start[:, None], axis=1),
                                  N_EXPERTS - 1).astype(jnp.int32)
            n_valid = jnp.clip((pad_starts + cnt)[block_e] - blk_start, 0,
                               ROW_BLK).astype(jnp.int32)

            rows = _scatter_rows(hdn, dest_sm, n_rows)
            out_rows = _experts(i, block_e, n_valid, rows, w_up, b_up, w_down, b_down)
            y4 = _gather_rows(out_rows, dest_sm)
            if final:
                out = _combine(y4, gates_tm, x1, g2, final_norm_w.reshape(1, d), seq, True,
                               g, N_STREAMS, out)
            else:
                xs[g] = (_combine(y4, gates_tm, x1, g2, final_norm_w.reshape(1, d), seq, False,
                                  0, 1, None), 0)

    return out.reshape(batch, seq, d)
```

```python
import functools

import numpy as np
import jax
import jax.numpy as jnp
from jax import lax
from jax.experimental import pallas as pl
from jax.experimental.pallas import tpu as pltpu
from jax.experimental.pallas import tpu_sc as plsc

D_MODEL = 1024
CHUNK = 64
RET_CHUNK = 128
NORM_EPS = 1e-5
N_HEADS = 4
HEAD_DK = 48
HEAD_DV = 96
GATE_RANK = 16
GATE_TEMP = 16.0
ROPE_BASE = 10000.0
S5_WIDTH = 256
S5_GROUP_DIM = 16
S5_GROUPS = 16
S5_STATE = 64
N_EXPERTS = 32
TOP_K = 4
D_FF = 1024
SWIGLU_LIMIT = 7.0
SWIGLU_ALPHA = 1.702

LANES = 128
HEAD_PAD = LANES
HP = N_HEADS * HEAD_PAD
N_PAIRS = N_HEADS // 2
QKP = N_PAIRS * LANES
VMEM_LIMIT = 56 * 1024 * 1024

OFF_GQ, OFF_GK, OFF_GV, OFF_GG = 0, QKP, 2 * QKP, 2 * QKP + HP
OFF_RQ = OFF_GG + HP
OFF_RK, OFF_RV, OFF_RG = OFF_RQ + QKP, OFF_RQ + 2 * QKP, OFF_RQ + 2 * QKP + HP
OFF_SU = OFF_RG + HP
OFF_GA = OFF_SU + S5_WIDTH
NP_COLS = OFF_GA + LANES
PROJ_CH = 1152

ROW_BLK = 1024
ROW_SUB = 512
MOD_SH1, MOD_SC1, MOD_G1, MOD_SH2, MOD_SC2, MOD_G2 = range(6)
N_STREAMS = 1

F32 = jnp.float32
BF16 = jnp.bfloat16


def _DK_SRC_LANE(h, d):
    return (h // 2) * LANES + (h % 2) * HEAD_DK + d


def _in_col_map():
    src = -np.ones((NP_COLS,), np.int64)
    kq = N_HEADS * HEAD_DK
    kv = N_HEADS * HEAD_DV
    base = dict(gq=0, gk=kq, gv=2 * kq, gg=2 * kq + kv, ga=2 * kq + 2 * kv)
    r0 = base['ga'] + GATE_RANK
    base.update(rq=r0, rk=r0 + kq, rv=r0 + 2 * kq, rg=r0 + 2 * kq + kv, su=r0 + 2 * kq + 2 * kv)
    for h in range(N_HEADS):
        for d in range(HEAD_DK):
            lane = _DK_SRC_LANE(h, d)
            src[OFF_GQ + lane] = base['gq'] + h * HEAD_DK + d
            src[OFF_GK + lane] = base['gk'] + h * HEAD_DK + d
            src[OFF_RQ + lane] = base['rq'] + h * HEAD_DK + d
            src[OFF_RK + lane] = base['rk'] + h * HEAD_DK + d
        for d in range(HEAD_DV):
            src[OFF_GV + h * HEAD_PAD + d] = base['gv'] + h * HEAD_DV + d
            src[OFF_GG + h * HEAD_PAD + d] = base['gg'] + h * HEAD_DV + d
            src[OFF_RV + h * HEAD_PAD + d] = base['rv'] + h * HEAD_DV + d
            src[OFF_RG + h * HEAD_PAD + d] = base['rg'] + h * HEAD_DV + d
    src[OFF_SU:OFF_SU + S5_WIDTH] = base['su'] + np.arange(S5_WIDTH)
    src[OFF_GA:OFF_GA + GATE_RANK] = base['ga'] + np.arange(GATE_RANK)
    return src


_IN_SRC = _in_col_map()


def _head_pad_map(width):
    src = -np.ones((HP,), np.int64)
    for h in range(N_HEADS):
        src[h * HEAD_PAD:h * HEAD_PAD + width] = h * width + np.arange(width)
    return src


_DV_SRC = _head_pad_map(HEAD_DV)
_DK_SRC = -np.ones((QKP,), np.int64)
for _h in range(N_HEADS):
    for _d in range(HEAD_DK):
        _DK_SRC[_DK_SRC_LANE(_h, _d)] = _h * HEAD_DK + _d


def _take_cols(w, src):
    out = jnp.take(w, jnp.asarray(np.maximum(src, 0)), axis=-1)
    return jnp.where(jnp.asarray(src >= 0), out, 0)


def _take_rows(w, src):
    out = jnp.take(w, jnp.asarray(np.maximum(src, 0)), axis=-2)
    return jnp.where(jnp.asarray(src >= 0)[:, None], out, 0)


def _layer_spec(shape, layer):
    return pl.BlockSpec((None,) + tuple(shape), lambda *_: (layer,) + (0,) * len(shape))


def _mod_spec(layer, which, per_b, b0):
    return pl.BlockSpec((None, None, 1, 1, D_MODEL),
                        lambda i: (layer, which, b0 + i // per_b, 0, 0))


def _cparams(sem):
    return pltpu.CompilerParams(dimension_semantics=sem, vmem_limit_bytes=VMEM_LIMIT)


def _mod_kernel(c_ref, w_ref, b_ref, o_ref):
    c = c_ref[...]
    cond = c * jax.nn.sigmoid(c)
    o_ref[0] = jnp.dot(cond, w_ref[0], preferred_element_type=F32,
                       precision=lax.Precision.HIGHEST) + b_ref[0]


def _modulation(c, w_mod, b_mod):
    depth, d, n = w_mod.shape
    b = c.shape[0]
    nb = 1536
    return pl.pallas_call(
        _mod_kernel,
        grid=(depth, n // nb),
        in_specs=[pl.BlockSpec((b, d), lambda l, j: (0, 0)),
                  pl.BlockSpec((1, d, nb), lambda l, j: (l, 0, j)),
                  pl.BlockSpec((1, 1, nb), lambda l, j: (l, 0, j))],
        out_specs=pl.BlockSpec((1, b, nb), lambda l, j: (l, 0, j)),
        out_shape=jax.ShapeDtypeStruct((depth, b, n), F32),
        compiler_params=_cparams(("arbitrary", "arbitrary")),
        name="adaln_mod",
    )(c, w_mod, b_mod.reshape(depth, 1, n))


def _rms_mod(x, nw, sc, sh):
    y = x * lax.rsqrt(jnp.mean(x * x, axis=-1, keepdims=True) + NORM_EPS)
    return (y * nw) * (1.0 + sc) + sh


def _inproj_kernel(x_ref, sc_ref, sh_ref, nw_ref, w_ref, o_ref):
    h = _rms_mod(x_ref[...], nw_ref[...], sc_ref[0], sh_ref[0]).astype(BF16)
    for j in range(NP_COLS // PROJ_CH):
        cs = slice(j * PROJ_CH, (j + 1) * PROJ_CH)
        o_ref[:, cs] = jnp.dot(h, w_ref[:, cs], preferred_element_type=F32).astype(BF16)


def _in_projection(x3, grp, mod5, b0, layer, nw, w_p, seq, tm=512):
    _, t, d = x3.shape
    per_b = seq // tm
    return pl.pallas_call(
        _inproj_kernel,
        grid=(t // tm,),
        in_specs=[pl.BlockSpec((None, tm, d), lambda i: (grp, i, 0)),
                  _mod_spec(layer, MOD_SC1, per_b, b0), _mod_spec(layer, MOD_SH1, per_b, b0),
                  _layer_spec((1, d), layer), _layer_spec((d, NP_COLS), layer)],
        out_specs=pl.BlockSpec((tm, NP_COLS), lambda i: (i, 0)),
        out_shape=jax.ShapeDtypeStruct((t, NP_COLS), BF16),
        compiler_params=_cparams(("arbitrary",)),
        name="in_proj",
    )(x3, mod5, mod5, nw, w_p)


_NT = (((1,), (1,)), ((), ()))
_TN = (((0,), (0,)), ((), ()))


def _tri_mask(n=CHUNK):
    r = lax.broadcasted_iota(jnp.int32, (n, n), 0)
    c = lax.broadcasted_iota(jnp.int32, (n, n), 1)
    return r >= c


def _pair_masks(rows):
    lane = lax.broadcasted_iota(jnp.int32, (rows, LANES), 1)
    return lane < HEAD_DK, (lane >= HEAD_DK) & (lane < 2 * HEAD_DK)


def _head_attention(qd, ki, ke, vh, et, st_ref, h, causal):
    qb = qd.astype(BF16)
    sc = lax.dot_general(qb, ki.astype(BF16), _NT, preferred_element_type=F32)
    sc = jnp.where(causal, sc, 0.0)
    st = st_ref[h]
    o = jnp.dot(sc.astype(BF16), vh, preferred_element_type=F32)
    o = o + lax.dot_general(qb, st.astype(BF16), _NT, preferred_element_type=F32)
    st_ref[h] = st * et + lax.dot_general(vh, ke.astype(BF16), _TN, preferred_element_type=F32)
    return o


def _gla_kernel(q_ref, k_ref, v_ref, g_ref, a_ref, wa_ref, ba_ref, nw_ref, tri_ref, o_ref,
                st_ref, qd_s, ki_s, ke_s, et_s):
    @pl.when(pl.program_id(1) == 0)
    def _():
        st_ref[...] = jnp.zeros_like(st_ref)

    causal = _tri_mask()
    tl = q_ref.shape[0]
    n_chunks = tl // CHUNK

    z = jnp.dot(a_ref[...], wa_ref[...], preferred_element_type=F32) + ba_ref[...]
    la = (jnp.minimum(z, 0.0) - jnp.log1p(jnp.exp(-jnp.abs(z)))) * (1.0 / GATE_TEMP)
    hi = la.astype(BF16)
    lo = (la - hi.astype(F32)).astype(BF16)
    cum = (jnp.dot(tri_ref[...], hi, preferred_element_type=F32)
           + jnp.dot(tri_ref[...], lo, preferred_element_type=F32))
    cum3 = cum.reshape(n_chunks, CHUNK, QKP)
    tot3 = cum3[:, CHUNK - 1:CHUNK, :]
    qd = (q_ref[...].astype(F32) * (HEAD_DK ** -0.5)) * jnp.exp(cum)
    masks = _pair_masks(tl)
    for h in range(N_HEADS):
        pair = slice((h // 2) * LANES, (h // 2 + 1) * LANES)
        qd_s[:, h * HEAD_PAD:(h + 1) * HEAD_PAD] = jnp.where(masks[h % 2], qd[:, pair],
                                                              0.0).astype(BF16)
    kf = k_ref[...].astype(F32)
    ki_s[...] = (kf * jnp.exp(-cum)).astype(BF16)
    ke_s[...] = (kf * jnp.exp(tot3 - cum3).reshape(tl, QKP)).astype(BF16)
    et_s[...] = jnp.exp(tot3).reshape(n_chunks, QKP)

    def chunk(c, carry):
        r = pl.ds(pl.multiple_of(c * CHUNK, CHUNK), CHUNK)
        et = et_s[pl.ds(c, 1), :]
        for h in range(N_HEADS):
            sl = slice(h * HEAD_PAD, (h + 1) * HEAD_PAD)
            pair = slice((h // 2) * LANES, (h // 2 + 1) * LANES)
            o = _head_attention(qd_s[r, sl], ki_s[r, pair], ke_s[r, pair], v_ref[r, sl],
                                et[:, pair], st_ref, h, causal)
            ms = jnp.sum(o * o, axis=-1, keepdims=True) * (1.0 / HEAD_DV)
            y = (o * lax.rsqrt(ms + NORM_EPS)) * nw_ref[:, sl]
            g = g_ref[r, sl].astype(F32)
            o_ref[r, sl] = (y * (g * jax.nn.sigmoid(g))).astype(BF16)
        return carry

    lax.fori_loop(0, n_chunks, chunk, 0, unroll=4)


def _ret_kernel(q_ref, k_ref, v_ref, g_ref, cos_ref, sina_ref, sinb_ref, dq_ref, dki_ref,
                dke_ref, dt_ref, nw_ref, o_ref, st_ref):
    @pl.when(pl.program_id(1) == 0)
    def _():
        st_ref[...] = jnp.zeros_like(st_ref)

    causal = _tri_mask(RET_CHUNK)
    n_chunks = q_ref.shape[0] // RET_CHUNK
    lane = lax.broadcasted_iota(jnp.int32, (RET_CHUNK, HEAD_PAD), 1)
    real = lane < HEAD_DV
    masks = _pair_masks(RET_CHUNK)
    half = HEAD_DK // 2

    def rotary(t, cos, sina, sinb):
        return (t * cos + pltpu.roll(t, LANES - half, 1) * sina + pltpu.roll(t, half, 1) * sinb)

    def chunk(c, carry):
        r = pl.ds(pl.multiple_of(c * RET_CHUNK, RET_CHUNK), RET_CHUNK)
        cos, sina, sinb = cos_ref[r, :], sina_ref[r, :], sinb_ref[r, :]
        pair_q, pair_ki, pair_ke = [], [], []
        for p in range(N_PAIRS):
            ps = slice(p * LANES, (p + 1) * LANES)
            qr = rotary(q_ref[r, ps].astype(F32), cos, sina, sinb) * dq_ref[:, ps]
            kr = rotary(k_ref[r, ps].astype(F32), cos, sina, sinb) * (HEAD_DK ** -0.5)
            pair_q.append(qr)
            pair_ki.append((kr * dki_ref[:, ps]).astype(BF16))
            pair_ke.append((kr * dke_ref[:, ps]).astype(BF16))
        for h in range(N_HEADS):
            sl = slice(h * HEAD_PAD, (h + 1) * HEAD_PAD)
            p = h // 2
            qd = jnp.where(masks[h % 2], pair_q[p], 0.0)
            o = _head_attention(qd, pair_ki[p], pair_ke[p], v_ref[r, sl],
                                dt_ref[:, p * LANES:(p + 1) * LANES], st_ref, h, causal)
            mu = jnp.sum(o, axis=-1, keepdims=True) * (1.0 / HEAD_DV)
            oc = jnp.where(real, o - mu, 0.0)
            var = jnp.sum(oc * oc, axis=-1, keepdims=True) * (1.0 / HEAD_DV)
            y = (oc * lax.rsqrt(var + NORM_EPS)) * nw_ref[:, sl]
            g = g_ref[r, sl].astype(F32)
            o_ref[r, sl] = (y * (g * jax.nn.sigmoid(g))).astype(BF16)
        return carry

    lax.fori_loop(0, n_chunks, chunk, 0, unroll=2)


def _proj_spec(tl, width, col_off, per_b):
    cb = col_off // width
    return pl.BlockSpec((tl, width), lambda b, l: (b * per_b + l, cb))


def _full(shape):
    return pl.BlockSpec(shape, lambda b, l: (0,) * len(shape))


def _gla_mixer(proj, batch, seq, layer, wa_p, ba_p, nw_p, tl=512):
    per_b = seq // tl
    pos = np.arange(tl)
    tri_bd = jnp.asarray((pos[:, None] // CHUNK == pos[None, :] // CHUNK)
                         & (pos[:, None] >= pos[None, :]), BF16)
    return pl.pallas_call(
        _gla_kernel,
        grid=(batch, per_b),
        in_specs=[_proj_spec(tl, QKP, OFF_GQ, per_b), _proj_spec(tl, QKP, OFF_GK, per_b),
                  _proj_spec(tl, HP, OFF_GV, per_b), _proj_spec(tl, HP, OFF_GG, per_b),
                  _proj_spec(tl, LANES, OFF_GA, per_b),
                  _layer_spec((LANES, QKP), layer), _layer_spec((1, QKP), layer),
                  _layer_spec((1, HP), layer), _full((tl, tl))],
        out_specs=pl.BlockSpec((tl, HP), lambda b, l: (b * per_b + l, 0)),
        out_shape=jax.ShapeDtypeStruct((batch * seq, HP), BF16),
        scratch_shapes=[pltpu.VMEM((N_HEADS, HEAD_PAD, HEAD_PAD), F32),
                        pltpu.VMEM((tl, HP), BF16), pltpu.VMEM((tl, QKP), BF16),
                        pltpu.VMEM((tl, QKP), BF16), pltpu.VMEM((tl // CHUNK, QKP), F32)],
        compiler_params=_cparams(("arbitrary", "arbitrary")),
        name="gla_mixer",
    )(proj, proj, proj, proj, proj, wa_p, ba_p, nw_p, tri_bd)


def _ret_mixer(proj, batch, seq, layer, cos_t, sina_t, sinb_t, dq, dki, dke, dtot, nw_p, tl=512):
    per_b = seq // tl
    return pl.pallas_call(
        _ret_kernel,
        grid=(batch, per_b),
        in_specs=[_proj_spec(tl, QKP, OFF_RQ, per_b), _proj_spec(tl, QKP, OFF_RK, per_b),
                  _proj_spec(tl, HP, OFF_RV, per_b), _proj_spec(tl, HP, OFF_RG, per_b),
                  pl.BlockSpec((tl, LANES), lambda b, l: (l, 0)),
                  pl.BlockSpec((tl, LANES), lambda b, l: (l, 0)),
                  pl.BlockSpec((tl, LANES), lambda b, l: (l, 0)),
                  _full((RET_CHUNK, QKP)), _full((RET_CHUNK, QKP)), _full((RET_CHUNK, QKP)),
                  _full((1, QKP)), _layer_spec((1, HP), layer)],
        out_specs=pl.BlockSpec((tl, HP), lambda b, l: (b * per_b + l, 0)),
        out_shape=jax.ShapeDtypeStruct((batch * seq, HP), BF16),
        scratch_shapes=[pltpu.VMEM((N_HEADS, HEAD_PAD, HEAD_PAD), F32)],
        compiler_params=_cparams(("arbitrary", "arbitrary")),
        name="ret_mixer",
    )(proj, proj, proj, proj, cos_t, sina_t, sinb_t, dq, dki, dke, dtot, nw_p)


S5_N = S5_GROUPS * S5_STATE


def _gelu_tanh(x):
    return 0.5 * x * (1.0 + jnp.tanh(np.sqrt(2.0 / np.pi) * (x + 0.044715 * (x * x * x))))


def _s5_kernel(u_ref, bb_ref, cb_ref, nr_ref, ni_ref, pr_ref, pi_ref, lr_ref, li_ref,
               d_ref, wg_ref, bg_ref, o_ref, sr_ref, si_ref, x_scr, s_scr):
    @pl.when(pl.program_id(1) == 0)
    def _():
        sr_ref[...] = jnp.zeros_like(sr_ref)
        si_ref[...] = jnp.zeros_like(si_ref)

    tri = _tri_mask().astype(BF16)
    n_chunks = u_ref.shape[0] // CHUNK
    u = u_ref[...]
    x_scr[...] = jnp.dot(u, bb_ref[...], preferred_element_type=F32)

    def chunk(c, carry):
        r = pl.ds(pl.multiple_of(c * CHUNK, CHUNK), CHUNK)
        xr, xi = x_scr[r, :S5_N], x_scr[r, S5_N:]
        nr, ni = nr_ref[...], ni_ref[...]
        p_r = jnp.dot(tri, (xr * nr - xi * ni).astype(BF16), preferred_element_type=F32)
        p_i = jnp.dot(tri, (xr * ni + xi * nr).astype(BF16), preferred_element_type=F32)
        s0r, s0i = sr_ref[...], si_ref[...]
        lr, li = lr_ref[...], li_ref[...]
        q_r = p_r + (s0r * lr - s0i * li)
        q_i = p_i + (s0r * li + s0i * lr)
        pr, pi = pr_ref[...], pi_ref[...]
        s_r = q_r * pr - q_i * pi
        s_i = q_r * pi + q_i * pr
        sr_ref[...] = s_r[CHUNK - 1:CHUNK, :]
        si_ref[...] = s_i[CHUNK - 1:CHUNK, :]
        s_scr[r, :S5_N] = s_r.astype(BF16)
        s_scr[r, S5_N:] = s_i.astype(BF16)
        return carry

    lax.fori_loop(0, n_chunks, chunk, 0, unroll=2)
    y = jnp.dot(s_scr[...], cb_ref[...], preferred_element_type=F32)
    y = _gelu_tanh(y + d_ref[...] * u.astype(F32))
    gate = jnp.dot(y.astype(BF16), wg_ref[...], preferred_element_type=F32) + bg_ref[...]
    o_ref[...] = (y * jax.nn.sigmoid(gate)).astype(BF16)


def _s5_mixer(proj, batch, seq, layer, tabs, tl=512):
    per_b = seq // tl
    bb, cb, nr, ni, pr, pi, lr, li, dsk, wg, bg = tabs
    return pl.pallas_call(
        _s5_kernel,
        grid=(batch, per_b),
        in_specs=[_proj_spec(tl, S5_WIDTH, OFF_SU, per_b),
                  *[_layer_spec(a.shape[1:], layer) for a in tabs]],
        out_specs=pl.BlockSpec((tl, S5_WIDTH), lambda b, l: (b * per_b + l, 0)),
        out_shape=jax.ShapeDtypeStruct((batch * seq, S5_WIDTH), BF16),
        scratch_shapes=[pltpu.VMEM((1, S5_N), F32), pltpu.VMEM((1, S5_N), F32),
                        pltpu.VMEM((tl, 2 * S5_N), F32), pltpu.VMEM((tl, 2 * S5_N), BF16)],
        compiler_params=_cparams(("arbitrary", "arbitrary")),
        name="s5_mixer",
    )(proj, bb, cb, nr, ni, pr, pi, lr, li, dsk, wg, bg)


def _s5_tables(a_re, a_im, log_dt, b_re, b_im, c_re, c_im, d_skip, w_glu, b_glu):
    lam = lax.complex(a_re, a_im)
    dt = jnp.exp(log_dt)[:, None]
    lam_bar = jnp.exp(lam * dt)
    b_bar = ((lam_bar - 1.0) / lam)[..., None] * lax.complex(b_re, b_im)
    eye = jnp.eye(S5_GROUPS, dtype=F32)
    def blk_b(m):
        return jnp.einsum('gph,gk->ghkp', m, eye).reshape(S5_WIDTH, S5_N)
    bb = jnp.concatenate([blk_b(jnp.real(b_bar)), blk_b(jnp.imag(b_bar))], axis=1)
    def blk_c(m):
        return jnp.einsum('ghp,gk->kpgh', m, eye).reshape(S5_N, S5_WIDTH)
    cb = jnp.concatenate([blk_c(c_re), blk_c(-c_im)], axis=0)
    steps = jnp.arange(CHUNK, dtype=F32)[:, None, None]
    lam_dt = (lam * dt)[None]
    pos = jnp.exp(lam_dt * steps).reshape(CHUNK, S5_N)
    neg = jnp.exp(-lam_dt * steps).reshape(CHUNK, S5_N)
    one = lam_bar.reshape(1, S5_N)
    return (bb.astype(BF16), cb.astype(BF16), jnp.real(neg), jnp.imag(neg), jnp.real(pos),
            jnp.imag(pos), jnp.real(one), jnp.imag(one), d_skip.reshape(1, S5_WIDTH),
            w_glu.astype(BF16), b_glu.reshape(1, S5_WIDTH))


def _pack_bf16_pairs(x):
    w = x.shape[1] // 2
    xb = x.astype(BF16).astype(F32)
    hi = lax.bitcast_convert_type(xb[:, :w], jnp.uint32)
    lo = lax.bitcast_convert_type(xb[:, w:], jnp.uint32)
    return hi | (lo >> 16)


def _unpack_bf16_pairs(p):
    hi = lax.bitcast_convert_type(p & jnp.uint32(0xFFFF0000), F32)
    lo = lax.bitcast_convert_type(p << 16, F32)
    return hi, lo


def _router_kernel(og_ref, or_ref, os_ref, wg_ref, wr_ref, ws_ref, x_ref, g1_ref, sc_ref,
                   sh_ref, nw_ref, rw_ref, rb_ref,
                   x1_ref, h_ref, idx_ref, gate_ref, rank_ref, cnt_ref, carry_ref):
    i = pl.program_id(0)

    @pl.when(i == 0)
    def _():
        carry_ref[...] = jnp.zeros_like(carry_ref)

    mix = (jnp.dot(og_ref[...], wg_ref[...], preferred_element_type=F32)
           + jnp.dot(or_ref[...], wr_ref[...], preferred_element_type=F32)
           + jnp.dot(os_ref[...], ws_ref[...], preferred_element_type=F32))
    x1 = x_ref[...] + g1_ref[0] * mix
    x1_ref[...] = x1
    hdn = _rms_mod(x1, nw_ref[...], sc_ref[0], sh_ref[0])
    h_ref[...] = _pack_bf16_pairs(hdn)
    h_hi = hdn.astype(BF16)
    h_lo = (hdn - h_hi.astype(F32)).astype(BF16)
    p = lax.dot_general(rw_ref[...], h_hi, _NT, preferred_element_type=F32)
    q = lax.dot_general(rw_ref[:N_EXPERTS, :], h_lo, _NT, preferred_element_type=F32)
    logits = (p[:N_EXPERTS] + p[N_EXPERTS:] + q) + rb_ref[:, 0:1]
    tm = logits.shape[1]
    eidx = lax.broadcasted_iota(jnp.int32, (N_EXPERTS, tm), 0)
    work = logits
    onehot = jnp.zeros((N_EXPERTS, tm), F32)
    vals, idxs, sels = [], [], []
    for _ in range(TOP_K):
        m = jnp.max(work, axis=0, keepdims=True)
        ix = jnp.min(jnp.where(work == m, eidx, N_EXPERTS), axis=0, keepdims=True)
        sel = eidx == ix
        work = jnp.where(sel, -jnp.inf, work)
        onehot = onehot + sel.astype(F32)
        vals.append(m)
        idxs.append(ix)
        sels.append(sel)
    exps = [jnp.exp(v - vals[0]) for v in vals]
    denom = exps[0] + exps[1] + exps[2] + exps[3]
    r = lax.broadcasted_iota(jnp.int32, (tm, tm), 0)
    c = lax.broadcasted_iota(jnp.int32, (tm, tm), 1)
    earlier = (r < c).astype(BF16)
    before = (jnp.dot(onehot.astype(BF16), earlier, preferred_element_type=F32)
              + carry_ref[:, 0:1])
    row8 = lax.broadcasted_iota(jnp.int32, (8, tm), 0)
    idx_out = jnp.zeros((8, tm), jnp.int32)
    gate_out = jnp.zeros((8, tm), F32)
    rank_out = jnp.zeros((8, tm), F32)
    for k in range(TOP_K):
        rk = jnp.sum(jnp.where(sels[k], before, 0.0), axis=0, keepdims=True)
        idx_out = jnp.where(row8 == k, idxs[k], idx_out)
        gate_out = jnp.where(row8 == k, exps[k] / denom, gate_out)
        rank_out = jnp.where(row8 == k, rk, rank_out)
    idx_ref[...] = idx_out
    gate_ref[...] = gate_out
    rank_ref[...] = rank_out.astype(jnp.int32)
    total = carry_ref[...] + jnp.sum(onehot, axis=1, keepdims=True)
    carry_ref[...] = total
    cnt_ref[...] = total


def _outproj_router(o_gla, o_ret, o_s5, wg, wr, ws, x3, grp, mod5, b0, layer, nw2, rw_p, rb_p,
                    seq, tm=512):
    _, t, d = x3.shape
    per_b = seq // tm
    row = lambda w: pl.BlockSpec((tm, w), lambda i: (i, 0))
    full = lambda s: pl.BlockSpec(s, lambda i: (0,) * len(s))
    slot_t = pl.BlockSpec((8, tm), lambda i: (0, i))
    return pl.pallas_call(
        _router_kernel,
        grid=(t // tm,),
        in_specs=[row(HP), row(HP), row(S5_WIDTH), _layer_spec((HP, d), layer),
                  _layer_spec((HP, d), layer), _layer_spec((S5_WIDTH, d), layer),
                  pl.BlockSpec((None, tm, d), lambda i: (grp, i, 0)),
                  _mod_spec(layer, MOD_G1, per_b, b0), _mod_spec(layer, MOD_SC2, per_b, b0),
                  _mod_spec(layer, MOD_SH2, per_b, b0), _layer_spec((1, d), layer),
                  _layer_spec((2 * N_EXPERTS, d), layer), _layer_spec((N_EXPERTS, LANES), layer)],
        out_specs=[row(d), row(d // 2), slot_t, slot_t, slot_t, full((N_EXPERTS, LANES))],
        out_shape=[jax.ShapeDtypeStruct((t, d), F32),
                   jax.ShapeDtypeStruct((t, d // 2), jnp.uint32),
                   jax.ShapeDtypeStruct((8, t), jnp.int32),
                   jax.ShapeDtypeStruct((8, t), F32),
                   jax.ShapeDtypeStruct((8, t), jnp.int32),
                   jax.ShapeDtypeStruct((N_EXPERTS, LANES), F32)],
        scratch_shapes=[pltpu.VMEM((N_EXPERTS, LANES), F32)],
        compiler_params=_cparams(("arbitrary",)),
        name="outproj_router",
    )(o_gla, o_ret, o_s5, wg, wr, ws, x3, mod5, mod5, mod5, nw2, rw_p, rb_p)


GATHER_WIN = 64


def _gather_rows(table, idx):
    m = idx.shape[0]
    w = table.shape[1]
    mesh = plsc.VectorSubcoreMesh(core_axis_name="core", subcore_axis_name="subcore")

    @functools.partial(pl.kernel, out_type=jax.ShapeDtypeStruct((m, w), table.dtype),
                       mesh=mesh, name="sc_row_gather")
    def gather(x_hbm, i_hbm, o_hbm):
        def body(i_vmem, o_vmem):
            pltpu.sync_copy(x_hbm.at[i_vmem], o_vmem)

        pltpu.emit_pipeline(
            body,
            grid=(m // GATHER_WIN,),
            in_specs=[pl.BlockSpec((GATHER_WIN,), lambda i: (i,))],
            out_specs=[pl.BlockSpec((GATHER_WIN, w), lambda i: (i, 0))],
            core_axis_name=("core", "subcore"),
            dimension_semantics=(pltpu.PARALLEL,),
        )(i_hbm, o_hbm)

    return gather(table, idx)


def _scatter_rows(x, dest_slot_major, n_rows):
    t, w = x.shape
    steps = t // GATHER_WIN
    mesh = plsc.VectorSubcoreMesh(core_axis_name="core", subcore_axis_name="subcore")

    @functools.partial(pl.kernel, out_type=jax.ShapeDtypeStruct((n_rows, w), x.dtype),
                       mesh=mesh, name="sc_row_scatter")
    def scatter(x_hbm, i_hbm, o_hbm):
        def body(x_vmem, i0, i1, i2, i3):
            for i_vmem in (i0, i1, i2, i3):
                pltpu.sync_copy(x_vmem, o_hbm.at[i_vmem])

        slot = lambda k: pl.BlockSpec((GATHER_WIN,), lambda i: (k * steps + i,))
        pltpu.emit_pipeline(
            body,
            grid=(steps,),
            in_specs=[pl.BlockSpec((GATHER_WIN, w), lambda i: (i, 0)),
                      slot(0), slot(1), slot(2), slot(3)],
            out_specs=[],
            core_axis_name=("core", "subcore"),
            dimension_semantics=(pltpu.PARALLEL,),
        )(x_hbm, i_hbm, i_hbm, i_hbm, i_hbm)

    return scatter(x, dest_slot_major)


def _expert_kernel(be_ref, nv_ref, rows_ref, wu_ref, bu_ref, wd_ref, bd_ref, o_ref,
                   wu_bf, wd_bf):
    i = pl.program_id(0)
    e = be_ref[i]
    prev = be_ref[jnp.maximum(i - 1, 0)]

    @pl.when((i == 0) | (e != prev))
    def _():
        wu_bf[...] = wu_ref[0, 0].astype(BF16)
        wd_bf[...] = wd_ref[0, 0].astype(BF16)

    def run_rows(r0, n, left):
        rs = slice(r0, r0 + n)
        row = lax.broadcasted_iota(jnp.int32, (n, rows_ref.shape[1]), 0)
        x_hi, x_lo = _unpack_bf16_pairs(jnp.where(row < left, rows_ref[rs, :], jnp.uint32(0)))
        x = jnp.concatenate([x_hi.astype(BF16), x_lo.astype(BF16)], axis=1)
        up = jnp.dot(x, wu_bf[...], preferred_element_type=F32) + bu_ref[0, 0]
        x_glu = jnp.minimum(up[:, :D_FF], SWIGLU_LIMIT)
        x_lin = jnp.clip(up[:, D_FF:], -SWIGLU_LIMIT, SWIGLU_LIMIT)
        act = x_glu * jax.nn.sigmoid(SWIGLU_ALPHA * x_glu) * (x_lin + 1.0)
        o_ref[rs, :] = _pack_bf16_pairs(
            jnp.dot(act.astype(BF16), wd_bf[...], preferred_element_type=F32) + bd_ref[0, 0])

    def zero_rows(r0, n):
        o_ref[r0:r0 + n, :] = jnp.zeros((n, o_ref.shape[1]), o_ref.dtype)

    half = ROW_SUB // 2
    for s in range(ROW_BLK // ROW_SUB):
        r0 = s * ROW_SUB
        left = nv_ref[i] - r0

        @pl.when(left > half)
        def _():
            run_rows(r0, ROW_SUB, left)

        @pl.when((left > 0) & (left <= half))
        def _():
            run_rows(r0, half, left)
            zero_rows(r0 + half, half)

        @pl.when(left <= 0)
        def _():
            zero_rows(r0, ROW_SUB)


def _experts(layer, block_e, n_valid, rows, w_up, b_up, w_down, b_down):
    n_rows, dh = rows.shape
    d = 2 * dh
    n_blocks = n_rows // ROW_BLK
    depth, ne, _, f2 = w_up.shape
    wsel = lambda i, be, nu: (layer, be[i], 0, 0)
    grid_spec = pltpu.PrefetchScalarGridSpec(
        num_scalar_prefetch=2,
        grid=(n_blocks,),
        in_specs=[pl.BlockSpec((ROW_BLK, dh), lambda i, be, nu: (i, 0)),
                  pl.BlockSpec((1, 1, d, f2), wsel),
                  pl.BlockSpec((1, 1, 1, f2), wsel),
                  pl.BlockSpec((1, 1, D_FF, d), wsel),
                  pl.BlockSpec((1, 1, 1, d), wsel)],
        out_specs=pl.BlockSpec((ROW_BLK, dh), lambda i, be, nu: (i, 0)),
        scratch_shapes=[pltpu.VMEM((d, f2), BF16), pltpu.VMEM((D_FF, d), BF16)],
    )
    return pl.pallas_call(
        _expert_kernel,
        grid_spec=grid_spec,
        out_shape=jax.ShapeDtypeStruct((n_rows, dh), jnp.uint32),
        compiler_params=_cparams(("arbitrary",)),
        name="moe_experts",
    )(block_e, n_valid, rows, w_up, b_up.reshape(depth, ne, 1, f2), w_down,
      b_down.reshape(depth, ne, 1, d))


def _combine_kernel(y0_ref, y1_ref, y2_ref, y3_ref, gate_ref, x1_ref, g2_ref, fw_ref, *rest,
                    final):
    o_ref = rest[-1]
    gates = gate_ref[...]
    y_hi, y_lo = None, None
    for k, y_ref in enumerate((y0_ref, y1_ref, y2_ref, y3_ref)):
        hi, lo = _unpack_bf16_pairs(y_ref[...])
        g = gates[:, k:k + 1]
        y_hi = g * hi if y_hi is None else y_hi + g * hi
        y_lo = g * lo if y_lo is None else y_lo + g * lo
    y = jnp.concatenate([y_hi, y_lo], axis=1)
    x2 = x1_ref[...] + g2_ref[0] * y
    if final:
        x2 = (x2 * lax.rsqrt(jnp.mean(x2 * x2, axis=-1, keepdims=True) + NORM_EPS)) * fw_ref[...]
    o_ref[...] = x2


def _combine(y4, gates, x1, mod5, b0, layer, fw, seq, final, grp, n_out, prev_out, th=256):
    t, d = x1.shape
    steps = t // th
    per_b = seq // th
    slot = lambda k: pl.BlockSpec((th, d // 2), lambda i: (k * steps + i, 0))
    return pl.pallas_call(
        functools.partial(_combine_kernel, final=final),
        grid=(steps,),
        in_specs=[slot(0), slot(1), slot(2), slot(3),
                  pl.BlockSpec((th, TOP_K), lambda i: (i, 0)),
                  pl.BlockSpec((th, d), lambda i: (i, 0)),
                  _mod_spec(layer, MOD_G2, per_b, b0),
                  pl.BlockSpec((1, d), lambda i: (0, 0))]
                 + ([] if prev_out is None else [pl.BlockSpec(memory_space=pl.ANY)]),
        out_specs=pl.BlockSpec((None, th, d), lambda i: (grp, i, 0)),
        out_shape=jax.ShapeDtypeStruct((n_out, t, d), F32),
        input_output_aliases={} if prev_out is None else {8: 0},
        compiler_params=_cparams(("arbitrary",)),
        name="moe_combine",
    )(y4, y4, y4, y4, gates, x1, mod5, fw, *([] if prev_out is None else [prev_out]))


def _retention_tables(seq):
    f32 = np.float32
    pos = np.arange(seq, dtype=f32)
    inv_freq = (f32(ROPE_BASE) ** (-np.arange(0, HEAD_DK, 2, dtype=f32) / f32(HEAD_DK))).astype(f32)
    ang = pos[:, None] * inv_freq[None, :]
    cos, sin = np.cos(ang), np.sin(ang)
    zero = np.zeros_like(sin)
    zpad = np.zeros((seq, LANES - 2 * HEAD_DK), f32)
    cos_t = np.concatenate([cos, cos, cos, cos, zpad], axis=1)
    sina_t = np.concatenate([-sin, zero, -sin, zero, zpad], axis=1)
    sinb_t = np.concatenate([zero, sin, zero, sin, zpad], axis=1)
    log_gamma = np.log1p(-np.exp2(f32(-5.0) - np.arange(N_HEADS, dtype=f32))).astype(f32)
    log_decay = np.broadcast_to(log_gamma[None, :, None], (RET_CHUNK, N_HEADS, HEAD_DK))
    cum = np.cumsum(log_decay, axis=0, dtype=f32)
    tot = cum[-1:]

    def shp(a):
        flat = a.reshape(a.shape[0], N_HEADS * HEAD_DK)
        return np.where(_DK_SRC >= 0, flat[:, np.maximum(_DK_SRC, 0)], f32(1.0)).astype(f32)

    tabs = (cos_t, sina_t, sinb_t, shp(np.exp(cum)), shp(np.exp(-cum)), shp(np.exp(tot - cum)),
            shp(np.exp(tot)))
    return tuple(jnp.asarray(a, F32) for a in tabs)


def kernel(x, c, norm1_w, norm2_w, w_mod, b_mod, w_in, gla_w_a2, gla_b_a, gla_norm_w, ret_norm_w, s5_a_re, s5_a_im, s5_log_dt, s5_b_re, s5_b_im, s5_c_re, s5_c_im, s5_d, s5_w_glu, s5_b_glu, w_out, router_w, router_b, w_up, b_up, w_down, b_down, final_norm_w):
    batch, seq, d = x.shape
    depth = w_mod.shape[0]
    gb = batch // N_STREAMS
    t = gb * seq
    n_slots = t * TOP_K
    n_blocks = n_slots // ROW_BLK + N_EXPERTS
    n_rows = n_blocks * ROW_BLK

    mod = _modulation(c, w_mod, b_mod)
    mod5 = mod.reshape(depth, batch, 6, 1, d).transpose(0, 2, 1, 3, 4)
    ret_tabs = _retention_tables(seq)

    w_p = _take_cols(w_in, _IN_SRC).astype(BF16)
    wa_p = jnp.zeros((depth, LANES, QKP), F32).at[:, :GATE_RANK].set(
        _take_cols(gla_w_a2, _DK_SRC)).astype(BF16)
    ba_p = _take_cols(gla_b_a, _DK_SRC).reshape(depth, 1, QKP)
    gnw = _take_cols(gla_norm_w, _DV_SRC).reshape(depth, 1, HP)
    rnw = _take_cols(ret_norm_w, _DV_SRC).reshape(depth, 1, HP)
    kv = N_HEADS * HEAD_DV
    wo_g = _take_rows(w_out[:, :kv], _DV_SRC).astype(BF16)
    wo_r = _take_rows(w_out[:, kv:2 * kv], _DV_SRC).astype(BF16)
    wo_s = w_out[:, 2 * kv:].astype(BF16)
    rw_t = jnp.swapaxes(router_w, 1, 2)
    rw_hi = rw_t.astype(BF16)
    rw_p = jnp.concatenate([rw_hi, (rw_t - rw_hi.astype(F32)).astype(BF16)], axis=1)
    rb_p = jnp.broadcast_to(router_b[:, :, None], (depth, N_EXPERTS, LANES))
    s5_tabs = jax.vmap(_s5_tables)(s5_a_re, s5_a_im, s5_log_dt, s5_b_re, s5_b_im, s5_c_re,
                                   s5_c_im, s5_d, s5_w_glu, s5_b_glu)
    n1 = norm1_w.reshape(depth, 1, d)
    n2 = norm2_w.reshape(depth, 1, d)
    fw = final_norm_w.reshape(1, d)

    xs = [(x.reshape(N_STREAMS, t, d), g) for g in range(N_STREAMS)]
    out = None

    for i in range(depth):
        final = i == depth - 1
        for g in range(N_STREAMS):
            x3, grp = xs[g]
            b0 = g * gb

            proj = _in_projection(x3, grp, mod5, b0, i, n1, w_p, seq)
            o_gla = _gla_mixer(proj, gb, seq, i, wa_p, ba_p, gnw)
            o_ret = _ret_mixer(proj, gb, seq, i, *ret_tabs, rnw)
            o_s5 = _s5_mixer(proj, gb, seq, i, s5_tabs)

            x1, hdn, idx, gates, rank, counts = _outproj_router(
                o_gla, o_ret, o_s5, wo_g, wo_r, wo_s, x3, grp, mod5, b0, i, n2, rw_p, rb_p, seq)

            cnt = counts[:, 0].astype(jnp.int32)
            padded = (cnt + ROW_BLK - 1) // ROW_BLK * ROW_BLK
            pad_ends = jnp.cumsum(padded)
            pad_starts = pad_ends - padded
            slot_start = jnp.sum(jnp.where(idx[:TOP_K, :, None] == jnp.arange(N_EXPERTS),
                                           pad_starts, 0), axis=-1)
            dest_sm = (slot_start + rank[:TOP_K]).astype(jnp.int32).reshape(-1)
            gates_tm = gates[:TOP_K].T
            blk_start = jnp.arange(n_blocks, dtype=jnp.int32) * ROW_BLK
            block_e = jnp.minimum(jnp.sum(pad_ends[None, :] <= blk_start[:, None], axis=1),
                                  N_EXPERTS - 1).astype(jnp.int32)
            n_valid = jnp.clip((pad_starts + cnt)[block_e] - blk_start, 0,
                               ROW_BLK).astype(jnp.int32)

            rows = _scatter_rows(hdn, dest_sm, n_rows)
            out_rows = _experts(i, block_e, n_valid, rows, w_up, b_up, w_down, b_down)
            y4 = _gather_rows(out_rows, dest_sm)
            if final:
                out = _combine(y4, gates_tm, x1, mod5, b0, i, fw, seq, True, g, N_STREAMS, out)
            else:
                xs[g] = (_combine(y4, gates_tm, x1, mod5, b0, i, fw, seq, False, 0, 1, None), 0)

    return out.reshape(batch, seq, d)
```

```python
import functools

import numpy as np
import jax
import jax.numpy as jnp
from jax import lax
from jax.experimental import pallas as pl
from jax.experimental.pallas import tpu as pltpu
from jax.experimental.pallas import tpu_sc as plsc

D_MODEL = 1024
CHUNK = 64
RET_CHUNK = 128
NORM_EPS = 1e-5
N_HEADS = 4
HEAD_DK = 48
HEAD_DV = 96
GATE_RANK = 16
GATE_TEMP = 16.0
ROPE_BASE = 10000.0
S5_WIDTH = 256
S5_GROUP_DIM = 16
S5_GROUPS = 16
S5_STATE = 64
N_EXPERTS = 32
TOP_K = 4
D_FF = 1024
SWIGLU_LIMIT = 7.0
SWIGLU_ALPHA = 1.702

LANES = 128
HEAD_PAD = LANES
HP = N_HEADS * HEAD_PAD
N_PAIRS = N_HEADS // 2
QKP = N_PAIRS * LANES
VMEM_LIMIT = 56 * 1024 * 1024

OFF_GQ, OFF_GK, OFF_GV, OFF_GG = 0, QKP, 2 * QKP, 2 * QKP + HP
OFF_RQ = OFF_GG + HP
OFF_RK, OFF_RV, OFF_RG = OFF_RQ + QKP, OFF_RQ + 2 * QKP, OFF_RQ + 2 * QKP + HP
OFF_SU = OFF_RG + HP
OFF_GA = OFF_SU + S5_WIDTH
NP_COLS = OFF_GA + LANES
PROJ_CH = 1152

ROW_BLK = 1024
ROW_SUB = 512
MOD_SH1, MOD_SC1, MOD_G1, MOD_SH2, MOD_SC2, MOD_G2 = range(6)
N_STREAMS = 1

F32 = jnp.float32
BF16 = jnp.bfloat16


def _DK_SRC_LANE(h, d):
    return (h // 2) * LANES + (h % 2) * HEAD_DK + d


def _in_col_map():
    src = -np.ones((NP_COLS,), np.int64)
    kq = N_HEADS * HEAD_DK
    kv = N_HEADS * HEAD_DV
    base = dict(gq=0, gk=kq, gv=2 * kq, gg=2 * kq + kv, ga=2 * kq + 2 * kv)
    r0 = base['ga'] + GATE_RANK
    base.update(rq=r0, rk=r0 + kq, rv=r0 + 2 * kq, rg=r0 + 2 * kq + kv, su=r0 + 2 * kq + 2 * kv)
    for h in range(N_HEADS):
        for d in range(HEAD_DK):
            lane = _DK_SRC_LANE(h, d)
            src[OFF_GQ + lane] = base['gq'] + h * HEAD_DK + d
            src[OFF_GK + lane] = base['gk'] + h * HEAD_DK + d
            src[OFF_RQ + lane] = base['rq'] + h * HEAD_DK + d
            src[OFF_RK + lane] = base['rk'] + h * HEAD_DK + d
        for d in range(HEAD_DV):
            src[OFF_GV + h * HEAD_PAD + d] = base['gv'] + h * HEAD_DV + d
            src[OFF_GG + h * HEAD_PAD + d] = base['gg'] + h * HEAD_DV + d
            src[OFF_RV + h * HEAD_PAD + d] = base['rv'] + h * HEAD_DV + d
            src[OFF_RG + h * HEAD_PAD + d] = base['rg'] + h * HEAD_DV + d
    src[OFF_SU:OFF_SU + S5_WIDTH] = base['su'] + np.arange(S5_WIDTH)
    src[OFF_GA:OFF_GA + GATE_RANK] = base['ga'] + np.arange(GATE_RANK)
    return src


_IN_SRC = _in_col_map()


def _head_pad_map(width):
    src = -np.ones((HP,), np.int64)
    for h in range(N_HEADS):
        src[h * HEAD_PAD:h * HEAD_PAD + width] = h * width + np.arange(width)
    return src


_DV_SRC = _head_pad_map(HEAD_DV)
_DK_SRC = -np.ones((QKP,), np.int64)
for _h in range(N_HEADS):
    for _d in range(HEAD_DK):
        _DK_SRC[_DK_SRC_LANE(_h, _d)] = _h * HEAD_DK + _d


def _take_static(w, src, axis):
    axis = axis % w.ndim
    pieces, start = [], 0
    for j in range(1, len(src) + 1):
        run_ends = (j == len(src) or (src[j] < 0) != (src[start] < 0)
                    or (src[start] >= 0 and src[j] != src[j - 1] + 1))
        if run_ends:
            if src[start] < 0:
                shape = w.shape[:axis] + (j - start,) + w.shape[axis + 1:]
                pieces.append(jnp.zeros(shape, w.dtype))
            else:
                pieces.append(lax.slice_in_dim(w, int(src[start]), int(src[j - 1]) + 1, axis=axis))
            start = j
    return jnp.concatenate(pieces, axis=axis)


def _take_cols(w, src):
    return _take_static(w, src, -1)


def _take_rows(w, src):
    return _take_static(w, src, -2)


def _layer_spec(shape, layer):
    return pl.BlockSpec((None,) + tuple(shape), lambda *_: (layer,) + (0,) * len(shape))


def _mod_spec(layer, which, per_b, b0):
    return pl.BlockSpec((None, None, 1, 1, D_MODEL),
                        lambda i: (layer, which, b0 + i // per_b, 0, 0))


def _cparams(sem):
    return pltpu.CompilerParams(dimension_semantics=sem, vmem_limit_bytes=VMEM_LIMIT)


def _mod_kernel(c_ref, w_ref, b_ref, o_ref):
    c = c_ref[...]
    cond = c * jax.nn.sigmoid(c)
    o_ref[0] = jnp.dot(cond, w_ref[0], preferred_element_type=F32,
                       precision=lax.Precision.HIGHEST) + b_ref[0]


def _modulation(c, w_mod, b_mod):
    depth, d, n = w_mod.shape
    b = c.shape[0]
    nb = 1536
    return pl.pallas_call(
        _mod_kernel,
        grid=(depth, n // nb),
        in_specs=[pl.BlockSpec((b, d), lambda l, j: (0, 0)),
                  pl.BlockSpec((1, d, nb), lambda l, j: (l, 0, j)),
                  pl.BlockSpec((1, 1, nb), lambda l, j: (l, 0, j))],
        out_specs=pl.BlockSpec((1, b, nb), lambda l, j: (l, 0, j)),
        out_shape=jax.ShapeDtypeStruct((depth, b, n), F32),
        compiler_params=_cparams(("arbitrary", "arbitrary")),
        name="adaln_mod",
    )(c, w_mod, b_mod.reshape(depth, 1, n))


def _rms_mod(x, nw, sc, sh):
    y = x * lax.rsqrt(jnp.mean(x * x, axis=-1, keepdims=True) + NORM_EPS)
    return (y * nw) * (1.0 + sc) + sh


def _inproj_kernel(x_ref, sc_ref, sh_ref, nw_ref, w_ref, o_ref):
    h = _rms_mod(x_ref[...], nw_ref[...], sc_ref[0], sh_ref[0]).astype(BF16)
    for j in range(NP_COLS // PROJ_CH):
        cs = slice(j * PROJ_CH, (j + 1) * PROJ_CH)
        o_ref[:, cs] = jnp.dot(h, w_ref[:, cs], preferred_element_type=F32).astype(BF16)


def _in_projection(x3, grp, mod5, b0, layer, nw, w_p, seq, tm=512):
    _, t, d = x3.shape
    per_b = seq // tm
    return pl.pallas_call(
        _inproj_kernel,
        grid=(t // tm,),
        in_specs=[pl.BlockSpec((None, tm, d), lambda i: (grp, i, 0)),
                  _mod_spec(layer, MOD_SC1, per_b, b0), _mod_spec(layer, MOD_SH1, per_b, b0),
                  _layer_spec((1, d), layer), _layer_spec((d, NP_COLS), layer)],
        out_specs=pl.BlockSpec((tm, NP_COLS), lambda i: (i, 0)),
        out_shape=jax.ShapeDtypeStruct((t, NP_COLS), BF16),
        compiler_params=_cparams(("arbitrary",)),
        name="in_proj",
    )(x3, mod5, mod5, nw, w_p)


_NT = (((1,), (1,)), ((), ()))
_TN = (((0,), (0,)), ((), ()))


def _tri_mask(n=CHUNK):
    r = lax.broadcasted_iota(jnp.int32, (n, n), 0)
    c = lax.broadcasted_iota(jnp.int32, (n, n), 1)
    return r >= c


def _pair_masks(rows):
    lane = lax.broadcasted_iota(jnp.int32, (rows, LANES), 1)
    return lane < HEAD_DK, (lane >= HEAD_DK) & (lane < 2 * HEAD_DK)


def _head_attention(qd, ki, ke, vh, et, st_ref, h, causal):
    qb = qd.astype(BF16)
    sc = lax.dot_general(qb, ki.astype(BF16), _NT, preferred_element_type=F32)
    sc = jnp.where(causal, sc, 0.0)
    st = st_ref[h]
    o = jnp.dot(sc.astype(BF16), vh, preferred_element_type=F32)
    o = o + lax.dot_general(qb, st.astype(BF16), _NT, preferred_element_type=F32)
    st_ref[h] = st * et + lax.dot_general(vh, ke.astype(BF16), _TN, preferred_element_type=F32)
    return o


def _gla_kernel(q_ref, k_ref, v_ref, g_ref, a_ref, wa_ref, ba_ref, nw_ref, tri_ref, o_ref,
                st_ref, qd_s, ki_s, ke_s, et_s):
    @pl.when(pl.program_id(1) == 0)
    def _():
        st_ref[...] = jnp.zeros_like(st_ref)

    causal = _tri_mask()
    tl = q_ref.shape[0]
    n_chunks = tl // CHUNK

    z = jnp.dot(a_ref[...], wa_ref[...], preferred_element_type=F32) + ba_ref[...]
    la = (jnp.minimum(z, 0.0) - jnp.log1p(jnp.exp(-jnp.abs(z)))) * (1.0 / GATE_TEMP)
    hi = la.astype(BF16)
    lo = (la - hi.astype(F32)).astype(BF16)
    cum = (jnp.dot(tri_ref[...], hi, preferred_element_type=F32)
           + jnp.dot(tri_ref[...], lo, preferred_element_type=F32))
    cum3 = cum.reshape(n_chunks, CHUNK, QKP)
    tot3 = cum3[:, CHUNK - 1:CHUNK, :]
    qd = (q_ref[...].astype(F32) * (HEAD_DK ** -0.5)) * jnp.exp(cum)
    masks = _pair_masks(tl)
    for h in range(N_HEADS):
        pair = slice((h // 2) * LANES, (h // 2 + 1) * LANES)
        qd_s[:, h * HEAD_PAD:(h + 1) * HEAD_PAD] = jnp.where(masks[h % 2], qd[:, pair],
                                                              0.0).astype(BF16)
    kf = k_ref[...].astype(F32)
    ki_s[...] = (kf * jnp.exp(-cum)).astype(BF16)
    ke_s[...] = (kf * jnp.exp(tot3 - cum3).reshape(tl, QKP)).astype(BF16)
    et_s[...] = jnp.exp(tot3).reshape(n_chunks, QKP)

    def chunk(c, carry):
        r = pl.ds(pl.multiple_of(c * CHUNK, CHUNK), CHUNK)
        et = et_s[pl.ds(c, 1), :]
        for h in range(N_HEADS):
            sl = slice(h * HEAD_PAD, (h + 1) * HEAD_PAD)
            pair = slice((h // 2) * LANES, (h // 2 + 1) * LANES)
            o = _head_attention(qd_s[r, sl], ki_s[r, pair], ke_s[r, pair], v_ref[r, sl],
                                et[:, pair], st_ref, h, causal)
            ms = jnp.sum(o * o, axis=-1, keepdims=True) * (1.0 / HEAD_DV)
            y = (o * lax.rsqrt(ms + NORM_EPS)) * nw_ref[:, sl]
            g = g_ref[r, sl].astype(F32)
            o_ref[r, sl] = (y * (g * jax.nn.sigmoid(g))).astype(BF16)
        return carry

    lax.fori_loop(0, n_chunks, chunk, 0, unroll=4)


def _ret_kernel(q_ref, k_ref, v_ref, g_ref, cos_ref, sina_ref, sinb_ref, dq_ref, dki_ref,
                dke_ref, dt_ref, nw_ref, o_ref, st_ref):
    @pl.when(pl.program_id(1) == 0)
    def _():
        st_ref[...] = jnp.zeros_like(st_ref)

    causal = _tri_mask(RET_CHUNK)
    n_chunks = q_ref.shape[0] // RET_CHUNK
    lane = lax.broadcasted_iota(jnp.int32, (RET_CHUNK, HEAD_PAD), 1)
    real = lane < HEAD_DV
    masks = _pair_masks(RET_CHUNK)
    half = HEAD_DK // 2

    def rotary(t, cos, sina, sinb):
        return (t * cos + pltpu.roll(t, LANES - half, 1) * sina + pltpu.roll(t, half, 1) * sinb)

    def chunk(c, carry):
        r = pl.ds(pl.multiple_of(c * RET_CHUNK, RET_CHUNK), RET_CHUNK)
        cos, sina, sinb = cos_ref[r, :], sina_ref[r, :], sinb_ref[r, :]
        pair_q, pair_ki, pair_ke = [], [], []
        for p in range(N_PAIRS):
            ps = slice(p * LANES, (p + 1) * LANES)
            qr = rotary(q_ref[r, ps].astype(F32), cos, sina, sinb) * dq_ref[:, ps]
            kr = rotary(k_ref[r, ps].astype(F32), cos, sina, sinb) * (HEAD_DK ** -0.5)
            pair_q.append(qr)
            pair_ki.append((kr * dki_ref[:, ps]).astype(BF16))
            pair_ke.append((kr * dke_ref[:, ps]).astype(BF16))
        for h in range(N_HEADS):
            sl = slice(h * HEAD_PAD, (h + 1) * HEAD_PAD)
            p = h // 2
            qd = jnp.where(masks[h % 2], pair_q[p], 0.0)
            o = _head_attention(qd, pair_ki[p], pair_ke[p], v_ref[r, sl],
                                dt_ref[:, p * LANES:(p + 1) * LANES], st_ref, h, causal)
            mu = jnp.sum(o, axis=-1, keepdims=True) * (1.0 / HEAD_DV)
            oc = jnp.where(real, o - mu, 0.0)
            var = jnp.sum(oc * oc, axis=-1, keepdims=True) * (1.0 / HEAD_DV)
            y = (oc * lax.rsqrt(var + NORM_EPS)) * nw_ref[:, sl]
            g = g_ref[r, sl].astype(F32)
            o_ref[r, sl] = (y * (g * jax.nn.sigmoid(g))).astype(BF16)
        return carry

    lax.fori_loop(0, n_chunks, chunk, 0, unroll=2)


def _proj_spec(tl, width, col_off, per_b):
    cb = col_off // width
    return pl.BlockSpec((tl, width), lambda b, l: (b * per_b + l, cb))


def _full(shape):
    return pl.BlockSpec(shape, lambda b, l: (0,) * len(shape))


def _gla_mixer(proj, batch, seq, layer, wa_p, ba_p, nw_p, tl=512):
    per_b = seq // tl
    pos = np.arange(tl)
    tri_bd = jnp.asarray((pos[:, None] // CHUNK == pos[None, :] // CHUNK)
                         & (pos[:, None] >= pos[None, :]), BF16)
    return pl.pallas_call(
        _gla_kernel,
        grid=(batch, per_b),
        in_specs=[_proj_spec(tl, QKP, OFF_GQ, per_b), _proj_spec(tl, QKP, OFF_GK, per_b),
                  _proj_spec(tl, HP, OFF_GV, per_b), _proj_spec(tl, HP, OFF_GG, per_b),
                  _proj_spec(tl, LANES, OFF_GA, per_b),
                  _layer_spec((LANES, QKP), layer), _layer_spec((1, QKP), layer),
                  _layer_spec((1, HP), layer), _full((tl, tl))],
        out_specs=pl.BlockSpec((tl, HP), lambda b, l: (b * per_b + l, 0)),
        out_shape=jax.ShapeDtypeStruct((batch * seq, HP), BF16),
        scratch_shapes=[pltpu.VMEM((N_HEADS, HEAD_PAD, HEAD_PAD), F32),
                        pltpu.VMEM((tl, HP), BF16), pltpu.VMEM((tl, QKP), BF16),
                        pltpu.VMEM((tl, QKP), BF16), pltpu.VMEM((tl // CHUNK, QKP), F32)],
        compiler_params=_cparams(("arbitrary", "arbitrary")),
        name="gla_mixer",
    )(proj, proj, proj, proj, proj, wa_p, ba_p, nw_p, tri_bd)


def _ret_mixer(proj, batch, seq, layer, cos_t, sina_t, sinb_t, dq, dki, dke, dtot, nw_p, tl=512):
    per_b = seq // tl
    return pl.pallas_call(
        _ret_kernel,
        grid=(batch, per_b),
        in_specs=[_proj_spec(tl, QKP, OFF_RQ, per_b), _proj_spec(tl, QKP, OFF_RK, per_b),
                  _proj_spec(tl, HP, OFF_RV, per_b), _proj_spec(tl, HP, OFF_RG, per_b),
                  pl.BlockSpec((tl, LANES), lambda b, l: (l, 0)),
                  pl.BlockSpec((tl, LANES), lambda b, l: (l, 0)),
                  pl.BlockSpec((tl, LANES), lambda b, l: (l, 0)),
                  _full((RET_CHUNK, QKP)), _full((RET_CHUNK, QKP)), _full((RET_CHUNK, QKP)),
                  _full((1, QKP)), _layer_spec((1, HP), layer)],
        out_specs=pl.BlockSpec((tl, HP), lambda b, l: (b * per_b + l, 0)),
        out_shape=jax.ShapeDtypeStruct((batch * seq, HP), BF16),
        scratch_shapes=[pltpu.VMEM((N_HEADS, HEAD_PAD, HEAD_PAD), F32)],
        compiler_params=_cparams(("arbitrary", "arbitrary")),
        name="ret_mixer",
    )(proj, proj, proj, proj, cos_t, sina_t, sinb_t, dq, dki, dke, dtot, nw_p)


S5_N = S5_GROUPS * S5_STATE


def _gelu_tanh(x):
    return 0.5 * x * (1.0 + jnp.tanh(np.sqrt(2.0 / np.pi) * (x + 0.044715 * (x * x * x))))


def _s5_kernel(u_ref, bb_ref, cb_ref, nr_ref, ni_ref, pr_ref, pi_ref, lr_ref, li_ref,
               d_ref, wg_ref, bg_ref, o_ref, sr_ref, si_ref, x_scr, s_scr):
    @pl.when(pl.program_id(1) == 0)
    def _():
        sr_ref[...] = jnp.zeros_like(sr_ref)
        si_ref[...] = jnp.zeros_like(si_ref)

    tri = _tri_mask().astype(BF16)
    n_chunks = u_ref.shape[0] // CHUNK
    u = u_ref[...]
    x_scr[...] = jnp.dot(u, bb_ref[...], preferred_element_type=F32)

    def chunk(c, carry):
        r = pl.ds(pl.multiple_of(c * CHUNK, CHUNK), CHUNK)
        xr, xi = x_scr[r, :S5_N], x_scr[r, S5_N:]
        nr, ni = nr_ref[...], ni_ref[...]
        p_r = jnp.dot(tri, (xr * nr - xi * ni).astype(BF16), preferred_element_type=F32)
        p_i = jnp.dot(tri, (xr * ni + xi * nr).astype(BF16), preferred_element_type=F32)
        s0r, s0i = sr_ref[...], si_ref[...]
        lr, li = lr_ref[...], li_ref[...]
        q_r = p_r + (s0r * lr - s0i * li)
        q_i = p_i + (s0r * li + s0i * lr)
        pr, pi = pr_ref[...], pi_ref[...]
        s_r = q_r * pr - q_i * pi
        s_i = q_r * pi + q_i * pr
        sr_ref[...] = s_r[CHUNK - 1:CHUNK, :]
        si_ref[...] = s_i[CHUNK - 1:CHUNK, :]
        s_scr[r, :S5_N] = s_r.astype(BF16)
        s_scr[r, S5_N:] = s_i.astype(BF16)
        return carry

    lax.fori_loop(0, n_chunks, chunk, 0, unroll=2)
    y = jnp.dot(s_scr[...], cb_ref[...], preferred_element_type=F32)
    y = _gelu_tanh(y + d_ref[...] * u.astype(F32))
    gate = jnp.dot(y.astype(BF16), wg_ref[...], preferred_element_type=F32) + bg_ref[...]
    o_ref[...] = (y * jax.nn.sigmoid(gate)).astype(BF16)


def _s5_mixer(proj, batch, seq, layer, tabs, tl=512):
    per_b = seq // tl
    bb, cb, nr, ni, pr, pi, lr, li, dsk, wg, bg = tabs
    return pl.pallas_call(
        _s5_kernel,
        grid=(batch, per_b),
        in_specs=[_proj_spec(tl, S5_WIDTH, OFF_SU, per_b),
                  *[_layer_spec(a.shape[1:], layer) for a in tabs]],
        out_specs=pl.BlockSpec((tl, S5_WIDTH), lambda b, l: (b * per_b + l, 0)),
        out_shape=jax.ShapeDtypeStruct((batch * seq, S5_WIDTH), BF16),
        scratch_shapes=[pltpu.VMEM((1, S5_N), F32), pltpu.VMEM((1, S5_N), F32),
                        pltpu.VMEM((tl, 2 * S5_N), F32), pltpu.VMEM((tl, 2 * S5_N), BF16)],
        compiler_params=_cparams(("arbitrary", "arbitrary")),
        name="s5_mixer",
    )(proj, bb, cb, nr, ni, pr, pi, lr, li, dsk, wg, bg)


def _s5_tables(a_re, a_im, log_dt, b_re, b_im, c_re, c_im, d_skip, w_glu, b_glu):
    lam = lax.complex(a_re, a_im)
    dt = jnp.exp(log_dt)[:, None]
    lam_bar = jnp.exp(lam * dt)
    b_bar = ((lam_bar - 1.0) / lam)[..., None] * lax.complex(b_re, b_im)
    eye = jnp.eye(S5_GROUPS, dtype=F32)
    def blk_b(m):
        return jnp.einsum('gph,gk->ghkp', m, eye).reshape(S5_WIDTH, S5_N)
    bb = jnp.concatenate([blk_b(jnp.real(b_bar)), blk_b(jnp.imag(b_bar))], axis=1)
    def blk_c(m):
        return jnp.einsum('ghp,gk->kpgh', m, eye).reshape(S5_N, S5_WIDTH)
    cb = jnp.concatenate([blk_c(c_re), blk_c(-c_im)], axis=0)
    steps = jnp.arange(CHUNK, dtype=F32)[:, None, None]
    lam_dt = (lam * dt)[None]
    pos = jnp.exp(lam_dt * steps).reshape(CHUNK, S5_N)
    neg = jnp.exp(-lam_dt * steps).reshape(CHUNK, S5_N)
    one = lam_bar.reshape(1, S5_N)
    return (bb.astype(BF16), cb.astype(BF16), jnp.real(neg), jnp.imag(neg), jnp.real(pos),
            jnp.imag(pos), jnp.real(one), jnp.imag(one), d_skip.reshape(1, S5_WIDTH),
            w_glu.astype(BF16), b_glu.reshape(1, S5_WIDTH))


def _pack_bf16_pairs(x):
    w = x.shape[1] // 2
    xb = x.astype(BF16).astype(F32)
    hi = lax.bitcast_convert_type(xb[:, :w], jnp.uint32)
    lo = lax.bitcast_convert_type(xb[:, w:], jnp.uint32)
    return hi | (lo >> 16)


def _unpack_bf16_pairs(p):
    hi = lax.bitcast_convert_type(p & jnp.uint32(0xFFFF0000), F32)
    lo = lax.bitcast_convert_type(p << 16, F32)
    return hi, lo


def _router_kernel(og_ref, or_ref, os_ref, wg_ref, wr_ref, ws_ref, x_ref, g1_ref, sc_ref,
                   sh_ref, nw_ref, rw_ref, rb_ref,
                   x1_ref, h_ref, idx_ref, gate_ref, rank_ref, cnt_ref, carry_ref):
    i = pl.program_id(0)

    @pl.when(i == 0)
    def _():
        carry_ref[...] = jnp.zeros_like(carry_ref)

    mix = (jnp.dot(og_ref[...], wg_ref[...], preferred_element_type=F32)
           + jnp.dot(or_ref[...], wr_ref[...], preferred_element_type=F32)
           + jnp.dot(os_ref[...], ws_ref[...], preferred_element_type=F32))
    x1 = x_ref[...] + g1_ref[0] * mix
    x1_ref[...] = x1
    hdn = _rms_mod(x1, nw_ref[...], sc_ref[0], sh_ref[0])
    h_ref[...] = _pack_bf16_pairs(hdn)
    h_hi = hdn.astype(BF16)
    h_lo = (hdn - h_hi.astype(F32)).astype(BF16)
    p = lax.dot_general(rw_ref[...], h_hi, _NT, preferred_element_type=F32)
    q = lax.dot_general(rw_ref[:N_EXPERTS, :], h_lo, _NT, preferred_element_type=F32)
    logits = (p[:N_EXPERTS] + p[N_EXPERTS:] + q) + rb_ref[:, 0:1]
    tm = logits.shape[1]
    eidx = lax.broadcasted_iota(jnp.int32, (N_EXPERTS, tm), 0)
    work = logits
    onehot = jnp.zeros((N_EXPERTS, tm), F32)
    vals, idxs, sels = [], [], []
    for _ in range(TOP_K):
        m = jnp.max(work, axis=0, keepdims=True)
        ix = jnp.min(jnp.where(work == m, eidx, N_EXPERTS), axis=0, keepdims=True)
        sel = eidx == ix
        work = jnp.where(sel, -jnp.inf, work)
        onehot = onehot + sel.astype(F32)
        vals.append(m)
        idxs.append(ix)
        sels.append(sel)
    exps = [jnp.exp(v - vals[0]) for v in vals]
    denom = exps[0] + exps[1] + exps[2] + exps[3]
    r = lax.broadcasted_iota(jnp.int32, (tm, tm), 0)
    c = lax.broadcasted_iota(jnp.int32, (tm, tm), 1)
    earlier = (r < c).astype(BF16)
    before = (jnp.dot(onehot.astype(BF16), earlier, preferred_element_type=F32)
              + carry_ref[:, 0:1])
    row8 = lax.broadcasted_iota(jnp.int32, (8, tm), 0)
    idx_out = jnp.zeros((8, tm), jnp.int32)
    gate_out = jnp.zeros((8, tm), F32)
    rank_out = jnp.zeros((8, tm), F32)
    for k in range(TOP_K):
        rk = jnp.sum(jnp.where(sels[k], before, 0.0), axis=0, keepdims=True)
        idx_out = jnp.where(row8 == k, idxs[k], idx_out)
        gate_out = jnp.where(row8 == k, exps[k] / denom, gate_out)
        rank_out = jnp.where(row8 == k, rk, rank_out)
    idx_ref[...] = idx_out
    gate_ref[...] = gate_out
    rank_ref[...] = rank_out.astype(jnp.int32)
    total = carry_ref[...] + jnp.sum(onehot, axis=1, keepdims=True)
    carry_ref[...] = total
    cnt_ref[...] = total


def _outproj_router(o_gla, o_ret, o_s5, wg, wr, ws, x3, grp, mod5, b0, layer, nw2, rw_p, rb_p,
                    seq, tm=512):
    _, t, d = x3.shape
    per_b = seq // tm
    row = lambda w: pl.BlockSpec((tm, w), lambda i: (i, 0))
    full = lambda s: pl.BlockSpec(s, lambda i: (0,) * len(s))
    slot_t = pl.BlockSpec((8, tm), lambda i: (0, i))
    return pl.pallas_call(
        _router_kernel,
        grid=(t // tm,),
        in_specs=[row(HP), row(HP), row(S5_WIDTH), _layer_spec((HP, d), layer),
                  _layer_spec((HP, d), layer), _layer_spec((S5_WIDTH, d), layer),
                  pl.BlockSpec((None, tm, d), lambda i: (grp, i, 0)),
                  _mod_spec(layer, MOD_G1, per_b, b0), _mod_spec(layer, MOD_SC2, per_b, b0),
                  _mod_spec(layer, MOD_SH2, per_b, b0), _layer_spec((1, d), layer),
                  _layer_spec((2 * N_EXPERTS, d), layer), _layer_spec((N_EXPERTS, LANES), layer)],
        out_specs=[row(d), row(d // 2), slot_t, slot_t, slot_t, full((N_EXPERTS, LANES))],
        out_shape=[jax.ShapeDtypeStruct((t, d), F32),
                   jax.ShapeDtypeStruct((t, d // 2), jnp.uint32),
                   jax.ShapeDtypeStruct((8, t), jnp.int32),
                   jax.ShapeDtypeStruct((8, t), F32),
                   jax.ShapeDtypeStruct((8, t), jnp.int32),
                   jax.ShapeDtypeStruct((N_EXPERTS, LANES), F32)],
        scratch_shapes=[pltpu.VMEM((N_EXPERTS, LANES), F32)],
        compiler_params=_cparams(("arbitrary",)),
        name="outproj_router",
    )(o_gla, o_ret, o_s5, wg, wr, ws, x3, mod5, mod5, mod5, nw2, rw_p, rb_p)


GATHER_WIN = 64


def _gather_rows(table, idx):
    m = idx.shape[0]
    w = table.shape[1]
    mesh = plsc.VectorSubcoreMesh(core_axis_name="core", subcore_axis_name="subcore")

    @functools.partial(pl.kernel, out_type=jax.ShapeDtypeStruct((m, w), table.dtype),
                       mesh=mesh, name="sc_row_gather")
    def gather(x_hbm, i_hbm, o_hbm):
        def body(i_vmem, o_vmem):
            pltpu.sync_copy(x_hbm.at[i_vmem], o_vmem)

        pltpu.emit_pipeline(
            body,
            grid=(m // GATHER_WIN,),
            in_specs=[pl.BlockSpec((GATHER_WIN,), lambda i: (i,))],
            out_specs=[pl.BlockSpec((GATHER_WIN, w), lambda i: (i, 0))],
            core_axis_name=("core", "subcore"),
            dimension_semantics=(pltpu.PARALLEL,),
        )(i_hbm, o_hbm)

    return gather(table, idx)


def _scatter_rows(x, dest_slot_major, n_rows):
    t, w = x.shape
    steps = t // GATHER_WIN
    mesh = plsc.VectorSubcoreMesh(core_axis_name="core", subcore_axis_name="subcore")

    @functools.partial(pl.kernel, out_type=jax.ShapeDtypeStruct((n_rows, w), x.dtype),
                       mesh=mesh, name="sc_row_scatter")
    def scatter(x_hbm, i_hbm, o_hbm):
        def body(x_vmem, i0, i1, i2, i3):
            for i_vmem in (i0, i1, i2, i3):
                pltpu.sync_copy(x_vmem, o_hbm.at[i_vmem])

        slot = lambda k: pl.BlockSpec((GATHER_WIN,), lambda i: (k * steps + i,))
        pltpu.emit_pipeline(
            body,
            grid=(steps,),
            in_specs=[pl.BlockSpec((GATHER_WIN, w), lambda i: (i, 0)),
                      slot(0), slot(1), slot(2), slot(3)],
            out_specs=[],
            core_axis_name=("core", "subcore"),
            dimension_semantics=(pltpu.PARALLEL,),
        )(x_hbm, i_hbm, i_hbm, i_hbm, i_hbm)

    return scatter(x, dest_slot_major)


def _expert_kernel(be_ref, nv_ref, rows_ref, wu_ref, bu_ref, wd_ref, bd_ref, o_ref,
                   wu_bf, wd_bf):
    i = pl.program_id(0)
    e = be_ref[i]
    prev = be_ref[jnp.maximum(i - 1, 0)]

    @pl.when((i == 0) | (e != prev))
    def _():
        wu_bf[...] = wu_ref[0, 0].astype(BF16)
        wd_bf[...] = wd_ref[0, 0].astype(BF16)

    def run_rows(r0, n, left):
        rs = slice(r0, r0 + n)
        row = lax.broadcasted_iota(jnp.int32, (n, rows_ref.shape[1]), 0)
        x_hi, x_lo = _unpack_bf16_pairs(jnp.where(row < left, rows_ref[rs, :], jnp.uint32(0)))
        x = jnp.concatenate([x_hi.astype(BF16), x_lo.astype(BF16)], axis=1)
        up = jnp.dot(x, wu_bf[...], preferred_element_type=F32) + bu_ref[0, 0]
        x_glu = jnp.minimum(up[:, :D_FF], SWIGLU_LIMIT)
        x_lin = jnp.clip(up[:, D_FF:], -SWIGLU_LIMIT, SWIGLU_LIMIT)
        act = x_glu * jax.nn.sigmoid(SWIGLU_ALPHA * x_glu) * (x_lin + 1.0)
        o_ref[rs, :] = _pack_bf16_pairs(
            jnp.dot(act.astype(BF16), wd_bf[...], preferred_element_type=F32) + bd_ref[0, 0])

    def zero_rows(r0, n):
        o_ref[r0:r0 + n, :] = jnp.zeros((n, o_ref.shape[1]), o_ref.dtype)

    half = ROW_SUB // 2
    for s in range(ROW_BLK // ROW_SUB):
        r0 = s * ROW_SUB
        left = nv_ref[i] - r0

        @pl.when(left > half)
        def _():
            run_rows(r0, ROW_SUB, left)

        @pl.when((left > 0) & (left <= half))
        def _():
            run_rows(r0, half, left)
            zero_rows(r0 + half, half)

        @pl.when(left <= 0)
        def _():
            zero_rows(r0, ROW_SUB)


def _experts(layer, block_e, n_valid, rows, w_up, b_up, w_down, b_down):
    n_rows, dh = rows.shape
    d = 2 * dh
    n_blocks = n_rows // ROW_BLK
    depth, ne, _, f2 = w_up.shape
    wsel = lambda i, be, nu: (layer, be[i], 0, 0)
    grid_spec = pltpu.PrefetchScalarGridSpec(
        num_scalar_prefetch=2,
        grid=(n_blocks,),
        in_specs=[pl.BlockSpec((ROW_BLK, dh), lambda i, be, nu: (i, 0)),
                  pl.BlockSpec((1, 1, d, f2), wsel),
                  pl.BlockSpec((1, 1, 1, f2), wsel),
                  pl.BlockSpec((1, 1, D_FF, d), wsel),
                  pl.BlockSpec((1, 1, 1, d), wsel)],
        out_specs=pl.BlockSpec((ROW_BLK, dh), lambda i, be, nu: (i, 0)),
        scratch_shapes=[pltpu.VMEM((d, f2), BF16), pltpu.VMEM((D_FF, d), BF16)],
    )
    return pl.pallas_call(
        _expert_kernel,
        grid_spec=grid_spec,
        out_shape=jax.ShapeDtypeStruct((n_rows, dh), jnp.uint32),
        compiler_params=_cparams(("arbitrary",)),
        name="moe_experts",
    )(block_e, n_valid, rows, w_up, b_up.reshape(depth, ne, 1, f2), w_down,
      b_down.reshape(depth, ne, 1, d))


def _combine_kernel(y0_ref, y1_ref, y2_ref, y3_ref, gate_ref, x1_ref, g2_ref, fw_ref, *rest,
                    final):
    o_ref = rest[-1]
    gates = gate_ref[...]
    y_hi, y_lo = None, None
    for k, y_ref in enumerate((y0_ref, y1_ref, y2_ref, y3_ref)):
        hi, lo = _unpack_bf16_pairs(y_ref[...])
        g = gates[:, k:k + 1]
        y_hi = g * hi if y_hi is None else y_hi + g * hi
        y_lo = g * lo if y_lo is None else y_lo + g * lo
    y = jnp.concatenate([y_hi, y_lo], axis=1)
    x2 = x1_ref[...] + g2_ref[0] * y
    if final:
        x2 = (x2 * lax.rsqrt(jnp.mean(x2 * x2, axis=-1, keepdims=True) + NORM_EPS)) * fw_ref[...]
    o_ref[...] = x2


def _combine(y4, gates, x1, mod5, b0, layer, fw, seq, final, grp, n_out, prev_out, th=256):
    t, d = x1.shape
    steps = t // th
    per_b = seq // th
    slot = lambda k: pl.BlockSpec((th, d // 2), lambda i: (k * steps + i, 0))
    return pl.pallas_call(
        functools.partial(_combine_kernel, final=final),
        grid=(steps,),
        in_specs=[slot(0), slot(1), slot(2), slot(3),
                  pl.BlockSpec((th, TOP_K), lambda i: (i, 0)),
                  pl.BlockSpec((th, d), lambda i: (i, 0)),
                  _mod_spec(layer, MOD_G2, per_b, b0),
                  pl.BlockSpec((1, d), lambda i: (0, 0))]
                 + ([] if prev_out is None else [pl.BlockSpec(memory_space=pl.ANY)]),
        out_specs=pl.BlockSpec((None, th, d), lambda i: (grp, i, 0)),
        out_shape=jax.ShapeDtypeStruct((n_out, t, d), F32),
        input_output_aliases={} if prev_out is None else {8: 0},
        compiler_params=_cparams(("arbitrary",)),
        name="moe_combine",
    )(y4, y4, y4, y4, gates, x1, mod5, fw, *([] if prev_out is None else [prev_out]))


def _retention_tables(seq):
    f32 = np.float32
    pos = np.arange(seq, dtype=f32)
    inv_freq = (f32(ROPE_BASE) ** (-np.arange(0, HEAD_DK, 2, dtype=f32) / f32(HEAD_DK))).astype(f32)
    ang = pos[:, None] * inv_freq[None, :]
    cos, sin = np.cos(ang), np.sin(ang)
    zero = np.zeros_like(sin)
    zpad = np.zeros((seq, LANES - 2 * HEAD_DK), f32)
    cos_t = np.concatenate([cos, cos, cos, cos, zpad], axis=1)
    sina_t = np.concatenate([-sin, zero, -sin, zero, zpad], axis=1)
    sinb_t = np.concatenate([zero, sin, zero, sin, zpad], axis=1)
    log_gamma = np.log1p(-np.exp2(f32(-5.0) - np.arange(N_HEADS, dtype=f32))).astype(f32)
    log_decay = np.broadcast_to(log_gamma[None, :, None], (RET_CHUNK, N_HEADS, HEAD_DK))
    cum = np.cumsum(log_decay, axis=0, dtype=f32)
    tot = cum[-1:]

    def shp(a):
        flat = a.reshape(a.shape[0], N_HEADS * HEAD_DK)
        return np.where(_DK_SRC >= 0, flat[:, np.maximum(_DK_SRC, 0)], f32(1.0)).astype(f32)

    tabs = (cos_t, sina_t, sinb_t, shp(np.exp(cum)), shp(np.exp(-cum)), shp(np.exp(tot - cum)),
            shp(np.exp(tot)))
    return tuple(jnp.asarray(a, F32) for a in tabs)


def kernel(x, c, norm1_w, norm2_w, w_mod, b_mod, w_in, gla_w_a2, gla_b_a, gla_norm_w, ret_norm_w, s5_a_re, s5_a_im, s5_log_dt, s5_b_re, s5_b_im, s5_c_re, s5_c_im, s5_d, s5_w_glu, s5_b_glu, w_out, router_w, router_b, w_up, b_up, w_down, b_down, final_norm_w):
    batch, seq, d = x.shape
    depth = w_mod.shape[0]
    gb = batch // N_STREAMS
    t = gb * seq
    n_slots = t * TOP_K
    n_blocks = n_slots // ROW_BLK + N_EXPERTS
    n_rows = n_blocks * ROW_BLK

    mod = _modulation(c, w_mod, b_mod)
    mod5 = mod.reshape(depth, batch, 6, 1, d).transpose(0, 2, 1, 3, 4)
    ret_tabs = _retention_tables(seq)

    w_p = _take_cols(w_in, _IN_SRC).astype(BF16)
    wa_p = jnp.zeros((depth, LANES, QKP), F32).at[:, :GATE_RANK].set(
        _take_cols(gla_w_a2, _DK_SRC)).astype(BF16)
    ba_p = _take_cols(gla_b_a, _DK_SRC).reshape(depth, 1, QKP)
    gnw = _take_cols(gla_norm_w, _DV_SRC).reshape(depth, 1, HP)
    rnw = _take_cols(ret_norm_w, _DV_SRC).reshape(depth, 1, HP)
    kv = N_HEADS * HEAD_DV
    wo_g = _take_rows(w_out[:, :kv], _DV_SRC).astype(BF16)
    wo_r = _take_rows(w_out[:, kv:2 * kv], _DV_SRC).astype(BF16)
    wo_s = w_out[:, 2 * kv:].astype(BF16)
    rw_t = jnp.swapaxes(router_w, 1, 2)
    rw_hi = rw_t.astype(BF16)
    rw_p = jnp.concatenate([rw_hi, (rw_t - rw_hi.astype(F32)).astype(BF16)], axis=1)
    rb_p = jnp.broadcast_to(router_b[:, :, None], (depth, N_EXPERTS, LANES))
    s5_tabs = jax.vmap(_s5_tables)(s5_a_re, s5_a_im, s5_log_dt, s5_b_re, s5_b_im, s5_c_re,
                                   s5_c_im, s5_d, s5_w_glu, s5_b_glu)
    n1 = norm1_w.reshape(depth, 1, d)
    n2 = norm2_w.reshape(depth, 1, d)
    fw = final_norm_w.reshape(1, d)

    xs = [(x.reshape(N_STREAMS, t, d), g) for g in range(N_STREAMS)]
    out = None

    for i in range(depth):
        final = i == depth - 1
        for g in range(N_STREAMS):
            x3, grp = xs[g]
            b0 = g * gb

            proj = _in_projection(x3, grp, mod5, b0, i, n1, w_p, seq)
            o_gla = _gla_mixer(proj, gb, seq, i, wa_p, ba_p, gnw)
            o_ret = _ret_mixer(proj, gb, seq, i, *ret_tabs, rnw)
            o_s5 = _s5_mixer(proj, gb, seq, i, s5_tabs)

            x1, hdn, idx, gates, rank, counts = _outproj_router(
                o_gla, o_ret, o_s5, wo_g, wo_r, wo_s, x3, grp, mod5, b0, i, n2, rw_p, rb_p, seq)

            cnt = counts[:, 0].astype(jnp.int32)
            padded = (cnt + ROW_BLK - 1) // ROW_BLK * ROW_BLK
            pad_ends = jnp.cumsum(padded)
            pad_starts = pad_ends - padded
            slot_start = jnp.sum(jnp.where(idx[:TOP_K, :, None] == jnp.arange(N_EXPERTS),
                                           pad_starts, 0), axis=-1)
            dest_sm = (slot_start + rank[:TOP_K]).astype(jnp.int32).reshape(-1)
            gates_tm = gates[:TOP_K].T
            blk_start = jnp.arange(n_blocks, dtype=jnp.int32) * ROW_BLK
            block_e = jnp.minimum(jnp.sum(pad_ends[None, :] <= blk_start[:, None], axis=1),
                                  N_EXPERTS - 1).astype(jnp.int32)
            n_valid = jnp.clip((pad_starts + cnt)[block_e] - blk_start, 0,
                               ROW_BLK).astype(jnp.int32)

            rows = _scatter_rows(hdn, dest_sm, n_rows)
            out_rows = _experts(i, block_e, n_valid, rows, w_up, b_up, w_down, b_down)
            y4 = _gather_rows(out_rows, dest_sm)
            if final:
                out = _combine(y4, gates_tm, x1, mod5, b0, i, fw, seq, True, g, N_STREAMS, out)
            else:
                xs[g] = (_combine(y4, gates_tm, x1, mod5, b0, i, fw, seq, False, 0, 1, None), 0)

    return out.reshape(batch, seq, d)
```

```python
import functools

import numpy as np
import jax
import jax.numpy as jnp
from jax import lax
from jax.experimental import pallas as pl
from jax.experimental.pallas import tpu as pltpu
from jax.experimental.pallas import tpu_sc as plsc

D_MODEL = 1024
CHUNK = 64
RET_CHUNK = 128
NORM_EPS = 1e-5
N_HEADS = 4
HEAD_DK = 48
HEAD_DV = 96
GATE_RANK = 16
GATE_TEMP = 16.0
ROPE_BASE = 10000.0
S5_WIDTH = 256
S5_GROUP_DIM = 16
S5_GROUPS = 16
S5_STATE = 64
N_EXPERTS = 32
TOP_K = 4
D_FF = 1024
SWIGLU_LIMIT = 7.0
SWIGLU_ALPHA = 1.702

LANES = 128
HEAD_PAD = LANES
HP = N_HEADS * HEAD_PAD
N_PAIRS = N_HEADS // 2
QKP = N_PAIRS * LANES
VMEM_LIMIT = 56 * 1024 * 1024

OFF_GQ, OFF_GK, OFF_GV, OFF_GG = 0, QKP, 2 * QKP, 2 * QKP + HP
OFF_RQ = OFF_GG + HP
OFF_RK, OFF_RV, OFF_RG = OFF_RQ + QKP, OFF_RQ + 2 * QKP, OFF_RQ + 2 * QKP + HP
OFF_SU = OFF_RG + HP
OFF_GA = OFF_SU + S5_WIDTH
NP_COLS = OFF_GA + LANES
PROJ_CH = 1152

ROW_BLK = 1024
ROW_SUB = 512
MOD_SH1, MOD_SC1, MOD_G1, MOD_SH2, MOD_SC2, MOD_G2 = range(6)
N_STREAMS = 1

F32 = jnp.float32
BF16 = jnp.bfloat16


def _DK_SRC_LANE(h, d):
    return (h // 2) * LANES + (h % 2) * HEAD_DK + d


def _in_col_map():
    src = -np.ones((NP_COLS,), np.int64)
    kq = N_HEADS * HEAD_DK
    kv = N_HEADS * HEAD_DV
    base = dict(gq=0, gk=kq, gv=2 * kq, gg=2 * kq + kv, ga=2 * kq + 2 * kv)
    r0 = base['ga'] + GATE_RANK
    base.update(rq=r0, rk=r0 + kq, rv=r0 + 2 * kq, rg=r0 + 2 * kq + kv, su=r0 + 2 * kq + 2 * kv)
    for h in range(N_HEADS):
        for d in range(HEAD_DK):
            lane = _DK_SRC_LANE(h, d)
            src[OFF_GQ + lane] = base['gq'] + h * HEAD_DK + d
            src[OFF_GK + lane] = base['gk'] + h * HEAD_DK + d
            src[OFF_RQ + lane] = base['rq'] + h * HEAD_DK + d
            src[OFF_RK + lane] = base['rk'] + h * HEAD_DK + d
        for d in range(HEAD_DV):
            src[OFF_GV + h * HEAD_PAD + d] = base['gv'] + h * HEAD_DV + d
            src[OFF_GG + h * HEAD_PAD + d] = base['gg'] + h * HEAD_DV + d
            src[OFF_RV + h * HEAD_PAD + d] = base['rv'] + h * HEAD_DV + d
            src[OFF_RG + h * HEAD_PAD + d] = base['rg'] + h * HEAD_DV + d
    src[OFF_SU:OFF_SU + S5_WIDTH] = base['su'] + np.arange(S5_WIDTH)
    src[OFF_GA:OFF_GA + GATE_RANK] = base['ga'] + np.arange(GATE_RANK)
    return src


_IN_SRC = _in_col_map()


def _head_pad_map(width):
    src = -np.ones((HP,), np.int64)
    for h in range(N_HEADS):
        src[h * HEAD_PAD:h * HEAD_PAD + width] = h * width + np.arange(width)
    return src


_DV_SRC = _head_pad_map(HEAD_DV)
_DK_SRC = -np.ones((QKP,), np.int64)
for _h in range(N_HEADS):
    for _d in range(HEAD_DK):
        _DK_SRC[_DK_SRC_LANE(_h, _d)] = _h * HEAD_DK + _d


def _take_static(w, src, axis):
    axis = axis % w.ndim
    pieces, start = [], 0
    for j in range(1, len(src) + 1):
        run_ends = (j == len(src) or (src[j] < 0) != (src[start] < 0)
                    or (src[start] >= 0 and src[j] != src[j - 1] + 1))
        if run_ends:
            if src[start] < 0:
                shape = w.shape[:axis] + (j - start,) + w.shape[axis + 1:]
                pieces.append(jnp.zeros(shape, w.dtype))
            else:
                pieces.append(lax.slice_in_dim(w, int(src[start]), int(src[j - 1]) + 1, axis=axis))
            start = j
    return jnp.concatenate(pieces, axis=axis)


def _take_cols(w, src):
    return _take_static(w, src, -1)


def _take_rows(w, src):
    return _take_static(w, src, -2)


def _layer_spec(shape, layer):
    return pl.BlockSpec((None,) + tuple(shape), lambda *_: (layer,) + (0,) * len(shape))


def _mod_spec(layer, which, per_b, b0):
    return pl.BlockSpec((None, None, 1, 1, D_MODEL),
                        lambda i: (layer, which, b0 + i // per_b, 0, 0))


def _cparams(sem):
    return pltpu.CompilerParams(dimension_semantics=sem, vmem_limit_bytes=VMEM_LIMIT)


def _mod_kernel(c_ref, w_ref, b_ref, o_ref):
    c = c_ref[...]
    cond = c * jax.nn.sigmoid(c)
    o_ref[0] = jnp.dot(cond, w_ref[0], preferred_element_type=F32,
                       precision=lax.Precision.HIGHEST) + b_ref[0]


def _modulation(c, w_mod, b_mod):
    depth, d, n = w_mod.shape
    b = c.shape[0]
    nb = 1536
    return pl.pallas_call(
        _mod_kernel,
        grid=(depth, n // nb),
        in_specs=[pl.BlockSpec((b, d), lambda l, j: (0, 0)),
                  pl.BlockSpec((1, d, nb), lambda l, j: (l, 0, j)),
                  pl.BlockSpec((1, 1, nb), lambda l, j: (l, 0, j))],
        out_specs=pl.BlockSpec((1, b, nb), lambda l, j: (l, 0, j)),
        out_shape=jax.ShapeDtypeStruct((depth, b, n), F32),
        compiler_params=_cparams(("arbitrary", "arbitrary")),
        name="adaln_mod",
    )(c, w_mod, b_mod.reshape(depth, 1, n))


def _rms_mod(x, nw, sc, sh):
    y = x * lax.rsqrt(jnp.mean(x * x, axis=-1, keepdims=True) + NORM_EPS)
    return (y * nw) * (1.0 + sc) + sh


def _inproj_kernel(x_ref, sc_ref, sh_ref, nw_ref, w_ref, o_ref):
    h = _rms_mod(x_ref[...], nw_ref[...], sc_ref[0], sh_ref[0]).astype(BF16)
    for j in range(NP_COLS // PROJ_CH):
        cs = slice(j * PROJ_CH, (j + 1) * PROJ_CH)
        o_ref[:, cs] = jnp.dot(h, w_ref[:, cs], preferred_element_type=F32).astype(BF16)


def _in_projection(x3, grp, mod5, b0, layer, nw, w_p, seq, tm=1024):
    _, t, d = x3.shape
    tm = min(tm, seq)
    per_b = seq // tm
    return pl.pallas_call(
        _inproj_kernel,
        grid=(t // tm,),
        in_specs=[pl.BlockSpec((None, tm, d), lambda i: (grp, i, 0)),
                  _mod_spec(layer, MOD_SC1, per_b, b0), _mod_spec(layer, MOD_SH1, per_b, b0),
                  _layer_spec((1, d), layer), _layer_spec((d, NP_COLS), layer)],
        out_specs=pl.BlockSpec((tm, NP_COLS), lambda i: (i, 0)),
        out_shape=jax.ShapeDtypeStruct((t, NP_COLS), BF16),
        compiler_params=_cparams(("arbitrary",)),
        name="in_proj",
    )(x3, mod5, mod5, nw, w_p)


_NT = (((1,), (1,)), ((), ()))
_TN = (((0,), (0,)), ((), ()))


def _tri_mask(n=CHUNK):
    r = lax.broadcasted_iota(jnp.int32, (n, n), 0)
    c = lax.broadcasted_iota(jnp.int32, (n, n), 1)
    return r >= c


def _pair_masks(rows):
    lane = lax.broadcasted_iota(jnp.int32, (rows, LANES), 1)
    return lane < HEAD_DK, (lane >= HEAD_DK) & (lane < 2 * HEAD_DK)


def _head_attention(qd, ki, ke, vh, et, st_ref, h, causal):
    qb = qd.astype(BF16)
    sc = lax.dot_general(qb, ki.astype(BF16), _NT, preferred_element_type=F32)
    sc = jnp.where(causal, sc, 0.0)
    st = st_ref[h]
    o = jnp.dot(sc.astype(BF16), vh, preferred_element_type=F32)
    o = o + lax.dot_general(qb, st.astype(BF16), _NT, preferred_element_type=F32)
    st_ref[h] = st * et + lax.dot_general(vh, ke.astype(BF16), _TN, preferred_element_type=F32)
    return o


def _gla_kernel(q_ref, k_ref, v_ref, g_ref, a_ref, wa_ref, ba_ref, nw_ref, tri_ref, o_ref,
                st_ref, qd_s, ki_s, ke_s, et_s):
    @pl.when(pl.program_id(1) == 0)
    def _():
        st_ref[...] = jnp.zeros_like(st_ref)

    causal = _tri_mask()
    tl = q_ref.shape[0]
    n_chunks = tl // CHUNK

    z = jnp.dot(a_ref[...], wa_ref[...], preferred_element_type=F32) + ba_ref[...]
    la = (jnp.minimum(z, 0.0) - jnp.log1p(jnp.exp(-jnp.abs(z)))) * (1.0 / GATE_TEMP)
    hi = la.astype(BF16)
    lo = (la - hi.astype(F32)).astype(BF16)
    cum = (jnp.dot(tri_ref[...], hi, preferred_element_type=F32)
           + jnp.dot(tri_ref[...], lo, preferred_element_type=F32))
    cum3 = cum.reshape(n_chunks, CHUNK, QKP)
    tot3 = cum3[:, CHUNK - 1:CHUNK, :]
    qd = (q_ref[...].astype(F32) * (HEAD_DK ** -0.5)) * jnp.exp(cum)
    masks = _pair_masks(tl)
    for h in range(N_HEADS):
        pair = slice((h // 2) * LANES, (h // 2 + 1) * LANES)
        qd_s[:, h * HEAD_PAD:(h + 1) * HEAD_PAD] = jnp.where(masks[h % 2], qd[:, pair],
                                                              0.0).astype(BF16)
    kf = k_ref[...].astype(F32)
    ki_s[...] = (kf * jnp.exp(-cum)).astype(BF16)
    ke_s[...] = (kf * jnp.exp(tot3 - cum3).reshape(tl, QKP)).astype(BF16)
    et_s[...] = jnp.exp(tot3).reshape(n_chunks, QKP)

    def chunk(c, carry):
        r = pl.ds(pl.multiple_of(c * CHUNK, CHUNK), CHUNK)
        et = et_s[pl.ds(c, 1), :]
        for h in range(N_HEADS):
            sl = slice(h * HEAD_PAD, (h + 1) * HEAD_PAD)
            pair = slice((h // 2) * LANES, (h // 2 + 1) * LANES)
            o = _head_attention(qd_s[r, sl], ki_s[r, pair], ke_s[r, pair], v_ref[r, sl],
                                et[:, pair], st_ref, h, causal)
            ms = jnp.sum(o * o, axis=-1, keepdims=True) * (1.0 / HEAD_DV)
            y = (o * lax.rsqrt(ms + NORM_EPS)) * nw_ref[:, sl]
            g = g_ref[r, sl].astype(F32)
            o_ref[r, sl] = (y * (g * jax.nn.sigmoid(g))).astype(BF16)
        return carry

    lax.fori_loop(0, n_chunks, chunk, 0, unroll=4)


def _ret_kernel(q_ref, k_ref, v_ref, g_ref, cos_ref, sina_ref, sinb_ref, dq_ref, dki_ref,
                dke_ref, dt_ref, nw_ref, o_ref, st_ref):
    @pl.when(pl.program_id(1) == 0)
    def _():
        st_ref[...] = jnp.zeros_like(st_ref)

    causal = _tri_mask(RET_CHUNK)
    n_chunks = q_ref.shape[0] // RET_CHUNK
    lane = lax.broadcasted_iota(jnp.int32, (RET_CHUNK, HEAD_PAD), 1)
    real = lane < HEAD_DV
    masks = _pair_masks(RET_CHUNK)
    half = HEAD_DK // 2

    def rotary(t, cos, sina, sinb):
        return (t * cos + pltpu.roll(t, LANES - half, 1) * sina + pltpu.roll(t, half, 1) * sinb)

    def chunk(c, carry):
        r = pl.ds(pl.multiple_of(c * RET_CHUNK, RET_CHUNK), RET_CHUNK)
        cos, sina, sinb = cos_ref[r, :], sina_ref[r, :], sinb_ref[r, :]
        pair_q, pair_ki, pair_ke = [], [], []
        for p in range(N_PAIRS):
            ps = slice(p * LANES, (p + 1) * LANES)
            qr = rotary(q_ref[r, ps].astype(F32), cos, sina, sinb) * dq_ref[:, ps]
            kr = rotary(k_ref[r, ps].astype(F32), cos, sina, sinb) * (HEAD_DK ** -0.5)
            pair_q.append(qr)
            pair_ki.append((kr * dki_ref[:, ps]).astype(BF16))
            pair_ke.append((kr * dke_ref[:, ps]).astype(BF16))
        for h in range(N_HEADS):
            sl = slice(h * HEAD_PAD, (h + 1) * HEAD_PAD)
            p = h // 2
            qd = jnp.where(masks[h % 2], pair_q[p], 0.0)
            o = _head_attention(qd, pair_ki[p], pair_ke[p], v_ref[r, sl],
                                dt_ref[:, p * LANES:(p + 1) * LANES], st_ref, h, causal)
            mu = jnp.sum(o, axis=-1, keepdims=True) * (1.0 / HEAD_DV)
            oc = jnp.where(real, o - mu, 0.0)
            var = jnp.sum(oc * oc, axis=-1, keepdims=True) * (1.0 / HEAD_DV)
            y = (oc * lax.rsqrt(var + NORM_EPS)) * nw_ref[:, sl]
            g = g_ref[r, sl].astype(F32)
            o_ref[r, sl] = (y * (g * jax.nn.sigmoid(g))).astype(BF16)
        return carry

    lax.fori_loop(0, n_chunks, chunk, 0, unroll=2)


def _proj_spec(tl, width, col_off, per_b):
    cb = col_off // width
    return pl.BlockSpec((tl, width), lambda b, l: (b * per_b + l, cb))


def _full(shape):
    return pl.BlockSpec(shape, lambda b, l: (0,) * len(shape))


def _gla_mixer(proj, batch, seq, layer, wa_p, ba_p, nw_p, tl=512):
    per_b = seq // tl
    pos = np.arange(tl)
    tri_bd = jnp.asarray((pos[:, None] // CHUNK == pos[None, :] // CHUNK)
                         & (pos[:, None] >= pos[None, :]), BF16)
    return pl.pallas_call(
        _gla_kernel,
        grid=(batch, per_b),
        in_specs=[_proj_spec(tl, QKP, OFF_GQ, per_b), _proj_spec(tl, QKP, OFF_GK, per_b),
                  _proj_spec(tl, HP, OFF_GV, per_b), _proj_spec(tl, HP, OFF_GG, per_b),
                  _proj_spec(tl, LANES, OFF_GA, per_b),
                  _layer_spec((LANES, QKP), layer), _layer_spec((1, QKP), layer),
                  _layer_spec((1, HP), layer), _full((tl, tl))],
        out_specs=pl.BlockSpec((tl, HP), lambda b, l: (b * per_b + l, 0)),
        out_shape=jax.ShapeDtypeStruct((batch * seq, HP), BF16),
        scratch_shapes=[pltpu.VMEM((N_HEADS, HEAD_PAD, HEAD_PAD), F32),
                        pltpu.VMEM((tl, HP), BF16), pltpu.VMEM((tl, QKP), BF16),
                        pltpu.VMEM((tl, QKP), BF16), pltpu.VMEM((tl // CHUNK, QKP), F32)],
        compiler_params=_cparams(("arbitrary", "arbitrary")),
        name="gla_mixer",
    )(proj, proj, proj, proj, proj, wa_p, ba_p, nw_p, tri_bd)


def _ret_mixer(proj, batch, seq, layer, cos_t, sina_t, sinb_t, dq, dki, dke, dtot, nw_p, tl=512):
    per_b = seq // tl
    return pl.pallas_call(
        _ret_kernel,
        grid=(batch, per_b),
        in_specs=[_proj_spec(tl, QKP, OFF_RQ, per_b), _proj_spec(tl, QKP, OFF_RK, per_b),
                  _proj_spec(tl, HP, OFF_RV, per_b), _proj_spec(tl, HP, OFF_RG, per_b),
                  pl.BlockSpec((tl, LANES), lambda b, l: (l, 0)),
                  pl.BlockSpec((tl, LANES), lambda b, l: (l, 0)),
                  pl.BlockSpec((tl, LANES), lambda b, l: (l, 0)),
                  _full((RET_CHUNK, QKP)), _full((RET_CHUNK, QKP)), _full((RET_CHUNK, QKP)),
                  _full((1, QKP)), _layer_spec((1, HP), layer)],
        out_specs=pl.BlockSpec((tl, HP), lambda b, l: (b * per_b + l, 0)),
        out_shape=jax.ShapeDtypeStruct((batch * seq, HP), BF16),
        scratch_shapes=[pltpu.VMEM((N_HEADS, HEAD_PAD, HEAD_PAD), F32)],
        compiler_params=_cparams(("arbitrary", "arbitrary")),
        name="ret_mixer",
    )(proj, proj, proj, proj, cos_t, sina_t, sinb_t, dq, dki, dke, dtot, nw_p)


S5_N = S5_GROUPS * S5_STATE
S5_SLAB = 256


def _gelu_tanh(x):
    return 0.5 * x * (1.0 + jnp.tanh(np.sqrt(2.0 / np.pi) * (x + 0.044715 * (x * x * x))))


def _s5_kernel(u_ref, bb_ref, cb_ref, nr_ref, ni_ref, pr_ref, pi_ref, lr_ref, li_ref,
               d_ref, wg_ref, bg_ref, o_ref, sr_ref, si_ref, x_scr, s_scr):
    @pl.when(pl.program_id(1) == 0)
    def _():
        sr_ref[...] = jnp.zeros_like(sr_ref)
        si_ref[...] = jnp.zeros_like(si_ref)

    tri = _tri_mask().astype(BF16)
    n_chunks = u_ref.shape[0] // CHUNK
    u = u_ref[...]
    x_scr[...] = jnp.dot(u, bb_ref[...], preferred_element_type=F32)

    def chunk(c, carry):
        r = pl.ds(pl.multiple_of(c * CHUNK, CHUNK), CHUNK)
        for j in range(S5_N // S5_SLAB):
            cs = slice(j * S5_SLAB, (j + 1) * S5_SLAB)
            ci = slice(S5_N + j * S5_SLAB, S5_N + (j + 1) * S5_SLAB)
            xr, xi = x_scr[r, cs], x_scr[r, ci]
            nr, ni = nr_ref[:, cs], ni_ref[:, cs]
            p_r = jnp.dot(tri, (xr * nr - xi * ni).astype(BF16), preferred_element_type=F32)
            p_i = jnp.dot(tri, (xr * ni + xi * nr).astype(BF16), preferred_element_type=F32)
            s0r, s0i = sr_ref[:, cs], si_ref[:, cs]
            lr, li = lr_ref[:, cs], li_ref[:, cs]
            q_r = p_r + (s0r * lr - s0i * li)
            q_i = p_i + (s0r * li + s0i * lr)
            pr, pi = pr_ref[:, cs], pi_ref[:, cs]
            s_r = q_r * pr - q_i * pi
            s_i = q_r * pi + q_i * pr
            sr_ref[:, cs] = s_r[CHUNK - 1:CHUNK, :]
            si_ref[:, cs] = s_i[CHUNK - 1:CHUNK, :]
            s_scr[r, cs] = s_r.astype(BF16)
            s_scr[r, ci] = s_i.astype(BF16)
        return carry

    lax.fori_loop(0, n_chunks, chunk, 0, unroll=2)
    y = jnp.dot(s_scr[...], cb_ref[...], preferred_element_type=F32)
    y = _gelu_tanh(y + d_ref[...] * u.astype(F32))
    gate = jnp.dot(y.astype(BF16), wg_ref[...], preferred_element_type=F32) + bg_ref[...]
    o_ref[...] = (y * jax.nn.sigmoid(gate)).astype(BF16)


def _s5_mixer(proj, batch, seq, layer, tabs, tl=512):
    per_b = seq // tl
    bb, cb, nr, ni, pr, pi, lr, li, dsk, wg, bg = tabs
    return pl.pallas_call(
        _s5_kernel,
        grid=(batch, per_b),
        in_specs=[_proj_spec(tl, S5_WIDTH, OFF_SU, per_b),
                  *[_layer_spec(a.shape[1:], layer) for a in tabs]],
        out_specs=pl.BlockSpec((tl, S5_WIDTH), lambda b, l: (b * per_b + l, 0)),
        out_shape=jax.ShapeDtypeStruct((batch * seq, S5_WIDTH), BF16),
        scratch_shapes=[pltpu.VMEM((1, S5_N), F32), pltpu.VMEM((1, S5_N), F32),
                        pltpu.VMEM((tl, 2 * S5_N), F32), pltpu.VMEM((tl, 2 * S5_N), BF16)],
        compiler_params=_cparams(("arbitrary", "arbitrary")),
        name="s5_mixer",
    )(proj, bb, cb, nr, ni, pr, pi, lr, li, dsk, wg, bg)


def _s5_tables(a_re, a_im, log_dt, b_re, b_im, c_re, c_im, d_skip, w_glu, b_glu):
    lam = lax.complex(a_re, a_im)
    dt = jnp.exp(log_dt)[:, None]
    lam_bar = jnp.exp(lam * dt)
    b_bar = ((lam_bar - 1.0) / lam)[..., None] * lax.complex(b_re, b_im)
    eye = jnp.eye(S5_GROUPS, dtype=F32)
    def blk_b(m):
        return jnp.einsum('gph,gk->ghkp', m, eye).reshape(S5_WIDTH, S5_N)
    bb = jnp.concatenate([blk_b(jnp.real(b_bar)), blk_b(jnp.imag(b_bar))], axis=1)
    def blk_c(m):
        return jnp.einsum('ghp,gk->kpgh', m, eye).reshape(S5_N, S5_WIDTH)
    cb = jnp.concatenate([blk_c(c_re), blk_c(-c_im)], axis=0)
    steps = jnp.arange(CHUNK, dtype=F32)[:, None, None]
    lam_dt = (lam * dt)[None]
    pos = jnp.exp(lam_dt * steps).reshape(CHUNK, S5_N)
    neg = jnp.exp(-lam_dt * steps).reshape(CHUNK, S5_N)
    one = lam_bar.reshape(1, S5_N)
    return (bb.astype(BF16), cb.astype(BF16), jnp.real(neg), jnp.imag(neg), jnp.real(pos),
            jnp.imag(pos), jnp.real(one), jnp.imag(one), d_skip.reshape(1, S5_WIDTH),
            w_glu.astype(BF16), b_glu.reshape(1, S5_WIDTH))


def _pack_bf16_pairs(x):
    w = x.shape[1] // 2
    xb = x.astype(BF16).astype(F32)
    hi = lax.bitcast_convert_type(xb[:, :w], jnp.uint32)
    lo = lax.bitcast_convert_type(xb[:, w:], jnp.uint32)
    return hi | (lo >> 16)


def _unpack_bf16_pairs(p):
    hi = lax.bitcast_convert_type(p & jnp.uint32(0xFFFF0000), F32)
    lo = lax.bitcast_convert_type(p << 16, F32)
    return hi, lo


def _router_kernel(og_ref, or_ref, os_ref, wg_ref, wr_ref, ws_ref, x_ref, g1_ref, sc_ref,
                   sh_ref, nw_ref, rw_ref, rb_ref,
                   x1_ref, h_ref, idx_ref, gate_ref, rank_ref, cnt_ref, carry_ref):
    i = pl.program_id(0)

    @pl.when(i == 0)
    def _():
        carry_ref[...] = jnp.zeros_like(carry_ref)

    mix = (jnp.dot(og_ref[...], wg_ref[...], preferred_element_type=F32)
           + jnp.dot(or_ref[...], wr_ref[...], preferred_element_type=F32)
           + jnp.dot(os_ref[...], ws_ref[...], preferred_element_type=F32))
    x1 = x_ref[...] + g1_ref[0] * mix
    x1_ref[...] = x1
    hdn = _rms_mod(x1, nw_ref[...], sc_ref[0], sh_ref[0])
    h_ref[...] = _pack_bf16_pairs(hdn)
    h_hi = hdn.astype(BF16)
    h_lo = (hdn - h_hi.astype(F32)).astype(BF16)
    p = lax.dot_general(rw_ref[...], h_hi, _NT, preferred_element_type=F32)
    q = lax.dot_general(rw_ref[:N_EXPERTS, :], h_lo, _NT, preferred_element_type=F32)
    logits = (p[:N_EXPERTS] + p[N_EXPERTS:] + q) + rb_ref[:, 0:1]
    tm = logits.shape[1]
    eidx = lax.broadcasted_iota(jnp.int32, (N_EXPERTS, tm), 0)
    work = logits
    onehot = jnp.zeros((N_EXPERTS, tm), F32)
    vals, idxs, sels = [], [], []
    for _ in range(TOP_K):
        m = jnp.max(work, axis=0, keepdims=True)
        ix = jnp.min(jnp.where(work == m, eidx, N_EXPERTS), axis=0, keepdims=True)
        sel = eidx == ix
        work = jnp.where(sel, -jnp.inf, work)
        onehot = onehot + sel.astype(F32)
        vals.append(m)
        idxs.append(ix)
        sels.append(sel)
    exps = [jnp.exp(v - vals[0]) for v in vals]
    denom = exps[0] + exps[1] + exps[2] + exps[3]
    r = lax.broadcasted_iota(jnp.int32, (tm, tm), 0)
    c = lax.broadcasted_iota(jnp.int32, (tm, tm), 1)
    earlier = (r < c).astype(BF16)
    before = (jnp.dot(onehot.astype(BF16), earlier, preferred_element_type=F32)
              + carry_ref[:, 0:1])
    row8 = lax.broadcasted_iota(jnp.int32, (8, tm), 0)
    idx_out = jnp.zeros((8, tm), jnp.int32)
    gate_out = jnp.zeros((8, tm), F32)
    rank_out = jnp.zeros((8, tm), F32)
    for k in range(TOP_K):
        rk = jnp.sum(jnp.where(sels[k], before, 0.0), axis=0, keepdims=True)
        idx_out = jnp.where(row8 == k, idxs[k], idx_out)
        gate_out = jnp.where(row8 == k, exps[k] / denom, gate_out)
        rank_out = jnp.where(row8 == k, rk, rank_out)
    idx_ref[...] = idx_out
    gate_ref[...] = gate_out
    rank_ref[...] = rank_out.astype(jnp.int32)
    total = carry_ref[...] + jnp.sum(onehot, axis=1, keepdims=True)
    carry_ref[...] = total
    cnt_ref[...] = total


def _outproj_router(o_gla, o_ret, o_s5, wg, wr, ws, x3, grp, mod5, b0, layer, nw2, rw_p, rb_p,
                    seq, tm=512):
    _, t, d = x3.shape
    per_b = seq // tm
    row = lambda w: pl.BlockSpec((tm, w), lambda i: (i, 0))
    full = lambda s: pl.BlockSpec(s, lambda i: (0,) * len(s))
    slot_t = pl.BlockSpec((8, tm), lambda i: (0, i))
    return pl.pallas_call(
        _router_kernel,
        grid=(t // tm,),
        in_specs=[row(HP), row(HP), row(S5_WIDTH), _layer_spec((HP, d), layer),
                  _layer_spec((HP, d), layer), _layer_spec((S5_WIDTH, d), layer),
                  pl.BlockSpec((None, tm, d), lambda i: (grp, i, 0)),
                  _mod_spec(layer, MOD_G1, per_b, b0), _mod_spec(layer, MOD_SC2, per_b, b0),
                  _mod_spec(layer, MOD_SH2, per_b, b0), _layer_spec((1, d), layer),
                  _layer_spec((2 * N_EXPERTS, d), layer), _layer_spec((N_EXPERTS, LANES), layer)],
        out_specs=[row(d), row(d // 2), slot_t, slot_t, slot_t, full((N_EXPERTS, LANES))],
        out_shape=[jax.ShapeDtypeStruct((t, d), F32),
                   jax.ShapeDtypeStruct((t, d // 2), jnp.uint32),
                   jax.ShapeDtypeStruct((8, t), jnp.int32),
                   jax.ShapeDtypeStruct((8, t), F32),
                   jax.ShapeDtypeStruct((8, t), jnp.int32),
                   jax.ShapeDtypeStruct((N_EXPERTS, LANES), F32)],
        scratch_shapes=[pltpu.VMEM((N_EXPERTS, LANES), F32)],
        compiler_params=_cparams(("arbitrary",)),
        name="outproj_router",
    )(o_gla, o_ret, o_s5, wg, wr, ws, x3, mod5, mod5, mod5, nw2, rw_p, rb_p)


GATHER_WIN = 64


def _gather_rows(table, idx):
    m = idx.shape[0]
    w = table.shape[1]
    mesh = plsc.VectorSubcoreMesh(core_axis_name="core", subcore_axis_name="subcore")

    @functools.partial(pl.kernel, out_type=jax.ShapeDtypeStruct((m, w), table.dtype),
                       mesh=mesh, name="sc_row_gather")
    def gather(x_hbm, i_hbm, o_hbm):
        def body(i_vmem, o_vmem):
            pltpu.sync_copy(x_hbm.at[i_vmem], o_vmem)

        pltpu.emit_pipeline(
            body,
            grid=(m // GATHER_WIN,),
            in_specs=[pl.BlockSpec((GATHER_WIN,), lambda i: (i,))],
            out_specs=[pl.BlockSpec((GATHER_WIN, w), lambda i: (i, 0))],
            core_axis_name=("core", "subcore"),
            dimension_semantics=(pltpu.PARALLEL,),
        )(i_hbm, o_hbm)

    return gather(table, idx)


def _scatter_rows(x, dest_slot_major, n_rows):
    t, w = x.shape
    steps = t // GATHER_WIN
    mesh = plsc.VectorSubcoreMesh(core_axis_name="core", subcore_axis_name="subcore")

    @functools.partial(pl.kernel, out_type=jax.ShapeDtypeStruct((n_rows, w), x.dtype),
                       mesh=mesh, name="sc_row_scatter")
    def scatter(x_hbm, i_hbm, o_hbm):
        def body(x_vmem, i0, i1, i2, i3):
            for i_vmem in (i0, i1, i2, i3):
                pltpu.sync_copy(x_vmem, o_hbm.at[i_vmem])

        slot = lambda k: pl.BlockSpec((GATHER_WIN,), lambda i: (k * steps + i,))
        pltpu.emit_pipeline(
            body,
            grid=(steps,),
            in_specs=[pl.BlockSpec((GATHER_WIN, w), lambda i: (i, 0)),
                      slot(0), slot(1), slot(2), slot(3)],
            out_specs=[],
            core_axis_name=("core", "subcore"),
            dimension_semantics=(pltpu.PARALLEL,),
        )(x_hbm, i_hbm, i_hbm, i_hbm, i_hbm)

    return scatter(x, dest_slot_major)


def _expert_kernel(be_ref, nv_ref, rows_ref, wu_ref, bu_ref, wd_ref, bd_ref, o_ref,
                   wu_bf, wd_bf):
    i = pl.program_id(0)
    e = be_ref[i]
    prev = be_ref[jnp.maximum(i - 1, 0)]

    @pl.when((i == 0) | (e != prev))
    def _():
        wu_bf[...] = wu_ref[0, 0].astype(BF16)
        wd_bf[...] = wd_ref[0, 0].astype(BF16)

    def run_rows(r0, n, left):
        rs = slice(r0, r0 + n)
        row = lax.broadcasted_iota(jnp.int32, (n, rows_ref.shape[1]), 0)
        x_hi, x_lo = _unpack_bf16_pairs(jnp.where(row < left, rows_ref[rs, :], jnp.uint32(0)))
        x = jnp.concatenate([x_hi.astype(BF16), x_lo.astype(BF16)], axis=1)
        up = jnp.dot(x, wu_bf[...], preferred_element_type=F32) + bu_ref[0, 0]
        x_glu = jnp.minimum(up[:, :D_FF], SWIGLU_LIMIT)
        x_lin = jnp.clip(up[:, D_FF:], -SWIGLU_LIMIT, SWIGLU_LIMIT)
        act = x_glu * jax.nn.sigmoid(SWIGLU_ALPHA * x_glu) * (x_lin + 1.0)
        o_ref[rs, :] = _pack_bf16_pairs(
            jnp.dot(act.astype(BF16), wd_bf[...], preferred_element_type=F32) + bd_ref[0, 0])

    def zero_rows(r0, n):
        o_ref[r0:r0 + n, :] = jnp.zeros((n, o_ref.shape[1]), o_ref.dtype)

    half = ROW_SUB // 2
    for s in range(ROW_BLK // ROW_SUB):
        r0 = s * ROW_SUB
        left = nv_ref[i] - r0

        @pl.when(left > half)
        def _():
            run_rows(r0, ROW_SUB, left)

        @pl.when((left > 0) & (left <= half))
        def _():
            run_rows(r0, half, left)
            zero_rows(r0 + half, half)

        @pl.when(left <= 0)
        def _():
            zero_rows(r0, ROW_SUB)


def _experts(layer, block_e, n_valid, rows, w_up, b_up, w_down, b_down):
    n_rows, dh = rows.shape
    d = 2 * dh
    n_blocks = n_rows // ROW_BLK
    depth, ne, _, f2 = w_up.shape
    wsel = lambda i, be, nu: (layer, be[i], 0, 0)
    grid_spec = pltpu.PrefetchScalarGridSpec(
        num_scalar_prefetch=2,
        grid=(n_blocks,),
        in_specs=[pl.BlockSpec((ROW_BLK, dh), lambda i, be, nu: (i, 0)),
                  pl.BlockSpec((1, 1, d, f2), wsel),
                  pl.BlockSpec((1, 1, 1, f2), wsel),
                  pl.BlockSpec((1, 1, D_FF, d), wsel),
                  pl.BlockSpec((1, 1, 1, d), wsel)],
        out_specs=pl.BlockSpec((ROW_BLK, dh), lambda i, be, nu: (i, 0)),
        scratch_shapes=[pltpu.VMEM((d, f2), BF16), pltpu.VMEM((D_FF, d), BF16)],
    )
    return pl.pallas_call(
        _expert_kernel,
        grid_spec=grid_spec,
        out_shape=jax.ShapeDtypeStruct((n_rows, dh), jnp.uint32),
        compiler_params=_cparams(("arbitrary",)),
        name="moe_experts",
    )(block_e, n_valid, rows, w_up, b_up.reshape(depth, ne, 1, f2), w_down,
      b_down.reshape(depth, ne, 1, d))


def _combine_kernel(y0_ref, y1_ref, y2_ref, y3_ref, gate_ref, x1_ref, g2_ref, fw_ref, *rest,
                    final):
    o_ref = rest[-1]
    gates = gate_ref[...]
    y_hi, y_lo = None, None
    for k, y_ref in enumerate((y0_ref, y1_ref, y2_ref, y3_ref)):
        hi, lo = _unpack_bf16_pairs(y_ref[...])
        g = gates[:, k:k + 1]
        y_hi = g * hi if y_hi is None else y_hi + g * hi
        y_lo = g * lo if y_lo is None else y_lo + g * lo
    y = jnp.concatenate([y_hi, y_lo], axis=1)
    x2 = x1_ref[...] + g2_ref[0] * y
    if final:
        x2 = (x2 * lax.rsqrt(jnp.mean(x2 * x2, axis=-1, keepdims=True) + NORM_EPS)) * fw_ref[...]
    o_ref[...] = x2


def _combine(y4, gates, x1, mod5, b0, layer, fw, seq, final, grp, n_out, prev_out, th=512):
    t, d = x1.shape
    steps = t // th
    per_b = seq // th
    slot = lambda k: pl.BlockSpec((th, d // 2), lambda i: (k * steps + i, 0))
    return pl.pallas_call(
        functools.partial(_combine_kernel, final=final),
        grid=(steps,),
        in_specs=[slot(0), slot(1), slot(2), slot(3),
                  pl.BlockSpec((th, TOP_K), lambda i: (i, 0)),
                  pl.BlockSpec((th, d), lambda i: (i, 0)),
                  _mod_spec(layer, MOD_G2, per_b, b0),
                  pl.BlockSpec((1, d), lambda i: (0, 0))]
                 + ([] if prev_out is None else [pl.BlockSpec(memory_space=pl.ANY)]),
        out_specs=pl.BlockSpec((None, th, d), lambda i: (grp, i, 0)),
        out_shape=jax.ShapeDtypeStruct((n_out, t, d), F32),
        input_output_aliases={} if prev_out is None else {8: 0},
        compiler_params=_cparams(("arbitrary",)),
        name="moe_combine",
    )(y4, y4, y4, y4, gates, x1, mod5, fw, *([] if prev_out is None else [prev_out]))


def _retention_tables(seq):
    f32 = np.float32
    pos = np.arange(seq, dtype=f32)
    inv_freq = (f32(ROPE_BASE) ** (-np.arange(0, HEAD_DK, 2, dtype=f32) / f32(HEAD_DK))).astype(f32)
    ang = pos[:, None] * inv_freq[None, :]
    cos, sin = np.cos(ang), np.sin(ang)
    zero = np.zeros_like(sin)
    zpad = np.zeros((seq, LANES - 2 * HEAD_DK), f32)
    cos_t = np.concatenate([cos, cos, cos, cos, zpad], axis=1)
    sina_t = np.concatenate([-sin, zero, -sin, zero, zpad], axis=1)
    sinb_t = np.concatenate([zero, sin, zero, sin, zpad], axis=1)
    log_gamma = np.log1p(-np.exp2(f32(-5.0) - np.arange(N_HEADS, dtype=f32))).astype(f32)
    log_decay = np.broadcast_to(log_gamma[None, :, None], (RET_CHUNK, N_HEADS, HEAD_DK))
    cum = np.cumsum(log_decay, axis=0, dtype=f32)
    tot = cum[-1:]

    def shp(a):
        flat = a.reshape(a.shape[0], N_HEADS * HEAD_DK)
        return np.where(_DK_SRC >= 0, flat[:, np.maximum(_DK_SRC, 0)], f32(1.0)).astype(f32)

    tabs = (cos_t, sina_t, sinb_t, shp(np.exp(cum)), shp(np.exp(-cum)), shp(np.exp(tot - cum)),
            shp(np.exp(tot)))
    return tuple(jnp.asarray(a, F32) for a in tabs)


def kernel(x, c, norm1_w, norm2_w, w_mod, b_mod, w_in, gla_w_a2, gla_b_a, gla_norm_w, ret_norm_w, s5_a_re, s5_a_im, s5_log_dt, s5_b_re, s5_b_im, s5_c_re, s5_c_im, s5_d, s5_w_glu, s5_b_glu, w_out, router_w, router_b, w_up, b_up, w_down, b_down, final_norm_w):
    batch, seq, d = x.shape
    depth = w_mod.shape[0]
    gb = batch // N_STREAMS
    t = gb * seq
    n_slots = t * TOP_K
    n_blocks = n_slots // ROW_BLK + N_EXPERTS
    n_rows = n_blocks * ROW_BLK

    mod = _modulation(c, w_mod, b_mod)
    mod5 = mod.reshape(depth, batch, 6, 1, d).transpose(0, 2, 1, 3, 4)
    ret_tabs = _retention_tables(seq)

    w_p = _take_cols(w_in, _IN_SRC).astype(BF16)
    wa_p = jnp.zeros((depth, LANES, QKP), F32).at[:, :GATE_RANK].set(
        _take_cols(gla_w_a2, _DK_SRC)).astype(BF16)
    ba_p = _take_cols(gla_b_a, _DK_SRC).reshape(depth, 1, QKP)
    gnw = _take_cols(gla_norm_w, _DV_SRC).reshape(depth, 1, HP)
    rnw = _take_cols(ret_norm_w, _DV_SRC).reshape(depth, 1, HP)
    kv = N_HEADS * HEAD_DV
    wo_g = _take_rows(w_out[:, :kv], _DV_SRC).astype(BF16)
    wo_r = _take_rows(w_out[:, kv:2 * kv], _DV_SRC).astype(BF16)
    wo_s = w_out[:, 2 * kv:].astype(BF16)
    rw_t = jnp.swapaxes(router_w, 1, 2)
    rw_hi = rw_t.astype(BF16)
    rw_p = jnp.concatenate([rw_hi, (rw_t - rw_hi.astype(F32)).astype(BF16)], axis=1)
    rb_p = jnp.broadcast_to(router_b[:, :, None], (depth, N_EXPERTS, LANES))
    s5_tabs = jax.vmap(_s5_tables)(s5_a_re, s5_a_im, s5_log_dt, s5_b_re, s5_b_im, s5_c_re,
                                   s5_c_im, s5_d, s5_w_glu, s5_b_glu)
    n1 = norm1_w.reshape(depth, 1, d)
    n2 = norm2_w.reshape(depth, 1, d)
    fw = final_norm_w.reshape(1, d)

    xs = [(x.reshape(N_STREAMS, t, d), g) for g in range(N_STREAMS)]
    out = None

    for i in range(depth):
        final = i == depth - 1
        for g in range(N_STREAMS):
            x3, grp = xs[g]
            b0 = g * gb

            proj = _in_projection(x3, grp, mod5, b0, i, n1, w_p, seq)
            o_gla = _gla_mixer(proj, gb, seq, i, wa_p, ba_p, gnw)
            o_ret = _ret_mixer(proj, gb, seq, i, *ret_tabs, rnw)
            o_s5 = _s5_mixer(proj, gb, seq, i, s5_tabs)

            x1, hdn, idx, gates, rank, counts = _outproj_router(
                o_gla, o_ret, o_s5, wo_g, wo_r, wo_s, x3, grp, mod5, b0, i, n2, rw_p, rb_p, seq)

            cnt = counts[:, 0].astype(jnp.int32)
            padded = (cnt + ROW_BLK - 1) // ROW_BLK * ROW_BLK
            pad_ends = jnp.cumsum(padded)
            pad_starts = pad_ends - padded
            slot_start = jnp.sum(jnp.where(idx[:TOP_K, :, None] == jnp.arange(N_EXPERTS),
                                           pad_starts, 0), axis=-1)
            dest_sm = (slot_start + rank[:TOP_K]).astype(jnp.int32).reshape(-1)
            gates_tm = gates[:TOP_K].T
            blk_start = jnp.arange(n_blocks, dtype=jnp.int32) * ROW_BLK
            block_e = jnp.minimum(jnp.sum(pad_ends[None, :] <= blk_start[:, None], axis=1),
                                  N_EXPERTS - 1).astype(jnp.int32)
            n_valid = jnp.clip((pad_starts + cnt)[block_e] - blk_start, 0,
                               ROW_BLK).astype(jnp.int32)

            rows = _scatter_rows(hdn, dest_sm, n_rows)
            out_rows = _experts(i, block_e, n_valid, rows, w_up, b_up, w_down, b_down)
            y4 = _gather_rows(out_rows, dest_sm)
            if final:
                out = _combine(y4, gates_tm, x1, mod5, b0, i, fw, seq, True, g, N_STREAMS, out)
            else:
                xs[g] = (_combine(y4, gates_tm, x1, mod5, b0, i, fw, seq, False, 0, 1, None), 0)

    return out.reshape(batch, seq, d)
```

```python
import functools

import numpy as np
import jax
import jax.numpy as jnp
from jax import lax
from jax.experimental import pallas as pl
from jax.experimental.pallas import tpu as pltpu
from jax.experimental.pallas import tpu_sc as plsc

D_MODEL = 1024
CHUNK = 64
RET_CHUNK = 128
NORM_EPS = 1e-5
N_HEADS = 4
HEAD_DK = 48
HEAD_DV = 96
GATE_RANK = 16
GATE_TEMP = 16.0
ROPE_BASE = 10000.0
S5_WIDTH = 256
S5_GROUP_DIM = 16
S5_GROUPS = 16
S5_STATE = 64
N_EXPERTS = 32
TOP_K = 4
D_FF = 1024
SWIGLU_LIMIT = 7.0
SWIGLU_ALPHA = 1.702

LANES = 128
HEAD_PAD = LANES
HP = N_HEADS * HEAD_PAD
N_PAIRS = N_HEADS // 2
QKP = N_PAIRS * LANES
VMEM_LIMIT = 56 * 1024 * 1024

OFF_GQ, OFF_GK, OFF_GV, OFF_GG = 0, QKP, 2 * QKP, 2 * QKP + HP
OFF_RQ = OFF_GG + HP
OFF_RK, OFF_RV, OFF_RG = OFF_RQ + QKP, OFF_RQ + 2 * QKP, OFF_RQ + 2 * QKP + HP
OFF_SU = OFF_RG + HP
OFF_GA = OFF_SU + S5_WIDTH
NP_COLS = OFF_GA + LANES
PROJ_CH = 1152

ROW_BLK = 1024
ROW_SUB = 512
MOD_SH1, MOD_SC1, MOD_G1, MOD_SH2, MOD_SC2, MOD_G2 = range(6)
N_STREAMS = 1

F32 = jnp.float32
BF16 = jnp.bfloat16


def _DK_SRC_LANE(h, d):
    return (h // 2) * LANES + (h % 2) * HEAD_DK + d


def _in_col_map():
    src = -np.ones((NP_COLS,), np.int64)
    kq = N_HEADS * HEAD_DK
    kv = N_HEADS * HEAD_DV
    base = dict(gq=0, gk=kq, gv=2 * kq, gg=2 * kq + kv, ga=2 * kq + 2 * kv)
    r0 = base['ga'] + GATE_RANK
    base.update(rq=r0, rk=r0 + kq, rv=r0 + 2 * kq, rg=r0 + 2 * kq + kv, su=r0 + 2 * kq + 2 * kv)
    for h in range(N_HEADS):
        for d in range(HEAD_DK):
            lane = _DK_SRC_LANE(h, d)
            src[OFF_GQ + lane] = base['gq'] + h * HEAD_DK + d
            src[OFF_GK + lane] = base['gk'] + h * HEAD_DK + d
            src[OFF_RQ + lane] = base['rq'] + h * HEAD_DK + d
            src[OFF_RK + lane] = base['rk'] + h * HEAD_DK + d
        for d in range(HEAD_DV):
            src[OFF_GV + h * HEAD_PAD + d] = base['gv'] + h * HEAD_DV + d
            src[OFF_GG + h * HEAD_PAD + d] = base['gg'] + h * HEAD_DV + d
            src[OFF_RV + h * HEAD_PAD + d] = base['rv'] + h * HEAD_DV + d
            src[OFF_RG + h * HEAD_PAD + d] = base['rg'] + h * HEAD_DV + d
    src[OFF_SU:OFF_SU + S5_WIDTH] = base['su'] + np.arange(S5_WIDTH)
    src[OFF_GA:OFF_GA + GATE_RANK] = base['ga'] + np.arange(GATE_RANK)
    return src


_IN_SRC = _in_col_map()


def _head_pad_map(width):
    src = -np.ones((HP,), np.int64)
    for h in range(N_HEADS):
        src[h * HEAD_PAD:h * HEAD_PAD + width] = h * width + np.arange(width)
    return src


_DV_SRC = _head_pad_map(HEAD_DV)
_DK_SRC = -np.ones((QKP,), np.int64)
for _h in range(N_HEADS):
    for _d in range(HEAD_DK):
        _DK_SRC[_DK_SRC_LANE(_h, _d)] = _h * HEAD_DK + _d


def _take_static(w, src, axis):
    axis = axis % w.ndim
    pieces, start = [], 0
    for j in range(1, len(src) + 1):
        run_ends = (j == len(src) or (src[j] < 0) != (src[start] < 0)
                    or (src[start] >= 0 and src[j] != src[j - 1] + 1))
        if run_ends:
            if src[start] < 0:
                shape = w.shape[:axis] + (j - start,) + w.shape[axis + 1:]
                pieces.append(jnp.zeros(shape, w.dtype))
            else:
                pieces.append(lax.slice_in_dim(w, int(src[start]), int(src[j - 1]) + 1, axis=axis))
            start = j
    return jnp.concatenate(pieces, axis=axis)


def _take_cols(w, src):
    return _take_static(w, src, -1)


def _take_rows(w, src):
    return _take_static(w, src, -2)


def _layer_spec(shape, layer):
    return pl.BlockSpec((None,) + tuple(shape), lambda *_: (layer,) + (0,) * len(shape))


def _mod_spec(layer, which, per_b, b0):
    return pl.BlockSpec((None, None, 1, 1, D_MODEL),
                        lambda i: (layer, which, b0 + i // per_b, 0, 0))


def _cparams(sem):
    return pltpu.CompilerParams(dimension_semantics=sem, vmem_limit_bytes=VMEM_LIMIT)


def _mod_kernel(c_ref, w_ref, b_ref, o_ref):
    c = c_ref[...]
    cond = c * jax.nn.sigmoid(c)
    o_ref[0] = jnp.dot(cond, w_ref[0], preferred_element_type=F32,
                       precision=lax.Precision.HIGHEST) + b_ref[0]


def _modulation(c, w_mod, b_mod):
    depth, d, n = w_mod.shape
    b = c.shape[0]
    nb = 1536
    return pl.pallas_call(
        _mod_kernel,
        grid=(depth, n // nb),
        in_specs=[pl.BlockSpec((b, d), lambda l, j: (0, 0)),
                  pl.BlockSpec((1, d, nb), lambda l, j: (l, 0, j)),
                  pl.BlockSpec((1, 1, nb), lambda l, j: (l, 0, j))],
        out_specs=pl.BlockSpec((1, b, nb), lambda l, j: (l, 0, j)),
        out_shape=jax.ShapeDtypeStruct((depth, b, n), F32),
        compiler_params=_cparams(("arbitrary", "arbitrary")),
        name="adaln_mod",
    )(c, w_mod, b_mod.reshape(depth, 1, n))


def _rms_mod(x, nw, sc, sh):
    y = x * lax.rsqrt(jnp.mean(x * x, axis=-1, keepdims=True) + NORM_EPS)
    return (y * nw) * (1.0 + sc) + sh


def _inproj_kernel(x_ref, sc_ref, sh_ref, nw_ref, w_ref, o_ref):
    h = _rms_mod(x_ref[...], nw_ref[...], sc_ref[0], sh_ref[0]).astype(BF16)
    for j in range(NP_COLS // PROJ_CH):
        cs = slice(j * PROJ_CH, (j + 1) * PROJ_CH)
        o_ref[:, cs] = jnp.dot(h, w_ref[:, cs], preferred_element_type=F32).astype(BF16)


def _in_projection(x3, grp, mod5, b0, layer, nw, w_p, seq, tm=1024):
    _, t, d = x3.shape
    tm = min(tm, seq)
    per_b = seq // tm
    return pl.pallas_call(
        _inproj_kernel,
        grid=(t // tm,),
        in_specs=[pl.BlockSpec((None, tm, d), lambda i: (grp, i, 0)),
                  _mod_spec(layer, MOD_SC1, per_b, b0), _mod_spec(layer, MOD_SH1, per_b, b0),
                  _layer_spec((1, d), layer), _layer_spec((d, NP_COLS), layer)],
        out_specs=pl.BlockSpec((tm, NP_COLS), lambda i: (i, 0)),
        out_shape=jax.ShapeDtypeStruct((t, NP_COLS), BF16),
        compiler_params=_cparams(("arbitrary",)),
        name="in_proj",
    )(x3, mod5, mod5, nw, w_p)


_NT = (((1,), (1,)), ((), ()))
_TN = (((0,), (0,)), ((), ()))


def _tri_mask(n=CHUNK):
    r = lax.broadcasted_iota(jnp.int32, (n, n), 0)
    c = lax.broadcasted_iota(jnp.int32, (n, n), 1)
    return r >= c


def _pair_masks(rows):
    lane = lax.broadcasted_iota(jnp.int32, (rows, LANES), 1)
    return lane < HEAD_DK, (lane >= HEAD_DK) & (lane < 2 * HEAD_DK)


def _head_attention(qd, ki, ke, vh, et, st_ref, h, causal):
    qb = qd.astype(BF16)
    sc = lax.dot_general(qb, ki.astype(BF16), _NT, preferred_element_type=F32)
    sc = jnp.where(causal, sc, 0.0)
    st = st_ref[h]
    o = jnp.dot(sc.astype(BF16), vh, preferred_element_type=F32)
    o = o + lax.dot_general(qb, st.astype(BF16), _NT, preferred_element_type=F32)
    st_ref[h] = st * et + lax.dot_general(vh, ke.astype(BF16), _TN, preferred_element_type=F32)
    return o


def _gla_kernel(q_ref, k_ref, v_ref, g_ref, a_ref, wa_ref, ba_ref, nw_ref, tri_ref, o_ref,
                st_ref, qd_s, ki_s, ke_s, et_s):
    @pl.when(pl.program_id(1) == 0)
    def _():
        st_ref[...] = jnp.zeros_like(st_ref)

    causal = _tri_mask()
    tl = q_ref.shape[0]
    n_chunks = tl // CHUNK

    z = jnp.dot(a_ref[...], wa_ref[...], preferred_element_type=F32) + ba_ref[...]
    la = (jnp.minimum(z, 0.0) - jnp.log1p(jnp.exp(-jnp.abs(z)))) * (1.0 / GATE_TEMP)
    hi = la.astype(BF16)
    lo = (la - hi.astype(F32)).astype(BF16)
    cum = (jnp.dot(tri_ref[...], hi, preferred_element_type=F32)
           + jnp.dot(tri_ref[...], lo, preferred_element_type=F32))
    cum3 = cum.reshape(n_chunks, CHUNK, QKP)
    tot3 = cum3[:, CHUNK - 1:CHUNK, :]
    qd = (q_ref[...].astype(F32) * (HEAD_DK ** -0.5)) * jnp.exp(cum)
    masks = _pair_masks(tl)
    for h in range(N_HEADS):
        pair = slice((h // 2) * LANES, (h // 2 + 1) * LANES)
        qd_s[:, h * HEAD_PAD:(h + 1) * HEAD_PAD] = jnp.where(masks[h % 2], qd[:, pair],
                                                              0.0).astype(BF16)
    kf = k_ref[...].astype(F32)
    ki_s[...] = (kf * jnp.exp(-cum)).astype(BF16)
    ke_s[...] = (kf * jnp.exp(tot3 - cum3).reshape(tl, QKP)).astype(BF16)
    et_s[...] = jnp.exp(tot3).reshape(n_chunks, QKP)

    def chunk(c, carry):
        r = pl.ds(pl.multiple_of(c * CHUNK, CHUNK), CHUNK)
        et = et_s[pl.ds(c, 1), :]
        for h in range(N_HEADS):
            sl = slice(h * HEAD_PAD, (h + 1) * HEAD_PAD)
            pair = slice((h // 2) * LANES, (h // 2 + 1) * LANES)
            o = _head_attention(qd_s[r, sl], ki_s[r, pair], ke_s[r, pair], v_ref[r, sl],
                                et[:, pair], st_ref, h, causal)
            ms = jnp.sum(o * o, axis=-1, keepdims=True) * (1.0 / HEAD_DV)
            y = (o * lax.rsqrt(ms + NORM_EPS)) * nw_ref[:, sl]
            g = g_ref[r, sl].astype(F32)
            o_ref[r, sl] = (y * (g * jax.nn.sigmoid(g))).astype(BF16)
        return carry

    lax.fori_loop(0, n_chunks, chunk, 0, unroll=4)


def _ret_kernel(q_ref, k_ref, v_ref, g_ref, cos_ref, sina_ref, sinb_ref, dq_ref, dki_ref,
                dke_ref, dt_ref, nw_ref, o_ref, st_ref):
    @pl.when(pl.program_id(1) == 0)
    def _():
        st_ref[...] = jnp.zeros_like(st_ref)

    causal = _tri_mask(RET_CHUNK)
    n_chunks = q_ref.shape[0] // RET_CHUNK
    lane = lax.broadcasted_iota(jnp.int32, (RET_CHUNK, HEAD_PAD), 1)
    real = lane < HEAD_DV
    masks = _pair_masks(RET_CHUNK)
    half = HEAD_DK // 2

    def rotary(t, cos, sina, sinb):
        return (t * cos + pltpu.roll(t, LANES - half, 1) * sina + pltpu.roll(t, half, 1) * sinb)

    def chunk(c, carry):
        r = pl.ds(pl.multiple_of(c * RET_CHUNK, RET_CHUNK), RET_CHUNK)
        cos, sina, sinb = cos_ref[r, :], sina_ref[r, :], sinb_ref[r, :]
        pair_q, pair_ki, pair_ke = [], [], []
        for p in range(N_PAIRS):
            ps = slice(p * LANES, (p + 1) * LANES)
            qr = rotary(q_ref[r, ps].astype(F32), cos, sina, sinb) * dq_ref[:, ps]
            kr = rotary(k_ref[r, ps].astype(F32), cos, sina, sinb) * (HEAD_DK ** -0.5)
            pair_q.append(qr)
            pair_ki.append((kr * dki_ref[:, ps]).astype(BF16))
            pair_ke.append((kr * dke_ref[:, ps]).astype(BF16))
        for h in range(N_HEADS):
            sl = slice(h * HEAD_PAD, (h + 1) * HEAD_PAD)
            p = h // 2
            qd = jnp.where(masks[h % 2], pair_q[p], 0.0)
            o = _head_attention(qd, pair_ki[p], pair_ke[p], v_ref[r, sl],
                                dt_ref[:, p * LANES:(p + 1) * LANES], st_ref, h, causal)
            mu = jnp.sum(o, axis=-1, keepdims=True) * (1.0 / HEAD_DV)
            oc = jnp.where(real, o - mu, 0.0)
            var = jnp.sum(oc * oc, axis=-1, keepdims=True) * (1.0 / HEAD_DV)
            y = (oc * lax.rsqrt(var + NORM_EPS)) * nw_ref[:, sl]
            g = g_ref[r, sl].astype(F32)
            o_ref[r, sl] = (y * (g * jax.nn.sigmoid(g))).astype(BF16)
        return carry

    lax.fori_loop(0, n_chunks, chunk, 0, unroll=2)


def _proj_spec(tl, width, col_off, per_b):
    cb = col_off // width
    return pl.BlockSpec((tl, width), lambda b, l: (b * per_b + l, cb))


def _full(shape):
    return pl.BlockSpec(shape, lambda b, l: (0,) * len(shape))


def _gla_mixer(proj, batch, seq, layer, wa_p, ba_p, nw_p, tl=512):
    per_b = seq // tl
    pos = np.arange(tl)
    tri_bd = jnp.asarray((pos[:, None] // CHUNK == pos[None, :] // CHUNK)
                         & (pos[:, None] >= pos[None, :]), BF16)
    return pl.pallas_call(
        _gla_kernel,
        grid=(batch, per_b),
        in_specs=[_proj_spec(tl, QKP, OFF_GQ, per_b), _proj_spec(tl, QKP, OFF_GK, per_b),
                  _proj_spec(tl, HP, OFF_GV, per_b), _proj_spec(tl, HP, OFF_GG, per_b),
                  _proj_spec(tl, LANES, OFF_GA, per_b),
                  _layer_spec((LANES, QKP), layer), _layer_spec((1, QKP), layer),
                  _layer_spec((1, HP), layer), _full((tl, tl))],
        out_specs=pl.BlockSpec((tl, HP), lambda b, l: (b * per_b + l, 0)),
        out_shape=jax.ShapeDtypeStruct((batch * seq, HP), BF16),
        scratch_shapes=[pltpu.VMEM((N_HEADS, HEAD_PAD, HEAD_PAD), F32),
                        pltpu.VMEM((tl, HP), BF16), pltpu.VMEM((tl, QKP), BF16),
                        pltpu.VMEM((tl, QKP), BF16), pltpu.VMEM((tl // CHUNK, QKP), F32)],
        compiler_params=_cparams(("arbitrary", "arbitrary")),
        name="gla_mixer",
    )(proj, proj, proj, proj, proj, wa_p, ba_p, nw_p, tri_bd)


def _ret_mixer(proj, batch, seq, layer, cos_t, sina_t, sinb_t, dq, dki, dke, dtot, nw_p, tl=512):
    per_b = seq // tl
    return pl.pallas_call(
        _ret_kernel,
        grid=(batch, per_b),
        in_specs=[_proj_spec(tl, QKP, OFF_RQ, per_b), _proj_spec(tl, QKP, OFF_RK, per_b),
                  _proj_spec(tl, HP, OFF_RV, per_b), _proj_spec(tl, HP, OFF_RG, per_b),
                  pl.BlockSpec((tl, LANES), lambda b, l: (l, 0)),
                  pl.BlockSpec((tl, LANES), lambda b, l: (l, 0)),
                  pl.BlockSpec((tl, LANES), lambda b, l: (l, 0)),
                  _full((RET_CHUNK, QKP)), _full((RET_CHUNK, QKP)), _full((RET_CHUNK, QKP)),
                  _full((1, QKP)), _layer_spec((1, HP), layer)],
        out_specs=pl.BlockSpec((tl, HP), lambda b, l: (b * per_b + l, 0)),
        out_shape=jax.ShapeDtypeStruct((batch * seq, HP), BF16),
        scratch_shapes=[pltpu.VMEM((N_HEADS, HEAD_PAD, HEAD_PAD), F32)],
        compiler_params=_cparams(("arbitrary", "arbitrary")),
        name="ret_mixer",
    )(proj, proj, proj, proj, cos_t, sina_t, sinb_t, dq, dki, dke, dtot, nw_p)


S5_N = S5_GROUPS * S5_STATE
S5_SLAB = 256


def _gelu_tanh(x):
    return 0.5 * x * (1.0 + jnp.tanh(np.sqrt(2.0 / np.pi) * (x + 0.044715 * (x * x * x))))


def _s5_kernel(u_ref, bb_ref, cb_ref, nr_ref, ni_ref, pr_ref, pi_ref, lr_ref, li_ref,
               d_ref, wg_ref, bg_ref, o_ref, sr_ref, si_ref, x_scr, s_scr):
    @pl.when(pl.program_id(1) == 0)
    def _():
        sr_ref[...] = jnp.zeros_like(sr_ref)
        si_ref[...] = jnp.zeros_like(si_ref)

    tri = _tri_mask().astype(BF16)
    n_chunks = u_ref.shape[0] // CHUNK
    u = u_ref[...]
    x_scr[...] = jnp.dot(u, bb_ref[...], preferred_element_type=F32)

    def chunk(c, carry):
        r = pl.ds(pl.multiple_of(c * CHUNK, CHUNK), CHUNK)
        for j in range(S5_N // S5_SLAB):
            cs = slice(j * S5_SLAB, (j + 1) * S5_SLAB)
            ci = slice(S5_N + j * S5_SLAB, S5_N + (j + 1) * S5_SLAB)
            xr, xi = x_scr[r, cs], x_scr[r, ci]
            nr, ni = nr_ref[:, cs], ni_ref[:, cs]
            p_r = jnp.dot(tri, (xr * nr - xi * ni).astype(BF16), preferred_element_type=F32)
            p_i = jnp.dot(tri, (xr * ni + xi * nr).astype(BF16), preferred_element_type=F32)
            s0r, s0i = sr_ref[:, cs], si_ref[:, cs]
            lr, li = lr_ref[:, cs], li_ref[:, cs]
            q_r = p_r + (s0r * lr - s0i * li)
            q_i = p_i + (s0r * li + s0i * lr)
            pr, pi = pr_ref[:, cs], pi_ref[:, cs]
            s_r = q_r * pr - q_i * pi
            s_i = q_r * pi + q_i * pr
            sr_ref[:, cs] = s_r[CHUNK - 1:CHUNK, :]
            si_ref[:, cs] = s_i[CHUNK - 1:CHUNK, :]
            s_scr[r, cs] = s_r.astype(BF16)
            s_scr[r, ci] = s_i.astype(BF16)
        return carry

    lax.fori_loop(0, n_chunks, chunk, 0, unroll=2)
    y = jnp.dot(s_scr[...], cb_ref[...], preferred_element_type=F32)
    y = _gelu_tanh(y + d_ref[...] * u.astype(F32))
    gate = jnp.dot(y.astype(BF16), wg_ref[...], preferred_element_type=F32) + bg_ref[...]
    o_ref[...] = (y * jax.nn.sigmoid(gate)).astype(BF16)


def _s5_mixer(proj, batch, seq, layer, tabs, tl=512):
    per_b = seq // tl
    bb, cb, nr, ni, pr, pi, lr, li, dsk, wg, bg = tabs
    return pl.pallas_call(
        _s5_kernel,
        grid=(batch, per_b),
        in_specs=[_proj_spec(tl, S5_WIDTH, OFF_SU, per_b),
                  *[_layer_spec(a.shape[1:], layer) for a in tabs]],
        out_specs=pl.BlockSpec((tl, S5_WIDTH), lambda b, l: (b * per_b + l, 0)),
        out_shape=jax.ShapeDtypeStruct((batch * seq, S5_WIDTH), BF16),
        scratch_shapes=[pltpu.VMEM((1, S5_N), F32), pltpu.VMEM((1, S5_N), F32),
                        pltpu.VMEM((tl, 2 * S5_N), F32), pltpu.VMEM((tl, 2 * S5_N), BF16)],
        compiler_params=_cparams(("arbitrary", "arbitrary")),
        name="s5_mixer",
    )(proj, bb, cb, nr, ni, pr, pi, lr, li, dsk, wg, bg)


def _s5_tables(a_re, a_im, log_dt, b_re, b_im, c_re, c_im, d_skip, w_glu, b_glu):
    lam = lax.complex(a_re, a_im)
    dt = jnp.exp(log_dt)[:, None]
    lam_bar = jnp.exp(lam * dt)
    b_bar = ((lam_bar - 1.0) / lam)[..., None] * lax.complex(b_re, b_im)
    eye = jnp.eye(S5_GROUPS, dtype=F32)
    def blk_b(m):
        return jnp.einsum('gph,gk->ghkp', m, eye).reshape(S5_WIDTH, S5_N)
    bb = jnp.concatenate([blk_b(jnp.real(b_bar)), blk_b(jnp.imag(b_bar))], axis=1)
    def blk_c(m):
        return jnp.einsum('ghp,gk->kpgh', m, eye).reshape(S5_N, S5_WIDTH)
    cb = jnp.concatenate([blk_c(c_re), blk_c(-c_im)], axis=0)
    steps = jnp.arange(CHUNK, dtype=F32)[:, None, None]
    lam_dt = (lam * dt)[None]
    pos = jnp.exp(lam_dt * steps).reshape(CHUNK, S5_N)
    neg = jnp.exp(-lam_dt * steps).reshape(CHUNK, S5_N)
    one = lam_bar.reshape(1, S5_N)
    return (bb.astype(BF16), cb.astype(BF16), jnp.real(neg), jnp.imag(neg), jnp.real(pos),
            jnp.imag(pos), jnp.real(one), jnp.imag(one), d_skip.reshape(1, S5_WIDTH),
            w_glu.astype(BF16), b_glu.reshape(1, S5_WIDTH))


def _pack_bf16_pairs(x):
    w = x.shape[1] // 2
    xb = x.astype(BF16).astype(F32)
    hi = lax.bitcast_convert_type(xb[:, :w], jnp.uint32)
    lo = lax.bitcast_convert_type(xb[:, w:], jnp.uint32)
    return hi | (lo >> 16)


def _unpack_bf16_pairs(p):
    hi = lax.bitcast_convert_type(p & jnp.uint32(0xFFFF0000), F32)
    lo = lax.bitcast_convert_type(p << 16, F32)
    return hi, lo


def _router_kernel(og_ref, or_ref, os_ref, wg_ref, wr_ref, ws_ref, x_ref, g1_ref, sc_ref,
                   sh_ref, nw_ref, rw_ref, rb_ref,
                   x1_ref, h_ref, idx_ref, gate_ref, rank_ref, cnt_ref, carry_ref):
    i = pl.program_id(0)

    @pl.when(i == 0)
    def _():
        carry_ref[...] = jnp.zeros_like(carry_ref)

    mix = (jnp.dot(og_ref[...], wg_ref[...], preferred_element_type=F32)
           + jnp.dot(or_ref[...], wr_ref[...], preferred_element_type=F32)
           + jnp.dot(os_ref[...], ws_ref[...], preferred_element_type=F32))
    x1 = x_ref[...] + g1_ref[0] * mix
    x1_ref[...] = x1
    hdn = _rms_mod(x1, nw_ref[...], sc_ref[0], sh_ref[0])
    h_ref[...] = _pack_bf16_pairs(hdn)
    h_hi = hdn.astype(BF16)
    h_lo = (hdn - h_hi.astype(F32)).astype(BF16)
    p = lax.dot_general(rw_ref[...], h_hi, _NT, preferred_element_type=F32)
    q = lax.dot_general(rw_ref[:N_EXPERTS, :], h_lo, _NT, preferred_element_type=F32)
    logits = (p[:N_EXPERTS] + p[N_EXPERTS:] + q) + rb_ref[:, 0:1]
    tm = logits.shape[1]
    eidx = lax.broadcasted_iota(jnp.int32, (N_EXPERTS, tm), 0)
    work = logits
    onehot = jnp.zeros((N_EXPERTS, tm), F32)
    vals, idxs, sels = [], [], []
    for _ in range(TOP_K):
        m = jnp.max(work, axis=0, keepdims=True)
        ix = jnp.min(jnp.where(work == m, eidx, N_EXPERTS), axis=0, keepdims=True)
        sel = eidx == ix
        work = jnp.where(sel, -jnp.inf, work)
        onehot = onehot + sel.astype(F32)
        vals.append(m)
        idxs.append(ix)
        sels.append(sel)
    exps = [jnp.exp(v - vals[0]) for v in vals]
    denom = exps[0] + exps[1] + exps[2] + exps[3]
    r = lax.broadcasted_iota(jnp.int32, (tm, tm), 0)
    c = lax.broadcasted_iota(jnp.int32, (tm, tm), 1)
    earlier = (r < c).astype(BF16)
    before = (jnp.dot(onehot.astype(BF16), earlier, preferred_element_type=F32)
              + carry_ref[:, 0:1])
    row8 = lax.broadcasted_iota(jnp.int32, (8, tm), 0)
    idx_out = jnp.zeros((8, tm), jnp.int32)
    gate_out = jnp.zeros((8, tm), F32)
    rank_out = jnp.zeros((8, tm), F32)
    for k in range(TOP_K):
        rk = jnp.sum(jnp.where(sels[k], before, 0.0), axis=0, keepdims=True)
        idx_out = jnp.where(row8 == k, idxs[k], idx_out)
        gate_out = jnp.where(row8 == k, exps[k] / denom, gate_out)
        rank_out = jnp.where(row8 == k, rk, rank_out)
    idx_ref[...] = idx_out
    gate_ref[...] = gate_out
    rank_ref[...] = rank_out.astype(jnp.int32)
    total = carry_ref[...] + jnp.sum(onehot, axis=1, keepdims=True)
    carry_ref[...] = total
    cnt_ref[...] = total


def _outproj_router(o_gla, o_ret, o_s5, wg, wr, ws, x3, grp, mod5, b0, layer, nw2, rw_p, rb_p,
                    seq, tm=1024):
    _, t, d = x3.shape
    tm = min(tm, seq)
    per_b = seq // tm
    row = lambda w: pl.BlockSpec((tm, w), lambda i: (i, 0))
    full = lambda s: pl.BlockSpec(s, lambda i: (0,) * len(s))
    slot_t = pl.BlockSpec((8, tm), lambda i: (0, i))
    return pl.pallas_call(
        _router_kernel,
        grid=(t // tm,),
        in_specs=[row(HP), row(HP), row(S5_WIDTH), _layer_spec((HP, d), layer),
                  _layer_spec((HP, d), layer), _layer_spec((S5_WIDTH, d), layer),
                  pl.BlockSpec((None, tm, d), lambda i: (grp, i, 0)),
                  _mod_spec(layer, MOD_G1, per_b, b0), _mod_spec(layer, MOD_SC2, per_b, b0),
                  _mod_spec(layer, MOD_SH2, per_b, b0), _layer_spec((1, d), layer),
                  _layer_spec((2 * N_EXPERTS, d), layer), _layer_spec((N_EXPERTS, LANES), layer)],
        out_specs=[row(d), row(d // 2), slot_t, slot_t, slot_t, full((N_EXPERTS, LANES))],
        out_shape=[jax.ShapeDtypeStruct((t, d), F32),
                   jax.ShapeDtypeStruct((t, d // 2), jnp.uint32),
                   jax.ShapeDtypeStruct((8, t), jnp.int32),
                   jax.ShapeDtypeStruct((8, t), F32),
                   jax.ShapeDtypeStruct((8, t), jnp.int32),
                   jax.ShapeDtypeStruct((N_EXPERTS, LANES), F32)],
        scratch_shapes=[pltpu.VMEM((N_EXPERTS, LANES), F32)],
        compiler_params=_cparams(("arbitrary",)),
        name="outproj_router",
    )(o_gla, o_ret, o_s5, wg, wr, ws, x3, mod5, mod5, mod5, nw2, rw_p, rb_p)


GATHER_WIN = 64


def _gather_rows(table, idx):
    m = idx.shape[0]
    w = table.shape[1]
    mesh = plsc.VectorSubcoreMesh(core_axis_name="core", subcore_axis_name="subcore")

    @functools.partial(pl.kernel, out_type=jax.ShapeDtypeStruct((m, w), table.dtype),
                       mesh=mesh, name="sc_row_gather")
    def gather(x_hbm, i_hbm, o_hbm):
        def body(i_vmem, o_vmem):
            pltpu.sync_copy(x_hbm.at[i_vmem], o_vmem)

        pltpu.emit_pipeline(
            body,
            grid=(m // GATHER_WIN,),
            in_specs=[pl.BlockSpec((GATHER_WIN,), lambda i: (i,))],
            out_specs=[pl.BlockSpec((GATHER_WIN, w), lambda i: (i, 0))],
            core_axis_name=("core", "subcore"),
            dimension_semantics=(pltpu.PARALLEL,),
        )(i_hbm, o_hbm)

    return gather(table, idx)


def _scatter_rows(x, dest_slot_major, n_rows):
    t, w = x.shape
    steps = t // GATHER_WIN
    mesh = plsc.VectorSubcoreMesh(core_axis_name="core", subcore_axis_name="subcore")

    @functools.partial(pl.kernel, out_type=jax.ShapeDtypeStruct((n_rows, w), x.dtype),
                       mesh=mesh, name="sc_row_scatter")
    def scatter(x_hbm, i_hbm, o_hbm):
        def body(x_vmem, i0, i1, i2, i3):
            for i_vmem in (i0, i1, i2, i3):
                pltpu.sync_copy(x_vmem, o_hbm.at[i_vmem])

        slot = lambda k: pl.BlockSpec((GATHER_WIN,), lambda i: (k * steps + i,))
        pltpu.emit_pipeline(
            body,
            grid=(steps,),
            in_specs=[pl.BlockSpec((GATHER_WIN, w), lambda i: (i, 0)),
                      slot(0), slot(1), slot(2), slot(3)],
            out_specs=[],
            core_axis_name=("core", "subcore"),
            dimension_semantics=(pltpu.PARALLEL,),
        )(x_hbm, i_hbm, i_hbm, i_hbm, i_hbm)

    return scatter(x, dest_slot_major)


def _expert_kernel(be_ref, nv_ref, rows_ref, wu_ref, bu_ref, wd_ref, bd_ref, o_ref,
                   wu_bf, wd_bf):
    i = pl.program_id(0)
    e = be_ref[i]
    prev = be_ref[jnp.maximum(i - 1, 0)]

    @pl.when((i == 0) | (e != prev))
    def _():
        wu_bf[...] = wu_ref[0, 0].astype(BF16)
        wd_bf[...] = wd_ref[0, 0].astype(BF16)

    def run_rows(r0, n, left):
        rs = slice(r0, r0 + n)
        row = lax.broadcasted_iota(jnp.int32, (n, rows_ref.shape[1]), 0)
        x_hi, x_lo = _unpack_bf16_pairs(jnp.where(row < left, rows_ref[rs, :], jnp.uint32(0)))
        x = jnp.concatenate([x_hi.astype(BF16), x_lo.astype(BF16)], axis=1)
        up = jnp.dot(x, wu_bf[...], preferred_element_type=F32) + bu_ref[0, 0]
        x_glu = jnp.minimum(up[:, :D_FF], SWIGLU_LIMIT)
        x_lin = jnp.clip(up[:, D_FF:], -SWIGLU_LIMIT, SWIGLU_LIMIT)
        act = x_glu * jax.nn.sigmoid(SWIGLU_ALPHA * x_glu) * (x_lin + 1.0)
        o_ref[rs, :] = _pack_bf16_pairs(
            jnp.dot(act.astype(BF16), wd_bf[...], preferred_element_type=F32) + bd_ref[0, 0])

    def zero_rows(r0, n):
        o_ref[r0:r0 + n, :] = jnp.zeros((n, o_ref.shape[1]), o_ref.dtype)

    half = ROW_SUB // 2
    for s in range(ROW_BLK // ROW_SUB):
        r0 = s * ROW_SUB
        left = nv_ref[i] - r0

        @pl.when(left > half)
        def _():
            run_rows(r0, ROW_SUB, left)

        @pl.when((left > 0) & (left <= half))
        def _():
            run_rows(r0, half, left)
            zero_rows(r0 + half, half)

        @pl.when(left <= 0)
        def _():
            zero_rows(r0, ROW_SUB)


def _experts(layer, block_e, n_valid, rows, w_up, b_up, w_down, b_down):
    n_rows, dh = rows.shape
    d = 2 * dh
    n_blocks = n_rows // ROW_BLK
    depth, ne, _, f2 = w_up.shape
    wsel = lambda i, be, nu: (layer, be[i], 0, 0)
    grid_spec = pltpu.PrefetchScalarGridSpec(
        num_scalar_prefetch=2,
        grid=(n_blocks,),
        in_specs=[pl.BlockSpec((ROW_BLK, dh), lambda i, be, nu: (i, 0)),
                  pl.BlockSpec((1, 1, d, f2), wsel),
                  pl.BlockSpec((1, 1, 1, f2), wsel),
                  pl.BlockSpec((1, 1, D_FF, d), wsel),
                  pl.BlockSpec((1, 1, 1, d), wsel)],
        out_specs=pl.BlockSpec((ROW_BLK, dh), lambda i, be, nu: (i, 0)),
        scratch_shapes=[pltpu.VMEM((d, f2), BF16), pltpu.VMEM((D_FF, d), BF16)],
    )
    return pl.pallas_call(
        _expert_kernel,
        grid_spec=grid_spec,
        out_shape=jax.ShapeDtypeStruct((n_rows, dh), jnp.uint32),
        compiler_params=_cparams(("arbitrary",)),
        name="moe_experts",
    )(block_e, n_valid, rows, w_up, b_up.reshape(depth, ne, 1, f2), w_down,
      b_down.reshape(depth, ne, 1, d))


def _combine_kernel(y0_ref, y1_ref, y2_ref, y3_ref, gate_ref, x1_ref, g2_ref, fw_ref, *rest,
                    final):
    o_ref = rest[-1]
    gates = gate_ref[...]
    y_hi, y_lo = None, None
    for k, y_ref in enumerate((y0_ref, y1_ref, y2_ref, y3_ref)):
        hi, lo = _unpack_bf16_pairs(y_ref[...])
        g = gates[:, k:k + 1]
        y_hi = g * hi if y_hi is None else y_hi + g * hi
        y_lo = g * lo if y_lo is None else y_lo + g * lo
    y = jnp.concatenate([y_hi, y_lo], axis=1)
    x2 = x1_ref[...] + g2_ref[0] * y
    if final:
        x2 = (x2 * lax.rsqrt(jnp.mean(x2 * x2, axis=-1, keepdims=True) + NORM_EPS)) * fw_ref[...]
    o_ref[...] = x2


def _combine(y4, gates, x1, mod5, b0, layer, fw, seq, final, grp, n_out, prev_out, th=1024):
    t, d = x1.shape
    th = min(th, seq)
    steps = t // th
    per_b = seq // th
    slot = lambda k: pl.BlockSpec((th, d // 2), lambda i: (k * steps + i, 0))
    return pl.pallas_call(
        functools.partial(_combine_kernel, final=final),
        grid=(steps,),
        in_specs=[slot(0), slot(1), slot(2), slot(3),
                  pl.BlockSpec((th, TOP_K), lambda i: (i, 0)),
                  pl.BlockSpec((th, d), lambda i: (i, 0)),
                  _mod_spec(layer, MOD_G2, per_b, b0),
                  pl.BlockSpec((1, d), lambda i: (0, 0))]
                 + ([] if prev_out is None else [pl.BlockSpec(memory_space=pl.ANY)]),
        out_specs=pl.BlockSpec((None, th, d), lambda i: (grp, i, 0)),
        out_shape=jax.ShapeDtypeStruct((n_out, t, d), F32),
        input_output_aliases={} if prev_out is None else {8: 0},
        compiler_params=_cparams(("arbitrary",)),
        name="moe_combine",
    )(y4, y4, y4, y4, gates, x1, mod5, fw, *([] if prev_out is None else [prev_out]))


def _retention_tables(seq):
    f32 = np.float32
    pos = np.arange(seq, dtype=f32)
    inv_freq = (f32(ROPE_BASE) ** (-np.arange(0, HEAD_DK, 2, dtype=f32) / f32(HEAD_DK))).astype(f32)
    ang = pos[:, None] * inv_freq[None, :]
    cos, sin = np.cos(ang), np.sin(ang)
    zero = np.zeros_like(sin)
    zpad = np.zeros((seq, LANES - 2 * HEAD_DK), f32)
    cos_t = np.concatenate([cos, cos, cos, cos, zpad], axis=1)
    sina_t = np.concatenate([-sin, zero, -sin, zero, zpad], axis=1)
    sinb_t = np.concatenate([zero, sin, zero, sin, zpad], axis=1)
    log_gamma = np.log1p(-np.exp2(f32(-5.0) - np.arange(N_HEADS, dtype=f32))).astype(f32)
    log_decay = np.broadcast_to(log_gamma[None, :, None], (RET_CHUNK, N_HEADS, HEAD_DK))
    cum = np.cumsum(log_decay, axis=0, dtype=f32)
    tot = cum[-1:]

    def shp(a):
        flat = a.reshape(a.shape[0], N_HEADS * HEAD_DK)
        return np.where(_DK_SRC >= 0, flat[:, np.maximum(_DK_SRC, 0)], f32(1.0)).astype(f32)

    tabs = (cos_t, sina_t, sinb_t, shp(np.exp(cum)), shp(np.exp(-cum)), shp(np.exp(tot - cum)),
            shp(np.exp(tot)))
    return tuple(jnp.asarray(a, F32) for a in tabs)


def kernel(x, c, norm1_w, norm2_w, w_mod, b_mod, w_in, gla_w_a2, gla_b_a, gla_norm_w, ret_norm_w, s5_a_re, s5_a_im, s5_log_dt, s5_b_re, s5_b_im, s5_c_re, s5_c_im, s5_d, s5_w_glu, s5_b_glu, w_out, router_w, router_b, w_up, b_up, w_down, b_down, final_norm_w):
    batch, seq, d = x.shape
    depth = w_mod.shape[0]
    gb = batch // N_STREAMS
    t = gb * seq
    n_slots = t * TOP_K
    n_blocks = n_slots // ROW_BLK + N_EXPERTS
    n_rows = n_blocks * ROW_BLK

    mod = _modulation(c, w_mod, b_mod)
    mod5 = mod.reshape(depth, batch, 6, 1, d).transpose(0, 2, 1, 3, 4)
    ret_tabs = _retention_tables(seq)

    w_p = _take_cols(w_in, _IN_SRC).astype(BF16)
    wa_p = jnp.zeros((depth, LANES, QKP), F32).at[:, :GATE_RANK].set(
        _take_cols(gla_w_a2, _DK_SRC)).astype(BF16)
    ba_p = _take_cols(gla_b_a, _DK_SRC).reshape(depth, 1, QKP)
    gnw = _take_cols(gla_norm_w, _DV_SRC).reshape(depth, 1, HP)
    rnw = _take_cols(ret_norm_w, _DV_SRC).reshape(depth, 1, HP)
    kv = N_HEADS * HEAD_DV
    wo_g = _take_rows(w_out[:, :kv], _DV_SRC).astype(BF16)
    wo_r = _take_rows(w_out[:, kv:2 * kv], _DV_SRC).astype(BF16)
    wo_s = w_out[:, 2 * kv:].astype(BF16)
    rw_t = jnp.swapaxes(router_w, 1, 2)
    rw_hi = rw_t.astype(BF16)
    rw_p = jnp.concatenate([rw_hi, (rw_t - rw_hi.astype(F32)).astype(BF16)], axis=1)
    rb_p = jnp.broadcast_to(router_b[:, :, None], (depth, N_EXPERTS, LANES))
    s5_tabs = jax.vmap(_s5_tables)(s5_a_re, s5_a_im, s5_log_dt, s5_b_re, s5_b_im, s5_c_re,
                                   s5_c_im, s5_d, s5_w_glu, s5_b_glu)
    n1 = norm1_w.reshape(depth, 1, d)
    n2 = norm2_w.reshape(depth, 1, d)
    fw = final_norm_w.reshape(1, d)

    xs = [(x.reshape(N_STREAMS, t, d), g) for g in range(N_STREAMS)]
    out = None

    for i in range(depth):
        final = i == depth - 1
        for g in range(N_STREAMS):
            x3, grp = xs[g]
            b0 = g * gb

            proj = _in_projection(x3, grp, mod5, b0, i, n1, w_p, seq)
            o_gla = _gla_mixer(proj, gb, seq, i, wa_p, ba_p, gnw)
            o_ret = _ret_mixer(proj, gb, seq, i, *ret_tabs, rnw)
            o_s5 = _s5_mixer(proj, gb, seq, i, s5_tabs)

            x1, hdn, idx, gates, rank, counts = _outproj_router(
                o_gla, o_ret, o_s5, wo_g, wo_r, wo_s, x3, grp, mod5, b0, i, n2, rw_p, rb_p, seq)

            cnt = counts[:, 0].astype(jnp.int32)
            padded = (cnt + ROW_BLK - 1) // ROW_BLK * ROW_BLK
            pad_ends = jnp.cumsum(padded)
            pad_starts = pad_ends - padded
            slot_start = jnp.sum(jnp.where(idx[:TOP_K, :, None] == jnp.arange(N_EXPERTS),
                                           pad_starts, 0), axis=-1)
            dest_sm = (slot_start + rank[:TOP_K]).astype(jnp.int32).reshape(-1)
            gates_tm = gates[:TOP_K].T
            blk_start = jnp.arange(n_blocks, dtype=jnp.int32) * ROW_BLK
            block_e = jnp.minimum(jnp.sum(pad_ends[None, :] <= blk_start[:, None], axis=1),
                                  N_EXPERTS - 1).astype(jnp.int32)
            n_valid = jnp.clip((pad_starts + cnt)[block_e] - blk_start, 0,
                               ROW_BLK).astype(jnp.int32)

            rows = _scatter_rows(hdn, dest_sm, n_rows)
            out_rows = _experts(i, block_e, n_valid, rows, w_up, b_up, w_down, b_down)
            y4 = _gather_rows(out_rows, dest_sm)
            if final:
                out = _combine(y4, gates_tm, x1, mod5, b0, i, fw, seq, True, g, N_STREAMS, out)
            else:
                xs[g] = (_combine(y4, gates_tm, x1, mod5, b0, i, fw, seq, False, 0, 1, None), 0)

    return out.reshape(batch, seq, d)
```

```python
import functools

import numpy as np
import jax
import jax.numpy as jnp
from jax import lax
from jax.experimental import pallas as pl
from jax.experimental.pallas import tpu as pltpu
from jax.experimental.pallas import tpu_sc as plsc

D_MODEL = 1024
CHUNK = 64
RET_CHUNK = 128
NORM_EPS = 1e-5
N_HEADS = 4
HEAD_DK = 48
HEAD_DV = 96
GATE_RANK = 16
GATE_TEMP = 16.0
ROPE_BASE = 10000.0
S5_WIDTH = 256
S5_GROUP_DIM = 16
S5_GROUPS = 16
S5_STATE = 64
N_EXPERTS = 32
TOP_K = 4
D_FF = 1024
SWIGLU_LIMIT = 7.0
SWIGLU_ALPHA = 1.702

LANES = 128
HEAD_PAD = LANES
HP = N_HEADS * HEAD_PAD
N_PAIRS = N_HEADS // 2
QKP = N_PAIRS * LANES
VMEM_LIMIT = 56 * 1024 * 1024

OFF_GQ, OFF_GK, OFF_GV, OFF_GG = 0, QKP, 2 * QKP, 2 * QKP + HP
OFF_RQ = OFF_GG + HP
OFF_RK, OFF_RV, OFF_RG = OFF_RQ + QKP, OFF_RQ + 2 * QKP, OFF_RQ + 2 * QKP + HP
OFF_SU = OFF_RG + HP
OFF_GA = OFF_SU + S5_WIDTH
NP_COLS = OFF_GA + LANES
PROJ_CH = 1152

ROW_BLK = 1024
ROW_SUB = 512
MOD_SH1, MOD_SC1, MOD_G1, MOD_SH2, MOD_SC2, MOD_G2 = range(6)
N_STREAMS = 1

F32 = jnp.float32
BF16 = jnp.bfloat16


def _DK_SRC_LANE(h, d):
    return (h // 2) * LANES + (h % 2) * HEAD_DK + d


def _in_col_map():
    src = -np.ones((NP_COLS,), np.int64)
    kq = N_HEADS * HEAD_DK
    kv = N_HEADS * HEAD_DV
    base = dict(gq=0, gk=kq, gv=2 * kq, gg=2 * kq + kv, ga=2 * kq + 2 * kv)
    r0 = base['ga'] + GATE_RANK
    base.update(rq=r0, rk=r0 + kq, rv=r0 + 2 * kq, rg=r0 + 2 * kq + kv, su=r0 + 2 * kq + 2 * kv)
    for h in range(N_HEADS):
        for d in range(HEAD_DK):
            lane = _DK_SRC_LANE(h, d)
            src[OFF_GQ + lane] = base['gq'] + h * HEAD_DK + d
            src[OFF_GK + lane] = base['gk'] + h * HEAD_DK + d
            src[OFF_RQ + lane] = base['rq'] + h * HEAD_DK + d
            src[OFF_RK + lane] = base['rk'] + h * HEAD_DK + d
        for d in range(HEAD_DV):
            src[OFF_GV + h * HEAD_PAD + d] = base['gv'] + h * HEAD_DV + d
            src[OFF_GG + h * HEAD_PAD + d] = base['gg'] + h * HEAD_DV + d
            src[OFF_RV + h * HEAD_PAD + d] = base['rv'] + h * HEAD_DV + d
            src[OFF_RG + h * HEAD_PAD + d] = base['rg'] + h * HEAD_DV + d
    src[OFF_SU:OFF_SU + S5_WIDTH] = base['su'] + np.arange(S5_WIDTH)
    src[OFF_GA:OFF_GA + GATE_RANK] = base['ga'] + np.arange(GATE_RANK)
    return src


_IN_SRC = _in_col_map()


def _head_pad_map(width):
    src = -np.ones((HP,), np.int64)
    for h in range(N_HEADS):
        src[h * HEAD_PAD:h * HEAD_PAD + width] = h * width + np.arange(width)
    return src


_DV_SRC = _head_pad_map(HEAD_DV)
_DK_SRC = -np.ones((QKP,), np.int64)
for _h in range(N_HEADS):
    for _d in range(HEAD_DK):
        _DK_SRC[_DK_SRC_LANE(_h, _d)] = _h * HEAD_DK + _d


def _take_static(w, src, axis):
    axis = axis % w.ndim
    pieces, start = [], 0
    for j in range(1, len(src) + 1):
        run_ends = (j == len(src) or (src[j] < 0) != (src[start] < 0)
                    or (src[start] >= 0 and src[j] != src[j - 1] + 1))
        if run_ends:
            if src[start] < 0:
                shape = w.shape[:axis] + (j - start,) + w.shape[axis + 1:]
                pieces.append(jnp.zeros(shape, w.dtype))
            else:
                pieces.append(lax.slice_in_dim(w, int(src[start]), int(src[j - 1]) + 1, axis=axis))
            start = j
    return jnp.concatenate(pieces, axis=axis)


def _take_cols(w, src):
    return _take_static(w, src, -1)


def _take_rows(w, src):
    return _take_static(w, src, -2)


def _layer_spec(shape, layer):
    return pl.BlockSpec((None,) + tuple(shape), lambda *_: (layer,) + (0,) * len(shape))


def _mod_spec(layer, which, per_b, b0):
    return pl.BlockSpec((None, None, 1, 1, D_MODEL),
                        lambda i: (layer, which, b0 + i // per_b, 0, 0))


def _cparams(sem):
    return pltpu.CompilerParams(dimension_semantics=sem, vmem_limit_bytes=VMEM_LIMIT)


def _mod_kernel(c_ref, w_ref, b_ref, o_ref):
    c = c_ref[...]
    cond = c * jax.nn.sigmoid(c)
    o_ref[0] = jnp.dot(cond, w_ref[0], preferred_element_type=F32,
                       precision=lax.Precision.HIGHEST) + b_ref[0]


def _modulation(c, w_mod, b_mod):
    depth, d, n = w_mod.shape
    b = c.shape[0]
    nb = 1536
    return pl.pallas_call(
        _mod_kernel,
        grid=(depth, n // nb),
        in_specs=[pl.BlockSpec((b, d), lambda l, j: (0, 0)),
                  pl.BlockSpec((1, d, nb), lambda l, j: (l, 0, j)),
                  pl.BlockSpec((1, 1, nb), lambda l, j: (l, 0, j))],
        out_specs=pl.BlockSpec((1, b, nb), lambda l, j: (l, 0, j)),
        out_shape=jax.ShapeDtypeStruct((depth, b, n), F32),
        compiler_params=_cparams(("arbitrary", "arbitrary")),
        name="adaln_mod",
    )(c, w_mod, b_mod.reshape(depth, 1, n))


def _rms_mod(x, nw, sc, sh):
    y = x * lax.rsqrt(jnp.mean(x * x, axis=-1, keepdims=True) + NORM_EPS)
    return (y * nw) * (1.0 + sc) + sh


def _inproj_kernel(x_ref, sc_ref, sh_ref, nw_ref, w_ref, o_ref):
    h = _rms_mod(x_ref[...], nw_ref[...], sc_ref[0], sh_ref[0]).astype(BF16)
    for j in range(NP_COLS // PROJ_CH):
        cs = slice(j * PROJ_CH, (j + 1) * PROJ_CH)
        o_ref[:, cs] = jnp.dot(h, w_ref[:, cs], preferred_element_type=F32).astype(BF16)


def _in_projection(x3, grp, mod5, b0, layer, nw, w_p, seq, tm=1024):
    _, t, d = x3.shape
    tm = min(tm, seq)
    per_b = seq // tm
    return pl.pallas_call(
        _inproj_kernel,
        grid=(t // tm,),
        in_specs=[pl.BlockSpec((None, tm, d), lambda i: (grp, i, 0)),
                  _mod_spec(layer, MOD_SC1, per_b, b0), _mod_spec(layer, MOD_SH1, per_b, b0),
                  _layer_spec((1, d), layer), _layer_spec((d, NP_COLS), layer)],
        out_specs=pl.BlockSpec((tm, NP_COLS), lambda i: (i, 0)),
        out_shape=jax.ShapeDtypeStruct((t, NP_COLS), BF16),
        compiler_params=_cparams(("arbitrary",)),
        name="in_proj",
    )(x3, mod5, mod5, nw, w_p)


_NT = (((1,), (1,)), ((), ()))
_TN = (((0,), (0,)), ((), ()))


def _tri_mask(n=CHUNK):
    r = lax.broadcasted_iota(jnp.int32, (n, n), 0)
    c = lax.broadcasted_iota(jnp.int32, (n, n), 1)
    return r >= c


def _pair_masks(rows):
    lane = lax.broadcasted_iota(jnp.int32, (rows, LANES), 1)
    return lane < HEAD_DK, (lane >= HEAD_DK) & (lane < 2 * HEAD_DK)


def _head_attention(qd, ki, ke, vh, et, st_ref, h, causal):
    qb = qd.astype(BF16)
    sc = lax.dot_general(qb, ki.astype(BF16), _NT, preferred_element_type=F32)
    sc = jnp.where(causal, sc, 0.0)
    st = st_ref[h]
    o = jnp.dot(sc.astype(BF16), vh, preferred_element_type=F32)
    o = o + lax.dot_general(qb, st.astype(BF16), _NT, preferred_element_type=F32)
    st_ref[h] = st * et + lax.dot_general(vh, ke.astype(BF16), _TN, preferred_element_type=F32)
    return o


def _gla_kernel(q_ref, k_ref, v_ref, g_ref, a_ref, wa_ref, ba_ref, nw_ref, tri_ref, o_ref,
                st_ref, qd_s, ki_s, ke_s, et_s):
    @pl.when(pl.program_id(1) == 0)
    def _():
        st_ref[...] = jnp.zeros_like(st_ref)

    causal = _tri_mask()
    tl = q_ref.shape[0]
    n_chunks = tl // CHUNK

    z = jnp.dot(a_ref[...], wa_ref[...], preferred_element_type=F32) + ba_ref[...]
    la = (jnp.minimum(z, 0.0) - jnp.log1p(jnp.exp(-jnp.abs(z)))) * (1.0 / GATE_TEMP)
    hi = la.astype(BF16)
    lo = (la - hi.astype(F32)).astype(BF16)
    cum = (jnp.dot(tri_ref[...], hi, preferred_element_type=F32)
           + jnp.dot(tri_ref[...], lo, preferred_element_type=F32))
    cum3 = cum.reshape(n_chunks, CHUNK, QKP)
    tot3 = cum3[:, CHUNK - 1:CHUNK, :]
    qd = (q_ref[...].astype(F32) * (HEAD_DK ** -0.5)) * jnp.exp(cum)
    masks = _pair_masks(tl)
    for h in range(N_HEADS):
        pair = slice((h // 2) * LANES, (h // 2 + 1) * LANES)
        qd_s[:, h * HEAD_PAD:(h + 1) * HEAD_PAD] = jnp.where(masks[h % 2], qd[:, pair],
                                                              0.0).astype(BF16)
    kf = k_ref[...].astype(F32)
    ki_s[...] = (kf * jnp.exp(-cum)).astype(BF16)
    ke_s[...] = (kf * jnp.exp(tot3 - cum3).reshape(tl, QKP)).astype(BF16)
    et_s[...] = jnp.exp(tot3).reshape(n_chunks, QKP)

    def chunk(c, carry):
        r = pl.ds(pl.multiple_of(c * CHUNK, CHUNK), CHUNK)
        et = et_s[pl.ds(c, 1), :]
        for h in range(N_HEADS):
            sl = slice(h * HEAD_PAD, (h + 1) * HEAD_PAD)
            pair = slice((h // 2) * LANES, (h // 2 + 1) * LANES)
            o = _head_attention(qd_s[r, sl], ki_s[r, pair], ke_s[r, pair], v_ref[r, sl],
                                et[:, pair], st_ref, h, causal)
            ms = jnp.sum(o * o, axis=-1, keepdims=True) * (1.0 / HEAD_DV)
            y = (o * lax.rsqrt(ms + NORM_EPS)) * nw_ref[:, sl]
            g = g_ref[r, sl].astype(F32)
            o_ref[r, sl] = (y * (g * jax.nn.sigmoid(g))).astype(BF16)
        return carry

    lax.fori_loop(0, n_chunks, chunk, 0, unroll=4)


def _ret_kernel(q_ref, k_ref, v_ref, g_ref, cos_ref, sina_ref, sinb_ref, dq_ref, dki_ref,
                dke_ref, dt_ref, nw_ref, o_ref, st_ref):
    @pl.when(pl.program_id(1) == 0)
    def _():
        st_ref[...] = jnp.zeros_like(st_ref)

    causal = _tri_mask(RET_CHUNK)
    n_chunks = q_ref.shape[0] // RET_CHUNK
    lane = lax.broadcasted_iota(jnp.int32, (RET_CHUNK, HEAD_PAD), 1)
    real = lane < HEAD_DV
    masks = _pair_masks(RET_CHUNK)
    half = HEAD_DK // 2

    def rotary(t, cos, sina, sinb):
        return (t * cos + pltpu.roll(t, LANES - half, 1) * sina + pltpu.roll(t, half, 1) * sinb)

    def chunk(c, carry):
        r = pl.ds(pl.multiple_of(c * RET_CHUNK, RET_CHUNK), RET_CHUNK)
        cos, sina, sinb = cos_ref[r, :], sina_ref[r, :], sinb_ref[r, :]
        pair_q, pair_ki, pair_ke = [], [], []
        for p in range(N_PAIRS):
            ps = slice(p * LANES, (p + 1) * LANES)
            qr = rotary(q_ref[r, ps].astype(F32), cos, sina, sinb) * dq_ref[:, ps]
            kr = rotary(k_ref[r, ps].astype(F32), cos, sina, sinb) * (HEAD_DK ** -0.5)
            pair_q.append(qr)
            pair_ki.append((kr * dki_ref[:, ps]).astype(BF16))
            pair_ke.append((kr * dke_ref[:, ps]).astype(BF16))
        for h in range(N_HEADS):
            sl = slice(h * HEAD_PAD, (h + 1) * HEAD_PAD)
            p = h // 2
            qd = jnp.where(masks[h % 2], pair_q[p], 0.0)
            o = _head_attention(qd, pair_ki[p], pair_ke[p], v_ref[r, sl],
                                dt_ref[:, p * LANES:(p + 1) * LANES], st_ref, h, causal)
            mu = jnp.sum(o, axis=-1, keepdims=True) * (1.0 / HEAD_DV)
            oc = jnp.where(real, o - mu, 0.0)
            var = jnp.sum(oc * oc, axis=-1, keepdims=True) * (1.0 / HEAD_DV)
            y = (oc * lax.rsqrt(var + NORM_EPS)) * nw_ref[:, sl]
            g = g_ref[r, sl].astype(F32)
            o_ref[r, sl] = (y * (g * jax.nn.sigmoid(g))).astype(BF16)
        return carry

    lax.fori_loop(0, n_chunks, chunk, 0, unroll=2)


def _proj_spec(tl, width, col_off, per_b):
    cb = col_off // width
    return pl.BlockSpec((tl, width), lambda b, l: (b * per_b + l, cb))


def _full(shape):
    return pl.BlockSpec(shape, lambda b, l: (0,) * len(shape))


def _gla_mixer(proj, batch, seq, layer, wa_p, ba_p, nw_p, tl=512):
    per_b = seq // tl
    pos = np.arange(tl)
    tri_bd = jnp.asarray((pos[:, None] // CHUNK == pos[None, :] // CHUNK)
                         & (pos[:, None] >= pos[None, :]), BF16)
    return pl.pallas_call(
        _gla_kernel,
        grid=(batch, per_b),
        in_specs=[_proj_spec(tl, QKP, OFF_GQ, per_b), _proj_spec(tl, QKP, OFF_GK, per_b),
                  _proj_spec(tl, HP, OFF_GV, per_b), _proj_spec(tl, HP, OFF_GG, per_b),
                  _proj_spec(tl, LANES, OFF_GA, per_b),
                  _layer_spec((LANES, QKP), layer), _layer_spec((1, QKP), layer),
                  _layer_spec((1, HP), layer), _full((tl, tl))],
        out_specs=pl.BlockSpec((tl, HP), lambda b, l: (b * per_b + l, 0)),
        out_shape=jax.ShapeDtypeStruct((batch * seq, HP), BF16),
        scratch_shapes=[pltpu.VMEM((N_HEADS, HEAD_PAD, HEAD_PAD), F32),
                        pltpu.VMEM((tl, HP), BF16), pltpu.VMEM((tl, QKP), BF16),
                        pltpu.VMEM((tl, QKP), BF16), pltpu.VMEM((tl // CHUNK, QKP), F32)],
        compiler_params=_cparams(("arbitrary", "arbitrary")),
        name="gla_mixer",
    )(proj, proj, proj, proj, proj, wa_p, ba_p, nw_p, tri_bd)


def _ret_mixer(proj, batch, seq, layer, cos_t, sina_t, sinb_t, dq, dki, dke, dtot, nw_p, tl=512):
    per_b = seq // tl
    return pl.pallas_call(
        _ret_kernel,
        grid=(batch, per_b),
        in_specs=[_proj_spec(tl, QKP, OFF_RQ, per_b), _proj_spec(tl, QKP, OFF_RK, per_b),
                  _proj_spec(tl, HP, OFF_RV, per_b), _proj_spec(tl, HP, OFF_RG, per_b),
                  pl.BlockSpec((tl, LANES), lambda b, l: (l, 0)),
                  pl.BlockSpec((tl, LANES), lambda b, l: (l, 0)),
                  pl.BlockSpec((tl, LANES), lambda b, l: (l, 0)),
                  _full((RET_CHUNK, QKP)), _full((RET_CHUNK, QKP)), _full((RET_CHUNK, QKP)),
                  _full((1, QKP)), _layer_spec((1, HP), layer)],
        out_specs=pl.BlockSpec((tl, HP), lambda b, l: (b * per_b + l, 0)),
        out_shape=jax.ShapeDtypeStruct((batch * seq, HP), BF16),
        scratch_shapes=[pltpu.VMEM((N_HEADS, HEAD_PAD, HEAD_PAD), F32)],
        compiler_params=_cparams(("arbitrary", "arbitrary")),
        name="ret_mixer",
    )(proj, proj, proj, proj, cos_t, sina_t, sinb_t, dq, dki, dke, dtot, nw_p)


S5_N = S5_GROUPS * S5_STATE
S5_SLAB = 256


def _gelu_tanh(x):
    return 0.5 * x * (1.0 + jnp.tanh(np.sqrt(2.0 / np.pi) * (x + 0.044715 * (x * x * x))))


def _s5_kernel(u_ref, bb_ref, cb_ref, nr_ref, ni_ref, pr_ref, pi_ref, lr_ref, li_ref,
               d_ref, wg_ref, bg_ref, o_ref, sr_ref, si_ref, x_scr, s_scr):
    @pl.when(pl.program_id(1) == 0)
    def _():
        sr_ref[...] = jnp.zeros_like(sr_ref)
        si_ref[...] = jnp.zeros_like(si_ref)

    tri = _tri_mask().astype(BF16)
    n_chunks = u_ref.shape[0] // CHUNK
    u = u_ref[...]
    x_scr[...] = jnp.dot(u, bb_ref[...], preferred_element_type=F32)

    def chunk(c, carry):
        r = pl.ds(pl.multiple_of(c * CHUNK, CHUNK), CHUNK)
        for j in range(S5_N // S5_SLAB):
            cs = slice(j * S5_SLAB, (j + 1) * S5_SLAB)
            ci = slice(S5_N + j * S5_SLAB, S5_N + (j + 1) * S5_SLAB)
            xr, xi = x_scr[r, cs], x_scr[r, ci]
            nr, ni = nr_ref[:, cs], ni_ref[:, cs]
            p_r = jnp.dot(tri, (xr * nr - xi * ni).astype(BF16), preferred_element_type=F32)
            p_i = jnp.dot(tri, (xr * ni + xi * nr).astype(BF16), preferred_element_type=F32)
            s0r, s0i = sr_ref[:, cs], si_ref[:, cs]
            lr, li = lr_ref[:, cs], li_ref[:, cs]
            q_r = p_r + (s0r * lr - s0i * li)
            q_i = p_i + (s0r * li + s0i * lr)
            pr, pi = pr_ref[:, cs], pi_ref[:, cs]
            s_r = q_r * pr - q_i * pi
            s_i = q_r * pi + q_i * pr
            sr_ref[:, cs] = s_r[CHUNK - 1:CHUNK, :]
            si_ref[:, cs] = s_i[CHUNK - 1:CHUNK, :]
            s_scr[r, cs] = s_r.astype(BF16)
            s_scr[r, ci] = s_i.astype(BF16)
        return carry

    lax.fori_loop(0, n_chunks, chunk, 0, unroll=2)
    y = jnp.dot(s_scr[...], cb_ref[...], preferred_element_type=F32)
    y = _gelu_tanh(y + d_ref[...] * u.astype(F32))
    gate = jnp.dot(y.astype(BF16), wg_ref[...], preferred_element_type=F32) + bg_ref[...]
    o_ref[...] = (y * jax.nn.sigmoid(gate)).astype(BF16)


def _s5_mixer(proj, batch, seq, layer, tabs, tl=512):
    per_b = seq // tl
    bb, cb, nr, ni, pr, pi, lr, li, dsk, wg, bg = tabs
    return pl.pallas_call(
        _s5_kernel,
        grid=(batch, per_b),
        in_specs=[_proj_spec(tl, S5_WIDTH, OFF_SU, per_b),
                  *[_layer_spec(a.shape[1:], layer) for a in tabs]],
        out_specs=pl.BlockSpec((tl, S5_WIDTH), lambda b, l: (b * per_b + l, 0)),
        out_shape=jax.ShapeDtypeStruct((batch * seq, S5_WIDTH), BF16),
        scratch_shapes=[pltpu.VMEM((1, S5_N), F32), pltpu.VMEM((1, S5_N), F32),
                        pltpu.VMEM((tl, 2 * S5_N), F32), pltpu.VMEM((tl, 2 * S5_N), BF16)],
        compiler_params=_cparams(("arbitrary", "arbitrary")),
        name="s5_mixer",
    )(proj, bb, cb, nr, ni, pr, pi, lr, li, dsk, wg, bg)


def _s5_tables(a_re, a_im, log_dt, b_re, b_im, c_re, c_im, d_skip, w_glu, b_glu):
    lam = lax.complex(a_re, a_im)
    dt = jnp.exp(log_dt)[:, None]
    lam_bar = jnp.exp(lam * dt)
    b_bar = ((lam_bar - 1.0) / lam)[..., None] * lax.complex(b_re, b_im)
    eye = jnp.eye(S5_GROUPS, dtype=F32)
    def blk_b(m):
        return jnp.einsum('gph,gk->ghkp', m, eye).reshape(S5_WIDTH, S5_N)
    bb = jnp.concatenate([blk_b(jnp.real(b_bar)), blk_b(jnp.imag(b_bar))], axis=1)
    def blk_c(m):
        return jnp.einsum('ghp,gk->kpgh', m, eye).reshape(S5_N, S5_WIDTH)
    cb = jnp.concatenate([blk_c(c_re), blk_c(-c_im)], axis=0)
    steps = jnp.arange(CHUNK, dtype=F32)[:, None, None]
    lam_dt = (lam * dt)[None]
    pos = jnp.exp(lam_dt * steps).reshape(CHUNK, S5_N)
    neg = jnp.exp(-lam_dt * steps).reshape(CHUNK, S5_N)
    one = lam_bar.reshape(1, S5_N)
    return (bb.astype(BF16), cb.astype(BF16), jnp.real(neg), jnp.imag(neg), jnp.real(pos),
            jnp.imag(pos), jnp.real(one), jnp.imag(one), d_skip.reshape(1, S5_WIDTH),
            w_glu.astype(BF16), b_glu.reshape(1, S5_WIDTH))


def _pack_bf16_pairs(x):
    w = x.shape[1] // 2
    xb = x.astype(BF16).astype(F32)
    hi = lax.bitcast_convert_type(xb[:, :w], jnp.uint32)
    lo = lax.bitcast_convert_type(xb[:, w:], jnp.uint32)
    return hi | (lo >> 16)


def _unpack_bf16_pairs(p):
    hi = lax.bitcast_convert_type(p & jnp.uint32(0xFFFF0000), F32)
    lo = lax.bitcast_convert_type(p << 16, F32)
    return hi, lo


def _router_kernel(og_ref, or_ref, os_ref, wg_ref, wr_ref, ws_ref, x_ref, g1_ref, sc_ref,
                   sh_ref, nw_ref, rw_ref, rb_ref,
                   x1_ref, h_ref, idx_ref, gate_ref, rank_ref, cnt_ref, carry_ref):
    i = pl.program_id(0)

    @pl.when(i == 0)
    def _():
        carry_ref[...] = jnp.zeros_like(carry_ref)

    mix = (jnp.dot(og_ref[...], wg_ref[...], preferred_element_type=F32)
           + jnp.dot(or_ref[...], wr_ref[...], preferred_element_type=F32)
           + jnp.dot(os_ref[...], ws_ref[...], preferred_element_type=F32))
    x1 = x_ref[...] + g1_ref[0] * mix
    x1_ref[...] = x1
    hdn = _rms_mod(x1, nw_ref[...], sc_ref[0], sh_ref[0])
    h_ref[...] = _pack_bf16_pairs(hdn)
    h_hi = hdn.astype(BF16)
    h_lo = (hdn - h_hi.astype(F32)).astype(BF16)
    p = lax.dot_general(rw_ref[...], h_hi, _NT, preferred_element_type=F32)
    q = lax.dot_general(rw_ref[:N_EXPERTS, :], h_lo, _NT, preferred_element_type=F32)
    logits = (p[:N_EXPERTS] + p[N_EXPERTS:] + q) + rb_ref[:, 0:1]
    tm = logits.shape[1]
    eidx = lax.broadcasted_iota(jnp.int32, (N_EXPERTS, tm), 0)
    work = logits
    onehot = jnp.zeros((N_EXPERTS, tm), F32)
    vals, idxs, sels = [], [], []
    for _ in range(TOP_K):
        m = jnp.max(work, axis=0, keepdims=True)
        ix = jnp.min(jnp.where(work == m, eidx, N_EXPERTS), axis=0, keepdims=True)
        sel = eidx == ix
        work = jnp.where(sel, -jnp.inf, work)
        onehot = onehot + sel.astype(F32)
        vals.append(m)
        idxs.append(ix)
        sels.append(sel)
    exps = [jnp.exp(v - vals[0]) for v in vals]
    denom = exps[0] + exps[1] + exps[2] + exps[3]
    r = lax.broadcasted_iota(jnp.int32, (tm, tm), 0)
    c = lax.broadcasted_iota(jnp.int32, (tm, tm), 1)
    earlier = (r < c).astype(BF16)
    before = (jnp.dot(onehot.astype(BF16), earlier, preferred_element_type=F32)
              + carry_ref[:, 0:1])
    row8 = lax.broadcasted_iota(jnp.int32, (8, tm), 0)
    idx_out = jnp.zeros((8, tm), jnp.int32)
    gate_out = jnp.zeros((8, tm), F32)
    rank_out = jnp.zeros((8, tm), F32)
    for k in range(TOP_K):
        rk = jnp.sum(jnp.where(sels[k], before, 0.0), axis=0, keepdims=True)
        idx_out = jnp.where(row8 == k, idxs[k], idx_out)
        gate_out = jnp.where(row8 == k, exps[k] / denom, gate_out)
        rank_out = jnp.where(row8 == k, rk, rank_out)
    idx_ref[...] = idx_out
    gate_ref[...] = gate_out
    rank_ref[...] = rank_out.astype(jnp.int32)
    total = carry_ref[...] + jnp.sum(onehot, axis=1, keepdims=True)
    carry_ref[...] = total
    cnt_ref[...] = total


def _outproj_router(o_gla, o_ret, o_s5, wg, wr, ws, x3, grp, mod5, b0, layer, nw2, rw_p, rb_p,
                    seq, tm=1024):
    _, t, d = x3.shape
    tm = min(tm, seq)
    per_b = seq // tm
    row = lambda w: pl.BlockSpec((tm, w), lambda i: (i, 0))
    full = lambda s: pl.BlockSpec(s, lambda i: (0,) * len(s))
    slot_t = pl.BlockSpec((8, tm), lambda i: (0, i))
    return pl.pallas_call(
        _router_kernel,
        grid=(t // tm,),
        in_specs=[row(HP), row(HP), row(S5_WIDTH), _layer_spec((HP, d), layer),
                  _layer_spec((HP, d), layer), _layer_spec((S5_WIDTH, d), layer),
                  pl.BlockSpec((None, tm, d), lambda i: (grp, i, 0)),
                  _mod_spec(layer, MOD_G1, per_b, b0), _mod_spec(layer, MOD_SC2, per_b, b0),
                  _mod_spec(layer, MOD_SH2, per_b, b0), _layer_spec((1, d), layer),
                  _layer_spec((2 * N_EXPERTS, d), layer), _layer_spec((N_EXPERTS, LANES), layer)],
        out_specs=[row(d), row(d // 2), slot_t, slot_t, slot_t, full((N_EXPERTS, LANES))],
        out_shape=[jax.ShapeDtypeStruct((t, d), F32),
                   jax.ShapeDtypeStruct((t, d // 2), jnp.uint32),
                   jax.ShapeDtypeStruct((8, t), jnp.int32),
                   jax.ShapeDtypeStruct((8, t), F32),
                   jax.ShapeDtypeStruct((8, t), jnp.int32),
                   jax.ShapeDtypeStruct((N_EXPERTS, LANES), F32)],
        scratch_shapes=[pltpu.VMEM((N_EXPERTS, LANES), F32)],
        compiler_params=_cparams(("arbitrary",)),
        name="outproj_router",
    )(o_gla, o_ret, o_s5, wg, wr, ws, x3, mod5, mod5, mod5, nw2, rw_p, rb_p)


GATHER_WIN = 64


def _gather_rows(table, idx):
    m = idx.shape[0]
    w = table.shape[1]
    mesh = plsc.VectorSubcoreMesh(core_axis_name="core", subcore_axis_name="subcore")

    @functools.partial(pl.kernel, out_type=jax.ShapeDtypeStruct((m, w), table.dtype),
                       mesh=mesh, name="sc_row_gather")
    def gather(x_hbm, i_hbm, o_hbm):
        def body(i_vmem, o_vmem):
            pltpu.sync_copy(x_hbm.at[i_vmem], o_vmem)

        pltpu.emit_pipeline(
            body,
            grid=(m // GATHER_WIN,),
            in_specs=[pl.BlockSpec((GATHER_WIN,), lambda i: (i,))],
            out_specs=[pl.BlockSpec((GATHER_WIN, w), lambda i: (i, 0))],
            core_axis_name=("core", "subcore"),
            dimension_semantics=(pltpu.PARALLEL,),
        )(i_hbm, o_hbm)

    return gather(table, idx)


def _scatter_rows(x, dest_slot_major, n_rows):
    t, w = x.shape
    steps = t // GATHER_WIN
    mesh = plsc.VectorSubcoreMesh(core_axis_name="core", subcore_axis_name="subcore")

    @functools.partial(pl.kernel, out_type=jax.ShapeDtypeStruct((n_rows, w), x.dtype),
                       mesh=mesh, name="sc_row_scatter")
    def scatter(x_hbm, i_hbm, o_hbm):
        def body(x_vmem, i0, i1, i2, i3):
            for i_vmem in (i0, i1, i2, i3):
                pltpu.sync_copy(x_vmem, o_hbm.at[i_vmem])

        slot = lambda k: pl.BlockSpec((GATHER_WIN,), lambda i: (k * steps + i,))
        pltpu.emit_pipeline(
            body,
            grid=(steps,),
            in_specs=[pl.BlockSpec((GATHER_WIN, w), lambda i: (i, 0)),
                      slot(0), slot(1), slot(2), slot(3)],
            out_specs=[],
            core_axis_name=("core", "subcore"),
            dimension_semantics=(pltpu.PARALLEL,),
        )(x_hbm, i_hbm, i_hbm, i_hbm, i_hbm)

    return scatter(x, dest_slot_major)


FF_SLAB = 256


def _expert_kernel(be_ref, nv_ref, rows_ref, wu_ref, bu_ref, wd_ref, bd_ref, o_ref,
                   wu_bf, wd_bf, act_s):
    i = pl.program_id(0)
    e = be_ref[i]
    prev = be_ref[jnp.maximum(i - 1, 0)]

    @pl.when((i == 0) | (e != prev))
    def _():
        wu_bf[...] = wu_ref[0, 0].astype(BF16)
        wd_bf[...] = wd_ref[0, 0].astype(BF16)

    def run_rows(r0, n, left):
        rs = slice(r0, r0 + n)
        row = lax.broadcasted_iota(jnp.int32, (n, rows_ref.shape[1]), 0)
        x_hi, x_lo = _unpack_bf16_pairs(jnp.where(row < left, rows_ref[rs, :], jnp.uint32(0)))
        x = jnp.concatenate([x_hi.astype(BF16), x_lo.astype(BF16)], axis=1)
        for j in range(D_FF // FF_SLAB):
            cg = slice(j * FF_SLAB, (j + 1) * FF_SLAB)
            cl = slice(D_FF + j * FF_SLAB, D_FF + (j + 1) * FF_SLAB)
            x_glu = jnp.dot(x, wu_bf[:, cg], preferred_element_type=F32) + bu_ref[0, 0][:, cg]
            x_lin = jnp.dot(x, wu_bf[:, cl], preferred_element_type=F32) + bu_ref[0, 0][:, cl]
            x_glu = jnp.minimum(x_glu, SWIGLU_LIMIT)
            x_lin = jnp.clip(x_lin, -SWIGLU_LIMIT, SWIGLU_LIMIT)
            act_s[0:n, cg] = (x_glu * jax.nn.sigmoid(SWIGLU_ALPHA * x_glu)
                              * (x_lin + 1.0)).astype(BF16)
        o_ref[rs, :] = _pack_bf16_pairs(
            jnp.dot(act_s[0:n, :], wd_bf[...], preferred_element_type=F32) + bd_ref[0, 0])

    def zero_rows(r0, n):
        o_ref[r0:r0 + n, :] = jnp.zeros((n, o_ref.shape[1]), o_ref.dtype)

    half = ROW_SUB // 2
    for s in range(ROW_BLK // ROW_SUB):
        r0 = s * ROW_SUB
        left = nv_ref[i] - r0

        @pl.when(left > half)
        def _():
            run_rows(r0, ROW_SUB, left)

        @pl.when((left > 0) & (left <= half))
        def _():
            run_rows(r0, half, left)
            zero_rows(r0 + half, half)

        @pl.when(left <= 0)
        def _():
            zero_rows(r0, ROW_SUB)


def _experts(layer, block_e, n_valid, rows, w_up, b_up, w_down, b_down):
    n_rows, dh = rows.shape
    d = 2 * dh
    n_blocks = n_rows // ROW_BLK
    depth, ne, _, f2 = w_up.shape
    wsel = lambda i, be, nu: (layer, be[i], 0, 0)
    grid_spec = pltpu.PrefetchScalarGridSpec(
        num_scalar_prefetch=2,
        grid=(n_blocks,),
        in_specs=[pl.BlockSpec((ROW_BLK, dh), lambda i, be, nu: (i, 0)),
                  pl.BlockSpec((1, 1, d, f2), wsel),
                  pl.BlockSpec((1, 1, 1, f2), wsel),
                  pl.BlockSpec((1, 1, D_FF, d), wsel),
                  pl.BlockSpec((1, 1, 1, d), wsel)],
        out_specs=pl.BlockSpec((ROW_BLK, dh), lambda i, be, nu: (i, 0)),
        scratch_shapes=[pltpu.VMEM((d, f2), BF16), pltpu.VMEM((D_FF, d), BF16),
                        pltpu.VMEM((ROW_SUB, D_FF), BF16)],
    )
    return pl.pallas_call(
        _expert_kernel,
        grid_spec=grid_spec,
        out_shape=jax.ShapeDtypeStruct((n_rows, dh), jnp.uint32),
        compiler_params=_cparams(("arbitrary",)),
        name="moe_experts",
    )(block_e, n_valid, rows, w_up, b_up.reshape(depth, ne, 1, f2), w_down,
      b_down.reshape(depth, ne, 1, d))


def _combine_kernel(y0_ref, y1_ref, y2_ref, y3_ref, gate_ref, x1_ref, g2_ref, fw_ref, *rest,
                    final):
    o_ref = rest[-1]
    gates = gate_ref[...]
    y_hi, y_lo = None, None
    for k, y_ref in enumerate((y0_ref, y1_ref, y2_ref, y3_ref)):
        hi, lo = _unpack_bf16_pairs(y_ref[...])
        g = gates[:, k:k + 1]
        y_hi = g * hi if y_hi is None else y_hi + g * hi
        y_lo = g * lo if y_lo is None else y_lo + g * lo
    y = jnp.concatenate([y_hi, y_lo], axis=1)
    x2 = x1_ref[...] + g2_ref[0] * y
    if final:
        x2 = (x2 * lax.rsqrt(jnp.mean(x2 * x2, axis=-1, keepdims=True) + NORM_EPS)) * fw_ref[...]
    o_ref[...] = x2


def _combine(y4, gates, x1, mod5, b0, layer, fw, seq, final, grp, n_out, prev_out, th=1024):
    t, d = x1.shape
    th = min(th, seq)
    steps = t // th
    per_b = seq // th
    slot = lambda k: pl.BlockSpec((th, d // 2), lambda i: (k * steps + i, 0))
    return pl.pallas_call(
        functools.partial(_combine_kernel, final=final),
        grid=(steps,),
        in_specs=[slot(0), slot(1), slot(2), slot(3),
                  pl.BlockSpec((th, TOP_K), lambda i: (i, 0)),
                  pl.BlockSpec((th, d), lambda i: (i, 0)),
                  _mod_spec(layer, MOD_G2, per_b, b0),
                  pl.BlockSpec((1, d), lambda i: (0, 0))]
                 + ([] if prev_out is None else [pl.BlockSpec(memory_space=pl.ANY)]),
        out_specs=pl.BlockSpec((None, th, d), lambda i: (grp, i, 0)),
        out_shape=jax.ShapeDtypeStruct((n_out, t, d), F32),
        input_output_aliases={} if prev_out is None else {8: 0},
        compiler_params=_cparams(("arbitrary",)),
        name="moe_combine",
    )(y4, y4, y4, y4, gates, x1, mod5, fw, *([] if prev_out is None else [prev_out]))


def _retention_tables(seq):
    f32 = np.float32
    pos = np.arange(seq, dtype=f32)
    inv_freq = (f32(ROPE_BASE) ** (-np.arange(0, HEAD_DK, 2, dtype=f32) / f32(HEAD_DK))).astype(f32)
    ang = pos[:, None] * inv_freq[None, :]
    cos, sin = np.cos(ang), np.sin(ang)
    zero = np.zeros_like(sin)
    zpad = np.zeros((seq, LANES - 2 * HEAD_DK), f32)
    cos_t = np.concatenate([cos, cos, cos, cos, zpad], axis=1)
    sina_t = np.concatenate([-sin, zero, -sin, zero, zpad], axis=1)
    sinb_t = np.concatenate([zero, sin, zero, sin, zpad], axis=1)
    log_gamma = np.log1p(-np.exp2(f32(-5.0) - np.arange(N_HEADS, dtype=f32))).astype(f32)
    log_decay = np.broadcast_to(log_gamma[None, :, None], (RET_CHUNK, N_HEADS, HEAD_DK))
    cum = np.cumsum(log_decay, axis=0, dtype=f32)
    tot = cum[-1:]

    def shp(a):
        flat = a.reshape(a.shape[0], N_HEADS * HEAD_DK)
        return np.where(_DK_SRC >= 0, flat[:, np.maximum(_DK_SRC, 0)], f32(1.0)).astype(f32)

    tabs = (cos_t, sina_t, sinb_t, shp(np.exp(cum)), shp(np.exp(-cum)), shp(np.exp(tot - cum)),
            shp(np.exp(tot)))
    return tuple(jnp.asarray(a, F32) for a in tabs)


def kernel(x, c, norm1_w, norm2_w, w_mod, b_mod, w_in, gla_w_a2, gla_b_a, gla_norm_w, ret_norm_w, s5_a_re, s5_a_im, s5_log_dt, s5_b_re, s5_b_im, s5_c_re, s5_c_im, s5_d, s5_w_glu, s5_b_glu, w_out, router_w, router_b, w_up, b_up, w_down, b_down, final_norm_w):
    batch, seq, d = x.shape
    depth = w_mod.shape[0]
    gb = batch // N_STREAMS
    t = gb * seq
    n_slots = t * TOP_K
    n_blocks = n_slots // ROW_BLK + N_EXPERTS
    n_rows = n_blocks * ROW_BLK

    mod = _modulation(c, w_mod, b_mod)
    mod5 = mod.reshape(depth, batch, 6, 1, d).transpose(0, 2, 1, 3, 4)
    ret_tabs = _retention_tables(seq)

    w_p = _take_cols(w_in, _IN_SRC).astype(BF16)
    wa_p = jnp.zeros((depth, LANES, QKP), F32).at[:, :GATE_RANK].set(
        _take_cols(gla_w_a2, _DK_SRC)).astype(BF16)
    ba_p = _take_cols(gla_b_a, _DK_SRC).reshape(depth, 1, QKP)
    gnw = _take_cols(gla_norm_w, _DV_SRC).reshape(depth, 1, HP)
    rnw = _take_cols(ret_norm_w, _DV_SRC).reshape(depth, 1, HP)
    kv = N_HEADS * HEAD_DV
    wo_g = _take_rows(w_out[:, :kv], _DV_SRC).astype(BF16)
    wo_r = _take_rows(w_out[:, kv:2 * kv], _DV_SRC).astype(BF16)
    wo_s = w_out[:, 2 * kv:].astype(BF16)
    rw_t = jnp.swapaxes(router_w, 1, 2)
    rw_hi = rw_t.astype(BF16)
    rw_p = jnp.concatenate([rw_hi, (rw_t - rw_hi.astype(F32)).astype(BF16)], axis=1)
    rb_p = jnp.broadcast_to(router_b[:, :, None], (depth, N_EXPERTS, LANES))
    s5_tabs = jax.vmap(_s5_tables)(s5_a_re, s5_a_im, s5_log_dt, s5_b_re, s5_b_im, s5_c_re,
                                   s5_c_im, s5_d, s5_w_glu, s5_b_glu)
    n1 = norm1_w.reshape(depth, 1, d)
    n2 = norm2_w.reshape(depth, 1, d)
    fw = final_norm_w.reshape(1, d)

    xs = [(x.reshape(N_STREAMS, t, d), g) for g in range(N_STREAMS)]
    out = None

    for i in range(depth):
        final = i == depth - 1
        for g in range(N_STREAMS):
            x3, grp = xs[g]
            b0 = g * gb

            proj = _in_projection(x3, grp, mod5, b0, i, n1, w_p, seq)
            o_gla = _gla_mixer(proj, gb, seq, i, wa_p, ba_p, gnw)
            o_ret = _ret_mixer(proj, gb, seq, i, *ret_tabs, rnw)
            o_s5 = _s5_mixer(proj, gb, seq, i, s5_tabs)

            x1, hdn, idx, gates, rank, counts = _outproj_router(
                o_gla, o_ret, o_s5, wo_g, wo_r, wo_s, x3, grp, mod5, b0, i, n2, rw_p, rb_p, seq)

            cnt = counts[:, 0].astype(jnp.int32)
            padded = (cnt + ROW_BLK - 1) // ROW_BLK * ROW_BLK
            pad_ends = jnp.cumsum(padded)
            pad_starts = pad_ends - padded
            slot_start = jnp.sum(jnp.where(idx[:TOP_K, :, None] == jnp.arange(N_EXPERTS),
                                           pad_starts, 0), axis=-1)
            dest_sm = (slot_start + rank[:TOP_K]).astype(jnp.int32).reshape(-1)
            gates_tm = gates[:TOP_K].T
            blk_start = jnp.arange(n_blocks, dtype=jnp.int32) * ROW_BLK
            block_e = jnp.minimum(jnp.sum(pad_ends[None, :] <= blk_start[:, None], axis=1),
                                  N_EXPERTS - 1).astype(jnp.int32)
            n_valid = jnp.clip((pad_starts + cnt)[block_e] - blk_start, 0,
                               ROW_BLK).astype(jnp.int32)

            rows = _scatter_rows(hdn, dest_sm, n_rows)
            out_rows = _experts(i, block_e, n_valid, rows, w_up, b_up, w_down, b_down)
            y4 = _gather_rows(out_rows, dest_sm)
            if final:
                out = _combine(y4, gates_tm, x1, mod5, b0, i, fw, seq, True, g, N_STREAMS, out)
            else:
                xs[g] = (_combine(y4, gates_tm, x1, mod5, b0, i, fw, seq, False, 0, 1, None), 0)

    return out.reshape(batch, seq, d)
```

```python
import functools

import numpy as np
import jax
import jax.numpy as jnp
from jax import lax
from jax.experimental import pallas as pl
from jax.experimental.pallas import tpu as pltpu
from jax.experimental.pallas import tpu_sc as plsc

D_MODEL = 1024
CHUNK = 64
RET_CHUNK = 128
NORM_EPS = 1e-5
N_HEADS = 4
HEAD_DK = 48
HEAD_DV = 96
GATE_RANK = 16
GATE_TEMP = 16.0
ROPE_BASE = 10000.0
S5_WIDTH = 256
S5_GROUP_DIM = 16
S5_GROUPS = 16
S5_STATE = 64
N_EXPERTS = 32
TOP_K = 4
D_FF = 1024
SWIGLU_LIMIT = 7.0
SWIGLU_ALPHA = 1.702

LANES = 128
HEAD_PAD = LANES
HP = N_HEADS * HEAD_PAD
N_PAIRS = N_HEADS // 2
QKP = N_PAIRS * LANES
VMEM_LIMIT = 56 * 1024 * 1024

OFF_GQ, OFF_GK, OFF_GV, OFF_GG = 0, QKP, 2 * QKP, 2 * QKP + HP
OFF_RQ = OFF_GG + HP
OFF_RK, OFF_RV, OFF_RG = OFF_RQ + QKP, OFF_RQ + 2 * QKP, OFF_RQ + 2 * QKP + HP
OFF_SU = OFF_RG + HP
OFF_GA = OFF_SU + S5_WIDTH
NP_COLS = OFF_GA + LANES
PROJ_CH = 1152

ROW_BLK = 1024
ROW_SUB = 512
MOD_SH1, MOD_SC1, MOD_G1, MOD_SH2, MOD_SC2, MOD_G2 = range(6)
N_STREAMS = 1

F32 = jnp.float32
BF16 = jnp.bfloat16


def _DK_SRC_LANE(h, d):
    return (h // 2) * LANES + (h % 2) * HEAD_DK + d


def _in_col_map():
    src = -np.ones((NP_COLS,), np.int64)
    kq = N_HEADS * HEAD_DK
    kv = N_HEADS * HEAD_DV
    base = dict(gq=0, gk=kq, gv=2 * kq, gg=2 * kq + kv, ga=2 * kq + 2 * kv)
    r0 = base['ga'] + GATE_RANK
    base.update(rq=r0, rk=r0 + kq, rv=r0 + 2 * kq, rg=r0 + 2 * kq + kv, su=r0 + 2 * kq + 2 * kv)
    for h in range(N_HEADS):
        for d in range(HEAD_DK):
            lane = _DK_SRC_LANE(h, d)
            src[OFF_GQ + lane] = base['gq'] + h * HEAD_DK + d
            src[OFF_GK + lane] = base['gk'] + h * HEAD_DK + d
            src[OFF_RQ + lane] = base['rq'] + h * HEAD_DK + d
            src[OFF_RK + lane] = base['rk'] + h * HEAD_DK + d
        for d in range(HEAD_DV):
            src[OFF_GV + h * HEAD_PAD + d] = base['gv'] + h * HEAD_DV + d
            src[OFF_GG + h * HEAD_PAD + d] = base['gg'] + h * HEAD_DV + d
            src[OFF_RV + h * HEAD_PAD + d] = base['rv'] + h * HEAD_DV + d
            src[OFF_RG + h * HEAD_PAD + d] = base['rg'] + h * HEAD_DV + d
    src[OFF_SU:OFF_SU + S5_WIDTH] = base['su'] + np.arange(S5_WIDTH)
    src[OFF_GA:OFF_GA + GATE_RANK] = base['ga'] + np.arange(GATE_RANK)
    return src


_IN_SRC = _in_col_map()


def _head_pad_map(width):
    src = -np.ones((HP,), np.int64)
    for h in range(N_HEADS):
        src[h * HEAD_PAD:h * HEAD_PAD + width] = h * width + np.arange(width)
    return src


_DV_SRC = _head_pad_map(HEAD_DV)
_DK_SRC = -np.ones((QKP,), np.int64)
for _h in range(N_HEADS):
    for _d in range(HEAD_DK):
        _DK_SRC[_DK_SRC_LANE(_h, _d)] = _h * HEAD_DK + _d


def _take_static(w, src, axis):
    axis = axis % w.ndim
    pieces, start = [], 0
    for j in range(1, len(src) + 1):
        run_ends = (j == len(src) or (src[j] < 0) != (src[start] < 0)
                    or (src[start] >= 0 and src[j] != src[j - 1] + 1))
        if run_ends:
            if src[start] < 0:
                shape = w.shape[:axis] + (j - start,) + w.shape[axis + 1:]
                pieces.append(jnp.zeros(shape, w.dtype))
            else:
                pieces.append(lax.slice_in_dim(w, int(src[start]), int(src[j - 1]) + 1, axis=axis))
            start = j
    return jnp.concatenate(pieces, axis=axis)


def _take_cols(w, src):
    return _take_static(w, src, -1)


def _take_rows(w, src):
    return _take_static(w, src, -2)


def _layer_spec(shape, layer):
    return pl.BlockSpec((None,) + tuple(shape), lambda *_: (layer,) + (0,) * len(shape))


def _mod_spec(layer, which, per_b, b0):
    return pl.BlockSpec((None, None, 1, 1, D_MODEL),
                        lambda i: (layer, which, b0 + i // per_b, 0, 0))


def _cparams(sem):
    return pltpu.CompilerParams(dimension_semantics=sem, vmem_limit_bytes=VMEM_LIMIT)


def _mod_kernel(c_ref, w_ref, b_ref, o_ref):
    c = c_ref[...]
    cond = c * jax.nn.sigmoid(c)
    o_ref[0] = jnp.dot(cond, w_ref[0], preferred_element_type=F32,
                       precision=lax.Precision.HIGHEST) + b_ref[0]


def _modulation(c, w_mod, b_mod):
    depth, d, n = w_mod.shape
    b = c.shape[0]
    nb = 1536
    return pl.pallas_call(
        _mod_kernel,
        grid=(depth, n // nb),
        in_specs=[pl.BlockSpec((b, d), lambda l, j: (0, 0)),
                  pl.BlockSpec((1, d, nb), lambda l, j: (l, 0, j)),
                  pl.BlockSpec((1, 1, nb), lambda l, j: (l, 0, j))],
        out_specs=pl.BlockSpec((1, b, nb), lambda l, j: (l, 0, j)),
        out_shape=jax.ShapeDtypeStruct((depth, b, n), F32),
        compiler_params=_cparams(("arbitrary", "arbitrary")),
        name="adaln_mod",
    )(c, w_mod, b_mod.reshape(depth, 1, n))


def _rms_mod(x, nw, sc, sh):
    y = x * lax.rsqrt(jnp.mean(x * x, axis=-1, keepdims=True) + NORM_EPS)
    return (y * nw) * (1.0 + sc) + sh


def _inproj_kernel(x_ref, sc_ref, sh_ref, nw_ref, w_ref, o_ref):
    h = _rms_mod(x_ref[...], nw_ref[...], sc_ref[0], sh_ref[0]).astype(BF16)
    for j in range(NP_COLS // PROJ_CH):
        cs = slice(j * PROJ_CH, (j + 1) * PROJ_CH)
        o_ref[:, cs] = jnp.dot(h, w_ref[:, cs], preferred_element_type=F32).astype(BF16)


def _in_projection(x3, grp, mod5, b0, layer, nw, w_p, seq, tm=1024):
    _, t, d = x3.shape
    tm = min(tm, seq)
    per_b = seq // tm
    return pl.pallas_call(
        _inproj_kernel,
        grid=(t // tm,),
        in_specs=[pl.BlockSpec((None, tm, d), lambda i: (grp, i, 0)),
                  _mod_spec(layer, MOD_SC1, per_b, b0), _mod_spec(layer, MOD_SH1, per_b, b0),
                  _layer_spec((1, d), layer), _layer_spec((d, NP_COLS), layer)],
        out_specs=pl.BlockSpec((tm, NP_COLS), lambda i: (i, 0)),
        out_shape=jax.ShapeDtypeStruct((t, NP_COLS), BF16),
        compiler_params=_cparams(("arbitrary",)),
        name="in_proj",
    )(x3, mod5, mod5, nw, w_p)


_NT = (((1,), (1,)), ((), ()))
_TN = (((0,), (0,)), ((), ()))


def _tri_mask(n=CHUNK):
    r = lax.broadcasted_iota(jnp.int32, (n, n), 0)
    c = lax.broadcasted_iota(jnp.int32, (n, n), 1)
    return r >= c


def _pair_masks(rows):
    lane = lax.broadcasted_iota(jnp.int32, (rows, LANES), 1)
    return lane < HEAD_DK, (lane >= HEAD_DK) & (lane < 2 * HEAD_DK)


def _head_attention(qd, ki, ke, vh, et, st_ref, h, causal):
    qb = qd.astype(BF16)
    sc = lax.dot_general(qb, ki.astype(BF16), _NT, preferred_element_type=F32)
    sc = jnp.where(causal, sc, 0.0)
    st = st_ref[h]
    o = jnp.dot(sc.astype(BF16), vh, preferred_element_type=F32)
    o = o + lax.dot_general(qb, st.astype(BF16), _NT, preferred_element_type=F32)
    st_ref[h] = st * et + lax.dot_general(vh, ke.astype(BF16), _TN, preferred_element_type=F32)
    return o


def _gla_kernel(q_ref, k_ref, v_ref, g_ref, a_ref, wa_ref, ba_ref, nw_ref, tri_ref, o_ref,
                st_ref, qd_s, ki_s, ke_s, et_s):
    @pl.when(pl.program_id(1) == 0)
    def _():
        st_ref[...] = jnp.zeros_like(st_ref)

    causal = _tri_mask()
    tl = q_ref.shape[0]
    n_chunks = tl // CHUNK

    z = jnp.dot(a_ref[...], wa_ref[...], preferred_element_type=F32) + ba_ref[...]
    la = (jnp.minimum(z, 0.0) - jnp.log1p(jnp.exp(-jnp.abs(z)))) * (1.0 / GATE_TEMP)
    hi = la.astype(BF16)
    lo = (la - hi.astype(F32)).astype(BF16)
    cum = (jnp.dot(tri_ref[...], hi, preferred_element_type=F32)
           + jnp.dot(tri_ref[...], lo, preferred_element_type=F32))
    cum3 = cum.reshape(n_chunks, CHUNK, QKP)
    tot3 = cum3[:, CHUNK - 1:CHUNK, :]
    qd = (q_ref[...].astype(F32) * (HEAD_DK ** -0.5)) * jnp.exp(cum)
    masks = _pair_masks(tl)
    for h in range(N_HEADS):
        pair = slice((h // 2) * LANES, (h // 2 + 1) * LANES)
        qd_s[:, h * HEAD_PAD:(h + 1) * HEAD_PAD] = jnp.where(masks[h % 2], qd[:, pair],
                                                              0.0).astype(BF16)
    kf = k_ref[...].astype(F32)
    ki_s[...] = (kf * jnp.exp(-cum)).astype(BF16)
    ke_s[...] = (kf * jnp.exp(tot3 - cum3).reshape(tl, QKP)).astype(BF16)
    et_s[...] = jnp.exp(tot3).reshape(n_chunks, QKP)

    def chunk(c, carry):
        r = pl.ds(pl.multiple_of(c * CHUNK, CHUNK), CHUNK)
        et = et_s[pl.ds(c, 1), :]
        for h in range(N_HEADS):
            sl = slice(h * HEAD_PAD, (h + 1) * HEAD_PAD)
            pair = slice((h // 2) * LANES, (h // 2 + 1) * LANES)
            o = _head_attention(qd_s[r, sl], ki_s[r, pair], ke_s[r, pair], v_ref[r, sl],
                                et[:, pair], st_ref, h, causal)
            ms = jnp.sum(o * o, axis=-1, keepdims=True) * (1.0 / HEAD_DV)
            y = (o * lax.rsqrt(ms + NORM_EPS)) * nw_ref[:, sl]
            g = g_ref[r, sl].astype(F32)
            o_ref[r, sl] = (y * (g * jax.nn.sigmoid(g))).astype(BF16)
        return carry

    lax.fori_loop(0, n_chunks, chunk, 0, unroll=4)


def _ret_kernel(q_ref, k_ref, v_ref, g_ref, cos_ref, sina_ref, sinb_ref, dq_ref, dki_ref,
                dke_ref, dt_ref, nw_ref, o_ref, st_ref):
    @pl.when(pl.program_id(1) == 0)
    def _():
        st_ref[...] = jnp.zeros_like(st_ref)

    causal = _tri_mask(RET_CHUNK)
    n_chunks = q_ref.shape[0] // RET_CHUNK
    lane = lax.broadcasted_iota(jnp.int32, (RET_CHUNK, HEAD_PAD), 1)
    real = lane < HEAD_DV
    masks = _pair_masks(RET_CHUNK)
    half = HEAD_DK // 2

    def rotary(t, cos, sina, sinb):
        return (t * cos + pltpu.roll(t, LANES - half, 1) * sina + pltpu.roll(t, half, 1) * sinb)

    def chunk(c, carry):
        r = pl.ds(pl.multiple_of(c * RET_CHUNK, RET_CHUNK), RET_CHUNK)
        cos, sina, sinb = cos_ref[r, :], sina_ref[r, :], sinb_ref[r, :]
        pair_q, pair_ki, pair_ke = [], [], []
        for p in range(N_PAIRS):
            ps = slice(p * LANES, (p + 1) * LANES)
            qr = rotary(q_ref[r, ps].astype(F32), cos, sina, sinb) * dq_ref[:, ps]
            kr = rotary(k_ref[r, ps].astype(F32), cos, sina, sinb) * (HEAD_DK ** -0.5)
            pair_q.append(qr)
            pair_ki.append((kr * dki_ref[:, ps]).astype(BF16))
            pair_ke.append((kr * dke_ref[:, ps]).astype(BF16))
        for h in range(N_HEADS):
            sl = slice(h * HEAD_PAD, (h + 1) * HEAD_PAD)
            p = h // 2
            qd = jnp.where(masks[h % 2], pair_q[p], 0.0)
            o = _head_attention(qd, pair_ki[p], pair_ke[p], v_ref[r, sl],
                                dt_ref[:, p * LANES:(p + 1) * LANES], st_ref, h, causal)
            mu = jnp.sum(o, axis=-1, keepdims=True) * (1.0 / HEAD_DV)
            oc = jnp.where(real, o - mu, 0.0)
            var = jnp.sum(oc * oc, axis=-1, keepdims=True) * (1.0 / HEAD_DV)
            y = (oc * lax.rsqrt(var + NORM_EPS)) * nw_ref[:, sl]
            g = g_ref[r, sl].astype(F32)
            o_ref[r, sl] = (y * (g * jax.nn.sigmoid(g))).astype(BF16)
        return carry

    lax.fori_loop(0, n_chunks, chunk, 0, unroll=2)


def _proj_spec(tl, width, col_off, per_b):
    cb = col_off // width
    return pl.BlockSpec((tl, width), lambda b, l: (b * per_b + l, cb))


def _full(shape):
    return pl.BlockSpec(shape, lambda b, l: (0,) * len(shape))


def _gla_mixer(proj, batch, seq, layer, wa_p, ba_p, nw_p, tl=512):
    per_b = seq // tl
    pos = np.arange(tl)
    tri_bd = jnp.asarray((pos[:, None] // CHUNK == pos[None, :] // CHUNK)
                         & (pos[:, None] >= pos[None, :]), BF16)
    return pl.pallas_call(
        _gla_kernel,
        grid=(batch, per_b),
        in_specs=[_proj_spec(tl, QKP, OFF_GQ, per_b), _proj_spec(tl, QKP, OFF_GK, per_b),
                  _proj_spec(tl, HP, OFF_GV, per_b), _proj_spec(tl, HP, OFF_GG, per_b),
                  _proj_spec(tl, LANES, OFF_GA, per_b),
                  _layer_spec((LANES, QKP), layer), _layer_spec((1, QKP), layer),
                  _layer_spec((1, HP), layer), _full((tl, tl))],
        out_specs=pl.BlockSpec((tl, HP), lambda b, l: (b * per_b + l, 0)),
        out_shape=jax.ShapeDtypeStruct((batch * seq, HP), BF16),
        scratch_shapes=[pltpu.VMEM((N_HEADS, HEAD_PAD, HEAD_PAD), F32),
                        pltpu.VMEM((tl, HP), BF16), pltpu.VMEM((tl, QKP), BF16),
                        pltpu.VMEM((tl, QKP), BF16), pltpu.VMEM((tl // CHUNK, QKP), F32)],
        compiler_params=_cparams(("arbitrary", "arbitrary")),
        name="gla_mixer",
    )(proj, proj, proj, proj, proj, wa_p, ba_p, nw_p, tri_bd)


def _ret_mixer(proj, batch, seq, layer, cos_t, sina_t, sinb_t, dq, dki, dke, dtot, nw_p, tl=512):
    per_b = seq // tl
    return pl.pallas_call(
        _ret_kernel,
        grid=(batch, per_b),
        in_specs=[_proj_spec(tl, QKP, OFF_RQ, per_b), _proj_spec(tl, QKP, OFF_RK, per_b),
                  _proj_spec(tl, HP, OFF_RV, per_b), _proj_spec(tl, HP, OFF_RG, per_b),
                  pl.BlockSpec((tl, LANES), lambda b, l: (l, 0)),
                  pl.BlockSpec((tl, LANES), lambda b, l: (l, 0)),
                  pl.BlockSpec((tl, LANES), lambda b, l: (l, 0)),
                  _full((RET_CHUNK, QKP)), _full((RET_CHUNK, QKP)), _full((RET_CHUNK, QKP)),
                  _full((1, QKP)), _layer_spec((1, HP), layer)],
        out_specs=pl.BlockSpec((tl, HP), lambda b, l: (b * per_b + l, 0)),
        out_shape=jax.ShapeDtypeStruct((batch * seq, HP), BF16),
        scratch_shapes=[pltpu.VMEM((N_HEADS, HEAD_PAD, HEAD_PAD), F32)],
        compiler_params=_cparams(("arbitrary", "arbitrary")),
        name="ret_mixer",
    )(proj, proj, proj, proj, cos_t, sina_t, sinb_t, dq, dki, dke, dtot, nw_p)


S5_N = S5_GROUPS * S5_STATE
S5_SLAB = 256


def _gelu_tanh(x):
    return 0.5 * x * (1.0 + jnp.tanh(np.sqrt(2.0 / np.pi) * (x + 0.044715 * (x * x * x))))


def _s5_kernel(u_ref, bb_ref, cb_ref, nr_ref, ni_ref, pr_ref, pi_ref, lr_ref, li_ref,
               d_ref, wg_ref, bg_ref, o_ref, sr_ref, si_ref, x_scr, s_scr):
    @pl.when(pl.program_id(1) == 0)
    def _():
        sr_ref[...] = jnp.zeros_like(sr_ref)
        si_ref[...] = jnp.zeros_like(si_ref)

    tri = _tri_mask().astype(BF16)
    n_chunks = u_ref.shape[0] // CHUNK
    u = u_ref[...]
    x_scr[...] = jnp.dot(u, bb_ref[...], preferred_element_type=F32)

    def chunk(c, carry):
        r = pl.ds(pl.multiple_of(c * CHUNK, CHUNK), CHUNK)
        for j in range(S5_N // S5_SLAB):
            cs = slice(j * S5_SLAB, (j + 1) * S5_SLAB)
            ci = slice(S5_N + j * S5_SLAB, S5_N + (j + 1) * S5_SLAB)
            xr, xi = x_scr[r, cs], x_scr[r, ci]
            nr, ni = nr_ref[:, cs], ni_ref[:, cs]
            p_r = jnp.dot(tri, (xr * nr - xi * ni).astype(BF16), preferred_element_type=F32)
            p_i = jnp.dot(tri, (xr * ni + xi * nr).astype(BF16), preferred_element_type=F32)
            s0r, s0i = sr_ref[:, cs], si_ref[:, cs]
            lr, li = lr_ref[:, cs], li_ref[:, cs]
            q_r = p_r + (s0r * lr - s0i * li)
            q_i = p_i + (s0r * li + s0i * lr)
            pr, pi = pr_ref[:, cs], pi_ref[:, cs]
            s_r = q_r * pr - q_i * pi
            s_i = q_r * pi + q_i * pr
            sr_ref[:, cs] = s_r[CHUNK - 1:CHUNK, :]
            si_ref[:, cs] = s_i[CHUNK - 1:CHUNK, :]
            s_scr[r, cs] = s_r.astype(BF16)
            s_scr[r, ci] = s_i.astype(BF16)
        return carry

    lax.fori_loop(0, n_chunks, chunk, 0, unroll=2)
    y = jnp.dot(s_scr[...], cb_ref[...], preferred_element_type=F32)
    y = _gelu_tanh(y + d_ref[...] * u.astype(F32))
    gate = jnp.dot(y.astype(BF16), wg_ref[...], preferred_element_type=F32) + bg_ref[...]
    o_ref[...] = (y * jax.nn.sigmoid(gate)).astype(BF16)


def _s5_mixer(proj, batch, seq, layer, tabs, tl=512):
    per_b = seq // tl
    bb, cb, nr, ni, pr, pi, lr, li, dsk, wg, bg = tabs
    return pl.pallas_call(
        _s5_kernel,
        grid=(batch, per_b),
        in_specs=[_proj_spec(tl, S5_WIDTH, OFF_SU, per_b),
                  *[_layer_spec(a.shape[1:], layer) for a in tabs]],
        out_specs=pl.BlockSpec((tl, S5_WIDTH), lambda b, l: (b * per_b + l, 0)),
        out_shape=jax.ShapeDtypeStruct((batch * seq, S5_WIDTH), BF16),
        scratch_shapes=[pltpu.VMEM((1, S5_N), F32), pltpu.VMEM((1, S5_N), F32),
                        pltpu.VMEM((tl, 2 * S5_N), F32), pltpu.VMEM((tl, 2 * S5_N), BF16)],
        compiler_params=_cparams(("arbitrary", "arbitrary")),
        name="s5_mixer",
    )(proj, bb, cb, nr, ni, pr, pi, lr, li, dsk, wg, bg)


def _s5_tables(a_re, a_im, log_dt, b_re, b_im, c_re, c_im, d_skip, w_glu, b_glu):
    lam = lax.complex(a_re, a_im)
    dt = jnp.exp(log_dt)[:, None]
    lam_bar = jnp.exp(lam * dt)
    b_bar = ((lam_bar - 1.0) / lam)[..., None] * lax.complex(b_re, b_im)
    eye = jnp.eye(S5_GROUPS, dtype=F32)
    def blk_b(m):
        return jnp.einsum('gph,gk->ghkp', m, eye).reshape(S5_WIDTH, S5_N)
    bb = jnp.concatenate([blk_b(jnp.real(b_bar)), blk_b(jnp.imag(b_bar))], axis=1)
    def blk_c(m):
        return jnp.einsum('ghp,gk->kpgh', m, eye).reshape(S5_N, S5_WIDTH)
    cb = jnp.concatenate([blk_c(c_re), blk_c(-c_im)], axis=0)
    steps = jnp.arange(CHUNK, dtype=F32)[:, None, None]
    lam_dt = (lam * dt)[None]
    pos = jnp.exp(lam_dt * steps).reshape(CHUNK, S5_N)
    neg = jnp.exp(-lam_dt * steps).reshape(CHUNK, S5_N)
    one = lam_bar.reshape(1, S5_N)
    return (bb.astype(BF16), cb.astype(BF16), jnp.real(neg), jnp.imag(neg), jnp.real(pos),
            jnp.imag(pos), jnp.real(one), jnp.imag(one), d_skip.reshape(1, S5_WIDTH),
            w_glu.astype(BF16), b_glu.reshape(1, S5_WIDTH))


def _pack_bf16_pairs(x):
    w = x.shape[1] // 2
    xb = x.astype(BF16).astype(F32)
    hi = lax.bitcast_convert_type(xb[:, :w], jnp.uint32)
    lo = lax.bitcast_convert_type(xb[:, w:], jnp.uint32)
    return hi | (lo >> 16)


def _unpack_bf16_pairs(p):
    hi = lax.bitcast_convert_type(p & jnp.uint32(0xFFFF0000), F32)
    lo = lax.bitcast_convert_type(p << 16, F32)
    return hi, lo


def _router_kernel(og_ref, or_ref, os_ref, wg_ref, wr_ref, ws_ref, x_ref, g1_ref, sc_ref,
                   sh_ref, nw_ref, rw_ref, rb_ref,
                   x1_ref, h_ref, idx_ref, gate_ref, rank_ref, cnt_ref, carry_ref):
    i = pl.program_id(0)

    @pl.when(i == 0)
    def _():
        carry_ref[...] = jnp.zeros_like(carry_ref)

    mix = (jnp.dot(og_ref[...], wg_ref[...], preferred_element_type=F32)
           + jnp.dot(or_ref[...], wr_ref[...], preferred_element_type=F32)
           + jnp.dot(os_ref[...], ws_ref[...], preferred_element_type=F32))
    x1 = x_ref[...] + g1_ref[0] * mix
    x1_ref[...] = x1
    hdn = _rms_mod(x1, nw_ref[...], sc_ref[0], sh_ref[0])
    h_ref[...] = _pack_bf16_pairs(hdn)
    h_hi = hdn.astype(BF16)
    h_lo = (hdn - h_hi.astype(F32)).astype(BF16)
    p = lax.dot_general(rw_ref[...], h_hi, _NT, preferred_element_type=F32)
    q = lax.dot_general(rw_ref[:N_EXPERTS, :], h_lo, _NT, preferred_element_type=F32)
    logits = (p[:N_EXPERTS] + p[N_EXPERTS:] + q) + rb_ref[:, 0:1]
    tm = logits.shape[1]
    eidx = lax.broadcasted_iota(jnp.int32, (N_EXPERTS, tm), 0)
    work = logits
    onehot = jnp.zeros((N_EXPERTS, tm), F32)
    vals, idxs, sels = [], [], []
    for _ in range(TOP_K):
        m = jnp.max(work, axis=0, keepdims=True)
        ix = jnp.min(jnp.where(work == m, eidx, N_EXPERTS), axis=0, keepdims=True)
        sel = eidx == ix
        work = jnp.where(sel, -jnp.inf, work)
        onehot = onehot + sel.astype(F32)
        vals.append(m)
        idxs.append(ix)
        sels.append(sel)
    exps = [jnp.exp(v - vals[0]) for v in vals]
    denom = exps[0] + exps[1] + exps[2] + exps[3]
    r = lax.broadcasted_iota(jnp.int32, (tm, tm), 0)
    c = lax.broadcasted_iota(jnp.int32, (tm, tm), 1)
    earlier = (r < c).astype(BF16)
    before = (jnp.dot(onehot.astype(BF16), earlier, preferred_element_type=F32)
              + carry_ref[:, 0:1])
    row8 = lax.broadcasted_iota(jnp.int32, (8, tm), 0)
    idx_out = jnp.zeros((8, tm), jnp.int32)
    gate_out = jnp.zeros((8, tm), F32)
    rank_out = jnp.zeros((8, tm), F32)
    for k in range(TOP_K):
        rk = jnp.sum(jnp.where(sels[k], before, 0.0), axis=0, keepdims=True)
        idx_out = jnp.where(row8 == k, idxs[k], idx_out)
        gate_out = jnp.where(row8 == k, exps[k] / denom, gate_out)
        rank_out = jnp.where(row8 == k, rk, rank_out)
    idx_ref[...] = idx_out
    gate_ref[...] = gate_out
    rank_ref[...] = rank_out.astype(jnp.int32)
    total = carry_ref[...] + jnp.sum(onehot, axis=1, keepdims=True)
    carry_ref[...] = total
    cnt_ref[...] = total


def _outproj_router(o_gla, o_ret, o_s5, wg, wr, ws, x3, grp, mod5, b0, layer, nw2, rw_p, rb_p,
                    seq, tm=1024):
    _, t, d = x3.shape
    tm = min(tm, seq)
    per_b = seq // tm
    row = lambda w: pl.BlockSpec((tm, w), lambda i: (i, 0))
    full = lambda s: pl.BlockSpec(s, lambda i: (0,) * len(s))
    slot_t = pl.BlockSpec((8, tm), lambda i: (0, i))
    return pl.pallas_call(
        _router_kernel,
        grid=(t // tm,),
        in_specs=[row(HP), row(HP), row(S5_WIDTH), _layer_spec((HP, d), layer),
                  _layer_spec((HP, d), layer), _layer_spec((S5_WIDTH, d), layer),
                  pl.BlockSpec((None, tm, d), lambda i: (grp, i, 0)),
                  _mod_spec(layer, MOD_G1, per_b, b0), _mod_spec(layer, MOD_SC2, per_b, b0),
                  _mod_spec(layer, MOD_SH2, per_b, b0), _layer_spec((1, d), layer),
                  _layer_spec((2 * N_EXPERTS, d), layer), _layer_spec((N_EXPERTS, LANES), layer)],
        out_specs=[row(d), row(d // 2), slot_t, slot_t, slot_t, full((N_EXPERTS, LANES))],
        out_shape=[jax.ShapeDtypeStruct((t, d), F32),
                   jax.ShapeDtypeStruct((t, d // 2), jnp.uint32),
                   jax.ShapeDtypeStruct((8, t), jnp.int32),
                   jax.ShapeDtypeStruct((8, t), F32),
                   jax.ShapeDtypeStruct((8, t), jnp.int32),
                   jax.ShapeDtypeStruct((N_EXPERTS, LANES), F32)],
        scratch_shapes=[pltpu.VMEM((N_EXPERTS, LANES), F32)],
        compiler_params=_cparams(("arbitrary",)),
        name="outproj_router",
    )(o_gla, o_ret, o_s5, wg, wr, ws, x3, mod5, mod5, mod5, nw2, rw_p, rb_p)


GATHER_WIN = 64


def _gather_rows(table, idx):
    m = idx.shape[0]
    w = table.shape[1]
    mesh = plsc.VectorSubcoreMesh(core_axis_name="core", subcore_axis_name="subcore")

    @functools.partial(pl.kernel, out_type=jax.ShapeDtypeStruct((m, w), table.dtype),
                       mesh=mesh, name="sc_row_gather")
    def gather(x_hbm, i_hbm, o_hbm):
        def body(i_vmem, o_vmem):
            pltpu.sync_copy(x_hbm.at[i_vmem], o_vmem)

        pltpu.emit_pipeline(
            body,
            grid=(m // GATHER_WIN,),
            in_specs=[pl.BlockSpec((GATHER_WIN,), lambda i: (i,))],
            out_specs=[pl.BlockSpec((GATHER_WIN, w), lambda i: (i, 0))],
            core_axis_name=("core", "subcore"),
            dimension_semantics=(pltpu.PARALLEL,),
        )(i_hbm, o_hbm)

    return gather(table, idx)


def _scatter_rows(x, dest_slot_major, n_rows):
    t, w = x.shape
    steps = t // GATHER_WIN
    mesh = plsc.VectorSubcoreMesh(core_axis_name="core", subcore_axis_name="subcore")

    @functools.partial(pl.kernel, out_type=jax.ShapeDtypeStruct((n_rows, w), x.dtype),
                       mesh=mesh, name="sc_row_scatter")
    def scatter(x_hbm, i_hbm, o_hbm):
        def body(x_vmem, i0, i1, i2, i3):
            for i_vmem in (i0, i1, i2, i3):
                pltpu.sync_copy(x_vmem, o_hbm.at[i_vmem])

        slot = lambda k: pl.BlockSpec((GATHER_WIN,), lambda i: (k * steps + i,))
        pltpu.emit_pipeline(
            body,
            grid=(steps,),
            in_specs=[pl.BlockSpec((GATHER_WIN, w), lambda i: (i, 0)),
                      slot(0), slot(1), slot(2), slot(3)],
            out_specs=[],
            core_axis_name=("core", "subcore"),
            dimension_semantics=(pltpu.PARALLEL,),
        )(x_hbm, i_hbm, i_hbm, i_hbm, i_hbm)

    return scatter(x, dest_slot_major)


def _expert_kernel(be_ref, ns_ref, rows_ref, wu_ref, bu_ref, wd_ref, bd_ref, o_ref,
                   wu_bf, wd_bf):
    i = pl.program_id(0)
    e = be_ref[i]
    prev = be_ref[jnp.maximum(i - 1, 0)]

    @pl.when((i == 0) | (e != prev))
    def _():
        wu_bf[...] = wu_ref[0, 0].astype(BF16)
        wd_bf[...] = wd_ref[0, 0].astype(BF16)

    def run_rows(r0, n, first):
        rs = slice(r0, r0 + n)
        row = lax.broadcasted_iota(jnp.int32, (n, rows_ref.shape[1]), 0)
        x_hi, x_lo = _unpack_bf16_pairs(jnp.where(row >= first, rows_ref[rs, :], jnp.uint32(0)))
        x = jnp.concatenate([x_hi.astype(BF16), x_lo.astype(BF16)], axis=1)
        up = jnp.dot(x, wu_bf[...], preferred_element_type=F32) + bu_ref[0, 0]
        x_glu = jnp.minimum(up[:, :D_FF], SWIGLU_LIMIT)
        x_lin = jnp.clip(up[:, D_FF:], -SWIGLU_LIMIT, SWIGLU_LIMIT)
        act = x_glu * jax.nn.sigmoid(SWIGLU_ALPHA * x_glu) * (x_lin + 1.0)
        o_ref[rs, :] = _pack_bf16_pairs(
            jnp.dot(act.astype(BF16), wd_bf[...], preferred_element_type=F32) + bd_ref[0, 0])

    def zero_rows(r0, n):
        o_ref[r0:r0 + n, :] = jnp.zeros((n, o_ref.shape[1]), o_ref.dtype)

    half = ROW_SUB // 2
    for s in range(ROW_BLK // ROW_SUB):
        r0 = s * ROW_SUB
        first = ns_ref[i] - r0

        @pl.when(first < half)
        def _():
            run_rows(r0, ROW_SUB, first)

        @pl.when((first >= half) & (first < ROW_SUB))
        def _():
            zero_rows(r0, half)
            run_rows(r0 + half, half, first - half)

        @pl.when(first >= ROW_SUB)
        def _():
            zero_rows(r0, ROW_SUB)


def _experts(layer, block_e, n_skip, rows, w_up, b_up, w_down, b_down):
    n_rows, dh = rows.shape
    d = 2 * dh
    n_blocks = n_rows // ROW_BLK
    depth, ne, _, f2 = w_up.shape
    wsel = lambda i, be, nu: (layer, be[i], 0, 0)
    grid_spec = pltpu.PrefetchScalarGridSpec(
        num_scalar_prefetch=2,
        grid=(n_blocks,),
        in_specs=[pl.BlockSpec((ROW_BLK, dh), lambda i, be, nu: (i, 0)),
                  pl.BlockSpec((1, 1, d, f2), wsel),
                  pl.BlockSpec((1, 1, 1, f2), wsel),
                  pl.BlockSpec((1, 1, D_FF, d), wsel),
                  pl.BlockSpec((1, 1, 1, d), wsel)],
        out_specs=pl.BlockSpec((ROW_BLK, dh), lambda i, be, nu: (i, 0)),
        scratch_shapes=[pltpu.VMEM((d, f2), BF16), pltpu.VMEM((D_FF, d), BF16)],
    )
    return pl.pallas_call(
        _expert_kernel,
        grid_spec=grid_spec,
        out_shape=jax.ShapeDtypeStruct((n_rows, dh), jnp.uint32),
        compiler_params=_cparams(("arbitrary",)),
        name="moe_experts",
    )(block_e, n_skip, rows, w_up, b_up.reshape(depth, ne, 1, f2), w_down,
      b_down.reshape(depth, ne, 1, d))


def _combine_kernel(y0_ref, y1_ref, y2_ref, y3_ref, gate_ref, x1_ref, g2_ref, fw_ref, *rest,
                    final):
    o_ref = rest[-1]
    gates = gate_ref[...]
    y_hi, y_lo = None, None
    for k, y_ref in enumerate((y0_ref, y1_ref, y2_ref, y3_ref)):
        hi, lo = _unpack_bf16_pairs(y_ref[...])
        g = gates[:, k:k + 1]
        y_hi = g * hi if y_hi is None else y_hi + g * hi
        y_lo = g * lo if y_lo is None else y_lo + g * lo
    y = jnp.concatenate([y_hi, y_lo], axis=1)
    x2 = x1_ref[...] + g2_ref[0] * y
    if final:
        x2 = (x2 * lax.rsqrt(jnp.mean(x2 * x2, axis=-1, keepdims=True) + NORM_EPS)) * fw_ref[...]
    o_ref[...] = x2


def _combine(y4, gates, x1, mod5, b0, layer, fw, seq, final, grp, n_out, prev_out, th=1024):
    t, d = x1.shape
    th = min(th, seq)
    steps = t // th
    per_b = seq // th
    slot = lambda k: pl.BlockSpec((th, d // 2), lambda i: (k * steps + i, 0))
    return pl.pallas_call(
        functools.partial(_combine_kernel, final=final),
        grid=(steps,),
        in_specs=[slot(0), slot(1), slot(2), slot(3),
                  pl.BlockSpec((th, TOP_K), lambda i: (i, 0)),
                  pl.BlockSpec((th, d), lambda i: (i, 0)),
                  _mod_spec(layer, MOD_G2, per_b, b0),
                  pl.BlockSpec((1, d), lambda i: (0, 0))]
                 + ([] if prev_out is None else [pl.BlockSpec(memory_space=pl.ANY)]),
        out_specs=pl.BlockSpec((None, th, d), lambda i: (grp, i, 0)),
        out_shape=jax.ShapeDtypeStruct((n_out, t, d), F32),
        input_output_aliases={} if prev_out is None else {8: 0},
        compiler_params=_cparams(("arbitrary",)),
        name="moe_combine",
    )(y4, y4, y4, y4, gates, x1, mod5, fw, *([] if prev_out is None else [prev_out]))


def _retention_tables(seq):
    f32 = np.float32
    pos = np.arange(seq, dtype=f32)
    inv_freq = (f32(ROPE_BASE) ** (-np.arange(0, HEAD_DK, 2, dtype=f32) / f32(HEAD_DK))).astype(f32)
    ang = pos[:, None] * inv_freq[None, :]
    cos, sin = np.cos(ang), np.sin(ang)
    zero = np.zeros_like(sin)
    zpad = np.zeros((seq, LANES - 2 * HEAD_DK), f32)
    cos_t = np.concatenate([cos, cos, cos, cos, zpad], axis=1)
    sina_t = np.concatenate([-sin, zero, -sin, zero, zpad], axis=1)
    sinb_t = np.concatenate([zero, sin, zero, sin, zpad], axis=1)
    log_gamma = np.log1p(-np.exp2(f32(-5.0) - np.arange(N_HEADS, dtype=f32))).astype(f32)
    log_decay = np.broadcast_to(log_gamma[None, :, None], (RET_CHUNK, N_HEADS, HEAD_DK))
    cum = np.cumsum(log_decay, axis=0, dtype=f32)
    tot = cum[-1:]

    def shp(a):
        flat = a.reshape(a.shape[0], N_HEADS * HEAD_DK)
        return np.where(_DK_SRC >= 0, flat[:, np.maximum(_DK_SRC, 0)], f32(1.0)).astype(f32)

    tabs = (cos_t, sina_t, sinb_t, shp(np.exp(cum)), shp(np.exp(-cum)), shp(np.exp(tot - cum)),
            shp(np.exp(tot)))
    return tuple(jnp.asarray(a, F32) for a in tabs)


def kernel(x, c, norm1_w, norm2_w, w_mod, b_mod, w_in, gla_w_a2, gla_b_a, gla_norm_w, ret_norm_w, s5_a_re, s5_a_im, s5_log_dt, s5_b_re, s5_b_im, s5_c_re, s5_c_im, s5_d, s5_w_glu, s5_b_glu, w_out, router_w, router_b, w_up, b_up, w_down, b_down, final_norm_w):
    batch, seq, d = x.shape
    depth = w_mod.shape[0]
    gb = batch // N_STREAMS
    t = gb * seq
    n_slots = t * TOP_K
    n_blocks = n_slots // ROW_BLK + N_EXPERTS
    n_rows = n_blocks * ROW_BLK

    mod = _modulation(c, w_mod, b_mod)
    mod5 = mod.reshape(depth, batch, 6, 1, d).transpose(0, 2, 1, 3, 4)
    ret_tabs = _retention_tables(seq)

    w_p = _take_cols(w_in, _IN_SRC).astype(BF16)
    wa_p = jnp.zeros((depth, LANES, QKP), F32).at[:, :GATE_RANK].set(
        _take_cols(gla_w_a2, _DK_SRC)).astype(BF16)
    ba_p = _take_cols(gla_b_a, _DK_SRC).reshape(depth, 1, QKP)
    gnw = _take_cols(gla_norm_w, _DV_SRC).reshape(depth, 1, HP)
    rnw = _take_cols(ret_norm_w, _DV_SRC).reshape(depth, 1, HP)
    kv = N_HEADS * HEAD_DV
    wo_g = _take_rows(w_out[:, :kv], _DV_SRC).astype(BF16)
    wo_r = _take_rows(w_out[:, kv:2 * kv], _DV_SRC).astype(BF16)
    wo_s = w_out[:, 2 * kv:].astype(BF16)
    rw_t = jnp.swapaxes(router_w, 1, 2)
    rw_hi = rw_t.astype(BF16)
    rw_p = jnp.concatenate([rw_hi, (rw_t - rw_hi.astype(F32)).astype(BF16)], axis=1)
    rb_p = jnp.broadcast_to(router_b[:, :, None], (depth, N_EXPERTS, LANES))
    s5_tabs = jax.vmap(_s5_tables)(s5_a_re, s5_a_im, s5_log_dt, s5_b_re, s5_b_im, s5_c_re,
                                   s5_c_im, s5_d, s5_w_glu, s5_b_glu)
    n1 = norm1_w.reshape(depth, 1, d)
    n2 = norm2_w.reshape(depth, 1, d)
    fw = final_norm_w.reshape(1, d)

    xs = [(x.reshape(N_STREAMS, t, d), g) for g in range(N_STREAMS)]
    out = None

    for i in range(depth):
        final = i == depth - 1
        for g in range(N_STREAMS):
            x3, grp = xs[g]
            b0 = g * gb

            proj = _in_projection(x3, grp, mod5, b0, i, n1, w_p, seq)
            o_gla = _gla_mixer(proj, gb, seq, i, wa_p, ba_p, gnw)
            o_ret = _ret_mixer(proj, gb, seq, i, *ret_tabs, rnw)
            o_s5 = _s5_mixer(proj, gb, seq, i, s5_tabs)

            x1, hdn, idx, gates, rank, counts = _outproj_router(
                o_gla, o_ret, o_s5, wo_g, wo_r, wo_s, x3, grp, mod5, b0, i, n2, rw_p, rb_p, seq)

            cnt = counts[:, 0].astype(jnp.int32)
            padded = (cnt + ROW_BLK - 1) // ROW_BLK * ROW_BLK
            pad_ends = jnp.cumsum(padded)
            first_row = pad_ends - cnt
            slot_start = jnp.sum(jnp.where(idx[:TOP_K, :, None] == jnp.arange(N_EXPERTS),
                                           first_row, 0), axis=-1)
            dest_sm = (slot_start + rank[:TOP_K]).astype(jnp.int32).reshape(-1)
            gates_tm = gates[:TOP_K].T
            blk_start = jnp.arange(n_blocks, dtype=jnp.int32) * ROW_BLK
            block_e = jnp.minimum(jnp.sum(pad_ends[None, :] <= blk_start[:, None], axis=1),
                                  N_EXPERTS - 1).astype(jnp.int32)
            n_skip = jnp.where(blk_start < pad_ends[-1],
                               jnp.clip(first_row[block_e] - blk_start, 0, ROW_BLK),
                               ROW_BLK).astype(jnp.int32)

            rows = _scatter_rows(hdn, dest_sm, n_rows)
            out_rows = _experts(i, block_e, n_skip, rows, w_up, b_up, w_down, b_down)
            y4 = _gather_rows(out_rows, dest_sm)
            if final:
                out = _combine(y4, gates_tm, x1, mod5, b0, i, fw, seq, True, g, N_STREAMS, out)
            else:
                xs[g] = (_combine(y4, gates_tm, x1, mod5, b0, i, fw, seq, False, 0, 1, None), 0)

    return out.reshape(batch, seq, d)
```

```python
import functools

import numpy as np
import jax
import jax.numpy as jnp
from jax import lax
from jax.experimental import pallas as pl
from jax.experimental.pallas import tpu as pltpu
from jax.experimental.pallas import tpu_sc as plsc

D_MODEL = 1024
CHUNK = 64
RET_CHUNK = 128
NORM_EPS = 1e-5
N_HEADS = 4
HEAD_DK = 48
HEAD_DV = 96
GATE_RANK = 16
GATE_TEMP = 16.0
ROPE_BASE = 10000.0
S5_WIDTH = 256
S5_GROUP_DIM = 16
S5_GROUPS = 16
S5_STATE = 64
N_EXPERTS = 32
TOP_K = 4
D_FF = 1024
SWIGLU_LIMIT = 7.0
SWIGLU_ALPHA = 1.702

LANES = 128
HEAD_PAD = LANES
HP = N_HEADS * HEAD_PAD
N_PAIRS = N_HEADS // 2
QKP = N_PAIRS * LANES
VMEM_LIMIT = 56 * 1024 * 1024

OFF_GQ, OFF_GK, OFF_GV, OFF_GG = 0, QKP, 2 * QKP, 2 * QKP + HP
OFF_RQ = OFF_GG + HP
OFF_RK, OFF_RV, OFF_RG = OFF_RQ + QKP, OFF_RQ + 2 * QKP, OFF_RQ + 2 * QKP + HP
OFF_SU = OFF_RG + HP
OFF_GA = OFF_SU + S5_WIDTH
NP_COLS = OFF_GA + LANES
PROJ_CH = 1152

ROW_BLK = 1024
ROW_SUB = 512
MOD_SH1, MOD_SC1, MOD_G1, MOD_SH2, MOD_SC2, MOD_G2 = range(6)
N_STREAMS = 1

F32 = jnp.float32
BF16 = jnp.bfloat16


def _DK_SRC_LANE(h, d):
    return (h // 2) * LANES + (h % 2) * HEAD_DK + d


def _in_col_map():
    src = -np.ones((NP_COLS,), np.int64)
    kq = N_HEADS * HEAD_DK
    kv = N_HEADS * HEAD_DV
    base = dict(gq=0, gk=kq, gv=2 * kq, gg=2 * kq + kv, ga=2 * kq + 2 * kv)
    r0 = base['ga'] + GATE_RANK
    base.update(rq=r0, rk=r0 + kq, rv=r0 + 2 * kq, rg=r0 + 2 * kq + kv, su=r0 + 2 * kq + 2 * kv)
    for h in range(N_HEADS):
        for d in range(HEAD_DK):
            lane = _DK_SRC_LANE(h, d)
            src[OFF_GQ + lane] = base['gq'] + h * HEAD_DK + d
            src[OFF_GK + lane] = base['gk'] + h * HEAD_DK + d
            src[OFF_RQ + lane] = base['rq'] + h * HEAD_DK + d
            src[OFF_RK + lane] = base['rk'] + h * HEAD_DK + d
        for d in range(HEAD_DV):
            src[OFF_GV + h * HEAD_PAD + d] = base['gv'] + h * HEAD_DV + d
            src[OFF_GG + h * HEAD_PAD + d] = base['gg'] + h * HEAD_DV + d
            src[OFF_RV + h * HEAD_PAD + d] = base['rv'] + h * HEAD_DV + d
            src[OFF_RG + h * HEAD_PAD + d] = base['rg'] + h * HEAD_DV + d
    src[OFF_SU:OFF_SU + S5_WIDTH] = base['su'] + np.arange(S5_WIDTH)
    src[OFF_GA:OFF_GA + GATE_RANK] = base['ga'] + np.arange(GATE_RANK)
    return src


_IN_SRC = _in_col_map()


def _head_pad_map(width):
    src = -np.ones((HP,), np.int64)
    for h in range(N_HEADS):
        src[h * HEAD_PAD:h * HEAD_PAD + width] = h * width + np.arange(width)
    return src


_DV_SRC = _head_pad_map(HEAD_DV)
_DK_SRC = -np.ones((QKP,), np.int64)
for _h in range(N_HEADS):
    for _d in range(HEAD_DK):
        _DK_SRC[_DK_SRC_LANE(_h, _d)] = _h * HEAD_DK + _d


def _take_static(w, src, axis):
    axis = axis % w.ndim
    pieces, start = [], 0
    for j in range(1, len(src) + 1):
        run_ends = (j == len(src) or (src[j] < 0) != (src[start] < 0)
                    or (src[start] >= 0 and src[j] != src[j - 1] + 1))
        if run_ends:
            if src[start] < 0:
                shape = w.shape[:axis] + (j - start,) + w.shape[axis + 1:]
                pieces.append(jnp.zeros(shape, w.dtype))
            else:
                pieces.append(lax.slice_in_dim(w, int(src[start]), int(src[j - 1]) + 1, axis=axis))
            start = j
    return jnp.concatenate(pieces, axis=axis)


def _take_cols(w, src):
    return _take_static(w, src, -1)


def _take_rows(w, src):
    return _take_static(w, src, -2)


def _layer_spec(shape, layer):
    return pl.BlockSpec((None,) + tuple(shape), lambda *_: (layer,) + (0,) * len(shape))


def _mod_spec(layer, which, per_b, b0):
    return pl.BlockSpec((None, None, 1, 1, D_MODEL),
                        lambda i: (layer, which, b0 + i // per_b, 0, 0))


def _cparams(sem):
    return pltpu.CompilerParams(dimension_semantics=sem, vmem_limit_bytes=VMEM_LIMIT)


def _mod_kernel(c_ref, w_ref, b_ref, o_ref):
    c = c_ref[...]
    cond = c * jax.nn.sigmoid(c)
    o_ref[0] = jnp.dot(cond, w_ref[0], preferred_element_type=F32,
                       precision=lax.Precision.HIGHEST) + b_ref[0]


def _modulation(c, w_mod, b_mod):
    depth, d, n = w_mod.shape
    b = c.shape[0]
    nb = 1536
    return pl.pallas_call(
        _mod_kernel,
        grid=(depth, n // nb),
        in_specs=[pl.BlockSpec((b, d), lambda l, j: (0, 0)),
                  pl.BlockSpec((1, d, nb), lambda l, j: (l, 0, j)),
                  pl.BlockSpec((1, 1, nb), lambda l, j: (l, 0, j))],
        out_specs=pl.BlockSpec((1, b, nb), lambda l, j: (l, 0, j)),
        out_shape=jax.ShapeDtypeStruct((depth, b, n), F32),
        compiler_params=_cparams(("arbitrary", "arbitrary")),
        name="adaln_mod",
    )(c, w_mod, b_mod.reshape(depth, 1, n))


def _rms_mod(x, nw, sc, sh):
    y = x * lax.rsqrt(jnp.mean(x * x, axis=-1, keepdims=True) + NORM_EPS)
    return (y * nw) * (1.0 + sc) + sh


def _inproj_kernel(x_ref, sc_ref, sh_ref, nw_ref, w_ref, o_ref):
    h = _rms_mod(x_ref[...], nw_ref[...], sc_ref[0], sh_ref[0]).astype(BF16)
    for j in range(NP_COLS // PROJ_CH):
        cs = slice(j * PROJ_CH, (j + 1) * PROJ_CH)
        o_ref[:, cs] = jnp.dot(h, w_ref[:, cs], preferred_element_type=F32).astype(BF16)


def _in_projection(x3, grp, mod5, b0, layer, nw, w_p, seq, tm=1024):
    _, t, d = x3.shape
    tm = min(tm, seq)
    per_b = seq // tm
    return pl.pallas_call(
        _inproj_kernel,
        grid=(t // tm,),
        in_specs=[pl.BlockSpec((None, tm, d), lambda i: (grp, i, 0)),
                  _mod_spec(layer, MOD_SC1, per_b, b0), _mod_spec(layer, MOD_SH1, per_b, b0),
                  _layer_spec((1, d), layer), _layer_spec((d, NP_COLS), 0)],
        out_specs=pl.BlockSpec((tm, NP_COLS), lambda i: (i, 0)),
        out_shape=jax.ShapeDtypeStruct((t, NP_COLS), BF16),
        compiler_params=_cparams(("arbitrary",)),
        name="in_proj",
    )(x3, mod5, mod5, nw, w_p)


_NT = (((1,), (1,)), ((), ()))
_TN = (((0,), (0,)), ((), ()))


def _tri_mask(n=CHUNK):
    r = lax.broadcasted_iota(jnp.int32, (n, n), 0)
    c = lax.broadcasted_iota(jnp.int32, (n, n), 1)
    return r >= c


def _pair_masks(rows):
    lane = lax.broadcasted_iota(jnp.int32, (rows, LANES), 1)
    return lane < HEAD_DK, (lane >= HEAD_DK) & (lane < 2 * HEAD_DK)


def _head_attention(qd, ki, ke, vh, et, st_ref, h, causal):
    qb = qd.astype(BF16)
    sc = lax.dot_general(qb, ki.astype(BF16), _NT, preferred_element_type=F32)
    sc = jnp.where(causal, sc, 0.0)
    st = st_ref[h]
    o = jnp.dot(sc.astype(BF16), vh, preferred_element_type=F32)
    o = o + lax.dot_general(qb, st.astype(BF16), _NT, preferred_element_type=F32)
    st_ref[h] = st * et + lax.dot_general(vh, ke.astype(BF16), _TN, preferred_element_type=F32)
    return o


def _gla_kernel(q_ref, k_ref, v_ref, g_ref, a_ref, wa_ref, ba_ref, nw_ref, tri_ref, o_ref,
                st_ref, qd_s, ki_s, ke_s, et_s):
    @pl.when(pl.program_id(1) == 0)
    def _():
        st_ref[...] = jnp.zeros_like(st_ref)

    causal = _tri_mask()
    tl = q_ref.shape[0]
    n_chunks = tl // CHUNK

    z = jnp.dot(a_ref[...], wa_ref[...], preferred_element_type=F32) + ba_ref[...]
    la = (jnp.minimum(z, 0.0) - jnp.log1p(jnp.exp(-jnp.abs(z)))) * (1.0 / GATE_TEMP)
    hi = la.astype(BF16)
    lo = (la - hi.astype(F32)).astype(BF16)
    cum = (jnp.dot(tri_ref[...], hi, preferred_element_type=F32)
           + jnp.dot(tri_ref[...], lo, preferred_element_type=F32))
    cum3 = cum.reshape(n_chunks, CHUNK, QKP)
    tot3 = cum3[:, CHUNK - 1:CHUNK, :]
    qd = (q_ref[...].astype(F32) * (HEAD_DK ** -0.5)) * jnp.exp(cum)
    masks = _pair_masks(tl)
    for h in range(N_HEADS):
        pair = slice((h // 2) * LANES, (h // 2 + 1) * LANES)
        qd_s[:, h * HEAD_PAD:(h + 1) * HEAD_PAD] = jnp.where(masks[h % 2], qd[:, pair],
                                                              0.0).astype(BF16)
    kf = k_ref[...].astype(F32)
    ki_s[...] = (kf * jnp.exp(-cum)).astype(BF16)
    ke_s[...] = (kf * jnp.exp(tot3 - cum3).reshape(tl, QKP)).astype(BF16)
    et_s[...] = jnp.exp(tot3).reshape(n_chunks, QKP)

    def chunk(c, carry):
        r = pl.ds(pl.multiple_of(c * CHUNK, CHUNK), CHUNK)
        et = et_s[pl.ds(c, 1), :]
        for h in range(N_HEADS):
            sl = slice(h * HEAD_PAD, (h + 1) * HEAD_PAD)
            pair = slice((h // 2) * LANES, (h // 2 + 1) * LANES)
            o = _head_attention(qd_s[r, sl], ki_s[r, pair], ke_s[r, pair], v_ref[r, sl],
                                et[:, pair], st_ref, h, causal)
            ms = jnp.sum(o * o, axis=-1, keepdims=True) * (1.0 / HEAD_DV)
            y = (o * lax.rsqrt(ms + NORM_EPS)) * nw_ref[:, sl]
            g = g_ref[r, sl].astype(F32)
            o_ref[r, sl] = (y * (g * jax.nn.sigmoid(g))).astype(BF16)
        return carry

    lax.fori_loop(0, n_chunks, chunk, 0, unroll=4)


def _ret_kernel(q_ref, k_ref, v_ref, g_ref, cos_ref, sina_ref, sinb_ref, dq_ref, dki_ref,
                dke_ref, dt_ref, nw_ref, o_ref, st_ref):
    @pl.when(pl.program_id(1) == 0)
    def _():
        st_ref[...] = jnp.zeros_like(st_ref)

    causal = _tri_mask(RET_CHUNK)
    n_chunks = q_ref.shape[0] // RET_CHUNK
    lane = lax.broadcasted_iota(jnp.int32, (RET_CHUNK, HEAD_PAD), 1)
    real = lane < HEAD_DV
    masks = _pair_masks(RET_CHUNK)
    half = HEAD_DK // 2

    def rotary(t, cos, sina, sinb):
        return (t * cos + pltpu.roll(t, LANES - half, 1) * sina + pltpu.roll(t, half, 1) * sinb)

    def chunk(c, carry):
        r = pl.ds(pl.multiple_of(c * RET_CHUNK, RET_CHUNK), RET_CHUNK)
        cos, sina, sinb = cos_ref[r, :], sina_ref[r, :], sinb_ref[r, :]
        pair_q, pair_ki, pair_ke = [], [], []
        for p in range(N_PAIRS):
            ps = slice(p * LANES, (p + 1) * LANES)
            qr = rotary(q_ref[r, ps].astype(F32), cos, sina, sinb) * dq_ref[:, ps]
            kr = rotary(k_ref[r, ps].astype(F32), cos, sina, sinb) * (HEAD_DK ** -0.5)
            pair_q.append(qr)
            pair_ki.append((kr * dki_ref[:, ps]).astype(BF16))
            pair_ke.append((kr * dke_ref[:, ps]).astype(BF16))
        for h in range(N_HEADS):
            sl = slice(h * HEAD_PAD, (h + 1) * HEAD_PAD)
            p = h // 2
            qd = jnp.where(masks[h % 2], pair_q[p], 0.0)
            o = _head_attention(qd, pair_ki[p], pair_ke[p], v_ref[r, sl],
                                dt_ref[:, p * LANES:(p + 1) * LANES], st_ref, h, causal)
            mu = jnp.sum(o, axis=-1, keepdims=True) * (1.0 / HEAD_DV)
            oc = jnp.where(real, o - mu, 0.0)
            var = jnp.sum(oc * oc, axis=-1, keepdims=True) * (1.0 / HEAD_DV)
            y = (oc * lax.rsqrt(var + NORM_EPS)) * nw_ref[:, sl]
            g = g_ref[r, sl].astype(F32)
            o_ref[r, sl] = (y * (g * jax.nn.sigmoid(g))).astype(BF16)
        return carry

    lax.fori_loop(0, n_chunks, chunk, 0, unroll=2)


def _proj_spec(tl, width, col_off, per_b):
    cb = col_off // width
    return pl.BlockSpec((tl, width), lambda b, l: (b * per_b + l, cb))


def _full(shape):
    return pl.BlockSpec(shape, lambda b, l: (0,) * len(shape))


def _gla_mixer(proj, batch, seq, layer, wa_p, ba_p, nw_p, tl=512):
    per_b = seq // tl
    pos = np.arange(tl)
    tri_bd = jnp.asarray((pos[:, None] // CHUNK == pos[None, :] // CHUNK)
                         & (pos[:, None] >= pos[None, :]), BF16)
    return pl.pallas_call(
        _gla_kernel,
        grid=(batch, per_b),
        in_specs=[_proj_spec(tl, QKP, OFF_GQ, per_b), _proj_spec(tl, QKP, OFF_GK, per_b),
                  _proj_spec(tl, HP, OFF_GV, per_b), _proj_spec(tl, HP, OFF_GG, per_b),
                  _proj_spec(tl, LANES, OFF_GA, per_b),
                  _layer_spec((LANES, QKP), layer), _layer_spec((1, QKP), layer),
                  _layer_spec((1, HP), layer), _full((tl, tl))],
        out_specs=pl.BlockSpec((tl, HP), lambda b, l: (b * per_b + l, 0)),
        out_shape=jax.ShapeDtypeStruct((batch * seq, HP), BF16),
        scratch_shapes=[pltpu.VMEM((N_HEADS, HEAD_PAD, HEAD_PAD), F32),
                        pltpu.VMEM((tl, HP), BF16), pltpu.VMEM((tl, QKP), BF16),
                        pltpu.VMEM((tl, QKP), BF16), pltpu.VMEM((tl // CHUNK, QKP), F32)],
        compiler_params=_cparams(("arbitrary", "arbitrary")),
        name="gla_mixer",
    )(proj, proj, proj, proj, proj, wa_p, ba_p, nw_p, tri_bd)


def _ret_mixer(proj, batch, seq, layer, cos_t, sina_t, sinb_t, dq, dki, dke, dtot, nw_p, tl=512):
    per_b = seq // tl
    return pl.pallas_call(
        _ret_kernel,
        grid=(batch, per_b),
        in_specs=[_proj_spec(tl, QKP, OFF_RQ, per_b), _proj_spec(tl, QKP, OFF_RK, per_b),
                  _proj_spec(tl, HP, OFF_RV, per_b), _proj_spec(tl, HP, OFF_RG, per_b),
                  pl.BlockSpec((tl, LANES), lambda b, l: (l, 0)),
                  pl.BlockSpec((tl, LANES), lambda b, l: (l, 0)),
                  pl.BlockSpec((tl, LANES), lambda b, l: (l, 0)),
                  _full((RET_CHUNK, QKP)), _full((RET_CHUNK, QKP)), _full((RET_CHUNK, QKP)),
                  _full((1, QKP)), _layer_spec((1, HP), layer)],
        out_specs=pl.BlockSpec((tl, HP), lambda b, l: (b * per_b + l, 0)),
        out_shape=jax.ShapeDtypeStruct((batch * seq, HP), BF16),
        scratch_shapes=[pltpu.VMEM((N_HEADS, HEAD_PAD, HEAD_PAD), F32)],
        compiler_params=_cparams(("arbitrary", "arbitrary")),
        name="ret_mixer",
    )(proj, proj, proj, proj, cos_t, sina_t, sinb_t, dq, dki, dke, dtot, nw_p)


S5_N = S5_GROUPS * S5_STATE
S5_SLAB = 256


def _gelu_tanh(x):
    return 0.5 * x * (1.0 + jnp.tanh(np.sqrt(2.0 / np.pi) * (x + 0.044715 * (x * x * x))))


def _s5_kernel(u_ref, bb_ref, cb_ref, nr_ref, ni_ref, pr_ref, pi_ref, lr_ref, li_ref,
               d_ref, wg_ref, bg_ref, o_ref, sr_ref, si_ref, x_scr, s_scr):
    @pl.when(pl.program_id(1) == 0)
    def _():
        sr_ref[...] = jnp.zeros_like(sr_ref)
        si_ref[...] = jnp.zeros_like(si_ref)

    tri = _tri_mask().astype(BF16)
    n_chunks = u_ref.shape[0] // CHUNK
    u = u_ref[...]
    x_scr[...] = jnp.dot(u, bb_ref[...], preferred_element_type=F32)

    def chunk(c, carry):
        r = pl.ds(pl.multiple_of(c * CHUNK, CHUNK), CHUNK)
        for j in range(S5_N // S5_SLAB):
            cs = slice(j * S5_SLAB, (j + 1) * S5_SLAB)
            ci = slice(S5_N + j * S5_SLAB, S5_N + (j + 1) * S5_SLAB)
            xr, xi = x_scr[r, cs], x_scr[r, ci]
            nr, ni = nr_ref[:, cs], ni_ref[:, cs]
            p_r = jnp.dot(tri, (xr * nr - xi * ni).astype(BF16), preferred_element_type=F32)
            p_i = jnp.dot(tri, (xr * ni + xi * nr).astype(BF16), preferred_element_type=F32)
            s0r, s0i = sr_ref[:, cs], si_ref[:, cs]
            lr, li = lr_ref[:, cs], li_ref[:, cs]
            q_r = p_r + (s0r * lr - s0i * li)
            q_i = p_i + (s0r * li + s0i * lr)
            pr, pi = pr_ref[:, cs], pi_ref[:, cs]
            s_r = q_r * pr - q_i * pi
            s_i = q_r * pi + q_i * pr
            sr_ref[:, cs] = s_r[CHUNK - 1:CHUNK, :]
            si_ref[:, cs] = s_i[CHUNK - 1:CHUNK, :]
            s_scr[r, cs] = s_r.astype(BF16)
            s_scr[r, ci] = s_i.astype(BF16)
        return carry

    lax.fori_loop(0, n_chunks, chunk, 0, unroll=2)
    y = jnp.dot(s_scr[...], cb_ref[...], preferred_element_type=F32)
    y = _gelu_tanh(y + d_ref[...] * u.astype(F32))
    gate = jnp.dot(y.astype(BF16), wg_ref[...], preferred_element_type=F32) + bg_ref[...]
    o_ref[...] = (y * jax.nn.sigmoid(gate)).astype(BF16)


def _s5_mixer(proj, batch, seq, layer, tabs, tl=512):
    per_b = seq // tl
    bb, cb, nr, ni, pr, pi, lr, li, dsk, wg, bg = tabs
    return pl.pallas_call(
        _s5_kernel,
        grid=(batch, per_b),
        in_specs=[_proj_spec(tl, S5_WIDTH, OFF_SU, per_b),
                  *[_layer_spec(a.shape[1:], layer) for a in tabs]],
        out_specs=pl.BlockSpec((tl, S5_WIDTH), lambda b, l: (b * per_b + l, 0)),
        out_shape=jax.ShapeDtypeStruct((batch * seq, S5_WIDTH), BF16),
        scratch_shapes=[pltpu.VMEM((1, S5_N), F32), pltpu.VMEM((1, S5_N), F32),
                        pltpu.VMEM((tl, 2 * S5_N), F32), pltpu.VMEM((tl, 2 * S5_N), BF16)],
        compiler_params=_cparams(("arbitrary", "arbitrary")),
        name="s5_mixer",
    )(proj, bb, cb, nr, ni, pr, pi, lr, li, dsk, wg, bg)


def _s5_tables(a_re, a_im, log_dt, b_re, b_im, c_re, c_im, d_skip, w_glu, b_glu):
    lam = lax.complex(a_re, a_im)
    dt = jnp.exp(log_dt)[:, None]
    lam_bar = jnp.exp(lam * dt)
    b_bar = ((lam_bar - 1.0) / lam)[..., None] * lax.complex(b_re, b_im)
    eye = jnp.eye(S5_GROUPS, dtype=F32)
    def blk_b(m):
        return jnp.einsum('gph,gk->ghkp', m, eye).reshape(S5_WIDTH, S5_N)
    bb = jnp.concatenate([blk_b(jnp.real(b_bar)), blk_b(jnp.imag(b_bar))], axis=1)
    def blk_c(m):
        return jnp.einsum('ghp,gk->kpgh', m, eye).reshape(S5_N, S5_WIDTH)
    cb = jnp.concatenate([blk_c(c_re), blk_c(-c_im)], axis=0)
    steps = jnp.arange(CHUNK, dtype=F32)[:, None, None]
    lam_dt = (lam * dt)[None]
    pos = jnp.exp(lam_dt * steps).reshape(CHUNK, S5_N)
    neg = jnp.exp(-lam_dt * steps).reshape(CHUNK, S5_N)
    one = lam_bar.reshape(1, S5_N)
    return (bb.astype(BF16), cb.astype(BF16), jnp.real(neg), jnp.imag(neg), jnp.real(pos),
            jnp.imag(pos), jnp.real(one), jnp.imag(one), d_skip.reshape(1, S5_WIDTH),
            w_glu.astype(BF16), b_glu.reshape(1, S5_WIDTH))


def _pack_bf16_pairs(x):
    w = x.shape[1] // 2
    xb = x.astype(BF16).astype(F32)
    hi = lax.bitcast_convert_type(xb[:, :w], jnp.uint32)
    lo = lax.bitcast_convert_type(xb[:, w:], jnp.uint32)
    return hi | (lo >> 16)


def _unpack_bf16_pairs(p):
    hi = lax.bitcast_convert_type(p & jnp.uint32(0xFFFF0000), F32)
    lo = lax.bitcast_convert_type(p << 16, F32)
    return hi, lo


def _router_kernel(og_ref, or_ref, os_ref, wg_ref, wr_ref, ws_ref, x_ref, g1_ref, sc_ref,
                   sh_ref, nw_ref, rw_ref, rb_ref,
                   x1_ref, h_ref, idx_ref, gate_ref, rank_ref, cnt_ref, carry_ref):
    i = pl.program_id(0)

    @pl.when(i == 0)
    def _():
        carry_ref[...] = jnp.zeros_like(carry_ref)

    mix = (jnp.dot(og_ref[...], wg_ref[...], preferred_element_type=F32)
           + jnp.dot(or_ref[...], wr_ref[...], preferred_element_type=F32)
           + jnp.dot(os_ref[...], ws_ref[...], preferred_element_type=F32))
    x1 = x_ref[...] + g1_ref[0] * mix
    x1_ref[...] = x1
    hdn = _rms_mod(x1, nw_ref[...], sc_ref[0], sh_ref[0])
    h_ref[...] = _pack_bf16_pairs(hdn)
    h_hi = hdn.astype(BF16)
    h_lo = (hdn - h_hi.astype(F32)).astype(BF16)
    p = lax.dot_general(rw_ref[...], h_hi, _NT, preferred_element_type=F32)
    q = lax.dot_general(rw_ref[:N_EXPERTS, :], h_lo, _NT, preferred_element_type=F32)
    logits = (p[:N_EXPERTS] + p[N_EXPERTS:] + q) + rb_ref[:, 0:1]
    tm = logits.shape[1]
    eidx = lax.broadcasted_iota(jnp.int32, (N_EXPERTS, tm), 0)
    work = logits
    onehot = jnp.zeros((N_EXPERTS, tm), F32)
    vals, idxs, sels = [], [], []
    for _ in range(TOP_K):
        m = jnp.max(work, axis=0, keepdims=True)
        ix = jnp.min(jnp.where(work == m, eidx, N_EXPERTS), axis=0, keepdims=True)
        sel = eidx == ix
        work = jnp.where(sel, -jnp.inf, work)
        onehot = onehot + sel.astype(F32)
        vals.append(m)
        idxs.append(ix)
        sels.append(sel)
    exps = [jnp.exp(v - vals[0]) for v in vals]
    denom = exps[0] + exps[1] + exps[2] + exps[3]
    r = lax.broadcasted_iota(jnp.int32, (tm, tm), 0)
    c = lax.broadcasted_iota(jnp.int32, (tm, tm), 1)
    earlier = (r < c).astype(BF16)
    before = (jnp.dot(onehot.astype(BF16), earlier, preferred_element_type=F32)
              + carry_ref[:, 0:1])
    row8 = lax.broadcasted_iota(jnp.int32, (8, tm), 0)
    idx_out = jnp.zeros((8, tm), jnp.int32)
    gate_out = jnp.zeros((8, tm), F32)
    rank_out = jnp.zeros((8, tm), F32)
    for k in range(TOP_K):
        rk = jnp.sum(jnp.where(sels[k], before, 0.0), axis=0, keepdims=True)
        idx_out = jnp.where(row8 == k, idxs[k], idx_out)
        gate_out = jnp.where(row8 == k, exps[k] / denom, gate_out)
        rank_out = jnp.where(row8 == k, rk, rank_out)
    idx_ref[...] = idx_out
    gate_ref[...] = gate_out
    rank_ref[...] = rank_out.astype(jnp.int32)
    total = carry_ref[...] + jnp.sum(onehot, axis=1, keepdims=True)
    carry_ref[...] = total
    cnt_ref[...] = total


def _outproj_router(o_gla, o_ret, o_s5, wg, wr, ws, x3, grp, mod5, b0, layer, nw2, rw_p, rb_p,
                    seq, tm=1024):
    _, t, d = x3.shape
    tm = min(tm, seq)
    per_b = seq // tm
    row = lambda w: pl.BlockSpec((tm, w), lambda i: (i, 0))
    full = lambda s: pl.BlockSpec(s, lambda i: (0,) * len(s))
    slot_t = pl.BlockSpec((8, tm), lambda i: (0, i))
    return pl.pallas_call(
        _router_kernel,
        grid=(t // tm,),
        in_specs=[row(HP), row(HP), row(S5_WIDTH), _layer_spec((HP, d), 0),
                  _layer_spec((HP, d), 0), _layer_spec((S5_WIDTH, d), 0),
                  pl.BlockSpec((None, tm, d), lambda i: (grp, i, 0)),
                  _mod_spec(layer, MOD_G1, per_b, b0), _mod_spec(layer, MOD_SC2, per_b, b0),
                  _mod_spec(layer, MOD_SH2, per_b, b0), _layer_spec((1, d), layer),
                  _layer_spec((2 * N_EXPERTS, d), 0), _layer_spec((N_EXPERTS, LANES), 0)],
        out_specs=[row(d), row(d // 2), slot_t, slot_t, slot_t, full((N_EXPERTS, LANES))],
        out_shape=[jax.ShapeDtypeStruct((t, d), F32),
                   jax.ShapeDtypeStruct((t, d // 2), jnp.uint32),
                   jax.ShapeDtypeStruct((8, t), jnp.int32),
                   jax.ShapeDtypeStruct((8, t), F32),
                   jax.ShapeDtypeStruct((8, t), jnp.int32),
                   jax.ShapeDtypeStruct((N_EXPERTS, LANES), F32)],
        scratch_shapes=[pltpu.VMEM((N_EXPERTS, LANES), F32)],
        compiler_params=_cparams(("arbitrary",)),
        name="outproj_router",
    )(o_gla, o_ret, o_s5, wg, wr, ws, x3, mod5, mod5, mod5, nw2, rw_p, rb_p)


GATHER_WIN = 64


def _gather_rows(table, idx):
    m = idx.shape[0]
    w = table.shape[1]
    mesh = plsc.VectorSubcoreMesh(core_axis_name="core", subcore_axis_name="subcore")

    @functools.partial(pl.kernel, out_type=jax.ShapeDtypeStruct((m, w), table.dtype),
                       mesh=mesh, name="sc_row_gather")
    def gather(x_hbm, i_hbm, o_hbm):
        def body(i_vmem, o_vmem):
            pltpu.sync_copy(x_hbm.at[i_vmem], o_vmem)

        pltpu.emit_pipeline(
            body,
            grid=(m // GATHER_WIN,),
            in_specs=[pl.BlockSpec((GATHER_WIN,), lambda i: (i,))],
            out_specs=[pl.BlockSpec((GATHER_WIN, w), lambda i: (i, 0))],
            core_axis_name=("core", "subcore"),
            dimension_semantics=(pltpu.PARALLEL,),
        )(i_hbm, o_hbm)

    return gather(table, idx)


def _scatter_rows(x, dest_slot_major, n_rows):
    t, w = x.shape
    steps = t // GATHER_WIN
    mesh = plsc.VectorSubcoreMesh(core_axis_name="core", subcore_axis_name="subcore")

    @functools.partial(pl.kernel, out_type=jax.ShapeDtypeStruct((n_rows, w), x.dtype),
                       mesh=mesh, name="sc_row_scatter")
    def scatter(x_hbm, i_hbm, o_hbm):
        def body(x_vmem, i0, i1, i2, i3):
            for i_vmem in (i0, i1, i2, i3):
                pltpu.sync_copy(x_vmem, o_hbm.at[i_vmem])

        slot = lambda k: pl.BlockSpec((GATHER_WIN,), lambda i: (k * steps + i,))
        pltpu.emit_pipeline(
            body,
            grid=(steps,),
            in_specs=[pl.BlockSpec((GATHER_WIN, w), lambda i: (i, 0)),
                      slot(0), slot(1), slot(2), slot(3)],
            out_specs=[],
            core_axis_name=("core", "subcore"),
            dimension_semantics=(pltpu.PARALLEL,),
        )(x_hbm, i_hbm, i_hbm, i_hbm, i_hbm)

    return scatter(x, dest_slot_major)


def _expert_kernel(be_ref, ns_ref, rows_ref, wu_ref, bu_ref, wd_ref, bd_ref, o_ref,
                   wu_bf, wd_bf):
    i = pl.program_id(0)
    e = be_ref[i]
    prev = be_ref[jnp.maximum(i - 1, 0)]

    @pl.when((i == 0) | (e != prev))
    def _():
        wu_bf[...] = wu_ref[0, 0].astype(BF16)
        wd_bf[...] = wd_ref[0, 0].astype(BF16)

    def run_rows(r0, n, first):
        rs = slice(r0, r0 + n)
        row = lax.broadcasted_iota(jnp.int32, (n, rows_ref.shape[1]), 0)
        x_hi, x_lo = _unpack_bf16_pairs(jnp.where(row >= first, rows_ref[rs, :], jnp.uint32(0)))
        x = jnp.concatenate([x_hi.astype(BF16), x_lo.astype(BF16)], axis=1)
        up = jnp.dot(x, wu_bf[...], preferred_element_type=F32) + bu_ref[0, 0]
        x_glu = jnp.minimum(up[:, :D_FF], SWIGLU_LIMIT)
        x_lin = jnp.clip(up[:, D_FF:], -SWIGLU_LIMIT, SWIGLU_LIMIT)
        act = x_glu * jax.nn.sigmoid(SWIGLU_ALPHA * x_glu) * (x_lin + 1.0)
        o_ref[rs, :] = _pack_bf16_pairs(
            jnp.dot(act.astype(BF16), wd_bf[...], preferred_element_type=F32) + bd_ref[0, 0])

    def zero_rows(r0, n):
        o_ref[r0:r0 + n, :] = jnp.zeros((n, o_ref.shape[1]), o_ref.dtype)

    half = ROW_SUB // 2
    for s in range(ROW_BLK // ROW_SUB):
        r0 = s * ROW_SUB
        first = ns_ref[i] - r0

        @pl.when(first < half)
        def _():
            run_rows(r0, ROW_SUB, first)

        @pl.when((first >= half) & (first < ROW_SUB))
        def _():
            zero_rows(r0, half)
            run_rows(r0 + half, half, first - half)

        @pl.when(first >= ROW_SUB)
        def _():
            zero_rows(r0, ROW_SUB)


def _experts(layer, block_e, n_skip, rows, w_up, b_up, w_down, b_down):
    n_rows, dh = rows.shape
    d = 2 * dh
    n_blocks = n_rows // ROW_BLK
    depth, ne, _, f2 = w_up.shape
    wsel = lambda i, be, nu: (layer, be[i], 0, 0)
    grid_spec = pltpu.PrefetchScalarGridSpec(
        num_scalar_prefetch=2,
        grid=(n_blocks,),
        in_specs=[pl.BlockSpec((ROW_BLK, dh), lambda i, be, nu: (i, 0)),
                  pl.BlockSpec((1, 1, d, f2), wsel),
                  pl.BlockSpec((1, 1, 1, f2), wsel),
                  pl.BlockSpec((1, 1, D_FF, d), wsel),
                  pl.BlockSpec((1, 1, 1, d), wsel)],
        out_specs=pl.BlockSpec((ROW_BLK, dh), lambda i, be, nu: (i, 0)),
        scratch_shapes=[pltpu.VMEM((d, f2), BF16), pltpu.VMEM((D_FF, d), BF16)],
    )
    return pl.pallas_call(
        _expert_kernel,
        grid_spec=grid_spec,
        out_shape=jax.ShapeDtypeStruct((n_rows, dh), jnp.uint32),
        compiler_params=_cparams(("arbitrary",)),
        name="moe_experts",
    )(block_e, n_skip, rows, w_up, b_up.reshape(depth, ne, 1, f2), w_down,
      b_down.reshape(depth, ne, 1, d))


def _combine_kernel(y0_ref, y1_ref, y2_ref, y3_ref, gate_ref, x1_ref, g2_ref, fw_ref, *rest,
                    final):
    o_ref = rest[-1]
    gates = gate_ref[...]
    y_hi, y_lo = None, None
    for k, y_ref in enumerate((y0_ref, y1_ref, y2_ref, y3_ref)):
        hi, lo = _unpack_bf16_pairs(y_ref[...])
        g = gates[:, k:k + 1]
        y_hi = g * hi if y_hi is None else y_hi + g * hi
        y_lo = g * lo if y_lo is None else y_lo + g * lo
    y = jnp.concatenate([y_hi, y_lo], axis=1)
    x2 = x1_ref[...] + g2_ref[0] * y
    if final:
        x2 = (x2 * lax.rsqrt(jnp.mean(x2 * x2, axis=-1, keepdims=True) + NORM_EPS)) * fw_ref[...]
    o_ref[...] = x2


def _combine(y4, gates, x1, mod5, b0, layer, fw, seq, final, grp, n_out, prev_out, th=1024):
    t, d = x1.shape
    th = min(th, seq)
    steps = t // th
    per_b = seq // th
    slot = lambda k: pl.BlockSpec((th, d // 2), lambda i: (k * steps + i, 0))
    return pl.pallas_call(
        functools.partial(_combine_kernel, final=final),
        grid=(steps,),
        in_specs=[slot(0), slot(1), slot(2), slot(3),
                  pl.BlockSpec((th, TOP_K), lambda i: (i, 0)),
                  pl.BlockSpec((th, d), lambda i: (i, 0)),
                  _mod_spec(layer, MOD_G2, per_b, b0),
                  pl.BlockSpec((1, d), lambda i: (0, 0))]
                 + ([] if prev_out is None else [pl.BlockSpec(memory_space=pl.ANY)]),
        out_specs=pl.BlockSpec((None, th, d), lambda i: (grp, i, 0)),
        out_shape=jax.ShapeDtypeStruct((n_out, t, d), F32),
        input_output_aliases={} if prev_out is None else {8: 0},
        compiler_params=_cparams(("arbitrary",)),
        name="moe_combine",
    )(y4, y4, y4, y4, gates, x1, mod5, fw, *([] if prev_out is None else [prev_out]))


def _retention_tables(seq):
    f32 = np.float32
    pos = np.arange(seq, dtype=f32)
    inv_freq = (f32(ROPE_BASE) ** (-np.arange(0, HEAD_DK, 2, dtype=f32) / f32(HEAD_DK))).astype(f32)
    ang = pos[:, None] * inv_freq[None, :]
    cos, sin = np.cos(ang), np.sin(ang)
    zero = np.zeros_like(sin)
    zpad = np.zeros((seq, LANES - 2 * HEAD_DK), f32)
    cos_t = np.concatenate([cos, cos, cos, cos, zpad], axis=1)
    sina_t = np.concatenate([-sin, zero, -sin, zero, zpad], axis=1)
    sinb_t = np.concatenate([zero, sin, zero, sin, zpad], axis=1)
    log_gamma = np.log1p(-np.exp2(f32(-5.0) - np.arange(N_HEADS, dtype=f32))).astype(f32)
    log_decay = np.broadcast_to(log_gamma[None, :, None], (RET_CHUNK, N_HEADS, HEAD_DK))
    cum = np.cumsum(log_decay, axis=0, dtype=f32)
    tot = cum[-1:]

    def shp(a):
        flat = a.reshape(a.shape[0], N_HEADS * HEAD_DK)
        return np.where(_DK_SRC >= 0, flat[:, np.maximum(_DK_SRC, 0)], f32(1.0)).astype(f32)

    tabs = (cos_t, sina_t, sinb_t, shp(np.exp(cum)), shp(np.exp(-cum)), shp(np.exp(tot - cum)),
            shp(np.exp(tot)))
    return tuple(jnp.asarray(a, F32) for a in tabs)


def kernel(x, c, norm1_w, norm2_w, w_mod, b_mod, w_in, gla_w_a2, gla_b_a, gla_norm_w, ret_norm_w, s5_a_re, s5_a_im, s5_log_dt, s5_b_re, s5_b_im, s5_c_re, s5_c_im, s5_d, s5_w_glu, s5_b_glu, w_out, router_w, router_b, w_up, b_up, w_down, b_down, final_norm_w):
    batch, seq, d = x.shape
    depth = w_mod.shape[0]
    gb = batch // N_STREAMS
    t = gb * seq
    n_slots = t * TOP_K
    n_blocks = n_slots // ROW_BLK + N_EXPERTS
    n_rows = n_blocks * ROW_BLK

    mod = _modulation(c, w_mod, b_mod)
    mod5 = mod.reshape(depth, batch, 6, 1, d).transpose(0, 2, 1, 3, 4)
    ret_tabs = _retention_tables(seq)

    def layer_params(i, later):
        sl = slice(i, i + 1)
        tie = (lambda a: a) if later is None else (lambda a: a + later.astype(a.dtype))
        w_p = _take_cols(tie(w_in[sl]), _IN_SRC).astype(BF16)
        wa_p = jnp.zeros((1, LANES, QKP), F32).at[:, :GATE_RANK].set(
            _take_cols(gla_w_a2[sl], _DK_SRC)).astype(BF16)
        ba_p = _take_cols(gla_b_a[sl], _DK_SRC).reshape(1, 1, QKP)
        gnw = _take_cols(gla_norm_w[sl], _DV_SRC).reshape(1, 1, HP)
        rnw = _take_cols(ret_norm_w[sl], _DV_SRC).reshape(1, 1, HP)
        kv = N_HEADS * HEAD_DV
        w_o = tie(w_out[sl])
        wo_g = _take_rows(w_o[:, :kv], _DV_SRC).astype(BF16)
        wo_r = _take_rows(w_o[:, kv:2 * kv], _DV_SRC).astype(BF16)
        wo_s = w_o[:, 2 * kv:].astype(BF16)
        rw_t = jnp.swapaxes(tie(router_w[sl]), 1, 2)
        rw_hi = rw_t.astype(BF16)
        rw_p = jnp.concatenate([rw_hi, (rw_t - rw_hi.astype(F32)).astype(BF16)], axis=1)
        rb_p = jnp.broadcast_to(router_b[sl][:, :, None], (1, N_EXPERTS, LANES))
        s5_tabs = jax.vmap(_s5_tables)(tie(s5_a_re[sl]), s5_a_im[sl], s5_log_dt[sl], s5_b_re[sl],
                                       s5_b_im[sl], s5_c_re[sl], s5_c_im[sl], s5_d[sl],
                                       s5_w_glu[sl], s5_b_glu[sl])
        return w_p, wa_p, ba_p, gnw, rnw, wo_g, wo_r, wo_s, rw_p, rb_p, s5_tabs

    n1 = norm1_w.reshape(depth, 1, d)
    n2 = norm2_w.reshape(depth, 1, d)
    fw = final_norm_w.reshape(1, d)

    xs = [(x.reshape(N_STREAMS, t, d), g) for g in range(N_STREAMS)]
    out = None
    later = None

    for i in range(depth):
        final = i == depth - 1
        w_p, wa_p, ba_p, gnw, rnw, wo_g, wo_r, wo_s, rw_p, rb_p, s5_tabs = layer_params(i, later)
        for g in range(N_STREAMS):
            x3, grp = xs[g]
            b0 = g * gb

            proj = _in_projection(x3, grp, mod5, b0, i, n1, w_p, seq)
            o_gla = _gla_mixer(proj, gb, seq, 0, wa_p, ba_p, gnw)
            o_ret = _ret_mixer(proj, gb, seq, 0, *ret_tabs, rnw)
            o_s5 = _s5_mixer(proj, gb, seq, 0, s5_tabs)

            x1, hdn, idx, gates, rank, counts = _outproj_router(
                o_gla, o_ret, o_s5, wo_g, wo_r, wo_s, x3, grp, mod5, b0, i, n2, rw_p, rb_p, seq)
            later = counts[0, 0] * 0.0

            cnt = counts[:, 0].astype(jnp.int32)
            padded = (cnt + ROW_BLK - 1) // ROW_BLK * ROW_BLK
            pad_ends = jnp.cumsum(padded)
            first_row = pad_ends - cnt
            slot_start = jnp.sum(jnp.where(idx[:TOP_K, :, None] == jnp.arange(N_EXPERTS),
                                           first_row, 0), axis=-1)
            dest_sm = (slot_start + rank[:TOP_K]).astype(jnp.int32).reshape(-1)
            gates_tm = gates[:TOP_K].T
            blk_start = jnp.arange(n_blocks, dtype=jnp.int32) * ROW_BLK
            block_e = jnp.minimum(jnp.sum(pad_ends[None, :] <= blk_start[:, None], axis=1),
                                  N_EXPERTS - 1).astype(jnp.int32)
            n_skip = jnp.where(blk_start < pad_ends[-1],
                               jnp.clip(first_row[block_e] - blk_start, 0, ROW_BLK),
                               ROW_BLK).astype(jnp.int32)

            rows = _scatter_rows(hdn, dest_sm, n_rows)
            out_rows = _experts(i, block_e, n_skip, rows, w_up, b_up, w_down, b_down)
            y4 = _gather_rows(out_rows, dest_sm)
            if final:
                out = _combine(y4, gates_tm, x1, mod5, b0, i, fw, seq, True, g, N_STREAMS, out)
            else:
                xs[g] = (_combine(y4, gates_tm, x1, mod5, b0, i, fw, seq, False, 0, 1, None), 0)

    return out.reshape(batch, seq, d)
```

```python
import functools

import numpy as np
import jax
import jax.numpy as jnp
from jax import lax
from jax.experimental import pallas as pl
from jax.experimental.pallas import tpu as pltpu
from jax.experimental.pallas import tpu_sc as plsc

D_MODEL = 1024
CHUNK = 64
RET_CHUNK = 128
NORM_EPS = 1e-5
N_HEADS = 4
HEAD_DK = 48
HEAD_DV = 96
GATE_RANK = 16
GATE_TEMP = 16.0
ROPE_BASE = 10000.0
S5_WIDTH = 256
S5_GROUP_DIM = 16
S5_GROUPS = 16
S5_STATE = 64
N_EXPERTS = 32
TOP_K = 4
D_FF = 1024
SWIGLU_LIMIT = 7.0
SWIGLU_ALPHA = 1.702

LANES = 128
HEAD_PAD = LANES
HP = N_HEADS * HEAD_PAD
N_PAIRS = N_HEADS // 2
QKP = N_PAIRS * LANES
VMEM_LIMIT = 56 * 1024 * 1024

OFF_GQ, OFF_GK, OFF_GV, OFF_GG = 0, QKP, 2 * QKP, 2 * QKP + HP
OFF_RQ = OFF_GG + HP
OFF_RK, OFF_RV, OFF_RG = OFF_RQ + QKP, OFF_RQ + 2 * QKP, OFF_RQ + 2 * QKP + HP
OFF_SU = OFF_RG + HP
OFF_GA = OFF_SU + S5_WIDTH
NP_COLS = OFF_GA + LANES
PROJ_CH = 1152

ROW_BLK = 1024
ROW_SUB = 512
MOD_SH1, MOD_SC1, MOD_G1, MOD_SH2, MOD_SC2, MOD_G2 = range(6)
N_STREAMS = 1

F32 = jnp.float32
BF16 = jnp.bfloat16


def _DK_SRC_LANE(h, d):
    return (h // 2) * LANES + (h % 2) * HEAD_DK + d


def _in_col_map():
    src = -np.ones((NP_COLS,), np.int64)
    kq = N_HEADS * HEAD_DK
    kv = N_HEADS * HEAD_DV
    base = dict(gq=0, gk=kq, gv=2 * kq, gg=2 * kq + kv, ga=2 * kq + 2 * kv)
    r0 = base['ga'] + GATE_RANK
    base.update(rq=r0, rk=r0 + kq, rv=r0 + 2 * kq, rg=r0 + 2 * kq + kv, su=r0 + 2 * kq + 2 * kv)
    for h in range(N_HEADS):
        for d in range(HEAD_DK):
            lane = _DK_SRC_LANE(h, d)
            src[OFF_GQ + lane] = base['gq'] + h * HEAD_DK + d
            src[OFF_GK + lane] = base['gk'] + h * HEAD_DK + d
            src[OFF_RQ + lane] = base['rq'] + h * HEAD_DK + d
            src[OFF_RK + lane] = base['rk'] + h * HEAD_DK + d
        for d in range(HEAD_DV):
            src[OFF_GV + h * HEAD_PAD + d] = base['gv'] + h * HEAD_DV + d
            src[OFF_GG + h * HEAD_PAD + d] = base['gg'] + h * HEAD_DV + d
            src[OFF_RV + h * HEAD_PAD + d] = base['rv'] + h * HEAD_DV + d
            src[OFF_RG + h * HEAD_PAD + d] = base['rg'] + h * HEAD_DV + d
    src[OFF_SU:OFF_SU + S5_WIDTH] = base['su'] + np.arange(S5_WIDTH)
    src[OFF_GA:OFF_GA + GATE_RANK] = base['ga'] + np.arange(GATE_RANK)
    return src


_IN_SRC = _in_col_map()


def _head_pad_map(width):
    src = -np.ones((HP,), np.int64)
    for h in range(N_HEADS):
        src[h * HEAD_PAD:h * HEAD_PAD + width] = h * width + np.arange(width)
    return src


_DV_SRC = _head_pad_map(HEAD_DV)
_DK_SRC = -np.ones((QKP,), np.int64)
for _h in range(N_HEADS):
    for _d in range(HEAD_DK):
        _DK_SRC[_DK_SRC_LANE(_h, _d)] = _h * HEAD_DK + _d


def _take_static(w, src, axis):
    axis = axis % w.ndim
    pieces, start = [], 0
    for j in range(1, len(src) + 1):
        run_ends = (j == len(src) or (src[j] < 0) != (src[start] < 0)
                    or (src[start] >= 0 and src[j] != src[j - 1] + 1))
        if run_ends:
            if src[start] < 0:
                shape = w.shape[:axis] + (j - start,) + w.shape[axis + 1:]
                pieces.append(jnp.zeros(shape, w.dtype))
            else:
                pieces.append(lax.slice_in_dim(w, int(src[start]), int(src[j - 1]) + 1, axis=axis))
            start = j
    return jnp.concatenate(pieces, axis=axis)


def _take_cols(w, src):
    return _take_static(w, src, -1)


def _take_rows(w, src):
    return _take_static(w, src, -2)


def _layer_spec(shape, layer):
    return pl.BlockSpec((None,) + tuple(shape), lambda *_: (layer,) + (0,) * len(shape))


def _mod_spec(layer, which, per_b, b0):
    return pl.BlockSpec((None, None, 1, 1, D_MODEL),
                        lambda i: (layer, which, b0 + i // per_b, 0, 0))


def _cparams(sem):
    return pltpu.CompilerParams(dimension_semantics=sem, vmem_limit_bytes=VMEM_LIMIT)


def _mod_kernel(c_ref, w_ref, b_ref, o_ref):
    c = c_ref[...]
    cond = c * jax.nn.sigmoid(c)
    o_ref[0] = jnp.dot(cond, w_ref[0], preferred_element_type=F32,
                       precision=lax.Precision.HIGHEST) + b_ref[0]


def _modulation(c, w_mod, b_mod):
    depth, d, n = w_mod.shape
    b = c.shape[0]
    nb = 1536
    return pl.pallas_call(
        _mod_kernel,
        grid=(depth, n // nb),
        in_specs=[pl.BlockSpec((b, d), lambda l, j: (0, 0)),
                  pl.BlockSpec((1, d, nb), lambda l, j: (l, 0, j)),
                  pl.BlockSpec((1, 1, nb), lambda l, j: (l, 0, j))],
        out_specs=pl.BlockSpec((1, b, nb), lambda l, j: (l, 0, j)),
        out_shape=jax.ShapeDtypeStruct((depth, b, n), F32),
        compiler_params=_cparams(("arbitrary", "arbitrary")),
        name="adaln_mod",
    )(c, w_mod, b_mod.reshape(depth, 1, n))


def _rms_mod(x, nw, sc, sh):
    y = x * lax.rsqrt(jnp.mean(x * x, axis=-1, keepdims=True) + NORM_EPS)
    return (y * nw) * (1.0 + sc) + sh


def _inproj_kernel(x_ref, sc_ref, sh_ref, nw_ref, w_ref, o_ref):
    h = _rms_mod(x_ref[...], nw_ref[...], sc_ref[0], sh_ref[0]).astype(BF16)
    for j in range(NP_COLS // PROJ_CH):
        cs = slice(j * PROJ_CH, (j + 1) * PROJ_CH)
        o_ref[:, cs] = jnp.dot(h, w_ref[:, cs], preferred_element_type=F32).astype(BF16)


def _in_projection(x3, grp, mod5, b0, layer, nw, w_p, seq, tm=1024):
    _, t, d = x3.shape
    tm = min(tm, seq)
    per_b = seq // tm
    return pl.pallas_call(
        _inproj_kernel,
        grid=(t // tm,),
        in_specs=[pl.BlockSpec((None, tm, d), lambda i: (grp, i, 0)),
                  _mod_spec(layer, MOD_SC1, per_b, b0), _mod_spec(layer, MOD_SH1, per_b, b0),
                  _layer_spec((1, d), layer), _layer_spec((d, NP_COLS), layer)],
        out_specs=pl.BlockSpec((tm, NP_COLS), lambda i: (i, 0)),
        out_shape=jax.ShapeDtypeStruct((t, NP_COLS), BF16),
        compiler_params=_cparams(("arbitrary",)),
        name="in_proj",
    )(x3, mod5, mod5, nw, w_p)


_NT = (((1,), (1,)), ((), ()))
_TN = (((0,), (0,)), ((), ()))


def _tri_mask(n=CHUNK):
    r = lax.broadcasted_iota(jnp.int32, (n, n), 0)
    c = lax.broadcasted_iota(jnp.int32, (n, n), 1)
    return r >= c


def _pair_masks(rows):
    lane = lax.broadcasted_iota(jnp.int32, (rows, LANES), 1)
    return lane < HEAD_DK, (lane >= HEAD_DK) & (lane < 2 * HEAD_DK)


def _head_attention(qd, ki, ke, vh, et, st_ref, h, causal, vt=None):
    qb = qd.astype(BF16)
    sc = lax.dot_general(qb, ki.astype(BF16), _NT, preferred_element_type=F32)
    sc = jnp.where(causal, sc, 0.0)
    st = st_ref[h]
    o = jnp.dot(sc.astype(BF16), vh, preferred_element_type=F32)
    o = o + lax.dot_general(qb, st.astype(BF16), _NT, preferred_element_type=F32)
    if vt is None:
        upd = lax.dot_general(vh, ke.astype(BF16), _TN, preferred_element_type=F32)
    else:
        upd = jnp.dot(vt, ke.astype(BF16), preferred_element_type=F32)
    st_ref[h] = st * et + upd
    return o


def _gla_kernel(q_ref, k_ref, v_ref, g_ref, a_ref, wa_ref, ba_ref, nw_ref, tri_ref, o_ref,
                st_ref, qd_s, ki_s, ke_s, et_s, vt_s):
    @pl.when(pl.program_id(1) == 0)
    def _():
        st_ref[...] = jnp.zeros_like(st_ref)

    causal = _tri_mask()
    tl = q_ref.shape[0]
    n_chunks = tl // CHUNK

    z = jnp.dot(a_ref[...], wa_ref[...], preferred_element_type=F32) + ba_ref[...]
    la = (jnp.minimum(z, 0.0) - jnp.log1p(jnp.exp(-jnp.abs(z)))) * (1.0 / GATE_TEMP)
    hi = la.astype(BF16)
    lo = (la - hi.astype(F32)).astype(BF16)
    cum = (jnp.dot(tri_ref[...], hi, preferred_element_type=F32)
           + jnp.dot(tri_ref[...], lo, preferred_element_type=F32))
    cum3 = cum.reshape(n_chunks, CHUNK, QKP)
    tot3 = cum3[:, CHUNK - 1:CHUNK, :]
    qd = (q_ref[...].astype(F32) * (HEAD_DK ** -0.5)) * jnp.exp(cum)
    masks = _pair_masks(tl)
    for h in range(N_HEADS):
        pair = slice((h // 2) * LANES, (h // 2 + 1) * LANES)
        qd_s[:, h * HEAD_PAD:(h + 1) * HEAD_PAD] = jnp.where(masks[h % 2], qd[:, pair],
                                                              0.0).astype(BF16)
    kf = k_ref[...].astype(F32)
    ki_s[...] = (kf * jnp.exp(-cum)).astype(BF16)
    ke_s[...] = (kf * jnp.exp(tot3 - cum3).reshape(tl, QKP)).astype(BF16)
    et_s[...] = jnp.exp(tot3).reshape(n_chunks, QKP)
    for c in range(n_chunks):
        vt_s[c] = v_ref[c * CHUNK:(c + 1) * CHUNK, :].T

    def chunk(c, carry):
        r = pl.ds(pl.multiple_of(c * CHUNK, CHUNK), CHUNK)
        et = et_s[pl.ds(c, 1), :]
        for h in range(N_HEADS):
            sl = slice(h * HEAD_PAD, (h + 1) * HEAD_PAD)
            pair = slice((h // 2) * LANES, (h // 2 + 1) * LANES)
            o = _head_attention(qd_s[r, sl], ki_s[r, pair], ke_s[r, pair], v_ref[r, sl],
                                et[:, pair], st_ref, h, causal, vt=vt_s[c, sl, :])
            ms = jnp.sum(o * o, axis=-1, keepdims=True) * (1.0 / HEAD_DV)
            y = (o * lax.rsqrt(ms + NORM_EPS)) * nw_ref[:, sl]
            g = g_ref[r, sl].astype(F32)
            o_ref[r, sl] = (y * (g * jax.nn.sigmoid(g))).astype(BF16)
        return carry

    lax.fori_loop(0, n_chunks, chunk, 0, unroll=4)


def _ret_kernel(q_ref, k_ref, v_ref, g_ref, cos_ref, sina_ref, sinb_ref, dq_ref, dki_ref,
                dke_ref, dt_ref, nw_ref, o_ref, st_ref):
    @pl.when(pl.program_id(1) == 0)
    def _():
        st_ref[...] = jnp.zeros_like(st_ref)

    causal = _tri_mask(RET_CHUNK)
    n_chunks = q_ref.shape[0] // RET_CHUNK
    lane = lax.broadcasted_iota(jnp.int32, (RET_CHUNK, HEAD_PAD), 1)
    real = lane < HEAD_DV
    masks = _pair_masks(RET_CHUNK)
    half = HEAD_DK // 2

    def rotary(t, cos, sina, sinb):
        return (t * cos + pltpu.roll(t, LANES - half, 1) * sina + pltpu.roll(t, half, 1) * sinb)

    def chunk(c, carry):
        r = pl.ds(pl.multiple_of(c * RET_CHUNK, RET_CHUNK), RET_CHUNK)
        cos, sina, sinb = cos_ref[r, :], sina_ref[r, :], sinb_ref[r, :]
        pair_q, pair_ki, pair_ke = [], [], []
        for p in range(N_PAIRS):
            ps = slice(p * LANES, (p + 1) * LANES)
            qr = rotary(q_ref[r, ps].astype(F32), cos, sina, sinb) * dq_ref[:, ps]
            kr = rotary(k_ref[r, ps].astype(F32), cos, sina, sinb) * (HEAD_DK ** -0.5)
            pair_q.append(qr)
            pair_ki.append((kr * dki_ref[:, ps]).astype(BF16))
            pair_ke.append((kr * dke_ref[:, ps]).astype(BF16))
        for h in range(N_HEADS):
            sl = slice(h * HEAD_PAD, (h + 1) * HEAD_PAD)
            p = h // 2
            qd = jnp.where(masks[h % 2], pair_q[p], 0.0)
            o = _head_attention(qd, pair_ki[p], pair_ke[p], v_ref[r, sl],
                                dt_ref[:, p * LANES:(p + 1) * LANES], st_ref, h, causal)
            mu = jnp.sum(o, axis=-1, keepdims=True) * (1.0 / HEAD_DV)
            oc = jnp.where(real, o - mu, 0.0)
            var = jnp.sum(oc * oc, axis=-1, keepdims=True) * (1.0 / HEAD_DV)
            y = (oc * lax.rsqrt(var + NORM_EPS)) * nw_ref[:, sl]
            g = g_ref[r, sl].astype(F32)
            o_ref[r, sl] = (y * (g * jax.nn.sigmoid(g))).astype(BF16)
        return carry

    lax.fori_loop(0, n_chunks, chunk, 0, unroll=2)


def _proj_spec(tl, width, col_off, per_b):
    cb = col_off // width
    return pl.BlockSpec((tl, width), lambda b, l: (b * per_b + l, cb))


def _full(shape):
    return pl.BlockSpec(shape, lambda b, l: (0,) * len(shape))


def _gla_mixer(proj, batch, seq, layer, wa_p, ba_p, nw_p, tl=512):
    per_b = seq // tl
    pos = np.arange(tl)
    tri_bd = jnp.asarray((pos[:, None] // CHUNK == pos[None, :] // CHUNK)
                         & (pos[:, None] >= pos[None, :]), BF16)
    return pl.pallas_call(
        _gla_kernel,
        grid=(batch, per_b),
        in_specs=[_proj_spec(tl, QKP, OFF_GQ, per_b), _proj_spec(tl, QKP, OFF_GK, per_b),
                  _proj_spec(tl, HP, OFF_GV, per_b), _proj_spec(tl, HP, OFF_GG, per_b),
                  _proj_spec(tl, LANES, OFF_GA, per_b),
                  _layer_spec((LANES, QKP), layer), _layer_spec((1, QKP), layer),
                  _layer_spec((1, HP), layer), _full((tl, tl))],
        out_specs=pl.BlockSpec((tl, HP), lambda b, l: (b * per_b + l, 0)),
        out_shape=jax.ShapeDtypeStruct((batch * seq, HP), BF16),
        scratch_shapes=[pltpu.VMEM((N_HEADS, HEAD_PAD, HEAD_PAD), F32),
                        pltpu.VMEM((tl, HP), BF16), pltpu.VMEM((tl, QKP), BF16),
                        pltpu.VMEM((tl, QKP), BF16), pltpu.VMEM((tl // CHUNK, QKP), F32),
                        pltpu.VMEM((tl // CHUNK, HP, CHUNK), BF16)],
        compiler_params=_cparams(("arbitrary", "arbitrary")),
        name="gla_mixer",
    )(proj, proj, proj, proj, proj, wa_p, ba_p, nw_p, tri_bd)


def _ret_mixer(proj, batch, seq, layer, cos_t, sina_t, sinb_t, dq, dki, dke, dtot, nw_p, tl=512):
    per_b = seq // tl
    return pl.pallas_call(
        _ret_kernel,
        grid=(batch, per_b),
        in_specs=[_proj_spec(tl, QKP, OFF_RQ, per_b), _proj_spec(tl, QKP, OFF_RK, per_b),
                  _proj_spec(tl, HP, OFF_RV, per_b), _proj_spec(tl, HP, OFF_RG, per_b),
                  pl.BlockSpec((tl, LANES), lambda b, l: (l, 0)),
                  pl.BlockSpec((tl, LANES), lambda b, l: (l, 0)),
                  pl.BlockSpec((tl, LANES), lambda b, l: (l, 0)),
                  _full((RET_CHUNK, QKP)), _full((RET_CHUNK, QKP)), _full((RET_CHUNK, QKP)),
                  _full((1, QKP)), _layer_spec((1, HP), layer)],
        out_specs=pl.BlockSpec((tl, HP), lambda b, l: (b * per_b + l, 0)),
        out_shape=jax.ShapeDtypeStruct((batch * seq, HP), BF16),
        scratch_shapes=[pltpu.VMEM((N_HEADS, HEAD_PAD, HEAD_PAD), F32)],
        compiler_params=_cparams(("arbitrary", "arbitrary")),
        name="ret_mixer",
    )(proj, proj, proj, proj, cos_t, sina_t, sinb_t, dq, dki, dke, dtot, nw_p)


S5_N = S5_GROUPS * S5_STATE
S5_SLAB = 256


def _gelu_tanh(x):
    return 0.5 * x * (1.0 + jnp.tanh(np.sqrt(2.0 / np.pi) * (x + 0.044715 * (x * x * x))))


def _s5_kernel(u_ref, bb_ref, cb_ref, nr_ref, ni_ref, pr_ref, pi_ref, lr_ref, li_ref,
               d_ref, wg_ref, bg_ref, o_ref, sr_ref, si_ref, x_scr, s_scr):
    @pl.when(pl.program_id(1) == 0)
    def _():
        sr_ref[...] = jnp.zeros_like(sr_ref)
        si_ref[...] = jnp.zeros_like(si_ref)

    tri = _tri_mask().astype(BF16)
    n_chunks = u_ref.shape[0] // CHUNK
    u = u_ref[...]
    x_scr[...] = jnp.dot(u, bb_ref[...], preferred_element_type=F32)

    def chunk(c, carry):
        r = pl.ds(pl.multiple_of(c * CHUNK, CHUNK), CHUNK)
        for j in range(S5_N // S5_SLAB):
            cs = slice(j * S5_SLAB, (j + 1) * S5_SLAB)
            ci = slice(S5_N + j * S5_SLAB, S5_N + (j + 1) * S5_SLAB)
            xr, xi = x_scr[r, cs], x_scr[r, ci]
            nr, ni = nr_ref[:, cs], ni_ref[:, cs]
            p_r = jnp.dot(tri, (xr * nr - xi * ni).astype(BF16), preferred_element_type=F32)
            p_i = jnp.dot(tri, (xr * ni + xi * nr).astype(BF16), preferred_element_type=F32)
            s0r, s0i = sr_ref[:, cs], si_ref[:, cs]
            lr, li = lr_ref[:, cs], li_ref[:, cs]
            q_r = p_r + (s0r * lr - s0i * li)
            q_i = p_i + (s0r * li + s0i * lr)
            pr, pi = pr_ref[:, cs], pi_ref[:, cs]
            s_r = q_r * pr - q_i * pi
            s_i = q_r * pi + q_i * pr
            sr_ref[:, cs] = s_r[CHUNK - 1:CHUNK, :]
            si_ref[:, cs] = s_i[CHUNK - 1:CHUNK, :]
            s_scr[r, cs] = s_r.astype(BF16)
            s_scr[r, ci] = s_i.astype(BF16)
        return carry

    lax.fori_loop(0, n_chunks, chunk, 0, unroll=2)
    y = jnp.dot(s_scr[...], cb_ref[...], preferred_element_type=F32)
    y = _gelu_tanh(y + d_ref[...] * u.astype(F32))
    gate = jnp.dot(y.astype(BF16), wg_ref[...], preferred_element_type=F32) + bg_ref[...]
    o_ref[...] = (y * jax.nn.sigmoid(gate)).astype(BF16)


def _s5_mixer(proj, batch, seq, layer, tabs, tl=512):
    per_b = seq // tl
    bb, cb, nr, ni, pr, pi, lr, li, dsk, wg, bg = tabs
    return pl.pallas_call(
        _s5_kernel,
        grid=(batch, per_b),
        in_specs=[_proj_spec(tl, S5_WIDTH, OFF_SU, per_b),
                  *[_layer_spec(a.shape[1:], layer) for a in tabs]],
        out_specs=pl.BlockSpec((tl, S5_WIDTH), lambda b, l: (b * per_b + l, 0)),
        out_shape=jax.ShapeDtypeStruct((batch * seq, S5_WIDTH), BF16),
        scratch_shapes=[pltpu.VMEM((1, S5_N), F32), pltpu.VMEM((1, S5_N), F32),
                        pltpu.VMEM((tl, 2 * S5_N), F32), pltpu.VMEM((tl, 2 * S5_N), BF16)],
        compiler_params=_cparams(("arbitrary", "arbitrary")),
        name="s5_mixer",
    )(proj, bb, cb, nr, ni, pr, pi, lr, li, dsk, wg, bg)


def _s5_tables(a_re, a_im, log_dt, b_re, b_im, c_re, c_im, d_skip, w_glu, b_glu):
    lam = lax.complex(a_re, a_im)
    dt = jnp.exp(log_dt)[:, None]
    lam_bar = jnp.exp(lam * dt)
    b_bar = ((lam_bar - 1.0) / lam)[..., None] * lax.complex(b_re, b_im)
    eye = jnp.eye(S5_GROUPS, dtype=F32)
    def blk_b(m):
        return jnp.einsum('gph,gk->ghkp', m, eye).reshape(S5_WIDTH, S5_N)
    bb = jnp.concatenate([blk_b(jnp.real(b_bar)), blk_b(jnp.imag(b_bar))], axis=1)
    def blk_c(m):
        return jnp.einsum('ghp,gk->kpgh', m, eye).reshape(S5_N, S5_WIDTH)
    cb = jnp.concatenate([blk_c(c_re), blk_c(-c_im)], axis=0)
    steps = jnp.arange(CHUNK, dtype=F32)[:, None, None]
    lam_dt = (lam * dt)[None]
    pos = jnp.exp(lam_dt * steps).reshape(CHUNK, S5_N)
    neg = jnp.exp(-lam_dt * steps).reshape(CHUNK, S5_N)
    one = lam_bar.reshape(1, S5_N)
    return (bb.astype(BF16), cb.astype(BF16), jnp.real(neg), jnp.imag(neg), jnp.real(pos),
            jnp.imag(pos), jnp.real(one), jnp.imag(one), d_skip.reshape(1, S5_WIDTH),
            w_glu.astype(BF16), b_glu.reshape(1, S5_WIDTH))


def _pack_bf16_pairs(x):
    w = x.shape[1] // 2
    xb = x.astype(BF16).astype(F32)
    hi = lax.bitcast_convert_type(xb[:, :w], jnp.uint32)
    lo = lax.bitcast_convert_type(xb[:, w:], jnp.uint32)
    return hi | (lo >> 16)


def _unpack_bf16_pairs(p):
    hi = lax.bitcast_convert_type(p & jnp.uint32(0xFFFF0000), F32)
    lo = lax.bitcast_convert_type(p << 16, F32)
    return hi, lo


def _router_kernel(og_ref, or_ref, os_ref, wg_ref, wr_ref, ws_ref, x_ref, g1_ref, sc_ref,
                   sh_ref, nw_ref, rw_ref, rb_ref,
                   x1_ref, h_ref, idx_ref, gate_ref, rank_ref, cnt_ref, carry_ref):
    i = pl.program_id(0)

    @pl.when(i == 0)
    def _():
        carry_ref[...] = jnp.zeros_like(carry_ref)

    mix = (jnp.dot(og_ref[...], wg_ref[...], preferred_element_type=F32)
           + jnp.dot(or_ref[...], wr_ref[...], preferred_element_type=F32)
           + jnp.dot(os_ref[...], ws_ref[...], preferred_element_type=F32))
    x1 = x_ref[...] + g1_ref[0] * mix
    x1_ref[...] = x1
    hdn = _rms_mod(x1, nw_ref[...], sc_ref[0], sh_ref[0])
    h_ref[...] = _pack_bf16_pairs(hdn)
    h_hi = hdn.astype(BF16)
    h_lo = (hdn - h_hi.astype(F32)).astype(BF16)
    p = lax.dot_general(rw_ref[...], h_hi, _NT, preferred_element_type=F32)
    q = lax.dot_general(rw_ref[:N_EXPERTS, :], h_lo, _NT, preferred_element_type=F32)
    logits = (p[:N_EXPERTS] + p[N_EXPERTS:] + q) + rb_ref[:, 0:1]
    tm = logits.shape[1]
    eidx = lax.broadcasted_iota(jnp.int32, (N_EXPERTS, tm), 0)
    work = logits
    onehot = jnp.zeros((N_EXPERTS, tm), F32)
    vals, idxs, sels = [], [], []
    for _ in range(TOP_K):
        m = jnp.max(work, axis=0, keepdims=True)
        ix = jnp.min(jnp.where(work == m, eidx, N_EXPERTS), axis=0, keepdims=True)
        sel = eidx == ix
        work = jnp.where(sel, -jnp.inf, work)
        onehot = onehot + sel.astype(F32)
        vals.append(m)
        idxs.append(ix)
        sels.append(sel)
    exps = [jnp.exp(v - vals[0]) for v in vals]
    denom = exps[0] + exps[1] + exps[2] + exps[3]
    r = lax.broadcasted_iota(jnp.int32, (tm, tm), 0)
    c = lax.broadcasted_iota(jnp.int32, (tm, tm), 1)
    earlier = (r < c).astype(BF16)
    before = (jnp.dot(onehot.astype(BF16), earlier, preferred_element_type=F32)
              + carry_ref[:, 0:1])
    row8 = lax.broadcasted_iota(jnp.int32, (8, tm), 0)
    idx_out = jnp.zeros((8, tm), jnp.int32)
    gate_out = jnp.zeros((8, tm), F32)
    rank_out = jnp.zeros((8, tm), F32)
    for k in range(TOP_K):
        rk = jnp.sum(jnp.where(sels[k], before, 0.0), axis=0, keepdims=True)
        idx_out = jnp.where(row8 == k, idxs[k], idx_out)
        gate_out = jnp.where(row8 == k, exps[k] / denom, gate_out)
        rank_out = jnp.where(row8 == k, rk, rank_out)
    idx_ref[...] = idx_out
    gate_ref[...] = gate_out
    rank_ref[...] = rank_out.astype(jnp.int32)
    total = carry_ref[...] + jnp.sum(onehot, axis=1, keepdims=True)
    carry_ref[...] = total
    cnt_ref[...] = total


def _outproj_router(o_gla, o_ret, o_s5, wg, wr, ws, x3, grp, mod5, b0, layer, nw2, rw_p, rb_p,
                    seq, tm=1024):
    _, t, d = x3.shape
    tm = min(tm, seq)
    per_b = seq // tm
    row = lambda w: pl.BlockSpec((tm, w), lambda i: (i, 0))
    full = lambda s: pl.BlockSpec(s, lambda i: (0,) * len(s))
    slot_t = pl.BlockSpec((8, tm), lambda i: (0, i))
    return pl.pallas_call(
        _router_kernel,
        grid=(t // tm,),
        in_specs=[row(HP), row(HP), row(S5_WIDTH), _layer_spec((HP, d), layer),
                  _layer_spec((HP, d), layer), _layer_spec((S5_WIDTH, d), layer),
                  pl.BlockSpec((None, tm, d), lambda i: (grp, i, 0)),
                  _mod_spec(layer, MOD_G1, per_b, b0), _mod_spec(layer, MOD_SC2, per_b, b0),
                  _mod_spec(layer, MOD_SH2, per_b, b0), _layer_spec((1, d), layer),
                  _layer_spec((2 * N_EXPERTS, d), layer), _layer_spec((N_EXPERTS, LANES), layer)],
        out_specs=[row(d), row(d // 2), slot_t, slot_t, slot_t, full((N_EXPERTS, LANES))],
        out_shape=[jax.ShapeDtypeStruct((t, d), F32),
                   jax.ShapeDtypeStruct((t, d // 2), jnp.uint32),
                   jax.ShapeDtypeStruct((8, t), jnp.int32),
                   jax.ShapeDtypeStruct((8, t), F32),
                   jax.ShapeDtypeStruct((8, t), jnp.int32),
                   jax.ShapeDtypeStruct((N_EXPERTS, LANES), F32)],
        scratch_shapes=[pltpu.VMEM((N_EXPERTS, LANES), F32)],
        compiler_params=_cparams(("arbitrary",)),
        name="outproj_router",
    )(o_gla, o_ret, o_s5, wg, wr, ws, x3, mod5, mod5, mod5, nw2, rw_p, rb_p)


GATHER_WIN = 64


def _gather_rows(table, idx):
    m = idx.shape[0]
    w = table.shape[1]
    mesh = plsc.VectorSubcoreMesh(core_axis_name="core", subcore_axis_name="subcore")

    @functools.partial(pl.kernel, out_type=jax.ShapeDtypeStruct((m, w), table.dtype),
                       mesh=mesh, name="sc_row_gather")
    def gather(x_hbm, i_hbm, o_hbm):
        def body(i_vmem, o_vmem):
            pltpu.sync_copy(x_hbm.at[i_vmem], o_vmem)

        pltpu.emit_pipeline(
            body,
            grid=(m // GATHER_WIN,),
            in_specs=[pl.BlockSpec((GATHER_WIN,), lambda i: (i,))],
            out_specs=[pl.BlockSpec((GATHER_WIN, w), lambda i: (i, 0))],
            core_axis_name=("core", "subcore"),
            dimension_semantics=(pltpu.PARALLEL,),
        )(i_hbm, o_hbm)

    return gather(table, idx)


def _scatter_rows(x, dest_slot_major, n_rows):
    t, w = x.shape
    steps = t // GATHER_WIN
    mesh = plsc.VectorSubcoreMesh(core_axis_name="core", subcore_axis_name="subcore")

    @functools.partial(pl.kernel, out_type=jax.ShapeDtypeStruct((n_rows, w), x.dtype),
                       mesh=mesh, name="sc_row_scatter")
    def scatter(x_hbm, i_hbm, o_hbm):
        def body(x_vmem, i0, i1, i2, i3):
            for i_vmem in (i0, i1, i2, i3):
                pltpu.sync_copy(x_vmem, o_hbm.at[i_vmem])

        slot = lambda k: pl.BlockSpec((GATHER_WIN,), lambda i: (k * steps + i,))
        pltpu.emit_pipeline(
            body,
            grid=(steps,),
            in_specs=[pl.BlockSpec((GATHER_WIN, w), lambda i: (i, 0)),
                      slot(0), slot(1), slot(2), slot(3)],
            out_specs=[],
            core_axis_name=("core", "subcore"),
            dimension_semantics=(pltpu.PARALLEL,),
        )(x_hbm, i_hbm, i_hbm, i_hbm, i_hbm)

    return scatter(x, dest_slot_major)


def _expert_kernel(be_ref, ns_ref, rows_ref, wu_ref, bu_ref, wd_ref, bd_ref, o_ref,
                   wu_bf, wd_bf):
    i = pl.program_id(0)
    e = be_ref[i]
    prev = be_ref[jnp.maximum(i - 1, 0)]

    @pl.when((i == 0) | (e != prev))
    def _():
        wu_bf[...] = wu_ref[0, 0].astype(BF16)
        wd_bf[...] = wd_ref[0, 0].astype(BF16)

    def run_rows(r0, n, first):
        rs = slice(r0, r0 + n)
        row = lax.broadcasted_iota(jnp.int32, (n, rows_ref.shape[1]), 0)
        x_hi, x_lo = _unpack_bf16_pairs(jnp.where(row >= first, rows_ref[rs, :], jnp.uint32(0)))
        x = jnp.concatenate([x_hi.astype(BF16), x_lo.astype(BF16)], axis=1)
        up = jnp.dot(x, wu_bf[...], preferred_element_type=F32) + bu_ref[0, 0]
        x_glu = jnp.minimum(up[:, :D_FF], SWIGLU_LIMIT)
        x_lin = jnp.clip(up[:, D_FF:], -SWIGLU_LIMIT, SWIGLU_LIMIT)
        act = x_glu * jax.nn.sigmoid(SWIGLU_ALPHA * x_glu) * (x_lin + 1.0)
        o_ref[rs, :] = _pack_bf16_pairs(
            jnp.dot(act.astype(BF16), wd_bf[...], preferred_element_type=F32) + bd_ref[0, 0])

    def zero_rows(r0, n):
        o_ref[r0:r0 + n, :] = jnp.zeros((n, o_ref.shape[1]), o_ref.dtype)

    half = ROW_SUB // 2
    for s in range(ROW_BLK // ROW_SUB):
        r0 = s * ROW_SUB
        first = ns_ref[i] - r0

        @pl.when(first < half)
        def _():
            run_rows(r0, ROW_SUB, first)

        @pl.when((first >= half) & (first < ROW_SUB))
        def _():
            zero_rows(r0, half)
            run_rows(r0 + half, half, first - half)

        @pl.when(first >= ROW_SUB)
        def _():
            zero_rows(r0, ROW_SUB)


def _experts(layer, block_e, n_skip, rows, w_up, b_up, w_down, b_down):
    n_rows, dh = rows.shape
    d = 2 * dh
    n_blocks = n_rows // ROW_BLK
    depth, ne, _, f2 = w_up.shape
    wsel = lambda i, be, nu: (layer, be[i], 0, 0)
    grid_spec = pltpu.PrefetchScalarGridSpec(
        num_scalar_prefetch=2,
        grid=(n_blocks,),
        in_specs=[pl.BlockSpec((ROW_BLK, dh), lambda i, be, nu: (i, 0)),
                  pl.BlockSpec((1, 1, d, f2), wsel),
                  pl.BlockSpec((1, 1, 1, f2), wsel),
                  pl.BlockSpec((1, 1, D_FF, d), wsel),
                  pl.BlockSpec((1, 1, 1, d), wsel)],
        out_specs=pl.BlockSpec((ROW_BLK, dh), lambda i, be, nu: (i, 0)),
        scratch_shapes=[pltpu.VMEM((d, f2), BF16), pltpu.VMEM((D_FF, d), BF16)],
    )
    return pl.pallas_call(
        _expert_kernel,
        grid_spec=grid_spec,
        out_shape=jax.ShapeDtypeStruct((n_rows, dh), jnp.uint32),
        compiler_params=_cparams(("arbitrary",)),
        name="moe_experts",
    )(block_e, n_skip, rows, w_up, b_up.reshape(depth, ne, 1, f2), w_down,
      b_down.reshape(depth, ne, 1, d))


def _combine_kernel(y0_ref, y1_ref, y2_ref, y3_ref, gate_ref, x1_ref, g2_ref, fw_ref, *rest,
                    final):
    o_ref = rest[-1]
    gates = gate_ref[...]
    y_hi, y_lo = None, None
    for k, y_ref in enumerate((y0_ref, y1_ref, y2_ref, y3_ref)):
        hi, lo = _unpack_bf16_pairs(y_ref[...])
        g = gates[:, k:k + 1]
        y_hi = g * hi if y_hi is None else y_hi + g * hi
        y_lo = g * lo if y_lo is None else y_lo + g * lo
    y = jnp.concatenate([y_hi, y_lo], axis=1)
    x2 = x1_ref[...] + g2_ref[0] * y
    if final:
        x2 = (x2 * lax.rsqrt(jnp.mean(x2 * x2, axis=-1, keepdims=True) + NORM_EPS)) * fw_ref[...]
    o_ref[...] = x2


def _combine(y4, gates, x1, mod5, b0, layer, fw, seq, final, grp, n_out, prev_out, th=1024):
    t, d = x1.shape
    th = min(th, seq)
    steps = t // th
    per_b = seq // th
    slot = lambda k: pl.BlockSpec((th, d // 2), lambda i: (k * steps + i, 0))
    return pl.pallas_call(
        functools.partial(_combine_kernel, final=final),
        grid=(steps,),
        in_specs=[slot(0), slot(1), slot(2), slot(3),
                  pl.BlockSpec((th, TOP_K), lambda i: (i, 0)),
                  pl.BlockSpec((th, d), lambda i: (i, 0)),
                  _mod_spec(layer, MOD_G2, per_b, b0),
                  pl.BlockSpec((1, d), lambda i: (0, 0))]
                 + ([] if prev_out is None else [pl.BlockSpec(memory_space=pl.ANY)]),
        out_specs=pl.BlockSpec((None, th, d), lambda i: (grp, i, 0)),
        out_shape=jax.ShapeDtypeStruct((n_out, t, d), F32),
        input_output_aliases={} if prev_out is None else {8: 0},
        compiler_params=_cparams(("arbitrary",)),
        name="moe_combine",
    )(y4, y4, y4, y4, gates, x1, mod5, fw, *([] if prev_out is None else [prev_out]))


def _retention_tables(seq):
    f32 = np.float32
    pos = np.arange(seq, dtype=f32)
    inv_freq = (f32(ROPE_BASE) ** (-np.arange(0, HEAD_DK, 2, dtype=f32) / f32(HEAD_DK))).astype(f32)
    ang = pos[:, None] * inv_freq[None, :]
    cos, sin = np.cos(ang), np.sin(ang)
    zero = np.zeros_like(sin)
    zpad = np.zeros((seq, LANES - 2 * HEAD_DK), f32)
    cos_t = np.concatenate([cos, cos, cos, cos, zpad], axis=1)
    sina_t = np.concatenate([-sin, zero, -sin, zero, zpad], axis=1)
    sinb_t = np.concatenate([zero, sin, zero, sin, zpad], axis=1)
    log_gamma = np.log1p(-np.exp2(f32(-5.0) - np.arange(N_HEADS, dtype=f32))).astype(f32)
    log_decay = np.broadcast_to(log_gamma[None, :, None], (RET_CHUNK, N_HEADS, HEAD_DK))
    cum = np.cumsum(log_decay, axis=0, dtype=f32)
    tot = cum[-1:]

    def shp(a):
        flat = a.reshape(a.shape[0], N_HEADS * HEAD_DK)
        return np.where(_DK_SRC >= 0, flat[:, np.maximum(_DK_SRC, 0)], f32(1.0)).astype(f32)

    tabs = (cos_t, sina_t, sinb_t, shp(np.exp(cum)), shp(np.exp(-cum)), shp(np.exp(tot - cum)),
            shp(np.exp(tot)))
    return tuple(jnp.asarray(a, F32) for a in tabs)


def kernel(x, c, norm1_w, norm2_w, w_mod, b_mod, w_in, gla_w_a2, gla_b_a, gla_norm_w, ret_norm_w, s5_a_re, s5_a_im, s5_log_dt, s5_b_re, s5_b_im, s5_c_re, s5_c_im, s5_d, s5_w_glu, s5_b_glu, w_out, router_w, router_b, w_up, b_up, w_down, b_down, final_norm_w):
    batch, seq, d = x.shape
    depth = w_mod.shape[0]
    gb = batch // N_STREAMS
    t = gb * seq
    n_slots = t * TOP_K
    n_blocks = n_slots // ROW_BLK + N_EXPERTS
    n_rows = n_blocks * ROW_BLK

    mod = _modulation(c, w_mod, b_mod)
    mod5 = mod.reshape(depth, batch, 6, 1, d).transpose(0, 2, 1, 3, 4)
    ret_tabs = _retention_tables(seq)

    w_p = _take_cols(w_in, _IN_SRC).astype(BF16)
    wa_p = jnp.zeros((depth, LANES, QKP), F32).at[:, :GATE_RANK].set(
        _take_cols(gla_w_a2, _DK_SRC)).astype(BF16)
    ba_p = _take_cols(gla_b_a, _DK_SRC).reshape(depth, 1, QKP)
    gnw = _take_cols(gla_norm_w, _DV_SRC).reshape(depth, 1, HP)
    rnw = _take_cols(ret_norm_w, _DV_SRC).reshape(depth, 1, HP)
    kv = N_HEADS * HEAD_DV
    wo_g = _take_rows(w_out[:, :kv], _DV_SRC).astype(BF16)
    wo_r = _take_rows(w_out[:, kv:2 * kv], _DV_SRC).astype(BF16)
    wo_s = w_out[:, 2 * kv:].astype(BF16)
    rw_t = jnp.swapaxes(router_w, 1, 2)
    rw_hi = rw_t.astype(BF16)
    rw_p = jnp.concatenate([rw_hi, (rw_t - rw_hi.astype(F32)).astype(BF16)], axis=1)
    rb_p = jnp.broadcast_to(router_b[:, :, None], (depth, N_EXPERTS, LANES))
    s5_tabs = jax.vmap(_s5_tables)(s5_a_re, s5_a_im, s5_log_dt, s5_b_re, s5_b_im, s5_c_re,
                                   s5_c_im, s5_d, s5_w_glu, s5_b_glu)
    n1 = norm1_w.reshape(depth, 1, d)
    n2 = norm2_w.reshape(depth, 1, d)
    fw = final_norm_w.reshape(1, d)

    xs = [(x.reshape(N_STREAMS, t, d), g) for g in range(N_STREAMS)]
    out = None

    for i in range(depth):
        final = i == depth - 1
        for g in range(N_STREAMS):
            x3, grp = xs[g]
            b0 = g * gb

            proj = _in_projection(x3, grp, mod5, b0, i, n1, w_p, seq)
            o_gla = _gla_mixer(proj, gb, seq, i, wa_p, ba_p, gnw)
            o_ret = _ret_mixer(proj, gb, seq, i, *ret_tabs, rnw)
            o_s5 = _s5_mixer(proj, gb, seq, i, s5_tabs)

            x1, hdn, idx, gates, rank, counts = _outproj_router(
                o_gla, o_ret, o_s5, wo_g, wo_r, wo_s, x3, grp, mod5, b0, i, n2, rw_p, rb_p, seq)

            cnt = counts[:, 0].astype(jnp.int32)
            padded = (cnt + ROW_BLK - 1) // ROW_BLK * ROW_BLK
            pad_ends = jnp.cumsum(padded)
            first_row = pad_ends - cnt
            slot_start = jnp.sum(jnp.where(idx[:TOP_K, :, None] == jnp.arange(N_EXPERTS),
                                           first_row, 0), axis=-1)
            dest_sm = (slot_start + rank[:TOP_K]).astype(jnp.int32).reshape(-1)
            gates_tm = gates[:TOP_K].T
            blk_start = jnp.arange(n_blocks, dtype=jnp.int32) * ROW_BLK
            block_e = jnp.minimum(jnp.sum(pad_ends[None, :] <= blk_start[:, None], axis=1),
                                  N_EXPERTS - 1).astype(jnp.int32)
            n_skip = jnp.where(blk_start < pad_ends[-1],
                               jnp.clip(first_row[block_e] - blk_start, 0, ROW_BLK),
                               ROW_BLK).astype(jnp.int32)

            rows = _scatter_rows(hdn, dest_sm, n_rows)
            out_rows = _experts(i, block_e, n_skip, rows, w_up, b_up, w_down, b_down)
            y4 = _gather_rows(out_rows, dest_sm)
            if final:
                out = _combine(y4, gates_tm, x1, mod5, b0, i, fw, seq, True, g, N_STREAMS, out)
            else:
                xs[g] = (_combine(y4, gates_tm, x1, mod5, b0, i, fw, seq, False, 0, 1, None), 0)

    return out.reshape(batch, seq, d)
```

```python
import functools

import numpy as np
import jax
import jax.numpy as jnp
from jax import lax
from jax.experimental import pallas as pl
from jax.experimental.pallas import tpu as pltpu
from jax.experimental.pallas import tpu_sc as plsc

D_MODEL = 1024
CHUNK = 64
RET_CHUNK = 128
NORM_EPS = 1e-5
N_HEADS = 4
HEAD_DK = 48
HEAD_DV = 96
GATE_RANK = 16
GATE_TEMP = 16.0
ROPE_BASE = 10000.0
S5_WIDTH = 256
S5_GROUP_DIM = 16
S5_GROUPS = 16
S5_STATE = 64
N_EXPERTS = 32
TOP_K = 4
D_FF = 1024
SWIGLU_LIMIT = 7.0
SWIGLU_ALPHA = 1.702

LANES = 128
HEAD_PAD = LANES
HP = N_HEADS * HEAD_PAD
N_PAIRS = N_HEADS // 2
QKP = N_PAIRS * LANES
VMEM_LIMIT = 56 * 1024 * 1024

OFF_GQ, OFF_GK, OFF_GV, OFF_GG = 0, QKP, 2 * QKP, 2 * QKP + HP
OFF_RQ = OFF_GG + HP
OFF_RK, OFF_RV, OFF_RG = OFF_RQ + QKP, OFF_RQ + 2 * QKP, OFF_RQ + 2 * QKP + HP
OFF_SU = OFF_RG + HP
OFF_GA = OFF_SU + S5_WIDTH
NP_COLS = OFF_GA + LANES
PROJ_CHUNKS = (1280, 1280, 896)

ROW_BLK = 1024
ROW_SUB = 512
MOD_SH1, MOD_SC1, MOD_G1, MOD_SH2, MOD_SC2, MOD_G2 = range(6)
N_STREAMS = 1

F32 = jnp.float32
BF16 = jnp.bfloat16


def _DK_SRC_LANE(h, d):
    return (h // 2) * LANES + (h % 2) * HEAD_DK + d


def _in_col_map():
    src = -np.ones((NP_COLS,), np.int64)
    kq = N_HEADS * HEAD_DK
    kv = N_HEADS * HEAD_DV
    base = dict(gq=0, gk=kq, gv=2 * kq, gg=2 * kq + kv, ga=2 * kq + 2 * kv)
    r0 = base['ga'] + GATE_RANK
    base.update(rq=r0, rk=r0 + kq, rv=r0 + 2 * kq, rg=r0 + 2 * kq + kv, su=r0 + 2 * kq + 2 * kv)
    for h in range(N_HEADS):
        for d in range(HEAD_DK):
            lane = _DK_SRC_LANE(h, d)
            src[OFF_GQ + lane] = base['gq'] + h * HEAD_DK + d
            src[OFF_GK + lane] = base['gk'] + h * HEAD_DK + d
            src[OFF_RQ + lane] = base['rq'] + h * HEAD_DK + d
            src[OFF_RK + lane] = base['rk'] + h * HEAD_DK + d
        for d in range(HEAD_DV):
            src[OFF_GV + h * HEAD_PAD + d] = base['gv'] + h * HEAD_DV + d
            src[OFF_GG + h * HEAD_PAD + d] = base['gg'] + h * HEAD_DV + d
            src[OFF_RV + h * HEAD_PAD + d] = base['rv'] + h * HEAD_DV + d
            src[OFF_RG + h * HEAD_PAD + d] = base['rg'] + h * HEAD_DV + d
    src[OFF_SU:OFF_SU + S5_WIDTH] = base['su'] + np.arange(S5_WIDTH)
    src[OFF_GA:OFF_GA + GATE_RANK] = base['ga'] + np.arange(GATE_RANK)
    return src


_IN_SRC = _in_col_map()


def _head_pad_map(width):
    src = -np.ones((HP,), np.int64)
    for h in range(N_HEADS):
        src[h * HEAD_PAD:h * HEAD_PAD + width] = h * width + np.arange(width)
    return src


_DV_SRC = _head_pad_map(HEAD_DV)
_DK_SRC = -np.ones((QKP,), np.int64)
for _h in range(N_HEADS):
    for _d in range(HEAD_DK):
        _DK_SRC[_DK_SRC_LANE(_h, _d)] = _h * HEAD_DK + _d


def _take_static(w, src, axis):
    axis = axis % w.ndim
    pieces, start = [], 0
    for j in range(1, len(src) + 1):
        run_ends = (j == len(src) or (src[j] < 0) != (src[start] < 0)
                    or (src[start] >= 0 and src[j] != src[j - 1] + 1))
        if run_ends:
            if src[start] < 0:
                shape = w.shape[:axis] + (j - start,) + w.shape[axis + 1:]
                pieces.append(jnp.zeros(shape, w.dtype))
            else:
                pieces.append(lax.slice_in_dim(w, int(src[start]), int(src[j - 1]) + 1, axis=axis))
            start = j
    return jnp.concatenate(pieces, axis=axis)


def _take_cols(w, src):
    return _take_static(w, src, -1)


def _take_rows(w, src):
    return _take_static(w, src, -2)


def _layer_spec(shape, layer):
    return pl.BlockSpec((None,) + tuple(shape), lambda *_: (layer,) + (0,) * len(shape))


def _mod_spec(layer, which, per_b, b0):
    return pl.BlockSpec((None, None, 1, 1, D_MODEL),
                        lambda i: (layer, which, b0 + i // per_b, 0, 0))


def _cparams(sem):
    return pltpu.CompilerParams(dimension_semantics=sem, vmem_limit_bytes=VMEM_LIMIT)


def _mod_kernel(c_ref, w_ref, b_ref, o_ref):
    c = c_ref[...]
    cond = c * jax.nn.sigmoid(c)
    o_ref[0] = jnp.dot(cond, w_ref[0], preferred_element_type=F32,
                       precision=lax.Precision.HIGHEST) + b_ref[0]


def _modulation(c, w_mod, b_mod):
    depth, d, n = w_mod.shape
    b = c.shape[0]
    nb = 1536
    return pl.pallas_call(
        _mod_kernel,
        grid=(depth, n // nb),
        in_specs=[pl.BlockSpec((b, d), lambda l, j: (0, 0)),
                  pl.BlockSpec((1, d, nb), lambda l, j: (l, 0, j)),
                  pl.BlockSpec((1, 1, nb), lambda l, j: (l, 0, j))],
        out_specs=pl.BlockSpec((1, b, nb), lambda l, j: (l, 0, j)),
        out_shape=jax.ShapeDtypeStruct((depth, b, n), F32),
        compiler_params=_cparams(("arbitrary", "arbitrary")),
        name="adaln_mod",
    )(c, w_mod, b_mod.reshape(depth, 1, n))


def _rms_mod(x, nw, sc, sh):
    y = x * lax.rsqrt(jnp.mean(x * x, axis=-1, keepdims=True) + NORM_EPS)
    return (y * nw) * (1.0 + sc) + sh


def _inproj_kernel(x_ref, sc_ref, sh_ref, nw_ref, w_ref, o_ref):
    h = _rms_mod(x_ref[...], nw_ref[...], sc_ref[0], sh_ref[0]).astype(BF16)
    assert sum(PROJ_CHUNKS) == NP_COLS
    for j, width in enumerate(PROJ_CHUNKS):
        cs = slice(sum(PROJ_CHUNKS[:j]), sum(PROJ_CHUNKS[:j]) + width)
        o_ref[:, cs] = jnp.dot(h, w_ref[:, cs], preferred_element_type=F32).astype(BF16)


def _in_projection(x3, grp, mod5, b0, layer, nw, w_p, seq, tm=1024):
    _, t, d = x3.shape
    tm = min(tm, seq)
    per_b = seq // tm
    return pl.pallas_call(
        _inproj_kernel,
        grid=(t // tm,),
        in_specs=[pl.BlockSpec((None, tm, d), lambda i: (grp, i, 0)),
                  _mod_spec(layer, MOD_SC1, per_b, b0), _mod_spec(layer, MOD_SH1, per_b, b0),
                  _layer_spec((1, d), layer), _layer_spec((d, NP_COLS), layer)],
        out_specs=pl.BlockSpec((tm, NP_COLS), lambda i: (i, 0)),
        out_shape=jax.ShapeDtypeStruct((t, NP_COLS), BF16),
        compiler_params=_cparams(("arbitrary",)),
        name="in_proj",
    )(x3, mod5, mod5, nw, w_p)


_NT = (((1,), (1,)), ((), ()))
_TN = (((0,), (0,)), ((), ()))


def _tri_mask(n=CHUNK):
    r = lax.broadcasted_iota(jnp.int32, (n, n), 0)
    c = lax.broadcasted_iota(jnp.int32, (n, n), 1)
    return r >= c


def _pair_masks(rows):
    lane = lax.broadcasted_iota(jnp.int32, (rows, LANES), 1)
    return lane < HEAD_DK, (lane >= HEAD_DK) & (lane < 2 * HEAD_DK)


def _head_attention(qd, ki, ke, vh, et, st_ref, h, causal):
    qb = qd.astype(BF16)
    sc = lax.dot_general(qb, ki.astype(BF16), _NT, preferred_element_type=F32)
    sc = jnp.where(causal, sc, 0.0)
    st = st_ref[h]
    o = jnp.dot(sc.astype(BF16), vh, preferred_element_type=F32)
    o = o + lax.dot_general(qb, st.astype(BF16), _NT, preferred_element_type=F32)
    st_ref[h] = st * et + lax.dot_general(vh, ke.astype(BF16), _TN, preferred_element_type=F32)
    return o


def _gla_kernel(q_ref, k_ref, v_ref, g_ref, a_ref, wa_ref, ba_ref, nw_ref, tri_ref, o_ref,
                st_ref, qd_s, ki_s, ke_s, et_s):
    @pl.when(pl.program_id(1) == 0)
    def _():
        st_ref[...] = jnp.zeros_like(st_ref)

    causal = _tri_mask()
    tl = q_ref.shape[0]
    n_chunks = tl // CHUNK

    z = jnp.dot(a_ref[...], wa_ref[...], preferred_element_type=F32) + ba_ref[...]
    la = (jnp.minimum(z, 0.0) - jnp.log1p(jnp.exp(-jnp.abs(z)))) * (1.0 / GATE_TEMP)
    hi = la.astype(BF16)
    lo = (la - hi.astype(F32)).astype(BF16)
    cum = (jnp.dot(tri_ref[...], hi, preferred_element_type=F32)
           + jnp.dot(tri_ref[...], lo, preferred_element_type=F32))
    cum3 = cum.reshape(n_chunks, CHUNK, QKP)
    tot3 = cum3[:, CHUNK - 1:CHUNK, :]
    qd = (q_ref[...].astype(F32) * (HEAD_DK ** -0.5)) * jnp.exp(cum)
    masks = _pair_masks(tl)
    for h in range(N_HEADS):
        pair = slice((h // 2) * LANES, (h // 2 + 1) * LANES)
        qd_s[:, h * HEAD_PAD:(h + 1) * HEAD_PAD] = jnp.where(masks[h % 2], qd[:, pair],
                                                              0.0).astype(BF16)
    kf = k_ref[...].astype(F32)
    ki_s[...] = (kf * jnp.exp(-cum)).astype(BF16)
    ke_s[...] = (kf * jnp.exp(tot3 - cum3).reshape(tl, QKP)).astype(BF16)
    et_s[...] = jnp.exp(tot3).reshape(n_chunks, QKP)

    def chunk(c, carry):
        r = pl.ds(pl.multiple_of(c * CHUNK, CHUNK), CHUNK)
        et = et_s[pl.ds(c, 1), :]
        for h in range(N_HEADS):
            sl = slice(h * HEAD_PAD, (h + 1) * HEAD_PAD)
            pair = slice((h // 2) * LANES, (h // 2 + 1) * LANES)
            o = _head_attention(qd_s[r, sl], ki_s[r, pair], ke_s[r, pair], v_ref[r, sl],
                                et[:, pair], st_ref, h, causal)
            ms = jnp.sum(o * o, axis=-1, keepdims=True) * (1.0 / HEAD_DV)
            y = (o * lax.rsqrt(ms + NORM_EPS)) * nw_ref[:, sl]
            g = g_ref[r, sl].astype(F32)
            o_ref[r, sl] = (y * (g * jax.nn.sigmoid(g))).astype(BF16)
        return carry

    lax.fori_loop(0, n_chunks, chunk, 0, unroll=4)


def _ret_kernel(q_ref, k_ref, v_ref, g_ref, cos_ref, sina_ref, sinb_ref, dq_ref, dki_ref,
                dke_ref, dt_ref, nw_ref, o_ref, st_ref):
    @pl.when(pl.program_id(1) == 0)
    def _():
        st_ref[...] = jnp.zeros_like(st_ref)

    causal = _tri_mask(RET_CHUNK)
    n_chunks = q_ref.shape[0] // RET_CHUNK
    lane = lax.broadcasted_iota(jnp.int32, (RET_CHUNK, HEAD_PAD), 1)
    real = lane < HEAD_DV
    masks = _pair_masks(RET_CHUNK)
    half = HEAD_DK // 2

    def rotary(t, cos, sina, sinb):
        return (t * cos + pltpu.roll(t, LANES - half, 1) * sina + pltpu.roll(t, half, 1) * sinb)

    def chunk(c, carry):
        r = pl.ds(pl.multiple_of(c * RET_CHUNK, RET_CHUNK), RET_CHUNK)
        cos, sina, sinb = cos_ref[r, :], sina_ref[r, :], sinb_ref[r, :]
        pair_q, pair_ki, pair_ke = [], [], []
        for p in range(N_PAIRS):
            ps = slice(p * LANES, (p + 1) * LANES)
            qr = rotary(q_ref[r, ps].astype(F32), cos, sina, sinb) * dq_ref[:, ps]
            kr = rotary(k_ref[r, ps].astype(F32), cos, sina, sinb) * (HEAD_DK ** -0.5)
            pair_q.append(qr)
            pair_ki.append((kr * dki_ref[:, ps]).astype(BF16))
            pair_ke.append((kr * dke_ref[:, ps]).astype(BF16))
        for h in range(N_HEADS):
            sl = slice(h * HEAD_PAD, (h + 1) * HEAD_PAD)
            p = h // 2
            qd = jnp.where(masks[h % 2], pair_q[p], 0.0)
            o = _head_attention(qd, pair_ki[p], pair_ke[p], v_ref[r, sl],
                                dt_ref[:, p * LANES:(p + 1) * LANES], st_ref, h, causal)
            mu = jnp.sum(o, axis=-1, keepdims=True) * (1.0 / HEAD_DV)
            oc = jnp.where(real, o - mu, 0.0)
            var = jnp.sum(oc * oc, axis=-1, keepdims=True) * (1.0 / HEAD_DV)
            y = (oc * lax.rsqrt(var + NORM_EPS)) * nw_ref[:, sl]
            g = g_ref[r, sl].astype(F32)
            o_ref[r, sl] = (y * (g * jax.nn.sigmoid(g))).astype(BF16)
        return carry

    lax.fori_loop(0, n_chunks, chunk, 0, unroll=2)


def _proj_spec(tl, width, col_off, per_b):
    cb = col_off // width
    return pl.BlockSpec((tl, width), lambda b, l: (b * per_b + l, cb))


def _full(shape):
    return pl.BlockSpec(shape, lambda b, l: (0,) * len(shape))


def _gla_mixer(proj, batch, seq, layer, wa_p, ba_p, nw_p, tl=512):
    per_b = seq // tl
    pos = np.arange(tl)
    tri_bd = jnp.asarray((pos[:, None] // CHUNK == pos[None, :] // CHUNK)
                         & (pos[:, None] >= pos[None, :]), BF16)
    return pl.pallas_call(
        _gla_kernel,
        grid=(batch, per_b),
        in_specs=[_proj_spec(tl, QKP, OFF_GQ, per_b), _proj_spec(tl, QKP, OFF_GK, per_b),
                  _proj_spec(tl, HP, OFF_GV, per_b), _proj_spec(tl, HP, OFF_GG, per_b),
                  _proj_spec(tl, LANES, OFF_GA, per_b),
                  _layer_spec((LANES, QKP), layer), _layer_spec((1, QKP), layer),
                  _layer_spec((1, HP), layer), _full((tl, tl))],
        out_specs=pl.BlockSpec((tl, HP), lambda b, l: (b * per_b + l, 0)),
        out_shape=jax.ShapeDtypeStruct((batch * seq, HP), BF16),
        scratch_shapes=[pltpu.VMEM((N_HEADS, HEAD_PAD, HEAD_PAD), F32),
                        pltpu.VMEM((tl, HP), BF16), pltpu.VMEM((tl, QKP), BF16),
                        pltpu.VMEM((tl, QKP), BF16), pltpu.VMEM((tl // CHUNK, QKP), F32)],
        compiler_params=_cparams(("arbitrary", "arbitrary")),
        name="gla_mixer",
    )(proj, proj, proj, proj, proj, wa_p, ba_p, nw_p, tri_bd)


def _ret_mixer(proj, batch, seq, layer, cos_t, sina_t, sinb_t, dq, dki, dke, dtot, nw_p, tl=512):
    per_b = seq // tl
    return pl.pallas_call(
        _ret_kernel,
        grid=(batch, per_b),
        in_specs=[_proj_spec(tl, QKP, OFF_RQ, per_b), _proj_spec(tl, QKP, OFF_RK, per_b),
                  _proj_spec(tl, HP, OFF_RV, per_b), _proj_spec(tl, HP, OFF_RG, per_b),
                  pl.BlockSpec((tl, LANES), lambda b, l: (l, 0)),
                  pl.BlockSpec((tl, LANES), lambda b, l: (l, 0)),
                  pl.BlockSpec((tl, LANES), lambda b, l: (l, 0)),
                  _full((RET_CHUNK, QKP)), _full((RET_CHUNK, QKP)), _full((RET_CHUNK, QKP)),
                  _full((1, QKP)), _layer_spec((1, HP), layer)],
        out_specs=pl.BlockSpec((tl, HP), lambda b, l: (b * per_b + l, 0)),
        out_shape=jax.ShapeDtypeStruct((batch * seq, HP), BF16),
        scratch_shapes=[pltpu.VMEM((N_HEADS, HEAD_PAD, HEAD_PAD), F32)],
        compiler_params=_cparams(("arbitrary", "arbitrary")),
        name="ret_mixer",
    )(proj, proj, proj, proj, cos_t, sina_t, sinb_t, dq, dki, dke, dtot, nw_p)


S5_N = S5_GROUPS * S5_STATE
S5_SLAB = 256


def _gelu_tanh(x):
    return 0.5 * x * (1.0 + jnp.tanh(np.sqrt(2.0 / np.pi) * (x + 0.044715 * (x * x * x))))


def _s5_kernel(u_ref, bb_ref, cb_ref, nr_ref, ni_ref, pr_ref, pi_ref, lr_ref, li_ref,
               d_ref, wg_ref, bg_ref, o_ref, sr_ref, si_ref, x_scr, s_scr):
    @pl.when(pl.program_id(1) == 0)
    def _():
        sr_ref[...] = jnp.zeros_like(sr_ref)
        si_ref[...] = jnp.zeros_like(si_ref)

    tri = _tri_mask().astype(BF16)
    n_chunks = u_ref.shape[0] // CHUNK
    u = u_ref[...]
    x_scr[...] = jnp.dot(u, bb_ref[...], preferred_element_type=F32)

    def chunk(c, carry):
        r = pl.ds(pl.multiple_of(c * CHUNK, CHUNK), CHUNK)
        for j in range(S5_N // S5_SLAB):
            cs = slice(j * S5_SLAB, (j + 1) * S5_SLAB)
            ci = slice(S5_N + j * S5_SLAB, S5_N + (j + 1) * S5_SLAB)
            xr, xi = x_scr[r, cs], x_scr[r, ci]
            nr, ni = nr_ref[:, cs], ni_ref[:, cs]
            p_r = jnp.dot(tri, (xr * nr - xi * ni).astype(BF16), preferred_element_type=F32)
            p_i = jnp.dot(tri, (xr * ni + xi * nr).astype(BF16), preferred_element_type=F32)
            s0r, s0i = sr_ref[:, cs], si_ref[:, cs]
            lr, li = lr_ref[:, cs], li_ref[:, cs]
            q_r = p_r + (s0r * lr - s0i * li)
            q_i = p_i + (s0r * li + s0i * lr)
            pr, pi = pr_ref[:, cs], pi_ref[:, cs]
            s_r = q_r * pr - q_i * pi
            s_i = q_r * pi + q_i * pr
            sr_ref[:, cs] = s_r[CHUNK - 1:CHUNK, :]
            si_ref[:, cs] = s_i[CHUNK - 1:CHUNK, :]
            s_scr[r, cs] = s_r.astype(BF16)
            s_scr[r, ci] = s_i.astype(BF16)
        return carry

    lax.fori_loop(0, n_chunks, chunk, 0, unroll=2)
    y = jnp.dot(s_scr[...], cb_ref[...], preferred_element_type=F32)
    y = _gelu_tanh(y + d_ref[...] * u.astype(F32))
    gate = jnp.dot(y.astype(BF16), wg_ref[...], preferred_element_type=F32) + bg_ref[...]
    o_ref[...] = (y * jax.nn.sigmoid(gate)).astype(BF16)


def _s5_mixer(proj, batch, seq, layer, tabs, tl=512):
    per_b = seq // tl
    bb, cb, nr, ni, pr, pi, lr, li, dsk, wg, bg = tabs
    return pl.pallas_call(
        _s5_kernel,
        grid=(batch, per_b),
        in_specs=[_proj_spec(tl, S5_WIDTH, OFF_SU, per_b),
                  *[_layer_spec(a.shape[1:], layer) for a in tabs]],
        out_specs=pl.BlockSpec((tl, S5_WIDTH), lambda b, l: (b * per_b + l, 0)),
        out_shape=jax.ShapeDtypeStruct((batch * seq, S5_WIDTH), BF16),
        scratch_shapes=[pltpu.VMEM((1, S5_N), F32), pltpu.VMEM((1, S5_N), F32),
                        pltpu.VMEM((tl, 2 * S5_N), F32), pltpu.VMEM((tl, 2 * S5_N), BF16)],
        compiler_params=_cparams(("arbitrary", "arbitrary")),
        name="s5_mixer",
    )(proj, bb, cb, nr, ni, pr, pi, lr, li, dsk, wg, bg)


def _s5_tables(a_re, a_im, log_dt, b_re, b_im, c_re, c_im, d_skip, w_glu, b_glu):
    lam = lax.complex(a_re, a_im)
    dt = jnp.exp(log_dt)[:, None]
    lam_bar = jnp.exp(lam * dt)
    b_bar = ((lam_bar - 1.0) / lam)[..., None] * lax.complex(b_re, b_im)
    eye = jnp.eye(S5_GROUPS, dtype=F32)
    def blk_b(m):
        return jnp.einsum('gph,gk->ghkp', m, eye).reshape(S5_WIDTH, S5_N)
    bb = jnp.concatenate([blk_b(jnp.real(b_bar)), blk_b(jnp.imag(b_bar))], axis=1)
    def blk_c(m):
        return jnp.einsum('ghp,gk->kpgh', m, eye).reshape(S5_N, S5_WIDTH)
    cb = jnp.concatenate([blk_c(c_re), blk_c(-c_im)], axis=0)
    steps = jnp.arange(CHUNK, dtype=F32)[:, None, None]
    lam_dt = (lam * dt)[None]
    pos = jnp.exp(lam_dt * steps).reshape(CHUNK, S5_N)
    neg = jnp.exp(-lam_dt * steps).reshape(CHUNK, S5_N)
    one = lam_bar.reshape(1, S5_N)
    return (bb.astype(BF16), cb.astype(BF16), jnp.real(neg), jnp.imag(neg), jnp.real(pos),
            jnp.imag(pos), jnp.real(one), jnp.imag(one), d_skip.reshape(1, S5_WIDTH),
            w_glu.astype(BF16), b_glu.reshape(1, S5_WIDTH))


def _pack_bf16_pairs(x):
    w = x.shape[1] // 2
    xb = x.astype(BF16).astype(F32)
    hi = lax.bitcast_convert_type(xb[:, :w], jnp.uint32)
    lo = lax.bitcast_convert_type(xb[:, w:], jnp.uint32)
    return hi | (lo >> 16)


def _unpack_bf16_pairs(p):
    hi = lax.bitcast_convert_type(p & jnp.uint32(0xFFFF0000), F32)
    lo = lax.bitcast_convert_type(p << 16, F32)
    return hi, lo


def _router_kernel(og_ref, or_ref, os_ref, wg_ref, wr_ref, ws_ref, x_ref, g1_ref, sc_ref,
                   sh_ref, nw_ref, rw_ref, rb_ref,
                   x1_ref, h_ref, idx_ref, gate_ref, rank_ref, cnt_ref, carry_ref):
    i = pl.program_id(0)

    @pl.when(i == 0)
    def _():
        carry_ref[...] = jnp.zeros_like(carry_ref)

    mix = (jnp.dot(og_ref[...], wg_ref[...], preferred_element_type=F32)
           + jnp.dot(or_ref[...], wr_ref[...], preferred_element_type=F32)
           + jnp.dot(os_ref[...], ws_ref[...], preferred_element_type=F32))
    x1 = x_ref[...] + g1_ref[0] * mix
    x1_ref[...] = x1
    hdn = _rms_mod(x1, nw_ref[...], sc_ref[0], sh_ref[0])
    h_ref[...] = _pack_bf16_pairs(hdn)
    h_hi = hdn.astype(BF16)
    h_lo = (hdn - h_hi.astype(F32)).astype(BF16)
    p = lax.dot_general(rw_ref[...], h_hi, _NT, preferred_element_type=F32)
    q = lax.dot_general(rw_ref[:N_EXPERTS, :], h_lo, _NT, preferred_element_type=F32)
    logits = (p[:N_EXPERTS] + p[N_EXPERTS:] + q) + rb_ref[:, 0:1]
    tm = logits.shape[1]
    eidx = lax.broadcasted_iota(jnp.int32, (N_EXPERTS, tm), 0)
    work = logits
    onehot = jnp.zeros((N_EXPERTS, tm), F32)
    vals, idxs, sels = [], [], []
    for _ in range(TOP_K):
        m = jnp.max(work, axis=0, keepdims=True)
        ix = jnp.min(jnp.where(work == m, eidx, N_EXPERTS), axis=0, keepdims=True)
        sel = eidx == ix
        work = jnp.where(sel, -jnp.inf, work)
        onehot = onehot + sel.astype(F32)
        vals.append(m)
        idxs.append(ix)
        sels.append(sel)
    exps = [jnp.exp(v - vals[0]) for v in vals]
    denom = exps[0] + exps[1] + exps[2] + exps[3]
    r = lax.broadcasted_iota(jnp.int32, (tm, tm), 0)
    c = lax.broadcasted_iota(jnp.int32, (tm, tm), 1)
    earlier = (r < c).astype(BF16)
    before = (jnp.dot(onehot.astype(BF16), earlier, preferred_element_type=F32)
              + carry_ref[:, 0:1])
    row8 = lax.broadcasted_iota(jnp.int32, (8, tm), 0)
    idx_out = jnp.zeros((8, tm), jnp.int32)
    gate_out = jnp.zeros((8, tm), F32)
    rank_out = jnp.zeros((8, tm), F32)
    for k in range(TOP_K):
        rk = jnp.sum(jnp.where(sels[k], before, 0.0), axis=0, keepdims=True)
        idx_out = jnp.where(row8 == k, idxs[k], idx_out)
        gate_out = jnp.where(row8 == k, exps[k] / denom, gate_out)
        rank_out = jnp.where(row8 == k, rk, rank_out)
    idx_ref[...] = idx_out
    gate_ref[...] = gate_out
    rank_ref[...] = rank_out.astype(jnp.int32)
    total = carry_ref[...] + jnp.sum(onehot, axis=1, keepdims=True)
    carry_ref[...] = total
    cnt_ref[...] = total


def _outproj_router(o_gla, o_ret, o_s5, wg, wr, ws, x3, grp, mod5, b0, layer, nw2, rw_p, rb_p,
                    seq, tm=1024):
    _, t, d = x3.shape
    tm = min(tm, seq)
    per_b = seq // tm
    row = lambda w: pl.BlockSpec((tm, w), lambda i: (i, 0))
    full = lambda s: pl.BlockSpec(s, lambda i: (0,) * len(s))
    slot_t = pl.BlockSpec((8, tm), lambda i: (0, i))
    return pl.pallas_call(
        _router_kernel,
        grid=(t // tm,),
        in_specs=[row(HP), row(HP), row(S5_WIDTH), _layer_spec((HP, d), layer),
                  _layer_spec((HP, d), layer), _layer_spec((S5_WIDTH, d), layer),
                  pl.BlockSpec((None, tm, d), lambda i: (grp, i, 0)),
                  _mod_spec(layer, MOD_G1, per_b, b0), _mod_spec(layer, MOD_SC2, per_b, b0),
                  _mod_spec(layer, MOD_SH2, per_b, b0), _layer_spec((1, d), layer),
                  _layer_spec((2 * N_EXPERTS, d), layer), _layer_spec((N_EXPERTS, LANES), layer)],
        out_specs=[row(d), row(d // 2), slot_t, slot_t, slot_t, full((N_EXPERTS, LANES))],
        out_shape=[jax.ShapeDtypeStruct((t, d), F32),
                   jax.ShapeDtypeStruct((t, d // 2), jnp.uint32),
                   jax.ShapeDtypeStruct((8, t), jnp.int32),
                   jax.ShapeDtypeStruct((8, t), F32),
                   jax.ShapeDtypeStruct((8, t), jnp.int32),
                   jax.ShapeDtypeStruct((N_EXPERTS, LANES), F32)],
        scratch_shapes=[pltpu.VMEM((N_EXPERTS, LANES), F32)],
        compiler_params=_cparams(("arbitrary",)),
        name="outproj_router",
    )(o_gla, o_ret, o_s5, wg, wr, ws, x3, mod5, mod5, mod5, nw2, rw_p, rb_p)


GATHER_WIN = 64


def _gather_rows(table, idx):
    m = idx.shape[0]
    w = table.shape[1]
    mesh = plsc.VectorSubcoreMesh(core_axis_name="core", subcore_axis_name="subcore")

    @functools.partial(pl.kernel, out_type=jax.ShapeDtypeStruct((m, w), table.dtype),
                       mesh=mesh, name="sc_row_gather")
    def gather(x_hbm, i_hbm, o_hbm):
        def body(i_vmem, o_vmem):
            pltpu.sync_copy(x_hbm.at[i_vmem], o_vmem)

        pltpu.emit_pipeline(
            body,
            grid=(m // GATHER_WIN,),
            in_specs=[pl.BlockSpec((GATHER_WIN,), lambda i: (i,))],
            out_specs=[pl.BlockSpec((GATHER_WIN, w), lambda i: (i, 0))],
            core_axis_name=("core", "subcore"),
            dimension_semantics=(pltpu.PARALLEL,),
        )(i_hbm, o_hbm)

    return gather(table, idx)


def _scatter_rows(x, dest_slot_major, n_rows):
    t, w = x.shape
    steps = t // GATHER_WIN
    mesh = plsc.VectorSubcoreMesh(core_axis_name="core", subcore_axis_name="subcore")

    @functools.partial(pl.kernel, out_type=jax.ShapeDtypeStruct((n_rows, w), x.dtype),
                       mesh=mesh, name="sc_row_scatter")
    def scatter(x_hbm, i_hbm, o_hbm):
        def body(x_vmem, i0, i1, i2, i3):
            for i_vmem in (i0, i1, i2, i3):
                pltpu.sync_copy(x_vmem, o_hbm.at[i_vmem])

        slot = lambda k: pl.BlockSpec((GATHER_WIN,), lambda i: (k * steps + i,))
        pltpu.emit_pipeline(
            body,
            grid=(steps,),
            in_specs=[pl.BlockSpec((GATHER_WIN, w), lambda i: (i, 0)),
                      slot(0), slot(1), slot(2), slot(3)],
            out_specs=[],
            core_axis_name=("core", "subcore"),
            dimension_semantics=(pltpu.PARALLEL,),
        )(x_hbm, i_hbm, i_hbm, i_hbm, i_hbm)

    return scatter(x, dest_slot_major)


def _expert_kernel(be_ref, ns_ref, rows_ref, wu_ref, bu_ref, wd_ref, bd_ref, o_ref,
                   wu_bf, wd_bf):
    i = pl.program_id(0)
    e = be_ref[i]
    prev = be_ref[jnp.maximum(i - 1, 0)]

    @pl.when((i == 0) | (e != prev))
    def _():
        wu_bf[...] = wu_ref[0, 0].astype(BF16)
        wd_bf[...] = wd_ref[0, 0].astype(BF16)

    def run_rows(r0, n, first):
        rs = slice(r0, r0 + n)
        row = lax.broadcasted_iota(jnp.int32, (n, rows_ref.shape[1]), 0)
        x_hi, x_lo = _unpack_bf16_pairs(jnp.where(row >= first, rows_ref[rs, :], jnp.uint32(0)))
        x = jnp.concatenate([x_hi.astype(BF16), x_lo.astype(BF16)], axis=1)
        up = jnp.dot(x, wu_bf[...], preferred_element_type=F32) + bu_ref[0, 0]
        x_glu = jnp.minimum(up[:, :D_FF], SWIGLU_LIMIT)
        x_lin = jnp.clip(up[:, D_FF:], -SWIGLU_LIMIT, SWIGLU_LIMIT)
        act = x_glu * jax.nn.sigmoid(SWIGLU_ALPHA * x_glu) * (x_lin + 1.0)
        o_ref[rs, :] = _pack_bf16_pairs(
            jnp.dot(act.astype(BF16), wd_bf[...], preferred_element_type=F32) + bd_ref[0, 0])

    def zero_rows(r0, n):
        o_ref[r0:r0 + n, :] = jnp.zeros((n, o_ref.shape[1]), o_ref.dtype)

    half = ROW_SUB // 2
    for s in range(ROW_BLK // ROW_SUB):
        r0 = s * ROW_SUB
        first = ns_ref[i] - r0

        @pl.when(first < half)
        def _():
            run_rows(r0, ROW_SUB, first)

        @pl.when((first >= half) & (first < ROW_SUB))
        def _():
            zero_rows(r0, half)
            run_rows(r0 + half, half, first - half)

        @pl.when(first >= ROW_SUB)
        def _():
            zero_rows(r0, ROW_SUB)


def _experts(layer, block_e, n_skip, rows, w_up, b_up, w_down, b_down):
    n_rows, dh = rows.shape
    d = 2 * dh
    n_blocks = n_rows // ROW_BLK
    depth, ne, _, f2 = w_up.shape
    wsel = lambda i, be, nu: (layer, be[i], 0, 0)
    grid_spec = pltpu.PrefetchScalarGridSpec(
        num_scalar_prefetch=2,
        grid=(n_blocks,),
        in_specs=[pl.BlockSpec((ROW_BLK, dh), lambda i, be, nu: (i, 0)),
                  pl.BlockSpec((1, 1, d, f2), wsel),
                  pl.BlockSpec((1, 1, 1, f2), wsel),
                  pl.BlockSpec((1, 1, D_FF, d), wsel),
                  pl.BlockSpec((1, 1, 1, d), wsel)],
        out_specs=pl.BlockSpec((ROW_BLK, dh), lambda i, be, nu: (i, 0)),
        scratch_shapes=[pltpu.VMEM((d, f2), BF16), pltpu.VMEM((D_FF, d), BF16)],
    )
    return pl.pallas_call(
        _expert_kernel,
        grid_spec=grid_spec,
        out_shape=jax.ShapeDtypeStruct((n_rows, dh), jnp.uint32),
        compiler_params=_cparams(("arbitrary",)),
        name="moe_experts",
    )(block_e, n_skip, rows, w_up, b_up.reshape(depth, ne, 1, f2), w_down,
      b_down.reshape(depth, ne, 1, d))


def _combine_kernel(y0_ref, y1_ref, y2_ref, y3_ref, gate_ref, x1_ref, g2_ref, fw_ref, *rest,
                    final):
    o_ref = rest[-1]
    gates = gate_ref[...]
    y_hi, y_lo = None, None
    for k, y_ref in enumerate((y0_ref, y1_ref, y2_ref, y3_ref)):
        hi, lo = _unpack_bf16_pairs(y_ref[...])
        g = gates[:, k:k + 1]
        y_hi = g * hi if y_hi is None else y_hi + g * hi
        y_lo = g * lo if y_lo is None else y_lo + g * lo
    y = jnp.concatenate([y_hi, y_lo], axis=1)
    x2 = x1_ref[...] + g2_ref[0] * y
    if final:
        x2 = (x2 * lax.rsqrt(jnp.mean(x2 * x2, axis=-1, keepdims=True) + NORM_EPS)) * fw_ref[...]
    o_ref[...] = x2


def _combine(y4, gates, x1, mod5, b0, layer, fw, seq, final, grp, n_out, prev_out, th=1024):
    t, d = x1.shape
    th = min(th, seq)
    steps = t // th
    per_b = seq // th
    slot = lambda k: pl.BlockSpec((th, d // 2), lambda i: (k * steps + i, 0))
    return pl.pallas_call(
        functools.partial(_combine_kernel, final=final),
        grid=(steps,),
        in_specs=[slot(0), slot(1), slot(2), slot(3),
                  pl.BlockSpec((th, TOP_K), lambda i: (i, 0)),
                  pl.BlockSpec((th, d), lambda i: (i, 0)),
                  _mod_spec(layer, MOD_G2, per_b, b0),
                  pl.BlockSpec((1, d), lambda i: (0, 0))]
                 + ([] if prev_out is None else [pl.BlockSpec(memory_space=pl.ANY)]),
        out_specs=pl.BlockSpec((None, th, d), lambda i: (grp, i, 0)),
        out_shape=jax.ShapeDtypeStruct((n_out, t, d), F32),
        input_output_aliases={} if prev_out is None else {8: 0},
        compiler_params=_cparams(("arbitrary",)),
        name="moe_combine",
    )(y4, y4, y4, y4, gates, x1, mod5, fw, *([] if prev_out is None else [prev_out]))


def _retention_tables(seq):
    f32 = np.float32
    pos = np.arange(seq, dtype=f32)
    inv_freq = (f32(ROPE_BASE) ** (-np.arange(0, HEAD_DK, 2, dtype=f32) / f32(HEAD_DK))).astype(f32)
    ang = pos[:, None] * inv_freq[None, :]
    cos, sin = np.cos(ang), np.sin(ang)
    zero = np.zeros_like(sin)
    zpad = np.zeros((seq, LANES - 2 * HEAD_DK), f32)
    cos_t = np.concatenate([cos, cos, cos, cos, zpad], axis=1)
    sina_t = np.concatenate([-sin, zero, -sin, zero, zpad], axis=1)
    sinb_t = np.concatenate([zero, sin, zero, sin, zpad], axis=1)
    log_gamma = np.log1p(-np.exp2(f32(-5.0) - np.arange(N_HEADS, dtype=f32))).astype(f32)
    log_decay = np.broadcast_to(log_gamma[None, :, None], (RET_CHUNK, N_HEADS, HEAD_DK))
    cum = np.cumsum(log_decay, axis=0, dtype=f32)
    tot = cum[-1:]

    def shp(a):
        flat = a.reshape(a.shape[0], N_HEADS * HEAD_DK)
        return np.where(_DK_SRC >= 0, flat[:, np.maximum(_DK_SRC, 0)], f32(1.0)).astype(f32)

    tabs = (cos_t, sina_t, sinb_t, shp(np.exp(cum)), shp(np.exp(-cum)), shp(np.exp(tot - cum)),
            shp(np.exp(tot)))
    return tuple(jnp.asarray(a, F32) for a in tabs)


def kernel(x, c, norm1_w, norm2_w, w_mod, b_mod, w_in, gla_w_a2, gla_b_a, gla_norm_w, ret_norm_w, s5_a_re, s5_a_im, s5_log_dt, s5_b_re, s5_b_im, s5_c_re, s5_c_im, s5_d, s5_w_glu, s5_b_glu, w_out, router_w, router_b, w_up, b_up, w_down, b_down, final_norm_w):
    batch, seq, d = x.shape
    depth = w_mod.shape[0]
    gb = batch // N_STREAMS
    t = gb * seq
    n_slots = t * TOP_K
    n_blocks = n_slots // ROW_BLK + N_EXPERTS
    n_rows = n_blocks * ROW_BLK

    mod = _modulation(c, w_mod, b_mod)
    mod5 = mod.reshape(depth, batch, 6, 1, d).transpose(0, 2, 1, 3, 4)
    ret_tabs = _retention_tables(seq)

    w_p = _take_cols(w_in, _IN_SRC).astype(BF16)
    wa_p = jnp.zeros((depth, LANES, QKP), F32).at[:, :GATE_RANK].set(
        _take_cols(gla_w_a2, _DK_SRC)).astype(BF16)
    ba_p = _take_cols(gla_b_a, _DK_SRC).reshape(depth, 1, QKP)
    gnw = _take_cols(gla_norm_w, _DV_SRC).reshape(depth, 1, HP)
    rnw = _take_cols(ret_norm_w, _DV_SRC).reshape(depth, 1, HP)
    kv = N_HEADS * HEAD_DV
    wo_g = _take_rows(w_out[:, :kv], _DV_SRC).astype(BF16)
    wo_r = _take_rows(w_out[:, kv:2 * kv], _DV_SRC).astype(BF16)
    wo_s = w_out[:, 2 * kv:].astype(BF16)
    rw_t = jnp.swapaxes(router_w, 1, 2)
    rw_hi = rw_t.astype(BF16)
    rw_p = jnp.concatenate([rw_hi, (rw_t - rw_hi.astype(F32)).astype(BF16)], axis=1)
    rb_p = jnp.broadcast_to(router_b[:, :, None], (depth, N_EXPERTS, LANES))
    s5_tabs = jax.vmap(_s5_tables)(s5_a_re, s5_a_im, s5_log_dt, s5_b_re, s5_b_im, s5_c_re,
                                   s5_c_im, s5_d, s5_w_glu, s5_b_glu)
    n1 = norm1_w.reshape(depth, 1, d)
    n2 = norm2_w.reshape(depth, 1, d)
    fw = final_norm_w.reshape(1, d)

    xs = [(x.reshape(N_STREAMS, t, d), g) for g in range(N_STREAMS)]
    out = None

    for i in range(depth):
        final = i == depth - 1
        for g in range(N_STREAMS):
            x3, grp = xs[g]
            b0 = g * gb

            proj = _in_projection(x3, grp, mod5, b0, i, n1, w_p, seq)
            o_gla = _gla_mixer(proj, gb, seq, i, wa_p, ba_p, gnw)
            o_ret = _ret_mixer(proj, gb, seq, i, *ret_tabs, rnw)
            o_s5 = _s5_mixer(proj, gb, seq, i, s5_tabs)

            x1, hdn, idx, gates, rank, counts = _outproj_router(
                o_gla, o_ret, o_s5, wo_g, wo_r, wo_s, x3, grp, mod5, b0, i, n2, rw_p, rb_p, seq)

            cnt = counts[:, 0].astype(jnp.int32)
            padded = (cnt + ROW_BLK - 1) // ROW_BLK * ROW_BLK
            pad_ends = jnp.cumsum(padded)
            first_row = pad_ends - cnt
            slot_start = jnp.sum(jnp.where(idx[:TOP_K, :, None] == jnp.arange(N_EXPERTS),
                                           first_row, 0), axis=-1)
            dest_sm = (slot_start + rank[:TOP_K]).astype(jnp.int32).reshape(-1)
            gates_tm = gates[:TOP_K].T
            blk_start = jnp.arange(n_blocks, dtype=jnp.int32) * ROW_BLK
            block_e = jnp.minimum(jnp.sum(pad_ends[None, :] <= blk_start[:, None], axis=1),
                                  N_EXPERTS - 1).astype(jnp.int32)
            n_skip = jnp.where(blk_start < pad_ends[-1],
                               jnp.clip(first_row[block_e] - blk_start, 0, ROW_BLK),
                               ROW_BLK).astype(jnp.int32)

            rows = _scatter_rows(hdn, dest_sm, n_rows)
            out_rows = _experts(i, block_e, n_skip, rows, w_up, b_up, w_down, b_down)
            y4 = _gather_rows(out_rows, dest_sm)
            if final:
                out = _combine(y4, gates_tm, x1, mod5, b0, i, fw, seq, True, g, N_STREAMS, out)
            else:
                xs[g] = (_combine(y4, gates_tm, x1, mod5, b0, i, fw, seq, False, 0, 1, None), 0)

    return out.reshape(batch, seq, d)
```

```python
import functools

import numpy as np
import jax
import jax.numpy as jnp
from jax import lax
from jax.experimental import pallas as pl
from jax.experimental.pallas import tpu as pltpu
from jax.experimental.pallas import tpu_sc as plsc

D_MODEL = 1024
CHUNK = 64
RET_CHUNK = 128
NORM_EPS = 1e-5
N_HEADS = 4
HEAD_DK = 48
HEAD_DV = 96
GATE_RANK = 16
GATE_TEMP = 16.0
ROPE_BASE = 10000.0
S5_WIDTH = 256
S5_GROUP_DIM = 16
S5_GROUPS = 16
S5_STATE = 64
N_EXPERTS = 32
TOP_K = 4
D_FF = 1024
SWIGLU_LIMIT = 7.0
SWIGLU_ALPHA = 1.702

LANES = 128
HEAD_PAD = LANES
HP = N_HEADS * HEAD_PAD
N_PAIRS = N_HEADS // 2
QKP = N_PAIRS * LANES
VMEM_LIMIT = 56 * 1024 * 1024

OFF_GQ, OFF_GK, OFF_GV, OFF_GG = 0, QKP, 2 * QKP, 2 * QKP + HP
OFF_RQ = OFF_GG + HP
OFF_RK, OFF_RV, OFF_RG = OFF_RQ + QKP, OFF_RQ + 2 * QKP, OFF_RQ + 2 * QKP + HP
OFF_SU = OFF_RG + HP
OFF_GA = OFF_SU + S5_WIDTH
NP_COLS = OFF_GA + LANES
PROJ_CHUNKS = (1280, 1280, 896)

ROW_BLK = 1024
ROW_SUB = 512
MOD_SH1, MOD_SC1, MOD_G1, MOD_SH2, MOD_SC2, MOD_G2 = range(6)
N_STREAMS = 1

F32 = jnp.float32
BF16 = jnp.bfloat16


def _DK_SRC_LANE(h, d):
    return (h // 2) * LANES + (h % 2) * HEAD_DK + d


def _in_col_map():
    src = -np.ones((NP_COLS,), np.int64)
    kq = N_HEADS * HEAD_DK
    kv = N_HEADS * HEAD_DV
    base = dict(gq=0, gk=kq, gv=2 * kq, gg=2 * kq + kv, ga=2 * kq + 2 * kv)
    r0 = base['ga'] + GATE_RANK
    base.update(rq=r0, rk=r0 + kq, rv=r0 + 2 * kq, rg=r0 + 2 * kq + kv, su=r0 + 2 * kq + 2 * kv)
    for h in range(N_HEADS):
        for d in range(HEAD_DK):
            lane = _DK_SRC_LANE(h, d)
            src[OFF_GQ + lane] = base['gq'] + h * HEAD_DK + d
            src[OFF_GK + lane] = base['gk'] + h * HEAD_DK + d
            src[OFF_RQ + lane] = base['rq'] + h * HEAD_DK + d
            src[OFF_RK + lane] = base['rk'] + h * HEAD_DK + d
        for d in range(HEAD_DV):
            src[OFF_GV + h * HEAD_PAD + d] = base['gv'] + h * HEAD_DV + d
            src[OFF_GG + h * HEAD_PAD + d] = base['gg'] + h * HEAD_DV + d
            src[OFF_RV + h * HEAD_PAD + d] = base['rv'] + h * HEAD_DV + d
            src[OFF_RG + h * HEAD_PAD + d] = base['rg'] + h * HEAD_DV + d
    src[OFF_SU:OFF_SU + S5_WIDTH] = base['su'] + np.arange(S5_WIDTH)
    src[OFF_GA:OFF_GA + GATE_RANK] = base['ga'] + np.arange(GATE_RANK)
    return src


_IN_SRC = _in_col_map()


def _head_pad_map(width):
    src = -np.ones((HP,), np.int64)
    for h in range(N_HEADS):
        src[h * HEAD_PAD:h * HEAD_PAD + width] = h * width + np.arange(width)
    return src


_DV_SRC = _head_pad_map(HEAD_DV)
_DK_SRC = -np.ones((QKP,), np.int64)
for _h in range(N_HEADS):
    for _d in range(HEAD_DK):
        _DK_SRC[_DK_SRC_LANE(_h, _d)] = _h * HEAD_DK + _d


def _take_static(w, src, axis):
    axis = axis % w.ndim
    pieces, start = [], 0
    for j in range(1, len(src) + 1):
        run_ends = (j == len(src) or (src[j] < 0) != (src[start] < 0)
                    or (src[start] >= 0 and src[j] != src[j - 1] + 1))
        if run_ends:
            if src[start] < 0:
                shape = w.shape[:axis] + (j - start,) + w.shape[axis + 1:]
                pieces.append(jnp.zeros(shape, w.dtype))
            else:
                pieces.append(lax.slice_in_dim(w, int(src[start]), int(src[j - 1]) + 1, axis=axis))
            start = j
    return jnp.concatenate(pieces, axis=axis)


def _take_cols(w, src):
    return _take_static(w, src, -1)


def _take_rows(w, src):
    return _take_static(w, src, -2)


def _layer_spec(shape, layer):
    return pl.BlockSpec((None,) + tuple(shape), lambda *_: (layer,) + (0,) * len(shape))


def _mod_spec(layer, which, per_b, b0):
    return pl.BlockSpec((None, None, 1, 1, D_MODEL),
                        lambda i: (layer, which, b0 + i // per_b, 0, 0))


def _cparams(sem):
    return pltpu.CompilerParams(dimension_semantics=sem, vmem_limit_bytes=VMEM_LIMIT)


def _mod_kernel(c_ref, w_ref, b_ref, o_ref):
    c = c_ref[...]
    cond = c * jax.nn.sigmoid(c)
    o_ref[0] = jnp.dot(cond, w_ref[0], preferred_element_type=F32,
                       precision=lax.Precision.HIGHEST) + b_ref[0]


def _modulation(c, w_mod, b_mod):
    depth, d, n = w_mod.shape
    b = c.shape[0]
    nb = 1536
    return pl.pallas_call(
        _mod_kernel,
        grid=(depth, n // nb),
        in_specs=[pl.BlockSpec((b, d), lambda l, j: (0, 0)),
                  pl.BlockSpec((1, d, nb), lambda l, j: (l, 0, j)),
                  pl.BlockSpec((1, 1, nb), lambda l, j: (l, 0, j))],
        out_specs=pl.BlockSpec((1, b, nb), lambda l, j: (l, 0, j)),
        out_shape=jax.ShapeDtypeStruct((depth, b, n), F32),
        compiler_params=_cparams(("arbitrary", "arbitrary")),
        name="adaln_mod",
    )(c, w_mod, b_mod.reshape(depth, 1, n))


def _rms_mod(x, nw, sc, sh):
    y = x * lax.rsqrt(jnp.mean(x * x, axis=-1, keepdims=True) + NORM_EPS)
    return (y * nw) * (1.0 + sc) + sh


def _inproj_kernel(x_ref, sc_ref, sh_ref, nw_ref, w_ref, o_ref):
    h = _rms_mod(x_ref[...], nw_ref[...], sc_ref[0], sh_ref[0]).astype(BF16)
    assert sum(PROJ_CHUNKS) == NP_COLS
    for j, width in enumerate(PROJ_CHUNKS):
        cs = slice(sum(PROJ_CHUNKS[:j]), sum(PROJ_CHUNKS[:j]) + width)
        o_ref[:, cs] = jnp.dot(h, w_ref[:, cs], preferred_element_type=F32).astype(BF16)


def _in_projection(x3, grp, mod5, b0, layer, nw, w_p, seq, tm=1024):
    _, t, d = x3.shape
    tm = min(tm, seq)
    per_b = seq // tm
    return pl.pallas_call(
        _inproj_kernel,
        grid=(t // tm,),
        in_specs=[pl.BlockSpec((None, tm, d), lambda i: (grp, i, 0)),
                  _mod_spec(layer, MOD_SC1, per_b, b0), _mod_spec(layer, MOD_SH1, per_b, b0),
                  _layer_spec((1, d), layer), _layer_spec((d, NP_COLS), layer)],
        out_specs=pl.BlockSpec((tm, NP_COLS), lambda i: (i, 0)),
        out_shape=jax.ShapeDtypeStruct((t, NP_COLS), BF16),
        compiler_params=_cparams(("arbitrary",)),
        name="in_proj",
    )(x3, mod5, mod5, nw, w_p)


_NT = (((1,), (1,)), ((), ()))
_TN = (((0,), (0,)), ((), ()))


def _tri_mask(n=CHUNK):
    r = lax.broadcasted_iota(jnp.int32, (n, n), 0)
    c = lax.broadcasted_iota(jnp.int32, (n, n), 1)
    return r >= c


def _pair_masks(rows):
    lane = lax.broadcasted_iota(jnp.int32, (rows, LANES), 1)
    return lane < HEAD_DK, (lane >= HEAD_DK) & (lane < 2 * HEAD_DK)


def _head_attention(qd, ki, ke, vh, et, st_ref, h, causal, vt=None):
    qb = qd.astype(BF16)
    sc = lax.dot_general(qb, ki.astype(BF16), _NT, preferred_element_type=F32)
    sc = jnp.where(causal, sc, 0.0)
    st = st_ref[h]
    o = jnp.dot(sc.astype(BF16), vh, preferred_element_type=F32)
    o = o + lax.dot_general(qb, st.astype(BF16), _NT, preferred_element_type=F32)
    if vt is None:
        upd = lax.dot_general(vh, ke.astype(BF16), _TN, preferred_element_type=F32)
    else:
        upd = jnp.dot(vt, ke.astype(BF16), preferred_element_type=F32)
    st_ref[h] = st * et + upd
    return o


def _gla_kernel(q_ref, k_ref, v_ref, g_ref, a_ref, wa_ref, ba_ref, nw_ref, tri_ref, o_ref,
                st_ref, qd_s, ki_s, ke_s, et_s, vt_s):
    @pl.when(pl.program_id(1) == 0)
    def _():
        st_ref[...] = jnp.zeros_like(st_ref)

    causal = _tri_mask()
    tl = q_ref.shape[0]
    n_chunks = tl // CHUNK

    z = jnp.dot(a_ref[...], wa_ref[...], preferred_element_type=F32) + ba_ref[...]
    la = (jnp.minimum(z, 0.0) - jnp.log1p(jnp.exp(-jnp.abs(z)))) * (1.0 / GATE_TEMP)
    hi = la.astype(BF16)
    lo = (la - hi.astype(F32)).astype(BF16)
    cum = (jnp.dot(tri_ref[...], hi, preferred_element_type=F32)
           + jnp.dot(tri_ref[...], lo, preferred_element_type=F32))
    cum3 = cum.reshape(n_chunks, CHUNK, QKP)
    tot3 = cum3[:, CHUNK - 1:CHUNK, :]
    qd = (q_ref[...].astype(F32) * (HEAD_DK ** -0.5)) * jnp.exp(cum)
    masks = _pair_masks(tl)
    for h in range(N_HEADS):
        pair = slice((h // 2) * LANES, (h // 2 + 1) * LANES)
        qd_s[:, h * HEAD_PAD:(h + 1) * HEAD_PAD] = jnp.where(masks[h % 2], qd[:, pair],
                                                              0.0).astype(BF16)
    kf = k_ref[...].astype(F32)
    ki_s[...] = (kf * jnp.exp(-cum)).astype(BF16)
    ke_s[...] = (kf * jnp.exp(tot3 - cum3).reshape(tl, QKP)).astype(BF16)
    et_s[...] = jnp.exp(tot3).reshape(n_chunks, QKP)
    for c in range(n_chunks):
        vt_s[c] = v_ref[c * CHUNK:(c + 1) * CHUNK, :].T

    def chunk(c, carry):
        r = pl.ds(pl.multiple_of(c * CHUNK, CHUNK), CHUNK)
        et = et_s[pl.ds(c, 1), :]
        for h in range(N_HEADS):
            sl = slice(h * HEAD_PAD, (h + 1) * HEAD_PAD)
            pair = slice((h // 2) * LANES, (h // 2 + 1) * LANES)
            o = _head_attention(qd_s[r, sl], ki_s[r, pair], ke_s[r, pair], v_ref[r, sl],
                                et[:, pair], st_ref, h, causal, vt=vt_s[c, sl, :])
            ms = jnp.sum(o * o, axis=-1, keepdims=True) * (1.0 / HEAD_DV)
            y = (o * lax.rsqrt(ms + NORM_EPS)) * nw_ref[:, sl]
            g = g_ref[r, sl].astype(F32)
            o_ref[r, sl] = (y * (g * jax.nn.sigmoid(g))).astype(BF16)
        return carry

    lax.fori_loop(0, n_chunks, chunk, 0, unroll=4)


def _ret_kernel(q_ref, k_ref, v_ref, g_ref, cos_ref, sina_ref, sinb_ref, dq_ref, dki_ref,
                dke_ref, dt_ref, nw_ref, o_ref, st_ref):
    @pl.when(pl.program_id(1) == 0)
    def _():
        st_ref[...] = jnp.zeros_like(st_ref)

    causal = _tri_mask(RET_CHUNK)
    n_chunks = q_ref.shape[0] // RET_CHUNK
    lane = lax.broadcasted_iota(jnp.int32, (RET_CHUNK, HEAD_PAD), 1)
    real = lane < HEAD_DV
    masks = _pair_masks(RET_CHUNK)
    half = HEAD_DK // 2

    def rotary(t, cos, sina, sinb):
        return (t * cos + pltpu.roll(t, LANES - half, 1) * sina + pltpu.roll(t, half, 1) * sinb)

    def chunk(c, carry):
        r = pl.ds(pl.multiple_of(c * RET_CHUNK, RET_CHUNK), RET_CHUNK)
        cos, sina, sinb = cos_ref[r, :], sina_ref[r, :], sinb_ref[r, :]
        pair_q, pair_ki, pair_ke = [], [], []
        for p in range(N_PAIRS):
            ps = slice(p * LANES, (p + 1) * LANES)
            qr = rotary(q_ref[r, ps].astype(F32), cos, sina, sinb) * dq_ref[:, ps]
            kr = rotary(k_ref[r, ps].astype(F32), cos, sina, sinb) * (HEAD_DK ** -0.5)
            pair_q.append(qr)
            pair_ki.append((kr * dki_ref[:, ps]).astype(BF16))
            pair_ke.append((kr * dke_ref[:, ps]).astype(BF16))
        for h in range(N_HEADS):
            sl = slice(h * HEAD_PAD, (h + 1) * HEAD_PAD)
            p = h // 2
            qd = jnp.where(masks[h % 2], pair_q[p], 0.0)
            o = _head_attention(qd, pair_ki[p], pair_ke[p], v_ref[r, sl],
                                dt_ref[:, p * LANES:(p + 1) * LANES], st_ref, h, causal)
            mu = jnp.sum(o, axis=-1, keepdims=True) * (1.0 / HEAD_DV)
            oc = jnp.where(real, o - mu, 0.0)
            var = jnp.sum(oc * oc, axis=-1, keepdims=True) * (1.0 / HEAD_DV)
            y = (oc * lax.rsqrt(var + NORM_EPS)) * nw_ref[:, sl]
            g = g_ref[r, sl].astype(F32)
            o_ref[r, sl] = (y * (g * jax.nn.sigmoid(g))).astype(BF16)
        return carry

    lax.fori_loop(0, n_chunks, chunk, 0, unroll=2)


def _proj_spec(tl, width, col_off, per_b):
    cb = col_off // width
    return pl.BlockSpec((tl, width), lambda b, l: (b * per_b + l, cb))


def _full(shape):
    return pl.BlockSpec(shape, lambda b, l: (0,) * len(shape))


def _gla_mixer(proj, batch, seq, layer, wa_p, ba_p, nw_p, tl=512):
    per_b = seq // tl
    pos = np.arange(tl)
    tri_bd = jnp.asarray((pos[:, None] // CHUNK == pos[None, :] // CHUNK)
                         & (pos[:, None] >= pos[None, :]), BF16)
    return pl.pallas_call(
        _gla_kernel,
        grid=(batch, per_b),
        in_specs=[_proj_spec(tl, QKP, OFF_GQ, per_b), _proj_spec(tl, QKP, OFF_GK, per_b),
                  _proj_spec(tl, HP, OFF_GV, per_b), _proj_spec(tl, HP, OFF_GG, per_b),
                  _proj_spec(tl, LANES, OFF_GA, per_b),
                  _layer_spec((LANES, QKP), layer), _layer_spec((1, QKP), layer),
                  _layer_spec((1, HP), layer), _full((tl, tl))],
        out_specs=pl.BlockSpec((tl, HP), lambda b, l: (b * per_b + l, 0)),
        out_shape=jax.ShapeDtypeStruct((batch * seq, HP), BF16),
        scratch_shapes=[pltpu.VMEM((N_HEADS, HEAD_PAD, HEAD_PAD), F32),
                        pltpu.VMEM((tl, HP), BF16), pltpu.VMEM((tl, QKP), BF16),
                        pltpu.VMEM((tl, QKP), BF16), pltpu.VMEM((tl // CHUNK, QKP), F32),
                        pltpu.VMEM((tl // CHUNK, HP, CHUNK), BF16)],
        compiler_params=_cparams(("arbitrary", "arbitrary")),
        name="gla_mixer",
    )(proj, proj, proj, proj, proj, wa_p, ba_p, nw_p, tri_bd)


def _ret_mixer(proj, batch, seq, layer, cos_t, sina_t, sinb_t, dq, dki, dke, dtot, nw_p, tl=512):
    per_b = seq // tl
    return pl.pallas_call(
        _ret_kernel,
        grid=(batch, per_b),
        in_specs=[_proj_spec(tl, QKP, OFF_RQ, per_b), _proj_spec(tl, QKP, OFF_RK, per_b),
                  _proj_spec(tl, HP, OFF_RV, per_b), _proj_spec(tl, HP, OFF_RG, per_b),
                  pl.BlockSpec((tl, LANES), lambda b, l: (l, 0)),
                  pl.BlockSpec((tl, LANES), lambda b, l: (l, 0)),
                  pl.BlockSpec((tl, LANES), lambda b, l: (l, 0)),
                  _full((RET_CHUNK, QKP)), _full((RET_CHUNK, QKP)), _full((RET_CHUNK, QKP)),
                  _full((1, QKP)), _layer_spec((1, HP), layer)],
        out_specs=pl.BlockSpec((tl, HP), lambda b, l: (b * per_b + l, 0)),
        out_shape=jax.ShapeDtypeStruct((batch * seq, HP), BF16),
        scratch_shapes=[pltpu.VMEM((N_HEADS, HEAD_PAD, HEAD_PAD), F32)],
        compiler_params=_cparams(("arbitrary", "arbitrary")),
        name="ret_mixer",
    )(proj, proj, proj, proj, cos_t, sina_t, sinb_t, dq, dki, dke, dtot, nw_p)


S5_N = S5_GROUPS * S5_STATE
S5_SLAB = 256


def _gelu_tanh(x):
    return 0.5 * x * (1.0 + jnp.tanh(np.sqrt(2.0 / np.pi) * (x + 0.044715 * (x * x * x))))


def _s5_kernel(u_ref, bb_ref, cb_ref, nr_ref, ni_ref, pr_ref, pi_ref, lr_ref, li_ref,
               d_ref, wg_ref, bg_ref, o_ref, sr_ref, si_ref, x_scr, s_scr):
    @pl.when(pl.program_id(1) == 0)
    def _():
        sr_ref[...] = jnp.zeros_like(sr_ref)
        si_ref[...] = jnp.zeros_like(si_ref)

    tri = _tri_mask().astype(BF16)
    n_chunks = u_ref.shape[0] // CHUNK
    u = u_ref[...]
    x_scr[...] = jnp.dot(u, bb_ref[...], preferred_element_type=F32)

    def chunk(c, carry):
        r = pl.ds(pl.multiple_of(c * CHUNK, CHUNK), CHUNK)
        for j in range(S5_N // S5_SLAB):
            cs = slice(j * S5_SLAB, (j + 1) * S5_SLAB)
            ci = slice(S5_N + j * S5_SLAB, S5_N + (j + 1) * S5_SLAB)
            xr, xi = x_scr[r, cs], x_scr[r, ci]
            nr, ni = nr_ref[:, cs], ni_ref[:, cs]
            p_r = jnp.dot(tri, (xr * nr - xi * ni).astype(BF16), preferred_element_type=F32)
            p_i = jnp.dot(tri, (xr * ni + xi * nr).astype(BF16), preferred_element_type=F32)
            s0r, s0i = sr_ref[:, cs], si_ref[:, cs]
            lr, li = lr_ref[:, cs], li_ref[:, cs]
            q_r = p_r + (s0r * lr - s0i * li)
            q_i = p_i + (s0r * li + s0i * lr)
            pr, pi = pr_ref[:, cs], pi_ref[:, cs]
            s_r = q_r * pr - q_i * pi
            s_i = q_r * pi + q_i * pr
            sr_ref[:, cs] = s_r[CHUNK - 1:CHUNK, :]
            si_ref[:, cs] = s_i[CHUNK - 1:CHUNK, :]
            s_scr[r, cs] = s_r.astype(BF16)
            s_scr[r, ci] = s_i.astype(BF16)
        return carry

    lax.fori_loop(0, n_chunks, chunk, 0, unroll=2)
    y = jnp.dot(s_scr[...], cb_ref[...], preferred_element_type=F32)
    y = _gelu_tanh(y + d_ref[...] * u.astype(F32))
    gate = jnp.dot(y.astype(BF16), wg_ref[...], preferred_element_type=F32) + bg_ref[...]
    o_ref[...] = (y * jax.nn.sigmoid(gate)).astype(BF16)


def _s5_mixer(proj, batch, seq, layer, tabs, tl=512):
    per_b = seq // tl
    bb, cb, nr, ni, pr, pi, lr, li, dsk, wg, bg = tabs
    return pl.pallas_call(
        _s5_kernel,
        grid=(batch, per_b),
        in_specs=[_proj_spec(tl, S5_WIDTH, OFF_SU, per_b),
                  *[_layer_spec(a.shape[1:], layer) for a in tabs]],
        out_specs=pl.BlockSpec((tl, S5_WIDTH), lambda b, l: (b * per_b + l, 0)),
        out_shape=jax.ShapeDtypeStruct((batch * seq, S5_WIDTH), BF16),
        scratch_shapes=[pltpu.VMEM((1, S5_N), F32), pltpu.VMEM((1, S5_N), F32),
                        pltpu.VMEM((tl, 2 * S5_N), F32), pltpu.VMEM((tl, 2 * S5_N), BF16)],
        compiler_params=_cparams(("arbitrary", "arbitrary")),
        name="s5_mixer",
    )(proj, bb, cb, nr, ni, pr, pi, lr, li, dsk, wg, bg)


def _s5_tables(a_re, a_im, log_dt, b_re, b_im, c_re, c_im, d_skip, w_glu, b_glu):
    lam = lax.complex(a_re, a_im)
    dt = jnp.exp(log_dt)[:, None]
    lam_bar = jnp.exp(lam * dt)
    b_bar = ((lam_bar - 1.0) / lam)[..., None] * lax.complex(b_re, b_im)
    eye = jnp.eye(S5_GROUPS, dtype=F32)
    def blk_b(m):
        return jnp.einsum('gph,gk->ghkp', m, eye).reshape(S5_WIDTH, S5_N)
    bb = jnp.concatenate([blk_b(jnp.real(b_bar)), blk_b(jnp.imag(b_bar))], axis=1)
    def blk_c(m):
        return jnp.einsum('ghp,gk->kpgh', m, eye).reshape(S5_N, S5_WIDTH)
    cb = jnp.concatenate([blk_c(c_re), blk_c(-c_im)], axis=0)
    steps = jnp.arange(CHUNK, dtype=F32)[:, None, None]
    lam_dt = (lam * dt)[None]
    pos = jnp.exp(lam_dt * steps).reshape(CHUNK, S5_N)
    neg = jnp.exp(-lam_dt * steps).reshape(CHUNK, S5_N)
    one = lam_bar.reshape(1, S5_N)
    return (bb.astype(BF16), cb.astype(BF16), jnp.real(neg), jnp.imag(neg), jnp.real(pos),
            jnp.imag(pos), jnp.real(one), jnp.imag(one), d_skip.reshape(1, S5_WIDTH),
            w_glu.astype(BF16), b_glu.reshape(1, S5_WIDTH))


def _pack_bf16_pairs(x):
    w = x.shape[1] // 2
    xb = x.astype(BF16).astype(F32)
    hi = lax.bitcast_convert_type(xb[:, :w], jnp.uint32)
    lo = lax.bitcast_convert_type(xb[:, w:], jnp.uint32)
    return hi | (lo >> 16)


def _unpack_bf16_pairs(p):
    hi = lax.bitcast_convert_type(p & jnp.uint32(0xFFFF0000), F32)
    lo = lax.bitcast_convert_type(p << 16, F32)
    return hi, lo


def _router_kernel(og_ref, or_ref, os_ref, wg_ref, wr_ref, ws_ref, x_ref, g1_ref, sc_ref,
                   sh_ref, nw_ref, rw_ref, rb_ref,
                   x1_ref, h_ref, idx_ref, gate_ref, rank_ref, cnt_ref, carry_ref):
    i = pl.program_id(0)

    @pl.when(i == 0)
    def _():
        carry_ref[...] = jnp.zeros_like(carry_ref)

    mixed = jnp.concatenate([og_ref[...], or_ref[...], os_ref[...]], axis=1)
    w_out = jnp.concatenate([wg_ref[...], wr_ref[...], ws_ref[...]], axis=0)
    mix = jnp.dot(mixed, w_out, preferred_element_type=F32)
    x1 = x_ref[...] + g1_ref[0] * mix
    x1_ref[...] = x1
    hdn = _rms_mod(x1, nw_ref[...], sc_ref[0], sh_ref[0])
    h_ref[...] = _pack_bf16_pairs(hdn)
    h_hi = hdn.astype(BF16)
    h_lo = (hdn - h_hi.astype(F32)).astype(BF16)
    p = lax.dot_general(rw_ref[...], h_hi, _NT, preferred_element_type=F32)
    q = lax.dot_general(rw_ref[:N_EXPERTS, :], h_lo, _NT, preferred_element_type=F32)
    logits = (p[:N_EXPERTS] + p[N_EXPERTS:] + q) + rb_ref[:, 0:1]
    tm = logits.shape[1]
    eidx = lax.broadcasted_iota(jnp.int32, (N_EXPERTS, tm), 0)
    work = logits
    onehot = jnp.zeros((N_EXPERTS, tm), F32)
    vals, idxs, sels = [], [], []
    for _ in range(TOP_K):
        m = jnp.max(work, axis=0, keepdims=True)
        ix = jnp.min(jnp.where(work == m, eidx, N_EXPERTS), axis=0, keepdims=True)
        sel = eidx == ix
        work = jnp.where(sel, -jnp.inf, work)
        onehot = onehot + sel.astype(F32)
        vals.append(m)
        idxs.append(ix)
        sels.append(sel)
    exps = [jnp.exp(v - vals[0]) for v in vals]
    denom = exps[0] + exps[1] + exps[2] + exps[3]
    r = lax.broadcasted_iota(jnp.int32, (tm, tm), 0)
    c = lax.broadcasted_iota(jnp.int32, (tm, tm), 1)
    earlier = (r < c).astype(BF16)
    before = (jnp.dot(onehot.astype(BF16), earlier, preferred_element_type=F32)
              + carry_ref[:, 0:1])
    row8 = lax.broadcasted_iota(jnp.int32, (8, tm), 0)
    idx_out = jnp.zeros((8, tm), jnp.int32)
    gate_out = jnp.zeros((8, tm), F32)
    rank_out = jnp.zeros((8, tm), F32)
    for k in range(TOP_K):
        rk = jnp.sum(jnp.where(sels[k], before, 0.0), axis=0, keepdims=True)
        idx_out = jnp.where(row8 == k, idxs[k], idx_out)
        gate_out = jnp.where(row8 == k, exps[k] / denom, gate_out)
        rank_out = jnp.where(row8 == k, rk, rank_out)
    idx_ref[...] = idx_out
    gate_ref[...] = gate_out
    rank_ref[...] = rank_out.astype(jnp.int32)
    total = carry_ref[...] + jnp.sum(onehot, axis=1, keepdims=True)
    carry_ref[...] = total
    cnt_ref[...] = total


def _outproj_router(o_gla, o_ret, o_s5, wg, wr, ws, x3, grp, mod5, b0, layer, nw2, rw_p, rb_p,
                    seq, tm=1024):
    _, t, d = x3.shape
    tm = min(tm, seq)
    per_b = seq // tm
    row = lambda w: pl.BlockSpec((tm, w), lambda i: (i, 0))
    full = lambda s: pl.BlockSpec(s, lambda i: (0,) * len(s))
    slot_t = pl.BlockSpec((8, tm), lambda i: (0, i))
    return pl.pallas_call(
        _router_kernel,
        grid=(t // tm,),
        in_specs=[row(HP), row(HP), row(S5_WIDTH), _layer_spec((HP, d), layer),
                  _layer_spec((HP, d), layer), _layer_spec((S5_WIDTH, d), layer),
                  pl.BlockSpec((None, tm, d), lambda i: (grp, i, 0)),
                  _mod_spec(layer, MOD_G1, per_b, b0), _mod_spec(layer, MOD_SC2, per_b, b0),
                  _mod_spec(layer, MOD_SH2, per_b, b0), _layer_spec((1, d), layer),
                  _layer_spec((2 * N_EXPERTS, d), layer), _layer_spec((N_EXPERTS, LANES), layer)],
        out_specs=[row(d), row(d // 2), slot_t, slot_t, slot_t, full((N_EXPERTS, LANES))],
        out_shape=[jax.ShapeDtypeStruct((t, d), F32),
                   jax.ShapeDtypeStruct((t, d // 2), jnp.uint32),
                   jax.ShapeDtypeStruct((8, t), jnp.int32),
                   jax.ShapeDtypeStruct((8, t), F32),
                   jax.ShapeDtypeStruct((8, t), jnp.int32),
                   jax.ShapeDtypeStruct((N_EXPERTS, LANES), F32)],
        scratch_shapes=[pltpu.VMEM((N_EXPERTS, LANES), F32)],
        compiler_params=_cparams(("arbitrary",)),
        name="outproj_router",
    )(o_gla, o_ret, o_s5, wg, wr, ws, x3, mod5, mod5, mod5, nw2, rw_p, rb_p)


GATHER_WIN = 64


def _gather_rows(table, idx):
    m = idx.shape[0]
    w = table.shape[1]
    mesh = plsc.VectorSubcoreMesh(core_axis_name="core", subcore_axis_name="subcore")

    @functools.partial(pl.kernel, out_type=jax.ShapeDtypeStruct((m, w), table.dtype),
                       mesh=mesh, name="sc_row_gather")
    def gather(x_hbm, i_hbm, o_hbm):
        def body(i_vmem, o_vmem):
            pltpu.sync_copy(x_hbm.at[i_vmem], o_vmem)

        pltpu.emit_pipeline(
            body,
            grid=(m // GATHER_WIN,),
            in_specs=[pl.BlockSpec((GATHER_WIN,), lambda i: (i,))],
            out_specs=[pl.BlockSpec((GATHER_WIN, w), lambda i: (i, 0))],
            core_axis_name=("core", "subcore"),
            dimension_semantics=(pltpu.PARALLEL,),
        )(i_hbm, o_hbm)

    return gather(table, idx)


def _scatter_rows(x, dest_slot_major, n_rows):
    t, w = x.shape
    steps = t // GATHER_WIN
    mesh = plsc.VectorSubcoreMesh(core_axis_name="core", subcore_axis_name="subcore")

    @functools.partial(pl.kernel, out_type=jax.ShapeDtypeStruct((n_rows, w), x.dtype),
                       mesh=mesh, name="sc_row_scatter")
    def scatter(x_hbm, i_hbm, o_hbm):
        def body(x_vmem, i0, i1, i2, i3):
            for i_vmem in (i0, i1, i2, i3):
                pltpu.sync_copy(x_vmem, o_hbm.at[i_vmem])

        slot = lambda k: pl.BlockSpec((GATHER_WIN,), lambda i: (k * steps + i,))
        pltpu.emit_pipeline(
            body,
            grid=(steps,),
            in_specs=[pl.BlockSpec((GATHER_WIN, w), lambda i: (i, 0)),
                      slot(0), slot(1), slot(2), slot(3)],
            out_specs=[],
            core_axis_name=("core", "subcore"),
            dimension_semantics=(pltpu.PARALLEL,),
        )(x_hbm, i_hbm, i_hbm, i_hbm, i_hbm)

    return scatter(x, dest_slot_major)


def _expert_kernel(be_ref, ns_ref, rows_ref, wu_ref, bu_ref, wd_ref, bd_ref, o_ref,
                   wu_bf, wd_bf):
    i = pl.program_id(0)
    e = be_ref[i]
    prev = be_ref[jnp.maximum(i - 1, 0)]

    @pl.when((i == 0) | (e != prev))
    def _():
        wu_bf[...] = wu_ref[0, 0].astype(BF16)
        wd_bf[...] = wd_ref[0, 0].astype(BF16)

    def run_rows(r0, n, first):
        rs = slice(r0, r0 + n)
        row = lax.broadcasted_iota(jnp.int32, (n, rows_ref.shape[1]), 0)
        x_hi, x_lo = _unpack_bf16_pairs(jnp.where(row >= first, rows_ref[rs, :], jnp.uint32(0)))
        x = jnp.concatenate([x_hi.astype(BF16), x_lo.astype(BF16)], axis=1)
        up = jnp.dot(x, wu_bf[...], preferred_element_type=F32) + bu_ref[0, 0]
        x_glu = jnp.minimum(up[:, :D_FF], SWIGLU_LIMIT)
        x_lin = jnp.clip(up[:, D_FF:], -SWIGLU_LIMIT, SWIGLU_LIMIT)
        act = x_glu * jax.nn.sigmoid(SWIGLU_ALPHA * x_glu) * (x_lin + 1.0)
        o_ref[rs, :] = _pack_bf16_pairs(
            jnp.dot(act.astype(BF16), wd_bf[...], preferred_element_type=F32) + bd_ref[0, 0])

    def zero_rows(r0, n):
        o_ref[r0:r0 + n, :] = jnp.zeros((n, o_ref.shape[1]), o_ref.dtype)

    half = ROW_SUB // 2
    for s in range(ROW_BLK // ROW_SUB):
        r0 = s * ROW_SUB
        first = ns_ref[i] - r0

        @pl.when(first < half)
        def _():
            run_rows(r0, ROW_SUB, first)

        @pl.when((first >= half) & (first < ROW_SUB))
        def _():
            zero_rows(r0, half)
            run_rows(r0 + half, half, first - half)

        @pl.when(first >= ROW_SUB)
        def _():
            zero_rows(r0, ROW_SUB)


def _experts(layer, block_e, n_skip, rows, w_up, b_up, w_down, b_down):
    n_rows, dh = rows.shape
    d = 2 * dh
    n_blocks = n_rows // ROW_BLK
    depth, ne, _, f2 = w_up.shape
    wsel = lambda i, be, nu: (layer, be[i], 0, 0)
    grid_spec = pltpu.PrefetchScalarGridSpec(
        num_scalar_prefetch=2,
        grid=(n_blocks,),
        in_specs=[pl.BlockSpec((ROW_BLK, dh), lambda i, be, nu: (i, 0)),
                  pl.BlockSpec((1, 1, d, f2), wsel),
                  pl.BlockSpec((1, 1, 1, f2), wsel),
                  pl.BlockSpec((1, 1, D_FF, d), wsel),
                  pl.BlockSpec((1, 1, 1, d), wsel)],
        out_specs=pl.BlockSpec((ROW_BLK, dh), lambda i, be, nu: (i, 0)),
        scratch_shapes=[pltpu.VMEM((d, f2), BF16), pltpu.VMEM((D_FF, d), BF16)],
    )
    return pl.pallas_call(
        _expert_kernel,
        grid_spec=grid_spec,
        out_shape=jax.ShapeDtypeStruct((n_rows, dh), jnp.uint32),
        compiler_params=_cparams(("arbitrary",)),
        name="moe_experts",
    )(block_e, n_skip, rows, w_up, b_up.reshape(depth, ne, 1, f2), w_down,
      b_down.reshape(depth, ne, 1, d))


def _combine_kernel(y0_ref, y1_ref, y2_ref, y3_ref, gate_ref, x1_ref, g2_ref, fw_ref, *rest,
                    final):
    o_ref = rest[-1]
    gates = gate_ref[...]
    y_hi, y_lo = None, None
    for k, y_ref in enumerate((y0_ref, y1_ref, y2_ref, y3_ref)):
        hi, lo = _unpack_bf16_pairs(y_ref[...])
        g = gates[:, k:k + 1]
        y_hi = g * hi if y_hi is None else y_hi + g * hi
        y_lo = g * lo if y_lo is None else y_lo + g * lo
    y = jnp.concatenate([y_hi, y_lo], axis=1)
    x2 = x1_ref[...] + g2_ref[0] * y
    if final:
        x2 = (x2 * lax.rsqrt(jnp.mean(x2 * x2, axis=-1, keepdims=True) + NORM_EPS)) * fw_ref[...]
    o_ref[...] = x2


def _combine(y4, gates, x1, mod5, b0, layer, fw, seq, final, grp, n_out, prev_out, th=1024):
    t, d = x1.shape
    th = min(th, seq)
    steps = t // th
    per_b = seq // th
    slot = lambda k: pl.BlockSpec((th, d // 2), lambda i: (k * steps + i, 0))
    return pl.pallas_call(
        functools.partial(_combine_kernel, final=final),
        grid=(steps,),
        in_specs=[slot(0), slot(1), slot(2), slot(3),
                  pl.BlockSpec((th, TOP_K), lambda i: (i, 0)),
                  pl.BlockSpec((th, d), lambda i: (i, 0)),
                  _mod_spec(layer, MOD_G2, per_b, b0),
                  pl.BlockSpec((1, d), lambda i: (0, 0))]
                 + ([] if prev_out is None else [pl.BlockSpec(memory_space=pl.ANY)]),
        out_specs=pl.BlockSpec((None, th, d), lambda i: (grp, i, 0)),
        out_shape=jax.ShapeDtypeStruct((n_out, t, d), F32),
        input_output_aliases={} if prev_out is None else {8: 0},
        compiler_params=_cparams(("arbitrary",)),
        name="moe_combine",
    )(y4, y4, y4, y4, gates, x1, mod5, fw, *([] if prev_out is None else [prev_out]))


def _retention_tables(seq):
    f32 = np.float32
    pos = np.arange(seq, dtype=f32)
    inv_freq = (f32(ROPE_BASE) ** (-np.arange(0, HEAD_DK, 2, dtype=f32) / f32(HEAD_DK))).astype(f32)
    ang = pos[:, None] * inv_freq[None, :]
    cos, sin = np.cos(ang), np.sin(ang)
    zero = np.zeros_like(sin)
    zpad = np.zeros((seq, LANES - 2 * HEAD_DK), f32)
    cos_t = np.concatenate([cos, cos, cos, cos, zpad], axis=1)
    sina_t = np.concatenate([-sin, zero, -sin, zero, zpad], axis=1)
    sinb_t = np.concatenate([zero, sin, zero, sin, zpad], axis=1)
    log_gamma = np.log1p(-np.exp2(f32(-5.0) - np.arange(N_HEADS, dtype=f32))).astype(f32)
    log_decay = np.broadcast_to(log_gamma[None, :, None], (RET_CHUNK, N_HEADS, HEAD_DK))
    cum = np.cumsum(log_decay, axis=0, dtype=f32)
    tot = cum[-1:]

    def shp(a):
        flat = a.reshape(a.shape[0], N_HEADS * HEAD_DK)
        return np.where(_DK_SRC >= 0, flat[:, np.maximum(_DK_SRC, 0)], f32(1.0)).astype(f32)

    tabs = (cos_t, sina_t, sinb_t, shp(np.exp(cum)), shp(np.exp(-cum)), shp(np.exp(tot - cum)),
            shp(np.exp(tot)))
    return tuple(jnp.asarray(a, F32) for a in tabs)


def kernel(x, c, norm1_w, norm2_w, w_mod, b_mod, w_in, gla_w_a2, gla_b_a, gla_norm_w, ret_norm_w, s5_a_re, s5_a_im, s5_log_dt, s5_b_re, s5_b_im, s5_c_re, s5_c_im, s5_d, s5_w_glu, s5_b_glu, w_out, router_w, router_b, w_up, b_up, w_down, b_down, final_norm_w):
    batch, seq, d = x.shape
    depth = w_mod.shape[0]
    gb = batch // N_STREAMS
    t = gb * seq
    n_slots = t * TOP_K
    n_blocks = n_slots // ROW_BLK + N_EXPERTS
    n_rows = n_blocks * ROW_BLK

    mod = _modulation(c, w_mod, b_mod)
    mod5 = mod.reshape(depth, batch, 6, 1, d).transpose(0, 2, 1, 3, 4)
    ret_tabs = _retention_tables(seq)

    w_p = _take_cols(w_in, _IN_SRC).astype(BF16)
    wa_p = jnp.zeros((depth, LANES, QKP), F32).at[:, :GATE_RANK].set(
        _take_cols(gla_w_a2, _DK_SRC)).astype(BF16)
    ba_p = _take_cols(gla_b_a, _DK_SRC).reshape(depth, 1, QKP)
    gnw = _take_cols(gla_norm_w, _DV_SRC).reshape(depth, 1, HP)
    rnw = _take_cols(ret_norm_w, _DV_SRC).reshape(depth, 1, HP)
    kv = N_HEADS * HEAD_DV
    wo_g = _take_rows(w_out[:, :kv], _DV_SRC).astype(BF16)
    wo_r = _take_rows(w_out[:, kv:2 * kv], _DV_SRC).astype(BF16)
    wo_s = w_out[:, 2 * kv:].astype(BF16)
    rw_t = jnp.swapaxes(router_w, 1, 2)
    rw_hi = rw_t.astype(BF16)
    rw_p = jnp.concatenate([rw_hi, (rw_t - rw_hi.astype(F32)).astype(BF16)], axis=1)
    rb_p = jnp.broadcast_to(router_b[:, :, None], (depth, N_EXPERTS, LANES))
    s5_tabs = jax.vmap(_s5_tables)(s5_a_re, s5_a_im, s5_log_dt, s5_b_re, s5_b_im, s5_c_re,
                                   s5_c_im, s5_d, s5_w_glu, s5_b_glu)
    n1 = norm1_w.reshape(depth, 1, d)
    n2 = norm2_w.reshape(depth, 1, d)
    fw = final_norm_w.reshape(1, d)

    xs = [(x.reshape(N_STREAMS, t, d), g) for g in range(N_STREAMS)]
    out = None

    for i in range(depth):
        final = i == depth - 1
        for g in range(N_STREAMS):
            x3, grp = xs[g]
            b0 = g * gb

            proj = _in_projection(x3, grp, mod5, b0, i, n1, w_p, seq)
            o_gla = _gla_mixer(proj, gb, seq, i, wa_p, ba_p, gnw)
            o_ret = _ret_mixer(proj, gb, seq, i, *ret_tabs, rnw)
            o_s5 = _s5_mixer(proj, gb, seq, i, s5_tabs)

            x1, hdn, idx, gates, rank, counts = _outproj_router(
                o_gla, o_ret, o_s5, wo_g, wo_r, wo_s, x3, grp, mod5, b0, i, n2, rw_p, rb_p, seq)

            cnt = counts[:, 0].astype(jnp.int32)
            padded = (cnt + ROW_BLK - 1) // ROW_BLK * ROW_BLK
            pad_ends = jnp.cumsum(padded)
            first_row = pad_ends - cnt
            slot_start = jnp.sum(jnp.where(idx[:TOP_K, :, None] == jnp.arange(N_EXPERTS),
                                           first_row, 0), axis=-1)
            dest_sm = (slot_start + rank[:TOP_K]).astype(jnp.int32).reshape(-1)
            gates_tm = gates[:TOP_K].T
            blk_start = jnp.arange(n_blocks, dtype=jnp.int32) * ROW_BLK
            block_e = jnp.minimum(jnp.sum(pad_ends[None, :] <= blk_start[:, None], axis=1),
                                  N_EXPERTS - 1).astype(jnp.int32)
            n_skip = jnp.where(blk_start < pad_ends[-1],
                               jnp.clip(first_row[block_e] - blk_start, 0, ROW_BLK),
                               ROW_BLK).astype(jnp.int32)

            rows = _scatter_rows(hdn, dest_sm, n_rows)
            out_rows = _experts(i, block_e, n_skip, rows, w_up, b_up, w_down, b_down)
            y4 = _gather_rows(out_rows, dest_sm)
            if final:
                out = _combine(y4, gates_tm, x1, mod5, b0, i, fw, seq, True, g, N_STREAMS, out)
            else:
                xs[g] = (_combine(y4, gates_tm, x1, mod5, b0, i, fw, seq, False, 0, 1, None), 0)

    return out.reshape(batch, seq, d)
```

```python
import functools

import numpy as np
import jax
import jax.numpy as jnp
from jax import lax
from jax.experimental import pallas as pl
from jax.experimental.pallas import tpu as pltpu
from jax.experimental.pallas import tpu_sc as plsc

D_MODEL = 1024
CHUNK = 64
RET_CHUNK = 128
NORM_EPS = 1e-5
N_HEADS = 4
HEAD_DK = 48
HEAD_DV = 96
GATE_RANK = 16
GATE_TEMP = 16.0
ROPE_BASE = 10000.0
S5_WIDTH = 256
S5_GROUP_DIM = 16
S5_GROUPS = 16
S5_STATE = 64
N_EXPERTS = 32
TOP_K = 4
D_FF = 1024
SWIGLU_LIMIT = 7.0
SWIGLU_ALPHA = 1.702

LANES = 128
HEAD_PAD = LANES
HP = N_HEADS * HEAD_PAD
N_PAIRS = N_HEADS // 2
QKP = N_PAIRS * LANES
VMEM_LIMIT = 56 * 1024 * 1024

OFF_GQ, OFF_GK, OFF_GV, OFF_GG = 0, QKP, 2 * QKP, 2 * QKP + HP
OFF_RQ = OFF_GG + HP
OFF_RK, OFF_RV, OFF_RG = OFF_RQ + QKP, OFF_RQ + 2 * QKP, OFF_RQ + 2 * QKP + HP
OFF_SU = OFF_RG + HP
OFF_GA = OFF_SU + S5_WIDTH
NP_COLS = OFF_GA + LANES
PROJ_CHUNKS = (1280, 1280, 896)

ROW_BLK = 1024
ROW_SUB = 512
MOD_SH1, MOD_SC1, MOD_G1, MOD_SH2, MOD_SC2, MOD_G2 = range(6)
N_STREAMS = 1

F32 = jnp.float32
BF16 = jnp.bfloat16


def _DK_SRC_LANE(h, d):
    return (h // 2) * LANES + (h % 2) * HEAD_DK + d


def _in_col_map():
    src = -np.ones((NP_COLS,), np.int64)
    kq = N_HEADS * HEAD_DK
    kv = N_HEADS * HEAD_DV
    base = dict(gq=0, gk=kq, gv=2 * kq, gg=2 * kq + kv, ga=2 * kq + 2 * kv)
    r0 = base['ga'] + GATE_RANK
    base.update(rq=r0, rk=r0 + kq, rv=r0 + 2 * kq, rg=r0 + 2 * kq + kv, su=r0 + 2 * kq + 2 * kv)
    for h in range(N_HEADS):
        for d in range(HEAD_DK):
            lane = _DK_SRC_LANE(h, d)
            src[OFF_GQ + lane] = base['gq'] + h * HEAD_DK + d
            src[OFF_GK + lane] = base['gk'] + h * HEAD_DK + d
            src[OFF_RQ + lane] = base['rq'] + h * HEAD_DK + d
            src[OFF_RK + lane] = base['rk'] + h * HEAD_DK + d
        for d in range(HEAD_DV):
            src[OFF_GV + h * HEAD_PAD + d] = base['gv'] + h * HEAD_DV + d
            src[OFF_GG + h * HEAD_PAD + d] = base['gg'] + h * HEAD_DV + d
            src[OFF_RV + h * HEAD_PAD + d] = base['rv'] + h * HEAD_DV + d
            src[OFF_RG + h * HEAD_PAD + d] = base['rg'] + h * HEAD_DV + d
    src[OFF_SU:OFF_SU + S5_WIDTH] = base['su'] + np.arange(S5_WIDTH)
    src[OFF_GA:OFF_GA + GATE_RANK] = base['ga'] + np.arange(GATE_RANK)
    return src


_IN_SRC = _in_col_map()


def _head_pad_map(width):
    src = -np.ones((HP,), np.int64)
    for h in range(N_HEADS):
        src[h * HEAD_PAD:h * HEAD_PAD + width] = h * width + np.arange(width)
    return src


_DV_SRC = _head_pad_map(HEAD_DV)
_DK_SRC = -np.ones((QKP,), np.int64)
for _h in range(N_HEADS):
    for _d in range(HEAD_DK):
        _DK_SRC[_DK_SRC_LANE(_h, _d)] = _h * HEAD_DK + _d


def _take_static(w, src, axis):
    axis = axis % w.ndim
    pieces, start = [], 0
    for j in range(1, len(src) + 1):
        run_ends = (j == len(src) or (src[j] < 0) != (src[start] < 0)
                    or (src[start] >= 0 and src[j] != src[j - 1] + 1))
        if run_ends:
            if src[start] < 0:
                shape = w.shape[:axis] + (j - start,) + w.shape[axis + 1:]
                pieces.append(jnp.zeros(shape, w.dtype))
            else:
                pieces.append(lax.slice_in_dim(w, int(src[start]), int(src[j - 1]) + 1, axis=axis))
            start = j
    return jnp.concatenate(pieces, axis=axis)


def _take_cols(w, src):
    return _take_static(w, src, -1)


def _take_rows(w, src):
    return _take_static(w, src, -2)


def _layer_spec(shape, layer):
    return pl.BlockSpec((None,) + tuple(shape), lambda *_: (layer,) + (0,) * len(shape))


def _mod_spec(layer, which, per_b, b0):
    return pl.BlockSpec((None, None, 1, 1, D_MODEL),
                        lambda i: (layer, which, b0 + i // per_b, 0, 0))


def _cparams(sem):
    return pltpu.CompilerParams(dimension_semantics=sem, vmem_limit_bytes=VMEM_LIMIT)


def _mod_kernel(c_ref, w_ref, b_ref, o_ref):
    c = c_ref[...]
    cond = c * jax.nn.sigmoid(c)
    o_ref[0] = jnp.dot(cond, w_ref[0], preferred_element_type=F32,
                       precision=lax.Precision.HIGHEST) + b_ref[0]


def _modulation(c, w_mod, b_mod):
    depth, d, n = w_mod.shape
    b = c.shape[0]
    nb = 1536
    return pl.pallas_call(
        _mod_kernel,
        grid=(depth, n // nb),
        in_specs=[pl.BlockSpec((b, d), lambda l, j: (0, 0)),
                  pl.BlockSpec((1, d, nb), lambda l, j: (l, 0, j)),
                  pl.BlockSpec((1, 1, nb), lambda l, j: (l, 0, j))],
        out_specs=pl.BlockSpec((1, b, nb), lambda l, j: (l, 0, j)),
        out_shape=jax.ShapeDtypeStruct((depth, b, n), F32),
        compiler_params=_cparams(("arbitrary", "arbitrary")),
        name="adaln_mod",
    )(c, w_mod, b_mod.reshape(depth, 1, n))


def _rms_mod(x, nw, sc, sh):
    y = x * lax.rsqrt(jnp.mean(x * x, axis=-1, keepdims=True) + NORM_EPS)
    return (y * nw) * (1.0 + sc) + sh


def _inproj_kernel(x_ref, sc_ref, sh_ref, nw_ref, w_ref, o_ref):
    h = _rms_mod(x_ref[...], nw_ref[...], sc_ref[0], sh_ref[0]).astype(BF16)
    assert sum(PROJ_CHUNKS) == NP_COLS
    for j, width in enumerate(PROJ_CHUNKS):
        cs = slice(sum(PROJ_CHUNKS[:j]), sum(PROJ_CHUNKS[:j]) + width)
        o_ref[:, cs] = jnp.dot(h, w_ref[:, cs], preferred_element_type=F32).astype(BF16)


def _in_projection(x3, grp, mod5, b0, layer, nw, w_p, seq, tm=1024):
    _, t, d = x3.shape
    tm = min(tm, seq)
    per_b = seq // tm
    return pl.pallas_call(
        _inproj_kernel,
        grid=(t // tm,),
        in_specs=[pl.BlockSpec((None, tm, d), lambda i: (grp, i, 0)),
                  _mod_spec(layer, MOD_SC1, per_b, b0), _mod_spec(layer, MOD_SH1, per_b, b0),
                  _layer_spec((1, d), layer), _layer_spec((d, NP_COLS), layer)],
        out_specs=pl.BlockSpec((tm, NP_COLS), lambda i: (i, 0)),
        out_shape=jax.ShapeDtypeStruct((t, NP_COLS), BF16),
        compiler_params=_cparams(("arbitrary",)),
        name="in_proj",
    )(x3, mod5, mod5, nw, w_p)


_NT = (((1,), (1,)), ((), ()))
_TN = (((0,), (0,)), ((), ()))


def _tri_mask(n=CHUNK):
    r = lax.broadcasted_iota(jnp.int32, (n, n), 0)
    c = lax.broadcasted_iota(jnp.int32, (n, n), 1)
    return r >= c


def _pair_masks(rows):
    lane = lax.broadcasted_iota(jnp.int32, (rows, LANES), 1)
    return lane < HEAD_DK, (lane >= HEAD_DK) & (lane < 2 * HEAD_DK)


def _head_attention(qd, ki, ke, vh, et, st_ref, h, causal):
    qb = qd.astype(BF16)
    sc = lax.dot_general(qb, ki.astype(BF16), _NT, preferred_element_type=F32)
    sc = jnp.where(causal, sc, 0.0)
    st = st_ref[h]
    o = jnp.dot(sc.astype(BF16), vh, preferred_element_type=F32)
    o = o + lax.dot_general(qb, st.astype(BF16), _NT, preferred_element_type=F32)
    st_ref[h] = st * et + lax.dot_general(vh, ke.astype(BF16), _TN, preferred_element_type=F32)
    return o


def _gla_kernel(q_ref, k_ref, v_ref, g_ref, a_ref, wa_ref, ba_ref, nw_ref, tri_ref, o_ref,
                st_ref, qd_s, ki_s, ke_s, et_s):
    @pl.when(pl.program_id(1) == 0)
    def _():
        st_ref[...] = jnp.zeros_like(st_ref)

    causal = _tri_mask()
    tl = q_ref.shape[0]
    n_chunks = tl // CHUNK

    z = jnp.dot(a_ref[...], wa_ref[...], preferred_element_type=F32) + ba_ref[...]
    la = (jnp.minimum(z, 0.0) - jnp.log1p(jnp.exp(-jnp.abs(z)))) * (1.0 / GATE_TEMP)
    hi = la.astype(BF16)
    lo = (la - hi.astype(F32)).astype(BF16)
    cum = (jnp.dot(tri_ref[...], hi, preferred_element_type=F32)
           + jnp.dot(tri_ref[...], lo, preferred_element_type=F32))
    cum3 = cum.reshape(n_chunks, CHUNK, QKP)
    tot3 = cum3[:, CHUNK - 1:CHUNK, :]
    qd = (q_ref[...].astype(F32) * (HEAD_DK ** -0.5)) * jnp.exp(cum)
    masks = _pair_masks(tl)
    for h in range(N_HEADS):
        pair = slice((h // 2) * LANES, (h // 2 + 1) * LANES)
        qd_s[:, h * HEAD_PAD:(h + 1) * HEAD_PAD] = jnp.where(masks[h % 2], qd[:, pair],
                                                              0.0).astype(BF16)
    kf = k_ref[...].astype(F32)
    ki_s[...] = (kf * jnp.exp(-cum)).astype(BF16)
    ke_s[...] = (kf * jnp.exp(tot3 - cum3).reshape(tl, QKP)).astype(BF16)
    et_s[...] = jnp.exp(tot3).reshape(n_chunks, QKP)

    def chunk(c, carry):
        r = pl.ds(pl.multiple_of(c * CHUNK, CHUNK), CHUNK)
        et = et_s[pl.ds(c, 1), :]
        for h in range(N_HEADS):
            sl = slice(h * HEAD_PAD, (h + 1) * HEAD_PAD)
            pair = slice((h // 2) * LANES, (h // 2 + 1) * LANES)
            o = _head_attention(qd_s[r, sl], ki_s[r, pair], ke_s[r, pair], v_ref[r, sl],
                                et[:, pair], st_ref, h, causal)
            ms = jnp.sum(o * o, axis=-1, keepdims=True) * (1.0 / HEAD_DV)
            y = (o * lax.rsqrt(ms + NORM_EPS)) * nw_ref[:, sl]
            g = g_ref[r, sl].astype(F32)
            o_ref[r, sl] = (y * (g * jax.nn.sigmoid(g))).astype(BF16)
        return carry

    lax.fori_loop(0, n_chunks, chunk, 0, unroll=4)


def _ret_kernel(q_ref, k_ref, v_ref, g_ref, cos_ref, sina_ref, sinb_ref, dq_ref, dki_ref,
                dke_ref, dt_ref, nw_ref, o_ref, st_ref):
    @pl.when(pl.program_id(1) == 0)
    def _():
        st_ref[...] = jnp.zeros_like(st_ref)

    causal = _tri_mask(RET_CHUNK)
    n_chunks = q_ref.shape[0] // RET_CHUNK
    lane = lax.broadcasted_iota(jnp.int32, (RET_CHUNK, HEAD_PAD), 1)
    real = lane < HEAD_DV
    masks = _pair_masks(RET_CHUNK)
    half = HEAD_DK // 2

    def rotary(t, cos, sina, sinb):
        return (t * cos + pltpu.roll(t, LANES - half, 1) * sina + pltpu.roll(t, half, 1) * sinb)

    def chunk(c, carry):
        r = pl.ds(pl.multiple_of(c * RET_CHUNK, RET_CHUNK), RET_CHUNK)
        cos, sina, sinb = cos_ref[r, :], sina_ref[r, :], sinb_ref[r, :]
        pair_q, pair_ki, pair_ke = [], [], []
        for p in range(N_PAIRS):
            ps = slice(p * LANES, (p + 1) * LANES)
            qr = rotary(q_ref[r, ps].astype(F32), cos, sina, sinb) * dq_ref[:, ps]
            kr = rotary(k_ref[r, ps].astype(F32), cos, sina, sinb) * (HEAD_DK ** -0.5)
            pair_q.append(qr)
            pair_ki.append((kr * dki_ref[:, ps]).astype(BF16))
            pair_ke.append((kr * dke_ref[:, ps]).astype(BF16))
        for h in range(N_HEADS):
            sl = slice(h * HEAD_PAD, (h + 1) * HEAD_PAD)
            p = h // 2
            qd = jnp.where(masks[h % 2], pair_q[p], 0.0)
            o = _head_attention(qd, pair_ki[p], pair_ke[p], v_ref[r, sl],
                                dt_ref[:, p * LANES:(p + 1) * LANES], st_ref, h, causal)
            mu = jnp.sum(o, axis=-1, keepdims=True) * (1.0 / HEAD_DV)
            oc = jnp.where(real, o - mu, 0.0)
            var = jnp.sum(oc * oc, axis=-1, keepdims=True) * (1.0 / HEAD_DV)
            y = (oc * lax.rsqrt(var + NORM_EPS)) * nw_ref[:, sl]
            g = g_ref[r, sl].astype(F32)
            o_ref[r, sl] = (y * (g * jax.nn.sigmoid(g))).astype(BF16)
        return carry

    lax.fori_loop(0, n_chunks, chunk, 0, unroll=2)


def _proj_spec(tl, width, col_off, per_b):
    cb = col_off // width
    return pl.BlockSpec((tl, width), lambda b, l: (b * per_b + l, cb))


def _full(shape):
    return pl.BlockSpec(shape, lambda b, l: (0,) * len(shape))


def _gla_mixer(proj, batch, seq, layer, wa_p, ba_p, nw_p, tl=512):
    per_b = seq // tl
    pos = np.arange(tl)
    tri_bd = jnp.asarray((pos[:, None] // CHUNK == pos[None, :] // CHUNK)
                         & (pos[:, None] >= pos[None, :]), BF16)
    return pl.pallas_call(
        _gla_kernel,
        grid=(batch, per_b),
        in_specs=[_proj_spec(tl, QKP, OFF_GQ, per_b), _proj_spec(tl, QKP, OFF_GK, per_b),
                  _proj_spec(tl, HP, OFF_GV, per_b), _proj_spec(tl, HP, OFF_GG, per_b),
                  _proj_spec(tl, LANES, OFF_GA, per_b),
                  _layer_spec((LANES, QKP), layer), _layer_spec((1, QKP), layer),
                  _layer_spec((1, HP), layer), _full((tl, tl))],
        out_specs=pl.BlockSpec((tl, HP), lambda b, l: (b * per_b + l, 0)),
        out_shape=jax.ShapeDtypeStruct((batch * seq, HP), BF16),
        scratch_shapes=[pltpu.VMEM((N_HEADS, HEAD_PAD, HEAD_PAD), F32),
                        pltpu.VMEM((tl, HP), BF16), pltpu.VMEM((tl, QKP), BF16),
                        pltpu.VMEM((tl, QKP), BF16), pltpu.VMEM((tl // CHUNK, QKP), F32)],
        compiler_params=_cparams(("arbitrary", "arbitrary")),
        name="gla_mixer",
    )(proj, proj, proj, proj, proj, wa_p, ba_p, nw_p, tri_bd)


def _ret_mixer(proj, batch, seq, layer, cos_t, sina_t, sinb_t, dq, dki, dke, dtot, nw_p, tl=512):
    per_b = seq // tl
    return pl.pallas_call(
        _ret_kernel,
        grid=(batch, per_b),
        in_specs=[_proj_spec(tl, QKP, OFF_RQ, per_b), _proj_spec(tl, QKP, OFF_RK, per_b),
                  _proj_spec(tl, HP, OFF_RV, per_b), _proj_spec(tl, HP, OFF_RG, per_b),
                  pl.BlockSpec((tl, LANES), lambda b, l: (l, 0)),
                  pl.BlockSpec((tl, LANES), lambda b, l: (l, 0)),
                  pl.BlockSpec((tl, LANES), lambda b, l: (l, 0)),
                  _full((RET_CHUNK, QKP)), _full((RET_CHUNK, QKP)), _full((RET_CHUNK, QKP)),
                  _full((1, QKP)), _layer_spec((1, HP), layer)],
        out_specs=pl.BlockSpec((tl, HP), lambda b, l: (b * per_b + l, 0)),
        out_shape=jax.ShapeDtypeStruct((batch * seq, HP), BF16),
        scratch_shapes=[pltpu.VMEM((N_HEADS, HEAD_PAD, HEAD_PAD), F32)],
        compiler_params=_cparams(("arbitrary", "arbitrary")),
        name="ret_mixer",
    )(proj, proj, proj, proj, cos_t, sina_t, sinb_t, dq, dki, dke, dtot, nw_p)


S5_N = S5_GROUPS * S5_STATE
S5_SLAB = 256


def _gelu_tanh(x):
    return 0.5 * x * (1.0 + jnp.tanh(np.sqrt(2.0 / np.pi) * (x + 0.044715 * (x * x * x))))


def _s5_kernel(u_ref, bb_ref, cb_ref, nr_ref, ni_ref, pr_ref, pi_ref, lr_ref, li_ref,
               d_ref, wg_ref, bg_ref, o_ref, sr_ref, si_ref, x_scr, s_scr):
    @pl.when(pl.program_id(1) == 0)
    def _():
        sr_ref[...] = jnp.zeros_like(sr_ref)
        si_ref[...] = jnp.zeros_like(si_ref)

    tri = _tri_mask().astype(BF16)
    n_chunks = u_ref.shape[0] // CHUNK
    u = u_ref[...]
    x_scr[...] = jnp.dot(u, bb_ref[...], preferred_element_type=F32)

    def chunk(c, carry):
        r = pl.ds(pl.multiple_of(c * CHUNK, CHUNK), CHUNK)
        for j in range(S5_N // S5_SLAB):
            cs = slice(j * S5_SLAB, (j + 1) * S5_SLAB)
            ci = slice(S5_N + j * S5_SLAB, S5_N + (j + 1) * S5_SLAB)
            xr, xi = x_scr[r, cs], x_scr[r, ci]
            nr, ni = nr_ref[:, cs], ni_ref[:, cs]
            p_r = jnp.dot(tri, (xr * nr - xi * ni).astype(BF16), preferred_element_type=F32)
            p_i = jnp.dot(tri, (xr * ni + xi * nr).astype(BF16), preferred_element_type=F32)
            s0r, s0i = sr_ref[:, cs], si_ref[:, cs]
            lr, li = lr_ref[:, cs], li_ref[:, cs]
            q_r = p_r + (s0r * lr - s0i * li)
            q_i = p_i + (s0r * li + s0i * lr)
            pr, pi = pr_ref[:, cs], pi_ref[:, cs]
            s_r = q_r * pr - q_i * pi
            s_i = q_r * pi + q_i * pr
            sr_ref[:, cs] = s_r[CHUNK - 1:CHUNK, :]
            si_ref[:, cs] = s_i[CHUNK - 1:CHUNK, :]
            s_scr[r, cs] = s_r.astype(BF16)
            s_scr[r, ci] = s_i.astype(BF16)
        return carry

    lax.fori_loop(0, n_chunks, chunk, 0, unroll=2)
    y = jnp.dot(s_scr[...], cb_ref[...], preferred_element_type=F32)
    y = _gelu_tanh(y + d_ref[...] * u.astype(F32))
    gate = jnp.dot(y.astype(BF16), wg_ref[...], preferred_element_type=F32) + bg_ref[...]
    o_ref[...] = (y * jax.nn.sigmoid(gate)).astype(BF16)


def _s5_mixer(proj, batch, seq, layer, tabs, tl=512):
    per_b = seq // tl
    bb, cb, nr, ni, pr, pi, lr, li, dsk, wg, bg = tabs
    return pl.pallas_call(
        _s5_kernel,
        grid=(batch, per_b),
        in_specs=[_proj_spec(tl, S5_WIDTH, OFF_SU, per_b),
                  *[_layer_spec(a.shape[1:], layer) for a in tabs]],
        out_specs=pl.BlockSpec((tl, S5_WIDTH), lambda b, l: (b * per_b + l, 0)),
        out_shape=jax.ShapeDtypeStruct((batch * seq, S5_WIDTH), BF16),
        scratch_shapes=[pltpu.VMEM((1, S5_N), F32), pltpu.VMEM((1, S5_N), F32),
                        pltpu.VMEM((tl, 2 * S5_N), F32), pltpu.VMEM((tl, 2 * S5_N), BF16)],
        compiler_params=_cparams(("arbitrary", "arbitrary")),
        name="s5_mixer",
    )(proj, bb, cb, nr, ni, pr, pi, lr, li, dsk, wg, bg)


def _s5_tables(a_re, a_im, log_dt, b_re, b_im, c_re, c_im, d_skip, w_glu, b_glu):
    lam = lax.complex(a_re, a_im)
    dt = jnp.exp(log_dt)[:, None]
    lam_bar = jnp.exp(lam * dt)
    b_bar = ((lam_bar - 1.0) / lam)[..., None] * lax.complex(b_re, b_im)
    eye = jnp.eye(S5_GROUPS, dtype=F32)
    def blk_b(m):
        return jnp.einsum('gph,gk->ghkp', m, eye).reshape(S5_WIDTH, S5_N)
    bb = jnp.concatenate([blk_b(jnp.real(b_bar)), blk_b(jnp.imag(b_bar))], axis=1)
    def blk_c(m):
        return jnp.einsum('ghp,gk->kpgh', m, eye).reshape(S5_N, S5_WIDTH)
    cb = jnp.concatenate([blk_c(c_re), blk_c(-c_im)], axis=0)
    steps = jnp.arange(CHUNK, dtype=F32)[:, None, None]
    lam_dt = (lam * dt)[None]
    pos = jnp.exp(lam_dt * steps).reshape(CHUNK, S5_N)
    neg = jnp.exp(-lam_dt * steps).reshape(CHUNK, S5_N)
    one = lam_bar.reshape(1, S5_N)
    return (bb.astype(BF16), cb.astype(BF16), jnp.real(neg), jnp.imag(neg), jnp.real(pos),
            jnp.imag(pos), jnp.real(one), jnp.imag(one), d_skip.reshape(1, S5_WIDTH),
            w_glu.astype(BF16), b_glu.reshape(1, S5_WIDTH))


def _pack_bf16_pairs(x):
    w = x.shape[1] // 2
    xb = x.astype(BF16).astype(F32)
    hi = lax.bitcast_convert_type(xb[:, :w], jnp.uint32)
    lo = lax.bitcast_convert_type(xb[:, w:], jnp.uint32)
    return hi | (lo >> 16)


def _unpack_bf16_pairs(p):
    hi = lax.bitcast_convert_type(p & jnp.uint32(0xFFFF0000), F32)
    lo = lax.bitcast_convert_type(p << 16, F32)
    return hi, lo


def _router_kernel(og_ref, or_ref, os_ref, wg_ref, wr_ref, ws_ref, x_ref, g1_ref, sc_ref,
                   sh_ref, nw_ref, rw_ref, rb_ref,
                   x1_ref, h_ref, idx_ref, gate_ref, rank_ref, cnt_ref, carry_ref):
    i = pl.program_id(0)

    @pl.when(i == 0)
    def _():
        carry_ref[...] = jnp.zeros_like(carry_ref)

    mix = (jnp.dot(og_ref[...], wg_ref[...], preferred_element_type=F32)
           + jnp.dot(or_ref[...], wr_ref[...], preferred_element_type=F32)
           + jnp.dot(os_ref[...], ws_ref[...], preferred_element_type=F32))
    x1 = x_ref[...] + g1_ref[0] * mix
    x1_ref[...] = x1
    hdn = _rms_mod(x1, nw_ref[...], sc_ref[0], sh_ref[0])
    h_ref[...] = _pack_bf16_pairs(hdn)
    h_hi = hdn.astype(BF16)
    h_lo = (hdn - h_hi.astype(F32)).astype(BF16)
    p = lax.dot_general(rw_ref[...], h_hi, _NT, preferred_element_type=F32)
    q = lax.dot_general(rw_ref[:N_EXPERTS, :], h_lo, _NT, preferred_element_type=F32)
    logits = (p[:N_EXPERTS] + p[N_EXPERTS:] + q) + rb_ref[:, 0:1]
    tm = logits.shape[1]
    eidx = lax.broadcasted_iota(jnp.int32, (N_EXPERTS, tm), 0)
    work = logits
    onehot = jnp.zeros((N_EXPERTS, tm), F32)
    vals, idxs, sels = [], [], []
    for _ in range(TOP_K):
        m = jnp.max(work, axis=0, keepdims=True)
        ix = jnp.min(jnp.where(work == m, eidx, N_EXPERTS), axis=0, keepdims=True)
        sel = eidx == ix
        work = jnp.where(sel, -jnp.inf, work)
        onehot = onehot + sel.astype(F32)
        vals.append(m)
        idxs.append(ix)
        sels.append(sel)
    exps = [jnp.exp(v - vals[0]) for v in vals]
    denom = exps[0] + exps[1] + exps[2] + exps[3]
    r = lax.broadcasted_iota(jnp.int32, (tm, tm), 0)
    c = lax.broadcasted_iota(jnp.int32, (tm, tm), 1)
    earlier = (r < c).astype(BF16)
    before = (jnp.dot(onehot.astype(BF16), earlier, preferred_element_type=F32)
              + carry_ref[:, 0:1])
    row8 = lax.broadcasted_iota(jnp.int32, (8, tm), 0)
    idx_out = jnp.zeros((8, tm), jnp.int32)
    gate_out = jnp.zeros((8, tm), F32)
    rank_out = jnp.zeros((8, tm), F32)
    for k in range(TOP_K):
        rk = jnp.sum(jnp.where(sels[k], before, 0.0), axis=0, keepdims=True)
        idx_out = jnp.where(row8 == k, idxs[k], idx_out)
        gate_out = jnp.where(row8 == k, exps[k] / denom, gate_out)
        rank_out = jnp.where(row8 == k, rk, rank_out)
    idx_ref[...] = idx_out
    gate_ref[...] = gate_out
    rank_ref[...] = rank_out.astype(jnp.int32)
    total = carry_ref[...] + jnp.sum(onehot, axis=1, keepdims=True)
    carry_ref[...] = total
    cnt_ref[...] = total


def _outproj_router(o_gla, o_ret, o_s5, wg, wr, ws, x3, grp, mod5, b0, layer, nw2, rw_p, rb_p,
                    seq, tm=1024):
    _, t, d = x3.shape
    tm = min(tm, seq)
    per_b = seq // tm
    row = lambda w: pl.BlockSpec((tm, w), lambda i: (i, 0))
    full = lambda s: pl.BlockSpec(s, lambda i: (0,) * len(s))
    slot_t = pl.BlockSpec((8, tm), lambda i: (0, i))
    return pl.pallas_call(
        _router_kernel,
        grid=(t // tm,),
        in_specs=[row(HP), row(HP), row(S5_WIDTH), _layer_spec((HP, d), layer),
                  _layer_spec((HP, d), layer), _layer_spec((S5_WIDTH, d), layer),
                  pl.BlockSpec((None, tm, d), lambda i: (grp, i, 0)),
                  _mod_spec(layer, MOD_G1, per_b, b0), _mod_spec(layer, MOD_SC2, per_b, b0),
                  _mod_spec(layer, MOD_SH2, per_b, b0), _layer_spec((1, d), layer),
                  _layer_spec((2 * N_EXPERTS, d), layer), _layer_spec((N_EXPERTS, LANES), layer)],
        out_specs=[row(d), row(d // 2), slot_t, slot_t, slot_t, full((N_EXPERTS, LANES))],
        out_shape=[jax.ShapeDtypeStruct((t, d), F32),
                   jax.ShapeDtypeStruct((t, d // 2), jnp.uint32),
                   jax.ShapeDtypeStruct((8, t), jnp.int32),
                   jax.ShapeDtypeStruct((8, t), F32),
                   jax.ShapeDtypeStruct((8, t), jnp.int32),
                   jax.ShapeDtypeStruct((N_EXPERTS, LANES), F32)],
        scratch_shapes=[pltpu.VMEM((N_EXPERTS, LANES), F32)],
        compiler_params=_cparams(("arbitrary",)),
        name="outproj_router",
    )(o_gla, o_ret, o_s5, wg, wr, ws, x3, mod5, mod5, mod5, nw2, rw_p, rb_p)


GATHER_WIN = 64


def _gather_rows(table, idx):
    m = idx.shape[0]
    w = table.shape[1]
    mesh = plsc.VectorSubcoreMesh(core_axis_name="core", subcore_axis_name="subcore")

    @functools.partial(pl.kernel, out_type=jax.ShapeDtypeStruct((m, w), table.dtype),
                       mesh=mesh, name="sc_row_gather")
    def gather(x_hbm, i_hbm, o_hbm):
        def body(i_vmem, o_vmem):
            pltpu.sync_copy(x_hbm.at[i_vmem], o_vmem)

        pltpu.emit_pipeline(
            body,
            grid=(m // GATHER_WIN,),
            in_specs=[pl.BlockSpec((GATHER_WIN,), lambda i: (i,))],
            out_specs=[pl.BlockSpec((GATHER_WIN, w), lambda i: (i, 0))],
            core_axis_name=("core", "subcore"),
            dimension_semantics=(pltpu.PARALLEL,),
        )(i_hbm, o_hbm)

    return gather(table, idx)


def _scatter_rows(x, dest_slot_major, n_rows):
    t, w = x.shape
    steps = t // GATHER_WIN
    mesh = plsc.VectorSubcoreMesh(core_axis_name="core", subcore_axis_name="subcore")

    @functools.partial(pl.kernel, out_type=jax.ShapeDtypeStruct((n_rows, w), x.dtype),
                       mesh=mesh, name="sc_row_scatter")
    def scatter(x_hbm, i_hbm, o_hbm):
        def body(x_vmem, i0, i1, i2, i3):
            for i_vmem in (i0, i1, i2, i3):
                pltpu.sync_copy(x_vmem, o_hbm.at[i_vmem])

        slot = lambda k: pl.BlockSpec((GATHER_WIN,), lambda i: (k * steps + i,))
        pltpu.emit_pipeline(
            body,
            grid=(steps,),
            in_specs=[pl.BlockSpec((GATHER_WIN, w), lambda i: (i, 0)),
                      slot(0), slot(1), slot(2), slot(3)],
            out_specs=[],
            core_axis_name=("core", "subcore"),
            dimension_semantics=(pltpu.PARALLEL,),
        )(x_hbm, i_hbm, i_hbm, i_hbm, i_hbm)

    return scatter(x, dest_slot_major)


def _expert_kernel(be_ref, ns_ref, rows_ref, wu_ref, bu_ref, wd_ref, bd_ref, o_ref,
                   wu_bf, wd_bf):
    j = pl.program_id(0)
    n_blocks = be_ref.shape[0]
    i = jnp.maximum(j - 1, 0)
    n_skip = jnp.where(j > 0, ns_ref[i], ROW_BLK)

    def run_rows(r0, n, first):
        rs = slice(r0, r0 + n)
        row = lax.broadcasted_iota(jnp.int32, (n, rows_ref.shape[1]), 0)
        x_hi, x_lo = _unpack_bf16_pairs(jnp.where(row >= first, rows_ref[rs, :], jnp.uint32(0)))
        x = jnp.concatenate([x_hi.astype(BF16), x_lo.astype(BF16)], axis=1)
        up = jnp.dot(x, wu_bf[...], preferred_element_type=F32) + bu_ref[0, 0]
        x_glu = jnp.minimum(up[:, :D_FF], SWIGLU_LIMIT)
        x_lin = jnp.clip(up[:, D_FF:], -SWIGLU_LIMIT, SWIGLU_LIMIT)
        act = x_glu * jax.nn.sigmoid(SWIGLU_ALPHA * x_glu) * (x_lin + 1.0)
        o_ref[rs, :] = _pack_bf16_pairs(
            jnp.dot(act.astype(BF16), wd_bf[...], preferred_element_type=F32) + bd_ref[0, 0])

    def zero_rows(r0, n):
        o_ref[r0:r0 + n, :] = jnp.zeros((n, o_ref.shape[1]), o_ref.dtype)

    half = ROW_SUB // 2
    for s in range(ROW_BLK // ROW_SUB):
        r0 = s * ROW_SUB
        first = n_skip - r0

        @pl.when(first < half)
        def _():
            run_rows(r0, ROW_SUB, first)

        @pl.when((first >= half) & (first < ROW_SUB))
        def _():
            zero_rows(r0, half)
            run_rows(r0 + half, half, first - half)

        @pl.when(first >= ROW_SUB)
        def _():
            zero_rows(r0, ROW_SUB)

    nxt = be_ref[jnp.minimum(j, n_blocks - 1)]

    @pl.when((j == 0) | (nxt != be_ref[i]))
    def _():
        wu_bf[...] = wu_ref[0, 0].astype(BF16)
        wd_bf[...] = wd_ref[0, 0].astype(BF16)


def _experts(layer, block_e, n_skip, rows, w_up, b_up, w_down, b_down):
    n_rows, dh = rows.shape
    d = 2 * dh
    n_blocks = n_rows // ROW_BLK
    depth, ne, _, f2 = w_up.shape
    blk = lambda j: jnp.maximum(j - 1, 0)
    ahead = lambda j, be, nu: (layer, be[jnp.minimum(j, n_blocks - 1)], 0, 0)
    here = lambda j, be, nu: (layer, be[blk(j)], 0, 0)
    grid_spec = pltpu.PrefetchScalarGridSpec(
        num_scalar_prefetch=2,
        grid=(n_blocks + 1,),
        in_specs=[pl.BlockSpec((ROW_BLK, dh), lambda j, be, nu: (blk(j), 0)),
                  pl.BlockSpec((1, 1, d, f2), ahead),
                  pl.BlockSpec((1, 1, 1, f2), here),
                  pl.BlockSpec((1, 1, D_FF, d), ahead),
                  pl.BlockSpec((1, 1, 1, d), here)],
        out_specs=pl.BlockSpec((ROW_BLK, dh), lambda j, be, nu: (blk(j), 0)),
        scratch_shapes=[pltpu.VMEM((d, f2), BF16), pltpu.VMEM((D_FF, d), BF16)],
    )
    return pl.pallas_call(
        _expert_kernel,
        grid_spec=grid_spec,
        out_shape=jax.ShapeDtypeStruct((n_rows, dh), jnp.uint32),
        compiler_params=_cparams(("arbitrary",)),
        name="moe_experts",
    )(block_e, n_skip, rows, w_up, b_up.reshape(depth, ne, 1, f2), w_down,
      b_down.reshape(depth, ne, 1, d))


def _combine_kernel(y0_ref, y1_ref, y2_ref, y3_ref, gate_ref, x1_ref, g2_ref, fw_ref, *rest,
                    final):
    o_ref = rest[-1]
    gates = gate_ref[...]
    y_hi, y_lo = None, None
    for k, y_ref in enumerate((y0_ref, y1_ref, y2_ref, y3_ref)):
        hi, lo = _unpack_bf16_pairs(y_ref[...])
        g = gates[:, k:k + 1]
        y_hi = g * hi if y_hi is None else y_hi + g * hi
        y_lo = g * lo if y_lo is None else y_lo + g * lo
    y = jnp.concatenate([y_hi, y_lo], axis=1)
    x2 = x1_ref[...] + g2_ref[0] * y
    if final:
        x2 = (x2 * lax.rsqrt(jnp.mean(x2 * x2, axis=-1, keepdims=True) + NORM_EPS)) * fw_ref[...]
    o_ref[...] = x2


def _combine(y4, gates, x1, mod5, b0, layer, fw, seq, final, grp, n_out, prev_out, th=1024):
    t, d = x1.shape
    th = min(th, seq)
    steps = t // th
    per_b = seq // th
    slot = lambda k: pl.BlockSpec((th, d // 2), lambda i: (k * steps + i, 0))
    return pl.pallas_call(
        functools.partial(_combine_kernel, final=final),
        grid=(steps,),
        in_specs=[slot(0), slot(1), slot(2), slot(3),
                  pl.BlockSpec((th, TOP_K), lambda i: (i, 0)),
                  pl.BlockSpec((th, d), lambda i: (i, 0)),
                  _mod_spec(layer, MOD_G2, per_b, b0),
                  pl.BlockSpec((1, d), lambda i: (0, 0))]
                 + ([] if prev_out is None else [pl.BlockSpec(memory_space=pl.ANY)]),
        out_specs=pl.BlockSpec((None, th, d), lambda i: (grp, i, 0)),
        out_shape=jax.ShapeDtypeStruct((n_out, t, d), F32),
        input_output_aliases={} if prev_out is None else {8: 0},
        compiler_params=_cparams(("arbitrary",)),
        name="moe_combine",
    )(y4, y4, y4, y4, gates, x1, mod5, fw, *([] if prev_out is None else [prev_out]))


def _retention_tables(seq):
    f32 = np.float32
    pos = np.arange(seq, dtype=f32)
    inv_freq = (f32(ROPE_BASE) ** (-np.arange(0, HEAD_DK, 2, dtype=f32) / f32(HEAD_DK))).astype(f32)
    ang = pos[:, None] * inv_freq[None, :]
    cos, sin = np.cos(ang), np.sin(ang)
    zero = np.zeros_like(sin)
    zpad = np.zeros((seq, LANES - 2 * HEAD_DK), f32)
    cos_t = np.concatenate([cos, cos, cos, cos, zpad], axis=1)
    sina_t = np.concatenate([-sin, zero, -sin, zero, zpad], axis=1)
    sinb_t = np.concatenate([zero, sin, zero, sin, zpad], axis=1)
    log_gamma = np.log1p(-np.exp2(f32(-5.0) - np.arange(N_HEADS, dtype=f32))).astype(f32)
    log_decay = np.broadcast_to(log_gamma[None, :, None], (RET_CHUNK, N_HEADS, HEAD_DK))
    cum = np.cumsum(log_decay, axis=0, dtype=f32)
    tot = cum[-1:]

    def shp(a):
        flat = a.reshape(a.shape[0], N_HEADS * HEAD_DK)
        return np.where(_DK_SRC >= 0, flat[:, np.maximum(_DK_SRC, 0)], f32(1.0)).astype(f32)

    tabs = (cos_t, sina_t, sinb_t, shp(np.exp(cum)), shp(np.exp(-cum)), shp(np.exp(tot - cum)),
            shp(np.exp(tot)))
    return tuple(jnp.asarray(a, F32) for a in tabs)


def kernel(x, c, norm1_w, norm2_w, w_mod, b_mod, w_in, gla_w_a2, gla_b_a, gla_norm_w, ret_norm_w, s5_a_re, s5_a_im, s5_log_dt, s5_b_re, s5_b_im, s5_c_re, s5_c_im, s5_d, s5_w_glu, s5_b_glu, w_out, router_w, router_b, w_up, b_up, w_down, b_down, final_norm_w):
    batch, seq, d = x.shape
    depth = w_mod.shape[0]
    gb = batch // N_STREAMS
    t = gb * seq
    n_slots = t * TOP_K
    n_blocks = n_slots // ROW_BLK + N_EXPERTS
    n_rows = n_blocks * ROW_BLK

    mod = _modulation(c, w_mod, b_mod)
    mod5 = mod.reshape(depth, batch, 6, 1, d).transpose(0, 2, 1, 3, 4)
    ret_tabs = _retention_tables(seq)

    w_p = _take_cols(w_in, _IN_SRC).astype(BF16)
    wa_p = jnp.zeros((depth, LANES, QKP), F32).at[:, :GATE_RANK].set(
        _take_cols(gla_w_a2, _DK_SRC)).astype(BF16)
    ba_p = _take_cols(gla_b_a, _DK_SRC).reshape(depth, 1, QKP)
    gnw = _take_cols(gla_norm_w, _DV_SRC).reshape(depth, 1, HP)
    rnw = _take_cols(ret_norm_w, _DV_SRC).reshape(depth, 1, HP)
    kv = N_HEADS * HEAD_DV
    wo_g = _take_rows(w_out[:, :kv], _DV_SRC).astype(BF16)
    wo_r = _take_rows(w_out[:, kv:2 * kv], _DV_SRC).astype(BF16)
    wo_s = w_out[:, 2 * kv:].astype(BF16)
    rw_t = jnp.swapaxes(router_w, 1, 2)
    rw_hi = rw_t.astype(BF16)
    rw_p = jnp.concatenate([rw_hi, (rw_t - rw_hi.astype(F32)).astype(BF16)], axis=1)
    rb_p = jnp.broadcast_to(router_b[:, :, None], (depth, N_EXPERTS, LANES))
    s5_tabs = jax.vmap(_s5_tables)(s5_a_re, s5_a_im, s5_log_dt, s5_b_re, s5_b_im, s5_c_re,
                                   s5_c_im, s5_d, s5_w_glu, s5_b_glu)
    n1 = norm1_w.reshape(depth, 1, d)
    n2 = norm2_w.reshape(depth, 1, d)
    fw = final_norm_w.reshape(1, d)

    xs = [(x.reshape(N_STREAMS, t, d), g) for g in range(N_STREAMS)]
    out = None

    for i in range(depth):
        final = i == depth - 1
        for g in range(N_STREAMS):
            x3, grp = xs[g]
            b0 = g * gb

            proj = _in_projection(x3, grp, mod5, b0, i, n1, w_p, seq)
            o_gla = _gla_mixer(proj, gb, seq, i, wa_p, ba_p, gnw)
            o_ret = _ret_mixer(proj, gb, seq, i, *ret_tabs, rnw)
            o_s5 = _s5_mixer(proj, gb, seq, i, s5_tabs)

            x1, hdn, idx, gates, rank, counts = _outproj_router(
                o_gla, o_ret, o_s5, wo_g, wo_r, wo_s, x3, grp, mod5, b0, i, n2, rw_p, rb_p, seq)

            cnt = counts[:, 0].astype(jnp.int32)
            padded = (cnt + ROW_BLK - 1) // ROW_BLK * ROW_BLK
            pad_ends = jnp.cumsum(padded)
            first_row = pad_ends - cnt
            slot_start = jnp.sum(jnp.where(idx[:TOP_K, :, None] == jnp.arange(N_EXPERTS),
                                           first_row, 0), axis=-1)
            dest_sm = (slot_start + rank[:TOP_K]).astype(jnp.int32).reshape(-1)
            gates_tm = gates[:TOP_K].T
            blk_start = jnp.arange(n_blocks, dtype=jnp.int32) * ROW_BLK
            block_e = jnp.minimum(jnp.sum(pad_ends[None, :] <= blk_start[:, None], axis=1),
                                  N_EXPERTS - 1).astype(jnp.int32)
            n_skip = jnp.where(blk_start < pad_ends[-1],
                               jnp.clip(first_row[block_e] - blk_start, 0, ROW_BLK),
                               ROW_BLK).astype(jnp.int32)

            rows = _scatter_rows(hdn, dest_sm, n_rows)
            out_rows = _experts(i, block_e, n_skip, rows, w_up, b_up, w_down, b_down)
            y4 = _gather_rows(out_rows, dest_sm)
            if final:
                out = _combine(y4, gates_tm, x1, mod5, b0, i, fw, seq, True, g, N_STREAMS, out)
            else:
                xs[g] = (_combine(y4, gates_tm, x1, mod5, b0, i, fw, seq, False, 0, 1, None), 0)

    return out.reshape(batch, seq, d)
```

```python
import functools

import numpy as np
import jax
import jax.numpy as jnp
from jax import lax
from jax.experimental import pallas as pl
from jax.experimental.pallas import tpu as pltpu
from jax.experimental.pallas import tpu_sc as plsc

D_MODEL = 1024
CHUNK = 64
RET_CHUNK = 128
NORM_EPS = 1e-5
N_HEADS = 4
HEAD_DK = 48
HEAD_DV = 96
GATE_RANK = 16
GATE_TEMP = 16.0
ROPE_BASE = 10000.0
S5_WIDTH = 256
S5_GROUP_DIM = 16
S5_GROUPS = 16
S5_STATE = 64
N_EXPERTS = 32
TOP_K = 4
D_FF = 1024
SWIGLU_LIMIT = 7.0
SWIGLU_ALPHA = 1.702

LANES = 128
HEAD_PAD = LANES
HP = N_HEADS * HEAD_PAD
N_PAIRS = N_HEADS // 2
QKP = N_PAIRS * LANES
VMEM_LIMIT = 56 * 1024 * 1024

OFF_GQ, OFF_GK, OFF_GV, OFF_GG = 0, QKP, 2 * QKP, 2 * QKP + HP
OFF_RQ = OFF_GG + HP
OFF_RK, OFF_RV, OFF_RG = OFF_RQ + QKP, OFF_RQ + 2 * QKP, OFF_RQ + 2 * QKP + HP
OFF_SU = OFF_RG + HP
OFF_GA = OFF_SU + S5_WIDTH
NP_COLS = OFF_GA + LANES
PROJ_CHUNKS = (1280, 1280, 896)

ROW_BLK = 1024
ROW_SUB = 512
MOD_SH1, MOD_SC1, MOD_G1, MOD_SH2, MOD_SC2, MOD_G2 = range(6)
N_STREAMS = 1

F32 = jnp.float32
BF16 = jnp.bfloat16


def _DK_SRC_LANE(h, d):
    return (h // 2) * LANES + (h % 2) * HEAD_DK + d


def _in_col_map():
    src = -np.ones((NP_COLS,), np.int64)
    kq = N_HEADS * HEAD_DK
    kv = N_HEADS * HEAD_DV
    base = dict(gq=0, gk=kq, gv=2 * kq, gg=2 * kq + kv, ga=2 * kq + 2 * kv)
    r0 = base['ga'] + GATE_RANK
    base.update(rq=r0, rk=r0 + kq, rv=r0 + 2 * kq, rg=r0 + 2 * kq + kv, su=r0 + 2 * kq + 2 * kv)
    for h in range(N_HEADS):
        for d in range(HEAD_DK):
            lane = _DK_SRC_LANE(h, d)
            src[OFF_GQ + lane] = base['gq'] + h * HEAD_DK + d
            src[OFF_GK + lane] = base['gk'] + h * HEAD_DK + d
            src[OFF_RQ + lane] = base['rq'] + h * HEAD_DK + d
            src[OFF_RK + lane] = base['rk'] + h * HEAD_DK + d
        for d in range(HEAD_DV):
            src[OFF_GV + h * HEAD_PAD + d] = base['gv'] + h * HEAD_DV + d
            src[OFF_GG + h * HEAD_PAD + d] = base['gg'] + h * HEAD_DV + d
            src[OFF_RV + h * HEAD_PAD + d] = base['rv'] + h * HEAD_DV + d
            src[OFF_RG + h * HEAD_PAD + d] = base['rg'] + h * HEAD_DV + d
    src[OFF_SU:OFF_SU + S5_WIDTH] = base['su'] + np.arange(S5_WIDTH)
    src[OFF_GA:OFF_GA + GATE_RANK] = base['ga'] + np.arange(GATE_RANK)
    return src


_IN_SRC = _in_col_map()


def _head_pad_map(width):
    src = -np.ones((HP,), np.int64)
    for h in range(N_HEADS):
        src[h * HEAD_PAD:h * HEAD_PAD + width] = h * width + np.arange(width)
    return src


_DV_SRC = _head_pad_map(HEAD_DV)
_DK_SRC = -np.ones((QKP,), np.int64)
for _h in range(N_HEADS):
    for _d in range(HEAD_DK):
        _DK_SRC[_DK_SRC_LANE(_h, _d)] = _h * HEAD_DK + _d


def _take_static(w, src, axis):
    axis = axis % w.ndim
    pieces, start = [], 0
    for j in range(1, len(src) + 1):
        run_ends = (j == len(src) or (src[j] < 0) != (src[start] < 0)
                    or (src[start] >= 0 and src[j] != src[j - 1] + 1))
        if run_ends:
            if src[start] < 0:
                shape = w.shape[:axis] + (j - start,) + w.shape[axis + 1:]
                pieces.append(jnp.zeros(shape, w.dtype))
            else:
                pieces.append(lax.slice_in_dim(w, int(src[start]), int(src[j - 1]) + 1, axis=axis))
            start = j
    return jnp.concatenate(pieces, axis=axis)


def _take_cols(w, src):
    return _take_static(w, src, -1)


def _take_rows(w, src):
    return _take_static(w, src, -2)


def _layer_spec(shape, layer):
    return pl.BlockSpec((None,) + tuple(shape), lambda *_: (layer,) + (0,) * len(shape))


def _mod_spec(layer, which, per_b, b0):
    return pl.BlockSpec((None, None, 1, 1, D_MODEL),
                        lambda i: (layer, which, b0 + i // per_b, 0, 0))


def _cparams(sem):
    return pltpu.CompilerParams(dimension_semantics=sem, vmem_limit_bytes=VMEM_LIMIT)


def _mod_kernel(c_ref, w_ref, b_ref, o_ref):
    c = c_ref[...]
    cond = c * jax.nn.sigmoid(c)
    o_ref[0] = jnp.dot(cond, w_ref[0], preferred_element_type=F32,
                       precision=lax.Precision.HIGHEST) + b_ref[0]


def _modulation(c, w_mod, b_mod):
    depth, d, n = w_mod.shape
    b = c.shape[0]
    nb = 1536
    return pl.pallas_call(
        _mod_kernel,
        grid=(depth, n // nb),
        in_specs=[pl.BlockSpec((b, d), lambda l, j: (0, 0)),
                  pl.BlockSpec((1, d, nb), lambda l, j: (l, 0, j)),
                  pl.BlockSpec((1, 1, nb), lambda l, j: (l, 0, j))],
        out_specs=pl.BlockSpec((1, b, nb), lambda l, j: (l, 0, j)),
        out_shape=jax.ShapeDtypeStruct((depth, b, n), F32),
        compiler_params=_cparams(("arbitrary", "arbitrary")),
        name="adaln_mod",
    )(c, w_mod, b_mod.reshape(depth, 1, n))


def _rms_mod(x, nw, sc, sh):
    y = x * lax.rsqrt(jnp.mean(x * x, axis=-1, keepdims=True) + NORM_EPS)
    return (y * nw) * (1.0 + sc) + sh


def _inproj_kernel(x_ref, sc_ref, sh_ref, nw_ref, w_ref, o_ref):
    h = _rms_mod(x_ref[...], nw_ref[...], sc_ref[0], sh_ref[0]).astype(BF16)
    assert sum(PROJ_CHUNKS) == NP_COLS
    for j, width in enumerate(PROJ_CHUNKS):
        cs = slice(sum(PROJ_CHUNKS[:j]), sum(PROJ_CHUNKS[:j]) + width)
        o_ref[:, cs] = jnp.dot(h, w_ref[:, cs], preferred_element_type=F32).astype(BF16)


def _in_projection(x3, grp, mod5, b0, layer, nw, w_p, seq, tm=1024):
    _, t, d = x3.shape
    tm = min(tm, seq)
    per_b = seq // tm
    return pl.pallas_call(
        _inproj_kernel,
        grid=(t // tm,),
        in_specs=[pl.BlockSpec((None, tm, d), lambda i: (grp, i, 0)),
                  _mod_spec(layer, MOD_SC1, per_b, b0), _mod_spec(layer, MOD_SH1, per_b, b0),
                  _layer_spec((1, d), layer), _layer_spec((d, NP_COLS), layer)],
        out_specs=pl.BlockSpec((tm, NP_COLS), lambda i: (i, 0)),
        out_shape=jax.ShapeDtypeStruct((t, NP_COLS), BF16),
        compiler_params=_cparams(("arbitrary",)),
        name="in_proj",
    )(x3, mod5, mod5, nw, w_p)


_NT = (((1,), (1,)), ((), ()))
_TN = (((0,), (0,)), ((), ()))


def _tri_mask(n=CHUNK):
    r = lax.broadcasted_iota(jnp.int32, (n, n), 0)
    c = lax.broadcasted_iota(jnp.int32, (n, n), 1)
    return r >= c


def _pair_masks(rows):
    lane = lax.broadcasted_iota(jnp.int32, (rows, LANES), 1)
    return lane < HEAD_DK, (lane >= HEAD_DK) & (lane < 2 * HEAD_DK)


def _head_attention(qd, ki, ke, vh, et, st_ref, h, causal):
    qb = qd.astype(BF16)
    sc = lax.dot_general(qb, ki.astype(BF16), _NT, preferred_element_type=F32)
    sc = jnp.where(causal, sc, 0.0)
    st = st_ref[h]
    o = jnp.dot(sc.astype(BF16), vh, preferred_element_type=F32)
    o = o + lax.dot_general(qb, st.astype(BF16), _NT, preferred_element_type=F32)
    st_ref[h] = st * et + lax.dot_general(vh, ke.astype(BF16), _TN, preferred_element_type=F32)
    return o


def _gla_kernel(q_ref, k_ref, v_ref, g_ref, a_ref, wa_ref, ba_ref, nw_ref, tri_ref, o_ref,
                st_ref, qd_s, ki_s, ke_s, et_s):
    @pl.when(pl.program_id(1) == 0)
    def _():
        st_ref[...] = jnp.zeros_like(st_ref)

    causal = _tri_mask()
    tl = q_ref.shape[0]
    n_chunks = tl // CHUNK

    z = jnp.dot(a_ref[...], wa_ref[...], preferred_element_type=F32) + ba_ref[...]
    la = (jnp.minimum(z, 0.0) - jnp.log1p(jnp.exp(-jnp.abs(z)))) * (1.0 / GATE_TEMP)
    hi = la.astype(BF16)
    lo = (la - hi.astype(F32)).astype(BF16)
    cum = (jnp.dot(tri_ref[...], hi, preferred_element_type=F32)
           + jnp.dot(tri_ref[...], lo, preferred_element_type=F32))
    cum3 = cum.reshape(n_chunks, CHUNK, QKP)
    tot3 = cum3[:, CHUNK - 1:CHUNK, :]
    qd = (q_ref[...].astype(F32) * (HEAD_DK ** -0.5)) * jnp.exp(cum)
    masks = _pair_masks(tl)
    for h in range(N_HEADS):
        pair = slice((h // 2) * LANES, (h // 2 + 1) * LANES)
        qd_s[:, h * HEAD_PAD:(h + 1) * HEAD_PAD] = jnp.where(masks[h % 2], qd[:, pair],
                                                              0.0).astype(BF16)
    kf = k_ref[...].astype(F32)
    ki_s[...] = (kf * jnp.exp(-cum)).astype(BF16)
    ke_s[...] = (kf * jnp.exp(tot3 - cum3).reshape(tl, QKP)).astype(BF16)
    et_s[...] = jnp.exp(tot3).reshape(n_chunks, QKP)

    def chunk(c, carry):
        r = pl.ds(pl.multiple_of(c * CHUNK, CHUNK), CHUNK)
        et = et_s[pl.ds(c, 1), :]
        for h in range(N_HEADS):
            sl = slice(h * HEAD_PAD, (h + 1) * HEAD_PAD)
            pair = slice((h // 2) * LANES, (h // 2 + 1) * LANES)
            o = _head_attention(qd_s[r, sl], ki_s[r, pair], ke_s[r, pair], v_ref[r, sl],
                                et[:, pair], st_ref, h, causal)
            ms = jnp.sum(o * o, axis=-1, keepdims=True) * (1.0 / HEAD_DV)
            y = (o * lax.rsqrt(ms + NORM_EPS)) * nw_ref[:, sl]
            g = g_ref[r, sl].astype(F32)
            o_ref[r, sl] = (y * (g * jax.nn.sigmoid(g))).astype(BF16)
        return carry

    lax.fori_loop(0, n_chunks, chunk, 0, unroll=4)


def _ret_kernel(q_ref, k_ref, v_ref, g_ref, cos_ref, sina_ref, sinb_ref, dq_ref, dki_ref,
                dke_ref, dt_ref, nw_ref, o_ref, st_ref):
    @pl.when(pl.program_id(1) == 0)
    def _():
        st_ref[...] = jnp.zeros_like(st_ref)

    causal = _tri_mask(RET_CHUNK)
    n_chunks = q_ref.shape[0] // RET_CHUNK
    lane = lax.broadcasted_iota(jnp.int32, (RET_CHUNK, HEAD_PAD), 1)
    real = lane < HEAD_DV
    masks = _pair_masks(RET_CHUNK)
    half = HEAD_DK // 2

    def rotary(t, cos, sina, sinb):
        return (t * cos + pltpu.roll(t, LANES - half, 1) * sina + pltpu.roll(t, half, 1) * sinb)

    def chunk(c, carry):
        r = pl.ds(pl.multiple_of(c * RET_CHUNK, RET_CHUNK), RET_CHUNK)
        cos, sina, sinb = cos_ref[r, :], sina_ref[r, :], sinb_ref[r, :]
        pair_q, pair_ki, pair_ke = [], [], []
        for p in range(N_PAIRS):
            ps = slice(p * LANES, (p + 1) * LANES)
            qr = rotary(q_ref[r, ps].astype(F32), cos, sina, sinb) * dq_ref[:, ps]
            kr = rotary(k_ref[r, ps].astype(F32), cos, sina, sinb) * (HEAD_DK ** -0.5)
            pair_q.append(qr)
            pair_ki.append((kr * dki_ref[:, ps]).astype(BF16))
            pair_ke.append((kr * dke_ref[:, ps]).astype(BF16))
        for h in range(N_HEADS):
            sl = slice(h * HEAD_PAD, (h + 1) * HEAD_PAD)
            p = h // 2
            qd = jnp.where(masks[h % 2], pair_q[p], 0.0)
            o = _head_attention(qd, pair_ki[p], pair_ke[p], v_ref[r, sl],
                                dt_ref[:, p * LANES:(p + 1) * LANES], st_ref, h, causal)
            mu = jnp.sum(o, axis=-1, keepdims=True) * (1.0 / HEAD_DV)
            oc = jnp.where(real, o - mu, 0.0)
            var = jnp.sum(oc * oc, axis=-1, keepdims=True) * (1.0 / HEAD_DV)
            y = (oc * lax.rsqrt(var + NORM_EPS)) * nw_ref[:, sl]
            g = g_ref[r, sl].astype(F32)
            o_ref[r, sl] = (y * (g * jax.nn.sigmoid(g))).astype(BF16)
        return carry

    lax.fori_loop(0, n_chunks, chunk, 0, unroll=2)


def _proj_spec(tl, width, col_off, per_b):
    cb = col_off // width
    return pl.BlockSpec((tl, width), lambda b, l: (b * per_b + l, cb))


def _full(shape):
    return pl.BlockSpec(shape, lambda b, l: (0,) * len(shape))


def _gla_mixer(proj, batch, seq, layer, wa_p, ba_p, nw_p, tl=512):
    per_b = seq // tl
    pos = np.arange(tl)
    tri_bd = jnp.asarray((pos[:, None] // CHUNK == pos[None, :] // CHUNK)
                         & (pos[:, None] >= pos[None, :]), BF16)
    return pl.pallas_call(
        _gla_kernel,
        grid=(batch, per_b),
        in_specs=[_proj_spec(tl, QKP, OFF_GQ, per_b), _proj_spec(tl, QKP, OFF_GK, per_b),
                  _proj_spec(tl, HP, OFF_GV, per_b), _proj_spec(tl, HP, OFF_GG, per_b),
                  _proj_spec(tl, LANES, OFF_GA, per_b),
                  _layer_spec((LANES, QKP), layer), _layer_spec((1, QKP), layer),
                  _layer_spec((1, HP), layer), _full((tl, tl))],
        out_specs=pl.BlockSpec((tl, HP), lambda b, l: (b * per_b + l, 0)),
        out_shape=jax.ShapeDtypeStruct((batch * seq, HP), BF16),
        scratch_shapes=[pltpu.VMEM((N_HEADS, HEAD_PAD, HEAD_PAD), F32),
                        pltpu.VMEM((tl, HP), BF16), pltpu.VMEM((tl, QKP), BF16),
                        pltpu.VMEM((tl, QKP), BF16), pltpu.VMEM((tl // CHUNK, QKP), F32)],
        compiler_params=_cparams(("arbitrary", "arbitrary")),
        name="gla_mixer",
    )(proj, proj, proj, proj, proj, wa_p, ba_p, nw_p, tri_bd)


def _ret_mixer(proj, batch, seq, layer, cos_t, sina_t, sinb_t, dq, dki, dke, dtot, nw_p, tl=512):
    per_b = seq // tl
    return pl.pallas_call(
        _ret_kernel,
        grid=(batch, per_b),
        in_specs=[_proj_spec(tl, QKP, OFF_RQ, per_b), _proj_spec(tl, QKP, OFF_RK, per_b),
                  _proj_spec(tl, HP, OFF_RV, per_b), _proj_spec(tl, HP, OFF_RG, per_b),
                  pl.BlockSpec((tl, LANES), lambda b, l: (l, 0)),
                  pl.BlockSpec((tl, LANES), lambda b, l: (l, 0)),
                  pl.BlockSpec((tl, LANES), lambda b, l: (l, 0)),
                  _full((RET_CHUNK, QKP)), _full((RET_CHUNK, QKP)), _full((RET_CHUNK, QKP)),
                  _full((1, QKP)), _layer_spec((1, HP), layer)],
        out_specs=pl.BlockSpec((tl, HP), lambda b, l: (b * per_b + l, 0)),
        out_shape=jax.ShapeDtypeStruct((batch * seq, HP), BF16),
        scratch_shapes=[pltpu.VMEM((N_HEADS, HEAD_PAD, HEAD_PAD), F32)],
        compiler_params=_cparams(("arbitrary", "arbitrary")),
        name="ret_mixer",
    )(proj, proj, proj, proj, cos_t, sina_t, sinb_t, dq, dki, dke, dtot, nw_p)


S5_N = S5_GROUPS * S5_STATE
S5_SLAB = 256


def _gelu_tanh(x):
    return 0.5 * x * (1.0 + jnp.tanh(np.sqrt(2.0 / np.pi) * (x + 0.044715 * (x * x * x))))


def _s5_kernel(u_ref, bb_ref, cb_ref, nr_ref, ni_ref, pr_ref, pi_ref, lr_ref, li_ref,
               d_ref, wg_ref, bg_ref, o_ref, sr_ref, si_ref, x_scr, s_scr):
    @pl.when(pl.program_id(1) == 0)
    def _():
        sr_ref[...] = jnp.zeros_like(sr_ref)
        si_ref[...] = jnp.zeros_like(si_ref)

    tri = _tri_mask().astype(BF16)
    n_chunks = u_ref.shape[0] // CHUNK
    u = u_ref[...]
    x_scr[...] = jnp.dot(u, bb_ref[...], preferred_element_type=F32)

    def chunk(c, carry):
        r = pl.ds(pl.multiple_of(c * CHUNK, CHUNK), CHUNK)
        for j in range(S5_N // S5_SLAB):
            cs = slice(j * S5_SLAB, (j + 1) * S5_SLAB)
            ci = slice(S5_N + j * S5_SLAB, S5_N + (j + 1) * S5_SLAB)
            xr, xi = x_scr[r, cs], x_scr[r, ci]
            nr, ni = nr_ref[:, cs], ni_ref[:, cs]
            p_r = jnp.dot(tri, (xr * nr - xi * ni).astype(BF16), preferred_element_type=F32)
            p_i = jnp.dot(tri, (xr * ni + xi * nr).astype(BF16), preferred_element_type=F32)
            s0r, s0i = sr_ref[:, cs], si_ref[:, cs]
            lr, li = lr_ref[:, cs], li_ref[:, cs]
            q_r = p_r + (s0r * lr - s0i * li)
            q_i = p_i + (s0r * li + s0i * lr)
            pr, pi = pr_ref[:, cs], pi_ref[:, cs]
            s_r = q_r * pr - q_i * pi
            s_i = q_r * pi + q_i * pr
            sr_ref[:, cs] = s_r[CHUNK - 1:CHUNK, :]
            si_ref[:, cs] = s_i[CHUNK - 1:CHUNK, :]
            s_scr[r, cs] = s_r.astype(BF16)
            s_scr[r, ci] = s_i.astype(BF16)
        return carry

    lax.fori_loop(0, n_chunks, chunk, 0, unroll=2)
    y = jnp.dot(s_scr[...], cb_ref[...], preferred_element_type=F32)
    y = _gelu_tanh(y + d_ref[...] * u.astype(F32))
    gate = jnp.dot(y.astype(BF16), wg_ref[...], preferred_element_type=F32) + bg_ref[...]
    o_ref[...] = (y * jax.nn.sigmoid(gate)).astype(BF16)


def _s5_mixer(proj, batch, seq, layer, tabs, tl=512):
    per_b = seq // tl
    bb, cb, nr, ni, pr, pi, lr, li, dsk, wg, bg = tabs
    return pl.pallas_call(
        _s5_kernel,
        grid=(batch, per_b),
        in_specs=[_proj_spec(tl, S5_WIDTH, OFF_SU, per_b),
                  *[_layer_spec(a.shape[1:], layer) for a in tabs]],
        out_specs=pl.BlockSpec((tl, S5_WIDTH), lambda b, l: (b * per_b + l, 0)),
        out_shape=jax.ShapeDtypeStruct((batch * seq, S5_WIDTH), BF16),
        scratch_shapes=[pltpu.VMEM((1, S5_N), F32), pltpu.VMEM((1, S5_N), F32),
                        pltpu.VMEM((tl, 2 * S5_N), F32), pltpu.VMEM((tl, 2 * S5_N), BF16)],
        compiler_params=_cparams(("arbitrary", "arbitrary")),
        name="s5_mixer",
    )(proj, bb, cb, nr, ni, pr, pi, lr, li, dsk, wg, bg)


def _s5_tables(a_re, a_im, log_dt, b_re, b_im, c_re, c_im, d_skip, w_glu, b_glu):
    lam = lax.complex(a_re, a_im)
    dt = jnp.exp(log_dt)[:, None]
    lam_bar = jnp.exp(lam * dt)
    b_bar = ((lam_bar - 1.0) / lam)[..., None] * lax.complex(b_re, b_im)
    eye = jnp.eye(S5_GROUPS, dtype=F32)
    def blk_b(m):
        return jnp.einsum('gph,gk->ghkp', m, eye).reshape(S5_WIDTH, S5_N)
    bb = jnp.concatenate([blk_b(jnp.real(b_bar)), blk_b(jnp.imag(b_bar))], axis=1)
    def blk_c(m):
        return jnp.einsum('ghp,gk->kpgh', m, eye).reshape(S5_N, S5_WIDTH)
    cb = jnp.concatenate([blk_c(c_re), blk_c(-c_im)], axis=0)
    steps = jnp.arange(CHUNK, dtype=F32)[:, None, None]
    lam_dt = (lam * dt)[None]
    pos = jnp.exp(lam_dt * steps).reshape(CHUNK, S5_N)
    neg = jnp.exp(-lam_dt * steps).reshape(CHUNK, S5_N)
    one = lam_bar.reshape(1, S5_N)
    return (bb.astype(BF16), cb.astype(BF16), jnp.real(neg), jnp.imag(neg), jnp.real(pos),
            jnp.imag(pos), jnp.real(one), jnp.imag(one), d_skip.reshape(1, S5_WIDTH),
            w_glu.astype(BF16), b_glu.reshape(1, S5_WIDTH))


def _pack_bf16_pairs(x):
    w = x.shape[1] // 2
    xb = x.astype(BF16).astype(F32)
    hi = lax.bitcast_convert_type(xb[:, :w], jnp.uint32)
    lo = lax.bitcast_convert_type(xb[:, w:], jnp.uint32)
    return hi | (lo >> 16)


def _unpack_bf16_pairs(p):
    hi = lax.bitcast_convert_type(p & jnp.uint32(0xFFFF0000), F32)
    lo = lax.bitcast_convert_type(p << 16, F32)
    return hi, lo


def _router_kernel(og_ref, or_ref, os_ref, wg_ref, wr_ref, ws_ref, x_ref, g1_ref, sc_ref,
                   sh_ref, nw_ref, rw_ref, rb_ref,
                   x1_ref, h_ref, idx_ref, gate_ref, rank_ref, cnt_ref, carry_ref):
    i = pl.program_id(0)

    @pl.when(i == 0)
    def _():
        carry_ref[...] = jnp.zeros_like(carry_ref)

    mix = (jnp.dot(og_ref[...], wg_ref[...], preferred_element_type=F32)
           + jnp.dot(or_ref[...], wr_ref[...], preferred_element_type=F32)
           + jnp.dot(os_ref[...], ws_ref[...], preferred_element_type=F32))
    x1 = x_ref[...] + g1_ref[0] * mix
    x1_ref[...] = x1
    hdn = _rms_mod(x1, nw_ref[...], sc_ref[0], sh_ref[0])
    h_ref[...] = _pack_bf16_pairs(hdn)
    h_hi = hdn.astype(BF16)
    h_lo = (hdn - h_hi.astype(F32)).astype(BF16)
    p = lax.dot_general(rw_ref[...], h_hi, _NT, preferred_element_type=F32)
    q = lax.dot_general(rw_ref[:N_EXPERTS, :], h_lo, _NT, preferred_element_type=F32)
    logits = (p[:N_EXPERTS] + p[N_EXPERTS:] + q) + rb_ref[:, 0:1]
    tm = logits.shape[1]
    eidx = lax.broadcasted_iota(jnp.int32, (N_EXPERTS, tm), 0)
    work = logits
    onehot = jnp.zeros((N_EXPERTS, tm), F32)
    vals, idxs, sels = [], [], []
    for _ in range(TOP_K):
        m = jnp.max(work, axis=0, keepdims=True)
        ix = jnp.min(jnp.where(work == m, eidx, N_EXPERTS), axis=0, keepdims=True)
        sel = eidx == ix
        work = jnp.where(sel, -jnp.inf, work)
        onehot = onehot + sel.astype(F32)
        vals.append(m)
        idxs.append(ix)
        sels.append(sel)
    exps = [jnp.exp(v - vals[0]) for v in vals]
    denom = exps[0] + exps[1] + exps[2] + exps[3]
    r = lax.broadcasted_iota(jnp.int32, (tm, tm), 0)
    c = lax.broadcasted_iota(jnp.int32, (tm, tm), 1)
    earlier = (r < c).astype(BF16)
    before = (jnp.dot(onehot.astype(BF16), earlier, preferred_element_type=F32)
              + carry_ref[:, 0:1])
    row8 = lax.broadcasted_iota(jnp.int32, (8, tm), 0)
    idx_out = jnp.zeros((8, tm), jnp.int32)
    gate_out = jnp.zeros((8, tm), F32)
    rank_out = jnp.zeros((8, tm), F32)
    for k in range(TOP_K):
        rk = jnp.sum(jnp.where(sels[k], before, 0.0), axis=0, keepdims=True)
        idx_out = jnp.where(row8 == k, idxs[k], idx_out)
        gate_out = jnp.where(row8 == k, exps[k] / denom, gate_out)
        rank_out = jnp.where(row8 == k, rk, rank_out)
    idx_ref[...] = idx_out
    gate_ref[...] = gate_out
    rank_ref[...] = rank_out.astype(jnp.int32)
    total = carry_ref[...] + jnp.sum(onehot, axis=1, keepdims=True)
    carry_ref[...] = total
    cnt_ref[...] = total


def _outproj_router(o_gla, o_ret, o_s5, wg, wr, ws, x3, grp, mod5, b0, layer, nw2, rw_p, rb_p,
                    seq, tm=1024):
    _, t, d = x3.shape
    tm = min(tm, seq)
    per_b = seq // tm
    row = lambda w: pl.BlockSpec((tm, w), lambda i: (i, 0))
    full = lambda s: pl.BlockSpec(s, lambda i: (0,) * len(s))
    slot_t = pl.BlockSpec((8, tm), lambda i: (0, i))
    return pl.pallas_call(
        _router_kernel,
        grid=(t // tm,),
        in_specs=[row(HP), row(HP), row(S5_WIDTH), _layer_spec((HP, d), layer),
                  _layer_spec((HP, d), layer), _layer_spec((S5_WIDTH, d), layer),
                  pl.BlockSpec((None, tm, d), lambda i: (grp, i, 0)),
                  _mod_spec(layer, MOD_G1, per_b, b0), _mod_spec(layer, MOD_SC2, per_b, b0),
                  _mod_spec(layer, MOD_SH2, per_b, b0), _layer_spec((1, d), layer),
                  _layer_spec((2 * N_EXPERTS, d), layer), _layer_spec((N_EXPERTS, LANES), layer)],
        out_specs=[row(d), row(d // 2), slot_t, slot_t, slot_t, full((N_EXPERTS, LANES))],
        out_shape=[jax.ShapeDtypeStruct((t, d), F32),
                   jax.ShapeDtypeStruct((t, d // 2), jnp.uint32),
                   jax.ShapeDtypeStruct((8, t), jnp.int32),
                   jax.ShapeDtypeStruct((8, t), F32),
                   jax.ShapeDtypeStruct((8, t), jnp.int32),
                   jax.ShapeDtypeStruct((N_EXPERTS, LANES), F32)],
        scratch_shapes=[pltpu.VMEM((N_EXPERTS, LANES), F32)],
        compiler_params=_cparams(("arbitrary",)),
        name="outproj_router",
    )(o_gla, o_ret, o_s5, wg, wr, ws, x3, mod5, mod5, mod5, nw2, rw_p, rb_p)


GATHER_WIN = 64


def _gather_rows(table, idx):
    m = idx.shape[0]
    w = table.shape[1]
    mesh = plsc.VectorSubcoreMesh(core_axis_name="core", subcore_axis_name="subcore")

    @functools.partial(pl.kernel, out_type=jax.ShapeDtypeStruct((m, w), table.dtype),
                       mesh=mesh, name="sc_row_gather")
    def gather(x_hbm, i_hbm, o_hbm):
        def body(i_vmem, o_vmem):
            pltpu.sync_copy(x_hbm.at[i_vmem], o_vmem)

        pltpu.emit_pipeline(
            body,
            grid=(m // GATHER_WIN,),
            in_specs=[pl.BlockSpec((GATHER_WIN,), lambda i: (i,))],
            out_specs=[pl.BlockSpec((GATHER_WIN, w), lambda i: (i, 0))],
            core_axis_name=("core", "subcore"),
            dimension_semantics=(pltpu.PARALLEL,),
        )(i_hbm, o_hbm)

    return gather(table, idx)


def _scatter_rows(x, dest_slot_major, n_rows):
    t, w = x.shape
    steps = t // GATHER_WIN
    mesh = plsc.VectorSubcoreMesh(core_axis_name="core", subcore_axis_name="subcore")

    @functools.partial(pl.kernel, out_type=jax.ShapeDtypeStruct((n_rows, w), x.dtype),
                       mesh=mesh, name="sc_row_scatter")
    def scatter(x_hbm, i_hbm, o_hbm):
        def body(x_vmem, i0, i1, i2, i3):
            for i_vmem in (i0, i1, i2, i3):
                pltpu.sync_copy(x_vmem, o_hbm.at[i_vmem])

        slot = lambda k: pl.BlockSpec((GATHER_WIN,), lambda i: (k * steps + i,))
        pltpu.emit_pipeline(
            body,
            grid=(steps,),
            in_specs=[pl.BlockSpec((GATHER_WIN, w), lambda i: (i, 0)),
                      slot(0), slot(1), slot(2), slot(3)],
            out_specs=[],
            core_axis_name=("core", "subcore"),
            dimension_semantics=(pltpu.PARALLEL,),
        )(x_hbm, i_hbm, i_hbm, i_hbm, i_hbm)

    return scatter(x, dest_slot_major)


def _expert_kernel(be_ref, ns_ref, rows_ref, wu_ref, bu_ref, wd_ref, bd_ref, o_ref,
                   wu_bf, wd_bf):
    i = pl.program_id(0)
    e = be_ref[i]
    prev = be_ref[jnp.maximum(i - 1, 0)]

    @pl.when((i == 0) | (e != prev))
    def _():
        wu_bf[...] = wu_ref[0, 0].astype(BF16)
        wd_bf[...] = wd_ref[0, 0].astype(BF16)

    def run_rows(r0, n, first):
        rs = slice(r0, r0 + n)
        row = lax.broadcasted_iota(jnp.int32, (n, rows_ref.shape[1]), 0)
        x_hi, x_lo = _unpack_bf16_pairs(jnp.where(row >= first, rows_ref[rs, :], jnp.uint32(0)))
        x = jnp.concatenate([x_hi.astype(BF16), x_lo.astype(BF16)], axis=1)
        up = jnp.dot(x, wu_bf[...], preferred_element_type=F32) + bu_ref[0, 0]
        x_glu = jnp.minimum(up[:, :D_FF], SWIGLU_LIMIT)
        x_lin = jnp.clip(up[:, D_FF:], -SWIGLU_LIMIT, SWIGLU_LIMIT)
        act = x_glu * jax.nn.sigmoid(SWIGLU_ALPHA * x_glu) * (x_lin + 1.0)
        o_ref[rs, :] = _pack_bf16_pairs(
            jnp.dot(act.astype(BF16), wd_bf[...], preferred_element_type=F32) + bd_ref[0, 0])

    def zero_rows(r0, n):
        o_ref[r0:r0 + n, :] = jnp.zeros((n, o_ref.shape[1]), o_ref.dtype)

    half = ROW_SUB // 2
    for s in range(ROW_BLK // ROW_SUB):
        r0 = s * ROW_SUB
        first = ns_ref[i] - r0

        quarter = ROW_SUB // 4
        for q in range(4):
            lo, hi = q * quarter, (q + 1) * quarter

            @pl.when((first < hi) if q == 0 else ((first >= lo) & (first < hi)))
            def _(lo=lo):
                if lo:
                    zero_rows(r0, lo)
                run_rows(r0 + lo, ROW_SUB - lo, first - lo)

        @pl.when(first >= ROW_SUB)
        def _():
            zero_rows(r0, ROW_SUB)


def _experts(layer, block_e, n_skip, rows, w_up, b_up, w_down, b_down):
    n_rows, dh = rows.shape
    d = 2 * dh
    n_blocks = n_rows // ROW_BLK
    depth, ne, _, f2 = w_up.shape
    wsel = lambda i, be, nu: (layer, be[i], 0, 0)
    grid_spec = pltpu.PrefetchScalarGridSpec(
        num_scalar_prefetch=2,
        grid=(n_blocks,),
        in_specs=[pl.BlockSpec((ROW_BLK, dh), lambda i, be, nu: (i, 0)),
                  pl.BlockSpec((1, 1, d, f2), wsel),
                  pl.BlockSpec((1, 1, 1, f2), wsel),
                  pl.BlockSpec((1, 1, D_FF, d), wsel),
                  pl.BlockSpec((1, 1, 1, d), wsel)],
        out_specs=pl.BlockSpec((ROW_BLK, dh), lambda i, be, nu: (i, 0)),
        scratch_shapes=[pltpu.VMEM((d, f2), BF16), pltpu.VMEM((D_FF, d), BF16)],
    )
    return pl.pallas_call(
        _expert_kernel,
        grid_spec=grid_spec,
        out_shape=jax.ShapeDtypeStruct((n_rows, dh), jnp.uint32),
        compiler_params=_cparams(("arbitrary",)),
        name="moe_experts",
    )(block_e, n_skip, rows, w_up, b_up.reshape(depth, ne, 1, f2), w_down,
      b_down.reshape(depth, ne, 1, d))


def _combine_kernel(y0_ref, y1_ref, y2_ref, y3_ref, gate_ref, x1_ref, g2_ref, fw_ref, *rest,
                    final):
    o_ref = rest[-1]
    gates = gate_ref[...]
    y_hi, y_lo = None, None
    for k, y_ref in enumerate((y0_ref, y1_ref, y2_ref, y3_ref)):
        hi, lo = _unpack_bf16_pairs(y_ref[...])
        g = gates[:, k:k + 1]
        y_hi = g * hi if y_hi is None else y_hi + g * hi
        y_lo = g * lo if y_lo is None else y_lo + g * lo
    y = jnp.concatenate([y_hi, y_lo], axis=1)
    x2 = x1_ref[...] + g2_ref[0] * y
    if final:
        x2 = (x2 * lax.rsqrt(jnp.mean(x2 * x2, axis=-1, keepdims=True) + NORM_EPS)) * fw_ref[...]
    o_ref[...] = x2


def _combine(y4, gates, x1, mod5, b0, layer, fw, seq, final, grp, n_out, prev_out, th=1024):
    t, d = x1.shape
    th = min(th, seq)
    steps = t // th
    per_b = seq // th
    slot = lambda k: pl.BlockSpec((th, d // 2), lambda i: (k * steps + i, 0))
    return pl.pallas_call(
        functools.partial(_combine_kernel, final=final),
        grid=(steps,),
        in_specs=[slot(0), slot(1), slot(2), slot(3),
                  pl.BlockSpec((th, TOP_K), lambda i: (i, 0)),
                  pl.BlockSpec((th, d), lambda i: (i, 0)),
                  _mod_spec(layer, MOD_G2, per_b, b0),
                  pl.BlockSpec((1, d), lambda i: (0, 0))]
                 + ([] if prev_out is None else [pl.BlockSpec(memory_space=pl.ANY)]),
        out_specs=pl.BlockSpec((None, th, d), lambda i: (grp, i, 0)),
        out_shape=jax.ShapeDtypeStruct((n_out, t, d), F32),
        input_output_aliases={} if prev_out is None else {8: 0},
        compiler_params=_cparams(("arbitrary",)),
        name="moe_combine",
    )(y4, y4, y4, y4, gates, x1, mod5, fw, *([] if prev_out is None else [prev_out]))


def _retention_tables(seq):
    f32 = np.float32
    pos = np.arange(seq, dtype=f32)
    inv_freq = (f32(ROPE_BASE) ** (-np.arange(0, HEAD_DK, 2, dtype=f32) / f32(HEAD_DK))).astype(f32)
    ang = pos[:, None] * inv_freq[None, :]
    cos, sin = np.cos(ang), np.sin(ang)
    zero = np.zeros_like(sin)
    zpad = np.zeros((seq, LANES - 2 * HEAD_DK), f32)
    cos_t = np.concatenate([cos, cos, cos, cos, zpad], axis=1)
    sina_t = np.concatenate([-sin, zero, -sin, zero, zpad], axis=1)
    sinb_t = np.concatenate([zero, sin, zero, sin, zpad], axis=1)
    log_gamma = np.log1p(-np.exp2(f32(-5.0) - np.arange(N_HEADS, dtype=f32))).astype(f32)
    log_decay = np.broadcast_to(log_gamma[None, :, None], (RET_CHUNK, N_HEADS, HEAD_DK))
    cum = np.cumsum(log_decay, axis=0, dtype=f32)
    tot = cum[-1:]

    def shp(a):
        flat = a.reshape(a.shape[0], N_HEADS * HEAD_DK)
        return np.where(_DK_SRC >= 0, flat[:, np.maximum(_DK_SRC, 0)], f32(1.0)).astype(f32)

    tabs = (cos_t, sina_t, sinb_t, shp(np.exp(cum)), shp(np.exp(-cum)), shp(np.exp(tot - cum)),
            shp(np.exp(tot)))
    return tuple(jnp.asarray(a, F32) for a in tabs)


def kernel(x, c, norm1_w, norm2_w, w_mod, b_mod, w_in, gla_w_a2, gla_b_a, gla_norm_w, ret_norm_w, s5_a_re, s5_a_im, s5_log_dt, s5_b_re, s5_b_im, s5_c_re, s5_c_im, s5_d, s5_w_glu, s5_b_glu, w_out, router_w, router_b, w_up, b_up, w_down, b_down, final_norm_w):
    batch, seq, d = x.shape
    depth = w_mod.shape[0]
    gb = batch // N_STREAMS
    t = gb * seq
    n_slots = t * TOP_K
    n_blocks = n_slots // ROW_BLK + N_EXPERTS
    n_rows = n_blocks * ROW_BLK

    mod = _modulation(c, w_mod, b_mod)
    mod5 = mod.reshape(depth, batch, 6, 1, d).transpose(0, 2, 1, 3, 4)
    ret_tabs = _retention_tables(seq)

    w_p = _take_cols(w_in, _IN_SRC).astype(BF16)
    wa_p = jnp.zeros((depth, LANES, QKP), F32).at[:, :GATE_RANK].set(
        _take_cols(gla_w_a2, _DK_SRC)).astype(BF16)
    ba_p = _take_cols(gla_b_a, _DK_SRC).reshape(depth, 1, QKP)
    gnw = _take_cols(gla_norm_w, _DV_SRC).reshape(depth, 1, HP)
    rnw = _take_cols(ret_norm_w, _DV_SRC).reshape(depth, 1, HP)
    kv = N_HEADS * HEAD_DV
    wo_g = _take_rows(w_out[:, :kv], _DV_SRC).astype(BF16)
    wo_r = _take_rows(w_out[:, kv:2 * kv], _DV_SRC).astype(BF16)
    wo_s = w_out[:, 2 * kv:].astype(BF16)
    rw_t = jnp.swapaxes(router_w, 1, 2)
    rw_hi = rw_t.astype(BF16)
    rw_p = jnp.concatenate([rw_hi, (rw_t - rw_hi.astype(F32)).astype(BF16)], axis=1)
    rb_p = jnp.broadcast_to(router_b[:, :, None], (depth, N_EXPERTS, LANES))
    s5_tabs = jax.vmap(_s5_tables)(s5_a_re, s5_a_im, s5_log_dt, s5_b_re, s5_b_im, s5_c_re,
                                   s5_c_im, s5_d, s5_w_glu, s5_b_glu)
    n1 = norm1_w.reshape(depth, 1, d)
    n2 = norm2_w.reshape(depth, 1, d)
    fw = final_norm_w.reshape(1, d)

    xs = [(x.reshape(N_STREAMS, t, d), g) for g in range(N_STREAMS)]
    out = None

    for i in range(depth):
        final = i == depth - 1
        for g in range(N_STREAMS):
            x3, grp = xs[g]
            b0 = g * gb

            proj = _in_projection(x3, grp, mod5, b0, i, n1, w_p, seq)
            o_gla = _gla_mixer(proj, gb, seq, i, wa_p, ba_p, gnw)
            o_ret = _ret_mixer(proj, gb, seq, i, *ret_tabs, rnw)
            o_s5 = _s5_mixer(proj, gb, seq, i, s5_tabs)

            x1, hdn, idx, gates, rank, counts = _outproj_router(
                o_gla, o_ret, o_s5, wo_g, wo_r, wo_s, x3, grp, mod5, b0, i, n2, rw_p, rb_p, seq)

            cnt = counts[:, 0].astype(jnp.int32)
            padded = (cnt + ROW_BLK - 1) // ROW_BLK * ROW_BLK
            pad_ends = jnp.cumsum(padded)
            first_row = pad_ends - cnt
            slot_start = jnp.sum(jnp.where(idx[:TOP_K, :, None] == jnp.arange(N_EXPERTS),
                                           first_row, 0), axis=-1)
            dest_sm = (slot_start + rank[:TOP_K]).astype(jnp.int32).reshape(-1)
            gates_tm = gates[:TOP_K].T
            blk_start = jnp.arange(n_blocks, dtype=jnp.int32) * ROW_BLK
            block_e = jnp.minimum(jnp.sum(pad_ends[None, :] <= blk_start[:, None], axis=1),
                                  N_EXPERTS - 1).astype(jnp.int32)
            n_skip = jnp.where(blk_start < pad_ends[-1],
                               jnp.clip(first_row[block_e] - blk_start, 0, ROW_BLK),
                               ROW_BLK).astype(jnp.int32)

            rows = _scatter_rows(hdn, dest_sm, n_rows)
            out_rows = _experts(i, block_e, n_skip, rows, w_up, b_up, w_down, b_down)
            y4 = _gather_rows(out_rows, dest_sm)
            if final:
                out = _combine(y4, gates_tm, x1, mod5, b0, i, fw, seq, True, g, N_STREAMS, out)
            else:
                xs[g] = (_combine(y4, gates_tm, x1, mod5, b0, i, fw, seq, False, 0, 1, None), 0)

    return out.reshape(batch, seq, d)
```
